```python
import math
import jax, jax.numpy as jnp
from jax import lax
import numpy as np

D_MODEL = 1024
BATCH = 4
SEQ = 4096
DEPTH = 1
DEC_BATCH = 32
DEC_SEQ = 1
PAST_LEN = 16384
PAGE_SIZE = 128

D_SSM = 512
SSM_GROUP = 16
N_SSM_GROUPS = D_SSM // SSM_GROUP
SSM_STATE = 64
N_HEADS = 8
HEAD_DIM = 64
N_KV_HEADS = 2
GQA = N_HEADS // N_KV_HEADS
D_ATT = N_HEADS * HEAD_DIM
D_KV = 2 * N_KV_HEADS * HEAD_DIM
CMP_STRIDE = 16
CMP_BLOCK = 2 * CMP_STRIDE
SEL_BLOCK = 64
N_SEL = 16
WINDOW = 512
Q_BLOCK = 128
NUM_BUCKETS = 32
REL_MAX_DIST = 1024
N_EXPERTS = 32
TOP_K = 4
D_FF = 1024
SWIGLU_LIMIT = 7.0
SWIGLU_ALPHA = 1.702
MOE_BLOCK_MAX = 256
DN_ALPHA = (2 * DEPTH) ** 0.25
DN_BETA = (8 * DEPTH) ** -0.25
D_IN = D_SSM + D_ATT + 3 * D_KV + 3 * N_HEADS
NEG = -1e30
F32 = jnp.float32

kernel_name = "hymba_s5_nsa_moe_decode_step"


def layer_norm(x, eps=1e-5):
    xf = x.astype(F32)
    mu = xf.mean(-1, keepdims=True)
    var = jnp.mean(jnp.square(xf - mu), -1, keepdims=True)
    return (xf - mu) * lax.rsqrt(var + eps)


def modulate(x, shift, scale):
    return (layer_norm(x) * (1.0 + scale.astype(F32)) + shift.astype(F32)).astype(x.dtype)


def post_norm(x, gate, y, g, b):
    z = DN_ALPHA * x + gate * y
    return (layer_norm(z) * g.astype(F32) + b.astype(F32)).astype(x.dtype)


def adaln(c, w, b):
    m = (jax.nn.silu(c) @ w + b)[:, None, :]
    return jnp.split(m, 6, axis=-1)


def rel_bucket(dist):
    n = jnp.maximum(dist, 0)
    max_exact = NUM_BUCKETS // 2
    nf = jnp.maximum(n, 1).astype(F32)
    large = max_exact + (jnp.log(nf / max_exact) / math.log(REL_MAX_DIST / max_exact) * (NUM_BUCKETS - max_exact)).astype(jnp.int32)
    large = jnp.minimum(large, NUM_BUCKETS - 1)
    return jnp.where(n < max_exact, n, large)


def masked_softmax(s, mask):
    s = jnp.where(mask, s.astype(F32), NEG)
    m = jnp.max(s, -1, keepdims=True)
    p = jnp.where(mask, jnp.exp(s - m), 0.0)
    return p / jnp.maximum(p.sum(-1, keepdims=True), 1e-30)


def shared_bias(table, dist):
    bt = table[rel_bucket(dist)]
    return jnp.moveaxis(bt, -1, 0).reshape(N_KV_HEADS, GQA, *dist.shape).astype(F32)


def attend_shared(q, k, v, q_pos, k_pos, mask, table):
    s = jnp.einsum('bhgqd,bkhd->bhgqk', q, k).astype(F32) * HEAD_DIM ** -0.5
    s = s + shared_bias(table, q_pos[:, None] - k_pos[None, :])
    p = masked_softmax(s, mask)
    return jnp.einsum('bhgqk,bkhd->bhgqd', p.astype(v.dtype), v), p


def compress_kv(kv, phi_pe, phi_w1, phi_b1, phi_w2, phi_b2):
    B, L = kv.shape[:2]
    ch = kv.reshape(B, L // CMP_STRIDE, CMP_STRIDE, 2, N_KV_HEADS, HEAD_DIM)
    pe = jnp.transpose(phi_pe.reshape(2, 2, CMP_STRIDE, HEAD_DIM), (1, 2, 0, 3))[:, :, :, None, :]
    w1 = phi_w1.reshape(2, 2, CMP_STRIDE, HEAD_DIM, HEAD_DIM)
    first = jnp.einsum('bnjchd,cjde->bnche', ch + pe[0], w1[:, 0])
    second = jnp.einsum('bnjchd,cjde->bnche', ch + pe[1], w1[:, 1])
    hdn = jax.nn.gelu(first[:, :-1] + second[:, 1:] + phi_b1[:, None, :])
    return jnp.einsum('bnche,cef->bnchf', hdn, phi_w2) + phi_b2[:, None, :]


def attend_compressed(q, ckv, q_pos, table):
    nc = ckv.shape[1]
    c_end = jnp.arange(nc) * CMP_STRIDE + CMP_BLOCK - 1
    mask = c_end[None, :] <= q_pos[:, None]
    return attend_shared(q, ckv[:, :, 0], ckv[:, :, 1], q_pos, c_end, mask, table)


def select_blocks(p_cmp, q_pos, n_blocks):
    imp = p_cmp.sum(2)
    r = SEL_BLOCK // CMP_STRIDE
    padded = jnp.pad(imp, ((0, 0), (0, 0), (0, 0), (1, 2)))
    s = padded[..., :n_blocks * r].reshape(*imp.shape[:3], n_blocks, r).sum(-1) + padded[..., r:n_blocks * r + 1:r]
    blk = jnp.arange(n_blocks)[None, :]
    cur = (q_pos // SEL_BLOCK)[:, None]
    causal = blk <= cur
    forced = (blk == 0) | (blk == cur) | (blk == cur - 1)
    s = jnp.where(forced & causal, 1e4, jnp.where(causal, s, -1.0))
    _, idx = lax.top_k(s, min(N_SEL, n_blocks))
    return idx, idx <= cur


def gather_blocks_contig(kv, idx):
    B, L = kv.shape[:2]
    blocks = kv.reshape(B, L // SEL_BLOCK, SEL_BLOCK, 2, N_KV_HEADS, HEAD_DIM)
    b = jnp.arange(B)[:, None, None, None]
    h = jnp.arange(N_KV_HEADS)[None, :, None, None]
    return blocks[b, idx, :, :, h]


def gather_blocks_paged(pool, page_table, new_kv, idx):
    B = idx.shape[0]
    bpp = PAGE_SIZE // SEL_BLOCK
    n_past = page_table.shape[1] * bpp
    pages = pool.reshape(pool.shape[0], bpp, SEL_BLOCK, 2, N_KV_HEADS, HEAD_DIM)
    b = jnp.arange(B)[:, None, None, None]
    h = jnp.arange(N_KV_HEADS)[None, :, None, None]
    past_idx = jnp.minimum(idx, n_past - 1)
    phys = page_table[b, past_idx // bpp]
    past = pages[phys, past_idx % bpp, :, :, h]
    tn = new_kv.shape[1]
    n_new = -(-tn // SEL_BLOCK)
    new_blocks = jnp.pad(new_kv, ((0, 0), (0, n_new * SEL_BLOCK - tn), (0, 0), (0, 0), (0, 0))).reshape(B, n_new, SEL_BLOCK, 2, N_KV_HEADS, HEAD_DIM)
    new = new_blocks[b, jnp.clip(idx - n_past, 0, n_new - 1), :, :, h]
    return jnp.where((idx < n_past)[..., None, None, None], past, new)


def attend_selected(q, kv_sel, idx, valid, q_pos, table):
    B, H, Tq, ns = idx.shape
    k = kv_sel[..., 0, :].reshape(B, H, Tq, ns * SEL_BLOCK, HEAD_DIM)
    v = kv_sel[..., 1, :].reshape(B, H, Tq, ns * SEL_BLOCK, HEAD_DIM)
    k_pos = (idx[..., None] * SEL_BLOCK + jnp.arange(SEL_BLOCK)).reshape(B, H, Tq, ns * SEL_BLOCK)
    dist = q_pos[:, None] - k_pos
    mask = jnp.repeat(valid, SEL_BLOCK, axis=-1) & (dist >= 0)
    tbl = jnp.transpose(table.reshape(NUM_BUCKETS, N_KV_HEADS, GQA), (1, 0, 2))
    bias = tbl[jnp.arange(N_KV_HEADS)[None, :, None, None], rel_bucket(dist)]
    s = jnp.einsum('bhgqd,bhqkd->bhgqk', q, k).astype(F32) * HEAD_DIM ** -0.5 + jnp.moveaxis(bias, -1, 2).astype(F32)
    p = masked_softmax(s, mask[:, :, None])
    return jnp.einsum('bhgqk,bhqkd->bhgqd', p.astype(v.dtype), v)


def attend_window(q, wkv, q_pos, k_pos, table):
    d = q_pos[:, None] - k_pos[None, :]
    mask = (d >= 0) & (d <= WINDOW) & (k_pos[None, :] >= 0)
    o, _ = attend_shared(q, wkv[:, :, 0], wkv[:, :, 1], q_pos, k_pos, mask, table)
    return o


def nsa_core(q, gates, q_pos, ckv, gather_fn, n_blocks, wkv, w_pos, table):
    B, Tq = q.shape[:2]
    qh = jnp.transpose(q.reshape(B, Tq, N_KV_HEADS, GQA, HEAD_DIM), (0, 2, 3, 1, 4))
    o_cmp, p_cmp = attend_compressed(qh, ckv, q_pos, table)
    idx, valid = select_blocks(p_cmp, q_pos, n_blocks)
    o_sel = attend_selected(qh, gather_fn(idx), idx, valid, q_pos, table)
    o_win = attend_window(qh, wkv, q_pos, w_pos, table)
    o = jnp.stack([o_cmp, o_sel, o_win], -1)
    o = jnp.transpose(o, (0, 3, 1, 2, 4, 5)).reshape(B, Tq, N_HEADS, HEAD_DIM, 3)
    g = jax.nn.sigmoid(gates.astype(F32)).astype(o.dtype)
    return jnp.einsum('bqhdr,bqhr->bqhd', o, g).reshape(B, Tq, D_ATT)


def nsa_prompt(q, gates, kv_cmp, kv_sel, kv_win, phi_pe, phi_w1, phi_b1, phi_w2, phi_b2, table):
    B, T = q.shape[:2]
    ckv = compress_kv(kv_cmp, phi_pe, phi_w1, phi_b1, phi_w2, phi_b2)
    n_blocks = T // SEL_BLOCK
    qb = min(Q_BLOCK, T)
    wpad = jnp.pad(kv_win, ((0, 0), (WINDOW, 0), (0, 0), (0, 0), (0, 0)))

    def block(i):
        s = i * qb
        q_i = lax.dynamic_slice_in_dim(q, s, qb, axis=1)
        g_i = lax.dynamic_slice_in_dim(gates, s, qb, axis=1)
        w_i = lax.dynamic_slice_in_dim(wpad, s, WINDOW + qb, axis=1)
        q_pos = s + jnp.arange(qb)
        w_pos = s - WINDOW + jnp.arange(WINDOW + qb)
        return nsa_core(q_i, g_i, q_pos, ckv, lambda idx: gather_blocks_contig(kv_sel, idx), n_blocks, w_i, w_pos, table)

    out = lax.map(block, jnp.arange(T // qb))
    return jnp.moveaxis(out, 0, 1).reshape(B, T, D_ATT)


def nsa_sample(q, gates, kv_cmp, kv_sel, kv_win, pool_cmp, pool_sel, win_buf, page_table, phi_pe, phi_w1, phi_b1, phi_w2, phi_b2, table):
    B, tn = q.shape[:2]
    past_cmp = pool_cmp[page_table].reshape(B, PAST_LEN, 2, N_KV_HEADS, HEAD_DIM)
    L = PAST_LEN + tn
    Lp = -(-L // SEL_BLOCK) * SEL_BLOCK
    full = jnp.pad(jnp.concatenate([past_cmp, kv_cmp], 1), ((0, 0), (0, Lp - L), (0, 0), (0, 0), (0, 0)))
    ckv = compress_kv(full, phi_pe, phi_w1, phi_b1, phi_w2, phi_b2)
    q_pos = PAST_LEN + jnp.arange(tn)
    wkv = jnp.concatenate([win_buf, kv_win], 1)
    w_pos = PAST_LEN - win_buf.shape[1] + jnp.arange(wkv.shape[1])
    return nsa_core(q, gates, q_pos, ckv, lambda idx: gather_blocks_paged(pool_sel, page_table, kv_sel, idx), Lp // SEL_BLOCK, wkv, w_pos, table)


def ssm_branch(u, h0, lam_re, lam_im, log_dt, b_re, b_im, c_re, c_im, d_skip, w_glu, b_glu):
    B, T = u.shape[:2]
    uf = u.astype(F32).reshape(B, T, N_SSM_GROUPS, SSM_GROUP)
    lam = lax.complex(lam_re.astype(F32), lam_im.astype(F32))
    dt = jnp.exp(log_dt.astype(F32))[:, None]
    lam_bar = jnp.exp(lam * dt)
    bb = ((lam_bar - 1.0) / lam)[:, :, None] * lax.complex(b_re.astype(F32), b_im.astype(F32))
    bu = jnp.einsum('btgc,gpc->btgp', uf.astype(jnp.complex64), bb)
    bu = bu.at[:, 0].add(lam_bar * h0)
    a = jnp.broadcast_to(lam_bar, bu.shape)

    def comb(e1, e2):
        a1, b1 = e1
        a2, b2 = e2
        return a1 * a2, a2 * b1 + b2

    _, hs = lax.associative_scan(comb, (a, bu), axis=1)
    cc = lax.complex(c_re.astype(F32), c_im.astype(F32))
    y = jnp.einsum('btgp,gcp->btgc', hs, cc).real + d_skip.astype(F32).reshape(N_SSM_GROUPS, SSM_GROUP) * uf
    g = jax.nn.gelu(y.reshape(B, T, D_SSM))
    out = g * jax.nn.sigmoid(g @ w_glu.astype(F32) + b_glu.astype(F32))
    return out.astype(u.dtype), hs[:, -1]


def mixer_in(x, shift, scale, w_in):
    B, T = x.shape[:2]
    z = modulate(x, shift, scale) @ w_in
    cuts = [D_SSM, D_SSM + D_ATT, D_SSM + D_ATT + D_KV, D_SSM + D_ATT + 2 * D_KV, D_SSM + D_ATT + 3 * D_KV]
    u, q, kvc, kvs, kvw, g = jnp.split(z, cuts, axis=-1)
    kv_shape = (B, T, 2, N_KV_HEADS, HEAD_DIM)
    return (u, q.reshape(B, T, N_HEADS, HEAD_DIM), kvc.reshape(kv_shape), kvs.reshape(kv_shape),
            kvw.reshape(kv_shape), g.reshape(B, T, N_HEADS, 3))


def moe(h, w_router, b_router, w_gate_up, b_gate_up, w_down, b_down):
    shp = h.shape
    x = h.reshape(-1, D_MODEL)
    n = x.shape[0]
    logits = (x @ w_router).astype(F32) + b_router.astype(F32)
    top_v, top_e = lax.top_k(logits, TOP_K)
    top_w = jax.nn.softmax(top_v, axis=-1)
    e = top_e.reshape(-1)
    tok = jnp.repeat(jnp.arange(n, dtype=jnp.int32), TOP_K)
    order = jnp.argsort(e)
    e_s, tok_s, w_s = e[order], tok[order], top_w.reshape(-1)[order]
    blk = max(8, min(MOE_BLOCK_MAX, n * TOP_K // N_EXPERTS))
    counts = jnp.bincount(e, length=N_EXPERTS)
    pcounts = (counts + blk - 1) // blk * blk
    start = jnp.cumsum(counts) - counts
    pend = jnp.cumsum(pcounts)
    pstart = pend - pcounts
    dest = pstart[e_s] + jnp.arange(n * TOP_K) - start[e_s]
    n_blk = (n * TOP_K + N_EXPERTS * (blk - 1)) // blk
    rows = n_blk * blk
    row_tok = jnp.full((rows,), n, jnp.int32).at[dest].set(tok_s)
    row_w = jnp.zeros((rows,), F32).at[dest].set(w_s)
    blk_e = jnp.minimum(jnp.searchsorted(pend, jnp.arange(n_blk) * blk, side='right'), N_EXPERTS - 1)
    xb = jnp.concatenate([x, jnp.zeros((1, D_MODEL), x.dtype)])[row_tok].reshape(n_blk, blk, D_MODEL)

    def expert(args):
        xe, ei = args
        gu = xe @ w_gate_up[ei] + b_gate_up[ei]
        gate = jnp.minimum(gu[:, :D_FF], SWIGLU_LIMIT)
        up = jnp.clip(gu[:, D_FF:], -SWIGLU_LIMIT, SWIGLU_LIMIT)
        hh = (up + 1.0) * gate * jax.nn.sigmoid(SWIGLU_ALPHA * gate)
        return hh @ w_down[ei] + b_down[ei]

    yb = lax.map(expert, (xb, blk_e)).reshape(rows, D_MODEL)
    y = jnp.zeros((n + 1, D_MODEL), F32).at[row_tok].add(yb.astype(F32) * row_w[:, None])
    return y[:n].astype(h.dtype).reshape(shp)


def setup_inputs(seed: int = 0) -> dict:
    key = jax.random.key(seed)
    ks = iter(jax.random.split(key, 48))

    def nrm(shape, scale):
        return jax.random.normal(next(ks), shape, F32) * scale

    n_pages = PAST_LEN // PAGE_SIZE
    n_pool = (DEC_BATCH * n_pages * 5) // 4
    win_buf = min(WINDOW, PAST_LEN)
    kv_tail = (2, N_KV_HEADS, HEAD_DIM)
    G, P, C = N_SSM_GROUPS, SSM_STATE, SSM_GROUP
    inp = {}
    inp['x_prompt'] = nrm((BATCH, SEQ, D_MODEL), 1.0)
    inp['x_sample'] = nrm((DEC_BATCH, DEC_SEQ, D_MODEL), 1.0)
    inp['cache_cmp_kv'] = nrm((DEPTH, n_pool, PAGE_SIZE) + kv_tail, 1.0)
    inp['cache_sel_kv'] = nrm((DEPTH, n_pool, PAGE_SIZE) + kv_tail, 1.0)
    inp['state_win_kv'] = nrm((DEPTH, DEC_BATCH, win_buf) + kv_tail, 1.0)
    inp['state_ssm_re'] = nrm((DEPTH, DEC_BATCH, G, P), 1.0)
    inp['state_ssm_im'] = nrm((DEPTH, DEC_BATCH, G, P), 1.0)
    inp['page_table'] = jax.random.permutation(next(ks), n_pool)[:DEC_BATCH * n_pages].reshape(DEC_BATCH, n_pages).astype(jnp.int32)
    inp['c_prompt'] = nrm((BATCH, D_MODEL), 1.0)
    inp['c_sample'] = nrm((DEC_BATCH, D_MODEL), 1.0)
    inp['w_ada'] = nrm((DEPTH, D_MODEL, 6 * D_MODEL), D_MODEL ** -0.5)
    inp['b_ada'] = nrm((DEPTH, 6 * D_MODEL), 0.02)
    inp['w_in'] = nrm((DEPTH, D_MODEL, D_IN), D_MODEL ** -0.5)
    inp['lam_re'] = -0.5 + nrm((DEPTH, G, P), 0.01)
    inp['lam_im'] = math.pi * jnp.arange(P, dtype=F32) + nrm((DEPTH, G, P), 0.01)
    inp['log_dt'] = jax.random.uniform(next(ks), (DEPTH, G), F32, math.log(1e-3), math.log(1e-1))
    inp['b_re'] = nrm((DEPTH, G, P, C), (2 * C) ** -0.5)
    inp['b_im'] = nrm((DEPTH, G, P, C), (2 * C) ** -0.5)
    inp['c_re'] = nrm((DEPTH, G, C, P), P ** -0.5)
    inp['c_im'] = nrm((DEPTH, G, C, P), P ** -0.5)
    inp['d_skip'] = nrm((DEPTH, D_SSM), 1.0)
    inp['w_glu'] = nrm((DEPTH, D_SSM, D_SSM), D_SSM ** -0.5)
    inp['b_glu'] = nrm((DEPTH, D_SSM), 0.02)
    inp['phi_pe'] = nrm((DEPTH, 2, CMP_BLOCK, HEAD_DIM), 0.02)
    inp['phi_w1'] = nrm((DEPTH, 2, CMP_BLOCK, HEAD_DIM, HEAD_DIM), (CMP_BLOCK * HEAD_DIM) ** -0.5)
    inp['phi_b1'] = nrm((DEPTH, 2, HEAD_DIM), 0.02)
    inp['phi_w2'] = nrm((DEPTH, 2, HEAD_DIM, HEAD_DIM), HEAD_DIM ** -0.5)
    inp['phi_b2'] = nrm((DEPTH, 2, HEAD_DIM), 0.02)
    inp['rel_bias'] = nrm((NUM_BUCKETS, N_HEADS), 0.1)
    inp['w_out'] = nrm((DEPTH, D_SSM + D_ATT, D_MODEL), (D_SSM + D_ATT) ** -0.5 * DN_BETA)
    inp['ln1_g'] = 1.0 + nrm((DEPTH, D_MODEL), 0.02)
    inp['ln1_b'] = nrm((DEPTH, D_MODEL), 0.02)
    inp['w_router'] = nrm((DEPTH, D_MODEL, N_EXPERTS), D_MODEL ** -0.5)
    inp['b_router'] = nrm((DEPTH, N_EXPERTS), 0.01)
    inp['w_gate_up'] = nrm((DEPTH, N_EXPERTS, D_MODEL, 2 * D_FF), D_MODEL ** -0.5)
    inp['b_gate_up'] = nrm((DEPTH, N_EXPERTS, 2 * D_FF), 0.02)
    inp['w_down'] = nrm((DEPTH, N_EXPERTS, D_FF, D_MODEL), D_FF ** -0.5 * DN_BETA)
    inp['b_down'] = nrm((DEPTH, N_EXPERTS, D_MODEL), 0.02)
    inp['ln2_g'] = 1.0 + nrm((DEPTH, D_MODEL), 0.02)
    inp['ln2_b'] = nrm((DEPTH, D_MODEL), 0.02)
    return inp


def reference(x_prompt, x_sample, cache_cmp_kv, cache_sel_kv, state_win_kv, state_ssm_re, state_ssm_im, page_table,
              c_prompt, c_sample, w_ada, b_ada, w_in, lam_re, lam_im, log_dt, b_re, b_im, c_re, c_im, d_skip,
              w_glu, b_glu, phi_pe, phi_w1, phi_b1, phi_w2, phi_b2, rel_bias, w_out, ln1_g, ln1_b,
              w_router, b_router, w_gate_up, b_gate_up, w_down, b_down, ln2_g, ln2_b):
    xp, xs = x_prompt, x_sample
    cmp_p, cmp_s, sel_p, sel_s, win_p, win_s, re_p, im_p, re_s, im_s = ([] for _ in range(10))
    for l in range(DEPTH):
        ssm_w = (lam_re[l], lam_im[l], log_dt[l], b_re[l], b_im[l], c_re[l], c_im[l], d_skip[l], w_glu[l], b_glu[l])
        phi = (phi_pe[l], phi_w1[l], phi_b1[l], phi_w2[l], phi_b2[l])
        moe_w = (w_router[l], b_router[l], w_gate_up[l], b_gate_up[l], w_down[l], b_down[l])

        m = adaln(c_prompt, w_ada[l], b_ada[l])
        u, q, kvc, kvs, kvw, g = mixer_in(xp, m[0], m[1], w_in[l])
        h0 = jnp.zeros((xp.shape[0], N_SSM_GROUPS, SSM_STATE), jnp.complex64)
        ssm_y, h_last = ssm_branch(u, h0, *ssm_w)
        att = nsa_prompt(q, g, kvc, kvs, kvw, *phi, rel_bias)
        xp = post_norm(xp, m[2], jnp.concatenate([ssm_y, att], -1) @ w_out[l], ln1_g[l], ln1_b[l])
        xp = post_norm(xp, m[5], moe(modulate(xp, m[3], m[4]), *moe_w), ln2_g[l], ln2_b[l])
        cmp_p.append(kvc)
        sel_p.append(kvs)
        win_p.append(kvw[:, -min(WINDOW, kvw.shape[1]):])
        re_p.append(h_last.real.astype(state_ssm_re.dtype))
        im_p.append(h_last.imag.astype(state_ssm_im.dtype))

        m = adaln(c_sample, w_ada[l], b_ada[l])
        u, q, kvc, kvs, kvw, g = mixer_in(xs, m[0], m[1], w_in[l])
        h0 = lax.complex(state_ssm_re[l].astype(F32), state_ssm_im[l].astype(F32))
        ssm_y, h_last = ssm_branch(u, h0, *ssm_w)
        att = nsa_sample(q, g, kvc, kvs, kvw, cache_cmp_kv[l], cache_sel_kv[l], state_win_kv[l], page_table, *phi, rel_bias)
        xs = post_norm(xs, m[2], jnp.concatenate([ssm_y, att], -1) @ w_out[l], ln1_g[l], ln1_b[l])
        xs = post_norm(xs, m[5], moe(modulate(xs, m[3], m[4]), *moe_w), ln2_g[l], ln2_b[l])
        cmp_s.append(kvc)
        sel_s.append(kvs)
        win_s.append(jnp.concatenate([state_win_kv[l], kvw], 1)[:, -state_win_kv.shape[2]:])
        re_s.append(h_last.real.astype(state_ssm_re.dtype))
        im_s.append(h_last.imag.astype(state_ssm_im.dtype))

    return (xp, xs, jnp.stack(cmp_p), jnp.stack(cmp_s), jnp.stack(sel_p), jnp.stack(sel_s),
            jnp.stack(win_p), jnp.stack(win_s), jnp.stack(re_p), jnp.stack(im_p), jnp.stack(re_s), jnp.stack(im_s))
```

```python
import functools
import math

import numpy as np
import jax
import jax.numpy as jnp
from jax import lax
from jax.experimental import pallas as pl
from jax.experimental.pallas import tpu as pltpu

D_MODEL = 1024
DEPTH = 1
PAST_LEN = 16384
PAGE_SIZE = 128
D_SSM = 512
SSM_GROUP = 16
N_SSM_GROUPS = D_SSM // SSM_GROUP
SSM_STATE = 64
N_HEADS = 8
HEAD_DIM = 64
N_KV_HEADS = 2
GQA = N_HEADS // N_KV_HEADS
D_ATT = N_HEADS * HEAD_DIM
D_KV = 2 * N_KV_HEADS * HEAD_DIM
CMP_STRIDE = 16
CMP_BLOCK = 2 * CMP_STRIDE
SEL_BLOCK = 64
N_SEL = 16
WINDOW = 512
NUM_BUCKETS = 32
REL_MAX_DIST = 1024
N_EXPERTS = 32
TOP_K = 4
D_FF = 1024
SWIGLU_LIMIT = 7.0
SWIGLU_ALPHA = 1.702
DN_ALPHA = (2 * DEPTH) ** 0.25
D_IN = D_SSM + D_ATT + 3 * D_KV + 3 * N_HEADS
NEG = -1e30
F32 = jnp.float32
BF16 = jnp.bfloat16
HIGHEST = lax.Precision.HIGHEST

LANE = 128
D_IN_PAD = 1920
GATE_COL = D_SSM + D_ATT + 3 * D_KV
SSM_CHUNK = 16
ATT_TQ = 128
ATT_TK = 128
MOE_ROWS = 256
VMEM_LIMIT = 48 * 1024 * 1024
LN_EPS = 1e-5


def _cparams(*sem):
    return pltpu.CompilerParams(dimension_semantics=sem, vmem_limit_bytes=VMEM_LIMIT)


def _nt_dot(a, b):
    return lax.dot_general(a, b, (((1,), (1,)), ((), ())), preferred_element_type=F32)


def _layer_norm(x):
    mu = jnp.mean(x, axis=-1, keepdims=True)
    xc = x - mu
    var = jnp.mean(xc * xc, axis=-1, keepdims=True)
    return xc * lax.rsqrt(var + LN_EPS)


def _adaln_kernel(c_ref, w_ref, b_ref, o_ref):
    c = c_ref[...]
    s = c * jax.nn.sigmoid(c)
    o_ref[...] = jnp.dot(s, w_ref[...], precision=HIGHEST, preferred_element_type=F32) + b_ref[...]


def _adaln(c, w, b):
    n, d = c.shape
    dout = w.shape[1]
    tn = 1024
    return pl.pallas_call(
        _adaln_kernel,
        out_shape=jax.ShapeDtypeStruct((n, dout), F32),
        grid=(dout // tn,),
        in_specs=[pl.BlockSpec((n, d), lambda j: (0, 0)),
                  pl.BlockSpec((d, tn), lambda j: (0, j)),
                  pl.BlockSpec((1, tn), lambda j: (0, j))],
        out_specs=pl.BlockSpec((n, tn), lambda j: (0, j)),
        compiler_params=_cparams("arbitrary"),
        name="adaln",
    )(c, w, b.reshape(1, dout))


def _mixer_in_kernel(x_ref, sh_ref, sc_ref, w_ref, u_ref, q_ref, kvc_ref, kvs_ref, kvw_ref, g_ref):
    h = _layer_norm(x_ref[0]) * (1.0 + sc_ref[0]) + sh_ref[0]
    z = jnp.dot(h.astype(BF16), w_ref[...], preferred_element_type=F32)
    c0 = D_SSM
    c1 = c0 + D_ATT
    c2 = c1 + D_KV
    c3 = c2 + D_KV
    c4 = c3 + D_KV
    u_ref[0] = z[:, :c0]
    q_ref[0] = z[:, c0:c1].astype(BF16)
    kvc_ref[0] = z[:, c1:c2]
    kvs_ref[0] = z[:, c2:c3]
    kvw_ref[0] = z[:, c3:c4]
    g_ref[0] = z[:, c4:c4 + LANE]


def _mixer_in(x, shift, scale, w_pad, tm):
    B, T, D = x.shape
    R = shift.shape[1]
    rb = 1 if R == 1 else tm
    mod_map = (lambda b, i: (b, 0, 0)) if R == 1 else (lambda b, i: (b, i, 0))
    row = lambda n: pl.BlockSpec((1, tm, n), lambda b, i: (b, i, 0))
    outs = (jax.ShapeDtypeStruct((B, T, D_SSM), F32), jax.ShapeDtypeStruct((B, T, D_ATT), BF16),
            jax.ShapeDtypeStruct((B, T, D_KV), F32), jax.ShapeDtypeStruct((B, T, D_KV), F32),
            jax.ShapeDtypeStruct((B, T, D_KV), F32), jax.ShapeDtypeStruct((B, T, LANE), F32))
    return pl.pallas_call(
        _mixer_in_kernel,
        out_shape=outs,
        grid=(B, T // tm),
        in_specs=[row(D), pl.BlockSpec((1, rb, D), mod_map), pl.BlockSpec((1, rb, D), mod_map),
                  pl.BlockSpec((D, D_IN_PAD), lambda b, i: (0, 0))],
        out_specs=(row(D_SSM), row(D_ATT), row(D_KV), row(D_KV), row(D_KV), row(LANE)),
        compiler_params=_cparams("parallel", "parallel"),
        name="mixer_in",
    )(x, shift, scale, w_pad)


def _ssm_tables(lam_re, lam_im, log_dt, b_re, b_im, c_re, c_im, L, n_levels):
    G, P = lam_re.shape
    C = b_re.shape[-1]
    dt = jnp.exp(log_dt.astype(F32))[:, None]
    er, ei = lam_re * dt, lam_im * dt

    def power(k):
        kk = k.astype(F32)[:, None, None]
        mag = jnp.exp(kk * er)
        return mag * jnp.cos(kk * ei), mag * jnp.sin(kk * ei)

    lb_re, lb_im = power(jnp.ones((1,), F32))
    nr, ni = lb_re[0] - 1.0, lb_im[0]
    den = lam_re * lam_re + lam_im * lam_im
    fr = (nr * lam_re + ni * lam_im) / den
    fi = (ni * lam_re - nr * lam_im) / den
    bbr = fr[:, :, None] * b_re - fi[:, :, None] * b_im
    bbi = fr[:, :, None] * b_im + fi[:, :, None] * b_re
    pr, pi = power(jnp.arange(L + 1))
    clr = c_re[None] * pr[:, :, None, :] - c_im[None] * pi[:, :, None, :]
    cli = c_re[None] * pi[:, :, None, :] + c_im[None] * pr[:, :, None, :]
    kern = (jnp.einsum('kgcp,gpd->kgcd', clr[:L], bbr, precision=HIGHEST)
            - jnp.einsum('kgcp,gpd->kgcd', cli[:L], bbi, precision=HIGHEST))
    kz = jnp.concatenate([kern, jnp.zeros((1,) + kern.shape[1:], F32)], 0)
    ts = np.arange(L)
    lag = ts[None, :] - ts[:, None]
    lag = np.where(lag >= 0, lag, L)
    toep = kz[lag]
    toep = jnp.transpose(toep, (2, 0, 4, 1, 3)).reshape(G, L * C, L * C)
    rev = L - 1 - ts
    wsr = pr[rev][:, :, :, None] * bbr[None] - pi[rev][:, :, :, None] * bbi[None]
    wsi = pr[rev][:, :, :, None] * bbi[None] + pi[rev][:, :, :, None] * bbr[None]
    ws = jnp.concatenate([jnp.transpose(wsr, (1, 0, 3, 2)), jnp.transpose(wsi, (1, 0, 3, 2))], -1)
    ws = ws.reshape(G, L * C, 2 * P)
    wy = jnp.concatenate([jnp.transpose(clr[1:], (1, 3, 0, 2)), -jnp.transpose(cli[1:], (1, 3, 0, 2))], 1)
    wy = wy.reshape(G, 2 * P, L * C)
    lr, li = power(L * (2 ** jnp.arange(n_levels)))
    ar = jnp.transpose(jnp.concatenate([lr, lr], -1), (1, 0, 2))
    ai = jnp.transpose(jnp.concatenate([-li, li], -1), (1, 0, 2))
    return toep.astype(BF16), ws.astype(BF16), wy.astype(BF16), ar, ai, (lb_re[0], lb_im[0], bbr, bbi)


def _ssm_kernel(u_ref, toep_ref, ws_ref, wy_ref, ar_ref, ai_ref, y_ref, hl_ref, *, nb, nc, n_levels):
    u = u_ref[0]
    y1 = jnp.dot(u, toep_ref[0], preferred_element_type=F32)
    s = jnp.dot(u, ws_ref[0], preferred_element_type=F32)
    p2 = s.shape[-1]
    rows = lax.broadcasted_iota(jnp.int32, (nc, p2), 0)
    prev = []
    for b in range(nb):
        h = s[b * nc:(b + 1) * nc]
        for k in range(n_levels):
            d = 1 << k
            sh = jnp.where(rows >= d, pltpu.roll(h, d, axis=0), 0.0)
            sw = pltpu.roll(sh, p2 // 2, axis=1)
            h = h + ar_ref[0, k:k + 1, :] * sh + ai_ref[0, k:k + 1, :] * sw
        hl_ref[0, b:b + 1, :] = h[nc - 1:nc, :]
        prev.append(jnp.where(rows >= 1, pltpu.roll(h, 1, axis=0), 0.0))
    hp = jnp.concatenate(prev, axis=0)
    y2 = jnp.dot(hp.astype(BF16), wy_ref[0], preferred_element_type=F32)
    y_ref[0] = y1 + y2


def _ssm_prompt(u, tables):
    toep, ws, wy, ar, ai, _ = tables
    B, T, _ = u.shape
    G, C, L = N_SSM_GROUPS, SSM_GROUP, SSM_CHUNK
    nc = T // L
    n_levels = ar.shape[1]
    ug = jnp.transpose(u.reshape(B, nc, L, G, C), (3, 0, 1, 2, 4)).reshape(G, B * nc, L * C).astype(BF16)
    grp = lambda r, c: pl.BlockSpec((1, r, c), lambda g: (g, 0, 0))
    y, hl = pl.pallas_call(
        functools.partial(_ssm_kernel, nb=B, nc=nc, n_levels=n_levels),
        out_shape=(jax.ShapeDtypeStruct((G, B * nc, L * C), F32),
                   jax.ShapeDtypeStruct((G, B, 2 * SSM_STATE), F32)),
        grid=(G,),
        in_specs=[grp(B * nc, L * C), grp(L * C, L * C), grp(L * C, 2 * SSM_STATE),
                  grp(2 * SSM_STATE, L * C), grp(n_levels, 2 * SSM_STATE), grp(n_levels, 2 * SSM_STATE)],
        out_specs=(grp(B * nc, L * C), grp(B, 2 * SSM_STATE)),
        compiler_params=_cparams("parallel"),
        name="ssm_prompt",
    )(ug, toep, ws, wy, ar, ai)
    y = jnp.transpose(y.reshape(G, B, nc, L, C), (1, 2, 3, 0, 4)).reshape(B, T, D_SSM)
    return y, jnp.transpose(hl, (1, 0, 2))


def _ssm_step_kernel(u_ref, h0_ref, bb_ref, lr_ref, li_ref, cy_ref, y_ref, h_ref):
    p = lr_ref.shape[-1] // 2
    bu = jnp.einsum('gbc,gcp->gbp', u_ref[...], bb_ref[...], preferred_element_type=F32)
    h0 = h0_ref[...]
    h0s = jnp.concatenate([h0[..., p:], h0[..., :p]], axis=-1)
    h = lr_ref[...] * h0 + li_ref[...] * h0s + bu
    h_ref[...] = h
    y_ref[...] = jnp.einsum('gbp,gpc->gbc', h.astype(BF16), cy_ref[...], preferred_element_type=F32)


def _ssm_sample(u, h0_re, h0_im, tables, c_re, c_im):
    lb_re, lb_im, bbr, bbi = tables[-1]
    B = u.shape[0]
    G, C, P = N_SSM_GROUPS, SSM_GROUP, SSM_STATE
    ug = jnp.transpose(u.reshape(B, G, C), (1, 0, 2)).astype(BF16)
    h0 = jnp.transpose(jnp.concatenate([h0_re, h0_im], -1), (1, 0, 2)).astype(F32)
    bb = jnp.concatenate([jnp.transpose(bbr, (0, 2, 1)), jnp.transpose(bbi, (0, 2, 1))], -1).astype(BF16)
    lr = jnp.concatenate([lb_re, lb_re], -1)[:, None, :]
    li = jnp.concatenate([-lb_im, lb_im], -1)[:, None, :]
    cy = jnp.concatenate([jnp.transpose(c_re, (0, 2, 1)), -jnp.transpose(c_im, (0, 2, 1))], 1).astype(BF16)
    y, h = pl.pallas_call(
        _ssm_step_kernel,
        out_shape=(jax.ShapeDtypeStruct((G, B, C), F32), jax.ShapeDtypeStruct((G, B, 2 * P), F32)),
        name="ssm_step",
    )(ug, h0, bb, lr, li, cy)
    return jnp.transpose(y, (1, 0, 2)).reshape(B, D_SSM), jnp.transpose(h, (1, 0, 2))


def _compress_tables(phi_pe, phi_w1, phi_b1, phi_w2, phi_b2):
    S, H, Dh = CMP_STRIDE, N_KV_HEADS, HEAD_DIM
    w1 = phi_w1.reshape(2, 2, S, Dh, Dh)
    eye_c = jnp.eye(2, dtype=F32)
    eye_h = jnp.eye(H, dtype=F32)
    wbig = jnp.einsum('cajde,xc,yh->jxydache', w1, eye_c, eye_h).reshape(S * 2 * H * Dh, 2 * 2 * H * Dh)
    pe = jnp.transpose(phi_pe.reshape(2, 2, S, Dh), (1, 2, 0, 3))
    pe_rows = jnp.broadcast_to(pe[:, :, :, None, :], (2, S, 2, H, Dh)).reshape(2, 1, S * 2 * H * Dh)
    b1 = jnp.broadcast_to(phi_b1[:, None, :], (2, H, Dh)).reshape(1, 2 * H * Dh)
    w2 = jnp.einsum('cef,cx,hy->chexyf', phi_w2, eye_c, eye_h).reshape(2 * H * Dh, 2 * H * Dh)
    b2 = jnp.broadcast_to(phi_b2[:, None, :], (2, H, Dh)).reshape(1, 2 * H * Dh)
    return wbig.astype(BF16), pe_rows, b1, w2.astype(BF16), b2


def _compress_in_kernel(x_ref, pe_ref, w_ref, z_ref):
    x = x_ref[0]
    n = w_ref.shape[1] // 2
    z_ref[0, :, :n] = jnp.dot((x + pe_ref[0]).astype(BF16), w_ref[:, :n], preferred_element_type=F32)
    z_ref[0, :, n:] = jnp.dot((x + pe_ref[1]).astype(BF16), w_ref[:, n:], preferred_element_type=F32)


def _compress_out_kernel(z_ref, b1_ref, w2_ref, b2_ref, o_ref):
    z = z_ref[0]
    n = z.shape[-1] // 2
    rows = z.shape[0]
    second = pltpu.roll(z[:, n:], rows - 1, axis=0)
    hdn = jax.nn.gelu(z[:, :n] + second + b1_ref[...])
    o_ref[0] = jnp.dot(hdn.astype(BF16), w2_ref[...], preferred_element_type=F32) + b2_ref[...]


def _compress(x2, tables, tr):
    wbig, pe_rows, b1, w2, b2 = tables
    B, n, K = x2.shape
    N2 = wbig.shape[1]
    z = pl.pallas_call(
        _compress_in_kernel,
        out_shape=jax.ShapeDtypeStruct((B, n, N2), F32),
        grid=(B, n // tr),
        in_specs=[pl.BlockSpec((1, tr, K), lambda b, i: (b, i, 0)),
                  pl.BlockSpec((2, 1, K), lambda b, i: (0, 0, 0)),
                  pl.BlockSpec((K, N2), lambda b, i: (0, 0))],
        out_specs=pl.BlockSpec((1, tr, N2), lambda b, i: (b, i, 0)),
        compiler_params=_cparams("parallel", "parallel"),
        name="compress_in",
    )(x2, pe_rows, wbig)
    return pl.pallas_call(
        _compress_out_kernel,
        out_shape=jax.ShapeDtypeStruct((B, n, N2 // 2), F32),
        grid=(B,),
        in_specs=[pl.BlockSpec((1, n, N2), lambda b: (b, 0, 0)),
                  pl.BlockSpec((1, N2 // 2), lambda b: (0, 0)),
                  pl.BlockSpec((N2 // 2, N2 // 2), lambda b: (0, 0)),
                  pl.BlockSpec((1, N2 // 2), lambda b: (0, 0))],
        out_specs=pl.BlockSpec((1, n, N2 // 2), lambda b: (b, 0, 0)),
        compiler_params=_cparams("parallel"),
        name="compress_out",
    )(z, b1, w2, b2)


def _rel_bucket(dist):
    n = jnp.maximum(dist, 0)
    max_exact = NUM_BUCKETS // 2
    nf = jnp.maximum(n, 1).astype(F32)
    large = max_exact + (jnp.log(nf / max_exact) / math.log(REL_MAX_DIST / max_exact)
                         * (NUM_BUCKETS - max_exact)).astype(jnp.int32)
    large = jnp.minimum(large, NUM_BUCKETS - 1)
    return jnp.where(n < max_exact, n, large)


def _bias_by_distance(rel_bias, n_max):
    return jnp.transpose(rel_bias[_rel_bucket(jnp.arange(n_max))].astype(F32))


def _pool_matrix(n_cmp_pad, n_blk_pad):
    r = SEL_BLOCK // CMP_STRIDE
    i = np.arange(n_cmp_pad)[None, :]
    j = np.arange(n_blk_pad)[:, None]
    return ((i >= r * j - 1) & (i <= r * j + r - 1)).astype(np.float32)


def _cmp_select_kernel(q_ref, k_ref, vt_ref, bias_ref, pool_ref, o_ref, sel_ref, *, tq, n_cmp):
    qt = pl.program_id(2)
    q = q_ref[0, 0].reshape(GQA * tq, HEAD_DIM)
    k = k_ref[0, 0]
    nc = k.shape[0]
    s = _nt_dot(k, q) * (HEAD_DIM ** -0.5)
    s = s + jnp.concatenate([bias_ref[g] for g in range(GQA)], axis=-1)
    t_pos = qt * tq + (lax.broadcasted_iota(jnp.int32, (nc, GQA * tq), 1) % tq)
    ci = lax.broadcasted_iota(jnp.int32, (nc, GQA * tq), 0)
    mask = (ci * CMP_STRIDE + CMP_BLOCK - 1 <= t_pos) & (ci < n_cmp)
    s = jnp.where(mask, s, NEG)
    m = jnp.max(s, axis=0, keepdims=True)
    p = jnp.where(mask, jnp.exp(s - m), 0.0)
    p = p / jnp.maximum(jnp.sum(p, axis=0, keepdims=True), 1e-30)
    ot = jnp.dot(vt_ref[0, 0], p.astype(BF16), preferred_element_type=F32)
    o_ref[0] = jnp.concatenate([ot[:, g * tq:(g + 1) * tq].T for g in range(GQA)], axis=-1)
    imp = p[:, 0:tq]
    for g in range(1, GQA):
        imp = imp + p[:, g * tq:(g + 1) * tq]
    sb = jnp.dot(pool_ref[...], imp, precision=HIGHEST, preferred_element_type=F32)
    nb = sb.shape[0]
    blk = lax.broadcasted_iota(jnp.int32, (nb, tq), 0)
    cur = (qt * tq + lax.broadcasted_iota(jnp.int32, (nb, tq), 1)) // SEL_BLOCK
    causal = blk <= cur
    forced = (blk == 0) | (blk == cur) | (blk == cur - 1)
    sc = jnp.where(forced & causal, 1e4, jnp.where(causal, sb, -1.0))
    rank = jnp.zeros((nb, tq), jnp.int32)
    for i in range(nb):
        row = sc[i:i + 1, :]
        ahead = (row > sc) | ((row == sc) & (blk > i))
        rank = rank + ahead.astype(jnp.int32)
    sel = ((rank < N_SEL) & causal).astype(F32)
    sel_ref[0, 0] = sel.T


def _cmp_select_prompt(q5, kc, vct, bias_t, pool, n_cmp):
    B, _, _, T, _ = q5.shape
    NC = kc.shape[2]
    NB = pool.shape[0]
    tq = ATT_TQ
    return pl.pallas_call(
        functools.partial(_cmp_select_kernel, tq=tq, n_cmp=n_cmp),
        out_shape=(jax.ShapeDtypeStruct((B, T, D_ATT), F32),
                   jax.ShapeDtypeStruct((B, N_KV_HEADS, T, NB), F32)),
        grid=(B, N_KV_HEADS, T // tq),
        in_specs=[pl.BlockSpec((1, 1, GQA, tq, HEAD_DIM), lambda b, h, i: (b, h, 0, i, 0)),
                  pl.BlockSpec((1, 1, NC, HEAD_DIM), lambda b, h, i: (b, h, 0, 0)),
                  pl.BlockSpec((1, 1, HEAD_DIM, NC), lambda b, h, i: (b, h, 0, 0)),
                  pl.BlockSpec((GQA, NC, tq), lambda b, h, i: (h, 0, i)),
                  pl.BlockSpec((NB, NC), lambda b, h, i: (0, 0))],
        out_specs=(pl.BlockSpec((1, tq, GQA * HEAD_DIM), lambda b, h, i: (b, i, h)),
                   pl.BlockSpec((1, 1, tq, NB), lambda b, h, i: (b, h, i, 0))),
        compiler_params=_cparams("parallel", "parallel", "parallel"),
        name="cmp_select_prompt",
    )(q5, kc, vct, bias_t, pool)


def _sel_win_kernel(q_ref, ks_ref, vs_ref, kw_ref, vw_ref, sel_ref, tz_ref, ex_ref, os_ref, ow_ref, *, tq):
    tk = ATT_TK
    qt = pl.program_id(2)
    q = q_ref[0, 0].reshape(GQA * tq, HEAD_DIM)
    selm = sel_ref[0, 0].astype(BF16)
    r = lax.broadcasted_iota(jnp.int32, (tq, tk), 0)
    c = lax.broadcasted_iota(jnp.int32, (tq, tk), 1)
    rc = r - c

    def make_step(k_ref, v_ref, use_sel):
        def step(kt, carry):
            m, l, acc = carry
            off = pl.multiple_of(kt * tk, tk)
            k = k_ref[0, 0, pl.ds(off, tk), :]
            v = v_ref[0, 0, pl.ds(off, tk), :]
            d = qt - kt
            s = _nt_dot(q, k).reshape(GQA, tq, tk) * (HEAD_DIM ** -0.5) + tz_ref[0, :, d]
            dist = d * tk + rc
            if use_sel:
                picked = jnp.dot(selm, ex_ref[kt], preferred_element_type=F32)
                mask = (picked > 0.5) & (dist >= 0)
            else:
                mask = (dist >= 0) & (dist <= WINDOW)
            mask = mask[None]
            s = jnp.where(mask, s, NEG)
            m_new = jnp.maximum(m, jnp.max(s, axis=-1, keepdims=True))
            alpha = jnp.exp(m - m_new)
            p = jnp.where(mask, jnp.exp(s - m_new), 0.0)
            l = alpha * l + jnp.sum(p, axis=-1, keepdims=True)
            pv = jnp.dot(p.reshape(GQA * tq, tk).astype(BF16), v, preferred_element_type=F32)
            acc = alpha * acc + pv.reshape(GQA, tq, HEAD_DIM)
            return m_new, l, acc
        return step

    def init():
        return (jnp.full((GQA, tq, 1), NEG, F32), jnp.zeros((GQA, tq, 1), F32),
                jnp.zeros((GQA, tq, HEAD_DIM), F32))

    def finish(carry):
        _, l, acc = carry
        o = acc / jnp.maximum(l, 1e-30)
        return jnp.concatenate([o[g] for g in range(GQA)], axis=-1)

    os_ref[0] = finish(lax.fori_loop(0, qt + 1, make_step(ks_ref, vs_ref, True), init()))
    lo = jnp.maximum(qt - WINDOW // tk, 0)
    ow_ref[0] = finish(lax.fori_loop(lo, qt + 1, make_step(kw_ref, vw_ref, False), init()))


def _toeplitz_bias_tiles(bias_n, n_diag, tq, tk):
    r = np.arange(tq)[:, None]
    c = np.arange(tk)[None, :]
    dist = np.maximum(np.arange(n_diag)[:, None, None] * tk + (r - c)[None], 0)
    return bias_n[:, dist].reshape(N_KV_HEADS, GQA, n_diag, tq, tk)


def _expand_matrix(n_blk, n_kt, tk):
    j = np.arange(n_blk)[None, :, None]
    key = np.arange(n_kt)[:, None, None] * tk + np.arange(tk)[None, None, :]
    return (key // SEL_BLOCK == j).astype(np.float32)


def _sel_win_prompt(q5, ks, vs, kw, vw, sel, tz, ex):
    B, _, _, T, _ = q5.shape
    NB = sel.shape[-1]
    tq = ATT_TQ
    n_diag = tz.shape[2]
    kv_spec = pl.BlockSpec((1, 1, T, HEAD_DIM), lambda b, h, i: (b, h, 0, 0))
    o_spec = pl.BlockSpec((1, tq, GQA * HEAD_DIM), lambda b, h, i: (b, i, h))
    return pl.pallas_call(
        functools.partial(_sel_win_kernel, tq=tq),
        out_shape=(jax.ShapeDtypeStruct((B, T, D_ATT), F32), jax.ShapeDtypeStruct((B, T, D_ATT), F32)),
        grid=(B, N_KV_HEADS, T // tq),
        in_specs=[pl.BlockSpec((1, 1, GQA, tq, HEAD_DIM), lambda b, h, i: (b, h, 0, i, 0)),
                  kv_spec, kv_spec, kv_spec, kv_spec,
                  pl.BlockSpec((1, 1, tq, NB), lambda b, h, i: (b, h, i, 0)),
                  pl.BlockSpec((1, GQA, n_diag, tq, ATT_TK), lambda b, h, i: (h, 0, 0, 0, 0)),
                  pl.BlockSpec(ex.shape, lambda b, h, i: (0, 0, 0))],
        out_specs=(o_spec, o_spec),
        compiler_params=_cparams("parallel", "parallel", "parallel"),
        name="sel_win_prompt",
    )(q5, ks, vs, kw, vw, sel, tz, ex)


def _gate_expand_matrix():
    m = np.zeros((3, LANE, D_ATT), np.float32)
    for r in range(3):
        for h in range(N_HEADS):
            m[r, h * 3 + r, h * HEAD_DIM:(h + 1) * HEAD_DIM] = 1.0
    return m


def _post_mixer_kernel(y_ref, u_ref, oc_ref, os_ref, ow_ref, g_ref, x_ref, gate_ref, sh_ref, sc_ref,
                       dskip_ref, wglu_ref, bglu_ref, gexp_ref, wout_ref, lng_ref, lnb_ref,
                       wr_ref, br_ref, x1_ref, hm_ref, te_ref, tw_ref):
    y = y_ref[0] + dskip_ref[...] * u_ref[0]
    gl = jax.nn.gelu(y)
    ssm = gl * jax.nn.sigmoid(jnp.dot(gl.astype(BF16), wglu_ref[...], preferred_element_type=F32)
                              + bglu_ref[...])
    sg = jax.nn.sigmoid(g_ref[0])
    att = jnp.zeros_like(oc_ref[0])
    for r, o_ref in enumerate((oc_ref, os_ref, ow_ref)):
        att = att + jnp.dot(sg, gexp_ref[r], precision=HIGHEST, preferred_element_type=F32) * o_ref[0]
    h = (jnp.dot(ssm.astype(BF16), wout_ref[:D_SSM, :], preferred_element_type=F32)
         + jnp.dot(att.astype(BF16), wout_ref[D_SSM:, :], preferred_element_type=F32))
    z = DN_ALPHA * x_ref[0] + gate_ref[0] * h
    x1 = _layer_norm(z) * lng_ref[...] + lnb_ref[...]
    x1_ref[0] = x1
    hm = _layer_norm(x1) * (1.0 + sc_ref[0]) + sh_ref[0]
    hm_ref[0] = hm.astype(BF16)
    logits = jnp.dot(hm, wr_ref[...], precision=HIGHEST, preferred_element_type=F32) + br_ref[...]
    lane = lax.broadcasted_iota(jnp.int32, logits.shape, 1)
    work = jnp.where(lane < N_EXPERTS, logits, -jnp.inf)
    te = jnp.zeros(logits.shape, jnp.int32)
    tv = jnp.zeros(logits.shape, F32)
    for k in range(TOP_K):
        best = jnp.max(work, axis=-1, keepdims=True)
        arg = jnp.min(jnp.where(work == best, lane, LANE), axis=-1, keepdims=True)
        te = jnp.where(lane == k, arg, te)
        tv = jnp.where(lane == k, best, tv)
        work = jnp.where(lane == arg, -jnp.inf, work)
    ex = jnp.where(lane < TOP_K, jnp.exp(tv - tv[:, 0:1]), 0.0)
    te_ref[0] = te
    tw_ref[0] = ex / jnp.sum(ex, axis=-1, keepdims=True)


def _post_mixer(y, u, oc, osel, ow, g, x, gate, shift, scale, w, tm):
    B, T, D = x.shape
    R = gate.shape[1]
    rb = 1 if R == 1 else tm
    mod_map = (lambda b, i: (b, 0, 0)) if R == 1 else (lambda b, i: (b, i, 0))
    row = lambda n: pl.BlockSpec((1, tm, n), lambda b, i: (b, i, 0))
    mod = pl.BlockSpec((1, rb, D), mod_map)
    full = lambda a: pl.BlockSpec(a.shape, lambda b, i: (0,) * a.ndim)
    consts = (w['d_skip'], w['w_glu'], w['b_glu'], w['gexp'], w['w_out'], w['ln1_g'], w['ln1_b'],
              w['w_router'], w['b_router'])
    return pl.pallas_call(
        _post_mixer_kernel,
        out_shape=(jax.ShapeDtypeStruct((B, T, D), F32), jax.ShapeDtypeStruct((B, T, D), BF16),
                   jax.ShapeDtypeStruct((B, T, LANE), jnp.int32), jax.ShapeDtypeStruct((B, T, LANE), F32)),
        grid=(B, T // tm),
        in_specs=[row(D_SSM), row(D_SSM), row(D_ATT), row(D_ATT), row(D_ATT), row(LANE), row(D),
                  mod, mod, mod] + [full(a) for a in consts],
        out_specs=(row(D), row(D), row(LANE), row(LANE)),
        compiler_params=_cparams("parallel", "parallel"),
        name="post_mixer",
    )(y, u, oc, osel, ow, g, x, gate, shift, scale, *consts)


def _expert_kernel(be_ref, nu_ref, x_ref, wgu_ref, bgu_ref, wd_ref, bd_ref, o_ref, wgu_s, wd_s):
    i = pl.program_id(0)
    prev = be_ref[jnp.maximum(i - 1, 0)]
    fresh = (i == 0) | (be_ref[i] != prev)

    @pl.when(fresh)
    def _():
        wgu_s[...] = wgu_ref[0].astype(BF16)
        wd_s[...] = wd_ref[0].astype(BF16)

    @pl.when(i < nu_ref[0])
    def _():
        gu = jnp.dot(x_ref[...], wgu_s[...], preferred_element_type=F32) + bgu_ref[0]
        gate = jnp.minimum(gu[:, :D_FF], SWIGLU_LIMIT)
        up = jnp.clip(gu[:, D_FF:], -SWIGLU_LIMIT, SWIGLU_LIMIT)
        hh = (up + 1.0) * gate * jax.nn.sigmoid(SWIGLU_ALPHA * gate)
        o_ref[...] = jnp.dot(hh.astype(BF16), wd_s[...], preferred_element_type=F32) + bd_ref[0]

    @pl.when(i >= nu_ref[0])
    def _():
        o_ref[...] = jnp.zeros_like(o_ref)


def _experts(xb, blk_e, n_used, w_gate_up, b_gate_up, w_down, b_down):
    rows, D = xb.shape
    n_blk = rows // MOE_ROWS
    grid_spec = pltpu.PrefetchScalarGridSpec(
        num_scalar_prefetch=2,
        grid=(n_blk,),
        in_specs=[pl.BlockSpec((MOE_ROWS, D), lambda i, be, nu: (i, 0)),
                  pl.BlockSpec((1, D, 2 * D_FF), lambda i, be, nu: (be[i], 0, 0)),
                  pl.BlockSpec((1, 1, 2 * D_FF), lambda i, be, nu: (be[i], 0, 0)),
                  pl.BlockSpec((1, D_FF, D), lambda i, be, nu: (be[i], 0, 0)),
                  pl.BlockSpec((1, 1, D), lambda i, be, nu: (be[i], 0, 0))],
        out_specs=pl.BlockSpec((MOE_ROWS, D), lambda i, be, nu: (i, 0)),
        scratch_shapes=[pltpu.VMEM((D, 2 * D_FF), BF16), pltpu.VMEM((D_FF, D), BF16)],
    )
    return pl.pallas_call(
        _expert_kernel,
        out_shape=jax.ShapeDtypeStruct((rows, D), F32),
        grid_spec=grid_spec,
        compiler_params=_cparams("arbitrary"),
        name="moe_experts",
    )(blk_e, n_used, xb, w_gate_up, b_gate_up.reshape(N_EXPERTS, 1, 2 * D_FF), w_down,
      b_down.reshape(N_EXPERTS, 1, D))


def _moe_dispatch(top_e, n):
    blk = MOE_ROWS
    nk = n * TOP_K
    e = top_e.reshape(-1)
    order = jnp.argsort(e)
    e_s = e[order]
    counts = jnp.bincount(e, length=N_EXPERTS)
    pcounts = (counts + blk - 1) // blk * blk
    start = jnp.cumsum(counts) - counts
    pend = jnp.cumsum(pcounts)
    pstart = pend - pcounts
    dest_sorted = (pstart[e_s] + jnp.arange(nk) - start[e_s]).astype(jnp.int32)
    n_blk = (nk + N_EXPERTS * (blk - 1)) // blk
    rows = n_blk * blk
    row_tok = jnp.full((rows,), n, jnp.int32).at[dest_sorted].set((order // TOP_K).astype(jnp.int32))
    dest = jnp.zeros((nk,), jnp.int32).at[order].set(dest_sorted)
    blk_e = jnp.minimum(jnp.searchsorted(pend, jnp.arange(n_blk) * blk, side='right'),
                        N_EXPERTS - 1).astype(jnp.int32)
    n_used = (pend[-1] // blk).astype(jnp.int32).reshape(1)
    return row_tok, dest.reshape(n, TOP_K), blk_e, n_used


def _final_kernel(x_ref, y0_ref, y1_ref, y2_ref, y3_ref, tw_ref, gate_ref, lng_ref, lnb_ref, o_ref):
    tw = tw_ref[0]
    y = jnp.zeros_like(x_ref[0])
    for k, y_ref in enumerate((y0_ref, y1_ref, y2_ref, y3_ref)):
        y = y + tw[:, k:k + 1] * y_ref[0]
    z = DN_ALPHA * x_ref[0] + gate_ref[0] * y
    o_ref[0] = _layer_norm(z) * lng_ref[...] + lnb_ref[...]


def _final(x1, ys, tw, gate, ln_g, ln_b, tm):
    B, T, D = x1.shape
    R = gate.shape[1]
    rb = 1 if R == 1 else tm
    mod_map = (lambda b, i: (b, 0, 0)) if R == 1 else (lambda b, i: (b, i, 0))
    row = lambda n: pl.BlockSpec((1, tm, n), lambda b, i: (b, i, 0))
    vec = pl.BlockSpec((1, D), lambda b, i: (0, 0))
    return pl.pallas_call(
        _final_kernel,
        out_shape=jax.ShapeDtypeStruct((B, T, D), F32),
        grid=(B, T // tm),
        in_specs=[row(D), row(D), row(D), row(D), row(D), row(LANE),
                  pl.BlockSpec((1, rb, D), mod_map), vec, vec],
        out_specs=row(D),
        compiler_params=_cparams("parallel", "parallel"),
        name="moe_combine_ln",
    )(x1, *ys, tw, gate, ln_g, ln_b)


def _cmp_select_step_kernel(q_ref, k_ref, v_ref, bias_ref, pool_ref, o_ref, idx_ref, *, n_cmp, n_blk, q_pos):
    q = q_ref[0].astype(BF16)
    ncp = k_ref.shape[2]
    nbp = pool_ref.shape[1]
    row = lax.broadcasted_iota(jnp.int32, (N_HEADS, 1), 0)
    first = row < GQA
    s = jnp.where(first, _nt_dot(q, k_ref[0, 0]), _nt_dot(q, k_ref[0, 1])) * (HEAD_DIM ** -0.5)
    s = s + bias_ref[...]
    ci = lax.broadcasted_iota(jnp.int32, (N_HEADS, ncp), 1)
    mask = (ci * CMP_STRIDE + CMP_BLOCK - 1 <= q_pos) & (ci < n_cmp)
    s = jnp.where(mask, s, NEG)
    m = jnp.max(s, axis=-1, keepdims=True)
    p = jnp.where(mask, jnp.exp(s - m), 0.0)
    p = p / jnp.maximum(jnp.sum(p, axis=-1, keepdims=True), 1e-30)
    pb = p.astype(BF16)
    o_ref[0] = jnp.where(first, jnp.dot(pb, v_ref[0, 0], preferred_element_type=F32),
                         jnp.dot(pb, v_ref[0, 1], preferred_element_type=F32))
    imp0 = jnp.sum(jnp.where(first, p, 0.0), axis=0, keepdims=True)
    imp1 = jnp.sum(jnp.where(first, 0.0, p), axis=0, keepdims=True)
    imp = jnp.where(first, imp0, imp1)
    sb = jnp.dot(imp, pool_ref[...], precision=HIGHEST, preferred_element_type=F32)
    cur = q_pos // SEL_BLOCK
    bi = lax.broadcasted_iota(jnp.int32, (nbp, nbp), 0)
    bj = lax.broadcasted_iota(jnp.int32, (nbp, nbp), 1)
    blk = lax.broadcasted_iota(jnp.int32, (1, nbp), 1)
    causal = blk <= cur
    forced = (blk == 0) | (blk == cur) | (blk == cur - 1)
    rsel = lax.broadcasted_iota(jnp.int32, (N_SEL, nbp), 0)
    for h in range(N_KV_HEADS):
        sc = jnp.where(forced & causal, 1e4, jnp.where(causal, sb[h * GQA:h * GQA + 1, :], -1.0))
        sc = jnp.where(blk < n_blk, sc, -2.0)
        scb = jnp.broadcast_to(sc, (nbp, nbp))
        col = jnp.sum(jnp.where(bi == bj, scb, 0.0), axis=1, keepdims=True)
        ahead = (col > scb) | ((col == scb) & (bi < bj))
        rank = jnp.sum(ahead.astype(jnp.int32), axis=0, keepdims=True)
        hit = jnp.broadcast_to(rank, (N_SEL, nbp)) == rsel
        idx = jnp.sum(jnp.where(hit, jnp.broadcast_to(blk, (N_SEL, nbp)), 0), axis=1, keepdims=True)
        idx_ref[0, h] = jnp.broadcast_to(idx, (N_SEL, LANE))


def _cmp_select_step(q, kc, vc, bias, pool, n_cmp, n_blk, q_pos):
    B = q.shape[0]
    NCp = kc.shape[2]
    return pl.pallas_call(
        functools.partial(_cmp_select_step_kernel, n_cmp=n_cmp, n_blk=n_blk, q_pos=q_pos),
        out_shape=(jax.ShapeDtypeStruct((B, N_HEADS, HEAD_DIM), F32),
                   jax.ShapeDtypeStruct((B, N_KV_HEADS, N_SEL, LANE), jnp.int32)),
        grid=(B,),
        in_specs=[pl.BlockSpec((1, N_HEADS, HEAD_DIM), lambda b: (b, 0, 0)),
                  pl.BlockSpec((1, N_KV_HEADS, NCp, HEAD_DIM), lambda b: (b, 0, 0, 0)),
                  pl.BlockSpec((1, N_KV_HEADS, NCp, HEAD_DIM), lambda b: (b, 0, 0, 0)),
                  pl.BlockSpec(bias.shape, lambda b: (0, 0)),
                  pl.BlockSpec(pool.shape, lambda b: (0, 0))],
        out_specs=(pl.BlockSpec((1, N_HEADS, HEAD_DIM), lambda b: (b, 0, 0)),
                   pl.BlockSpec((1, N_KV_HEADS, N_SEL, LANE), lambda b: (b, 0, 0, 0))),
        compiler_params=_cparams("parallel"),
        name="cmp_select_step",
    )(q, kc, vc, bias, pool)


def _sel_step_kernel(phys_ref, idx_ref, q_ref, pool_ref, new_ref, bias_ref, o_ref, m_s, l_s, acc_s,
                     *, n_past, q_pos):
    b, h, j = pl.program_id(0), pl.program_id(1), pl.program_id(2)
    idx = idx_ref[(b * N_KV_HEADS + h) * N_SEL + j]

    @pl.when(j == 0)
    def _():
        m_s[...] = jnp.full_like(m_s, NEG)
        l_s[...] = jnp.zeros_like(l_s)
        acc_s[...] = jnp.zeros_like(acc_s)

    kv = jnp.where(idx >= n_past, new_ref[0], pool_ref[0])
    k = jnp.where(h == 0, kv[:, 0:HEAD_DIM], kv[:, HEAD_DIM:2 * HEAD_DIM]).astype(BF16)
    v = jnp.where(h == 0, kv[:, 2 * HEAD_DIM:3 * HEAD_DIM], kv[:, 3 * HEAD_DIM:]).astype(BF16)
    s = _nt_dot(q_ref[0].astype(BF16), k) * (HEAD_DIM ** -0.5) + bias_ref[0]
    kpos = idx * SEL_BLOCK + lax.broadcasted_iota(jnp.int32, s.shape, 1)
    mask = (kpos <= q_pos) & (idx <= q_pos // SEL_BLOCK)
    s = jnp.where(mask, s, NEG)
    m_new = jnp.maximum(m_s[...], jnp.max(s, axis=-1, keepdims=True))
    alpha = jnp.exp(m_s[...] - m_new)
    p = jnp.where(mask, jnp.exp(s - m_new), 0.0)
    l_s[...] = alpha * l_s[...] + jnp.sum(p, axis=-1, keepdims=True)
    acc_s[...] = alpha * acc_s[...] + jnp.dot(p.astype(BF16), v, preferred_element_type=F32)
    m_s[...] = m_new

    @pl.when(j == N_SEL - 1)
    def _():
        o_ref[0, 0] = acc_s[...] / jnp.maximum(l_s[...], 1e-30)


def _sel_step(q, pool_blocks, new_blocks, bias_blk, phys, idx_flat, n_past, q_pos):
    B = q.shape[0]
    grid_spec = pltpu.PrefetchScalarGridSpec(
        num_scalar_prefetch=2,
        grid=(B, N_KV_HEADS, N_SEL),
        in_specs=[pl.BlockSpec((1, N_HEADS, HEAD_DIM), lambda b, h, j, ph, ix: (b, 0, 0)),
                  pl.BlockSpec((1, SEL_BLOCK, D_KV),
                               lambda b, h, j, ph, ix: (ph[(b * N_KV_HEADS + h) * N_SEL + j], 0, 0)),
                  pl.BlockSpec((1, SEL_BLOCK, D_KV), lambda b, h, j, ph, ix: (b, 0, 0)),
                  pl.BlockSpec((1, N_HEADS, SEL_BLOCK),
                               lambda b, h, j, ph, ix: (ix[(b * N_KV_HEADS + h) * N_SEL + j], 0, 0))],
        out_specs=pl.BlockSpec((1, 1, N_HEADS, HEAD_DIM), lambda b, h, j, ph, ix: (b, h, 0, 0)),
        scratch_shapes=[pltpu.VMEM((N_HEADS, 1), F32), pltpu.VMEM((N_HEADS, 1), F32),
                        pltpu.VMEM((N_HEADS, HEAD_DIM), F32)],
    )
    return pl.pallas_call(
        functools.partial(_sel_step_kernel, n_past=n_past, q_pos=q_pos),
        out_shape=jax.ShapeDtypeStruct((B, N_KV_HEADS, N_HEADS, HEAD_DIM), F32),
        grid_spec=grid_spec,
        compiler_params=_cparams("arbitrary", "arbitrary", "arbitrary"),
        name="sel_step",
    )(phys, idx_flat, q, pool_blocks, new_blocks, bias_blk)


def _win_step_kernel(q_ref, w_ref, new_ref, bias_ref, bias0_ref, o_ref):
    q = q_ref[0]
    qb = q.astype(BF16)
    row = lax.broadcasted_iota(jnp.int32, (N_HEADS, 1), 0)
    first = row < GQA
    w = w_ref[0]
    hd = HEAD_DIM
    kb = [w[:, h * hd:(h + 1) * hd].astype(BF16) for h in range(N_KV_HEADS)]
    vb = [w[:, (N_KV_HEADS + h) * hd:(N_KV_HEADS + h + 1) * hd].astype(BF16) for h in range(N_KV_HEADS)]
    s = jnp.where(first, _nt_dot(qb, kb[0]), _nt_dot(qb, kb[1])) * (hd ** -0.5) + bias_ref[...]
    new = new_ref[0]
    kn = jnp.where(first, new[:, 0:hd], new[:, hd:2 * hd])
    vn = jnp.where(first, new[:, 2 * hd:3 * hd], new[:, 3 * hd:])
    sn = jnp.sum(q * kn, axis=-1, keepdims=True) * (hd ** -0.5) + bias0_ref[...]
    m = jnp.maximum(jnp.max(s, axis=-1, keepdims=True), sn)
    p = jnp.exp(s - m)
    pn = jnp.exp(sn - m)
    l = jnp.sum(p, axis=-1, keepdims=True) + pn
    pb = p.astype(BF16)
    acc = jnp.where(first, jnp.dot(pb, vb[0], preferred_element_type=F32),
                    jnp.dot(pb, vb[1], preferred_element_type=F32)) + pn * vn
    o_ref[0] = acc / jnp.maximum(l, 1e-30)


def _win_step(q, win, new, bias, bias0):
    B, W, _ = win.shape
    return pl.pallas_call(
        _win_step_kernel,
        out_shape=jax.ShapeDtypeStruct((B, N_HEADS, HEAD_DIM), F32),
        grid=(B,),
        in_specs=[pl.BlockSpec((1, N_HEADS, HEAD_DIM), lambda b: (b, 0, 0)),
                  pl.BlockSpec((1, W, D_KV), lambda b: (b, 0, 0)),
                  pl.BlockSpec((1, 1, D_KV), lambda b: (b, 0, 0)),
                  pl.BlockSpec((N_HEADS, W), lambda b: (0, 0)),
                  pl.BlockSpec((N_HEADS, 1), lambda b: (0, 0))],
        out_specs=pl.BlockSpec((1, N_HEADS, HEAD_DIM), lambda b: (b, 0, 0)),
        compiler_params=_cparams("parallel"),
        name="win_step",
    )(q, win, new, bias, bias0)


def _split_heads(kv, dtype):
    B, L, _ = kv.shape
    kv5 = kv.reshape(B, L, 2, N_KV_HEADS, HEAD_DIM)
    return (jnp.transpose(kv5[:, :, 0], (0, 2, 1, 3)).astype(dtype),
            jnp.transpose(kv5[:, :, 1], (0, 2, 1, 3)).astype(dtype))


def _nsa_prompt(q, kvc, kvs, kvw, cmp_tab, rel_bias):
    B, T, _ = q.shape
    nc = T // CMP_STRIDE
    nb = T // SEL_BLOCK
    ckv = _compress(kvc.reshape(B, nc, CMP_STRIDE * D_KV), cmp_tab, tr=min(nc, 256))
    kc, vc = _split_heads(ckv, BF16)
    vct = jnp.transpose(vc, (0, 1, 3, 2))
    bias_n = _bias_by_distance(rel_bias, T)
    dist_ct = np.maximum(np.arange(T)[None, :] - (np.arange(nc)[:, None] * CMP_STRIDE + CMP_BLOCK - 1), 0)
    bias_ct = bias_n[:, dist_ct]
    pool = jnp.asarray(_pool_matrix(nc, nb))
    q5 = jnp.transpose(q.reshape(B, T, N_KV_HEADS, GQA, HEAD_DIM), (0, 2, 3, 1, 4))
    o_cmp, sel = _cmp_select_prompt(q5, kc, vct, bias_ct, pool, nc - 1)
    ks, vs = _split_heads(kvs, BF16)
    kw, vw = _split_heads(kvw, BF16)
    tz = _toeplitz_bias_tiles(bias_n, T // ATT_TK, ATT_TQ, ATT_TK)
    ex = jnp.asarray(_expand_matrix(nb, T // ATT_TK, ATT_TK), dtype=BF16)
    o_sel, o_win = _sel_win_prompt(q5, ks, vs, kw, vw, sel, tz, ex)
    return o_cmp, o_sel, o_win


def _nsa_sample(q, kvc, kvs, kvw, pool_cmp, pool_sel, win_buf, page_table, cmp_tab, rel_bias):
    B = q.shape[0]
    n_pages = page_table.shape[1]
    past_len = n_pages * PAGE_SIZE
    q_pos = past_len
    lp = -(-(past_len + 1) // SEL_BLOCK) * SEL_BLOCK
    n_cmp = lp // CMP_STRIDE - 1
    n_blk = lp // SEL_BLOCK
    n_chunks = -(-(n_cmp + 1) // 24) * 24
    past = pool_cmp[page_table].reshape(B, past_len, D_KV)
    full = jnp.concatenate([past, kvc[:, None, :],
                            jnp.zeros((B, n_chunks * CMP_STRIDE - past_len - 1, D_KV), F32)], 1)
    ckv = _compress(full.reshape(B, n_chunks, CMP_STRIDE * D_KV), cmp_tab, tr=n_chunks // 3)
    ncp = -(-n_chunks // LANE) * LANE
    nbp = -(-n_blk // LANE) * LANE
    ckv = jnp.pad(ckv, ((0, 0), (0, ncp - n_chunks), (0, 0)))
    kc, vc = _split_heads(ckv, BF16)
    bias_n = _bias_by_distance(rel_bias, q_pos + 1)
    dist_c = np.maximum(q_pos - (np.arange(ncp) * CMP_STRIDE + CMP_BLOCK - 1), 0)
    pool = jnp.asarray(_pool_matrix(ncp, nbp).T)
    q3 = q.reshape(B, N_HEADS, HEAD_DIM)
    o_cmp, idx = _cmp_select_step(q3, kc, vc, bias_n[:, dist_c], pool, n_cmp, n_blk, q_pos)
    idx = idx[..., 0]
    bpp = PAGE_SIZE // SEL_BLOCK
    n_past = n_pages * bpp
    past_idx = jnp.minimum(idx, n_past - 1)
    phys = jnp.take_along_axis(page_table, (past_idx // bpp).reshape(B, -1), axis=1).reshape(idx.shape)
    phys = (phys * bpp + past_idx % bpp).reshape(-1).astype(jnp.int32)
    pool_blocks = pool_sel.reshape(pool_sel.shape[0] * bpp, SEL_BLOCK, D_KV)
    new_blocks = jnp.pad(kvs[:, None, :], ((0, 0), (0, SEL_BLOCK - 1), (0, 0)))
    dist_b = np.maximum(q_pos - (np.arange(n_blk)[:, None] * SEL_BLOCK + np.arange(SEL_BLOCK)[None, :]), 0)
    bias_blk = jnp.transpose(bias_n[:, dist_b], (1, 0, 2))
    o_sel = _sel_step(q3, pool_blocks, new_blocks, bias_blk, phys, idx.reshape(-1).astype(jnp.int32),
                      n_past, q_pos)
    o_sel = jnp.concatenate([o_sel[:, h, h * GQA:(h + 1) * GQA] for h in range(N_KV_HEADS)], axis=1)
    wb = win_buf.shape[1]
    bias_w = bias_n[:, wb - np.arange(wb)]
    o_win = _win_step(q3, win_buf.reshape(B, wb, D_KV), kvw[:, None, :], bias_w, bias_n[:, 0:1])
    return o_cmp.reshape(B, D_ATT), o_sel.reshape(B, D_ATT), o_win.reshape(B, D_ATT)


def kernel(x_prompt, x_sample, cache_cmp_kv, cache_sel_kv, state_win_kv, state_ssm_re, state_ssm_im, page_table,
           c_prompt, c_sample, w_ada, b_ada, w_in, lam_re, lam_im, log_dt, b_re, b_im, c_re, c_im, d_skip,
           w_glu, b_glu, phi_pe, phi_w1, phi_b1, phi_w2, phi_b2, rel_bias, w_out, ln1_g, ln1_b,
           w_router, b_router, w_gate_up, b_gate_up, w_down, b_down, ln2_g, ln2_b):
    assert w_ada.shape[0] == DEPTH == 1
    l = 0
    Bp, T, D = x_prompt.shape
    Bs = x_sample.shape[0]
    kv_tail = (2, N_KV_HEADS, HEAD_DIM)

    n_c = Bp + Bs
    c_all = jnp.pad(jnp.concatenate([c_prompt, c_sample], 0), ((0, -n_c % 8), (0, 0)))
    m_all = _adaln(c_all, w_ada[l], b_ada[l])
    m_p = m_all[:Bp].reshape(Bp, 6, D)
    m_s = m_all[Bp:n_c].reshape(Bs, 6, D)
    mod_p = [m_p[:, i:i + 1, :] for i in range(6)]
    mod_s = [m_s[None, :, i, :] for i in range(6)]

    w_in_pad = jnp.pad(w_in[l], ((0, 0), (0, D_IN_PAD - D_IN))).astype(BF16)
    n_levels = max(1, int(math.log2(T // SSM_CHUNK)))
    ssm_tab = _ssm_tables(lam_re[l], lam_im[l], log_dt[l], b_re[l], b_im[l], c_re[l], c_im[l],
                          SSM_CHUNK, n_levels)
    cmp_tab = _compress_tables(phi_pe[l], phi_w1[l], phi_b1[l], phi_w2[l], phi_b2[l])
    w_post = dict(
        d_skip=d_skip[l].reshape(1, D_SSM), w_glu=w_glu[l].astype(BF16), b_glu=b_glu[l].reshape(1, D_SSM),
        gexp=jnp.asarray(_gate_expand_matrix()), w_out=w_out[l].astype(BF16),
        ln1_g=ln1_g[l].reshape(1, D), ln1_b=ln1_b[l].reshape(1, D),
        w_router=jnp.pad(w_router[l], ((0, 0), (0, LANE - N_EXPERTS))),
        b_router=jnp.pad(b_router[l], (0, LANE - N_EXPERTS)).reshape(1, LANE))

    u, q, kvc, kvs, kvw, g = _mixer_in(x_prompt, mod_p[0], mod_p[1], w_in_pad, tm=512)
    y_ssm, h_p = _ssm_prompt(u, ssm_tab)
    o_cmp, o_sel, o_win = _nsa_prompt(q, kvc, kvs, kvw, cmp_tab, rel_bias)
    x1_p, hm_p, te_p, tw_p = _post_mixer(y_ssm, u, o_cmp, o_sel, o_win, g, x_prompt,
                                         mod_p[2], mod_p[3], mod_p[4], w_post, tm=256)

    u_s, q_s, kvc_s, kvs_s, kvw_s, g_s = _mixer_in(x_sample.reshape(1, Bs, D), mod_s[0], mod_s[1],
                                                   w_in_pad, tm=Bs)
    y_s, h_s = _ssm_sample(u_s[0], state_ssm_re[l], state_ssm_im[l], ssm_tab, c_re[l], c_im[l])
    oc_s, os_s, ow_s = _nsa_sample(q_s[0].astype(F32), kvc_s[0], kvs_s[0], kvw_s[0], cache_cmp_kv[l],
                                   cache_sel_kv[l], state_win_kv[l], page_table, cmp_tab, rel_bias)
    x1_s, hm_s, te_s, tw_s = _post_mixer(y_s[None], u_s, oc_s[None], os_s[None], ow_s[None], g_s,
                                         x_sample.reshape(1, Bs, D), mod_s[2], mod_s[3], mod_s[4],
                                         w_post, tm=Bs)

    n_p = Bp * T
    n_all = n_p + Bs
    hm_all = jnp.concatenate([hm_p.reshape(n_p, D), hm_s.reshape(Bs, D)], 0)
    te_all = jnp.concatenate([te_p.reshape(n_p, LANE), te_s.reshape(Bs, LANE)], 0)[:, :TOP_K]
    row_tok, dest, blk_e, n_used = _moe_dispatch(te_all, n_all)
    xb = jnp.concatenate([hm_all, jnp.zeros((1, D), BF16)], 0)[row_tok]
    yb = _experts(xb, blk_e, n_used, w_gate_up[l], b_gate_up[l], w_down[l], b_down[l])
    ys = [yb[dest[:, k]] for k in range(TOP_K)]
    ln2g, ln2b = ln2_g[l].reshape(1, D), ln2_b[l].reshape(1, D)
    out_p = _final(x1_p, [y[:n_p].reshape(Bp, T, D) for y in ys], tw_p, mod_p[5], ln2g, ln2b, tm=512)
    out_s = _final(x1_s, [y[n_p:].reshape(1, Bs, D) for y in ys], tw_s, mod_s[5], ln2g, ln2b, tm=Bs)

    wlen = min(WINDOW, T)
    win_s = jnp.concatenate([state_win_kv[l], kvw_s[0].reshape(Bs, 1, *kv_tail)], 1)[:, -state_win_kv.shape[2]:]
    p_state = SSM_STATE
    return (out_p, out_s.reshape(Bs, 1, D),
            kvc.reshape(1, Bp, T, *kv_tail), kvc_s[0].reshape(1, Bs, 1, *kv_tail),
            kvs.reshape(1, Bp, T, *kv_tail), kvs_s[0].reshape(1, Bs, 1, *kv_tail),
            kvw[:, T - wlen:].reshape(1, Bp, wlen, *kv_tail), win_s[None],
            h_p[None, ..., :p_state], h_p[None, ..., p_state:],
            h_s[None, ..., :p_state], h_s[None, ..., p_state:])
```

```python
import functools
import math

import numpy as np
import jax
import jax.numpy as jnp
from jax import lax
from jax.experimental import pallas as pl
from jax.experimental.pallas import tpu as pltpu

D_MODEL = 1024
DEPTH = 1
PAST_LEN = 16384
PAGE_SIZE = 128
D_SSM = 512
SSM_GROUP = 16
N_SSM_GROUPS = D_SSM // SSM_GROUP
SSM_STATE = 64
N_HEADS = 8
HEAD_DIM = 64
N_KV_HEADS = 2
GQA = N_HEADS // N_KV_HEADS
D_ATT = N_HEADS * HEAD_DIM
D_KV = 2 * N_KV_HEADS * HEAD_DIM
CMP_STRIDE = 16
CMP_BLOCK = 2 * CMP_STRIDE
SEL_BLOCK = 64
N_SEL = 16
WINDOW = 512
NUM_BUCKETS = 32
REL_MAX_DIST = 1024
N_EXPERTS = 32
TOP_K = 4
D_FF = 1024
SWIGLU_LIMIT = 7.0
SWIGLU_ALPHA = 1.702
DN_ALPHA = (2 * DEPTH) ** 0.25
D_IN = D_SSM + D_ATT + 3 * D_KV + 3 * N_HEADS
NEG = -1e30
F32 = jnp.float32
BF16 = jnp.bfloat16
HIGHEST = lax.Precision.HIGHEST

LANE = 128
D_IN_PAD = 1920
GATE_COL = D_SSM + D_ATT + 3 * D_KV
SSM_CHUNK = 16
ATT_TQ = 128
ATT_TK = 128
MOE_ROWS = 256
VMEM_LIMIT = 48 * 1024 * 1024
LN_EPS = 1e-5


def _cparams(*sem):
    return pltpu.CompilerParams(dimension_semantics=sem, vmem_limit_bytes=VMEM_LIMIT)


def _nt_dot(a, b):
    return lax.dot_general(a, b, (((1,), (1,)), ((), ())), preferred_element_type=F32)


def _layer_norm(x):
    mu = jnp.mean(x, axis=-1, keepdims=True)
    xc = x - mu
    var = jnp.mean(xc * xc, axis=-1, keepdims=True)
    return xc * lax.rsqrt(var + LN_EPS)


def _adaln_kernel(c_ref, w_ref, b_ref, o_ref):
    c = c_ref[...]
    s = c * jax.nn.sigmoid(c)
    o_ref[...] = jnp.dot(s, w_ref[...], precision=HIGHEST, preferred_element_type=F32) + b_ref[...]


def _adaln(c, w, b):
    n, d = c.shape
    dout = w.shape[1]
    tn = 1024
    return pl.pallas_call(
        _adaln_kernel,
        out_shape=jax.ShapeDtypeStruct((n, dout), F32),
        grid=(dout // tn,),
        in_specs=[pl.BlockSpec((n, d), lambda j: (0, 0)),
                  pl.BlockSpec((d, tn), lambda j: (0, j)),
                  pl.BlockSpec((1, tn), lambda j: (0, j))],
        out_specs=pl.BlockSpec((n, tn), lambda j: (0, j)),
        compiler_params=_cparams("arbitrary"),
        name="adaln",
    )(c, w, b.reshape(1, dout))


def _mixer_in_kernel(x_ref, sh_ref, sc_ref, w_ref, u_ref, q_ref, kvc_ref, kvs_ref, kvw_ref, g_ref):
    h = _layer_norm(x_ref[0]) * (1.0 + sc_ref[0]) + sh_ref[0]
    z = jnp.dot(h.astype(BF16), w_ref[...], preferred_element_type=F32)
    c0 = D_SSM
    c1 = c0 + D_ATT
    c2 = c1 + D_KV
    c3 = c2 + D_KV
    c4 = c3 + D_KV
    u_ref[0] = z[:, :c0]
    q_ref[0] = z[:, c0:c1].astype(BF16)
    kvc_ref[0] = z[:, c1:c2]
    kvs_ref[0] = z[:, c2:c3]
    kvw_ref[0] = z[:, c3:c4]
    g_ref[0] = z[:, c4:c4 + LANE]


def _mixer_in(x, shift, scale, w_pad, tm):
    B, T, D = x.shape
    R = shift.shape[1]
    rb = 1 if R == 1 else tm
    mod_map = (lambda b, i: (b, 0, 0)) if R == 1 else (lambda b, i: (b, i, 0))
    row = lambda n: pl.BlockSpec((1, tm, n), lambda b, i: (b, i, 0))
    outs = (jax.ShapeDtypeStruct((B, T, D_SSM), F32), jax.ShapeDtypeStruct((B, T, D_ATT), BF16),
            jax.ShapeDtypeStruct((B, T, D_KV), F32), jax.ShapeDtypeStruct((B, T, D_KV), F32),
            jax.ShapeDtypeStruct((B, T, D_KV), F32), jax.ShapeDtypeStruct((B, T, LANE), F32))
    return pl.pallas_call(
        _mixer_in_kernel,
        out_shape=outs,
        grid=(B, T // tm),
        in_specs=[row(D), pl.BlockSpec((1, rb, D), mod_map), pl.BlockSpec((1, rb, D), mod_map),
                  pl.BlockSpec((D, D_IN_PAD), lambda b, i: (0, 0))],
        out_specs=(row(D_SSM), row(D_ATT), row(D_KV), row(D_KV), row(D_KV), row(LANE)),
        compiler_params=_cparams("parallel", "parallel"),
        name="mixer_in",
    )(x, shift, scale, w_pad)


def _ssm_tables(lam_re, lam_im, log_dt, b_re, b_im, c_re, c_im, L, n_levels):
    G, P = lam_re.shape
    C = b_re.shape[-1]
    dt = jnp.exp(log_dt.astype(F32))[:, None]
    er, ei = lam_re * dt, lam_im * dt

    def power(k):
        kk = k.astype(F32)[:, None, None]
        mag = jnp.exp(kk * er)
        return mag * jnp.cos(kk * ei), mag * jnp.sin(kk * ei)

    lb_re, lb_im = power(jnp.ones((1,), F32))
    nr, ni = lb_re[0] - 1.0, lb_im[0]
    den = lam_re * lam_re + lam_im * lam_im
    fr = (nr * lam_re + ni * lam_im) / den
    fi = (ni * lam_re - nr * lam_im) / den
    bbr = fr[:, :, None] * b_re - fi[:, :, None] * b_im
    bbi = fr[:, :, None] * b_im + fi[:, :, None] * b_re
    pr, pi = power(jnp.arange(L + 1))
    clr = c_re[None] * pr[:, :, None, :] - c_im[None] * pi[:, :, None, :]
    cli = c_re[None] * pi[:, :, None, :] + c_im[None] * pr[:, :, None, :]
    kern = (jnp.einsum('kgcp,gpd->kgcd', clr[:L], bbr, precision=HIGHEST)
            - jnp.einsum('kgcp,gpd->kgcd', cli[:L], bbi, precision=HIGHEST))
    kz = jnp.concatenate([kern, jnp.zeros((1,) + kern.shape[1:], F32)], 0)
    ts = np.arange(L)
    lag = ts[None, :] - ts[:, None]
    lag = np.where(lag >= 0, lag, L)
    toep = kz[lag]
    toep = jnp.transpose(toep, (2, 0, 4, 1, 3)).reshape(G, L * C, L * C)
    rev = L - 1 - ts
    wsr = pr[rev][:, :, :, None] * bbr[None] - pi[rev][:, :, :, None] * bbi[None]
    wsi = pr[rev][:, :, :, None] * bbi[None] + pi[rev][:, :, :, None] * bbr[None]
    ws = jnp.concatenate([jnp.transpose(wsr, (1, 0, 3, 2)), jnp.transpose(wsi, (1, 0, 3, 2))], -1)
    ws = ws.reshape(G, L * C, 2 * P)
    wy = jnp.concatenate([jnp.transpose(clr[1:], (1, 3, 0, 2)), -jnp.transpose(cli[1:], (1, 3, 0, 2))], 1)
    wy = wy.reshape(G, 2 * P, L * C)
    lr, li = power(L * (2 ** jnp.arange(n_levels)))
    ar = jnp.transpose(jnp.concatenate([lr, lr], -1), (1, 0, 2))
    ai = jnp.transpose(jnp.concatenate([-li, li], -1), (1, 0, 2))
    return toep.astype(BF16), ws.astype(BF16), wy.astype(BF16), ar, ai, (lb_re[0], lb_im[0], bbr, bbi)


def _ssm_kernel(u_ref, toep_ref, ws_ref, wy_ref, ar_ref, ai_ref, y_ref, hl_ref, *, nb, nc, n_levels):
    u = u_ref[0]
    y1 = jnp.dot(u, toep_ref[0], preferred_element_type=F32)
    s = jnp.dot(u, ws_ref[0], preferred_element_type=F32)
    p2 = s.shape[-1]
    rows = lax.broadcasted_iota(jnp.int32, (nc, p2), 0)
    prev = []
    for b in range(nb):
        h = s[b * nc:(b + 1) * nc]
        for k in range(n_levels):
            d = 1 << k
            sh = jnp.where(rows >= d, pltpu.roll(h, d, axis=0), 0.0)
            sw = pltpu.roll(sh, p2 // 2, axis=1)
            h = h + ar_ref[0, k:k + 1, :] * sh + ai_ref[0, k:k + 1, :] * sw
        hl_ref[0, b:b + 1, :] = h[nc - 1:nc, :]
        prev.append(jnp.where(rows >= 1, pltpu.roll(h, 1, axis=0), 0.0))
    hp = jnp.concatenate(prev, axis=0)
    y2 = jnp.dot(hp.astype(BF16), wy_ref[0], preferred_element_type=F32)
    y_ref[0] = y1 + y2


def _ssm_prompt(u, tables):
    toep, ws, wy, ar, ai, _ = tables
    B, T, _ = u.shape
    G, C, L = N_SSM_GROUPS, SSM_GROUP, SSM_CHUNK
    nc = T // L
    n_levels = ar.shape[1]
    ug = jnp.transpose(u.reshape(B, nc, L, G, C), (3, 0, 1, 2, 4)).reshape(G, B * nc, L * C).astype(BF16)
    grp = lambda r, c: pl.BlockSpec((1, r, c), lambda g: (g, 0, 0))
    y, hl = pl.pallas_call(
        functools.partial(_ssm_kernel, nb=B, nc=nc, n_levels=n_levels),
        out_shape=(jax.ShapeDtypeStruct((G, B * nc, L * C), F32),
                   jax.ShapeDtypeStruct((G, B, 2 * SSM_STATE), F32)),
        grid=(G,),
        in_specs=[grp(B * nc, L * C), grp(L * C, L * C), grp(L * C, 2 * SSM_STATE),
                  grp(2 * SSM_STATE, L * C), grp(n_levels, 2 * SSM_STATE), grp(n_levels, 2 * SSM_STATE)],
        out_specs=(grp(B * nc, L * C), grp(B, 2 * SSM_STATE)),
        compiler_params=_cparams("parallel"),
        name="ssm_prompt",
    )(ug, toep, ws, wy, ar, ai)
    y = jnp.transpose(y.reshape(G, B, nc, L, C), (1, 2, 3, 0, 4)).reshape(B, T, D_SSM)
    return y, jnp.transpose(hl, (1, 0, 2))


def _ssm_step_kernel(u_ref, h0_ref, bb_ref, lr_ref, li_ref, cy_ref, y_ref, h_ref):
    p = lr_ref.shape[-1] // 2
    bu = jnp.einsum('gbc,gcp->gbp', u_ref[...], bb_ref[...], preferred_element_type=F32)
    h0 = h0_ref[...]
    h0s = jnp.concatenate([h0[..., p:], h0[..., :p]], axis=-1)
    h = lr_ref[...] * h0 + li_ref[...] * h0s + bu
    h_ref[...] = h
    y_ref[...] = jnp.einsum('gbp,gpc->gbc', h.astype(BF16), cy_ref[...], preferred_element_type=F32)


def _ssm_sample(u, h0_re, h0_im, tables, c_re, c_im):
    lb_re, lb_im, bbr, bbi = tables[-1]
    B = u.shape[0]
    G, C, P = N_SSM_GROUPS, SSM_GROUP, SSM_STATE
    ug = jnp.transpose(u.reshape(B, G, C), (1, 0, 2)).astype(BF16)
    h0 = jnp.transpose(jnp.concatenate([h0_re, h0_im], -1), (1, 0, 2)).astype(F32)
    bb = jnp.concatenate([jnp.transpose(bbr, (0, 2, 1)), jnp.transpose(bbi, (0, 2, 1))], -1).astype(BF16)
    lr = jnp.concatenate([lb_re, lb_re], -1)[:, None, :]
    li = jnp.concatenate([-lb_im, lb_im], -1)[:, None, :]
    cy = jnp.concatenate([jnp.transpose(c_re, (0, 2, 1)), -jnp.transpose(c_im, (0, 2, 1))], 1).astype(BF16)
    y, h = pl.pallas_call(
        _ssm_step_kernel,
        out_shape=(jax.ShapeDtypeStruct((G, B, C), F32), jax.ShapeDtypeStruct((G, B, 2 * P), F32)),
        name="ssm_step",
    )(ug, h0, bb, lr, li, cy)
    return jnp.transpose(y, (1, 0, 2)).reshape(B, D_SSM), jnp.transpose(h, (1, 0, 2))


def _compress_tables(phi_pe, phi_w1, phi_b1, phi_w2, phi_b2):
    S, H, Dh = CMP_STRIDE, N_KV_HEADS, HEAD_DIM
    w1 = phi_w1.reshape(2, 2, S, Dh, Dh)
    eye_c = jnp.eye(2, dtype=F32)
    eye_h = jnp.eye(H, dtype=F32)
    wbig = jnp.einsum('cajde,xc,yh->jxydache', w1, eye_c, eye_h).reshape(S * 2 * H * Dh, 2 * 2 * H * Dh)
    pe = jnp.transpose(phi_pe.reshape(2, 2, S, Dh), (1, 2, 0, 3))
    pe_rows = jnp.broadcast_to(pe[:, :, :, None, :], (2, S, 2, H, Dh)).reshape(2, 1, S * 2 * H * Dh)
    b1 = jnp.broadcast_to(phi_b1[:, None, :], (2, H, Dh)).reshape(1, 2 * H * Dh)
    w2 = jnp.einsum('cef,cx,hy->chexyf', phi_w2, eye_c, eye_h).reshape(2 * H * Dh, 2 * H * Dh)
    b2 = jnp.broadcast_to(phi_b2[:, None, :], (2, H, Dh)).reshape(1, 2 * H * Dh)
    return wbig.astype(BF16), pe_rows, b1, w2.astype(BF16), b2


def _compress_in_kernel(x_ref, pe_ref, w_ref, z_ref):
    x = x_ref[0]
    n = w_ref.shape[1] // 2
    z_ref[0, :, :n] = jnp.dot((x + pe_ref[0]).astype(BF16), w_ref[:, :n], preferred_element_type=F32)
    z_ref[0, :, n:] = jnp.dot((x + pe_ref[1]).astype(BF16), w_ref[:, n:], preferred_element_type=F32)


def _compress_out_kernel(z_ref, b1_ref, w2_ref, b2_ref, o_ref):
    z = z_ref[0]
    n = z.shape[-1] // 2
    rows = z.shape[0]
    second = pltpu.roll(z[:, n:], rows - 1, axis=0)
    hdn = jax.nn.gelu(z[:, :n] + second + b1_ref[...])
    o_ref[0] = jnp.dot(hdn.astype(BF16), w2_ref[...], preferred_element_type=F32) + b2_ref[...]


def _compress(x2, tables, tr):
    wbig, pe_rows, b1, w2, b2 = tables
    B, n, K = x2.shape
    N2 = wbig.shape[1]
    z = pl.pallas_call(
        _compress_in_kernel,
        out_shape=jax.ShapeDtypeStruct((B, n, N2), F32),
        grid=(B, n // tr),
        in_specs=[pl.BlockSpec((1, tr, K), lambda b, i: (b, i, 0)),
                  pl.BlockSpec((2, 1, K), lambda b, i: (0, 0, 0)),
                  pl.BlockSpec((K, N2), lambda b, i: (0, 0))],
        out_specs=pl.BlockSpec((1, tr, N2), lambda b, i: (b, i, 0)),
        compiler_params=_cparams("parallel", "parallel"),
        name="compress_in",
    )(x2, pe_rows, wbig)
    return pl.pallas_call(
        _compress_out_kernel,
        out_shape=jax.ShapeDtypeStruct((B, n, N2 // 2), F32),
        grid=(B,),
        in_specs=[pl.BlockSpec((1, n, N2), lambda b: (b, 0, 0)),
                  pl.BlockSpec((1, N2 // 2), lambda b: (0, 0)),
                  pl.BlockSpec((N2 // 2, N2 // 2), lambda b: (0, 0)),
                  pl.BlockSpec((1, N2 // 2), lambda b: (0, 0))],
        out_specs=pl.BlockSpec((1, n, N2 // 2), lambda b: (b, 0, 0)),
        compiler_params=_cparams("parallel"),
        name="compress_out",
    )(z, b1, w2, b2)


def _rel_bucket(dist):
    n = jnp.maximum(dist, 0)
    max_exact = NUM_BUCKETS // 2
    nf = jnp.maximum(n, 1).astype(F32)
    large = max_exact + (jnp.log(nf / max_exact) / math.log(REL_MAX_DIST / max_exact)
                         * (NUM_BUCKETS - max_exact)).astype(jnp.int32)
    large = jnp.minimum(large, NUM_BUCKETS - 1)
    return jnp.where(n < max_exact, n, large)


def _bias_by_distance(rel_bias, n_max):
    return jnp.transpose(rel_bias[_rel_bucket(jnp.arange(n_max))].astype(F32))


def _bias_windows(bias_n, first_dist, width):
    first_dist = np.asarray(first_dist)
    pad = int(max(0, -first_dist.min()))
    need = int(first_dist.max()) + width
    assert need <= bias_n.shape[1]
    ext = jnp.concatenate([jnp.broadcast_to(bias_n[:, :1], (N_HEADS, pad)), bias_n[:, :need]], axis=1)
    stride = ext.shape[1]
    starts = (np.arange(N_HEADS)[:, None] * stride + (first_dist + pad)[None, :]).reshape(-1, 1)
    dn = lax.GatherDimensionNumbers(offset_dims=(1,), collapsed_slice_dims=(), start_index_map=(0,))
    rows = lax.gather(ext.reshape(-1), jnp.asarray(starts, jnp.int32), dn, slice_sizes=(width,))
    return rows.reshape(N_HEADS, len(first_dist), width)


def _pool_matrix(n_cmp_pad, n_blk_pad):
    r = SEL_BLOCK // CMP_STRIDE
    i = np.arange(n_cmp_pad)[None, :]
    j = np.arange(n_blk_pad)[:, None]
    return ((i >= r * j - 1) & (i <= r * j + r - 1)).astype(np.float32)


def _cmp_select_kernel(q_ref, k_ref, vt_ref, bias_ref, pool_ref, o_ref, sel_ref, *, tq, n_cmp):
    qt = pl.program_id(2)
    n_qt = pl.num_programs(2)
    q = q_ref[0, 0].reshape(GQA * tq, HEAD_DIM)
    k = k_ref[0, 0]
    nc = k.shape[0]
    s = _nt_dot(k, q)
    row0 = pl.multiple_of((n_qt - 1 - qt) * 8, 8)
    s = s + jnp.concatenate([bias_ref[g, pl.ds(row0, nc), :] for g in range(GQA)], axis=-1)
    t_pos = qt * tq + (lax.broadcasted_iota(jnp.int32, (nc, GQA * tq), 1) % tq)
    ci = lax.broadcasted_iota(jnp.int32, (nc, GQA * tq), 0)
    mask = (ci * CMP_STRIDE + CMP_BLOCK - 1 <= t_pos) & (ci < n_cmp)
    s = jnp.where(mask, s, NEG)
    m = jnp.max(s, axis=0, keepdims=True)
    p = jnp.where(mask, jnp.exp(s - m), 0.0)
    p = p / jnp.maximum(jnp.sum(p, axis=0, keepdims=True), 1e-30)
    ot = jnp.dot(vt_ref[0, 0], p.astype(BF16), preferred_element_type=F32)
    o_ref[0] = jnp.concatenate([ot[:, g * tq:(g + 1) * tq].T for g in range(GQA)], axis=-1)
    imp = p[:, 0:tq]
    for g in range(1, GQA):
        imp = imp + p[:, g * tq:(g + 1) * tq]
    sb = jnp.dot(pool_ref[...], imp, precision=HIGHEST, preferred_element_type=F32)
    nb = sb.shape[0]
    blk = lax.broadcasted_iota(jnp.int32, (nb, tq), 0)
    cur = (qt * tq + lax.broadcasted_iota(jnp.int32, (nb, tq), 1)) // SEL_BLOCK
    causal = blk <= cur
    forced = (blk == 0) | (blk == cur) | (blk == cur - 1)
    sc = jnp.where(forced & causal, 1e4, jnp.where(causal, sb, -1.0))
    rank = jnp.zeros((nb, tq), jnp.int32)
    for i in range(nb):
        row = sc[i:i + 1, :]
        ahead = (row > sc) | ((row == sc) & (blk > i))
        rank = rank + ahead.astype(jnp.int32)
    sel_ref[0, 0] = jnp.where((rank < N_SEL) & causal, 0.0, NEG)


def _cmp_bias_table(bias_n, n_qt, n_rb, tq):
    assert tq == 8 * CMP_STRIDE
    j = np.arange(n_rb + n_qt - 1)[:, None]
    il = np.arange(8)[None, :]
    first = tq * (n_qt - 1 - j) - CMP_STRIDE * il - (CMP_BLOCK - 1)
    return _bias_windows(bias_n, first.reshape(-1), tq)


def _cmp_select_prompt(q5, kc, vct, bias_tab, pool, n_cmp):
    B, _, _, T, _ = q5.shape
    NC = kc.shape[2]
    NB = pool.shape[0]
    R = bias_tab.shape[1]
    tq = ATT_TQ
    return pl.pallas_call(
        functools.partial(_cmp_select_kernel, tq=tq, n_cmp=n_cmp),
        out_shape=(jax.ShapeDtypeStruct((B, T, D_ATT), F32),
                   jax.ShapeDtypeStruct((B, N_KV_HEADS, NB, T), F32)),
        grid=(B, N_KV_HEADS, T // tq),
        in_specs=[pl.BlockSpec((1, 1, GQA, tq, HEAD_DIM), lambda b, h, i: (b, h, 0, i, 0)),
                  pl.BlockSpec((1, 1, NC, HEAD_DIM), lambda b, h, i: (b, h, 0, 0)),
                  pl.BlockSpec((1, 1, HEAD_DIM, NC), lambda b, h, i: (b, h, 0, 0)),
                  pl.BlockSpec((GQA, R, tq), lambda b, h, i: (h, 0, 0)),
                  pl.BlockSpec((NB, NC), lambda b, h, i: (0, 0))],
        out_specs=(pl.BlockSpec((1, tq, GQA * HEAD_DIM), lambda b, h, i: (b, i, h)),
                   pl.BlockSpec((1, 1, NB, tq), lambda b, h, i: (b, h, 0, i))),
        compiler_params=_cparams("parallel", "parallel", "parallel"),
        name="cmp_select_prompt",
    )(q5, kc, vct, bias_tab, pool)


def _sel_win_kernel(q_ref, ks_ref, vst_ref, kw_ref, vwt_ref, sel_ref, tzs_ref, tzw_ref, os_ref, ow_ref, *, tq):
    tk = ATT_TK
    qt = pl.program_id(2)
    q = [q_ref[0, 0, g] for g in range(GQA)]
    n_ds = tzs_ref.shape[2]
    n_dw = tzw_ref.shape[2]
    per_tile = tk // SEL_BLOCK

    def make_step(k_ref, vt_ref, tz_ref, n_d, use_sel):
        def step(kt, carry):
            off = pl.multiple_of(kt * tk, tk)
            k = k_ref[0, 0, pl.ds(off, tk), :]
            vt = vt_ref[0, 0, :, pl.ds(off, tk)]
            d = jnp.minimum(qt - kt, n_d - 1)
            if use_sel:
                rows = sel_ref[0, 0, pl.ds(kt * per_tile, per_tile), :]
                selb = jnp.concatenate([jnp.broadcast_to(rows[i:i + 1], (SEL_BLOCK, tq))
                                        for i in range(per_tile)], axis=0)
            out = []
            for g in range(GQA):
                m, l, acc = carry[g]
                s = _nt_dot(k, q[g]) + tz_ref[0, g, d]
                if use_sel:
                    s = s + selb
                m_new = jnp.maximum(m, jnp.max(s, axis=0, keepdims=True))
                alpha = jnp.exp(m - m_new)
                p = jnp.exp(s - m_new)
                l = alpha * l + jnp.sum(p, axis=0, keepdims=True)
                acc = alpha * acc + jnp.dot(vt, p.astype(BF16), preferred_element_type=F32)
                out.append((m_new, l, acc))
            return tuple(out)
        return step

    def init():
        return tuple((jnp.full((1, tq), NEG, F32), jnp.zeros((1, tq), F32), jnp.zeros((HEAD_DIM, tq), F32))
                     for _ in range(GQA))

    def finish(carry):
        return jnp.concatenate([(acc / jnp.maximum(l, 1e-30)).T for _, l, acc in carry], axis=-1)

    os_ref[0] = finish(lax.fori_loop(0, qt + 1, make_step(ks_ref, vst_ref, tzs_ref, n_ds, True), init()))
    lo = jnp.maximum(qt - (n_dw - 1), 0)
    ow_ref[0] = finish(lax.fori_loop(lo, qt + 1, make_step(kw_ref, vwt_ref, tzw_ref, n_dw, False), init()))


def _diag_bias_tiles(bias_n, n_diag, tq, tk, max_dist):
    d = np.arange(n_diag)[:, None]
    c = np.arange(tk)[None, :]
    tiles = _bias_windows(bias_n, (d * tk - c).reshape(-1), tq).reshape(N_HEADS, n_diag, tk, tq)
    dist = (d * tk - c)[:, :, None] + np.arange(tq)[None, None, :]
    ok = (dist >= 0) & (dist <= max_dist)
    return jnp.where(ok[None], tiles, NEG).reshape(N_KV_HEADS, GQA, n_diag, tk, tq)


def _sel_win_prompt(q5, ks, vst, kw, vwt, sel, tzs, tzw):
    B, _, _, T, _ = q5.shape
    NB = sel.shape[2]
    tq = ATT_TQ
    k_spec = pl.BlockSpec((1, 1, T, HEAD_DIM), lambda b, h, i: (b, h, 0, 0))
    vt_spec = pl.BlockSpec((1, 1, HEAD_DIM, T), lambda b, h, i: (b, h, 0, 0))
    tz_spec = lambda tz: pl.BlockSpec((1,) + tz.shape[1:], lambda b, h, i: (h, 0, 0, 0, 0))
    o_spec = pl.BlockSpec((1, tq, GQA * HEAD_DIM), lambda b, h, i: (b, i, h))
    return pl.pallas_call(
        functools.partial(_sel_win_kernel, tq=tq),
        out_shape=(jax.ShapeDtypeStruct((B, T, D_ATT), F32), jax.ShapeDtypeStruct((B, T, D_ATT), F32)),
        grid=(B, N_KV_HEADS, T // tq),
        in_specs=[pl.BlockSpec((1, 1, GQA, tq, HEAD_DIM), lambda b, h, i: (b, h, 0, i, 0)),
                  k_spec, vt_spec, k_spec, vt_spec,
                  pl.BlockSpec((1, 1, NB, tq), lambda b, h, i: (b, h, 0, i)),
                  tz_spec(tzs), tz_spec(tzw)],
        out_specs=(o_spec, o_spec),
        compiler_params=_cparams("parallel", "parallel", "parallel"),
        name="sel_win_prompt",
    )(q5, ks, vst, kw, vwt, sel, tzs, tzw)


def _gate_expand_matrix():
    m = np.zeros((3, LANE, D_ATT), np.float32)
    for r in range(3):
        for h in range(N_HEADS):
            m[r, h * 3 + r, h * HEAD_DIM:(h + 1) * HEAD_DIM] = 1.0
    return m


def _post_mixer_kernel(y_ref, u_ref, oc_ref, os_ref, ow_ref, g_ref, x_ref, gate_ref, sh_ref, sc_ref,
                       dskip_ref, wglu_ref, bglu_ref, gexp_ref, wout_ref, lng_ref, lnb_ref,
                       wr_ref, br_ref, x1_ref, hm_ref, te_ref, tw_ref):
    y = y_ref[0] + dskip_ref[...] * u_ref[0]
    gl = jax.nn.gelu(y)
    ssm = gl * jax.nn.sigmoid(jnp.dot(gl.astype(BF16), wglu_ref[...], preferred_element_type=F32)
                              + bglu_ref[...])
    sg = jax.nn.sigmoid(g_ref[0])
    att = jnp.zeros_like(oc_ref[0])
    for r, o_ref in enumerate((oc_ref, os_ref, ow_ref)):
        att = att + jnp.dot(sg, gexp_ref[r], precision=HIGHEST, preferred_element_type=F32) * o_ref[0]
    h = (jnp.dot(ssm.astype(BF16), wout_ref[:D_SSM, :], preferred_element_type=F32)
         + jnp.dot(att.astype(BF16), wout_ref[D_SSM:, :], preferred_element_type=F32))
    z = DN_ALPHA * x_ref[0] + gate_ref[0] * h
    x1 = _layer_norm(z) * lng_ref[...] + lnb_ref[...]
    x1_ref[0] = x1
    hm = _layer_norm(x1) * (1.0 + sc_ref[0]) + sh_ref[0]
    hm_ref[0] = hm
    logits = jnp.dot(hm, wr_ref[...], precision=HIGHEST, preferred_element_type=F32) + br_ref[...]
    lane = lax.broadcasted_iota(jnp.int32, logits.shape, 1)
    work = jnp.where(lane < N_EXPERTS, logits, -jnp.inf)
    te = jnp.zeros(logits.shape, jnp.int32)
    tv = jnp.zeros(logits.shape, F32)
    for k in range(TOP_K):
        best = jnp.max(work, axis=-1, keepdims=True)
        arg = jnp.min(jnp.where(work == best, lane, LANE), axis=-1, keepdims=True)
        te = jnp.where(lane == k, arg, te)
        tv = jnp.where(lane == k, best, tv)
        work = jnp.where(lane == arg, -jnp.inf, work)
    ex = jnp.where(lane < TOP_K, jnp.exp(tv - tv[:, 0:1]), 0.0)
    te_ref[0] = te
    tw_ref[0] = ex / jnp.sum(ex, axis=-1, keepdims=True)


def _post_mixer(y, u, oc, osel, ow, g, x, gate, shift, scale, w, tm):
    B, T, D = x.shape
    R = gate.shape[1]
    rb = 1 if R == 1 else tm
    mod_map = (lambda b, i: (b, 0, 0)) if R == 1 else (lambda b, i: (b, i, 0))
    row = lambda n: pl.BlockSpec((1, tm, n), lambda b, i: (b, i, 0))
    mod = pl.BlockSpec((1, rb, D), mod_map)
    full = lambda a: pl.BlockSpec(a.shape, lambda b, i: (0,) * a.ndim)
    consts = (w['d_skip'], w['w_glu'], w['b_glu'], w['gexp'], w['w_out'], w['ln1_g'], w['ln1_b'],
              w['w_router'], w['b_router'])
    return pl.pallas_call(
        _post_mixer_kernel,
        out_shape=(jax.ShapeDtypeStruct((B, T, D), F32), jax.ShapeDtypeStruct((B, T, D), F32),
                   jax.ShapeDtypeStruct((B, T, LANE), jnp.int32), jax.ShapeDtypeStruct((B, T, LANE), F32)),
        grid=(B, T // tm),
        in_specs=[row(D_SSM), row(D_SSM), row(D_ATT), row(D_ATT), row(D_ATT), row(LANE), row(D),
                  mod, mod, mod] + [full(a) for a in consts],
        out_specs=(row(D), row(D), row(LANE), row(LANE)),
        compiler_params=_cparams("parallel", "parallel"),
        name="post_mixer",
    )(y, u, oc, osel, ow, g, x, gate, shift, scale, *consts)


def _expert_kernel(be_ref, nu_ref, x_ref, wgu_ref, bgu_ref, wd_ref, bd_ref, o_ref, wgu_s, wd_s):
    i = pl.program_id(0)
    prev = be_ref[jnp.maximum(i - 1, 0)]
    fresh = (i == 0) | (be_ref[i] != prev)

    @pl.when(fresh)
    def _():
        wgu_s[...] = wgu_ref[0].astype(BF16)
        wd_s[...] = wd_ref[0].astype(BF16)

    @pl.when(i < nu_ref[0])
    def _():
        gu = jnp.dot(x_ref[...].astype(BF16), wgu_s[...], preferred_element_type=F32) + bgu_ref[0]
        gate = jnp.minimum(gu[:, :D_FF], SWIGLU_LIMIT)
        up = jnp.clip(gu[:, D_FF:], -SWIGLU_LIMIT, SWIGLU_LIMIT)
        hh = (up + 1.0) * gate * jax.nn.sigmoid(SWIGLU_ALPHA * gate)
        o_ref[...] = jnp.dot(hh.astype(BF16), wd_s[...], preferred_element_type=F32) + bd_ref[0]

    @pl.when(i >= nu_ref[0])
    def _():
        o_ref[...] = jnp.zeros_like(o_ref)


def _experts(xb, blk_e, n_used, w_gate_up, b_gate_up, w_down, b_down):
    rows, D = xb.shape
    n_blk = rows // MOE_ROWS
    grid_spec = pltpu.PrefetchScalarGridSpec(
        num_scalar_prefetch=2,
        grid=(n_blk,),
        in_specs=[pl.BlockSpec((MOE_ROWS, D), lambda i, be, nu: (i, 0)),
                  pl.BlockSpec((1, D, 2 * D_FF), lambda i, be, nu: (be[i], 0, 0)),
                  pl.BlockSpec((1, 1, 2 * D_FF), lambda i, be, nu: (be[i], 0, 0)),
                  pl.BlockSpec((1, D_FF, D), lambda i, be, nu: (be[i], 0, 0)),
                  pl.BlockSpec((1, 1, D), lambda i, be, nu: (be[i], 0, 0))],
        out_specs=pl.BlockSpec((MOE_ROWS, D), lambda i, be, nu: (i, 0)),
        scratch_shapes=[pltpu.VMEM((D, 2 * D_FF), BF16), pltpu.VMEM((D_FF, D), BF16)],
    )
    return pl.pallas_call(
        _expert_kernel,
        out_shape=jax.ShapeDtypeStruct((rows, D), F32),
        grid_spec=grid_spec,
        compiler_params=_cparams("arbitrary"),
        name="moe_experts",
    )(blk_e, n_used, xb, w_gate_up, b_gate_up.reshape(N_EXPERTS, 1, 2 * D_FF), w_down,
      b_down.reshape(N_EXPERTS, 1, D))


def _moe_dispatch(top_e, n):
    blk = MOE_ROWS
    nk = n * TOP_K
    e = top_e.reshape(-1)
    order = jnp.argsort(e)
    e_s = e[order]
    counts = jnp.bincount(e, length=N_EXPERTS)
    pcounts = (counts + blk - 1) // blk * blk
    start = jnp.cumsum(counts) - counts
    pend = jnp.cumsum(pcounts)
    pstart = pend - pcounts
    dest_sorted = (pstart[e_s] + jnp.arange(nk) - start[e_s]).astype(jnp.int32)
    n_blk = (nk + N_EXPERTS * (blk - 1)) // blk
    rows = n_blk * blk
    row_tok = jnp.full((rows,), n, jnp.int32).at[dest_sorted].set((order // TOP_K).astype(jnp.int32))
    dest = jnp.zeros((nk,), jnp.int32).at[order].set(dest_sorted)
    blk_e = jnp.sum(pend[None, :] <= (jnp.arange(n_blk) * blk)[:, None], axis=1)
    blk_e = jnp.minimum(blk_e, N_EXPERTS - 1).astype(jnp.int32)
    n_used = (pend[-1] // blk).astype(jnp.int32).reshape(1)
    return row_tok, dest.reshape(n, TOP_K), blk_e, n_used


def _final_kernel(x_ref, y0_ref, y1_ref, y2_ref, y3_ref, tw_ref, gate_ref, lng_ref, lnb_ref, o_ref):
    tw = tw_ref[0]
    y = jnp.zeros_like(x_ref[0])
    for k, y_ref in enumerate((y0_ref, y1_ref, y2_ref, y3_ref)):
        y = y + tw[:, k:k + 1] * y_ref[0]
    z = DN_ALPHA * x_ref[0] + gate_ref[0] * y
    o_ref[0] = _layer_norm(z) * lng_ref[...] + lnb_ref[...]


def _final(x1, ys, tw, gate, ln_g, ln_b, tm):
    B, T, D = x1.shape
    R = gate.shape[1]
    rb = 1 if R == 1 else tm
    mod_map = (lambda b, i: (b, 0, 0)) if R == 1 else (lambda b, i: (b, i, 0))
    row = lambda n: pl.BlockSpec((1, tm, n), lambda b, i: (b, i, 0))
    vec = pl.BlockSpec((1, D), lambda b, i: (0, 0))
    return pl.pallas_call(
        _final_kernel,
        out_shape=jax.ShapeDtypeStruct((B, T, D), F32),
        grid=(B, T // tm),
        in_specs=[row(D), row(D), row(D), row(D), row(D), row(LANE),
                  pl.BlockSpec((1, rb, D), mod_map), vec, vec],
        out_specs=row(D),
        compiler_params=_cparams("parallel", "parallel"),
        name="moe_combine_ln",
    )(x1, *ys, tw, gate, ln_g, ln_b)


def _cmp_select_step_kernel(q_ref, k_ref, v_ref, bias_ref, pool_ref, o_ref, idx_ref, *, n_cmp, n_blk, q_pos):
    q = q_ref[0].astype(BF16)
    ncp = k_ref.shape[2]
    nbp = pool_ref.shape[1]
    row = lax.broadcasted_iota(jnp.int32, (N_HEADS, 1), 0)
    first = row < GQA
    s = jnp.where(first, _nt_dot(q, k_ref[0, 0]), _nt_dot(q, k_ref[0, 1])) * (HEAD_DIM ** -0.5)
    s = s + bias_ref[...]
    ci = lax.broadcasted_iota(jnp.int32, (N_HEADS, ncp), 1)
    mask = (ci * CMP_STRIDE + CMP_BLOCK - 1 <= q_pos) & (ci < n_cmp)
    s = jnp.where(mask, s, NEG)
    m = jnp.max(s, axis=-1, keepdims=True)
    p = jnp.where(mask, jnp.exp(s - m), 0.0)
    p = p / jnp.maximum(jnp.sum(p, axis=-1, keepdims=True), 1e-30)
    pb = p.astype(BF16)
    o_ref[0] = jnp.where(first, jnp.dot(pb, v_ref[0, 0], preferred_element_type=F32),
                         jnp.dot(pb, v_ref[0, 1], preferred_element_type=F32))
    imp0 = jnp.sum(jnp.where(first, p, 0.0), axis=0, keepdims=True)
    imp1 = jnp.sum(jnp.where(first, 0.0, p), axis=0, keepdims=True)
    imp = jnp.where(first, imp0, imp1)
    sb = jnp.dot(imp, pool_ref[...], precision=HIGHEST, preferred_element_type=F32)
    cur = q_pos // SEL_BLOCK
    bi = lax.broadcasted_iota(jnp.int32, (nbp, nbp), 0)
    bj = lax.broadcasted_iota(jnp.int32, (nbp, nbp), 1)
    blk = lax.broadcasted_iota(jnp.int32, (1, nbp), 1)
    causal = blk <= cur
    forced = (blk == 0) | (blk == cur) | (blk == cur - 1)
    rsel = lax.broadcasted_iota(jnp.int32, (N_SEL, nbp), 0)
    for h in range(N_KV_HEADS):
        sc = jnp.where(forced & causal, 1e4, jnp.where(causal, sb[h * GQA:h * GQA + 1, :], -1.0))
        sc = jnp.where(blk < n_blk, sc, -2.0)
        scb = jnp.broadcast_to(sc, (nbp, nbp))
        col = jnp.sum(jnp.where(bi == bj, scb, 0.0), axis=1, keepdims=True)
        ahead = (col > scb) | ((col == scb) & (bi < bj))
        rank = jnp.sum(ahead.astype(jnp.int32), axis=0, keepdims=True)
        hit = jnp.broadcast_to(rank, (N_SEL, nbp)) == rsel
        idx = jnp.sum(jnp.where(hit, jnp.broadcast_to(blk, (N_SEL, nbp)), 0), axis=1, keepdims=True)
        idx_ref[0, h] = jnp.broadcast_to(idx, (N_SEL, LANE))


def _cmp_select_step(q, kc, vc, bias, pool, n_cmp, n_blk, q_pos):
    B = q.shape[0]
    NCp = kc.shape[2]
    return pl.pallas_call(
        functools.partial(_cmp_select_step_kernel, n_cmp=n_cmp, n_blk=n_blk, q_pos=q_pos),
        out_shape=(jax.ShapeDtypeStruct((B, N_HEADS, HEAD_DIM), F32),
                   jax.ShapeDtypeStruct((B, N_KV_HEADS, N_SEL, LANE), jnp.int32)),
        grid=(B,),
        in_specs=[pl.BlockSpec((1, N_HEADS, HEAD_DIM), lambda b: (b, 0, 0)),
                  pl.BlockSpec((1, N_KV_HEADS, NCp, HEAD_DIM), lambda b: (b, 0, 0, 0)),
                  pl.BlockSpec((1, N_KV_HEADS, NCp, HEAD_DIM), lambda b: (b, 0, 0, 0)),
                  pl.BlockSpec(bias.shape, lambda b: (0, 0)),
                  pl.BlockSpec(pool.shape, lambda b: (0, 0))],
        out_specs=(pl.BlockSpec((1, N_HEADS, HEAD_DIM), lambda b: (b, 0, 0)),
                   pl.BlockSpec((1, N_KV_HEADS, N_SEL, LANE), lambda b: (b, 0, 0, 0))),
        compiler_params=_cparams("parallel"),
        name="cmp_select_step",
    )(q, kc, vc, bias, pool)


def _sel_step_kernel(idx_ref, q_ref, kv_ref, new_ref, bias_ref, kpos_ref, o_ref, *, n_past, q_pos):
    b, h = pl.program_id(0), pl.program_id(1)
    base = (b * N_KV_HEADS + h) * N_SEL
    new = new_ref[0]
    segs = [jnp.where(idx_ref[base + j] >= n_past, new, kv_ref[0, 0, j * SEL_BLOCK:(j + 1) * SEL_BLOCK, :])
            for j in range(N_SEL)]
    kv = jnp.concatenate(segs, axis=0)
    hd = HEAD_DIM
    k = jnp.where(h == 0, kv[:, 0:hd], kv[:, hd:2 * hd]).astype(BF16)
    v = jnp.where(h == 0, kv[:, 2 * hd:3 * hd], kv[:, 3 * hd:]).astype(BF16)
    s = _nt_dot(q_ref[0].astype(BF16), k) * (hd ** -0.5) + bias_ref[0, 0]
    mask = kpos_ref[0, 0] <= q_pos
    s = jnp.where(mask, s, NEG)
    m = jnp.max(s, axis=-1, keepdims=True)
    p = jnp.where(mask, jnp.exp(s - m), 0.0)
    l = jnp.sum(p, axis=-1, keepdims=True)
    o_ref[0, 0] = jnp.dot(p.astype(BF16), v, preferred_element_type=F32) / jnp.maximum(l, 1e-30)


def _sel_step(q, kv_sel, new_blocks, bias_sel, kpos, idx_flat, n_past, q_pos):
    B = q.shape[0]
    nk = N_SEL * SEL_BLOCK
    grid_spec = pltpu.PrefetchScalarGridSpec(
        num_scalar_prefetch=1,
        grid=(B, N_KV_HEADS),
        in_specs=[pl.BlockSpec((1, N_HEADS, HEAD_DIM), lambda b, h, ix: (b, 0, 0)),
                  pl.BlockSpec((1, 1, nk, D_KV), lambda b, h, ix: (b, h, 0, 0)),
                  pl.BlockSpec((1, SEL_BLOCK, D_KV), lambda b, h, ix: (b, 0, 0)),
                  pl.BlockSpec((1, 1, N_HEADS, nk), lambda b, h, ix: (b, h, 0, 0)),
                  pl.BlockSpec((1, 1, 1, nk), lambda b, h, ix: (b, h, 0, 0))],
        out_specs=pl.BlockSpec((1, 1, N_HEADS, HEAD_DIM), lambda b, h, ix: (b, h, 0, 0)),
    )
    return pl.pallas_call(
        functools.partial(_sel_step_kernel, n_past=n_past, q_pos=q_pos),
        out_shape=jax.ShapeDtypeStruct((B, N_KV_HEADS, N_HEADS, HEAD_DIM), F32),
        grid_spec=grid_spec,
        compiler_params=_cparams("arbitrary", "arbitrary"),
        name="sel_step",
    )(idx_flat, q, kv_sel, new_blocks, bias_sel, kpos)


def _win_step_kernel(q_ref, w_ref, new_ref, bias_ref, bias0_ref, o_ref):
    q = q_ref[0]
    qb = q.astype(BF16)
    row = lax.broadcasted_iota(jnp.int32, (N_HEADS, 1), 0)
    first = row < GQA
    w = w_ref[0]
    hd = HEAD_DIM
    kb = [w[:, h * hd:(h + 1) * hd].astype(BF16) for h in range(N_KV_HEADS)]
    vb = [w[:, (N_KV_HEADS + h) * hd:(N_KV_HEADS + h + 1) * hd].astype(BF16) for h in range(N_KV_HEADS)]
    s = jnp.where(first, _nt_dot(qb, kb[0]), _nt_dot(qb, kb[1])) * (hd ** -0.5) + bias_ref[...]
    new = new_ref[0]
    kn = jnp.where(first, new[:, 0:hd], new[:, hd:2 * hd])
    vn = jnp.where(first, new[:, 2 * hd:3 * hd], new[:, 3 * hd:])
    sn = jnp.sum(q * kn, axis=-1, keepdims=True) * (hd ** -0.5) + bias0_ref[...]
    m = jnp.maximum(jnp.max(s, axis=-1, keepdims=True), sn)
    p = jnp.exp(s - m)
    pn = jnp.exp(sn - m)
    l = jnp.sum(p, axis=-1, keepdims=True) + pn
    pb = p.astype(BF16)
    acc = jnp.where(first, jnp.dot(pb, vb[0], preferred_element_type=F32),
                    jnp.dot(pb, vb[1], preferred_element_type=F32)) + pn * vn
    o_ref[0] = acc / jnp.maximum(l, 1e-30)


def _win_step(q, win, new, bias, bias0):
    B, W, _ = win.shape
    return pl.pallas_call(
        _win_step_kernel,
        out_shape=jax.ShapeDtypeStruct((B, N_HEADS, HEAD_DIM), F32),
        grid=(B,),
        in_specs=[pl.BlockSpec((1, N_HEADS, HEAD_DIM), lambda b: (b, 0, 0)),
                  pl.BlockSpec((1, W, D_KV), lambda b: (b, 0, 0)),
                  pl.BlockSpec((1, 1, D_KV), lambda b: (b, 0, 0)),
                  pl.BlockSpec((N_HEADS, W), lambda b: (0, 0)),
                  pl.BlockSpec((N_HEADS, 1), lambda b: (0, 0))],
        out_specs=pl.BlockSpec((1, N_HEADS, HEAD_DIM), lambda b: (b, 0, 0)),
        compiler_params=_cparams("parallel"),
        name="win_step",
    )(q, win, new, bias, bias0)


def _split_heads(kv, dtype):
    B, L, _ = kv.shape
    kv5 = kv.reshape(B, L, 2, N_KV_HEADS, HEAD_DIM)
    return (jnp.transpose(kv5[:, :, 0], (0, 2, 1, 3)).astype(dtype),
            jnp.transpose(kv5[:, :, 1], (0, 2, 1, 3)).astype(dtype))


def _nsa_prompt(q, kvc, kvs, kvw, cmp_tab, rel_bias):
    B, T, _ = q.shape
    nc = T // CMP_STRIDE
    nb = T // SEL_BLOCK
    ckv = _compress(kvc.reshape(B, nc, CMP_STRIDE * D_KV), cmp_tab, tr=min(nc, 256))
    kc, vc = _split_heads(ckv, BF16)
    vct = jnp.transpose(vc, (0, 1, 3, 2))
    bias_n = _bias_by_distance(rel_bias, T)
    n_qt, n_kt = T // ATT_TQ, T // ATT_TK
    bias_tab = _cmp_bias_table(bias_n, n_qt, nc // 8, ATT_TQ)
    pool = jnp.asarray(_pool_matrix(nc, nb))
    scale = HEAD_DIM ** -0.5
    q5 = jnp.transpose((q * scale).reshape(B, T, N_KV_HEADS, GQA, HEAD_DIM), (0, 2, 3, 1, 4))
    o_cmp, sel = _cmp_select_prompt(q5, kc, vct, bias_tab, pool, nc - 1)
    ks, vs = _split_heads(kvs, BF16)
    kw, vw = _split_heads(kvw, BF16)
    n_ds = min(n_kt, -(-(REL_MAX_DIST + ATT_TK - 1) // ATT_TK) + 1)
    n_dw = min(n_kt, WINDOW // ATT_TK + 1)
    tzs = _diag_bias_tiles(bias_n, n_ds, ATT_TQ, ATT_TK, np.inf)
    tzw = _diag_bias_tiles(bias_n, n_dw, ATT_TQ, ATT_TK, WINDOW)
    o_sel, o_win = _sel_win_prompt(q5, ks, jnp.transpose(vs, (0, 1, 3, 2)), kw, jnp.transpose(vw, (0, 1, 3, 2)),
                                   sel, tzs, tzw)
    return o_cmp, o_sel, o_win


def _nsa_sample(q, kvc, kvs, kvw, pool_cmp, pool_sel, win_buf, page_table, cmp_tab, rel_bias):
    B = q.shape[0]
    n_pages = page_table.shape[1]
    past_len = n_pages * PAGE_SIZE
    q_pos = past_len
    lp = -(-(past_len + 1) // SEL_BLOCK) * SEL_BLOCK
    n_cmp = lp // CMP_STRIDE - 1
    n_blk = lp // SEL_BLOCK
    n_chunks = -(-(n_cmp + 1) // 24) * 24
    past = pool_cmp[page_table].reshape(B, past_len, D_KV)
    full = jnp.concatenate([past, kvc[:, None, :],
                            jnp.zeros((B, n_chunks * CMP_STRIDE - past_len - 1, D_KV), F32)], 1)
    ckv = _compress(full.reshape(B, n_chunks, CMP_STRIDE * D_KV), cmp_tab, tr=n_chunks // 3)
    ncp = -(-n_chunks // LANE) * LANE
    nbp = -(-n_blk // LANE) * LANE
    ckv = jnp.pad(ckv, ((0, 0), (0, ncp - n_chunks), (0, 0)))
    kc, vc = _split_heads(ckv, BF16)
    bias_n = _bias_by_distance(rel_bias, q_pos + 1)
    dist_c = np.maximum(q_pos - (np.arange(ncp) * CMP_STRIDE + CMP_BLOCK - 1), 0)
    pool = jnp.asarray(_pool_matrix(ncp, nbp).T)
    q3 = q.reshape(B, N_HEADS, HEAD_DIM)
    o_cmp, idx = _cmp_select_step(q3, kc, vc, bias_n[:, dist_c], pool, n_cmp, n_blk, q_pos)
    idx = idx[..., 0]
    bpp = PAGE_SIZE // SEL_BLOCK
    n_past = n_pages * bpp
    past_idx = jnp.minimum(idx, n_past - 1)
    page = jnp.take_along_axis(page_table, (past_idx // bpp).reshape(B, -1), axis=1).reshape(idx.shape)
    pool6 = pool_sel.reshape(pool_sel.shape[0], bpp, SEL_BLOCK, 2, N_KV_HEADS, HEAD_DIM)
    kv_sel = pool6[page, past_idx % bpp].reshape(B, N_KV_HEADS, N_SEL * SEL_BLOCK, D_KV)
    new_blocks = jnp.pad(kvs[:, None, :], ((0, 0), (0, SEL_BLOCK - 1), (0, 0)))
    first = q_pos - np.arange(n_blk) * SEL_BLOCK - (SEL_BLOCK - 1)
    bias_blk = jnp.transpose(_bias_windows(bias_n, first, SEL_BLOCK)[:, :, ::-1], (1, 0, 2))
    bias_sel = jnp.transpose(bias_blk[idx], (0, 1, 3, 2, 4)).reshape(B, N_KV_HEADS, N_HEADS, -1)
    kpos = idx[..., None] * SEL_BLOCK + jnp.arange(SEL_BLOCK)
    kpos = jnp.where((idx <= q_pos // SEL_BLOCK)[..., None], kpos, q_pos + 1)
    kpos = kpos.reshape(B, N_KV_HEADS, 1, -1).astype(jnp.int32)
    o_sel = _sel_step(q3, kv_sel, new_blocks, bias_sel, kpos, idx.reshape(-1).astype(jnp.int32),
                      n_past, q_pos)
    o_sel = jnp.concatenate([o_sel[:, h, h * GQA:(h + 1) * GQA] for h in range(N_KV_HEADS)], axis=1)
    wb = win_buf.shape[1]
    bias_w = bias_n[:, wb - np.arange(wb)]
    o_win = _win_step(q3, win_buf.reshape(B, wb, D_KV), kvw[:, None, :], bias_w, bias_n[:, 0:1])
    return o_cmp.reshape(B, D_ATT), o_sel.reshape(B, D_ATT), o_win.reshape(B, D_ATT)


def kernel(x_prompt, x_sample, cache_cmp_kv, cache_sel_kv, state_win_kv, state_ssm_re, state_ssm_im, page_table,
           c_prompt, c_sample, w_ada, b_ada, w_in, lam_re, lam_im, log_dt, b_re, b_im, c_re, c_im, d_skip,
           w_glu, b_glu, phi_pe, phi_w1, phi_b1, phi_w2, phi_b2, rel_bias, w_out, ln1_g, ln1_b,
           w_router, b_router, w_gate_up, b_gate_up, w_down, b_down, ln2_g, ln2_b):
    assert w_ada.shape[0] == DEPTH == 1
    l = 0
    Bp, T, D = x_prompt.shape
    Bs = x_sample.shape[0]
    kv_tail = (2, N_KV_HEADS, HEAD_DIM)

    n_c = Bp + Bs
    c_all = jnp.pad(jnp.concatenate([c_prompt, c_sample], 0), ((0, -n_c % 8), (0, 0)))
    m_all = _adaln(c_all, w_ada[l], b_ada[l])
    m_p = m_all[:Bp].reshape(Bp, 6, D)
    m_s = m_all[Bp:n_c].reshape(Bs, 6, D)
    mod_p = [m_p[:, i:i + 1, :] for i in range(6)]
    mod_s = [m_s[None, :, i, :] for i in range(6)]

    w_in_pad = jnp.pad(w_in[l], ((0, 0), (0, D_IN_PAD - D_IN))).astype(BF16)
    n_levels = max(1, int(math.log2(T // SSM_CHUNK)))
    ssm_tab = _ssm_tables(lam_re[l], lam_im[l], log_dt[l], b_re[l], b_im[l], c_re[l], c_im[l],
                          SSM_CHUNK, n_levels)
    cmp_tab = _compress_tables(phi_pe[l], phi_w1[l], phi_b1[l], phi_w2[l], phi_b2[l])
    w_post = dict(
        d_skip=d_skip[l].reshape(1, D_SSM), w_glu=w_glu[l].astype(BF16), b_glu=b_glu[l].reshape(1, D_SSM),
        gexp=jnp.asarray(_gate_expand_matrix()), w_out=w_out[l].astype(BF16),
        ln1_g=ln1_g[l].reshape(1, D), ln1_b=ln1_b[l].reshape(1, D),
        w_router=jnp.pad(w_router[l], ((0, 0), (0, LANE - N_EXPERTS))),
        b_router=jnp.pad(b_router[l], (0, LANE - N_EXPERTS)).reshape(1, LANE))

    u, q, kvc, kvs, kvw, g = _mixer_in(x_prompt, mod_p[0], mod_p[1], w_in_pad, tm=512)
    y_ssm, h_p = _ssm_prompt(u, ssm_tab)
    o_cmp, o_sel, o_win = _nsa_prompt(q, kvc, kvs, kvw, cmp_tab, rel_bias)
    x1_p, hm_p, te_p, tw_p = _post_mixer(y_ssm, u, o_cmp, o_sel, o_win, g, x_prompt,
                                         mod_p[2], mod_p[3], mod_p[4], w_post, tm=256)

    u_s, q_s, kvc_s, kvs_s, kvw_s, g_s = _mixer_in(x_sample.reshape(1, Bs, D), mod_s[0], mod_s[1],
                                                   w_in_pad, tm=Bs)
    y_s, h_s = _ssm_sample(u_s[0], state_ssm_re[l], state_ssm_im[l], ssm_tab, c_re[l], c_im[l])
    oc_s, os_s, ow_s = _nsa_sample(q_s[0].astype(F32), kvc_s[0], kvs_s[0], kvw_s[0], cache_cmp_kv[l],
                                   cache_sel_kv[l], state_win_kv[l], page_table, cmp_tab, rel_bias)
    x1_s, hm_s, te_s, tw_s = _post_mixer(y_s[None], u_s, oc_s[None], os_s[None], ow_s[None], g_s,
                                         x_sample.reshape(1, Bs, D), mod_s[2], mod_s[3], mod_s[4],
                                         w_post, tm=Bs)

    n_p = Bp * T
    n_all = n_p + Bs
    hm_all = jnp.concatenate([hm_p.reshape(n_p, D), hm_s.reshape(Bs, D)], 0)
    te_all = jnp.concatenate([te_p.reshape(n_p, LANE), te_s.reshape(Bs, LANE)], 0)[:, :TOP_K]
    row_tok, dest, blk_e, n_used = _moe_dispatch(te_all, n_all)
    xb = jnp.concatenate([hm_all, jnp.zeros((1, D), F32)], 0)[row_tok]
    yb = _experts(xb, blk_e, n_used, w_gate_up[l], b_gate_up[l], w_down[l], b_down[l])
    ys = [yb[dest[:, k]] for k in range(TOP_K)]
    ln2g, ln2b = ln2_g[l].reshape(1, D), ln2_b[l].reshape(1, D)
    out_p = _final(x1_p, [y[:n_p].reshape(Bp, T, D) for y in ys], tw_p, mod_p[5], ln2g, ln2b, tm=512)
    out_s = _final(x1_s, [y[n_p:].reshape(1, Bs, D) for y in ys], tw_s, mod_s[5], ln2g, ln2b, tm=Bs)

    wlen = min(WINDOW, T)
    win_s = jnp.concatenate([state_win_kv[l], kvw_s[0].reshape(Bs, 1, *kv_tail)], 1)[:, -state_win_kv.shape[2]:]
    p_state = SSM_STATE
    return (out_p, out_s.reshape(Bs, 1, D),
            kvc.reshape(1, Bp, T, *kv_tail), kvc_s[0].reshape(1, Bs, 1, *kv_tail),
            kvs.reshape(1, Bp, T, *kv_tail), kvs_s[0].reshape(1, Bs, 1, *kv_tail),
            kvw[:, T - wlen:].reshape(1, Bp, wlen, *kv_tail), win_s[None],
            h_p[None, ..., :p_state], h_p[None, ..., p_state:],
            h_s[None, ..., :p_state], h_s[None, ..., p_state:])
```

```python
import functools
import math

import numpy as np
import jax
import jax.numpy as jnp
from jax import lax
from jax.experimental import pallas as pl
from jax.experimental.pallas import tpu as pltpu

D_MODEL = 1024
DEPTH = 1
PAST_LEN = 16384
PAGE_SIZE = 128
D_SSM = 512
SSM_GROUP = 16
N_SSM_GROUPS = D_SSM // SSM_GROUP
SSM_STATE = 64
N_HEADS = 8
HEAD_DIM = 64
N_KV_HEADS = 2
GQA = N_HEADS // N_KV_HEADS
D_ATT = N_HEADS * HEAD_DIM
D_KV = 2 * N_KV_HEADS * HEAD_DIM
CMP_STRIDE = 16
CMP_BLOCK = 2 * CMP_STRIDE
SEL_BLOCK = 64
N_SEL = 16
WINDOW = 512
NUM_BUCKETS = 32
REL_MAX_DIST = 1024
N_EXPERTS = 32
TOP_K = 4
D_FF = 1024
SWIGLU_LIMIT = 7.0
SWIGLU_ALPHA = 1.702
DN_ALPHA = (2 * DEPTH) ** 0.25
D_IN = D_SSM + D_ATT + 3 * D_KV + 3 * N_HEADS
NEG = -1e30
F32 = jnp.float32
BF16 = jnp.bfloat16
HIGHEST = lax.Precision.HIGHEST

LANE = 128
D_IN_PAD = 1920
GATE_COL = D_SSM + D_ATT + 3 * D_KV
SSM_CHUNK = 16
ATT_TQ = 128
ATT_TK = 128
MOE_ROWS = 256
VMEM_LIMIT = 48 * 1024 * 1024
LN_EPS = 1e-5


def _cparams(*sem):
    return pltpu.CompilerParams(dimension_semantics=sem, vmem_limit_bytes=VMEM_LIMIT)


def _nt_dot(a, b):
    return lax.dot_general(a, b, (((1,), (1,)), ((), ())), preferred_element_type=F32)


def _layer_norm(x):
    mu = jnp.mean(x, axis=-1, keepdims=True)
    xc = x - mu
    var = jnp.mean(xc * xc, axis=-1, keepdims=True)
    return xc * lax.rsqrt(var + LN_EPS)


def _adaln_kernel(c_ref, w_ref, b_ref, o_ref):
    c = c_ref[...]
    s = c * jax.nn.sigmoid(c)
    o_ref[...] = jnp.dot(s, w_ref[...], precision=HIGHEST, preferred_element_type=F32) + b_ref[...]


def _adaln(c, w, b):
    n, d = c.shape
    dout = w.shape[1]
    tn = 1024
    return pl.pallas_call(
        _adaln_kernel,
        out_shape=jax.ShapeDtypeStruct((n, dout), F32),
        grid=(dout // tn,),
        in_specs=[pl.BlockSpec((n, d), lambda j: (0, 0)),
                  pl.BlockSpec((d, tn), lambda j: (0, j)),
                  pl.BlockSpec((1, tn), lambda j: (0, j))],
        out_specs=pl.BlockSpec((n, tn), lambda j: (0, j)),
        compiler_params=_cparams("arbitrary"),
        name="adaln",
    )(c, w, b.reshape(1, dout))


def _mixer_in_kernel(x_ref, sh_ref, sc_ref, w_ref, u_ref, q_ref, kvc_ref, kvs_ref, kvw_ref, g_ref):
    h = _layer_norm(x_ref[0]) * (1.0 + sc_ref[0]) + sh_ref[0]
    z = jnp.dot(h.astype(BF16), w_ref[...], preferred_element_type=F32)
    c0 = D_SSM
    c1 = c0 + D_ATT
    c2 = c1 + D_KV
    c3 = c2 + D_KV
    c4 = c3 + D_KV
    u_ref[0] = z[:, :c0]
    q_ref[0] = z[:, c0:c1].astype(BF16)
    kvc_ref[0] = z[:, c1:c2]
    kvs_ref[0] = z[:, c2:c3]
    kvw_ref[0] = z[:, c3:c4]
    g_ref[0] = z[:, c4:c4 + LANE]


def _mixer_in(x, shift, scale, w_pad, tm):
    B, T, D = x.shape
    R = shift.shape[1]
    rb = 1 if R == 1 else tm
    mod_map = (lambda b, i: (b, 0, 0)) if R == 1 else (lambda b, i: (b, i, 0))
    row = lambda n: pl.BlockSpec((1, tm, n), lambda b, i: (b, i, 0))
    outs = (jax.ShapeDtypeStruct((B, T, D_SSM), F32), jax.ShapeDtypeStruct((B, T, D_ATT), BF16),
            jax.ShapeDtypeStruct((B, T, D_KV), F32), jax.ShapeDtypeStruct((B, T, D_KV), F32),
            jax.ShapeDtypeStruct((B, T, D_KV), F32), jax.ShapeDtypeStruct((B, T, LANE), F32))
    return pl.pallas_call(
        _mixer_in_kernel,
        out_shape=outs,
        grid=(B, T // tm),
        in_specs=[row(D), pl.BlockSpec((1, rb, D), mod_map), pl.BlockSpec((1, rb, D), mod_map),
                  pl.BlockSpec((D, D_IN_PAD), lambda b, i: (0, 0))],
        out_specs=(row(D_SSM), row(D_ATT), row(D_KV), row(D_KV), row(D_KV), row(LANE)),
        compiler_params=_cparams("parallel", "parallel"),
        name="mixer_in",
    )(x, shift, scale, w_pad)


def _ssm_tables(lam_re, lam_im, log_dt, b_re, b_im, c_re, c_im, L, n_levels):
    G, P = lam_re.shape
    C = b_re.shape[-1]
    dt = jnp.exp(log_dt.astype(F32))[:, None]
    er, ei = lam_re * dt, lam_im * dt

    def power(k):
        kk = k.astype(F32)[:, None, None]
        mag = jnp.exp(kk * er)
        return mag * jnp.cos(kk * ei), mag * jnp.sin(kk * ei)

    lb_re, lb_im = power(jnp.ones((1,), F32))
    nr, ni = lb_re[0] - 1.0, lb_im[0]
    den = lam_re * lam_re + lam_im * lam_im
    fr = (nr * lam_re + ni * lam_im) / den
    fi = (ni * lam_re - nr * lam_im) / den
    bbr = fr[:, :, None] * b_re - fi[:, :, None] * b_im
    bbi = fr[:, :, None] * b_im + fi[:, :, None] * b_re
    pr, pi = power(jnp.arange(L + 1))
    clr = c_re[None] * pr[:, :, None, :] - c_im[None] * pi[:, :, None, :]
    cli = c_re[None] * pi[:, :, None, :] + c_im[None] * pr[:, :, None, :]
    kern = (jnp.einsum('kgcp,gpd->kgcd', clr[:L], bbr, precision=HIGHEST)
            - jnp.einsum('kgcp,gpd->kgcd', cli[:L], bbi, precision=HIGHEST))
    kz = jnp.concatenate([kern, jnp.zeros((1,) + kern.shape[1:], F32)], 0)
    ts = np.arange(L)
    lag = ts[None, :] - ts[:, None]
    lag = np.where(lag >= 0, lag, L)
    toep = kz[lag]
    toep = jnp.transpose(toep, (2, 0, 4, 1, 3)).reshape(G, L * C, L * C)
    rev = L - 1 - ts
    wsr = pr[rev][:, :, :, None] * bbr[None] - pi[rev][:, :, :, None] * bbi[None]
    wsi = pr[rev][:, :, :, None] * bbi[None] + pi[rev][:, :, :, None] * bbr[None]
    ws = jnp.concatenate([jnp.transpose(wsr, (1, 0, 3, 2)), jnp.transpose(wsi, (1, 0, 3, 2))], -1)
    ws = ws.reshape(G, L * C, 2 * P)
    wy = jnp.concatenate([jnp.transpose(clr[1:], (1, 3, 0, 2)), -jnp.transpose(cli[1:], (1, 3, 0, 2))], 1)
    wy = wy.reshape(G, 2 * P, L * C)
    lr, li = power(L * (2 ** jnp.arange(n_levels)))
    ar = jnp.transpose(jnp.concatenate([lr, lr], -1), (1, 0, 2))
    ai = jnp.transpose(jnp.concatenate([-li, li], -1), (1, 0, 2))
    return toep.astype(BF16), ws.astype(BF16), wy.astype(BF16), ar, ai, (lb_re[0], lb_im[0], bbr, bbi)


def _ssm_kernel(u_ref, toep_ref, ws_ref, wy_ref, ar_ref, ai_ref, y_ref, hl_ref, *, nb, nc, n_levels):
    u = u_ref[0]
    y1 = jnp.dot(u, toep_ref[0], preferred_element_type=F32)
    s = jnp.dot(u, ws_ref[0], preferred_element_type=F32)
    p2 = s.shape[-1]
    rows = lax.broadcasted_iota(jnp.int32, (nc, p2), 0)
    prev = []
    for b in range(nb):
        h = s[b * nc:(b + 1) * nc]
        for k in range(n_levels):
            d = 1 << k
            sh = jnp.where(rows >= d, pltpu.roll(h, d, axis=0), 0.0)
            sw = pltpu.roll(sh, p2 // 2, axis=1)
            h = h + ar_ref[0, k:k + 1, :] * sh + ai_ref[0, k:k + 1, :] * sw
        hl_ref[0, b:b + 1, :] = h[nc - 1:nc, :]
        prev.append(jnp.where(rows >= 1, pltpu.roll(h, 1, axis=0), 0.0))
    hp = jnp.concatenate(prev, axis=0)
    y2 = jnp.dot(hp.astype(BF16), wy_ref[0], preferred_element_type=F32)
    y_ref[0] = y1 + y2


def _ssm_prompt(u, tables):
    toep, ws, wy, ar, ai, _ = tables
    B, T, _ = u.shape
    G, C, L = N_SSM_GROUPS, SSM_GROUP, SSM_CHUNK
    nc = T // L
    n_levels = ar.shape[1]
    ug = jnp.transpose(u.reshape(B, nc, L, G, C), (3, 0, 1, 2, 4)).reshape(G, B * nc, L * C).astype(BF16)
    grp = lambda r, c: pl.BlockSpec((1, r, c), lambda g: (g, 0, 0))
    y, hl = pl.pallas_call(
        functools.partial(_ssm_kernel, nb=B, nc=nc, n_levels=n_levels),
        out_shape=(jax.ShapeDtypeStruct((G, B * nc, L * C), F32),
                   jax.ShapeDtypeStruct((G, B, 2 * SSM_STATE), F32)),
        grid=(G,),
        in_specs=[grp(B * nc, L * C), grp(L * C, L * C), grp(L * C, 2 * SSM_STATE),
                  grp(2 * SSM_STATE, L * C), grp(n_levels, 2 * SSM_STATE), grp(n_levels, 2 * SSM_STATE)],
        out_specs=(grp(B * nc, L * C), grp(B, 2 * SSM_STATE)),
        compiler_params=_cparams("parallel"),
        name="ssm_prompt",
    )(ug, toep, ws, wy, ar, ai)
    y = jnp.transpose(y.reshape(G, B, nc, L, C), (1, 2, 3, 0, 4)).reshape(B, T, D_SSM)
    return y, jnp.transpose(hl, (1, 0, 2))


def _ssm_step_kernel(u_ref, h0_ref, bb_ref, lr_ref, li_ref, cy_ref, y_ref, h_ref):
    p = lr_ref.shape[-1] // 2
    bu = jnp.einsum('gbc,gcp->gbp', u_ref[...], bb_ref[...], preferred_element_type=F32)
    h0 = h0_ref[...]
    h0s = jnp.concatenate([h0[..., p:], h0[..., :p]], axis=-1)
    h = lr_ref[...] * h0 + li_ref[...] * h0s + bu
    h_ref[...] = h
    y_ref[...] = jnp.einsum('gbp,gpc->gbc', h.astype(BF16), cy_ref[...], preferred_element_type=F32)


def _ssm_sample(u, h0_re, h0_im, tables, c_re, c_im):
    lb_re, lb_im, bbr, bbi = tables[-1]
    B = u.shape[0]
    G, C, P = N_SSM_GROUPS, SSM_GROUP, SSM_STATE
    ug = jnp.transpose(u.reshape(B, G, C), (1, 0, 2)).astype(BF16)
    h0 = jnp.transpose(jnp.concatenate([h0_re, h0_im], -1), (1, 0, 2)).astype(F32)
    bb = jnp.concatenate([jnp.transpose(bbr, (0, 2, 1)), jnp.transpose(bbi, (0, 2, 1))], -1).astype(BF16)
    lr = jnp.concatenate([lb_re, lb_re], -1)[:, None, :]
    li = jnp.concatenate([-lb_im, lb_im], -1)[:, None, :]
    cy = jnp.concatenate([jnp.transpose(c_re, (0, 2, 1)), -jnp.transpose(c_im, (0, 2, 1))], 1).astype(BF16)
    y, h = pl.pallas_call(
        _ssm_step_kernel,
        out_shape=(jax.ShapeDtypeStruct((G, B, C), F32), jax.ShapeDtypeStruct((G, B, 2 * P), F32)),
        name="ssm_step",
    )(ug, h0, bb, lr, li, cy)
    return jnp.transpose(y, (1, 0, 2)).reshape(B, D_SSM), jnp.transpose(h, (1, 0, 2))


def _compress_tables(phi_pe, phi_w1, phi_b1, phi_w2, phi_b2):
    S, H, Dh = CMP_STRIDE, N_KV_HEADS, HEAD_DIM
    w1 = phi_w1.reshape(2, 2, S, Dh, Dh)
    eye_c = jnp.eye(2, dtype=F32)
    eye_h = jnp.eye(H, dtype=F32)
    wbig = jnp.einsum('cajde,xc,yh->jxydache', w1, eye_c, eye_h).reshape(S * 2 * H * Dh, 2 * 2 * H * Dh)
    pe = jnp.transpose(phi_pe.reshape(2, 2, S, Dh), (1, 2, 0, 3))
    pe_rows = jnp.broadcast_to(pe[:, :, :, None, :], (2, S, 2, H, Dh)).reshape(2, 1, S * 2 * H * Dh)
    b1 = jnp.broadcast_to(phi_b1[:, None, :], (2, H, Dh)).reshape(1, 2 * H * Dh)
    w2 = jnp.einsum('cef,cx,hy->chexyf', phi_w2, eye_c, eye_h).reshape(2 * H * Dh, 2 * H * Dh)
    b2 = jnp.broadcast_to(phi_b2[:, None, :], (2, H, Dh)).reshape(1, 2 * H * Dh)
    return wbig.astype(BF16), pe_rows, b1, w2.astype(BF16), b2


def _compress_in_kernel(x_ref, pe_ref, w_ref, z_ref):
    x = x_ref[0]
    n = w_ref.shape[1] // 2
    z_ref[0, :, :n] = jnp.dot((x + pe_ref[0]).astype(BF16), w_ref[:, :n], preferred_element_type=F32)
    z_ref[0, :, n:] = jnp.dot((x + pe_ref[1]).astype(BF16), w_ref[:, n:], preferred_element_type=F32)


def _compress_out_kernel(z_ref, b1_ref, w2_ref, b2_ref, o_ref):
    z = z_ref[0]
    n = z.shape[-1] // 2
    rows = z.shape[0]
    second = pltpu.roll(z[:, n:], rows - 1, axis=0)
    hdn = jax.nn.gelu(z[:, :n] + second + b1_ref[...])
    o_ref[0] = jnp.dot(hdn.astype(BF16), w2_ref[...], preferred_element_type=F32) + b2_ref[...]


def _compress(x2, tables, tr):
    wbig, pe_rows, b1, w2, b2 = tables
    B, n, K = x2.shape
    N2 = wbig.shape[1]
    z = pl.pallas_call(
        _compress_in_kernel,
        out_shape=jax.ShapeDtypeStruct((B, n, N2), F32),
        grid=(B, n // tr),
        in_specs=[pl.BlockSpec((1, tr, K), lambda b, i: (b, i, 0)),
                  pl.BlockSpec((2, 1, K), lambda b, i: (0, 0, 0)),
                  pl.BlockSpec((K, N2), lambda b, i: (0, 0))],
        out_specs=pl.BlockSpec((1, tr, N2), lambda b, i: (b, i, 0)),
        compiler_params=_cparams("parallel", "parallel"),
        name="compress_in",
    )(x2, pe_rows, wbig)
    return pl.pallas_call(
        _compress_out_kernel,
        out_shape=jax.ShapeDtypeStruct((B, n, N2 // 2), F32),
        grid=(B,),
        in_specs=[pl.BlockSpec((1, n, N2), lambda b: (b, 0, 0)),
                  pl.BlockSpec((1, N2 // 2), lambda b: (0, 0)),
                  pl.BlockSpec((N2 // 2, N2 // 2), lambda b: (0, 0)),
                  pl.BlockSpec((1, N2 // 2), lambda b: (0, 0))],
        out_specs=pl.BlockSpec((1, n, N2 // 2), lambda b: (b, 0, 0)),
        compiler_params=_cparams("parallel"),
        name="compress_out",
    )(z, b1, w2, b2)


def _rel_bucket(dist):
    n = jnp.maximum(dist, 0)
    max_exact = NUM_BUCKETS // 2
    nf = jnp.maximum(n, 1).astype(F32)
    large = max_exact + (jnp.log(nf / max_exact) / math.log(REL_MAX_DIST / max_exact)
                         * (NUM_BUCKETS - max_exact)).astype(jnp.int32)
    large = jnp.minimum(large, NUM_BUCKETS - 1)
    return jnp.where(n < max_exact, n, large)


def _bias_by_distance(rel_bias, n_max):
    return jnp.transpose(rel_bias[_rel_bucket(jnp.arange(n_max))].astype(F32))


def _shifted_chunks(bias_n, pad, n_chunks, width):
    n = min(bias_n.shape[1], n_chunks * width - pad)
    ext = jnp.concatenate([jnp.broadcast_to(bias_n[:, :1], (N_HEADS, pad)), bias_n[:, :n],
                           jnp.zeros((N_HEADS, n_chunks * width - pad - n), F32)], axis=1)
    return ext.reshape(N_HEADS, n_chunks, width)


def _bias_tables_kernel(ed_ref, ec_ref, tzs_ref, tzw_ref, cmp_ref, *, tq, tk, n_qt):
    n_ds, n_dw, n_j = tzs_ref.shape[1], tzw_ref.shape[1], cmp_ref.shape[1] // 8
    w = tq + tk
    c = lax.broadcasted_iota(jnp.int32, (tk, tq), 0)
    r = lax.broadcasted_iota(jnp.int32, (tk, tq), 1)
    for d in range(n_ds):
        v = jnp.concatenate([ed_ref[0, d:d + 1, :], ed_ref[0, d + 1:d + 2, :]], axis=1)
        t = pltpu.roll(jnp.broadcast_to(v, (tk, w)), w - (tk - 1), axis=1, stride=1, stride_axis=0)[:, :tq]
        dist = d * tk + r - c
        tzs_ref[0, d] = jnp.where(dist >= 0, t, NEG)
        if d < n_dw:
            tzw_ref[0, d] = jnp.where((dist >= 0) & (dist <= WINDOW), t, NEG)
    for j in range(n_j):
        dd = n_qt - 1 - j
        c0, c1 = max(dd, 0), max(dd + 1, 0)
        v = jnp.concatenate([ec_ref[0, c0:c0 + 1, :], ec_ref[0, c1:c1 + 1, :]], axis=1)
        t = pltpu.roll(jnp.broadcast_to(v, (8, w)), w - 7 * CMP_STRIDE, axis=1, stride=CMP_STRIDE, stride_axis=0)
        cmp_ref[0, j * 8:(j + 1) * 8, :] = t[:, :tq]


def _bias_tables(bias_n, n_qt, n_rb, n_ds, n_dw, tq, tk):
    assert tq == tk == 8 * CMP_STRIDE and n_dw <= n_ds
    n_j = n_rb + n_qt - 1
    ed = _shifted_chunks(bias_n, tk - 1, n_ds + 1, tq)
    ec = _shifted_chunks(bias_n, 7 * CMP_STRIDE + CMP_BLOCK - 1, n_qt + 1, tq)
    head = lambda a: pl.BlockSpec((1,) + a.shape[1:], lambda h: (h,) + (0,) * (a.ndim - 1))
    outs = (jax.ShapeDtypeStruct((N_HEADS, n_ds, tk, tq), F32), jax.ShapeDtypeStruct((N_HEADS, n_dw, tk, tq), F32),
            jax.ShapeDtypeStruct((N_HEADS, n_j * 8, tq), F32))
    tzs, tzw, cmp = pl.pallas_call(
        functools.partial(_bias_tables_kernel, tq=tq, tk=tk, n_qt=n_qt),
        out_shape=outs,
        grid=(N_HEADS,),
        in_specs=[head(ed), head(ec)],
        out_specs=tuple(head(o) for o in outs),
        compiler_params=_cparams("parallel"),
        name="bias_tables",
    )(ed, ec)
    grp = lambda a: a.reshape((N_KV_HEADS, GQA) + a.shape[1:])
    return grp(tzs), grp(tzw), cmp


def _pool_matrix(n_cmp_pad, n_blk_pad):
    r = SEL_BLOCK // CMP_STRIDE
    i = np.arange(n_cmp_pad)[None, :]
    j = np.arange(n_blk_pad)[:, None]
    return ((i >= r * j - 1) & (i <= r * j + r - 1)).astype(np.float32)


def _cmp_select_kernel(q_ref, k_ref, vt_ref, bias_ref, pool_ref, o_ref, sel_ref, *, tq, n_cmp):
    qt = pl.program_id(2)
    n_qt = pl.num_programs(2)
    q = q_ref[0, 0].reshape(GQA * tq, HEAD_DIM)
    k = k_ref[0, 0]
    nc = k.shape[0]
    s = _nt_dot(k, q)
    row0 = pl.multiple_of((n_qt - 1 - qt) * 8, 8)
    s = s + jnp.concatenate([bias_ref[g, pl.ds(row0, nc), :] for g in range(GQA)], axis=-1)
    t_pos = qt * tq + (lax.broadcasted_iota(jnp.int32, (nc, GQA * tq), 1) % tq)
    ci = lax.broadcasted_iota(jnp.int32, (nc, GQA * tq), 0)
    mask = (ci * CMP_STRIDE + CMP_BLOCK - 1 <= t_pos) & (ci < n_cmp)
    s = jnp.where(mask, s, NEG)
    m = jnp.max(s, axis=0, keepdims=True)
    p = jnp.where(mask, jnp.exp(s - m), 0.0)
    p = p / jnp.maximum(jnp.sum(p, axis=0, keepdims=True), 1e-30)
    ot = jnp.dot(vt_ref[0, 0], p.astype(BF16), preferred_element_type=F32)
    o_ref[0] = jnp.concatenate([ot[:, g * tq:(g + 1) * tq].T for g in range(GQA)], axis=-1)
    imp = p[:, 0:tq]
    for g in range(1, GQA):
        imp = imp + p[:, g * tq:(g + 1) * tq]
    sb = jnp.dot(pool_ref[...], imp, precision=HIGHEST, preferred_element_type=F32)
    nb = sb.shape[0]
    blk = lax.broadcasted_iota(jnp.int32, (nb, tq), 0)
    cur = (qt * tq + lax.broadcasted_iota(jnp.int32, (nb, tq), 1)) // SEL_BLOCK
    causal = blk <= cur
    forced = (blk == 0) | (blk == cur) | (blk == cur - 1)
    sc = jnp.where(forced & causal, 1e4, jnp.where(causal, sb, -1.0))
    rank = jnp.zeros((nb, tq), jnp.int32)
    for i in range(nb):
        row = sc[i:i + 1, :]
        ahead = (row > sc) | ((row == sc) & (blk > i))
        rank = rank + ahead.astype(jnp.int32)
    sel_ref[0, 0] = jnp.where((rank < N_SEL) & causal, 0.0, NEG)


def _cmp_select_prompt(q5, kc, vct, bias_tab, pool, n_cmp):
    B, _, _, T, _ = q5.shape
    NC = kc.shape[2]
    NB = pool.shape[0]
    R = bias_tab.shape[1]
    tq = ATT_TQ
    return pl.pallas_call(
        functools.partial(_cmp_select_kernel, tq=tq, n_cmp=n_cmp),
        out_shape=(jax.ShapeDtypeStruct((B, T, D_ATT), F32),
                   jax.ShapeDtypeStruct((B, N_KV_HEADS, NB, T), F32)),
        grid=(B, N_KV_HEADS, T // tq),
        in_specs=[pl.BlockSpec((1, 1, GQA, tq, HEAD_DIM), lambda b, h, i: (b, h, 0, i, 0)),
                  pl.BlockSpec((1, 1, NC, HEAD_DIM), lambda b, h, i: (b, h, 0, 0)),
                  pl.BlockSpec((1, 1, HEAD_DIM, NC), lambda b, h, i: (b, h, 0, 0)),
                  pl.BlockSpec((GQA, R, tq), lambda b, h, i: (h, 0, 0)),
                  pl.BlockSpec((NB, NC), lambda b, h, i: (0, 0))],
        out_specs=(pl.BlockSpec((1, tq, GQA * HEAD_DIM), lambda b, h, i: (b, i, h)),
                   pl.BlockSpec((1, 1, NB, tq), lambda b, h, i: (b, h, 0, i))),
        compiler_params=_cparams("parallel", "parallel", "parallel"),
        name="cmp_select_prompt",
    )(q5, kc, vct, bias_tab, pool)


def _sel_win_kernel(q_ref, ks_ref, vst_ref, kw_ref, vwt_ref, sel_ref, tzs_ref, tzw_ref, os_ref, ow_ref, *, tq):
    tk = ATT_TK
    qt = pl.program_id(2)
    q = q_ref[0, 0].reshape(GQA * tq, HEAD_DIM)
    n_ds = tzs_ref.shape[2]
    n_dw = tzw_ref.shape[2]
    per_tile = tk // SEL_BLOCK

    def make_step(k_ref, vt_ref, tz_ref, n_d, use_sel):
        def step(kt, carry):
            m, l, acc = carry
            off = pl.multiple_of(kt * tk, tk)
            k = k_ref[0, 0, pl.ds(off, tk), :]
            vt = vt_ref[0, 0, :, pl.ds(off, tk)]
            d = jnp.minimum(qt - kt, n_d - 1)
            bias = [tz_ref[0, g, d] for g in range(GQA)]
            if use_sel:
                rows = sel_ref[0, 0, pl.ds(kt * per_tile, per_tile), :]
                selb = jnp.concatenate([jnp.broadcast_to(rows[i:i + 1], (SEL_BLOCK, tq))
                                        for i in range(per_tile)], axis=0)
                bias = [b + selb for b in bias]
            s = _nt_dot(k, q) + jnp.concatenate(bias, axis=1)
            m_new = jnp.maximum(m, jnp.max(s, axis=0, keepdims=True))
            alpha = jnp.exp(m - m_new)
            p = jnp.exp(s - m_new)
            l = alpha * l + jnp.sum(p, axis=0, keepdims=True)
            acc = alpha * acc + jnp.dot(vt, p.astype(BF16), preferred_element_type=F32)
            return m_new, l, acc
        return step

    def init():
        return (jnp.full((1, GQA * tq), NEG, F32), jnp.zeros((1, GQA * tq), F32),
                jnp.zeros((HEAD_DIM, GQA * tq), F32))

    def finish(carry):
        _, l, acc = carry
        o = acc / jnp.maximum(l, 1e-30)
        return jnp.concatenate([o[:, g * tq:(g + 1) * tq].T for g in range(GQA)], axis=-1)

    os_ref[0] = finish(lax.fori_loop(0, qt + 1, make_step(ks_ref, vst_ref, tzs_ref, n_ds, True), init()))
    lo = jnp.maximum(qt - (n_dw - 1), 0)
    ow_ref[0] = finish(lax.fori_loop(lo, qt + 1, make_step(kw_ref, vwt_ref, tzw_ref, n_dw, False), init()))


def _sel_win_prompt(q5, ks, vst, kw, vwt, sel, tzs, tzw):
    B, _, _, T, _ = q5.shape
    NB = sel.shape[2]
    tq = ATT_TQ
    k_spec = pl.BlockSpec((1, 1, T, HEAD_DIM), lambda b, h, i: (b, h, 0, 0))
    vt_spec = pl.BlockSpec((1, 1, HEAD_DIM, T), lambda b, h, i: (b, h, 0, 0))
    tz_spec = lambda tz: pl.BlockSpec((1,) + tz.shape[1:], lambda b, h, i: (h, 0, 0, 0, 0))
    o_spec = pl.BlockSpec((1, tq, GQA * HEAD_DIM), lambda b, h, i: (b, i, h))
    return pl.pallas_call(
        functools.partial(_sel_win_kernel, tq=tq),
        out_shape=(jax.ShapeDtypeStruct((B, T, D_ATT), F32), jax.ShapeDtypeStruct((B, T, D_ATT), F32)),
        grid=(B, N_KV_HEADS, T // tq),
        in_specs=[pl.BlockSpec((1, 1, GQA, tq, HEAD_DIM), lambda b, h, i: (b, h, 0, i, 0)),
                  k_spec, vt_spec, k_spec, vt_spec,
                  pl.BlockSpec((1, 1, NB, tq), lambda b, h, i: (b, h, 0, i)),
                  tz_spec(tzs), tz_spec(tzw)],
        out_specs=(o_spec, o_spec),
        compiler_params=_cparams("parallel", "parallel", "parallel"),
        name="sel_win_prompt",
    )(q5, ks, vst, kw, vwt, sel, tzs, tzw)


def _gate_expand_matrix():
    m = np.zeros((3, LANE, D_ATT), np.float32)
    for r in range(3):
        for h in range(N_HEADS):
            m[r, h * 3 + r, h * HEAD_DIM:(h + 1) * HEAD_DIM] = 1.0
    return m


def _post_mixer_kernel(y_ref, u_ref, oc_ref, os_ref, ow_ref, g_ref, x_ref, gate_ref, sh_ref, sc_ref,
                       dskip_ref, wglu_ref, bglu_ref, gexp_ref, wout_ref, lng_ref, lnb_ref,
                       wr_ref, br_ref, x1_ref, hm_ref, te_ref, tw_ref):
    y = y_ref[0] + dskip_ref[...] * u_ref[0]
    gl = jax.nn.gelu(y)
    ssm = gl * jax.nn.sigmoid(jnp.dot(gl.astype(BF16), wglu_ref[...], preferred_element_type=F32)
                              + bglu_ref[...])
    sg = jax.nn.sigmoid(g_ref[0])
    att = jnp.zeros_like(oc_ref[0])
    for r, o_ref in enumerate((oc_ref, os_ref, ow_ref)):
        att = att + jnp.dot(sg, gexp_ref[r], precision=HIGHEST, preferred_element_type=F32) * o_ref[0]
    h = (jnp.dot(ssm.astype(BF16), wout_ref[:D_SSM, :], preferred_element_type=F32)
         + jnp.dot(att.astype(BF16), wout_ref[D_SSM:, :], preferred_element_type=F32))
    z = DN_ALPHA * x_ref[0] + gate_ref[0] * h
    x1 = _layer_norm(z) * lng_ref[...] + lnb_ref[...]
    x1_ref[0] = x1
    hm = _layer_norm(x1) * (1.0 + sc_ref[0]) + sh_ref[0]
    hm_ref[0] = hm
    logits = jnp.dot(hm, wr_ref[...], precision=HIGHEST, preferred_element_type=F32) + br_ref[...]
    lane = lax.broadcasted_iota(jnp.int32, logits.shape, 1)
    work = jnp.where(lane < N_EXPERTS, logits, -jnp.inf)
    te = jnp.zeros(logits.shape, jnp.int32)
    tv = jnp.zeros(logits.shape, F32)
    for k in range(TOP_K):
        best = jnp.max(work, axis=-1, keepdims=True)
        arg = jnp.min(jnp.where(work == best, lane, LANE), axis=-1, keepdims=True)
        te = jnp.where(lane == k, arg, te)
        tv = jnp.where(lane == k, best, tv)
        work = jnp.where(lane == arg, -jnp.inf, work)
    ex = jnp.where(lane < TOP_K, jnp.exp(tv - tv[:, 0:1]), 0.0)
    te_ref[0] = te
    tw_ref[0] = ex / jnp.sum(ex, axis=-1, keepdims=True)


def _post_mixer(y, u, oc, osel, ow, g, x, gate, shift, scale, w, tm):
    B, T, D = x.shape
    R = gate.shape[1]
    rb = 1 if R == 1 else tm
    mod_map = (lambda b, i: (b, 0, 0)) if R == 1 else (lambda b, i: (b, i, 0))
    row = lambda n: pl.BlockSpec((1, tm, n), lambda b, i: (b, i, 0))
    mod = pl.BlockSpec((1, rb, D), mod_map)
    full = lambda a: pl.BlockSpec(a.shape, lambda b, i: (0,) * a.ndim)
    consts = (w['d_skip'], w['w_glu'], w['b_glu'], w['gexp'], w['w_out'], w['ln1_g'], w['ln1_b'],
              w['w_router'], w['b_router'])
    return pl.pallas_call(
        _post_mixer_kernel,
        out_shape=(jax.ShapeDtypeStruct((B, T, D), F32), jax.ShapeDtypeStruct((B, T, D), F32),
                   jax.ShapeDtypeStruct((B, T, LANE), jnp.int32), jax.ShapeDtypeStruct((B, T, LANE), F32)),
        grid=(B, T // tm),
        in_specs=[row(D_SSM), row(D_SSM), row(D_ATT), row(D_ATT), row(D_ATT), row(LANE), row(D),
                  mod, mod, mod] + [full(a) for a in consts],
        out_specs=(row(D), row(D), row(LANE), row(LANE)),
        compiler_params=_cparams("parallel", "parallel"),
        name="post_mixer",
    )(y, u, oc, osel, ow, g, x, gate, shift, scale, *consts)


def _expert_kernel(be_ref, nu_ref, x_ref, wgu_ref, bgu_ref, wd_ref, bd_ref, o_ref, wgu_s, wd_s):
    i = pl.program_id(0)
    prev = be_ref[jnp.maximum(i - 1, 0)]
    fresh = (i == 0) | (be_ref[i] != prev)

    @pl.when(fresh)
    def _():
        wgu_s[...] = wgu_ref[0].astype(BF16)
        wd_s[...] = wd_ref[0].astype(BF16)

    @pl.when(i < nu_ref[0])
    def _():
        gu = jnp.dot(x_ref[...].astype(BF16), wgu_s[...], preferred_element_type=F32) + bgu_ref[0]
        gate = jnp.minimum(gu[:, :D_FF], SWIGLU_LIMIT)
        up = jnp.clip(gu[:, D_FF:], -SWIGLU_LIMIT, SWIGLU_LIMIT)
        hh = (up + 1.0) * gate * jax.nn.sigmoid(SWIGLU_ALPHA * gate)
        o_ref[...] = jnp.dot(hh.astype(BF16), wd_s[...], preferred_element_type=F32) + bd_ref[0]

    @pl.when(i >= nu_ref[0])
    def _():
        o_ref[...] = jnp.zeros_like(o_ref)


def _experts(xb, blk_e, n_used, w_gate_up, b_gate_up, w_down, b_down):
    rows, D = xb.shape
    n_blk = rows // MOE_ROWS
    grid_spec = pltpu.PrefetchScalarGridSpec(
        num_scalar_prefetch=2,
        grid=(n_blk,),
        in_specs=[pl.BlockSpec((MOE_ROWS, D), lambda i, be, nu: (i, 0)),
                  pl.BlockSpec((1, D, 2 * D_FF), lambda i, be, nu: (be[i], 0, 0)),
                  pl.BlockSpec((1, 1, 2 * D_FF), lambda i, be, nu: (be[i], 0, 0)),
                  pl.BlockSpec((1, D_FF, D), lambda i, be, nu: (be[i], 0, 0)),
                  pl.BlockSpec((1, 1, D), lambda i, be, nu: (be[i], 0, 0))],
        out_specs=pl.BlockSpec((MOE_ROWS, D), lambda i, be, nu: (i, 0)),
        scratch_shapes=[pltpu.VMEM((D, 2 * D_FF), BF16), pltpu.VMEM((D_FF, D), BF16)],
    )
    return pl.pallas_call(
        _expert_kernel,
        out_shape=jax.ShapeDtypeStruct((rows, D), F32),
        grid_spec=grid_spec,
        compiler_params=_cparams("arbitrary"),
        name="moe_experts",
    )(blk_e, n_used, xb, w_gate_up, b_gate_up.reshape(N_EXPERTS, 1, 2 * D_FF), w_down,
      b_down.reshape(N_EXPERTS, 1, D))


def _moe_dispatch(top_e, n):
    blk = MOE_ROWS
    nk = n * TOP_K
    e = top_e.reshape(-1)
    order = jnp.argsort(e)
    e_s = e[order]
    counts = jnp.bincount(e, length=N_EXPERTS)
    pcounts = (counts + blk - 1) // blk * blk
    start = jnp.cumsum(counts) - counts
    pend = jnp.cumsum(pcounts)
    pstart = pend - pcounts
    dest_sorted = (pstart[e_s] + jnp.arange(nk) - start[e_s]).astype(jnp.int32)
    n_blk = (nk + N_EXPERTS * (blk - 1)) // blk
    rows = n_blk * blk
    row_tok = jnp.full((rows,), n, jnp.int32).at[dest_sorted].set((order // TOP_K).astype(jnp.int32))
    dest = jnp.zeros((nk,), jnp.int32).at[order].set(dest_sorted)
    blk_e = jnp.sum(pend[None, :] <= (jnp.arange(n_blk) * blk)[:, None], axis=1)
    blk_e = jnp.minimum(blk_e, N_EXPERTS - 1).astype(jnp.int32)
    n_used = (pend[-1] // blk).astype(jnp.int32).reshape(1)
    return row_tok, dest.reshape(n, TOP_K), blk_e, n_used


def _final_kernel(x_ref, y0_ref, y1_ref, y2_ref, y3_ref, tw_ref, gate_ref, lng_ref, lnb_ref, o_ref):
    tw = tw_ref[0]
    y = jnp.zeros_like(x_ref[0])
    for k, y_ref in enumerate((y0_ref, y1_ref, y2_ref, y3_ref)):
        y = y + tw[:, k:k + 1] * y_ref[0]
    z = DN_ALPHA * x_ref[0] + gate_ref[0] * y
    o_ref[0] = _layer_norm(z) * lng_ref[...] + lnb_ref[...]


def _final(x1, ys, tw, gate, ln_g, ln_b, tm):
    B, T, D = x1.shape
    R = gate.shape[1]
    rb = 1 if R == 1 else tm
    mod_map = (lambda b, i: (b, 0, 0)) if R == 1 else (lambda b, i: (b, i, 0))
    row = lambda n: pl.BlockSpec((1, tm, n), lambda b, i: (b, i, 0))
    vec = pl.BlockSpec((1, D), lambda b, i: (0, 0))
    return pl.pallas_call(
        _final_kernel,
        out_shape=jax.ShapeDtypeStruct((B, T, D), F32),
        grid=(B, T // tm),
        in_specs=[row(D), row(D), row(D), row(D), row(D), row(LANE),
                  pl.BlockSpec((1, rb, D), mod_map), vec, vec],
        out_specs=row(D),
        compiler_params=_cparams("parallel", "parallel"),
        name="moe_combine_ln",
    )(x1, *ys, tw, gate, ln_g, ln_b)


def _cmp_select_step_kernel(q_ref, k_ref, v_ref, bias_ref, pool_ref, o_ref, idx_ref, *, n_cmp, n_blk, q_pos):
    q = q_ref[0].astype(BF16)
    ncp = k_ref.shape[2]
    nbp = pool_ref.shape[1]
    row = lax.broadcasted_iota(jnp.int32, (N_HEADS, 1), 0)
    first = row < GQA
    s = jnp.where(first, _nt_dot(q, k_ref[0, 0]), _nt_dot(q, k_ref[0, 1])) * (HEAD_DIM ** -0.5)
    s = s + bias_ref[...]
    ci = lax.broadcasted_iota(jnp.int32, (N_HEADS, ncp), 1)
    mask = (ci * CMP_STRIDE + CMP_BLOCK - 1 <= q_pos) & (ci < n_cmp)
    s = jnp.where(mask, s, NEG)
    m = jnp.max(s, axis=-1, keepdims=True)
    p = jnp.where(mask, jnp.exp(s - m), 0.0)
    p = p / jnp.maximum(jnp.sum(p, axis=-1, keepdims=True), 1e-30)
    pb = p.astype(BF16)
    o_ref[0] = jnp.where(first, jnp.dot(pb, v_ref[0, 0], preferred_element_type=F32),
                         jnp.dot(pb, v_ref[0, 1], preferred_element_type=F32))
    imp0 = jnp.sum(jnp.where(first, p, 0.0), axis=0, keepdims=True)
    imp1 = jnp.sum(jnp.where(first, 0.0, p), axis=0, keepdims=True)
    imp = jnp.where(first, imp0, imp1)
    sb = jnp.dot(imp, pool_ref[...], precision=HIGHEST, preferred_element_type=F32)
    cur = q_pos // SEL_BLOCK
    bi = lax.broadcasted_iota(jnp.int32, (nbp, nbp), 0)
    bj = lax.broadcasted_iota(jnp.int32, (nbp, nbp), 1)
    blk = lax.broadcasted_iota(jnp.int32, (1, nbp), 1)
    causal = blk <= cur
    forced = (blk == 0) | (blk == cur) | (blk == cur - 1)
    rsel = lax.broadcasted_iota(jnp.int32, (N_SEL, nbp), 0)
    for h in range(N_KV_HEADS):
        sc = jnp.where(forced & causal, 1e4, jnp.where(causal, sb[h * GQA:h * GQA + 1, :], -1.0))
        sc = jnp.where(blk < n_blk, sc, -2.0)
        scb = jnp.broadcast_to(sc, (nbp, nbp))
        col = jnp.sum(jnp.where(bi == bj, scb, 0.0), axis=1, keepdims=True)
        ahead = (col > scb) | ((col == scb) & (bi < bj))
        rank = jnp.sum(ahead.astype(jnp.int32), axis=0, keepdims=True)
        hit = jnp.broadcast_to(rank, (N_SEL, nbp)) == rsel
        idx = jnp.sum(jnp.where(hit, jnp.broadcast_to(blk, (N_SEL, nbp)), 0), axis=1, keepdims=True)
        idx_ref[0, h] = jnp.broadcast_to(idx, (N_SEL, LANE))


def _cmp_select_step(q, kc, vc, bias, pool, n_cmp, n_blk, q_pos):
    B = q.shape[0]
    NCp = kc.shape[2]
    return pl.pallas_call(
        functools.partial(_cmp_select_step_kernel, n_cmp=n_cmp, n_blk=n_blk, q_pos=q_pos),
        out_shape=(jax.ShapeDtypeStruct((B, N_HEADS, HEAD_DIM), F32),
                   jax.ShapeDtypeStruct((B, N_KV_HEADS, N_SEL, LANE), jnp.int32)),
        grid=(B,),
        in_specs=[pl.BlockSpec((1, N_HEADS, HEAD_DIM), lambda b: (b, 0, 0)),
                  pl.BlockSpec((1, N_KV_HEADS, NCp, HEAD_DIM), lambda b: (b, 0, 0, 0)),
                  pl.BlockSpec((1, N_KV_HEADS, NCp, HEAD_DIM), lambda b: (b, 0, 0, 0)),
                  pl.BlockSpec(bias.shape, lambda b: (0, 0)),
                  pl.BlockSpec(pool.shape, lambda b: (0, 0))],
        out_specs=(pl.BlockSpec((1, N_HEADS, HEAD_DIM), lambda b: (b, 0, 0)),
                   pl.BlockSpec((1, N_KV_HEADS, N_SEL, LANE), lambda b: (b, 0, 0, 0))),
        compiler_params=_cparams("parallel"),
        name="cmp_select_step",
    )(q, kc, vc, bias, pool)


def _sel_step_kernel(idx_ref, q_ref, kv_ref, new_ref, bias_ref, kpos_ref, o_ref, *, n_past, q_pos):
    b, h = pl.program_id(0), pl.program_id(1)
    base = (b * N_KV_HEADS + h) * N_SEL
    new = new_ref[0]
    segs = [jnp.where(idx_ref[base + j] >= n_past, new, kv_ref[0, 0, j * SEL_BLOCK:(j + 1) * SEL_BLOCK, :])
            for j in range(N_SEL)]
    kv = jnp.concatenate(segs, axis=0)
    hd = HEAD_DIM
    k = jnp.where(h == 0, kv[:, 0:hd], kv[:, hd:2 * hd]).astype(BF16)
    v = jnp.where(h == 0, kv[:, 2 * hd:3 * hd], kv[:, 3 * hd:]).astype(BF16)
    s = _nt_dot(q_ref[0].astype(BF16), k) * (hd ** -0.5) + bias_ref[0, 0]
    mask = kpos_ref[0, 0] <= q_pos
    s = jnp.where(mask, s, NEG)
    m = jnp.max(s, axis=-1, keepdims=True)
    p = jnp.where(mask, jnp.exp(s - m), 0.0)
    l = jnp.sum(p, axis=-1, keepdims=True)
    o_ref[0, 0] = jnp.dot(p.astype(BF16), v, preferred_element_type=F32) / jnp.maximum(l, 1e-30)


def _sel_step(q, kv_sel, new_blocks, bias_sel, kpos, idx_flat, n_past, q_pos):
    B = q.shape[0]
    nk = N_SEL * SEL_BLOCK
    grid_spec = pltpu.PrefetchScalarGridSpec(
        num_scalar_prefetch=1,
        grid=(B, N_KV_HEADS),
        in_specs=[pl.BlockSpec((1, N_HEADS, HEAD_DIM), lambda b, h, ix: (b, 0, 0)),
                  pl.BlockSpec((1, 1, nk, D_KV), lambda b, h, ix: (b, h, 0, 0)),
                  pl.BlockSpec((1, SEL_BLOCK, D_KV), lambda b, h, ix: (b, 0, 0)),
                  pl.BlockSpec((1, 1, N_HEADS, nk), lambda b, h, ix: (b, h, 0, 0)),
                  pl.BlockSpec((1, 1, 1, nk), lambda b, h, ix: (b, h, 0, 0))],
        out_specs=pl.BlockSpec((1, 1, N_HEADS, HEAD_DIM), lambda b, h, ix: (b, h, 0, 0)),
    )
    return pl.pallas_call(
        functools.partial(_sel_step_kernel, n_past=n_past, q_pos=q_pos),
        out_shape=jax.ShapeDtypeStruct((B, N_KV_HEADS, N_HEADS, HEAD_DIM), F32),
        grid_spec=grid_spec,
        compiler_params=_cparams("arbitrary", "arbitrary"),
        name="sel_step",
    )(idx_flat, q, kv_sel, new_blocks, bias_sel, kpos)


def _win_step_kernel(q_ref, w_ref, new_ref, bias_ref, bias0_ref, o_ref):
    q = q_ref[0]
    qb = q.astype(BF16)
    row = lax.broadcasted_iota(jnp.int32, (N_HEADS, 1), 0)
    first = row < GQA
    w = w_ref[0]
    hd = HEAD_DIM
    kb = [w[:, h * hd:(h + 1) * hd].astype(BF16) for h in range(N_KV_HEADS)]
    vb = [w[:, (N_KV_HEADS + h) * hd:(N_KV_HEADS + h + 1) * hd].astype(BF16) for h in range(N_KV_HEADS)]
    s = jnp.where(first, _nt_dot(qb, kb[0]), _nt_dot(qb, kb[1])) * (hd ** -0.5) + bias_ref[...]
    new = new_ref[0]
    kn = jnp.where(first, new[:, 0:hd], new[:, hd:2 * hd])
    vn = jnp.where(first, new[:, 2 * hd:3 * hd], new[:, 3 * hd:])
    sn = jnp.sum(q * kn, axis=-1, keepdims=True) * (hd ** -0.5) + bias0_ref[...]
    m = jnp.maximum(jnp.max(s, axis=-1, keepdims=True), sn)
    p = jnp.exp(s - m)
    pn = jnp.exp(sn - m)
    l = jnp.sum(p, axis=-1, keepdims=True) + pn
    pb = p.astype(BF16)
    acc = jnp.where(first, jnp.dot(pb, vb[0], preferred_element_type=F32),
                    jnp.dot(pb, vb[1], preferred_element_type=F32)) + pn * vn
    o_ref[0] = acc / jnp.maximum(l, 1e-30)


def _win_step(q, win, new, bias, bias0):
    B, W, _ = win.shape
    return pl.pallas_call(
        _win_step_kernel,
        out_shape=jax.ShapeDtypeStruct((B, N_HEADS, HEAD_DIM), F32),
        grid=(B,),
        in_specs=[pl.BlockSpec((1, N_HEADS, HEAD_DIM), lambda b: (b, 0, 0)),
                  pl.BlockSpec((1, W, D_KV), lambda b: (b, 0, 0)),
                  pl.BlockSpec((1, 1, D_KV), lambda b: (b, 0, 0)),
                  pl.BlockSpec((N_HEADS, W), lambda b: (0, 0)),
                  pl.BlockSpec((N_HEADS, 1), lambda b: (0, 0))],
        out_specs=pl.BlockSpec((1, N_HEADS, HEAD_DIM), lambda b: (b, 0, 0)),
        compiler_params=_cparams("parallel"),
        name="win_step",
    )(q, win, new, bias, bias0)


def _split_heads(kv, dtype):
    B, L, _ = kv.shape
    kv5 = kv.reshape(B, L, 2, N_KV_HEADS, HEAD_DIM)
    return (jnp.transpose(kv5[:, :, 0], (0, 2, 1, 3)).astype(dtype),
            jnp.transpose(kv5[:, :, 1], (0, 2, 1, 3)).astype(dtype))


def _nsa_prompt(q, kvc, kvs, kvw, cmp_tab, rel_bias):
    B, T, _ = q.shape
    nc = T // CMP_STRIDE
    nb = T // SEL_BLOCK
    ckv = _compress(kvc.reshape(B, nc, CMP_STRIDE * D_KV), cmp_tab, tr=min(nc, 256))
    kc, vc = _split_heads(ckv, BF16)
    vct = jnp.transpose(vc, (0, 1, 3, 2))
    bias_n = _bias_by_distance(rel_bias, T)
    n_qt, n_kt = T // ATT_TQ, T // ATT_TK
    n_ds = min(n_kt, -(-(REL_MAX_DIST + ATT_TK - 1) // ATT_TK) + 1)
    n_dw = min(n_kt, WINDOW // ATT_TK + 1)
    tzs, tzw, bias_tab = _bias_tables(bias_n, n_qt, nc // 8, n_ds, n_dw, ATT_TQ, ATT_TK)
    pool = jnp.asarray(_pool_matrix(nc, nb))
    scale = HEAD_DIM ** -0.5
    q5 = jnp.transpose((q * scale).reshape(B, T, N_KV_HEADS, GQA, HEAD_DIM), (0, 2, 3, 1, 4))
    o_cmp, sel = _cmp_select_prompt(q5, kc, vct, bias_tab, pool, nc - 1)
    ks, vs = _split_heads(kvs, BF16)
    kw, vw = _split_heads(kvw, BF16)
    o_sel, o_win = _sel_win_prompt(q5, ks, jnp.transpose(vs, (0, 1, 3, 2)), kw, jnp.transpose(vw, (0, 1, 3, 2)),
                                   sel, tzs, tzw)
    return o_cmp, o_sel, o_win


def _nsa_sample(q, kvc, kvs, kvw, pool_cmp, pool_sel, win_buf, page_table, cmp_tab, rel_bias):
    B = q.shape[0]
    n_pages = page_table.shape[1]
    past_len = n_pages * PAGE_SIZE
    q_pos = past_len
    lp = -(-(past_len + 1) // SEL_BLOCK) * SEL_BLOCK
    n_cmp = lp // CMP_STRIDE - 1
    n_blk = lp // SEL_BLOCK
    n_chunks = -(-(n_cmp + 1) // 24) * 24
    past = pool_cmp[page_table].reshape(B, past_len, D_KV)
    full = jnp.concatenate([past, kvc[:, None, :],
                            jnp.zeros((B, n_chunks * CMP_STRIDE - past_len - 1, D_KV), F32)], 1)
    ckv = _compress(full.reshape(B, n_chunks, CMP_STRIDE * D_KV), cmp_tab, tr=n_chunks // 3)
    ncp = -(-n_chunks // LANE) * LANE
    nbp = -(-n_blk // LANE) * LANE
    ckv = jnp.pad(ckv, ((0, 0), (0, ncp - n_chunks), (0, 0)))
    kc, vc = _split_heads(ckv, BF16)
    bias_n = _bias_by_distance(rel_bias, q_pos + 1)
    n_back = max(n_blk * SEL_BLOCK, ncp * CMP_STRIDE + CMP_BLOCK)
    back = jnp.concatenate([bias_n[:, ::-1], jnp.broadcast_to(bias_n[:, :1], (N_HEADS, n_back - q_pos - 1))], 1)
    bias_c = back[:, CMP_BLOCK - 1:CMP_BLOCK - 1 + ncp * CMP_STRIDE:CMP_STRIDE]
    pool = jnp.asarray(_pool_matrix(ncp, nbp).T)
    q3 = q.reshape(B, N_HEADS, HEAD_DIM)
    o_cmp, idx = _cmp_select_step(q3, kc, vc, bias_c, pool, n_cmp, n_blk, q_pos)
    idx = idx[..., 0]
    bpp = PAGE_SIZE // SEL_BLOCK
    n_past = n_pages * bpp
    past_idx = jnp.minimum(idx, n_past - 1)
    page = jnp.take_along_axis(page_table, (past_idx // bpp).reshape(B, -1), axis=1).reshape(idx.shape)
    pool6 = pool_sel.reshape(pool_sel.shape[0], bpp, SEL_BLOCK, 2, N_KV_HEADS, HEAD_DIM)
    kv_sel = pool6[page, past_idx % bpp].reshape(B, N_KV_HEADS, N_SEL * SEL_BLOCK, D_KV)
    new_blocks = jnp.pad(kvs[:, None, :], ((0, 0), (0, SEL_BLOCK - 1), (0, 0)))
    bias_blk = jnp.transpose(back[:, :n_blk * SEL_BLOCK].reshape(N_HEADS, n_blk, SEL_BLOCK), (1, 0, 2))
    bias_sel = jnp.transpose(bias_blk[idx], (0, 1, 3, 2, 4)).reshape(B, N_KV_HEADS, N_HEADS, -1)
    kpos = idx[..., None] * SEL_BLOCK + jnp.arange(SEL_BLOCK)
    kpos = jnp.where((idx <= q_pos // SEL_BLOCK)[..., None], kpos, q_pos + 1)
    kpos = kpos.reshape(B, N_KV_HEADS, 1, -1).astype(jnp.int32)
    o_sel = _sel_step(q3, kv_sel, new_blocks, bias_sel, kpos, idx.reshape(-1).astype(jnp.int32),
                      n_past, q_pos)
    o_sel = jnp.concatenate([o_sel[:, h, h * GQA:(h + 1) * GQA] for h in range(N_KV_HEADS)], axis=1)
    wb = win_buf.shape[1]
    bias_w = bias_n[:, 1:wb + 1][:, ::-1]
    o_win = _win_step(q3, win_buf.reshape(B, wb, D_KV), kvw[:, None, :], bias_w, bias_n[:, 0:1])
    return o_cmp.reshape(B, D_ATT), o_sel.reshape(B, D_ATT), o_win.reshape(B, D_ATT)


def kernel(x_prompt, x_sample, cache_cmp_kv, cache_sel_kv, state_win_kv, state_ssm_re, state_ssm_im, page_table,
           c_prompt, c_sample, w_ada, b_ada, w_in, lam_re, lam_im, log_dt, b_re, b_im, c_re, c_im, d_skip,
           w_glu, b_glu, phi_pe, phi_w1, phi_b1, phi_w2, phi_b2, rel_bias, w_out, ln1_g, ln1_b,
           w_router, b_router, w_gate_up, b_gate_up, w_down, b_down, ln2_g, ln2_b):
    assert w_ada.shape[0] == DEPTH == 1
    l = 0
    Bp, T, D = x_prompt.shape
    Bs = x_sample.shape[0]
    kv_tail = (2, N_KV_HEADS, HEAD_DIM)

    n_c = Bp + Bs
    c_all = jnp.pad(jnp.concatenate([c_prompt, c_sample], 0), ((0, -n_c % 8), (0, 0)))
    m_all = _adaln(c_all, w_ada[l], b_ada[l])
    m_p = m_all[:Bp].reshape(Bp, 6, D)
    m_s = m_all[Bp:n_c].reshape(Bs, 6, D)
    mod_p = [m_p[:, i:i + 1, :] for i in range(6)]
    mod_s = [m_s[None, :, i, :] for i in range(6)]

    w_in_pad = jnp.pad(w_in[l], ((0, 0), (0, D_IN_PAD - D_IN))).astype(BF16)
    n_levels = max(1, int(math.log2(T // SSM_CHUNK)))
    ssm_tab = _ssm_tables(lam_re[l], lam_im[l], log_dt[l], b_re[l], b_im[l], c_re[l], c_im[l],
                          SSM_CHUNK, n_levels)
    cmp_tab = _compress_tables(phi_pe[l], phi_w1[l], phi_b1[l], phi_w2[l], phi_b2[l])
    w_post = dict(
        d_skip=d_skip[l].reshape(1, D_SSM), w_glu=w_glu[l].astype(BF16), b_glu=b_glu[l].reshape(1, D_SSM),
        gexp=jnp.asarray(_gate_expand_matrix()), w_out=w_out[l].astype(BF16),
        ln1_g=ln1_g[l].reshape(1, D), ln1_b=ln1_b[l].reshape(1, D),
        w_router=jnp.pad(w_router[l], ((0, 0), (0, LANE - N_EXPERTS))),
        b_router=jnp.pad(b_router[l], (0, LANE - N_EXPERTS)).reshape(1, LANE))

    u, q, kvc, kvs, kvw, g = _mixer_in(x_prompt, mod_p[0], mod_p[1], w_in_pad, tm=512)
    y_ssm, h_p = _ssm_prompt(u, ssm_tab)
    o_cmp, o_sel, o_win = _nsa_prompt(q, kvc, kvs, kvw, cmp_tab, rel_bias)
    x1_p, hm_p, te_p, tw_p = _post_mixer(y_ssm, u, o_cmp, o_sel, o_win, g, x_prompt,
                                         mod_p[2], mod_p[3], mod_p[4], w_post, tm=256)

    u_s, q_s, kvc_s, kvs_s, kvw_s, g_s = _mixer_in(x_sample.reshape(1, Bs, D), mod_s[0], mod_s[1],
                                                   w_in_pad, tm=Bs)
    y_s, h_s = _ssm_sample(u_s[0], state_ssm_re[l], state_ssm_im[l], ssm_tab, c_re[l], c_im[l])
    oc_s, os_s, ow_s = _nsa_sample(q_s[0].astype(F32), kvc_s[0], kvs_s[0], kvw_s[0], cache_cmp_kv[l],
                                   cache_sel_kv[l], state_win_kv[l], page_table, cmp_tab, rel_bias)
    x1_s, hm_s, te_s, tw_s = _post_mixer(y_s[None], u_s, oc_s[None], os_s[None], ow_s[None], g_s,
                                         x_sample.reshape(1, Bs, D), mod_s[2], mod_s[3], mod_s[4],
                                         w_post, tm=Bs)

    n_p = Bp * T
    n_all = n_p + Bs
    hm_all = jnp.concatenate([hm_p.reshape(n_p, D), hm_s.reshape(Bs, D)], 0)
    te_all = jnp.concatenate([te_p.reshape(n_p, LANE), te_s.reshape(Bs, LANE)], 0)[:, :TOP_K]
    row_tok, dest, blk_e, n_used = _moe_dispatch(te_all, n_all)
    xb = jnp.concatenate([hm_all, jnp.zeros((1, D), F32)], 0)[row_tok]
    yb = _experts(xb, blk_e, n_used, w_gate_up[l], b_gate_up[l], w_down[l], b_down[l])
    ys = [yb[dest[:, k]] for k in range(TOP_K)]
    ln2g, ln2b = ln2_g[l].reshape(1, D), ln2_b[l].reshape(1, D)
    out_p = _final(x1_p, [y[:n_p].reshape(Bp, T, D) for y in ys], tw_p, mod_p[5], ln2g, ln2b, tm=512)
    out_s = _final(x1_s, [y[n_p:].reshape(1, Bs, D) for y in ys], tw_s, mod_s[5], ln2g, ln2b, tm=Bs)

    wlen = min(WINDOW, T)
    win_s = jnp.concatenate([state_win_kv[l], kvw_s[0].reshape(Bs, 1, *kv_tail)], 1)[:, -state_win_kv.shape[2]:]
    p_state = SSM_STATE
    return (out_p, out_s.reshape(Bs, 1, D),
            kvc.reshape(1, Bp, T, *kv_tail), kvc_s[0].reshape(1, Bs, 1, *kv_tail),
            kvs.reshape(1, Bp, T, *kv_tail), kvs_s[0].reshape(1, Bs, 1, *kv_tail),
            kvw[:, T - wlen:].reshape(1, Bp, wlen, *kv_tail), win_s[None],
            h_p[None, ..., :p_state], h_p[None, ..., p_state:],
            h_s[None, ..., :p_state], h_s[None, ..., p_state:])
```

```python
import functools
import math

import numpy as np
import jax
import jax.numpy as jnp
from jax import lax
from jax.experimental import pallas as pl
from jax.experimental.pallas import tpu as pltpu

D_MODEL = 1024
DEPTH = 1
PAST_LEN = 16384
PAGE_SIZE = 128
D_SSM = 512
SSM_GROUP = 16
N_SSM_GROUPS = D_SSM // SSM_GROUP
SSM_STATE = 64
N_HEADS = 8
HEAD_DIM = 64
N_KV_HEADS = 2
GQA = N_HEADS // N_KV_HEADS
D_ATT = N_HEADS * HEAD_DIM
D_KV = 2 * N_KV_HEADS * HEAD_DIM
CMP_STRIDE = 16
CMP_BLOCK = 2 * CMP_STRIDE
SEL_BLOCK = 64
N_SEL = 16
WINDOW = 512
NUM_BUCKETS = 32
REL_MAX_DIST = 1024
N_EXPERTS = 32
TOP_K = 4
D_FF = 1024
SWIGLU_LIMIT = 7.0
SWIGLU_ALPHA = 1.702
DN_ALPHA = (2 * DEPTH) ** 0.25
D_IN = D_SSM + D_ATT + 3 * D_KV + 3 * N_HEADS
NEG = -1e30
F32 = jnp.float32
BF16 = jnp.bfloat16
HIGHEST = lax.Precision.HIGHEST

LANE = 128
D_IN_PAD = 1920
GATE_COL = D_SSM + D_ATT + 3 * D_KV
SSM_CHUNK = 16
ATT_TQ = 128
ATT_TK = 128
MOE_ROWS = 256
VMEM_LIMIT = 48 * 1024 * 1024
LN_EPS = 1e-5


def _cparams(*sem):
    return pltpu.CompilerParams(dimension_semantics=sem, vmem_limit_bytes=VMEM_LIMIT)


def _nt_dot(a, b):
    return lax.dot_general(a, b, (((1,), (1,)), ((), ())), preferred_element_type=F32)


def _layer_norm(x):
    mu = jnp.mean(x, axis=-1, keepdims=True)
    xc = x - mu
    var = jnp.mean(xc * xc, axis=-1, keepdims=True)
    return xc * lax.rsqrt(var + LN_EPS)


def _adaln_kernel(c_ref, w_ref, b_ref, o_ref):
    c = c_ref[...]
    s = c * jax.nn.sigmoid(c)
    o_ref[...] = jnp.dot(s, w_ref[...], precision=HIGHEST, preferred_element_type=F32) + b_ref[...]


def _adaln(c, w, b):
    n, d = c.shape
    dout = w.shape[1]
    tn = 1024
    return pl.pallas_call(
        _adaln_kernel,
        out_shape=jax.ShapeDtypeStruct((n, dout), F32),
        grid=(dout // tn,),
        in_specs=[pl.BlockSpec((n, d), lambda j: (0, 0)),
                  pl.BlockSpec((d, tn), lambda j: (0, j)),
                  pl.BlockSpec((1, tn), lambda j: (0, j))],
        out_specs=pl.BlockSpec((n, tn), lambda j: (0, j)),
        compiler_params=_cparams("arbitrary"),
        name="adaln",
    )(c, w, b.reshape(1, dout))


def _mixer_in_kernel(x_ref, sh_ref, sc_ref, w_ref, u_ref, q_ref, kvc_ref, kvs_ref, kvw_ref, g_ref):
    h = _layer_norm(x_ref[0]) * (1.0 + sc_ref[0]) + sh_ref[0]
    z = jnp.dot(h.astype(BF16), w_ref[...], preferred_element_type=F32)
    c0 = D_SSM
    c1 = c0 + D_ATT
    c2 = c1 + D_KV
    c3 = c2 + D_KV
    c4 = c3 + D_KV
    u_ref[0] = z[:, :c0]
    q_ref[0] = z[:, c0:c1].astype(BF16)
    kvc_ref[0] = z[:, c1:c2]
    kvs_ref[0] = z[:, c2:c3]
    kvw_ref[0] = z[:, c3:c4]
    g_ref[0] = z[:, c4:c4 + LANE]


def _mixer_in(x, shift, scale, w_pad, tm):
    B, T, D = x.shape
    R = shift.shape[1]
    rb = 1 if R == 1 else tm
    mod_map = (lambda b, i: (b, 0, 0)) if R == 1 else (lambda b, i: (b, i, 0))
    row = lambda n: pl.BlockSpec((1, tm, n), lambda b, i: (b, i, 0))
    outs = (jax.ShapeDtypeStruct((B, T, D_SSM), F32), jax.ShapeDtypeStruct((B, T, D_ATT), BF16),
            jax.ShapeDtypeStruct((B, T, D_KV), F32), jax.ShapeDtypeStruct((B, T, D_KV), F32),
            jax.ShapeDtypeStruct((B, T, D_KV), F32), jax.ShapeDtypeStruct((B, T, LANE), F32))
    return pl.pallas_call(
        _mixer_in_kernel,
        out_shape=outs,
        grid=(B, T // tm),
        in_specs=[row(D), pl.BlockSpec((1, rb, D), mod_map), pl.BlockSpec((1, rb, D), mod_map),
                  pl.BlockSpec((D, D_IN_PAD), lambda b, i: (0, 0))],
        out_specs=(row(D_SSM), row(D_ATT), row(D_KV), row(D_KV), row(D_KV), row(LANE)),
        compiler_params=_cparams("parallel", "parallel"),
        name="mixer_in",
    )(x, shift, scale, w_pad)


def _ssm_tables(lam_re, lam_im, log_dt, b_re, b_im, c_re, c_im, L, n_levels):
    G, P = lam_re.shape
    C = b_re.shape[-1]
    dt = jnp.exp(log_dt.astype(F32))[:, None]
    er, ei = lam_re * dt, lam_im * dt

    def power(k):
        kk = k.astype(F32)[:, None, None]
        mag = jnp.exp(kk * er)
        return mag * jnp.cos(kk * ei), mag * jnp.sin(kk * ei)

    lb_re, lb_im = power(jnp.ones((1,), F32))
    nr, ni = lb_re[0] - 1.0, lb_im[0]
    den = lam_re * lam_re + lam_im * lam_im
    fr = (nr * lam_re + ni * lam_im) / den
    fi = (ni * lam_re - nr * lam_im) / den
    bbr = fr[:, :, None] * b_re - fi[:, :, None] * b_im
    bbi = fr[:, :, None] * b_im + fi[:, :, None] * b_re
    pr, pi = power(jnp.arange(L + 1))
    clr = c_re[None] * pr[:, :, None, :] - c_im[None] * pi[:, :, None, :]
    cli = c_re[None] * pi[:, :, None, :] + c_im[None] * pr[:, :, None, :]
    kern = (jnp.einsum('kgcp,gpd->kgcd', clr[:L], bbr, precision=HIGHEST)
            - jnp.einsum('kgcp,gpd->kgcd', cli[:L], bbi, precision=HIGHEST))
    kz = jnp.concatenate([kern, jnp.zeros((1,) + kern.shape[1:], F32)], 0)
    ts = np.arange(L)
    lag = ts[None, :] - ts[:, None]
    lag = np.where(lag >= 0, lag, L)
    toep = kz[lag]
    toep = jnp.transpose(toep, (2, 0, 4, 1, 3)).reshape(G, L * C, L * C)
    rev = L - 1 - ts
    wsr = pr[rev][:, :, :, None] * bbr[None] - pi[rev][:, :, :, None] * bbi[None]
    wsi = pr[rev][:, :, :, None] * bbi[None] + pi[rev][:, :, :, None] * bbr[None]
    ws = jnp.concatenate([jnp.transpose(wsr, (1, 0, 3, 2)), jnp.transpose(wsi, (1, 0, 3, 2))], -1)
    ws = ws.reshape(G, L * C, 2 * P)
    wy = jnp.concatenate([jnp.transpose(clr[1:], (1, 3, 0, 2)), -jnp.transpose(cli[1:], (1, 3, 0, 2))], 1)
    wy = wy.reshape(G, 2 * P, L * C)
    lr, li = power(L * (2 ** jnp.arange(n_levels)))
    ar = jnp.transpose(jnp.concatenate([lr, lr], -1), (1, 0, 2))
    ai = jnp.transpose(jnp.concatenate([-li, li], -1), (1, 0, 2))
    return toep.astype(BF16), ws.astype(BF16), wy.astype(BF16), ar, ai, (lb_re[0], lb_im[0], bbr, bbi)


def _ssm_kernel(u_ref, toep_ref, ws_ref, wy_ref, ar_ref, ai_ref, y_ref, hl_ref, *, nb, nc, n_levels):
    u = u_ref[0]
    y1 = jnp.dot(u, toep_ref[0], preferred_element_type=F32)
    s = jnp.dot(u, ws_ref[0], preferred_element_type=F32)
    p2 = s.shape[-1]
    rows = lax.broadcasted_iota(jnp.int32, (nc, p2), 0)
    prev = []
    for b in range(nb):
        h = s[b * nc:(b + 1) * nc]
        for k in range(n_levels):
            d = 1 << k
            sh = jnp.where(rows >= d, pltpu.roll(h, d, axis=0), 0.0)
            sw = pltpu.roll(sh, p2 // 2, axis=1)
            h = h + ar_ref[0, k:k + 1, :] * sh + ai_ref[0, k:k + 1, :] * sw
        hl_ref[0, b:b + 1, :] = h[nc - 1:nc, :]
        prev.append(jnp.where(rows >= 1, pltpu.roll(h, 1, axis=0), 0.0))
    hp = jnp.concatenate(prev, axis=0)
    y2 = jnp.dot(hp.astype(BF16), wy_ref[0], preferred_element_type=F32)
    y_ref[0] = y1 + y2


def _ssm_prompt(u, tables):
    toep, ws, wy, ar, ai, _ = tables
    B, T, _ = u.shape
    G, C, L = N_SSM_GROUPS, SSM_GROUP, SSM_CHUNK
    nc = T // L
    n_levels = ar.shape[1]
    ug = jnp.transpose(u.reshape(B, nc, L, G, C), (3, 0, 1, 2, 4)).reshape(G, B * nc, L * C).astype(BF16)
    grp = lambda r, c: pl.BlockSpec((1, r, c), lambda g: (g, 0, 0))
    y, hl = pl.pallas_call(
        functools.partial(_ssm_kernel, nb=B, nc=nc, n_levels=n_levels),
        out_shape=(jax.ShapeDtypeStruct((G, B * nc, L * C), F32),
                   jax.ShapeDtypeStruct((G, B, 2 * SSM_STATE), F32)),
        grid=(G,),
        in_specs=[grp(B * nc, L * C), grp(L * C, L * C), grp(L * C, 2 * SSM_STATE),
                  grp(2 * SSM_STATE, L * C), grp(n_levels, 2 * SSM_STATE), grp(n_levels, 2 * SSM_STATE)],
        out_specs=(grp(B * nc, L * C), grp(B, 2 * SSM_STATE)),
        compiler_params=_cparams("parallel"),
        name="ssm_prompt",
    )(ug, toep, ws, wy, ar, ai)
    y = jnp.transpose(y.reshape(G, B, nc, L, C), (1, 2, 3, 0, 4)).reshape(B, T, D_SSM)
    return y, jnp.transpose(hl, (1, 0, 2))


def _ssm_step_kernel(u_ref, h0_ref, bb_ref, lr_ref, li_ref, cy_ref, y_ref, h_ref):
    p = lr_ref.shape[-1] // 2
    bu = jnp.einsum('gbc,gcp->gbp', u_ref[...], bb_ref[...], preferred_element_type=F32)
    h0 = h0_ref[...]
    h0s = jnp.concatenate([h0[..., p:], h0[..., :p]], axis=-1)
    h = lr_ref[...] * h0 + li_ref[...] * h0s + bu
    h_ref[...] = h
    y_ref[...] = jnp.einsum('gbp,gpc->gbc', h.astype(BF16), cy_ref[...], preferred_element_type=F32)


def _ssm_sample(u, h0_re, h0_im, tables, c_re, c_im):
    lb_re, lb_im, bbr, bbi = tables[-1]
    B = u.shape[0]
    G, C, P = N_SSM_GROUPS, SSM_GROUP, SSM_STATE
    ug = jnp.transpose(u.reshape(B, G, C), (1, 0, 2)).astype(BF16)
    h0 = jnp.transpose(jnp.concatenate([h0_re, h0_im], -1), (1, 0, 2)).astype(F32)
    bb = jnp.concatenate([jnp.transpose(bbr, (0, 2, 1)), jnp.transpose(bbi, (0, 2, 1))], -1).astype(BF16)
    lr = jnp.concatenate([lb_re, lb_re], -1)[:, None, :]
    li = jnp.concatenate([-lb_im, lb_im], -1)[:, None, :]
    cy = jnp.concatenate([jnp.transpose(c_re, (0, 2, 1)), -jnp.transpose(c_im, (0, 2, 1))], 1).astype(BF16)
    y, h = pl.pallas_call(
        _ssm_step_kernel,
        out_shape=(jax.ShapeDtypeStruct((G, B, C), F32), jax.ShapeDtypeStruct((G, B, 2 * P), F32)),
        name="ssm_step",
    )(ug, h0, bb, lr, li, cy)
    return jnp.transpose(y, (1, 0, 2)).reshape(B, D_SSM), jnp.transpose(h, (1, 0, 2))


def _compress_tables(phi_pe, phi_w1, phi_b1, phi_w2, phi_b2):
    S, H, Dh = CMP_STRIDE, N_KV_HEADS, HEAD_DIM
    w1 = phi_w1.reshape(2, 2, S, Dh, Dh)
    eye_c = jnp.eye(2, dtype=F32)
    eye_h = jnp.eye(H, dtype=F32)
    wbig = jnp.einsum('cajde,xc,yh->jxydache', w1, eye_c, eye_h).reshape(S * 2 * H * Dh, 2 * 2 * H * Dh)
    pe = jnp.transpose(phi_pe.reshape(2, 2, S, Dh), (1, 2, 0, 3))
    pe_rows = jnp.broadcast_to(pe[:, :, :, None, :], (2, S, 2, H, Dh)).reshape(2, 1, S * 2 * H * Dh)
    b1 = jnp.broadcast_to(phi_b1[:, None, :], (2, H, Dh)).reshape(1, 2 * H * Dh)
    w2 = jnp.einsum('cef,cx,hy->chexyf', phi_w2, eye_c, eye_h).reshape(2 * H * Dh, 2 * H * Dh)
    b2 = jnp.broadcast_to(phi_b2[:, None, :], (2, H, Dh)).reshape(1, 2 * H * Dh)
    return wbig.astype(BF16), pe_rows, b1, w2.astype(BF16), b2


def _compress_in_kernel(x_ref, pe_ref, w_ref, z_ref):
    x = x_ref[0]
    n = w_ref.shape[1] // 2
    z_ref[0, :, :n] = jnp.dot((x + pe_ref[0]).astype(BF16), w_ref[:, :n], preferred_element_type=F32)
    z_ref[0, :, n:] = jnp.dot((x + pe_ref[1]).astype(BF16), w_ref[:, n:], preferred_element_type=F32)


def _compress_out_kernel(z_ref, b1_ref, w2_ref, b2_ref, o_ref):
    z = z_ref[0]
    n = z.shape[-1] // 2
    rows = z.shape[0]
    second = pltpu.roll(z[:, n:], rows - 1, axis=0)
    hdn = jax.nn.gelu(z[:, :n] + second + b1_ref[...])
    o_ref[0] = jnp.dot(hdn.astype(BF16), w2_ref[...], preferred_element_type=F32) + b2_ref[...]


def _compress(parts, tables):
    wbig, pe_rows, b1, w2, b2 = tables
    N2 = wbig.shape[1]
    zs = []
    for x2 in parts:
        B, n, K = x2.shape
        tr = math.gcd(n, 256)
        zs.append(pl.pallas_call(
            _compress_in_kernel,
            out_shape=jax.ShapeDtypeStruct((B, n, N2), F32),
            grid=(B, n // tr),
            in_specs=[pl.BlockSpec((1, tr, K), lambda b, i: (b, i, 0)),
                      pl.BlockSpec((2, 1, K), lambda b, i: (0, 0, 0)),
                      pl.BlockSpec((K, N2), lambda b, i: (0, 0))],
            out_specs=pl.BlockSpec((1, tr, N2), lambda b, i: (b, i, 0)),
            compiler_params=_cparams("parallel", "parallel"),
            name="compress_in",
        )(x2, pe_rows, wbig))
    z = zs[0] if len(zs) == 1 else jnp.concatenate(zs, axis=1)
    B, n, _ = z.shape
    return pl.pallas_call(
        _compress_out_kernel,
        out_shape=jax.ShapeDtypeStruct((B, n, N2 // 2), F32),
        grid=(B,),
        in_specs=[pl.BlockSpec((1, n, N2), lambda b: (b, 0, 0)),
                  pl.BlockSpec((1, N2 // 2), lambda b: (0, 0)),
                  pl.BlockSpec((N2 // 2, N2 // 2), lambda b: (0, 0)),
                  pl.BlockSpec((1, N2 // 2), lambda b: (0, 0))],
        out_specs=pl.BlockSpec((1, n, N2 // 2), lambda b: (b, 0, 0)),
        compiler_params=_cparams("parallel"),
        name="compress_out",
    )(z, b1, w2, b2)


def _rel_bucket(dist):
    n = jnp.maximum(dist, 0)
    max_exact = NUM_BUCKETS // 2
    nf = jnp.maximum(n, 1).astype(F32)
    large = max_exact + (jnp.log(nf / max_exact) / math.log(REL_MAX_DIST / max_exact)
                         * (NUM_BUCKETS - max_exact)).astype(jnp.int32)
    large = jnp.minimum(large, NUM_BUCKETS - 1)
    return jnp.where(n < max_exact, n, large)


def _bias_by_distance(rel_bias, n_max):
    onehot = (_rel_bucket(jnp.arange(n_max))[None, :] == jnp.arange(NUM_BUCKETS)[:, None]).astype(F32)
    return jnp.dot(jnp.transpose(rel_bias.astype(F32)), onehot, precision=HIGHEST)


def _shifted_chunks(bias_n, pad, n_chunks, width):
    n = min(bias_n.shape[1], n_chunks * width - pad)
    ext = jnp.concatenate([jnp.broadcast_to(bias_n[:, :1], (N_HEADS, pad)), bias_n[:, :n],
                           jnp.zeros((N_HEADS, n_chunks * width - pad - n), F32)], axis=1)
    return ext.reshape(N_HEADS, n_chunks, width)


def _bias_tables_kernel(ed_ref, ec_ref, tzs_ref, tzw_ref, cmp_ref, *, tq, tk, n_qt):
    n_ds, n_dw, n_j = tzs_ref.shape[1], tzw_ref.shape[1], cmp_ref.shape[1] // 8
    w = tq + tk
    c = lax.broadcasted_iota(jnp.int32, (tk, tq), 0)
    r = lax.broadcasted_iota(jnp.int32, (tk, tq), 1)
    for d in range(n_ds):
        v = jnp.concatenate([ed_ref[0, d:d + 1, :], ed_ref[0, d + 1:d + 2, :]], axis=1)
        t = pltpu.roll(jnp.broadcast_to(v, (tk, w)), w - (tk - 1), axis=1, stride=1, stride_axis=0)[:, :tq]
        dist = d * tk + r - c
        tzs_ref[0, d] = jnp.where(dist >= 0, t, NEG)
        if d < n_dw:
            tzw_ref[0, d] = jnp.where((dist >= 0) & (dist <= WINDOW), t, NEG)
    for j in range(n_j):
        dd = n_qt - 1 - j
        c0, c1 = max(dd, 0), max(dd + 1, 0)
        v = jnp.concatenate([ec_ref[0, c0:c0 + 1, :], ec_ref[0, c1:c1 + 1, :]], axis=1)
        t = pltpu.roll(jnp.broadcast_to(v, (8, w)), w - 7 * CMP_STRIDE, axis=1, stride=CMP_STRIDE, stride_axis=0)
        cmp_ref[0, j * 8:(j + 1) * 8, :] = t[:, :tq]


def _bias_tables(bias_n, n_qt, n_rb, n_ds, n_dw, tq, tk):
    assert tq == tk == 8 * CMP_STRIDE and n_dw <= n_ds
    n_j = n_rb + n_qt - 1
    ed = _shifted_chunks(bias_n, tk - 1, n_ds + 1, tq)
    ec = _shifted_chunks(bias_n, 7 * CMP_STRIDE + CMP_BLOCK - 1, n_qt + 1, tq)
    head = lambda a: pl.BlockSpec((1,) + a.shape[1:], lambda h: (h,) + (0,) * (a.ndim - 1))
    outs = (jax.ShapeDtypeStruct((N_HEADS, n_ds, tk, tq), F32), jax.ShapeDtypeStruct((N_HEADS, n_dw, tk, tq), F32),
            jax.ShapeDtypeStruct((N_HEADS, n_j * 8, tq), F32))
    tzs, tzw, cmp = pl.pallas_call(
        functools.partial(_bias_tables_kernel, tq=tq, tk=tk, n_qt=n_qt),
        out_shape=outs,
        grid=(N_HEADS,),
        in_specs=[head(ed), head(ec)],
        out_specs=tuple(head(o) for o in outs),
        compiler_params=_cparams("parallel"),
        name="bias_tables",
    )(ed, ec)
    grp = lambda a: a.reshape((N_KV_HEADS, GQA) + a.shape[1:])
    return grp(tzs), grp(tzw), cmp


def _pool_matrix(n_cmp_pad, n_blk_pad):
    r = SEL_BLOCK // CMP_STRIDE
    i = np.arange(n_cmp_pad)[None, :]
    j = np.arange(n_blk_pad)[:, None]
    return ((i >= r * j - 1) & (i <= r * j + r - 1)).astype(np.float32)


def _cmp_select_kernel(q_ref, k_ref, vt_ref, bias_ref, pool_ref, o_ref, sel_ref, *, tq, n_cmp):
    qt = pl.program_id(2)
    n_qt = pl.num_programs(2)
    q = q_ref[0, 0].reshape(GQA * tq, HEAD_DIM)
    k = k_ref[0, 0]
    nc = k.shape[0]
    s = _nt_dot(k, q)
    row0 = pl.multiple_of((n_qt - 1 - qt) * 8, 8)
    s = s + jnp.concatenate([bias_ref[g, pl.ds(row0, nc), :] for g in range(GQA)], axis=-1)
    t_pos = qt * tq + (lax.broadcasted_iota(jnp.int32, (nc, GQA * tq), 1) % tq)
    ci = lax.broadcasted_iota(jnp.int32, (nc, GQA * tq), 0)
    mask = (ci * CMP_STRIDE + CMP_BLOCK - 1 <= t_pos) & (ci < n_cmp)
    s = jnp.where(mask, s, NEG)
    m = jnp.max(s, axis=0, keepdims=True)
    p = jnp.where(mask, jnp.exp(s - m), 0.0)
    p = p / jnp.maximum(jnp.sum(p, axis=0, keepdims=True), 1e-30)
    ot = jnp.dot(vt_ref[0, 0], p.astype(BF16), preferred_element_type=F32)
    o_ref[0] = jnp.concatenate([ot[:, g * tq:(g + 1) * tq].T for g in range(GQA)], axis=-1)
    imp = p[:, 0:tq]
    for g in range(1, GQA):
        imp = imp + p[:, g * tq:(g + 1) * tq]
    sb = jnp.dot(pool_ref[...], imp, precision=HIGHEST, preferred_element_type=F32)
    nb = sb.shape[0]
    blk = lax.broadcasted_iota(jnp.int32, (nb, tq), 0)
    cur = (qt * tq + lax.broadcasted_iota(jnp.int32, (nb, tq), 1)) // SEL_BLOCK
    causal = blk <= cur
    forced = (blk == 0) | (blk == cur) | (blk == cur - 1)
    sc = jnp.where(forced & causal, 1e4, jnp.where(causal, sb, -1.0))
    rank = jnp.zeros((nb, tq), jnp.int32)
    for i in range(nb):
        row = sc[i:i + 1, :]
        ahead = (row > sc) | ((row == sc) & (blk > i))
        rank = rank + ahead.astype(jnp.int32)
    sel_ref[0, 0] = jnp.where((rank < N_SEL) & causal, 0.0, NEG)


def _cmp_select_prompt(q5, kc, vct, bias_tab, pool, n_cmp):
    B, _, _, T, _ = q5.shape
    NC = kc.shape[2]
    NB = pool.shape[0]
    R = bias_tab.shape[1]
    tq = ATT_TQ
    return pl.pallas_call(
        functools.partial(_cmp_select_kernel, tq=tq, n_cmp=n_cmp),
        out_shape=(jax.ShapeDtypeStruct((B, T, D_ATT), F32),
                   jax.ShapeDtypeStruct((B, N_KV_HEADS, NB, T), F32)),
        grid=(B, N_KV_HEADS, T // tq),
        in_specs=[pl.BlockSpec((1, 1, GQA, tq, HEAD_DIM), lambda b, h, i: (b, h, 0, i, 0)),
                  pl.BlockSpec((1, 1, NC, HEAD_DIM), lambda b, h, i: (b, h, 0, 0)),
                  pl.BlockSpec((1, 1, HEAD_DIM, NC), lambda b, h, i: (b, h, 0, 0)),
                  pl.BlockSpec((GQA, R, tq), lambda b, h, i: (h, 0, 0)),
                  pl.BlockSpec((NB, NC), lambda b, h, i: (0, 0))],
        out_specs=(pl.BlockSpec((1, tq, GQA * HEAD_DIM), lambda b, h, i: (b, i, h)),
                   pl.BlockSpec((1, 1, NB, tq), lambda b, h, i: (b, h, 0, i))),
        compiler_params=_cparams("parallel", "parallel", "parallel"),
        name="cmp_select_prompt",
    )(q5, kc, vct, bias_tab, pool)


def _sel_win_kernel(q_ref, ks_ref, vst_ref, kw_ref, vwt_ref, sel_ref, tzs_ref, tzw_ref, os_ref, ow_ref, *, tq):
    tk = ATT_TK
    qt = pl.program_id(2)
    q = q_ref[0, 0].reshape(GQA * tq, HEAD_DIM)
    n_ds = tzs_ref.shape[2]
    n_dw = tzw_ref.shape[2]
    per_tile = tk // SEL_BLOCK

    def make_step(k_ref, vt_ref, tz_ref, n_d, use_sel):
        def step(kt, carry):
            m, l, acc = carry
            off = pl.multiple_of(kt * tk, tk)
            k = k_ref[0, 0, pl.ds(off, tk), :]
            vt = vt_ref[0, 0, :, pl.ds(off, tk)]
            d = jnp.minimum(qt - kt, n_d - 1)
            bias = [tz_ref[0, g, d] for g in range(GQA)]
            if use_sel:
                rows = sel_ref[0, 0, pl.ds(kt * per_tile, per_tile), :]
                selb = jnp.concatenate([jnp.broadcast_to(rows[i:i + 1], (SEL_BLOCK, tq))
                                        for i in range(per_tile)], axis=0)
                bias = [b + selb for b in bias]
            s = _nt_dot(k, q) + jnp.concatenate(bias, axis=1)
            m_new = jnp.maximum(m, jnp.max(s, axis=0, keepdims=True))
            alpha = jnp.exp(m - m_new)
            p = jnp.exp(s - m_new)
            l = alpha * l + jnp.sum(p, axis=0, keepdims=True)
            acc = alpha * acc + jnp.dot(vt, p.astype(BF16), preferred_element_type=F32)
            return m_new, l, acc
        return step

    def init():
        return (jnp.full((1, GQA * tq), 0.5 * NEG, F32), jnp.zeros((1, GQA * tq), F32),
                jnp.zeros((HEAD_DIM, GQA * tq), F32))

    def sweep(step, lo, hi):
        n = hi - lo + 1

        def pair(i, carry):
            ca, cb = carry
            kt = lo + 2 * i
            return step(kt, ca), step(kt + 1, cb)

        ca, cb = lax.fori_loop(0, n // 2, pair, (init(), init()))
        ca = lax.cond(n % 2 == 1, lambda c: step(hi, c), lambda c: c, ca)
        (ma, la, acca), (mb, lb, accb) = ca, cb
        m = jnp.maximum(ma, mb)
        ea, eb = jnp.exp(ma - m), jnp.exp(mb - m)
        o = (acca * ea + accb * eb) / jnp.maximum(la * ea + lb * eb, 1e-30)
        return jnp.concatenate([o[:, g * tq:(g + 1) * tq].T for g in range(GQA)], axis=-1)

    os_ref[0] = sweep(make_step(ks_ref, vst_ref, tzs_ref, n_ds, True), 0, qt)
    ow_ref[0] = sweep(make_step(kw_ref, vwt_ref, tzw_ref, n_dw, False), jnp.maximum(qt - (n_dw - 1), 0), qt)


def _sel_win_prompt(q5, ks, vst, kw, vwt, sel, tzs, tzw):
    B, _, _, T, _ = q5.shape
    NB = sel.shape[2]
    tq = ATT_TQ
    k_spec = pl.BlockSpec((1, 1, T, HEAD_DIM), lambda b, h, i: (b, h, 0, 0))
    vt_spec = pl.BlockSpec((1, 1, HEAD_DIM, T), lambda b, h, i: (b, h, 0, 0))
    tz_spec = lambda tz: pl.BlockSpec((1,) + tz.shape[1:], lambda b, h, i: (h, 0, 0, 0, 0))
    o_spec = pl.BlockSpec((1, tq, GQA * HEAD_DIM), lambda b, h, i: (b, i, h))
    return pl.pallas_call(
        functools.partial(_sel_win_kernel, tq=tq),
        out_shape=(jax.ShapeDtypeStruct((B, T, D_ATT), F32), jax.ShapeDtypeStruct((B, T, D_ATT), F32)),
        grid=(B, N_KV_HEADS, T // tq),
        in_specs=[pl.BlockSpec((1, 1, GQA, tq, HEAD_DIM), lambda b, h, i: (b, h, 0, i, 0)),
                  k_spec, vt_spec, k_spec, vt_spec,
                  pl.BlockSpec((1, 1, NB, tq), lambda b, h, i: (b, h, 0, i)),
                  tz_spec(tzs), tz_spec(tzw)],
        out_specs=(o_spec, o_spec),
        compiler_params=_cparams("parallel", "parallel", "parallel"),
        name="sel_win_prompt",
    )(q5, ks, vst, kw, vwt, sel, tzs, tzw)


def _gate_expand_matrix():
    m = np.zeros((3, LANE, D_ATT), np.float32)
    for r in range(3):
        for h in range(N_HEADS):
            m[r, h * 3 + r, h * HEAD_DIM:(h + 1) * HEAD_DIM] = 1.0
    return m


def _post_mixer_kernel(y_ref, u_ref, oc_ref, os_ref, ow_ref, g_ref, x_ref, gate_ref, sh_ref, sc_ref,
                       dskip_ref, wglu_ref, bglu_ref, gexp_ref, wout_ref, lng_ref, lnb_ref,
                       wr_ref, br_ref, x1_ref, hm_ref, te_ref, tw_ref):
    y = y_ref[0] + dskip_ref[...] * u_ref[0]
    gl = jax.nn.gelu(y)
    ssm = gl * jax.nn.sigmoid(jnp.dot(gl.astype(BF16), wglu_ref[...], preferred_element_type=F32)
                              + bglu_ref[...])
    sg = jax.nn.sigmoid(g_ref[0])
    att = jnp.zeros_like(oc_ref[0])
    for r, o_ref in enumerate((oc_ref, os_ref, ow_ref)):
        att = att + jnp.dot(sg, gexp_ref[r], precision=HIGHEST, preferred_element_type=F32) * o_ref[0]
    h = (jnp.dot(ssm.astype(BF16), wout_ref[:D_SSM, :], preferred_element_type=F32)
         + jnp.dot(att.astype(BF16), wout_ref[D_SSM:, :], preferred_element_type=F32))
    z = DN_ALPHA * x_ref[0] + gate_ref[0] * h
    x1 = _layer_norm(z) * lng_ref[...] + lnb_ref[...]
    x1_ref[0] = x1
    hm = _layer_norm(x1) * (1.0 + sc_ref[0]) + sh_ref[0]
    hm_ref[0] = hm
    logits = jnp.dot(hm, wr_ref[...], precision=HIGHEST, preferred_element_type=F32) + br_ref[...]
    lane = lax.broadcasted_iota(jnp.int32, logits.shape, 1)
    work = jnp.where(lane < N_EXPERTS, logits, -jnp.inf)
    te = jnp.zeros(logits.shape, jnp.int32)
    tv = jnp.zeros(logits.shape, F32)
    for k in range(TOP_K):
        best = jnp.max(work, axis=-1, keepdims=True)
        arg = jnp.min(jnp.where(work == best, lane, LANE), axis=-1, keepdims=True)
        te = jnp.where(lane == k, arg, te)
        tv = jnp.where(lane == k, best, tv)
        work = jnp.where(lane == arg, -jnp.inf, work)
    ex = jnp.where(lane < TOP_K, jnp.exp(tv - tv[:, 0:1]), 0.0)
    te_ref[0] = te
    tw_ref[0] = ex / jnp.sum(ex, axis=-1, keepdims=True)


def _post_mixer(y, u, oc, osel, ow, g, x, gate, shift, scale, w, tm):
    B, T, D = x.shape
    R = gate.shape[1]
    rb = 1 if R == 1 else tm
    mod_map = (lambda b, i: (b, 0, 0)) if R == 1 else (lambda b, i: (b, i, 0))
    row = lambda n: pl.BlockSpec((1, tm, n), lambda b, i: (b, i, 0))
    mod = pl.BlockSpec((1, rb, D), mod_map)
    full = lambda a: pl.BlockSpec(a.shape, lambda b, i: (0,) * a.ndim)
    consts = (w['d_skip'], w['w_glu'], w['b_glu'], w['gexp'], w['w_out'], w['ln1_g'], w['ln1_b'],
              w['w_router'], w['b_router'])
    return pl.pallas_call(
        _post_mixer_kernel,
        out_shape=(jax.ShapeDtypeStruct((B, T, D), F32), jax.ShapeDtypeStruct((B, T, D), F32),
                   jax.ShapeDtypeStruct((B, T, LANE), jnp.int32), jax.ShapeDtypeStruct((B, T, LANE), F32)),
        grid=(B, T // tm),
        in_specs=[row(D_SSM), row(D_SSM), row(D_ATT), row(D_ATT), row(D_ATT), row(LANE), row(D),
                  mod, mod, mod] + [full(a) for a in consts],
        out_specs=(row(D), row(D), row(LANE), row(LANE)),
        compiler_params=_cparams("parallel", "parallel"),
        name="post_mixer",
    )(y, u, oc, osel, ow, g, x, gate, shift, scale, *consts)


def _expert_kernel(be_ref, nu_ref, x_ref, wgu_ref, bgu_ref, wd_ref, bd_ref, o_ref, wgu_s, wd_s):
    i = pl.program_id(0)
    prev = be_ref[jnp.maximum(i - 1, 0)]
    fresh = (i == 0) | (be_ref[i] != prev)

    @pl.when(fresh)
    def _():
        wgu_s[...] = wgu_ref[0].astype(BF16)
        wd_s[...] = wd_ref[0].astype(BF16)

    @pl.when(i < nu_ref[0])
    def _():
        gu = jnp.dot(x_ref[...].astype(BF16), wgu_s[...], preferred_element_type=F32) + bgu_ref[0]
        gate = jnp.minimum(gu[:, :D_FF], SWIGLU_LIMIT)
        up = jnp.clip(gu[:, D_FF:], -SWIGLU_LIMIT, SWIGLU_LIMIT)
        hh = (up + 1.0) * gate * jax.nn.sigmoid(SWIGLU_ALPHA * gate)
        o_ref[...] = jnp.dot(hh.astype(BF16), wd_s[...], preferred_element_type=F32) + bd_ref[0]

    @pl.when(i >= nu_ref[0])
    def _():
        o_ref[...] = jnp.zeros_like(o_ref)


def _experts(xb, blk_e, n_used, w_gate_up, b_gate_up, w_down, b_down):
    rows, D = xb.shape
    n_blk = rows // MOE_ROWS
    grid_spec = pltpu.PrefetchScalarGridSpec(
        num_scalar_prefetch=2,
        grid=(n_blk,),
        in_specs=[pl.BlockSpec((MOE_ROWS, D), lambda i, be, nu: (i, 0)),
                  pl.BlockSpec((1, D, 2 * D_FF), lambda i, be, nu: (be[i], 0, 0)),
                  pl.BlockSpec((1, 1, 2 * D_FF), lambda i, be, nu: (be[i], 0, 0)),
                  pl.BlockSpec((1, D_FF, D), lambda i, be, nu: (be[i], 0, 0)),
                  pl.BlockSpec((1, 1, D), lambda i, be, nu: (be[i], 0, 0))],
        out_specs=pl.BlockSpec((MOE_ROWS, D), lambda i, be, nu: (i, 0)),
        scratch_shapes=[pltpu.VMEM((D, 2 * D_FF), BF16), pltpu.VMEM((D_FF, D), BF16)],
    )
    return pl.pallas_call(
        _expert_kernel,
        out_shape=jax.ShapeDtypeStruct((rows, D), F32),
        grid_spec=grid_spec,
        compiler_params=_cparams("arbitrary"),
        name="moe_experts",
    )(blk_e, n_used, xb, w_gate_up, b_gate_up.reshape(N_EXPERTS, 1, 2 * D_FF), w_down,
      b_down.reshape(N_EXPERTS, 1, D))


def _moe_dispatch(top_e, n):
    blk = MOE_ROWS
    nk = n * TOP_K
    e = top_e.reshape(-1)
    order = jnp.argsort(e)
    e_s = e[order]
    counts = jnp.bincount(e, length=N_EXPERTS)
    pcounts = (counts + blk - 1) // blk * blk
    start = jnp.cumsum(counts) - counts
    pend = jnp.cumsum(pcounts)
    pstart = pend - pcounts
    dest_sorted = (pstart[e_s] + jnp.arange(nk) - start[e_s]).astype(jnp.int32)
    n_blk = (nk + N_EXPERTS * (blk - 1)) // blk
    rows = n_blk * blk
    row_tok = jnp.full((rows,), n, jnp.int32).at[dest_sorted].set((order // TOP_K).astype(jnp.int32))
    dest = jnp.zeros((nk,), jnp.int32).at[order].set(dest_sorted)
    blk_e = jnp.sum(pend[None, :] <= (jnp.arange(n_blk) * blk)[:, None], axis=1)
    blk_e = jnp.minimum(blk_e, N_EXPERTS - 1).astype(jnp.int32)
    n_used = (pend[-1] // blk).astype(jnp.int32).reshape(1)
    return row_tok, dest.reshape(n, TOP_K), blk_e, n_used


def _final_kernel(x_ref, y0_ref, y1_ref, y2_ref, y3_ref, tw_ref, gate_ref, lng_ref, lnb_ref, o_ref):
    tw = tw_ref[0]
    y = jnp.zeros_like(x_ref[0])
    for k, y_ref in enumerate((y0_ref, y1_ref, y2_ref, y3_ref)):
        y = y + tw[:, k:k + 1] * y_ref[0]
    z = DN_ALPHA * x_ref[0] + gate_ref[0] * y
    o_ref[0] = _layer_norm(z) * lng_ref[...] + lnb_ref[...]


def _final(x1, ys, tw, gate, ln_g, ln_b, tm):
    B, T, D = x1.shape
    R = gate.shape[1]
    rb = 1 if R == 1 else tm
    mod_map = (lambda b, i: (b, 0, 0)) if R == 1 else (lambda b, i: (b, i, 0))
    row = lambda n: pl.BlockSpec((1, tm, n), lambda b, i: (b, i, 0))
    vec = pl.BlockSpec((1, D), lambda b, i: (0, 0))
    return pl.pallas_call(
        _final_kernel,
        out_shape=jax.ShapeDtypeStruct((B, T, D), F32),
        grid=(B, T // tm),
        in_specs=[row(D), row(D), row(D), row(D), row(D), row(LANE),
                  pl.BlockSpec((1, rb, D), mod_map), vec, vec],
        out_specs=row(D),
        compiler_params=_cparams("parallel", "parallel"),
        name="moe_combine_ln",
    )(x1, *ys, tw, gate, ln_g, ln_b)


def _cmp_select_step_kernel(q_ref, k_ref, v_ref, bias_ref, pool_ref, o_ref, idx_ref, *, n_cmp, n_blk, q_pos):
    q = q_ref[0].astype(BF16)
    ncp = k_ref.shape[2]
    nbp = pool_ref.shape[1]
    row = lax.broadcasted_iota(jnp.int32, (N_HEADS, 1), 0)
    first = row < GQA
    s = jnp.where(first, _nt_dot(q, k_ref[0, 0]), _nt_dot(q, k_ref[0, 1])) * (HEAD_DIM ** -0.5)
    s = s + bias_ref[...]
    ci = lax.broadcasted_iota(jnp.int32, (N_HEADS, ncp), 1)
    mask = (ci * CMP_STRIDE + CMP_BLOCK - 1 <= q_pos) & (ci < n_cmp)
    s = jnp.where(mask, s, NEG)
    m = jnp.max(s, axis=-1, keepdims=True)
    p = jnp.where(mask, jnp.exp(s - m), 0.0)
    p = p / jnp.maximum(jnp.sum(p, axis=-1, keepdims=True), 1e-30)
    pb = p.astype(BF16)
    o_ref[0] = jnp.where(first, jnp.dot(pb, v_ref[0, 0], preferred_element_type=F32),
                         jnp.dot(pb, v_ref[0, 1], preferred_element_type=F32))
    imp0 = jnp.sum(jnp.where(first, p, 0.0), axis=0, keepdims=True)
    imp1 = jnp.sum(jnp.where(first, 0.0, p), axis=0, keepdims=True)
    imp = jnp.where(first, imp0, imp1)
    sb = jnp.dot(imp, pool_ref[...], precision=HIGHEST, preferred_element_type=F32)
    cur = q_pos // SEL_BLOCK
    bi = lax.broadcasted_iota(jnp.int32, (nbp, nbp), 0)
    bj = lax.broadcasted_iota(jnp.int32, (nbp, nbp), 1)
    blk = lax.broadcasted_iota(jnp.int32, (1, nbp), 1)
    causal = blk <= cur
    forced = (blk == 0) | (blk == cur) | (blk == cur - 1)
    rsel = lax.broadcasted_iota(jnp.int32, (N_SEL, nbp), 0)
    for h in range(N_KV_HEADS):
        sc = jnp.where(forced & causal, 1e4, jnp.where(causal, sb[h * GQA:h * GQA + 1, :], -1.0))
        sc = jnp.where(blk < n_blk, sc, -2.0)
        scb = jnp.broadcast_to(sc, (nbp, nbp))
        col = jnp.sum(jnp.where(bi == bj, scb, 0.0), axis=1, keepdims=True)
        ahead = (col > scb) | ((col == scb) & (bi < bj))
        rank = jnp.sum(ahead.astype(jnp.int32), axis=0, keepdims=True)
        hit = jnp.broadcast_to(rank, (N_SEL, nbp)) == rsel
        idx = jnp.sum(jnp.where(hit, jnp.broadcast_to(blk, (N_SEL, nbp)), 0), axis=1, keepdims=True)
        idx_ref[0, h] = jnp.broadcast_to(idx, (N_SEL, LANE))


def _cmp_select_step(q, kc, vc, bias, pool, n_cmp, n_blk, q_pos):
    B = q.shape[0]
    NCp = kc.shape[2]
    return pl.pallas_call(
        functools.partial(_cmp_select_step_kernel, n_cmp=n_cmp, n_blk=n_blk, q_pos=q_pos),
        out_shape=(jax.ShapeDtypeStruct((B, N_HEADS, HEAD_DIM), F32),
                   jax.ShapeDtypeStruct((B, N_KV_HEADS, N_SEL, LANE), jnp.int32)),
        grid=(B,),
        in_specs=[pl.BlockSpec((1, N_HEADS, HEAD_DIM), lambda b: (b, 0, 0)),
                  pl.BlockSpec((1, N_KV_HEADS, NCp, HEAD_DIM), lambda b: (b, 0, 0, 0)),
                  pl.BlockSpec((1, N_KV_HEADS, NCp, HEAD_DIM), lambda b: (b, 0, 0, 0)),
                  pl.BlockSpec(bias.shape, lambda b: (0, 0)),
                  pl.BlockSpec(pool.shape, lambda b: (0, 0))],
        out_specs=(pl.BlockSpec((1, N_HEADS, HEAD_DIM), lambda b: (b, 0, 0)),
                   pl.BlockSpec((1, N_KV_HEADS, N_SEL, LANE), lambda b: (b, 0, 0, 0))),
        compiler_params=_cparams("parallel"),
        name="cmp_select_step",
    )(q, kc, vc, bias, pool)


def _sel_step_kernel(idx_ref, q_ref, kv_ref, new_ref, bias_ref, kpos_ref, o_ref, *, n_past, q_pos):
    b, h = pl.program_id(0), pl.program_id(1)
    base = (b * N_KV_HEADS + h) * N_SEL
    new = new_ref[0]
    segs = [jnp.where(idx_ref[base + j] >= n_past, new, kv_ref[0, 0, j * SEL_BLOCK:(j + 1) * SEL_BLOCK, :])
            for j in range(N_SEL)]
    kv = jnp.concatenate(segs, axis=0)
    hd = HEAD_DIM
    k = jnp.where(h == 0, kv[:, 0:hd], kv[:, hd:2 * hd]).astype(BF16)
    v = jnp.where(h == 0, kv[:, 2 * hd:3 * hd], kv[:, 3 * hd:]).astype(BF16)
    s = _nt_dot(q_ref[0].astype(BF16), k) * (hd ** -0.5) + bias_ref[0, 0]
    mask = kpos_ref[0, 0] <= q_pos
    s = jnp.where(mask, s, NEG)
    m = jnp.max(s, axis=-1, keepdims=True)
    p = jnp.where(mask, jnp.exp(s - m), 0.0)
    l = jnp.sum(p, axis=-1, keepdims=True)
    o_ref[0, 0] = jnp.dot(p.astype(BF16), v, preferred_element_type=F32) / jnp.maximum(l, 1e-30)


def _sel_step(q, kv_sel, new_blocks, bias_sel, kpos, idx_flat, n_past, q_pos):
    B = q.shape[0]
    nk = N_SEL * SEL_BLOCK
    grid_spec = pltpu.PrefetchScalarGridSpec(
        num_scalar_prefetch=1,
        grid=(B, N_KV_HEADS),
        in_specs=[pl.BlockSpec((1, N_HEADS, HEAD_DIM), lambda b, h, ix: (b, 0, 0)),
                  pl.BlockSpec((1, 1, nk, D_KV), lambda b, h, ix: (b, h, 0, 0)),
                  pl.BlockSpec((1, SEL_BLOCK, D_KV), lambda b, h, ix: (b, 0, 0)),
                  pl.BlockSpec((1, 1, N_HEADS, nk), lambda b, h, ix: (b, h, 0, 0)),
                  pl.BlockSpec((1, 1, 1, nk), lambda b, h, ix: (b, h, 0, 0))],
        out_specs=pl.BlockSpec((1, 1, N_HEADS, HEAD_DIM), lambda b, h, ix: (b, h, 0, 0)),
    )
    return pl.pallas_call(
        functools.partial(_sel_step_kernel, n_past=n_past, q_pos=q_pos),
        out_shape=jax.ShapeDtypeStruct((B, N_KV_HEADS, N_HEADS, HEAD_DIM), F32),
        grid_spec=grid_spec,
        compiler_params=_cparams("arbitrary", "arbitrary"),
        name="sel_step",
    )(idx_flat, q, kv_sel, new_blocks, bias_sel, kpos)


def _win_step_kernel(q_ref, w_ref, new_ref, bias_ref, bias0_ref, o_ref):
    q = q_ref[0]
    qb = q.astype(BF16)
    row = lax.broadcasted_iota(jnp.int32, (N_HEADS, 1), 0)
    first = row < GQA
    w = w_ref[0]
    hd = HEAD_DIM
    kb = [w[:, h * hd:(h + 1) * hd].astype(BF16) for h in range(N_KV_HEADS)]
    vb = [w[:, (N_KV_HEADS + h) * hd:(N_KV_HEADS + h + 1) * hd].astype(BF16) for h in range(N_KV_HEADS)]
    s = jnp.where(first, _nt_dot(qb, kb[0]), _nt_dot(qb, kb[1])) * (hd ** -0.5) + bias_ref[...]
    new = new_ref[0]
    kn = jnp.where(first, new[:, 0:hd], new[:, hd:2 * hd])
    vn = jnp.where(first, new[:, 2 * hd:3 * hd], new[:, 3 * hd:])
    sn = jnp.sum(q * kn, axis=-1, keepdims=True) * (hd ** -0.5) + bias0_ref[...]
    m = jnp.maximum(jnp.max(s, axis=-1, keepdims=True), sn)
    p = jnp.exp(s - m)
    pn = jnp.exp(sn - m)
    l = jnp.sum(p, axis=-1, keepdims=True) + pn
    pb = p.astype(BF16)
    acc = jnp.where(first, jnp.dot(pb, vb[0], preferred_element_type=F32),
                    jnp.dot(pb, vb[1], preferred_element_type=F32)) + pn * vn
    o_ref[0] = acc / jnp.maximum(l, 1e-30)


def _win_step(q, win, new, bias, bias0):
    B, W, _ = win.shape
    return pl.pallas_call(
        _win_step_kernel,
        out_shape=jax.ShapeDtypeStruct((B, N_HEADS, HEAD_DIM), F32),
        grid=(B,),
        in_specs=[pl.BlockSpec((1, N_HEADS, HEAD_DIM), lambda b: (b, 0, 0)),
                  pl.BlockSpec((1, W, D_KV), lambda b: (b, 0, 0)),
                  pl.BlockSpec((1, 1, D_KV), lambda b: (b, 0, 0)),
                  pl.BlockSpec((N_HEADS, W), lambda b: (0, 0)),
                  pl.BlockSpec((N_HEADS, 1), lambda b: (0, 0))],
        out_specs=pl.BlockSpec((1, N_HEADS, HEAD_DIM), lambda b: (b, 0, 0)),
        compiler_params=_cparams("parallel"),
        name="win_step",
    )(q, win, new, bias, bias0)


def _split_heads(kv, dtype):
    B, L, _ = kv.shape
    kv5 = kv.reshape(B, L, 2, N_KV_HEADS, HEAD_DIM)
    return (jnp.transpose(kv5[:, :, 0], (0, 2, 1, 3)).astype(dtype),
            jnp.transpose(kv5[:, :, 1], (0, 2, 1, 3)).astype(dtype))


def _nsa_prompt(q, kvc, kvs, kvw, cmp_tab, rel_bias):
    B, T, _ = q.shape
    nc = T // CMP_STRIDE
    nb = T // SEL_BLOCK
    ckv = _compress([kvc.reshape(B, nc, CMP_STRIDE * D_KV)], cmp_tab)
    kc, vc = _split_heads(ckv, BF16)
    vct = jnp.transpose(vc, (0, 1, 3, 2))
    bias_n = _bias_by_distance(rel_bias, T)
    n_qt, n_kt = T // ATT_TQ, T // ATT_TK
    n_ds = min(n_kt, -(-(REL_MAX_DIST + ATT_TK - 1) // ATT_TK) + 1)
    n_dw = min(n_kt, WINDOW // ATT_TK + 1)
    tzs, tzw, bias_tab = _bias_tables(bias_n, n_qt, nc // 8, n_ds, n_dw, ATT_TQ, ATT_TK)
    pool = jnp.asarray(_pool_matrix(nc, nb))
    scale = HEAD_DIM ** -0.5
    q5 = jnp.transpose((q * scale).reshape(B, T, N_KV_HEADS, GQA, HEAD_DIM), (0, 2, 3, 1, 4))
    o_cmp, sel = _cmp_select_prompt(q5, kc, vct, bias_tab, pool, nc - 1)
    ks, vs = _split_heads(kvs, BF16)
    kw, vw = _split_heads(kvw, BF16)
    o_sel, o_win = _sel_win_prompt(q5, ks, jnp.transpose(vs, (0, 1, 3, 2)), kw, jnp.transpose(vw, (0, 1, 3, 2)),
                                   sel, tzs, tzw)
    return o_cmp, o_sel, o_win


def _nsa_sample(q, kvc, kvs, kvw, pool_cmp, pool_sel, win_buf, page_table, cmp_tab, rel_bias):
    B = q.shape[0]
    n_pages = page_table.shape[1]
    past_len = n_pages * PAGE_SIZE
    q_pos = past_len
    lp = -(-(past_len + 1) // SEL_BLOCK) * SEL_BLOCK
    n_cmp = lp // CMP_STRIDE - 1
    n_blk = lp // SEL_BLOCK
    n_past_chunks = past_len // CMP_STRIDE
    n_tail = 8
    assert n_past_chunks + n_tail >= n_cmp + 1
    n_chunks = n_past_chunks + n_tail
    past = pool_cmp[page_table].reshape(B, n_past_chunks, CMP_STRIDE * D_KV)
    tail = jnp.pad(kvc[:, None, :], ((0, 0), (0, n_tail * CMP_STRIDE - 1), (0, 0)))
    ckv = _compress([past, tail.reshape(B, n_tail, CMP_STRIDE * D_KV)], cmp_tab)
    ncp = -(-n_chunks // LANE) * LANE
    nbp = -(-n_blk // LANE) * LANE
    ckv = jnp.pad(ckv, ((0, 0), (0, ncp - n_chunks), (0, 0)))
    kc, vc = _split_heads(ckv, BF16)
    bias_n = _bias_by_distance(rel_bias, q_pos + 1)
    n_back = max(n_blk * SEL_BLOCK, ncp * CMP_STRIDE + CMP_BLOCK)
    back = jnp.concatenate([bias_n[:, ::-1], jnp.broadcast_to(bias_n[:, :1], (N_HEADS, n_back - q_pos - 1))], 1)
    bias_c = back[:, CMP_BLOCK - 1:CMP_BLOCK - 1 + ncp * CMP_STRIDE:CMP_STRIDE]
    pool = jnp.asarray(_pool_matrix(ncp, nbp).T)
    q3 = q.reshape(B, N_HEADS, HEAD_DIM)
    o_cmp, idx = _cmp_select_step(q3, kc, vc, bias_c, pool, n_cmp, n_blk, q_pos)
    idx = idx[..., 0]
    bpp = PAGE_SIZE // SEL_BLOCK
    n_past = n_pages * bpp
    past_idx = jnp.minimum(idx, n_past - 1)
    page = jnp.take_along_axis(page_table, (past_idx // bpp).reshape(B, -1), axis=1).reshape(idx.shape)
    pool6 = pool_sel.reshape(pool_sel.shape[0], bpp, SEL_BLOCK, 2, N_KV_HEADS, HEAD_DIM)
    kv_sel = pool6[page, past_idx % bpp].reshape(B, N_KV_HEADS, N_SEL * SEL_BLOCK, D_KV)
    new_blocks = jnp.pad(kvs[:, None, :], ((0, 0), (0, SEL_BLOCK - 1), (0, 0)))
    bias_blk = jnp.transpose(back[:, :n_blk * SEL_BLOCK].reshape(N_HEADS, n_blk, SEL_BLOCK), (1, 0, 2))
    bias_sel = jnp.transpose(bias_blk[idx], (0, 1, 3, 2, 4)).reshape(B, N_KV_HEADS, N_HEADS, -1)
    kpos = idx[..., None] * SEL_BLOCK + jnp.arange(SEL_BLOCK)
    kpos = jnp.where((idx <= q_pos // SEL_BLOCK)[..., None], kpos, q_pos + 1)
    kpos = kpos.reshape(B, N_KV_HEADS, 1, -1).astype(jnp.int32)
    o_sel = _sel_step(q3, kv_sel, new_blocks, bias_sel, kpos, idx.reshape(-1).astype(jnp.int32),
                      n_past, q_pos)
    o_sel = jnp.concatenate([o_sel[:, h, h * GQA:(h + 1) * GQA] for h in range(N_KV_HEADS)], axis=1)
    wb = win_buf.shape[1]
    bias_w = bias_n[:, 1:wb + 1][:, ::-1]
    o_win = _win_step(q3, win_buf.reshape(B, wb, D_KV), kvw[:, None, :], bias_w, bias_n[:, 0:1])
    return o_cmp.reshape(B, D_ATT), o_sel.reshape(B, D_ATT), o_win.reshape(B, D_ATT)


def kernel(x_prompt, x_sample, cache_cmp_kv, cache_sel_kv, state_win_kv, state_ssm_re, state_ssm_im, page_table,
           c_prompt, c_sample, w_ada, b_ada, w_in, lam_re, lam_im, log_dt, b_re, b_im, c_re, c_im, d_skip,
           w_glu, b_glu, phi_pe, phi_w1, phi_b1, phi_w2, phi_b2, rel_bias, w_out, ln1_g, ln1_b,
           w_router, b_router, w_gate_up, b_gate_up, w_down, b_down, ln2_g, ln2_b):
    assert w_ada.shape[0] == DEPTH == 1
    l = 0
    Bp, T, D = x_prompt.shape
    Bs = x_sample.shape[0]
    kv_tail = (2, N_KV_HEADS, HEAD_DIM)

    n_c = Bp + Bs
    c_all = jnp.pad(jnp.concatenate([c_prompt, c_sample], 0), ((0, -n_c % 8), (0, 0)))
    m_all = _adaln(c_all, w_ada[l], b_ada[l])
    m_p = m_all[:Bp].reshape(Bp, 6, D)
    m_s = m_all[Bp:n_c].reshape(Bs, 6, D)
    mod_p = [m_p[:, i:i + 1, :] for i in range(6)]
    mod_s = [m_s[None, :, i, :] for i in range(6)]

    w_in_pad = jnp.pad(w_in[l], ((0, 0), (0, D_IN_PAD - D_IN))).astype(BF16)
    n_levels = max(1, int(math.log2(T // SSM_CHUNK)))
    ssm_tab = _ssm_tables(lam_re[l], lam_im[l], log_dt[l], b_re[l], b_im[l], c_re[l], c_im[l],
                          SSM_CHUNK, n_levels)
    cmp_tab = _compress_tables(phi_pe[l], phi_w1[l], phi_b1[l], phi_w2[l], phi_b2[l])
    w_post = dict(
        d_skip=d_skip[l].reshape(1, D_SSM), w_glu=w_glu[l].astype(BF16), b_glu=b_glu[l].reshape(1, D_SSM),
        gexp=jnp.asarray(_gate_expand_matrix()), w_out=w_out[l].astype(BF16),
        ln1_g=ln1_g[l].reshape(1, D), ln1_b=ln1_b[l].reshape(1, D),
        w_router=jnp.pad(w_router[l], ((0, 0), (0, LANE - N_EXPERTS))),
        b_router=jnp.pad(b_router[l], (0, LANE - N_EXPERTS)).reshape(1, LANE))

    u, q, kvc, kvs, kvw, g = _mixer_in(x_prompt, mod_p[0], mod_p[1], w_in_pad, tm=512)
    y_ssm, h_p = _ssm_prompt(u, ssm_tab)
    o_cmp, o_sel, o_win = _nsa_prompt(q, kvc, kvs, kvw, cmp_tab, rel_bias)
    x1_p, hm_p, te_p, tw_p = _post_mixer(y_ssm, u, o_cmp, o_sel, o_win, g, x_prompt,
                                         mod_p[2], mod_p[3], mod_p[4], w_post, tm=256)

    u_s, q_s, kvc_s, kvs_s, kvw_s, g_s = _mixer_in(x_sample.reshape(1, Bs, D), mod_s[0], mod_s[1],
                                                   w_in_pad, tm=Bs)
    y_s, h_s = _ssm_sample(u_s[0], state_ssm_re[l], state_ssm_im[l], ssm_tab, c_re[l], c_im[l])
    oc_s, os_s, ow_s = _nsa_sample(q_s[0].astype(F32), kvc_s[0], kvs_s[0], kvw_s[0], cache_cmp_kv[l],
                                   cache_sel_kv[l], state_win_kv[l], page_table, cmp_tab, rel_bias)
    x1_s, hm_s, te_s, tw_s = _post_mixer(y_s[None], u_s, oc_s[None], os_s[None], ow_s[None], g_s,
                                         x_sample.reshape(1, Bs, D), mod_s[2], mod_s[3], mod_s[4],
                                         w_post, tm=Bs)

    n_p = Bp * T
    n_all = n_p + Bs
    hm_all = jnp.concatenate([hm_p.reshape(n_p, D), hm_s.reshape(Bs, D)], 0)
    te_all = jnp.concatenate([te_p.reshape(n_p, LANE), te_s.reshape(Bs, LANE)], 0)[:, :TOP_K]
    row_tok, dest, blk_e, n_used = _moe_dispatch(te_all, n_all)
    xb = jnp.concatenate([hm_all, jnp.zeros((1, D), F32)], 0)[row_tok]
    yb = _experts(xb, blk_e, n_used, w_gate_up[l], b_gate_up[l], w_down[l], b_down[l])
    ys = [yb[dest[:, k]] for k in range(TOP_K)]
    ln2g, ln2b = ln2_g[l].reshape(1, D), ln2_b[l].reshape(1, D)
    out_p = _final(x1_p, [y[:n_p].reshape(Bp, T, D) for y in ys], tw_p, mod_p[5], ln2g, ln2b, tm=512)
    out_s = _final(x1_s, [y[n_p:].reshape(1, Bs, D) for y in ys], tw_s, mod_s[5], ln2g, ln2b, tm=Bs)

    wlen = min(WINDOW, T)
    win_s = jnp.concatenate([state_win_kv[l], kvw_s[0].reshape(Bs, 1, *kv_tail)], 1)[:, -state_win_kv.shape[2]:]
    p_state = SSM_STATE
    return (out_p, out_s.reshape(Bs, 1, D),
            kvc.reshape(1, Bp, T, *kv_tail), kvc_s[0].reshape(1, Bs, 1, *kv_tail),
            kvs.reshape(1, Bp, T, *kv_tail), kvs_s[0].reshape(1, Bs, 1, *kv_tail),
            kvw[:, T - wlen:].reshape(1, Bp, wlen, *kv_tail), win_s[None],
            h_p[None, ..., :p_state], h_p[None, ..., p_state:],
            h_s[None, ..., :p_state], h_s[None, ..., p_state:])
```

```python
import functools
import math

import numpy as np
import jax
import jax.numpy as jnp
from jax import lax
from jax.experimental import pallas as pl
from jax.experimental.pallas import tpu as pltpu

D_MODEL = 1024
DEPTH = 1
PAST_LEN = 16384
PAGE_SIZE = 128
D_SSM = 512
SSM_GROUP = 16
N_SSM_GROUPS = D_SSM // SSM_GROUP
SSM_STATE = 64
N_HEADS = 8
HEAD_DIM = 64
N_KV_HEADS = 2
GQA = N_HEADS // N_KV_HEADS
D_ATT = N_HEADS * HEAD_DIM
D_KV = 2 * N_KV_HEADS * HEAD_DIM
CMP_STRIDE = 16
CMP_BLOCK = 2 * CMP_STRIDE
SEL_BLOCK = 64
N_SEL = 16
WINDOW = 512
NUM_BUCKETS = 32
REL_MAX_DIST = 1024
N_EXPERTS = 32
TOP_K = 4
D_FF = 1024
SWIGLU_LIMIT = 7.0
SWIGLU_ALPHA = 1.702
DN_ALPHA = (2 * DEPTH) ** 0.25
D_IN = D_SSM + D_ATT + 3 * D_KV + 3 * N_HEADS
NEG = -1e30
F32 = jnp.float32
BF16 = jnp.bfloat16
HIGHEST = lax.Precision.HIGHEST

LANE = 128
D_IN_PAD = 1920
GATE_COL = D_SSM + D_ATT + 3 * D_KV
SSM_CHUNK = 16
ATT_TQ = 128
ATT_TK = 128
MOE_ROWS = 256
PAGES_PER_STEP = 32
VMEM_LIMIT = 48 * 1024 * 1024
LN_EPS = 1e-5


def _cparams(*sem):
    return pltpu.CompilerParams(dimension_semantics=sem, vmem_limit_bytes=VMEM_LIMIT)


def _nt_dot(a, b):
    return lax.dot_general(a, b, (((1,), (1,)), ((), ())), preferred_element_type=F32)


def _layer_norm(x):
    mu = jnp.mean(x, axis=-1, keepdims=True)
    xc = x - mu
    var = jnp.mean(xc * xc, axis=-1, keepdims=True)
    return xc * lax.rsqrt(var + LN_EPS)


def _adaln_kernel(c_ref, w_ref, b_ref, o_ref):
    c = c_ref[...]
    s = c * jax.nn.sigmoid(c)
    o_ref[...] = jnp.dot(s, w_ref[...], precision=HIGHEST, preferred_element_type=F32) + b_ref[...]


def _adaln(c, w, b):
    n, d = c.shape
    dout = w.shape[1]
    tn = 1024
    return pl.pallas_call(
        _adaln_kernel,
        out_shape=jax.ShapeDtypeStruct((n, dout), F32),
        grid=(dout // tn,),
        in_specs=[pl.BlockSpec((n, d), lambda j: (0, 0)),
                  pl.BlockSpec((d, tn), lambda j: (0, j)),
                  pl.BlockSpec((1, tn), lambda j: (0, j))],
        out_specs=pl.BlockSpec((n, tn), lambda j: (0, j)),
        compiler_params=_cparams("arbitrary"),
        name="adaln",
    )(c, w, b.reshape(1, dout))


def _mixer_in_kernel(x_ref, sh_ref, sc_ref, w_ref, u_ref, q_ref, kvc_ref, kvs_ref, kvw_ref, g_ref):
    h = _layer_norm(x_ref[0]) * (1.0 + sc_ref[0]) + sh_ref[0]
    z = jnp.dot(h.astype(BF16), w_ref[...], preferred_element_type=F32)
    c0 = D_SSM
    c1 = c0 + D_ATT
    c2 = c1 + D_KV
    c3 = c2 + D_KV
    c4 = c3 + D_KV
    u_ref[0] = z[:, :c0]
    q_ref[0] = z[:, c0:c1].astype(BF16)
    kvc_ref[0] = z[:, c1:c2]
    kvs_ref[0] = z[:, c2:c3]
    kvw_ref[0] = z[:, c3:c4]
    g_ref[0] = z[:, c4:c4 + LANE]


def _mixer_in(x, shift, scale, w_pad, tm):
    B, T, D = x.shape
    R = shift.shape[1]
    rb = 1 if R == 1 else tm
    mod_map = (lambda b, i: (b, 0, 0)) if R == 1 else (lambda b, i: (b, i, 0))
    row = lambda n: pl.BlockSpec((1, tm, n), lambda b, i: (b, i, 0))
    outs = (jax.ShapeDtypeStruct((B, T, D_SSM), F32), jax.ShapeDtypeStruct((B, T, D_ATT), BF16),
            jax.ShapeDtypeStruct((B, T, D_KV), F32), jax.ShapeDtypeStruct((B, T, D_KV), F32),
            jax.ShapeDtypeStruct((B, T, D_KV), F32), jax.ShapeDtypeStruct((B, T, LANE), F32))
    return pl.pallas_call(
        _mixer_in_kernel,
        out_shape=outs,
        grid=(B, T // tm),
        in_specs=[row(D), pl.BlockSpec((1, rb, D), mod_map), pl.BlockSpec((1, rb, D), mod_map),
                  pl.BlockSpec((D, D_IN_PAD), lambda b, i: (0, 0))],
        out_specs=(row(D_SSM), row(D_ATT), row(D_KV), row(D_KV), row(D_KV), row(LANE)),
        compiler_params=_cparams("parallel", "parallel"),
        name="mixer_in",
    )(x, shift, scale, w_pad)


def _ssm_tables(lam_re, lam_im, log_dt, b_re, b_im, c_re, c_im, L, n_levels):
    G, P = lam_re.shape
    C = b_re.shape[-1]
    dt = jnp.exp(log_dt.astype(F32))[:, None]
    er, ei = lam_re * dt, lam_im * dt

    def power(k):
        kk = k.astype(F32)[:, None, None]
        mag = jnp.exp(kk * er)
        return mag * jnp.cos(kk * ei), mag * jnp.sin(kk * ei)

    lb_re, lb_im = power(jnp.ones((1,), F32))
    nr, ni = lb_re[0] - 1.0, lb_im[0]
    den = lam_re * lam_re + lam_im * lam_im
    fr = (nr * lam_re + ni * lam_im) / den
    fi = (ni * lam_re - nr * lam_im) / den
    bbr = fr[:, :, None] * b_re - fi[:, :, None] * b_im
    bbi = fr[:, :, None] * b_im + fi[:, :, None] * b_re
    pr, pi = power(jnp.arange(L + 1))
    clr = c_re[None] * pr[:, :, None, :] - c_im[None] * pi[:, :, None, :]
    cli = c_re[None] * pi[:, :, None, :] + c_im[None] * pr[:, :, None, :]
    kern = (jnp.einsum('kgcp,gpd->kgcd', clr[:L], bbr, precision=HIGHEST)
            - jnp.einsum('kgcp,gpd->kgcd', cli[:L], bbi, precision=HIGHEST))
    kz = jnp.concatenate([kern, jnp.zeros((1,) + kern.shape[1:], F32)], 0)
    ts = np.arange(L)
    lag = ts[None, :] - ts[:, None]
    lag = np.where(lag >= 0, lag, L)
    toep = kz[lag]
    toep = jnp.transpose(toep, (2, 0, 4, 1, 3)).reshape(G, L * C, L * C)
    rev = L - 1 - ts
    wsr = pr[rev][:, :, :, None] * bbr[None] - pi[rev][:, :, :, None] * bbi[None]
    wsi = pr[rev][:, :, :, None] * bbi[None] + pi[rev][:, :, :, None] * bbr[None]
    ws = jnp.concatenate([jnp.transpose(wsr, (1, 0, 3, 2)), jnp.transpose(wsi, (1, 0, 3, 2))], -1)
    ws = ws.reshape(G, L * C, 2 * P)
    wy = jnp.concatenate([jnp.transpose(clr[1:], (1, 3, 0, 2)), -jnp.transpose(cli[1:], (1, 3, 0, 2))], 1)
    wy = wy.reshape(G, 2 * P, L * C)
    lr, li = power(L * (2 ** jnp.arange(n_levels)))
    ar = jnp.transpose(jnp.concatenate([lr, lr], -1), (1, 0, 2))
    ai = jnp.transpose(jnp.concatenate([-li, li], -1), (1, 0, 2))
    return toep.astype(BF16), ws.astype(BF16), wy.astype(BF16), ar, ai, (lb_re[0], lb_im[0], bbr, bbi)


def _ssm_kernel(u_ref, toep_ref, ws_ref, wy_ref, ar_ref, ai_ref, y_ref, hl_ref, *, nb, nc, n_levels):
    u = u_ref[0]
    y1 = jnp.dot(u, toep_ref[0], preferred_element_type=F32)
    s = jnp.dot(u, ws_ref[0], preferred_element_type=F32)
    p2 = s.shape[-1]
    rows = lax.broadcasted_iota(jnp.int32, (nc, p2), 0)
    prev = []
    for b in range(nb):
        h = s[b * nc:(b + 1) * nc]
        for k in range(n_levels):
            d = 1 << k
            sh = jnp.where(rows >= d, pltpu.roll(h, d, axis=0), 0.0)
            sw = pltpu.roll(sh, p2 // 2, axis=1)
            h = h + ar_ref[0, k:k + 1, :] * sh + ai_ref[0, k:k + 1, :] * sw
        hl_ref[0, b:b + 1, :] = h[nc - 1:nc, :]
        prev.append(jnp.where(rows >= 1, pltpu.roll(h, 1, axis=0), 0.0))
    hp = jnp.concatenate(prev, axis=0)
    y2 = jnp.dot(hp.astype(BF16), wy_ref[0], preferred_element_type=F32)
    y_ref[0] = y1 + y2


def _ssm_prompt(u, tables):
    toep, ws, wy, ar, ai, _ = tables
    B, T, _ = u.shape
    G, C, L = N_SSM_GROUPS, SSM_GROUP, SSM_CHUNK
    nc = T // L
    n_levels = ar.shape[1]
    ug = jnp.transpose(u.reshape(B, nc, L, G, C), (3, 0, 1, 2, 4)).reshape(G, B * nc, L * C).astype(BF16)
    grp = lambda r, c: pl.BlockSpec((1, r, c), lambda g: (g, 0, 0))
    y, hl = pl.pallas_call(
        functools.partial(_ssm_kernel, nb=B, nc=nc, n_levels=n_levels),
        out_shape=(jax.ShapeDtypeStruct((G, B * nc, L * C), F32),
                   jax.ShapeDtypeStruct((G, B, 2 * SSM_STATE), F32)),
        grid=(G,),
        in_specs=[grp(B * nc, L * C), grp(L * C, L * C), grp(L * C, 2 * SSM_STATE),
                  grp(2 * SSM_STATE, L * C), grp(n_levels, 2 * SSM_STATE), grp(n_levels, 2 * SSM_STATE)],
        out_specs=(grp(B * nc, L * C), grp(B, 2 * SSM_STATE)),
        compiler_params=_cparams("parallel"),
        name="ssm_prompt",
    )(ug, toep, ws, wy, ar, ai)
    y = jnp.transpose(y.reshape(G, B, nc, L, C), (1, 2, 3, 0, 4)).reshape(B, T, D_SSM)
    return y, jnp.transpose(hl, (1, 0, 2))


def _ssm_step_kernel(u_ref, h0_ref, bb_ref, lr_ref, li_ref, cy_ref, y_ref, h_ref):
    p = lr_ref.shape[-1] // 2
    bu = jnp.einsum('gbc,gcp->gbp', u_ref[...], bb_ref[...], preferred_element_type=F32)
    h0 = h0_ref[...]
    h0s = jnp.concatenate([h0[..., p:], h0[..., :p]], axis=-1)
    h = lr_ref[...] * h0 + li_ref[...] * h0s + bu
    h_ref[...] = h
    y_ref[...] = jnp.einsum('gbp,gpc->gbc', h.astype(BF16), cy_ref[...], preferred_element_type=F32)


def _ssm_sample(u, h0_re, h0_im, tables, c_re, c_im):
    lb_re, lb_im, bbr, bbi = tables[-1]
    B = u.shape[0]
    G, C, P = N_SSM_GROUPS, SSM_GROUP, SSM_STATE
    ug = jnp.transpose(u.reshape(B, G, C), (1, 0, 2)).astype(BF16)
    h0 = jnp.transpose(jnp.concatenate([h0_re, h0_im], -1), (1, 0, 2)).astype(F32)
    bb = jnp.concatenate([jnp.transpose(bbr, (0, 2, 1)), jnp.transpose(bbi, (0, 2, 1))], -1).astype(BF16)
    lr = jnp.concatenate([lb_re, lb_re], -1)[:, None, :]
    li = jnp.concatenate([-lb_im, lb_im], -1)[:, None, :]
    cy = jnp.concatenate([jnp.transpose(c_re, (0, 2, 1)), -jnp.transpose(c_im, (0, 2, 1))], 1).astype(BF16)
    y, h = pl.pallas_call(
        _ssm_step_kernel,
        out_shape=(jax.ShapeDtypeStruct((G, B, C), F32), jax.ShapeDtypeStruct((G, B, 2 * P), F32)),
        name="ssm_step",
    )(ug, h0, bb, lr, li, cy)
    return jnp.transpose(y, (1, 0, 2)).reshape(B, D_SSM), jnp.transpose(h, (1, 0, 2))


def _compress_tables(phi_pe, phi_w1, phi_b1, phi_w2, phi_b2):
    S, H, Dh = CMP_STRIDE, N_KV_HEADS, HEAD_DIM
    w1 = phi_w1.reshape(2, 2, S, Dh, Dh)
    eye_c = jnp.eye(2, dtype=F32)
    eye_h = jnp.eye(H, dtype=F32)
    wbig = jnp.einsum('cajde,xc,yh->jxydache', w1, eye_c, eye_h).reshape(S * 2 * H * Dh, 2 * 2 * H * Dh)
    pe = jnp.transpose(phi_pe.reshape(2, 2, S, Dh), (1, 2, 0, 3))
    pe_rows = jnp.broadcast_to(pe[:, :, :, None, :], (2, S, 2, H, Dh)).reshape(2, 1, S * 2 * H * Dh)
    b1 = jnp.broadcast_to(phi_b1[:, None, :], (2, H, Dh)).reshape(1, 2 * H * Dh)
    w2 = jnp.einsum('cef,cx,hy->chexyf', phi_w2, eye_c, eye_h).reshape(2 * H * Dh, 2 * H * Dh)
    b2 = jnp.broadcast_to(phi_b2[:, None, :], (2, H, Dh)).reshape(1, 2 * H * Dh)
    return wbig.astype(BF16), pe_rows, b1, w2.astype(BF16), b2


def _compress_in_kernel(x_ref, pe_ref, w_ref, z_ref):
    x = x_ref[0]
    n = w_ref.shape[1] // 2
    z_ref[0, :, :n] = jnp.dot((x + pe_ref[0]).astype(BF16), w_ref[:, :n], preferred_element_type=F32)
    z_ref[0, :, n:] = jnp.dot((x + pe_ref[1]).astype(BF16), w_ref[:, n:], preferred_element_type=F32)


def _compress_in(x2, tables):
    wbig, pe_rows = tables[0], tables[1]
    N2 = wbig.shape[1]
    B, n, K = x2.shape
    tr = math.gcd(n, 256)
    return pl.pallas_call(
        _compress_in_kernel,
        out_shape=jax.ShapeDtypeStruct((B, n, N2), F32),
        grid=(B, n // tr),
        in_specs=[pl.BlockSpec((1, tr, K), lambda b, i: (b, i, 0)),
                  pl.BlockSpec((2, 1, K), lambda b, i: (0, 0, 0)),
                  pl.BlockSpec((K, N2), lambda b, i: (0, 0))],
        out_specs=pl.BlockSpec((1, tr, N2), lambda b, i: (b, i, 0)),
        compiler_params=_cparams("parallel", "parallel"),
        name="compress_in",
    )(x2, pe_rows, wbig)


def _compress_in_paged_kernel(pt_ref, *refs, n_pg):
    x_refs = refs[:n_pg]
    pe_ref, w_ref, z_ref = refs[n_pg:n_pg + 3]
    s_refs = refs[n_pg + 3:]
    for k in range(n_pg):
        t = x_refs[k][0].reshape(D_KV, PAGE_SIZE).T
        for c, s_ref in enumerate(s_refs):
            s_ref[k * PAGE_SIZE:(k + 1) * PAGE_SIZE, :] = t[:, c * LANE:(c + 1) * LANE]
    rows = n_pg * PAGE_SIZE // CMP_STRIDE
    n = w_ref.shape[1] // 2
    za = jnp.zeros((rows, n), F32)
    zb = jnp.zeros((rows, n), F32)
    for j in range(CMP_STRIDE):
        xj = jnp.concatenate([s_ref[pl.ds(j, rows, stride=CMP_STRIDE), :] for s_ref in s_refs], axis=1)
        col = slice(j * D_KV, (j + 1) * D_KV)
        za = za + jnp.dot((xj + pe_ref[0, :, col]).astype(BF16), w_ref[col, :n], preferred_element_type=F32)
        zb = zb + jnp.dot((xj + pe_ref[1, :, col]).astype(BF16), w_ref[col, n:], preferred_element_type=F32)
    z_ref[0, :, :n] = za
    z_ref[0, :, n:] = zb


def _compress_in_paged(pool_t, page_table, tables):
    wbig, pe_rows = tables[0], tables[1]
    N2 = wbig.shape[1]
    K = wbig.shape[0]
    B, n_pages = page_table.shape
    n_pg = math.gcd(n_pages, PAGES_PER_STEP)
    rows = n_pg * PAGE_SIZE // CMP_STRIDE
    page_spec = lambda k: pl.BlockSpec((1,) + pool_t.shape[1:],
                                       lambda b, i, pt, k=k: (pt[b, i * n_pg + k], 0, 0, 0, 0))
    grid_spec = pltpu.PrefetchScalarGridSpec(
        num_scalar_prefetch=1,
        grid=(B, n_pages // n_pg),
        in_specs=[page_spec(k) for k in range(n_pg)] + [
            pl.BlockSpec((2, 1, K), lambda b, i, pt: (0, 0, 0)),
            pl.BlockSpec((K, N2), lambda b, i, pt: (0, 0))],
        out_specs=pl.BlockSpec((1, rows, N2), lambda b, i, pt: (b, i, 0)),
        scratch_shapes=[pltpu.VMEM((n_pg * PAGE_SIZE, LANE), F32) for _ in range(D_KV // LANE)],
    )
    return pl.pallas_call(
        functools.partial(_compress_in_paged_kernel, n_pg=n_pg),
        out_shape=jax.ShapeDtypeStruct((B, n_pages * PAGE_SIZE // CMP_STRIDE, N2), F32),
        grid_spec=grid_spec,
        compiler_params=_cparams("arbitrary", "arbitrary"),
        name="compress_in_paged",
    )(page_table, *([pool_t] * n_pg), pe_rows, wbig)


def _compress_out_kernel(z_ref, b1_ref, w2_ref, b2_ref, o_ref):
    z = z_ref[0]
    n = z.shape[-1] // 2
    rows = z.shape[0]
    second = pltpu.roll(z[:, n:], rows - 1, axis=0)
    hdn = jax.nn.gelu(z[:, :n] + second + b1_ref[...])
    o_ref[0, :rows, :] = jnp.dot(hdn.astype(BF16), w2_ref[...], preferred_element_type=F32) + b2_ref[...]
    if o_ref.shape[1] > rows:
        o_ref[0, rows:, :] = jnp.zeros((o_ref.shape[1] - rows, n), F32)


def _compress_out(z, tables, n_out):
    _, _, b1, w2, b2 = tables
    B, n, N2 = z.shape
    return pl.pallas_call(
        _compress_out_kernel,
        out_shape=jax.ShapeDtypeStruct((B, n_out, N2 // 2), F32),
        grid=(B,),
        in_specs=[pl.BlockSpec((1, n, N2), lambda b: (b, 0, 0)),
                  pl.BlockSpec((1, N2 // 2), lambda b: (0, 0)),
                  pl.BlockSpec((N2 // 2, N2 // 2), lambda b: (0, 0)),
                  pl.BlockSpec((1, N2 // 2), lambda b: (0, 0))],
        out_specs=pl.BlockSpec((1, n_out, N2 // 2), lambda b: (b, 0, 0)),
        compiler_params=_cparams("parallel"),
        name="compress_out",
    )(z, b1, w2, b2)


def _rel_bucket(dist):
    n = jnp.maximum(dist, 0)
    max_exact = NUM_BUCKETS // 2
    nf = jnp.maximum(n, 1).astype(F32)
    large = max_exact + (jnp.log(nf / max_exact) / math.log(REL_MAX_DIST / max_exact)
                         * (NUM_BUCKETS - max_exact)).astype(jnp.int32)
    large = jnp.minimum(large, NUM_BUCKETS - 1)
    return jnp.where(n < max_exact, n, large)


def _bias_by_distance(rel_bias, n_max):
    onehot = (_rel_bucket(jnp.arange(n_max))[None, :] == jnp.arange(NUM_BUCKETS)[:, None]).astype(F32)
    return jnp.dot(jnp.transpose(rel_bias.astype(F32)), onehot, precision=HIGHEST)


def _shifted_chunks(bias_n, pad, n_chunks, width):
    n = min(bias_n.shape[1], n_chunks * width - pad)
    ext = jnp.concatenate([jnp.broadcast_to(bias_n[:, :1], (N_HEADS, pad)), bias_n[:, :n],
                           jnp.zeros((N_HEADS, n_chunks * width - pad - n), F32)], axis=1)
    return ext.reshape(N_HEADS, n_chunks, width)


def _bias_tables_kernel(ed_ref, ec_ref, tzs_ref, tzw_ref, cmp_ref, *, tq, tk, n_qt):
    n_ds, n_dw, n_j = tzs_ref.shape[1], tzw_ref.shape[1], cmp_ref.shape[1] // 8
    w = tq + tk
    c = lax.broadcasted_iota(jnp.int32, (tk, tq), 0)
    r = lax.broadcasted_iota(jnp.int32, (tk, tq), 1)
    for d in range(n_ds):
        v = jnp.concatenate([ed_ref[0, d:d + 1, :], ed_ref[0, d + 1:d + 2, :]], axis=1)
        t = pltpu.roll(jnp.broadcast_to(v, (tk, w)), w - (tk - 1), axis=1, stride=1, stride_axis=0)[:, :tq]
        dist = d * tk + r - c
        tzs_ref[0, d] = jnp.where(dist >= 0, t, NEG)
        if d < n_dw:
            tzw_ref[0, d] = jnp.where((dist >= 0) & (dist <= WINDOW), t, NEG)
    for j in range(n_j):
        dd = n_qt - 1 - j
        c0, c1 = max(dd, 0), max(dd + 1, 0)
        v = jnp.concatenate([ec_ref[0, c0:c0 + 1, :], ec_ref[0, c1:c1 + 1, :]], axis=1)
        t = pltpu.roll(jnp.broadcast_to(v, (8, w)), w - 7 * CMP_STRIDE, axis=1, stride=CMP_STRIDE, stride_axis=0)
        cmp_ref[0, j * 8:(j + 1) * 8, :] = t[:, :tq]


def _bias_tables(bias_n, n_qt, n_rb, n_ds, n_dw, tq, tk):
    assert tq == tk == 8 * CMP_STRIDE and n_dw <= n_ds
    n_j = n_rb + n_qt - 1
    ed = _shifted_chunks(bias_n, tk - 1, n_ds + 1, tq)
    ec = _shifted_chunks(bias_n, 7 * CMP_STRIDE + CMP_BLOCK - 1, n_qt + 1, tq)
    head = lambda a: pl.BlockSpec((1,) + a.shape[1:], lambda h: (h,) + (0,) * (a.ndim - 1))
    outs = (jax.ShapeDtypeStruct((N_HEADS, n_ds, tk, tq), F32), jax.ShapeDtypeStruct((N_HEADS, n_dw, tk, tq), F32),
            jax.ShapeDtypeStruct((N_HEADS, n_j * 8, tq), F32))
    tzs, tzw, cmp = pl.pallas_call(
        functools.partial(_bias_tables_kernel, tq=tq, tk=tk, n_qt=n_qt),
        out_shape=outs,
        grid=(N_HEADS,),
        in_specs=[head(ed), head(ec)],
        out_specs=tuple(head(o) for o in outs),
        compiler_params=_cparams("parallel"),
        name="bias_tables",
    )(ed, ec)
    grp = lambda a: a.reshape((N_KV_HEADS, GQA) + a.shape[1:])
    return grp(tzs), grp(tzw), cmp


def _pool_matrix(n_cmp_pad, n_blk_pad):
    r = SEL_BLOCK // CMP_STRIDE
    i = np.arange(n_cmp_pad)[None, :]
    j = np.arange(n_blk_pad)[:, None]
    return ((i >= r * j - 1) & (i <= r * j + r - 1)).astype(np.float32)


def _cmp_select_kernel(q_ref, k_ref, vt_ref, bias_ref, pool_ref, o_ref, sel_ref, *, tq, n_cmp):
    qt = pl.program_id(2)
    n_qt = pl.num_programs(2)
    q = q_ref[0, 0].reshape(GQA * tq, HEAD_DIM)
    k = k_ref[0, 0]
    nc = k.shape[0]
    s = _nt_dot(k, q)
    row0 = pl.multiple_of((n_qt - 1 - qt) * 8, 8)
    s = s + jnp.concatenate([bias_ref[g, pl.ds(row0, nc), :] for g in range(GQA)], axis=-1)
    t_pos = qt * tq + (lax.broadcasted_iota(jnp.int32, (nc, GQA * tq), 1) % tq)
    ci = lax.broadcasted_iota(jnp.int32, (nc, GQA * tq), 0)
    mask = (ci * CMP_STRIDE + CMP_BLOCK - 1 <= t_pos) & (ci < n_cmp)
    s = jnp.where(mask, s, NEG)
    m = jnp.max(s, axis=0, keepdims=True)
    p = jnp.where(mask, jnp.exp(s - m), 0.0)
    p = p / jnp.maximum(jnp.sum(p, axis=0, keepdims=True), 1e-30)
    ot = jnp.dot(vt_ref[0, 0], p.astype(BF16), preferred_element_type=F32)
    o_ref[0] = jnp.concatenate([ot[:, g * tq:(g + 1) * tq].T for g in range(GQA)], axis=-1)
    imp = p[:, 0:tq]
    for g in range(1, GQA):
        imp = imp + p[:, g * tq:(g + 1) * tq]
    sb = jnp.dot(pool_ref[...], imp, precision=HIGHEST, preferred_element_type=F32)
    nb = sb.shape[0]
    blk = lax.broadcasted_iota(jnp.int32, (nb, tq), 0)
    cur = (qt * tq + lax.broadcasted_iota(jnp.int32, (nb, tq), 1)) // SEL_BLOCK
    causal = blk <= cur
    forced = (blk == 0) | (blk == cur) | (blk == cur - 1)
    sc = jnp.where(forced & causal, 1e4, jnp.where(causal, sb, -1.0))
    rank = jnp.zeros((nb, tq), jnp.int32)
    for i in range(nb):
        row = sc[i:i + 1, :]
        ahead = (row > sc) | ((row == sc) & (blk > i))
        rank = rank + ahead.astype(jnp.int32)
    sel_ref[0, 0] = jnp.where((rank < N_SEL) & causal, 0.0, NEG)


def _cmp_select_prompt(q5, kc, vct, bias_tab, pool, n_cmp):
    B, _, _, T, _ = q5.shape
    NC = kc.shape[2]
    NB = pool.shape[0]
    R = bias_tab.shape[1]
    tq = ATT_TQ
    return pl.pallas_call(
        functools.partial(_cmp_select_kernel, tq=tq, n_cmp=n_cmp),
        out_shape=(jax.ShapeDtypeStruct((B, T, D_ATT), F32),
                   jax.ShapeDtypeStruct((B, N_KV_HEADS, NB, T), F32)),
        grid=(B, N_KV_HEADS, T // tq),
        in_specs=[pl.BlockSpec((1, 1, GQA, tq, HEAD_DIM), lambda b, h, i: (b, h, 0, i, 0)),
                  pl.BlockSpec((1, 1, NC, HEAD_DIM), lambda b, h, i: (b, h, 0, 0)),
                  pl.BlockSpec((1, 1, HEAD_DIM, NC), lambda b, h, i: (b, h, 0, 0)),
                  pl.BlockSpec((GQA, R, tq), lambda b, h, i: (h, 0, 0)),
                  pl.BlockSpec((NB, NC), lambda b, h, i: (0, 0))],
        out_specs=(pl.BlockSpec((1, tq, GQA * HEAD_DIM), lambda b, h, i: (b, i, h)),
                   pl.BlockSpec((1, 1, NB, tq), lambda b, h, i: (b, h, 0, i))),
        compiler_params=_cparams("parallel", "parallel", "parallel"),
        name="cmp_select_prompt",
    )(q5, kc, vct, bias_tab, pool)


def _sel_win_kernel(q_ref, ks_ref, vst_ref, kw_ref, vwt_ref, sel_ref, tzs_ref, tzw_ref, os_ref, ow_ref, *, tq):
    tk = ATT_TK
    qt = pl.program_id(2)
    q = q_ref[0, 0].reshape(GQA * tq, HEAD_DIM)
    n_ds = tzs_ref.shape[2]
    n_dw = tzw_ref.shape[2]
    per_tile = tk // SEL_BLOCK

    def make_step(k_ref, vt_ref, tz_ref, n_d, use_sel):
        def step(kt, carry):
            m, l, acc = carry
            off = pl.multiple_of(kt * tk, tk)
            k = k_ref[0, 0, pl.ds(off, tk), :]
            vt = vt_ref[0, 0, :, pl.ds(off, tk)]
            d = jnp.minimum(qt - kt, n_d - 1)
            bias = [tz_ref[0, g, d] for g in range(GQA)]
            if use_sel:
                rows = sel_ref[0, 0, pl.ds(kt * per_tile, per_tile), :]
                selb = jnp.concatenate([jnp.broadcast_to(rows[i:i + 1], (SEL_BLOCK, tq))
                                        for i in range(per_tile)], axis=0)
                bias = [b + selb for b in bias]
            s = _nt_dot(k, q) + jnp.concatenate(bias, axis=1)
            m_new = jnp.maximum(m, jnp.max(s, axis=0, keepdims=True))
            alpha = jnp.exp(m - m_new)
            p = jnp.exp(s - m_new)
            l = alpha * l + jnp.sum(p, axis=0, keepdims=True)
            acc = alpha * acc + jnp.dot(vt, p.astype(BF16), preferred_element_type=F32)
            return m_new, l, acc
        return step

    def init():
        return (jnp.full((1, GQA * tq), 0.5 * NEG, F32), jnp.zeros((1, GQA * tq), F32),
                jnp.zeros((HEAD_DIM, GQA * tq), F32))

    def sweep(step, lo, hi):
        n = hi - lo + 1

        def pair(i, carry):
            ca, cb = carry
            kt = lo + 2 * i
            return step(kt, ca), step(kt + 1, cb)

        ca, cb = lax.fori_loop(0, n // 2, pair, (init(), init()))
        ca = lax.cond(n % 2 == 1, lambda c: step(hi, c), lambda c: c, ca)
        (ma, la, acca), (mb, lb, accb) = ca, cb
        m = jnp.maximum(ma, mb)
        ea, eb = jnp.exp(ma - m), jnp.exp(mb - m)
        o = (acca * ea + accb * eb) / jnp.maximum(la * ea + lb * eb, 1e-30)
        return jnp.concatenate([o[:, g * tq:(g + 1) * tq].T for g in range(GQA)], axis=-1)

    os_ref[0] = sweep(make_step(ks_ref, vst_ref, tzs_ref, n_ds, True), 0, qt)
    ow_ref[0] = sweep(make_step(kw_ref, vwt_ref, tzw_ref, n_dw, False), jnp.maximum(qt - (n_dw - 1), 0), qt)


def _sel_win_prompt(q5, ks, vst, kw, vwt, sel, tzs, tzw):
    B, _, _, T, _ = q5.shape
    NB = sel.shape[2]
    tq = ATT_TQ
    k_spec = pl.BlockSpec((1, 1, T, HEAD_DIM), lambda b, h, i: (b, h, 0, 0))
    vt_spec = pl.BlockSpec((1, 1, HEAD_DIM, T), lambda b, h, i: (b, h, 0, 0))
    tz_spec = lambda tz: pl.BlockSpec((1,) + tz.shape[1:], lambda b, h, i: (h, 0, 0, 0, 0))
    o_spec = pl.BlockSpec((1, tq, GQA * HEAD_DIM), lambda b, h, i: (b, i, h))
    return pl.pallas_call(
        functools.partial(_sel_win_kernel, tq=tq),
        out_shape=(jax.ShapeDtypeStruct((B, T, D_ATT), F32), jax.ShapeDtypeStruct((B, T, D_ATT), F32)),
        grid=(B, N_KV_HEADS, T // tq),
        in_specs=[pl.BlockSpec((1, 1, GQA, tq, HEAD_DIM), lambda b, h, i: (b, h, 0, i, 0)),
                  k_spec, vt_spec, k_spec, vt_spec,
                  pl.BlockSpec((1, 1, NB, tq), lambda b, h, i: (b, h, 0, i)),
                  tz_spec(tzs), tz_spec(tzw)],
        out_specs=(o_spec, o_spec),
        compiler_params=_cparams("parallel", "parallel", "parallel"),
        name="sel_win_prompt",
    )(q5, ks, vst, kw, vwt, sel, tzs, tzw)


def _gate_expand_matrix():
    m = np.zeros((3, LANE, D_ATT), np.float32)
    for r in range(3):
        for h in range(N_HEADS):
            m[r, h * 3 + r, h * HEAD_DIM:(h + 1) * HEAD_DIM] = 1.0
    return m


def _post_mixer_kernel(y_ref, u_ref, oc_ref, os_ref, ow_ref, g_ref, x_ref, gate_ref, sh_ref, sc_ref,
                       dskip_ref, wglu_ref, bglu_ref, gexp_ref, wout_ref, lng_ref, lnb_ref,
                       wr_ref, br_ref, x1_ref, hm_ref, te_ref, tw_ref):
    y = y_ref[0] + dskip_ref[...] * u_ref[0]
    gl = jax.nn.gelu(y)
    ssm = gl * jax.nn.sigmoid(jnp.dot(gl.astype(BF16), wglu_ref[...], preferred_element_type=F32)
                              + bglu_ref[...])
    sg = jax.nn.sigmoid(g_ref[0])
    att = jnp.zeros_like(oc_ref[0])
    for r, o_ref in enumerate((oc_ref, os_ref, ow_ref)):
        att = att + jnp.dot(sg, gexp_ref[r], precision=HIGHEST, preferred_element_type=F32) * o_ref[0]
    h = (jnp.dot(ssm.astype(BF16), wout_ref[:D_SSM, :], preferred_element_type=F32)
         + jnp.dot(att.astype(BF16), wout_ref[D_SSM:, :], preferred_element_type=F32))
    z = DN_ALPHA * x_ref[0] + gate_ref[0] * h
    x1 = _layer_norm(z) * lng_ref[...] + lnb_ref[...]
    x1_ref[0] = x1
    hm = _layer_norm(x1) * (1.0 + sc_ref[0]) + sh_ref[0]
    hm_ref[0] = hm
    logits = jnp.dot(hm, wr_ref[...], precision=HIGHEST, preferred_element_type=F32) + br_ref[...]
    lane = lax.broadcasted_iota(jnp.int32, logits.shape, 1)
    work = jnp.where(lane < N_EXPERTS, logits, -jnp.inf)
    te = jnp.zeros(logits.shape, jnp.int32)
    tv = jnp.zeros(logits.shape, F32)
    for k in range(TOP_K):
        best = jnp.max(work, axis=-1, keepdims=True)
        arg = jnp.min(jnp.where(work == best, lane, LANE), axis=-1, keepdims=True)
        te = jnp.where(lane == k, arg, te)
        tv = jnp.where(lane == k, best, tv)
        work = jnp.where(lane == arg, -jnp.inf, work)
    ex = jnp.where(lane < TOP_K, jnp.exp(tv - tv[:, 0:1]), 0.0)
    te_ref[0] = te
    tw_ref[0] = ex / jnp.sum(ex, axis=-1, keepdims=True)


def _post_mixer(y, u, oc, osel, ow, g, x, gate, shift, scale, w, tm):
    B, T, D = x.shape
    R = gate.shape[1]
    rb = 1 if R == 1 else tm
    mod_map = (lambda b, i: (b, 0, 0)) if R == 1 else (lambda b, i: (b, i, 0))
    row = lambda n: pl.BlockSpec((1, tm, n), lambda b, i: (b, i, 0))
    mod = pl.BlockSpec((1, rb, D), mod_map)
    full = lambda a: pl.BlockSpec(a.shape, lambda b, i: (0,) * a.ndim)
    consts = (w['d_skip'], w['w_glu'], w['b_glu'], w['gexp'], w['w_out'], w['ln1_g'], w['ln1_b'],
              w['w_router'], w['b_router'])
    return pl.pallas_call(
        _post_mixer_kernel,
        out_shape=(jax.ShapeDtypeStruct((B, T, D), F32), jax.ShapeDtypeStruct((B, T, D), F32),
                   jax.ShapeDtypeStruct((B, T, LANE), jnp.int32), jax.ShapeDtypeStruct((B, T, LANE), F32)),
        grid=(B, T // tm),
        in_specs=[row(D_SSM), row(D_SSM), row(D_ATT), row(D_ATT), row(D_ATT), row(LANE), row(D),
                  mod, mod, mod] + [full(a) for a in consts],
        out_specs=(row(D), row(D), row(LANE), row(LANE)),
        compiler_params=_cparams("parallel", "parallel"),
        name="post_mixer",
    )(y, u, oc, osel, ow, g, x, gate, shift, scale, *consts)


def _expert_kernel(be_ref, nu_ref, x_ref, wgu_ref, bgu_ref, wd_ref, bd_ref, o_ref, wgu_s, wd_s):
    i = pl.program_id(0)
    prev = be_ref[jnp.maximum(i - 1, 0)]
    fresh = (i == 0) | (be_ref[i] != prev)

    @pl.when(fresh)
    def _():
        wgu_s[...] = wgu_ref[0].astype(BF16)
        wd_s[...] = wd_ref[0].astype(BF16)

    @pl.when(i < nu_ref[0])
    def _():
        gu = jnp.dot(x_ref[...].astype(BF16), wgu_s[...], preferred_element_type=F32) + bgu_ref[0]
        gate = jnp.minimum(gu[:, :D_FF], SWIGLU_LIMIT)
        up = jnp.clip(gu[:, D_FF:], -SWIGLU_LIMIT, SWIGLU_LIMIT)
        hh = (up + 1.0) * gate * jax.nn.sigmoid(SWIGLU_ALPHA * gate)
        o_ref[...] = jnp.dot(hh.astype(BF16), wd_s[...], preferred_element_type=F32) + bd_ref[0]

    @pl.when(i >= nu_ref[0])
    def _():
        o_ref[...] = jnp.zeros_like(o_ref)


def _experts(xb, blk_e, n_used, w_gate_up, b_gate_up, w_down, b_down):
    rows, D = xb.shape
    n_blk = rows // MOE_ROWS
    grid_spec = pltpu.PrefetchScalarGridSpec(
        num_scalar_prefetch=2,
        grid=(n_blk,),
        in_specs=[pl.BlockSpec((MOE_ROWS, D), lambda i, be, nu: (i, 0)),
                  pl.BlockSpec((1, D, 2 * D_FF), lambda i, be, nu: (be[i], 0, 0)),
                  pl.BlockSpec((1, 1, 2 * D_FF), lambda i, be, nu: (be[i], 0, 0)),
                  pl.BlockSpec((1, D_FF, D), lambda i, be, nu: (be[i], 0, 0)),
                  pl.BlockSpec((1, 1, D), lambda i, be, nu: (be[i], 0, 0))],
        out_specs=pl.BlockSpec((MOE_ROWS, D), lambda i, be, nu: (i, 0)),
        scratch_shapes=[pltpu.VMEM((D, 2 * D_FF), BF16), pltpu.VMEM((D_FF, D), BF16)],
    )
    return pl.pallas_call(
        _expert_kernel,
        out_shape=jax.ShapeDtypeStruct((rows, D), F32),
        grid_spec=grid_spec,
        compiler_params=_cparams("arbitrary"),
        name="moe_experts",
    )(blk_e, n_used, xb, w_gate_up, b_gate_up.reshape(N_EXPERTS, 1, 2 * D_FF), w_down,
      b_down.reshape(N_EXPERTS, 1, D))


def _moe_dispatch(top_e, n):
    blk = MOE_ROWS
    nk = n * TOP_K
    e = top_e.reshape(-1)
    order = jnp.argsort(e)
    e_s = e[order]
    counts = jnp.bincount(e, length=N_EXPERTS)
    pcounts = (counts + blk - 1) // blk * blk
    start = jnp.cumsum(counts) - counts
    pend = jnp.cumsum(pcounts)
    pstart = pend - pcounts
    dest_sorted = (pstart[e_s] + jnp.arange(nk) - start[e_s]).astype(jnp.int32)
    n_blk = (nk + N_EXPERTS * (blk - 1)) // blk
    rows = n_blk * blk
    row_tok = jnp.full((rows,), n, jnp.int32).at[dest_sorted].set((order // TOP_K).astype(jnp.int32))
    dest = jnp.zeros((nk,), jnp.int32).at[order].set(dest_sorted)
    blk_e = jnp.sum(pend[None, :] <= (jnp.arange(n_blk) * blk)[:, None], axis=1)
    blk_e = jnp.minimum(blk_e, N_EXPERTS - 1).astype(jnp.int32)
    n_used = (pend[-1] // blk).astype(jnp.int32).reshape(1)
    return row_tok, dest.reshape(n, TOP_K), blk_e, n_used


def _final_kernel(x_ref, y0_ref, y1_ref, y2_ref, y3_ref, tw_ref, gate_ref, lng_ref, lnb_ref, o_ref):
    tw = tw_ref[0]
    y = jnp.zeros_like(x_ref[0])
    for k, y_ref in enumerate((y0_ref, y1_ref, y2_ref, y3_ref)):
        y = y + tw[:, k:k + 1] * y_ref[0]
    z = DN_ALPHA * x_ref[0] + gate_ref[0] * y
    o_ref[0] = _layer_norm(z) * lng_ref[...] + lnb_ref[...]


def _final(x1, ys, tw, gate, ln_g, ln_b, tm):
    B, T, D = x1.shape
    R = gate.shape[1]
    rb = 1 if R == 1 else tm
    mod_map = (lambda b, i: (b, 0, 0)) if R == 1 else (lambda b, i: (b, i, 0))
    row = lambda n: pl.BlockSpec((1, tm, n), lambda b, i: (b, i, 0))
    vec = pl.BlockSpec((1, D), lambda b, i: (0, 0))
    return pl.pallas_call(
        _final_kernel,
        out_shape=jax.ShapeDtypeStruct((B, T, D), F32),
        grid=(B, T // tm),
        in_specs=[row(D), row(D), row(D), row(D), row(D), row(LANE),
                  pl.BlockSpec((1, rb, D), mod_map), vec, vec],
        out_specs=row(D),
        compiler_params=_cparams("parallel", "parallel"),
        name="moe_combine_ln",
    )(x1, *ys, tw, gate, ln_g, ln_b)


def _cmp_select_step_kernel(q_ref, kv_ref, bias_ref, pool_ref, o_ref, idx_ref, *, n_cmp, n_blk, q_pos):
    q = q_ref[0].astype(BF16)
    ncp = kv_ref.shape[1]
    nbp = pool_ref.shape[1]
    hd = HEAD_DIM
    kv = kv_ref[0]
    kb = [kv[:, h * hd:(h + 1) * hd].astype(BF16) for h in range(N_KV_HEADS)]
    vb = [kv[:, (N_KV_HEADS + h) * hd:(N_KV_HEADS + h + 1) * hd].astype(BF16) for h in range(N_KV_HEADS)]
    row = lax.broadcasted_iota(jnp.int32, (N_HEADS, 1), 0)
    first = row < GQA
    s = jnp.where(first, _nt_dot(q, kb[0]), _nt_dot(q, kb[1])) * (hd ** -0.5)
    s = s + bias_ref[...]
    ci = lax.broadcasted_iota(jnp.int32, (N_HEADS, ncp), 1)
    mask = (ci * CMP_STRIDE + CMP_BLOCK - 1 <= q_pos) & (ci < n_cmp)
    s = jnp.where(mask, s, NEG)
    m = jnp.max(s, axis=-1, keepdims=True)
    p = jnp.where(mask, jnp.exp(s - m), 0.0)
    p = p / jnp.maximum(jnp.sum(p, axis=-1, keepdims=True), 1e-30)
    pb = p.astype(BF16)
    o_ref[0] = jnp.where(first, jnp.dot(pb, vb[0], preferred_element_type=F32),
                         jnp.dot(pb, vb[1], preferred_element_type=F32))
    imp0 = jnp.sum(jnp.where(first, p, 0.0), axis=0, keepdims=True)
    imp1 = jnp.sum(jnp.where(first, 0.0, p), axis=0, keepdims=True)
    imp = jnp.where(first, imp0, imp1)
    sb = jnp.dot(imp, pool_ref[...], precision=HIGHEST, preferred_element_type=F32)
    cur = q_pos // SEL_BLOCK
    bi = lax.broadcasted_iota(jnp.int32, (nbp, nbp), 0)
    bj = lax.broadcasted_iota(jnp.int32, (nbp, nbp), 1)
    blk = lax.broadcasted_iota(jnp.int32, (1, nbp), 1)
    causal = blk <= cur
    forced = (blk == 0) | (blk == cur) | (blk == cur - 1)
    rsel = lax.broadcasted_iota(jnp.int32, (N_SEL, nbp), 0)
    for h in range(N_KV_HEADS):
        sc = jnp.where(forced & causal, 1e4, jnp.where(causal, sb[h * GQA:h * GQA + 1, :], -1.0))
        sc = jnp.where(blk < n_blk, sc, -2.0)
        scb = jnp.broadcast_to(sc, (nbp, nbp))
        col = jnp.sum(jnp.where(bi == bj, scb, 0.0), axis=1, keepdims=True)
        ahead = (col > scb) | ((col == scb) & (bi < bj))
        rank = jnp.sum(ahead.astype(jnp.int32), axis=0, keepdims=True)
        hit = jnp.broadcast_to(rank, (N_SEL, nbp)) == rsel
        idx = jnp.sum(jnp.where(hit, jnp.broadcast_to(blk, (N_SEL, nbp)), 0), axis=1, keepdims=True)
        idx_ref[0, h] = jnp.broadcast_to(idx, (N_SEL, LANE))


def _cmp_select_step(q, ckv, bias, pool, n_cmp, n_blk, q_pos):
    B = q.shape[0]
    NCp = ckv.shape[1]
    return pl.pallas_call(
        functools.partial(_cmp_select_step_kernel, n_cmp=n_cmp, n_blk=n_blk, q_pos=q_pos),
        out_shape=(jax.ShapeDtypeStruct((B, N_HEADS, HEAD_DIM), F32),
                   jax.ShapeDtypeStruct((B, N_KV_HEADS, N_SEL, LANE), jnp.int32)),
        grid=(B,),
        in_specs=[pl.BlockSpec((1, N_HEADS, HEAD_DIM), lambda b: (b, 0, 0)),
                  pl.BlockSpec((1, NCp, D_KV), lambda b: (b, 0, 0)),
                  pl.BlockSpec(bias.shape, lambda b: (0, 0)),
                  pl.BlockSpec(pool.shape, lambda b: (0, 0))],
        out_specs=(pl.BlockSpec((1, N_HEADS, HEAD_DIM), lambda b: (b, 0, 0)),
                   pl.BlockSpec((1, N_KV_HEADS, N_SEL, LANE), lambda b: (b, 0, 0, 0))),
        compiler_params=_cparams("parallel"),
        name="cmp_select_step",
    )(q, ckv, bias, pool)


def _sel_step_kernel(pg_ref, idx_ref, q_ref, *refs, n_past, q_pos):
    page_refs = refs[:N_SEL]
    new_ref, bias_ref, kpos_ref, o_ref = refs[N_SEL:]
    b, h = pl.program_id(0), pl.program_id(1)
    base = (b * N_KV_HEADS + h) * N_SEL
    kts, vts = [], []
    for j in range(N_SEL):
        is_new = idx_ref[base + j] >= n_past
        kts.append(jnp.where(is_new, new_ref[0, 0, 0], page_refs[j][0, 0, 0]))
        vts.append(jnp.where(is_new, new_ref[0, 1, 0], page_refs[j][0, 1, 0]))
    kt = jnp.concatenate(kts, axis=1).astype(BF16)
    vt = jnp.concatenate(vts, axis=1).astype(BF16)
    s = jnp.dot(q_ref[0].astype(BF16), kt, preferred_element_type=F32) * (HEAD_DIM ** -0.5) + bias_ref[0, 0]
    mask = kpos_ref[0, 0] <= q_pos
    s = jnp.where(mask, s, NEG)
    m = jnp.max(s, axis=-1, keepdims=True)
    p = jnp.where(mask, jnp.exp(s - m), 0.0)
    l = jnp.sum(p, axis=-1, keepdims=True)
    o_ref[0, 0] = _nt_dot(p.astype(BF16), vt) / jnp.maximum(l, 1e-30)


def _sel_step(q, pool_t, new_t, bias_sel, kpos, pages, idx_flat, n_past, q_pos):
    B = q.shape[0]
    nk = N_SEL * PAGE_SIZE
    slot = lambda b, h, j: (b * N_KV_HEADS + h) * N_SEL + j
    page_spec = lambda j: pl.BlockSpec((1, 2, 1, HEAD_DIM, PAGE_SIZE),
                                       lambda b, h, pg, ix, j=j: (pg[slot(b, h, j)], 0, h, 0, 0))
    grid_spec = pltpu.PrefetchScalarGridSpec(
        num_scalar_prefetch=2,
        grid=(B, N_KV_HEADS),
        in_specs=[pl.BlockSpec((1, N_HEADS, HEAD_DIM), lambda b, h, pg, ix: (b, 0, 0))]
        + [page_spec(j) for j in range(N_SEL)]
        + [pl.BlockSpec((1, 2, 1, HEAD_DIM, PAGE_SIZE), lambda b, h, pg, ix: (b, 0, h, 0, 0)),
           pl.BlockSpec((1, 1, N_HEADS, nk), lambda b, h, pg, ix: (b, h, 0, 0)),
           pl.BlockSpec((1, 1, 1, nk), lambda b, h, pg, ix: (b, h, 0, 0))],
        out_specs=pl.BlockSpec((1, 1, N_HEADS, HEAD_DIM), lambda b, h, pg, ix: (b, h, 0, 0)),
    )
    return pl.pallas_call(
        functools.partial(_sel_step_kernel, n_past=n_past, q_pos=q_pos),
        out_shape=jax.ShapeDtypeStruct((B, N_KV_HEADS, N_HEADS, HEAD_DIM), F32),
        grid_spec=grid_spec,
        compiler_params=_cparams("arbitrary", "arbitrary"),
        name="sel_step",
    )(pages, idx_flat, q, *([pool_t] * N_SEL), new_t, bias_sel, kpos)


def _win_step_kernel(q_ref, w_ref, new_ref, bias_ref, bias0_ref, o_ref):
    q = q_ref[0]
    qb = q.astype(BF16)
    row = lax.broadcasted_iota(jnp.int32, (N_HEADS, 1), 0)
    first = row < GQA
    w = w_ref[0]
    hd = HEAD_DIM
    kb = [w[:, h * hd:(h + 1) * hd].astype(BF16) for h in range(N_KV_HEADS)]
    vb = [w[:, (N_KV_HEADS + h) * hd:(N_KV_HEADS + h + 1) * hd].astype(BF16) for h in range(N_KV_HEADS)]
    s = jnp.where(first, _nt_dot(qb, kb[0]), _nt_dot(qb, kb[1])) * (hd ** -0.5) + bias_ref[...]
    new = new_ref[0]
    kn = jnp.where(first, new[:, 0:hd], new[:, hd:2 * hd])
    vn = jnp.where(first, new[:, 2 * hd:3 * hd], new[:, 3 * hd:])
    sn = jnp.sum(q * kn, axis=-1, keepdims=True) * (hd ** -0.5) + bias0_ref[...]
    m = jnp.maximum(jnp.max(s, axis=-1, keepdims=True), sn)
    p = jnp.exp(s - m)
    pn = jnp.exp(sn - m)
    l = jnp.sum(p, axis=-1, keepdims=True) + pn
    pb = p.astype(BF16)
    acc = jnp.where(first, jnp.dot(pb, vb[0], preferred_element_type=F32),
                    jnp.dot(pb, vb[1], preferred_element_type=F32)) + pn * vn
    o_ref[0] = acc / jnp.maximum(l, 1e-30)


def _win_step(q, win, new, bias, bias0):
    B, W, _ = win.shape
    return pl.pallas_call(
        _win_step_kernel,
        out_shape=jax.ShapeDtypeStruct((B, N_HEADS, HEAD_DIM), F32),
        grid=(B,),
        in_specs=[pl.BlockSpec((1, N_HEADS, HEAD_DIM), lambda b: (b, 0, 0)),
                  pl.BlockSpec((1, W, D_KV), lambda b: (b, 0, 0)),
                  pl.BlockSpec((1, 1, D_KV), lambda b: (b, 0, 0)),
                  pl.BlockSpec((N_HEADS, W), lambda b: (0, 0)),
                  pl.BlockSpec((N_HEADS, 1), lambda b: (0, 0))],
        out_specs=pl.BlockSpec((1, N_HEADS, HEAD_DIM), lambda b: (b, 0, 0)),
        compiler_params=_cparams("parallel"),
        name="win_step",
    )(q, win, new, bias, bias0)


def _split_heads(kv, dtype):
    B, L, _ = kv.shape
    kv5 = kv.reshape(B, L, 2, N_KV_HEADS, HEAD_DIM)
    return (jnp.transpose(kv5[:, :, 0], (0, 2, 1, 3)).astype(dtype),
            jnp.transpose(kv5[:, :, 1], (0, 2, 1, 3)).astype(dtype))


def _nsa_prompt(q, kvc, kvs, kvw, cmp_tab, rel_bias):
    B, T, _ = q.shape
    nc = T // CMP_STRIDE
    nb = T // SEL_BLOCK
    ckv = _compress_out(_compress_in(kvc.reshape(B, nc, CMP_STRIDE * D_KV), cmp_tab), cmp_tab, nc)
    kc, vc = _split_heads(ckv, BF16)
    vct = jnp.transpose(vc, (0, 1, 3, 2))
    bias_n = _bias_by_distance(rel_bias, T)
    n_qt, n_kt = T // ATT_TQ, T // ATT_TK
    n_ds = min(n_kt, -(-(REL_MAX_DIST + ATT_TK - 1) // ATT_TK) + 1)
    n_dw = min(n_kt, WINDOW // ATT_TK + 1)
    tzs, tzw, bias_tab = _bias_tables(bias_n, n_qt, nc // 8, n_ds, n_dw, ATT_TQ, ATT_TK)
    pool = jnp.asarray(_pool_matrix(nc, nb))
    scale = HEAD_DIM ** -0.5
    q5 = jnp.transpose((q * scale).reshape(B, T, N_KV_HEADS, GQA, HEAD_DIM), (0, 2, 3, 1, 4))
    o_cmp, sel = _cmp_select_prompt(q5, kc, vct, bias_tab, pool, nc - 1)
    ks, vs = _split_heads(kvs, BF16)
    kw, vw = _split_heads(kvw, BF16)
    o_sel, o_win = _sel_win_prompt(q5, ks, jnp.transpose(vs, (0, 1, 3, 2)), kw, jnp.transpose(vw, (0, 1, 3, 2)),
                                   sel, tzs, tzw)
    return o_cmp, o_sel, o_win


def _nsa_sample(q, kvc, kvs, kvw, pool_cmp, pool_sel, win_buf, page_table, cmp_tab, rel_bias):
    B = q.shape[0]
    n_pages = page_table.shape[1]
    past_len = n_pages * PAGE_SIZE
    q_pos = past_len
    lp = -(-(past_len + 1) // SEL_BLOCK) * SEL_BLOCK
    n_cmp = lp // CMP_STRIDE - 1
    n_blk = lp // SEL_BLOCK
    n_past_chunks = past_len // CMP_STRIDE
    n_tail = 8
    assert n_past_chunks + n_tail >= n_cmp + 1
    n_chunks = n_past_chunks + n_tail
    feature_major = lambda pool: jnp.transpose(pool, (0, 2, 3, 4, 1))
    z_past = _compress_in_paged(feature_major(pool_cmp), page_table, cmp_tab)
    tail = jnp.pad(kvc[:, None, :], ((0, 0), (0, n_tail * CMP_STRIDE - 1), (0, 0)))
    z_tail = _compress_in(tail.reshape(B, n_tail, CMP_STRIDE * D_KV), cmp_tab)
    ncp = -(-n_chunks // LANE) * LANE
    nbp = -(-n_blk // LANE) * LANE
    ckv = _compress_out(jnp.concatenate([z_past, z_tail], axis=1), cmp_tab, ncp)
    bias_n = _bias_by_distance(rel_bias, q_pos + 1)
    n_back = max((n_pages + 1) * PAGE_SIZE, ncp * CMP_STRIDE + CMP_BLOCK)
    back = jnp.concatenate([bias_n[:, ::-1], jnp.broadcast_to(bias_n[:, :1], (N_HEADS, n_back - q_pos - 1))], 1)
    bias_c = back[:, CMP_BLOCK - 1:CMP_BLOCK - 1 + ncp * CMP_STRIDE:CMP_STRIDE]
    pool = jnp.asarray(_pool_matrix(ncp, nbp).T)
    q3 = q.reshape(B, N_HEADS, HEAD_DIM)
    o_cmp, idx = _cmp_select_step(q3, ckv, bias_c, pool, n_cmp, n_blk, q_pos)
    idx = idx[..., 0]
    bpp = PAGE_SIZE // SEL_BLOCK
    n_past = n_pages * bpp
    lpage = idx // bpp
    pages = jnp.take_along_axis(page_table, jnp.minimum(lpage, n_pages - 1).reshape(B, -1), axis=1)
    new_t = jnp.pad(kvs.reshape(B, 2, N_KV_HEADS, HEAD_DIM, 1), ((0, 0),) * 4 + ((0, PAGE_SIZE - 1),))
    bias_page = jnp.transpose(back[:, :(n_pages + 1) * PAGE_SIZE].reshape(N_HEADS, n_pages + 1, PAGE_SIZE),
                              (1, 0, 2))
    bias_sel = jnp.transpose(bias_page[lpage], (0, 1, 3, 2, 4)).reshape(B, N_KV_HEADS, N_HEADS, -1)
    kpos = lpage[..., None] * PAGE_SIZE + jnp.arange(PAGE_SIZE)
    ok = (kpos // SEL_BLOCK == idx[..., None]) & (idx <= q_pos // SEL_BLOCK)[..., None]
    kpos = jnp.where(ok, kpos, q_pos + 1).reshape(B, N_KV_HEADS, 1, -1).astype(jnp.int32)
    o_sel = _sel_step(q3, feature_major(pool_sel), new_t, bias_sel, kpos, pages.reshape(-1).astype(jnp.int32),
                      idx.reshape(-1).astype(jnp.int32), n_past, q_pos)
    o_sel = jnp.concatenate([o_sel[:, h, h * GQA:(h + 1) * GQA] for h in range(N_KV_HEADS)], axis=1)
    wb = win_buf.shape[1]
    bias_w = bias_n[:, 1:wb + 1][:, ::-1]
    o_win = _win_step(q3, win_buf.reshape(B, wb, D_KV), kvw[:, None, :], bias_w, bias_n[:, 0:1])
    return o_cmp.reshape(B, D_ATT), o_sel.reshape(B, D_ATT), o_win.reshape(B, D_ATT)


def kernel(x_prompt, x_sample, cache_cmp_kv, cache_sel_kv, state_win_kv, state_ssm_re, state_ssm_im, page_table,
           c_prompt, c_sample, w_ada, b_ada, w_in, lam_re, lam_im, log_dt, b_re, b_im, c_re, c_im, d_skip,
           w_glu, b_glu, phi_pe, phi_w1, phi_b1, phi_w2, phi_b2, rel_bias, w_out, ln1_g, ln1_b,
           w_router, b_router, w_gate_up, b_gate_up, w_down, b_down, ln2_g, ln2_b):
    assert w_ada.shape[0] == DEPTH == 1
    l = 0
    Bp, T, D = x_prompt.shape
    Bs = x_sample.shape[0]
    kv_tail = (2, N_KV_HEADS, HEAD_DIM)

    n_c = Bp + Bs
    c_all = jnp.pad(jnp.concatenate([c_prompt, c_sample], 0), ((0, -n_c % 8), (0, 0)))
    m_all = _adaln(c_all, w_ada[l], b_ada[l])
    m_p = m_all[:Bp].reshape(Bp, 6, D)
    m_s = m_all[Bp:n_c].reshape(Bs, 6, D)
    mod_p = [m_p[:, i:i + 1, :] for i in range(6)]
    mod_s = [m_s[None, :, i, :] for i in range(6)]

    w_in_pad = jnp.pad(w_in[l], ((0, 0), (0, D_IN_PAD - D_IN))).astype(BF16)
    n_levels = max(1, int(math.log2(T // SSM_CHUNK)))
    ssm_tab = _ssm_tables(lam_re[l], lam_im[l], log_dt[l], b_re[l], b_im[l], c_re[l], c_im[l],
                          SSM_CHUNK, n_levels)
    cmp_tab = _compress_tables(phi_pe[l], phi_w1[l], phi_b1[l], phi_w2[l], phi_b2[l])
    w_post = dict(
        d_skip=d_skip[l].reshape(1, D_SSM), w_glu=w_glu[l].astype(BF16), b_glu=b_glu[l].reshape(1, D_SSM),
        gexp=jnp.asarray(_gate_expand_matrix()), w_out=w_out[l].astype(BF16),
        ln1_g=ln1_g[l].reshape(1, D), ln1_b=ln1_b[l].reshape(1, D),
        w_router=jnp.pad(w_router[l], ((0, 0), (0, LANE - N_EXPERTS))),
        b_router=jnp.pad(b_router[l], (0, LANE - N_EXPERTS)).reshape(1, LANE))

    u, q, kvc, kvs, kvw, g = _mixer_in(x_prompt, mod_p[0], mod_p[1], w_in_pad, tm=512)
    y_ssm, h_p = _ssm_prompt(u, ssm_tab)
    o_cmp, o_sel, o_win = _nsa_prompt(q, kvc, kvs, kvw, cmp_tab, rel_bias)
    x1_p, hm_p, te_p, tw_p = _post_mixer(y_ssm, u, o_cmp, o_sel, o_win, g, x_prompt,
                                         mod_p[2], mod_p[3], mod_p[4], w_post, tm=256)

    u_s, q_s, kvc_s, kvs_s, kvw_s, g_s = _mixer_in(x_sample.reshape(1, Bs, D), mod_s[0], mod_s[1],
                                                   w_in_pad, tm=Bs)
    y_s, h_s = _ssm_sample(u_s[0], state_ssm_re[l], state_ssm_im[l], ssm_tab, c_re[l], c_im[l])
    oc_s, os_s, ow_s = _nsa_sample(q_s[0].astype(F32), kvc_s[0], kvs_s[0], kvw_s[0], cache_cmp_kv[l],
                                   cache_sel_kv[l], state_win_kv[l], page_table, cmp_tab, rel_bias)
    x1_s, hm_s, te_s, tw_s = _post_mixer(y_s[None], u_s, oc_s[None], os_s[None], ow_s[None], g_s,
                                         x_sample.reshape(1, Bs, D), mod_s[2], mod_s[3], mod_s[4],
                                         w_post, tm=Bs)

    n_p = Bp * T
    n_all = n_p + Bs
    hm_all = jnp.concatenate([hm_p.reshape(n_p, D), hm_s.reshape(Bs, D)], 0)
    te_all = jnp.concatenate([te_p.reshape(n_p, LANE), te_s.reshape(Bs, LANE)], 0)[:, :TOP_K]
    row_tok, dest, blk_e, n_used = _moe_dispatch(te_all, n_all)
    xb = jnp.concatenate([hm_all, jnp.zeros((1, D), F32)], 0)[row_tok]
    yb = _experts(xb, blk_e, n_used, w_gate_up[l], b_gate_up[l], w_down[l], b_down[l])
    ys = [yb[dest[:, k]] for k in range(TOP_K)]
    ln2g, ln2b = ln2_g[l].reshape(1, D), ln2_b[l].reshape(1, D)
    out_p = _final(x1_p, [y[:n_p].reshape(Bp, T, D) for y in ys], tw_p, mod_p[5], ln2g, ln2b, tm=512)
    out_s = _final(x1_s, [y[n_p:].reshape(1, Bs, D) for y in ys], tw_s, mod_s[5], ln2g, ln2b, tm=Bs)

    wlen = min(WINDOW, T)
    win_s = jnp.concatenate([state_win_kv[l], kvw_s[0].reshape(Bs, 1, *kv_tail)], 1)[:, -state_win_kv.shape[2]:]
    p_state = SSM_STATE
    return (out_p, out_s.reshape(Bs, 1, D),
            kvc.reshape(1, Bp, T, *kv_tail), kvc_s[0].reshape(1, Bs, 1, *kv_tail),
            kvs.reshape(1, Bp, T, *kv_tail), kvs_s[0].reshape(1, Bs, 1, *kv_tail),
            kvw[:, T - wlen:].reshape(1, Bp, wlen, *kv_tail), win_s[None],
            h_p[None, ..., :p_state], h_p[None, ..., p_state:],
            h_s[None, ..., :p_state], h_s[None, ..., p_state:])
```

```python
import functools
import math

import numpy as np
import jax
import jax.numpy as jnp
from jax import lax
from jax.experimental import pallas as pl
from jax.experimental.pallas import tpu as pltpu

D_MODEL = 1024
DEPTH = 1
PAST_LEN = 16384
PAGE_SIZE = 128
D_SSM = 512
SSM_GROUP = 16
N_SSM_GROUPS = D_SSM // SSM_GROUP
SSM_STATE = 64
N_HEADS = 8
HEAD_DIM = 64
N_KV_HEADS = 2
GQA = N_HEADS // N_KV_HEADS
D_ATT = N_HEADS * HEAD_DIM
D_KV = 2 * N_KV_HEADS * HEAD_DIM
CMP_STRIDE = 16
CMP_BLOCK = 2 * CMP_STRIDE
SEL_BLOCK = 64
N_SEL = 16
WINDOW = 512
NUM_BUCKETS = 32
REL_MAX_DIST = 1024
N_EXPERTS = 32
TOP_K = 4
D_FF = 1024
SWIGLU_LIMIT = 7.0
SWIGLU_ALPHA = 1.702
DN_ALPHA = (2 * DEPTH) ** 0.25
D_IN = D_SSM + D_ATT + 3 * D_KV + 3 * N_HEADS
NEG = -1e30
F32 = jnp.float32
BF16 = jnp.bfloat16
HIGHEST = lax.Precision.HIGHEST

LANE = 128
D_IN_PAD = 1920
GATE_COL = D_SSM + D_ATT + 3 * D_KV
SSM_CHUNK = 16
ATT_TQ = 128
ATT_TK = 128
SEL_CHAINS = 4
MOE_ROWS = 256
PAGES_PER_STEP = 32
VMEM_LIMIT = 48 * 1024 * 1024
LN_EPS = 1e-5


def _cparams(*sem):
    return pltpu.CompilerParams(dimension_semantics=sem, vmem_limit_bytes=VMEM_LIMIT)


def _nt_dot(a, b):
    return lax.dot_general(a, b, (((1,), (1,)), ((), ())), preferred_element_type=F32)


def _layer_norm(x):
    mu = jnp.mean(x, axis=-1, keepdims=True)
    xc = x - mu
    var = jnp.mean(xc * xc, axis=-1, keepdims=True)
    return xc * lax.rsqrt(var + LN_EPS)


def _adaln_kernel(c_ref, w_ref, b_ref, o_ref):
    c = c_ref[...]
    s = c * jax.nn.sigmoid(c)
    o_ref[...] = jnp.dot(s, w_ref[...], precision=HIGHEST, preferred_element_type=F32) + b_ref[...]


def _adaln(c, w, b):
    n, d = c.shape
    dout = w.shape[1]
    tn = 1024
    return pl.pallas_call(
        _adaln_kernel,
        out_shape=jax.ShapeDtypeStruct((n, dout), F32),
        grid=(dout // tn,),
        in_specs=[pl.BlockSpec((n, d), lambda j: (0, 0)),
                  pl.BlockSpec((d, tn), lambda j: (0, j)),
                  pl.BlockSpec((1, tn), lambda j: (0, j))],
        out_specs=pl.BlockSpec((n, tn), lambda j: (0, j)),
        compiler_params=_cparams("arbitrary"),
        name="adaln",
    )(c, w, b.reshape(1, dout))


def _mixer_in_kernel(x_ref, sh_ref, sc_ref, w_ref, u_ref, q_ref, kvc_ref, kvs_ref, kvw_ref, g_ref):
    h = _layer_norm(x_ref[0]) * (1.0 + sc_ref[0]) + sh_ref[0]
    z = jnp.dot(h.astype(BF16), w_ref[...], preferred_element_type=F32)
    c0 = D_SSM
    c1 = c0 + D_ATT
    c2 = c1 + D_KV
    c3 = c2 + D_KV
    c4 = c3 + D_KV
    u_ref[0] = z[:, :c0]
    q_ref[0] = z[:, c0:c1].astype(BF16)
    kvc_ref[0] = z[:, c1:c2]
    kvs_ref[0] = z[:, c2:c3]
    kvw_ref[0] = z[:, c3:c4]
    g_ref[0] = z[:, c4:c4 + LANE]


def _mixer_in(x, shift, scale, w_pad, tm):
    B, T, D = x.shape
    R = shift.shape[1]
    rb = 1 if R == 1 else tm
    mod_map = (lambda b, i: (b, 0, 0)) if R == 1 else (lambda b, i: (b, i, 0))
    row = lambda n: pl.BlockSpec((1, tm, n), lambda b, i: (b, i, 0))
    outs = (jax.ShapeDtypeStruct((B, T, D_SSM), F32), jax.ShapeDtypeStruct((B, T, D_ATT), BF16),
            jax.ShapeDtypeStruct((B, T, D_KV), F32), jax.ShapeDtypeStruct((B, T, D_KV), F32),
            jax.ShapeDtypeStruct((B, T, D_KV), F32), jax.ShapeDtypeStruct((B, T, LANE), F32))
    return pl.pallas_call(
        _mixer_in_kernel,
        out_shape=outs,
        grid=(B, T // tm),
        in_specs=[row(D), pl.BlockSpec((1, rb, D), mod_map), pl.BlockSpec((1, rb, D), mod_map),
                  pl.BlockSpec((D, D_IN_PAD), lambda b, i: (0, 0))],
        out_specs=(row(D_SSM), row(D_ATT), row(D_KV), row(D_KV), row(D_KV), row(LANE)),
        compiler_params=_cparams("parallel", "parallel"),
        name="mixer_in",
    )(x, shift, scale, w_pad)


def _ssm_tables(lam_re, lam_im, log_dt, b_re, b_im, c_re, c_im, L, n_levels):
    G, P = lam_re.shape
    C = b_re.shape[-1]
    dt = jnp.exp(log_dt.astype(F32))[:, None]
    er, ei = lam_re * dt, lam_im * dt

    def power(k):
        kk = k.astype(F32)[:, None, None]
        mag = jnp.exp(kk * er)
        return mag * jnp.cos(kk * ei), mag * jnp.sin(kk * ei)

    lb_re, lb_im = power(jnp.ones((1,), F32))
    nr, ni = lb_re[0] - 1.0, lb_im[0]
    den = lam_re * lam_re + lam_im * lam_im
    fr = (nr * lam_re + ni * lam_im) / den
    fi = (ni * lam_re - nr * lam_im) / den
    bbr = fr[:, :, None] * b_re - fi[:, :, None] * b_im
    bbi = fr[:, :, None] * b_im + fi[:, :, None] * b_re
    pr, pi = power(jnp.arange(L + 1))
    clr = c_re[None] * pr[:, :, None, :] - c_im[None] * pi[:, :, None, :]
    cli = c_re[None] * pi[:, :, None, :] + c_im[None] * pr[:, :, None, :]
    kern = (jnp.einsum('kgcp,gpd->kgcd', clr[:L], bbr, precision=HIGHEST)
            - jnp.einsum('kgcp,gpd->kgcd', cli[:L], bbi, precision=HIGHEST))
    kz = jnp.concatenate([kern, jnp.zeros((1,) + kern.shape[1:], F32)], 0)
    ts = np.arange(L)
    lag = ts[None, :] - ts[:, None]
    lag = np.where(lag >= 0, lag, L)
    toep = kz[lag]
    toep = jnp.transpose(toep, (2, 0, 4, 1, 3)).reshape(G, L * C, L * C)
    rev = L - 1 - ts
    wsr = pr[rev][:, :, :, None] * bbr[None] - pi[rev][:, :, :, None] * bbi[None]
    wsi = pr[rev][:, :, :, None] * bbi[None] + pi[rev][:, :, :, None] * bbr[None]
    ws = jnp.concatenate([jnp.transpose(wsr, (1, 0, 3, 2)), jnp.transpose(wsi, (1, 0, 3, 2))], -1)
    ws = ws.reshape(G, L * C, 2 * P)
    wy = jnp.concatenate([jnp.transpose(clr[1:], (1, 3, 0, 2)), -jnp.transpose(cli[1:], (1, 3, 0, 2))], 1)
    wy = wy.reshape(G, 2 * P, L * C)
    lr, li = power(L * (2 ** jnp.arange(n_levels)))
    ar = jnp.transpose(jnp.concatenate([lr, lr], -1), (1, 0, 2))
    ai = jnp.transpose(jnp.concatenate([-li, li], -1), (1, 0, 2))
    return toep.astype(BF16), ws.astype(BF16), wy.astype(BF16), ar, ai, (lb_re[0], lb_im[0], bbr, bbi)


def _ssm_kernel(u_ref, toep_ref, ws_ref, wy_ref, ar_ref, ai_ref, y_ref, hl_ref, *, nb, nc, n_levels):
    u = u_ref[0]
    y1 = jnp.dot(u, toep_ref[0], preferred_element_type=F32)
    s = jnp.dot(u, ws_ref[0], preferred_element_type=F32)
    p2 = s.shape[-1]
    rows = lax.broadcasted_iota(jnp.int32, (nc, p2), 0)
    prev = []
    for b in range(nb):
        h = s[b * nc:(b + 1) * nc]
        for k in range(n_levels):
            d = 1 << k
            sh = jnp.where(rows >= d, pltpu.roll(h, d, axis=0), 0.0)
            sw = pltpu.roll(sh, p2 // 2, axis=1)
            h = h + ar_ref[0, k:k + 1, :] * sh + ai_ref[0, k:k + 1, :] * sw
        hl_ref[0, b:b + 1, :] = h[nc - 1:nc, :]
        prev.append(jnp.where(rows >= 1, pltpu.roll(h, 1, axis=0), 0.0))
    hp = jnp.concatenate(prev, axis=0)
    y2 = jnp.dot(hp.astype(BF16), wy_ref[0], preferred_element_type=F32)
    y_ref[0] = y1 + y2


def _ssm_prompt(u, tables):
    toep, ws, wy, ar, ai, _ = tables
    B, T, _ = u.shape
    G, C, L = N_SSM_GROUPS, SSM_GROUP, SSM_CHUNK
    nc = T // L
    n_levels = ar.shape[1]
    ug = jnp.transpose(u.reshape(B, nc, L, G, C), (3, 0, 1, 2, 4)).reshape(G, B * nc, L * C).astype(BF16)
    grp = lambda r, c: pl.BlockSpec((1, r, c), lambda g: (g, 0, 0))
    y, hl = pl.pallas_call(
        functools.partial(_ssm_kernel, nb=B, nc=nc, n_levels=n_levels),
        out_shape=(jax.ShapeDtypeStruct((G, B * nc, L * C), F32),
                   jax.ShapeDtypeStruct((G, B, 2 * SSM_STATE), F32)),
        grid=(G,),
        in_specs=[grp(B * nc, L * C), grp(L * C, L * C), grp(L * C, 2 * SSM_STATE),
                  grp(2 * SSM_STATE, L * C), grp(n_levels, 2 * SSM_STATE), grp(n_levels, 2 * SSM_STATE)],
        out_specs=(grp(B * nc, L * C), grp(B, 2 * SSM_STATE)),
        compiler_params=_cparams("parallel"),
        name="ssm_prompt",
    )(ug, toep, ws, wy, ar, ai)
    y = jnp.transpose(y.reshape(G, B, nc, L, C), (1, 2, 3, 0, 4)).reshape(B, T, D_SSM)
    return y, jnp.transpose(hl, (1, 0, 2))


def _ssm_step_kernel(u_ref, h0_ref, bb_ref, lr_ref, li_ref, cy_ref, y_ref, h_ref):
    p = lr_ref.shape[-1] // 2
    bu = jnp.einsum('gbc,gcp->gbp', u_ref[...], bb_ref[...], preferred_element_type=F32)
    h0 = h0_ref[...]
    h0s = jnp.concatenate([h0[..., p:], h0[..., :p]], axis=-1)
    h = lr_ref[...] * h0 + li_ref[...] * h0s + bu
    h_ref[...] = h
    y_ref[...] = jnp.einsum('gbp,gpc->gbc', h.astype(BF16), cy_ref[...], preferred_element_type=F32)


def _ssm_sample(u, h0_re, h0_im, tables, c_re, c_im):
    lb_re, lb_im, bbr, bbi = tables[-1]
    B = u.shape[0]
    G, C, P = N_SSM_GROUPS, SSM_GROUP, SSM_STATE
    ug = jnp.transpose(u.reshape(B, G, C), (1, 0, 2)).astype(BF16)
    h0 = jnp.transpose(jnp.concatenate([h0_re, h0_im], -1), (1, 0, 2)).astype(F32)
    bb = jnp.concatenate([jnp.transpose(bbr, (0, 2, 1)), jnp.transpose(bbi, (0, 2, 1))], -1).astype(BF16)
    lr = jnp.concatenate([lb_re, lb_re], -1)[:, None, :]
    li = jnp.concatenate([-lb_im, lb_im], -1)[:, None, :]
    cy = jnp.concatenate([jnp.transpose(c_re, (0, 2, 1)), -jnp.transpose(c_im, (0, 2, 1))], 1).astype(BF16)
    y, h = pl.pallas_call(
        _ssm_step_kernel,
        out_shape=(jax.ShapeDtypeStruct((G, B, C), F32), jax.ShapeDtypeStruct((G, B, 2 * P), F32)),
        name="ssm_step",
    )(ug, h0, bb, lr, li, cy)
    return jnp.transpose(y, (1, 0, 2)).reshape(B, D_SSM), jnp.transpose(h, (1, 0, 2))


def _compress_tables(phi_pe, phi_w1, phi_b1, phi_w2, phi_b2):
    S, H, Dh = CMP_STRIDE, N_KV_HEADS, HEAD_DIM
    w1 = phi_w1.reshape(2, 2, S, Dh, Dh)
    eye_c = jnp.eye(2, dtype=F32)
    eye_h = jnp.eye(H, dtype=F32)
    wbig = jnp.einsum('cajde,xc,yh->jxydache', w1, eye_c, eye_h).reshape(S * 2 * H * Dh, 2 * 2 * H * Dh)
    pe = jnp.transpose(phi_pe.reshape(2, 2, S, Dh), (1, 2, 0, 3))
    pe_rows = jnp.broadcast_to(pe[:, :, :, None, :], (2, S, 2, H, Dh)).reshape(2, 1, S * 2 * H * Dh)
    b1 = jnp.broadcast_to(phi_b1[:, None, :], (2, H, Dh)).reshape(1, 2 * H * Dh)
    w2 = jnp.einsum('cef,cx,hy->chexyf', phi_w2, eye_c, eye_h).reshape(2 * H * Dh, 2 * H * Dh)
    b2 = jnp.broadcast_to(phi_b2[:, None, :], (2, H, Dh)).reshape(1, 2 * H * Dh)
    return wbig.astype(BF16), pe_rows, b1, w2.astype(BF16), b2


def _compress_in_kernel(x_ref, pe_ref, w_ref, z_ref):
    x = x_ref[0]
    n = w_ref.shape[1] // 2
    z_ref[0, :, :n] = jnp.dot((x + pe_ref[0]).astype(BF16), w_ref[:, :n], preferred_element_type=F32)
    z_ref[0, :, n:] = jnp.dot((x + pe_ref[1]).astype(BF16), w_ref[:, n:], preferred_element_type=F32)


def _compress_in(x2, tables):
    wbig, pe_rows = tables[0], tables[1]
    N2 = wbig.shape[1]
    B, n, K = x2.shape
    tr = math.gcd(n, 256)
    return pl.pallas_call(
        _compress_in_kernel,
        out_shape=jax.ShapeDtypeStruct((B, n, N2), F32),
        grid=(B, n // tr),
        in_specs=[pl.BlockSpec((1, tr, K), lambda b, i: (b, i, 0)),
                  pl.BlockSpec((2, 1, K), lambda b, i: (0, 0, 0)),
                  pl.BlockSpec((K, N2), lambda b, i: (0, 0))],
        out_specs=pl.BlockSpec((1, tr, N2), lambda b, i: (b, i, 0)),
        compiler_params=_cparams("parallel", "parallel"),
        name="compress_in",
    )(x2, pe_rows, wbig)


def _compress_in_paged_kernel(pt_ref, *refs, n_pg):
    x_refs = refs[:n_pg]
    pe_ref, w_ref, z_ref = refs[n_pg:n_pg + 3]
    s_refs = refs[n_pg + 3:]
    for k in range(n_pg):
        t = x_refs[k][0].reshape(D_KV, PAGE_SIZE).T
        for c, s_ref in enumerate(s_refs):
            s_ref[k * PAGE_SIZE:(k + 1) * PAGE_SIZE, :] = t[:, c * LANE:(c + 1) * LANE]
    rows = n_pg * PAGE_SIZE // CMP_STRIDE
    n = w_ref.shape[1] // 2
    za = jnp.zeros((rows, n), F32)
    zb = jnp.zeros((rows, n), F32)
    for j in range(CMP_STRIDE):
        xj = jnp.concatenate([s_ref[pl.ds(j, rows, stride=CMP_STRIDE), :] for s_ref in s_refs], axis=1)
        col = slice(j * D_KV, (j + 1) * D_KV)
        za = za + jnp.dot((xj + pe_ref[0, :, col]).astype(BF16), w_ref[col, :n], preferred_element_type=F32)
        zb = zb + jnp.dot((xj + pe_ref[1, :, col]).astype(BF16), w_ref[col, n:], preferred_element_type=F32)
    z_ref[0, :, :n] = za
    z_ref[0, :, n:] = zb


def _compress_in_paged(pool_t, page_table, tables):
    wbig, pe_rows = tables[0], tables[1]
    N2 = wbig.shape[1]
    K = wbig.shape[0]
    B, n_pages = page_table.shape
    n_pg = math.gcd(n_pages, PAGES_PER_STEP)
    rows = n_pg * PAGE_SIZE // CMP_STRIDE
    page_spec = lambda k: pl.BlockSpec((1,) + pool_t.shape[1:],
                                       lambda b, i, pt, k=k: (pt[b, i * n_pg + k], 0, 0, 0, 0))
    grid_spec = pltpu.PrefetchScalarGridSpec(
        num_scalar_prefetch=1,
        grid=(B, n_pages // n_pg),
        in_specs=[page_spec(k) for k in range(n_pg)] + [
            pl.BlockSpec((2, 1, K), lambda b, i, pt: (0, 0, 0)),
            pl.BlockSpec((K, N2), lambda b, i, pt: (0, 0))],
        out_specs=pl.BlockSpec((1, rows, N2), lambda b, i, pt: (b, i, 0)),
        scratch_shapes=[pltpu.VMEM((n_pg * PAGE_SIZE, LANE), F32) for _ in range(D_KV // LANE)],
    )
    return pl.pallas_call(
        functools.partial(_compress_in_paged_kernel, n_pg=n_pg),
        out_shape=jax.ShapeDtypeStruct((B, n_pages * PAGE_SIZE // CMP_STRIDE, N2), F32),
        grid_spec=grid_spec,
        compiler_params=_cparams("arbitrary", "arbitrary"),
        name="compress_in_paged",
    )(page_table, *([pool_t] * n_pg), pe_rows, wbig)


def _compress_out_kernel(z_ref, b1_ref, w2_ref, b2_ref, o_ref):
    z = z_ref[0]
    n = z.shape[-1] // 2
    rows = z.shape[0]
    second = pltpu.roll(z[:, n:], rows - 1, axis=0)
    hdn = jax.nn.gelu(z[:, :n] + second + b1_ref[...])
    o_ref[0, :rows, :] = jnp.dot(hdn.astype(BF16), w2_ref[...], preferred_element_type=F32) + b2_ref[...]
    if o_ref.shape[1] > rows:
        o_ref[0, rows:, :] = jnp.zeros((o_ref.shape[1] - rows, n), F32)


def _compress_out(z, tables, n_out):
    _, _, b1, w2, b2 = tables
    B, n, N2 = z.shape
    return pl.pallas_call(
        _compress_out_kernel,
        out_shape=jax.ShapeDtypeStruct((B, n_out, N2 // 2), F32),
        grid=(B,),
        in_specs=[pl.BlockSpec((1, n, N2), lambda b: (b, 0, 0)),
                  pl.BlockSpec((1, N2 // 2), lambda b: (0, 0)),
                  pl.BlockSpec((N2 // 2, N2 // 2), lambda b: (0, 0)),
                  pl.BlockSpec((1, N2 // 2), lambda b: (0, 0))],
        out_specs=pl.BlockSpec((1, n_out, N2 // 2), lambda b: (b, 0, 0)),
        compiler_params=_cparams("parallel"),
        name="compress_out",
    )(z, b1, w2, b2)


def _rel_bucket(dist):
    n = jnp.maximum(dist, 0)
    max_exact = NUM_BUCKETS // 2
    nf = jnp.maximum(n, 1).astype(F32)
    large = max_exact + (jnp.log(nf / max_exact) / math.log(REL_MAX_DIST / max_exact)
                         * (NUM_BUCKETS - max_exact)).astype(jnp.int32)
    large = jnp.minimum(large, NUM_BUCKETS - 1)
    return jnp.where(n < max_exact, n, large)


def _bias_by_distance(rel_bias, n_max):
    onehot = (_rel_bucket(jnp.arange(n_max))[None, :] == jnp.arange(NUM_BUCKETS)[:, None]).astype(F32)
    return jnp.dot(jnp.transpose(rel_bias.astype(F32)), onehot, precision=HIGHEST)


def _shifted_chunks(bias_n, pad, n_chunks, width):
    n = min(bias_n.shape[1], n_chunks * width - pad)
    ext = jnp.concatenate([jnp.broadcast_to(bias_n[:, :1], (N_HEADS, pad)), bias_n[:, :n],
                           jnp.zeros((N_HEADS, n_chunks * width - pad - n), F32)], axis=1)
    return ext.reshape(N_HEADS, n_chunks, width)


def _bias_tables_kernel(ed_ref, ec_ref, tzs_ref, tzw_ref, cmp_ref, *, tq, tk, n_qt):
    n_ds, n_dw, n_j = tzs_ref.shape[1] - 1, tzw_ref.shape[1] - 1, cmp_ref.shape[1] // 8
    tzs_ref[0, n_ds] = jnp.full((tk, tq), NEG, F32)
    tzw_ref[0, n_dw] = jnp.full((tk, tq), NEG, F32)
    w = tq + tk
    c = lax.broadcasted_iota(jnp.int32, (tk, tq), 0)
    r = lax.broadcasted_iota(jnp.int32, (tk, tq), 1)
    for d in range(n_ds):
        v = jnp.concatenate([ed_ref[0, d:d + 1, :], ed_ref[0, d + 1:d + 2, :]], axis=1)
        t = pltpu.roll(jnp.broadcast_to(v, (tk, w)), w - (tk - 1), axis=1, stride=1, stride_axis=0)[:, :tq]
        dist = d * tk + r - c
        tzs_ref[0, d] = jnp.where(dist >= 0, t, NEG)
        if d < n_dw:
            tzw_ref[0, d] = jnp.where((dist >= 0) & (dist <= WINDOW), t, NEG)
    for j in range(n_j):
        dd = n_qt - 1 - j
        c0, c1 = max(dd, 0), max(dd + 1, 0)
        v = jnp.concatenate([ec_ref[0, c0:c0 + 1, :], ec_ref[0, c1:c1 + 1, :]], axis=1)
        t = pltpu.roll(jnp.broadcast_to(v, (8, w)), w - 7 * CMP_STRIDE, axis=1, stride=CMP_STRIDE, stride_axis=0)
        cmp_ref[0, j * 8:(j + 1) * 8, :] = t[:, :tq]


def _bias_tables(bias_n, n_qt, n_rb, n_ds, n_dw, tq, tk):
    assert tq == tk == 8 * CMP_STRIDE and n_dw <= n_ds
    n_j = n_rb + n_qt - 1
    ed = _shifted_chunks(bias_n, tk - 1, n_ds + 1, tq)
    ec = _shifted_chunks(bias_n, 7 * CMP_STRIDE + CMP_BLOCK - 1, n_qt + 1, tq)
    head = lambda a: pl.BlockSpec((1,) + a.shape[1:], lambda h: (h,) + (0,) * (a.ndim - 1))
    outs = (jax.ShapeDtypeStruct((N_HEADS, n_ds + 1, tk, tq), F32),
            jax.ShapeDtypeStruct((N_HEADS, n_dw + 1, tk, tq), F32),
            jax.ShapeDtypeStruct((N_HEADS, n_j * 8, tq), F32))
    tzs, tzw, cmp = pl.pallas_call(
        functools.partial(_bias_tables_kernel, tq=tq, tk=tk, n_qt=n_qt),
        out_shape=outs,
        grid=(N_HEADS,),
        in_specs=[head(ed), head(ec)],
        out_specs=tuple(head(o) for o in outs),
        compiler_params=_cparams("parallel"),
        name="bias_tables",
    )(ed, ec)
    grp = lambda a: a.reshape((N_KV_HEADS, GQA) + a.shape[1:])
    return grp(tzs), grp(tzw), cmp


def _pool_matrix(n_cmp_pad, n_blk_pad):
    r = SEL_BLOCK // CMP_STRIDE
    i = np.arange(n_cmp_pad)[None, :]
    j = np.arange(n_blk_pad)[:, None]
    return ((i >= r * j - 1) & (i <= r * j + r - 1)).astype(np.float32)


def _cmp_select_kernel(q_ref, k_ref, vt_ref, bias_ref, pool_ref, o_ref, sel_ref, *, tq, n_cmp):
    qt = pl.program_id(2)
    n_qt = pl.num_programs(2)
    q = q_ref[0, 0].reshape(GQA * tq, HEAD_DIM)
    k = k_ref[0, 0]
    nc = k.shape[0]
    s = _nt_dot(k, q)
    row0 = pl.multiple_of((n_qt - 1 - qt) * 8, 8)
    s = s + jnp.concatenate([bias_ref[g, pl.ds(row0, nc), :] for g in range(GQA)], axis=-1)
    t_pos = qt * tq + (lax.broadcasted_iota(jnp.int32, (nc, GQA * tq), 1) % tq)
    ci = lax.broadcasted_iota(jnp.int32, (nc, GQA * tq), 0)
    mask = (ci * CMP_STRIDE + CMP_BLOCK - 1 <= t_pos) & (ci < n_cmp)
    s = jnp.where(mask, s, NEG)
    m = jnp.max(s, axis=0, keepdims=True)
    p = jnp.where(mask, jnp.exp(s - m), 0.0)
    p = p / jnp.maximum(jnp.sum(p, axis=0, keepdims=True), 1e-30)
    ot = jnp.dot(vt_ref[0, 0], p.astype(BF16), preferred_element_type=F32)
    o_ref[0] = jnp.concatenate([ot[:, g * tq:(g + 1) * tq].T for g in range(GQA)], axis=-1)
    imp = p[:, 0:tq]
    for g in range(1, GQA):
        imp = imp + p[:, g * tq:(g + 1) * tq]
    sb = jnp.dot(pool_ref[...], imp, precision=HIGHEST, preferred_element_type=F32)
    nb = sb.shape[0]
    blk = lax.broadcasted_iota(jnp.int32, (nb, tq), 0)
    cur = (qt * tq + lax.broadcasted_iota(jnp.int32, (nb, tq), 1)) // SEL_BLOCK
    causal = blk <= cur
    forced = (blk == 0) | (blk == cur) | (blk == cur - 1)
    sc = jnp.where(forced & causal, 1e4, jnp.where(causal, sb, -1.0))
    rank = jnp.zeros((nb, tq), jnp.int32)
    for i in range(nb):
        row = sc[i:i + 1, :]
        ahead = (row > sc) | ((row == sc) & (blk > i))
        rank = rank + ahead.astype(jnp.int32)
    sel_ref[0, 0] = jnp.where((rank < N_SEL) & causal, 0.0, NEG)


def _cmp_select_prompt(q5, kc, vct, bias_tab, pool, n_cmp):
    B, _, _, T, _ = q5.shape
    NC = kc.shape[2]
    NB = pool.shape[0]
    R = bias_tab.shape[1]
    tq = ATT_TQ
    return pl.pallas_call(
        functools.partial(_cmp_select_kernel, tq=tq, n_cmp=n_cmp),
        out_shape=(jax.ShapeDtypeStruct((B, T, D_ATT), F32),
                   jax.ShapeDtypeStruct((B, N_KV_HEADS, NB, T), F32)),
        grid=(B, N_KV_HEADS, T // tq),
        in_specs=[pl.BlockSpec((1, 1, GQA, tq, HEAD_DIM), lambda b, h, i: (b, h, 0, i, 0)),
                  pl.BlockSpec((1, 1, NC, HEAD_DIM), lambda b, h, i: (b, h, 0, 0)),
                  pl.BlockSpec((1, 1, HEAD_DIM, NC), lambda b, h, i: (b, h, 0, 0)),
                  pl.BlockSpec((GQA, R, tq), lambda b, h, i: (h, 0, 0)),
                  pl.BlockSpec((NB, NC), lambda b, h, i: (0, 0))],
        out_specs=(pl.BlockSpec((1, tq, GQA * HEAD_DIM), lambda b, h, i: (b, i, h)),
                   pl.BlockSpec((1, 1, NB, tq), lambda b, h, i: (b, h, 0, i))),
        compiler_params=_cparams("parallel", "parallel", "parallel"),
        name="cmp_select_prompt",
    )(q5, kc, vct, bias_tab, pool)


def _sel_win_kernel(q_ref, ks_ref, vst_ref, kw_ref, vwt_ref, sel_ref, tzs_ref, tzw_ref, os_ref, ow_ref, *, tq):
    tk = ATT_TK
    qt = pl.program_id(2)
    q = q_ref[0, 0].reshape(GQA * tq, HEAD_DIM)
    width = GQA * tq
    per_tile = tk // SEL_BLOCK

    def make_sweep(k_ref, vt_ref, tz_ref, use_sel, n_chains):
        n_d = tz_ref.shape[2] - 1

        def scores(kt, hi):
            pad = kt > hi
            kt = jnp.minimum(kt, hi)
            off = pl.multiple_of(kt * tk, tk)
            k = k_ref[0, 0, pl.ds(off, tk), :]
            d = jnp.where(pad, n_d, jnp.minimum(qt - kt, n_d - 1))
            bias = [tz_ref[0, g, d] for g in range(GQA)]
            if use_sel:
                rows = sel_ref[0, 0, pl.ds(kt * per_tile, per_tile), :]
                selb = jnp.concatenate([jnp.broadcast_to(rows[i:i + 1], (SEL_BLOCK, tq))
                                        for i in range(per_tile)], axis=0)
                bias = [b + selb for b in bias]
            return _nt_dot(k, q) + jnp.concatenate(bias, axis=1)

        def values_t(kt, lo, hi):
            off = pl.multiple_of(jnp.clip(kt, lo, hi) * tk, tk)
            return vt_ref[0, 0, :, pl.ds(off, tk)]

        def sweep(lo, hi):
            n_trips = (hi - lo + n_chains) // n_chains
            chain0 = (jnp.full((1, width), 0.5 * NEG, F32), jnp.zeros((1, width), F32),
                      jnp.zeros((HEAD_DIM, width), F32), jnp.ones((1, width), F32), jnp.zeros((tk, width), BF16))

            def trip(i, chains):
                kt = lo + n_chains * i
                pv = [jnp.dot(values_t(kt - n_chains + c, lo, hi), chains[c][4], preferred_element_type=F32)
                      for c in range(n_chains)]
                ss = [scores(kt + c, hi) for c in range(n_chains)]
                out = []
                for c in range(n_chains):
                    m, l, acc, alpha_prev, _ = chains[c]
                    m_new = jnp.maximum(m, jnp.max(ss[c], axis=0, keepdims=True))
                    alpha = jnp.exp(m - m_new)
                    p = jnp.exp(ss[c] - m_new)
                    l = alpha * l + jnp.sum(p, axis=0, keepdims=True)
                    out.append((m_new, l, alpha_prev * acc + pv[c], alpha, p.astype(BF16)))
                return tuple(out)

            chains = lax.fori_loop(0, n_trips, trip, (chain0,) * n_chains)
            kt_last = lo + n_chains * (n_trips - 1)
            done = []
            for c in range(n_chains):
                m, l, acc, alpha, p = chains[c]
                done.append((m, l, alpha * acc + jnp.dot(values_t(kt_last + c, lo, hi), p,
                                                          preferred_element_type=F32)))
            m_all = functools.reduce(jnp.maximum, [m for m, _, _ in done])
            num = den = 0.0
            for m, l, acc in done:
                e = jnp.exp(m - m_all)
                num = num + acc * e
                den = den + l * e
            o = num / jnp.maximum(den, 1e-30)
            return jnp.concatenate([o[:, g * tq:(g + 1) * tq].T for g in range(GQA)], axis=-1)
        return sweep

    n_win = tzw_ref.shape[2] - 1
    os_ref[0] = make_sweep(ks_ref, vst_ref, tzs_ref, True, SEL_CHAINS)(0, qt)
    ow_ref[0] = make_sweep(kw_ref, vwt_ref, tzw_ref, False, n_win)(jnp.maximum(qt - (n_win - 1), 0), qt)


def _sel_win_prompt(q5, ks, vst, kw, vwt, sel, tzs, tzw):
    B, _, _, T, _ = q5.shape
    NB = sel.shape[2]
    tq = ATT_TQ
    k_spec = pl.BlockSpec((1, 1, T, HEAD_DIM), lambda b, h, i: (b, h, 0, 0))
    vt_spec = pl.BlockSpec((1, 1, HEAD_DIM, T), lambda b, h, i: (b, h, 0, 0))
    tz_spec = lambda tz: pl.BlockSpec((1,) + tz.shape[1:], lambda b, h, i: (h, 0, 0, 0, 0))
    o_spec = pl.BlockSpec((1, tq, GQA * HEAD_DIM), lambda b, h, i: (b, i, h))
    return pl.pallas_call(
        functools.partial(_sel_win_kernel, tq=tq),
        out_shape=(jax.ShapeDtypeStruct((B, T, D_ATT), F32), jax.ShapeDtypeStruct((B, T, D_ATT), F32)),
        grid=(B, N_KV_HEADS, T // tq),
        in_specs=[pl.BlockSpec((1, 1, GQA, tq, HEAD_DIM), lambda b, h, i: (b, h, 0, i, 0)),
                  k_spec, vt_spec, k_spec, vt_spec,
                  pl.BlockSpec((1, 1, NB, tq), lambda b, h, i: (b, h, 0, i)),
                  tz_spec(tzs), tz_spec(tzw)],
        out_specs=(o_spec, o_spec),
        compiler_params=_cparams("parallel", "parallel", "parallel"),
        name="sel_win_prompt",
    )(q5, ks, vst, kw, vwt, sel, tzs, tzw)


def _gate_expand_matrix():
    m = np.zeros((3, LANE, D_ATT), np.float32)
    for r in range(3):
        for h in range(N_HEADS):
            m[r, h * 3 + r, h * HEAD_DIM:(h + 1) * HEAD_DIM] = 1.0
    return m


def _post_mixer_kernel(y_ref, u_ref, oc_ref, os_ref, ow_ref, g_ref, x_ref, gate_ref, sh_ref, sc_ref,
                       dskip_ref, wglu_ref, bglu_ref, gexp_ref, wout_ref, lng_ref, lnb_ref,
                       wr_ref, br_ref, x1_ref, hm_ref, te_ref, tw_ref):
    y = y_ref[0] + dskip_ref[...] * u_ref[0]
    gl = jax.nn.gelu(y)
    ssm = gl * jax.nn.sigmoid(jnp.dot(gl.astype(BF16), wglu_ref[...], preferred_element_type=F32)
                              + bglu_ref[...])
    sg = jax.nn.sigmoid(g_ref[0])
    att = jnp.zeros_like(oc_ref[0])
    for r, o_ref in enumerate((oc_ref, os_ref, ow_ref)):
        att = att + jnp.dot(sg, gexp_ref[r], precision=HIGHEST, preferred_element_type=F32) * o_ref[0]
    h = (jnp.dot(ssm.astype(BF16), wout_ref[:D_SSM, :], preferred_element_type=F32)
         + jnp.dot(att.astype(BF16), wout_ref[D_SSM:, :], preferred_element_type=F32))
    z = DN_ALPHA * x_ref[0] + gate_ref[0] * h
    x1 = _layer_norm(z) * lng_ref[...] + lnb_ref[...]
    x1_ref[0] = x1
    hm = _layer_norm(x1) * (1.0 + sc_ref[0]) + sh_ref[0]
    hm_ref[0] = hm
    logits = jnp.dot(hm, wr_ref[...], precision=HIGHEST, preferred_element_type=F32) + br_ref[...]
    lane = lax.broadcasted_iota(jnp.int32, logits.shape, 1)
    work = jnp.where(lane < N_EXPERTS, logits, -jnp.inf)
    te = jnp.zeros(logits.shape, jnp.int32)
    tv = jnp.zeros(logits.shape, F32)
    for k in range(TOP_K):
        best = jnp.max(work, axis=-1, keepdims=True)
        arg = jnp.min(jnp.where(work == best, lane, LANE), axis=-1, keepdims=True)
        te = jnp.where(lane == k, arg, te)
        tv = jnp.where(lane == k, best, tv)
        work = jnp.where(lane == arg, -jnp.inf, work)
    ex = jnp.where(lane < TOP_K, jnp.exp(tv - tv[:, 0:1]), 0.0)
    te_ref[0] = te
    tw_ref[0] = ex / jnp.sum(ex, axis=-1, keepdims=True)


def _post_mixer(y, u, oc, osel, ow, g, x, gate, shift, scale, w, tm):
    B, T, D = x.shape
    R = gate.shape[1]
    rb = 1 if R == 1 else tm
    mod_map = (lambda b, i: (b, 0, 0)) if R == 1 else (lambda b, i: (b, i, 0))
    row = lambda n: pl.BlockSpec((1, tm, n), lambda b, i: (b, i, 0))
    mod = pl.BlockSpec((1, rb, D), mod_map)
    full = lambda a: pl.BlockSpec(a.shape, lambda b, i: (0,) * a.ndim)
    consts = (w['d_skip'], w['w_glu'], w['b_glu'], w['gexp'], w['w_out'], w['ln1_g'], w['ln1_b'],
              w['w_router'], w['b_router'])
    return pl.pallas_call(
        _post_mixer_kernel,
        out_shape=(jax.ShapeDtypeStruct((B, T, D), F32), jax.ShapeDtypeStruct((B, T, D), F32),
                   jax.ShapeDtypeStruct((B, T, LANE), jnp.int32), jax.ShapeDtypeStruct((B, T, LANE), F32)),
        grid=(B, T // tm),
        in_specs=[row(D_SSM), row(D_SSM), row(D_ATT), row(D_ATT), row(D_ATT), row(LANE), row(D),
                  mod, mod, mod] + [full(a) for a in consts],
        out_specs=(row(D), row(D), row(LANE), row(LANE)),
        compiler_params=_cparams("parallel", "parallel"),
        name="post_mixer",
    )(y, u, oc, osel, ow, g, x, gate, shift, scale, *consts)


def _expert_kernel(e_ref, blk_ref, lo_ref, hi_ref, first_ref, x_ref, wgu_ref, bgu_ref, wd_ref, bd_ref, o_ref,
                   wgu_s, wd_s):
    i = pl.program_id(0)
    fresh = (i == 0) | (e_ref[i] != e_ref[jnp.maximum(i - 1, 0)])

    @pl.when(fresh)
    def _():
        wgu_s[...] = wgu_ref[0].astype(BF16)
        wd_s[...] = wd_ref[0].astype(BF16)

    @pl.when(first_ref[i] == 1)
    def _():
        o_ref[...] = jnp.zeros_like(o_ref)

    @pl.when(hi_ref[i] > lo_ref[i])
    def _():
        gu = jnp.dot(x_ref[...].astype(BF16), wgu_s[...], preferred_element_type=F32) + bgu_ref[0]
        gate = jnp.minimum(gu[:, :D_FF], SWIGLU_LIMIT)
        up = jnp.clip(gu[:, D_FF:], -SWIGLU_LIMIT, SWIGLU_LIMIT)
        hh = (up + 1.0) * gate * jax.nn.sigmoid(SWIGLU_ALPHA * gate)
        y = jnp.dot(hh.astype(BF16), wd_s[...], preferred_element_type=F32) + bd_ref[0]
        row = blk_ref[i] * MOE_ROWS + lax.broadcasted_iota(jnp.int32, (MOE_ROWS, 1), 0)
        o_ref[...] = jnp.where((row >= lo_ref[i]) & (row < hi_ref[i]), y, o_ref[...])


def _experts(xb, items, w_gate_up, b_gate_up, w_down, b_down):
    rows, D = xb.shape
    n_items = items[0].shape[0]
    wmap = lambda i, e, blk, lo, hi, first: (e[i], 0, 0)
    rmap = lambda i, e, blk, lo, hi, first: (blk[i], 0)
    grid_spec = pltpu.PrefetchScalarGridSpec(
        num_scalar_prefetch=5,
        grid=(n_items,),
        in_specs=[pl.BlockSpec((MOE_ROWS, D), rmap),
                  pl.BlockSpec((1, D, 2 * D_FF), wmap),
                  pl.BlockSpec((1, 1, 2 * D_FF), wmap),
                  pl.BlockSpec((1, D_FF, D), wmap),
                  pl.BlockSpec((1, 1, D), wmap)],
        out_specs=pl.BlockSpec((MOE_ROWS, D), rmap),
        scratch_shapes=[pltpu.VMEM((D, 2 * D_FF), BF16), pltpu.VMEM((D_FF, D), BF16)],
    )
    return pl.pallas_call(
        _expert_kernel,
        out_shape=jax.ShapeDtypeStruct((rows, D), F32),
        grid_spec=grid_spec,
        compiler_params=_cparams("arbitrary"),
        name="moe_experts",
    )(*items, xb, w_gate_up, b_gate_up.reshape(N_EXPERTS, 1, 2 * D_FF), w_down,
      b_down.reshape(N_EXPERTS, 1, D))


def _moe_dispatch(top_e, n):
    blk = MOE_ROWS
    nk = n * TOP_K
    cb = 128
    assert nk % cb == 0
    e = top_e.reshape(-1)
    onehot = (e[:, None] == jnp.arange(N_EXPERTS)[None, :]).astype(F32)
    oh3 = onehot.reshape(nk // cb, cb, N_EXPERTS)
    tri = jnp.asarray(np.tril(np.ones((cb, cb), np.float32), -1))
    within = jnp.einsum('ij,bje->bie', tri, oh3, precision=HIGHEST)
    blk_tot = jnp.sum(oh3, axis=1)
    blk_off = jnp.cumsum(blk_tot, axis=0) - blk_tot
    counts = jnp.sum(blk_tot, axis=0)
    start = jnp.cumsum(counts) - counts
    dest = jnp.sum((within + blk_off[:, None, :] + start[None, None, :]) * oh3, axis=-1)
    dest = dest.reshape(nk).astype(jnp.int32)
    order = jnp.argsort(dest)
    n_blk = -(-nk // blk)
    row_tok = jnp.concatenate([(order // TOP_K).astype(jnp.int32), jnp.full((n_blk * blk - nk,), n, jnp.int32)])
    counts_i, start_i = counts.astype(jnp.int32), start.astype(jnp.int32)
    first_b = start_i // blk
    last_b = (start_i + counts_i - 1) // blk
    n_it = jnp.where(counts_i > 0, last_b - first_b + 1, 0)
    it_end = jnp.cumsum(n_it)
    it_start = it_end - n_it
    n_items = n_blk + N_EXPERTS - 1
    i = jnp.arange(n_items)
    live = i < it_end[-1]
    it_e = jnp.minimum(jnp.sum(it_end[None, :] <= i[:, None], axis=1), N_EXPERTS - 1)
    it_blk = jnp.where(live, first_b[it_e] + i - it_start[it_e], n_blk - 1)
    it_lo = jnp.where(live, start_i[it_e], 0)
    it_hi = jnp.where(live, start_i[it_e] + counts_i[it_e], 0)
    it_first = jnp.concatenate([jnp.ones((1,), jnp.int32), (it_blk[1:] != it_blk[:-1]).astype(jnp.int32)])
    items = tuple(a.astype(jnp.int32) for a in (it_e, it_blk, it_lo, it_hi, it_first))
    return row_tok, dest.reshape(n, TOP_K), items


def _final_kernel(x_ref, y0_ref, y1_ref, y2_ref, y3_ref, tw_ref, gate_ref, lng_ref, lnb_ref, o_ref):
    tw = tw_ref[0]
    y = jnp.zeros_like(x_ref[0])
    for k, y_ref in enumerate((y0_ref, y1_ref, y2_ref, y3_ref)):
        y = y + tw[:, k:k + 1] * y_ref[0]
    z = DN_ALPHA * x_ref[0] + gate_ref[0] * y
    o_ref[0] = _layer_norm(z) * lng_ref[...] + lnb_ref[...]


def _final(x1, ys, tw, gate, ln_g, ln_b, tm):
    B, T, D = x1.shape
    R = gate.shape[1]
    rb = 1 if R == 1 else tm
    mod_map = (lambda b, i: (b, 0, 0)) if R == 1 else (lambda b, i: (b, i, 0))
    row = lambda n: pl.BlockSpec((1, tm, n), lambda b, i: (b, i, 0))
    vec = pl.BlockSpec((1, D), lambda b, i: (0, 0))
    return pl.pallas_call(
        _final_kernel,
        out_shape=jax.ShapeDtypeStruct((B, T, D), F32),
        grid=(B, T // tm),
        in_specs=[row(D), row(D), row(D), row(D), row(D), row(LANE),
                  pl.BlockSpec((1, rb, D), mod_map), vec, vec],
        out_specs=row(D),
        compiler_params=_cparams("parallel", "parallel"),
        name="moe_combine_ln",
    )(x1, *ys, tw, gate, ln_g, ln_b)


def _cmp_select_step_kernel(q_ref, kv_ref, bias_ref, pool_ref, o_ref, idx_ref, *, n_cmp, n_blk, q_pos):
    q = q_ref[0].astype(BF16)
    ncp = kv_ref.shape[1]
    nbp = pool_ref.shape[1]
    hd = HEAD_DIM
    kv = kv_ref[0]
    kb = [kv[:, h * hd:(h + 1) * hd].astype(BF16) for h in range(N_KV_HEADS)]
    vb = [kv[:, (N_KV_HEADS + h) * hd:(N_KV_HEADS + h + 1) * hd].astype(BF16) for h in range(N_KV_HEADS)]
    row = lax.broadcasted_iota(jnp.int32, (N_HEADS, 1), 0)
    first = row < GQA
    s = jnp.where(first, _nt_dot(q, kb[0]), _nt_dot(q, kb[1])) * (hd ** -0.5)
    s = s + bias_ref[...]
    ci = lax.broadcasted_iota(jnp.int32, (N_HEADS, ncp), 1)
    mask = (ci * CMP_STRIDE + CMP_BLOCK - 1 <= q_pos) & (ci < n_cmp)
    s = jnp.where(mask, s, NEG)
    m = jnp.max(s, axis=-1, keepdims=True)
    p = jnp.where(mask, jnp.exp(s - m), 0.0)
    p = p / jnp.maximum(jnp.sum(p, axis=-1, keepdims=True), 1e-30)
    pb = p.astype(BF16)
    o_ref[0] = jnp.where(first, jnp.dot(pb, vb[0], preferred_element_type=F32),
                         jnp.dot(pb, vb[1], preferred_element_type=F32))
    imp0 = jnp.sum(jnp.where(first, p, 0.0), axis=0, keepdims=True)
    imp1 = jnp.sum(jnp.where(first, 0.0, p), axis=0, keepdims=True)
    imp = jnp.where(first, imp0, imp1)
    sb = jnp.dot(imp, pool_ref[...], precision=HIGHEST, preferred_element_type=F32)
    cur = q_pos // SEL_BLOCK
    bi = lax.broadcasted_iota(jnp.int32, (nbp, nbp), 0)
    bj = lax.broadcasted_iota(jnp.int32, (nbp, nbp), 1)
    blk = lax.broadcasted_iota(jnp.int32, (1, nbp), 1)
    causal = blk <= cur
    forced = (blk == 0) | (blk == cur) | (blk == cur - 1)
    rsel = lax.broadcasted_iota(jnp.int32, (N_SEL, nbp), 0)
    for h in range(N_KV_HEADS):
        sc = jnp.where(forced & causal, 1e4, jnp.where(causal, sb[h * GQA:h * GQA + 1, :], -1.0))
        sc = jnp.where(blk < n_blk, sc, -2.0)
        scb = jnp.broadcast_to(sc, (nbp, nbp))
        col = jnp.sum(jnp.where(bi == bj, scb, 0.0), axis=1, keepdims=True)
        ahead = (col > scb) | ((col == scb) & (bi < bj))
        rank = jnp.sum(ahead.astype(jnp.int32), axis=0, keepdims=True)
        hit = jnp.broadcast_to(rank, (N_SEL, nbp)) == rsel
        idx = jnp.sum(jnp.where(hit, jnp.broadcast_to(blk, (N_SEL, nbp)), 0), axis=1, keepdims=True)
        idx_ref[0, h] = jnp.broadcast_to(idx, (N_SEL, LANE))


def _cmp_select_step(q, ckv, bias, pool, n_cmp, n_blk, q_pos):
    B = q.shape[0]
    NCp = ckv.shape[1]
    return pl.pallas_call(
        functools.partial(_cmp_select_step_kernel, n_cmp=n_cmp, n_blk=n_blk, q_pos=q_pos),
        out_shape=(jax.ShapeDtypeStruct((B, N_HEADS, HEAD_DIM), F32),
                   jax.ShapeDtypeStruct((B, N_KV_HEADS, N_SEL, LANE), jnp.int32)),
        grid=(B,),
        in_specs=[pl.BlockSpec((1, N_HEADS, HEAD_DIM), lambda b: (b, 0, 0)),
                  pl.BlockSpec((1, NCp, D_KV), lambda b: (b, 0, 0)),
                  pl.BlockSpec(bias.shape, lambda b: (0, 0)),
                  pl.BlockSpec(pool.shape, lambda b: (0, 0))],
        out_specs=(pl.BlockSpec((1, N_HEADS, HEAD_DIM), lambda b: (b, 0, 0)),
                   pl.BlockSpec((1, N_KV_HEADS, N_SEL, LANE), lambda b: (b, 0, 0, 0))),
        compiler_params=_cparams("parallel"),
        name="cmp_select_step",
    )(q, ckv, bias, pool)


def _sel_step_kernel(pg_ref, idx_ref, q_ref, *refs, n_past, q_pos):
    page_refs = refs[:N_SEL]
    new_ref, bias_ref, kpos_ref, o_ref = refs[N_SEL:]
    b, h = pl.program_id(0), pl.program_id(1)
    base = (b * N_KV_HEADS + h) * N_SEL
    kts, vts = [], []
    for j in range(N_SEL):
        is_new = idx_ref[base + j] >= n_past
        kts.append(jnp.where(is_new, new_ref[0, 0, 0], page_refs[j][0, 0, 0]))
        vts.append(jnp.where(is_new, new_ref[0, 1, 0], page_refs[j][0, 1, 0]))
    kt = jnp.concatenate(kts, axis=1).astype(BF16)
    vt = jnp.concatenate(vts, axis=1).astype(BF16)
    s = jnp.dot(q_ref[0].astype(BF16), kt, preferred_element_type=F32) * (HEAD_DIM ** -0.5) + bias_ref[0, 0]
    mask = kpos_ref[0, 0] <= q_pos
    s = jnp.where(mask, s, NEG)
    m = jnp.max(s, axis=-1, keepdims=True)
    p = jnp.where(mask, jnp.exp(s - m), 0.0)
    l = jnp.sum(p, axis=-1, keepdims=True)
    o_ref[0, 0] = _nt_dot(p.astype(BF16), vt) / jnp.maximum(l, 1e-30)


def _sel_step(q, pool_t, new_t, bias_sel, kpos, pages, idx_flat, n_past, q_pos):
    B = q.shape[0]
    nk = N_SEL * PAGE_SIZE
    slot = lambda b, h, j: (b * N_KV_HEADS + h) * N_SEL + j
    page_spec = lambda j: pl.BlockSpec((1, 2, 1, HEAD_DIM, PAGE_SIZE),
                                       lambda b, h, pg, ix, j=j: (pg[slot(b, h, j)], 0, h, 0, 0))
    grid_spec = pltpu.PrefetchScalarGridSpec(
        num_scalar_prefetch=2,
        grid=(B, N_KV_HEADS),
        in_specs=[pl.BlockSpec((1, N_HEADS, HEAD_DIM), lambda b, h, pg, ix: (b, 0, 0))]
        + [page_spec(j) for j in range(N_SEL)]
        + [pl.BlockSpec((1, 2, 1, HEAD_DIM, PAGE_SIZE), lambda b, h, pg, ix: (b, 0, h, 0, 0)),
           pl.BlockSpec((1, 1, N_HEADS, nk), lambda b, h, pg, ix: (b, h, 0, 0)),
           pl.BlockSpec((1, 1, 1, nk), lambda b, h, pg, ix: (b, h, 0, 0))],
        out_specs=pl.BlockSpec((1, 1, N_HEADS, HEAD_DIM), lambda b, h, pg, ix: (b, h, 0, 0)),
    )
    return pl.pallas_call(
        functools.partial(_sel_step_kernel, n_past=n_past, q_pos=q_pos),
        out_shape=jax.ShapeDtypeStruct((B, N_KV_HEADS, N_HEADS, HEAD_DIM), F32),
        grid_spec=grid_spec,
        compiler_params=_cparams("arbitrary", "arbitrary"),
        name="sel_step",
    )(pages, idx_flat, q, *([pool_t] * N_SEL), new_t, bias_sel, kpos)


def _win_step_kernel(q_ref, w_ref, new_ref, bias_ref, bias0_ref, o_ref):
    q = q_ref[0]
    qb = q.astype(BF16)
    row = lax.broadcasted_iota(jnp.int32, (N_HEADS, 1), 0)
    first = row < GQA
    w = w_ref[0]
    hd = HEAD_DIM
    kb = [w[:, h * hd:(h + 1) * hd].astype(BF16) for h in range(N_KV_HEADS)]
    vb = [w[:, (N_KV_HEADS + h) * hd:(N_KV_HEADS + h + 1) * hd].astype(BF16) for h in range(N_KV_HEADS)]
    s = jnp.where(first, _nt_dot(qb, kb[0]), _nt_dot(qb, kb[1])) * (hd ** -0.5) + bias_ref[...]
    new = new_ref[0]
    kn = jnp.where(first, new[:, 0:hd], new[:, hd:2 * hd])
    vn = jnp.where(first, new[:, 2 * hd:3 * hd], new[:, 3 * hd:])
    sn = jnp.sum(q * kn, axis=-1, keepdims=True) * (hd ** -0.5) + bias0_ref[...]
    m = jnp.maximum(jnp.max(s, axis=-1, keepdims=True), sn)
    p = jnp.exp(s - m)
    pn = jnp.exp(sn - m)
    l = jnp.sum(p, axis=-1, keepdims=True) + pn
    pb = p.astype(BF16)
    acc = jnp.where(first, jnp.dot(pb, vb[0], preferred_element_type=F32),
                    jnp.dot(pb, vb[1], preferred_element_type=F32)) + pn * vn
    o_ref[0] = acc / jnp.maximum(l, 1e-30)


def _win_step(q, win, new, bias, bias0):
    B, W, _ = win.shape
    return pl.pallas_call(
        _win_step_kernel,
        out_shape=jax.ShapeDtypeStruct((B, N_HEADS, HEAD_DIM), F32),
        grid=(B,),
        in_specs=[pl.BlockSpec((1, N_HEADS, HEAD_DIM), lambda b: (b, 0, 0)),
                  pl.BlockSpec((1, W, D_KV), lambda b: (b, 0, 0)),
                  pl.BlockSpec((1, 1, D_KV), lambda b: (b, 0, 0)),
                  pl.BlockSpec((N_HEADS, W), lambda b: (0, 0)),
                  pl.BlockSpec((N_HEADS, 1), lambda b: (0, 0))],
        out_specs=pl.BlockSpec((1, N_HEADS, HEAD_DIM), lambda b: (b, 0, 0)),
        compiler_params=_cparams("parallel"),
        name="win_step",
    )(q, win, new, bias, bias0)


def _split_heads(kv, dtype):
    B, L, _ = kv.shape
    kv5 = kv.reshape(B, L, 2, N_KV_HEADS, HEAD_DIM)
    return (jnp.transpose(kv5[:, :, 0], (0, 2, 1, 3)).astype(dtype),
            jnp.transpose(kv5[:, :, 1], (0, 2, 1, 3)).astype(dtype))


def _nsa_prompt(q, kvc, kvs, kvw, cmp_tab, rel_bias):
    B, T, _ = q.shape
    nc = T // CMP_STRIDE
    nb = T // SEL_BLOCK
    ckv = _compress_out(_compress_in(kvc.reshape(B, nc, CMP_STRIDE * D_KV), cmp_tab), cmp_tab, nc)
    kc, vc = _split_heads(ckv, BF16)
    vct = jnp.transpose(vc, (0, 1, 3, 2))
    bias_n = _bias_by_distance(rel_bias, T)
    n_qt, n_kt = T // ATT_TQ, T // ATT_TK
    n_ds = min(n_kt, -(-(REL_MAX_DIST + ATT_TK - 1) // ATT_TK) + 1)
    n_dw = min(n_kt, WINDOW // ATT_TK + 1)
    tzs, tzw, bias_tab = _bias_tables(bias_n, n_qt, nc // 8, n_ds, n_dw, ATT_TQ, ATT_TK)
    pool = jnp.asarray(_pool_matrix(nc, nb))
    scale = HEAD_DIM ** -0.5
    q5 = jnp.transpose((q * scale).reshape(B, T, N_KV_HEADS, GQA, HEAD_DIM), (0, 2, 3, 1, 4))
    o_cmp, sel = _cmp_select_prompt(q5, kc, vct, bias_tab, pool, nc - 1)
    ks, vs = _split_heads(kvs, BF16)
    kw, vw = _split_heads(kvw, BF16)
    o_sel, o_win = _sel_win_prompt(q5, ks, jnp.transpose(vs, (0, 1, 3, 2)), kw, jnp.transpose(vw, (0, 1, 3, 2)),
                                   sel, tzs, tzw)
    return o_cmp, o_sel, o_win


def _nsa_sample(q, kvc, kvs, kvw, pool_cmp, pool_sel, win_buf, page_table, cmp_tab, rel_bias):
    B = q.shape[0]
    n_pages = page_table.shape[1]
    past_len = n_pages * PAGE_SIZE
    q_pos = past_len
    lp = -(-(past_len + 1) // SEL_BLOCK) * SEL_BLOCK
    n_cmp = lp // CMP_STRIDE - 1
    n_blk = lp // SEL_BLOCK
    n_past_chunks = past_len // CMP_STRIDE
    n_tail = 8
    assert n_past_chunks + n_tail >= n_cmp + 1
    n_chunks = n_past_chunks + n_tail
    feature_major = lambda pool: jnp.transpose(pool, (0, 2, 3, 4, 1))
    z_past = _compress_in_paged(feature_major(pool_cmp), page_table, cmp_tab)
    tail = jnp.pad(kvc[:, None, :], ((0, 0), (0, n_tail * CMP_STRIDE - 1), (0, 0)))
    z_tail = _compress_in(tail.reshape(B, n_tail, CMP_STRIDE * D_KV), cmp_tab)
    ncp = -(-n_chunks // LANE) * LANE
    nbp = -(-n_blk // LANE) * LANE
    ckv = _compress_out(jnp.concatenate([z_past, z_tail], axis=1), cmp_tab, ncp)
    bias_n = _bias_by_distance(rel_bias, q_pos + 1)
    n_back = max((n_pages + 1) * PAGE_SIZE, ncp * CMP_STRIDE + CMP_BLOCK)
    back = jnp.concatenate([bias_n[:, ::-1], jnp.broadcast_to(bias_n[:, :1], (N_HEADS, n_back - q_pos - 1))], 1)
    bias_c = back[:, CMP_BLOCK - 1:CMP_BLOCK - 1 + ncp * CMP_STRIDE:CMP_STRIDE]
    pool = jnp.asarray(_pool_matrix(ncp, nbp).T)
    q3 = q.reshape(B, N_HEADS, HEAD_DIM)
    o_cmp, idx = _cmp_select_step(q3, ckv, bias_c, pool, n_cmp, n_blk, q_pos)
    idx = idx[..., 0]
    bpp = PAGE_SIZE // SEL_BLOCK
    n_past = n_pages * bpp
    lpage = idx // bpp
    pages = jnp.take_along_axis(page_table, jnp.minimum(lpage, n_pages - 1).reshape(B, -1), axis=1)
    new_t = jnp.pad(kvs.reshape(B, 2, N_KV_HEADS, HEAD_DIM, 1), ((0, 0),) * 4 + ((0, PAGE_SIZE - 1),))
    bias_page = jnp.transpose(back[:, :(n_pages + 1) * PAGE_SIZE].reshape(N_HEADS, n_pages + 1, PAGE_SIZE),
                              (1, 0, 2))
    bias_sel = jnp.transpose(bias_page[lpage], (0, 1, 3, 2, 4)).reshape(B, N_KV_HEADS, N_HEADS, -1)
    kpos = lpage[..., None] * PAGE_SIZE + jnp.arange(PAGE_SIZE)
    ok = (kpos // SEL_BLOCK == idx[..., None]) & (idx <= q_pos // SEL_BLOCK)[..., None]
    kpos = jnp.where(ok, kpos, q_pos + 1).reshape(B, N_KV_HEADS, 1, -1).astype(jnp.int32)
    o_sel = _sel_step(q3, feature_major(pool_sel), new_t, bias_sel, kpos, pages.reshape(-1).astype(jnp.int32),
                      idx.reshape(-1).astype(jnp.int32), n_past, q_pos)
    o_sel = jnp.concatenate([o_sel[:, h, h * GQA:(h + 1) * GQA] for h in range(N_KV_HEADS)], axis=1)
    wb = win_buf.shape[1]
    bias_w = bias_n[:, 1:wb + 1][:, ::-1]
    o_win = _win_step(q3, win_buf.reshape(B, wb, D_KV), kvw[:, None, :], bias_w, bias_n[:, 0:1])
    return o_cmp.reshape(B, D_ATT), o_sel.reshape(B, D_ATT), o_win.reshape(B, D_ATT)


def kernel(x_prompt, x_sample, cache_cmp_kv, cache_sel_kv, state_win_kv, state_ssm_re, state_ssm_im, page_table,
           c_prompt, c_sample, w_ada, b_ada, w_in, lam_re, lam_im, log_dt, b_re, b_im, c_re, c_im, d_skip,
           w_glu, b_glu, phi_pe, phi_w1, phi_b1, phi_w2, phi_b2, rel_bias, w_out, ln1_g, ln1_b,
           w_router, b_router, w_gate_up, b_gate_up, w_down, b_down, ln2_g, ln2_b):
    assert w_ada.shape[0] == DEPTH == 1
    l = 0
    Bp, T, D = x_prompt.shape
    Bs = x_sample.shape[0]
    kv_tail = (2, N_KV_HEADS, HEAD_DIM)

    n_c = Bp + Bs
    c_all = jnp.pad(jnp.concatenate([c_prompt, c_sample], 0), ((0, -n_c % 8), (0, 0)))
    m_all = _adaln(c_all, w_ada[l], b_ada[l])
    m_p = m_all[:Bp].reshape(Bp, 6, D)
    m_s = m_all[Bp:n_c].reshape(Bs, 6, D)
    mod_p = [m_p[:, i:i + 1, :] for i in range(6)]
    mod_s = [m_s[None, :, i, :] for i in range(6)]

    w_in_pad = jnp.pad(w_in[l], ((0, 0), (0, D_IN_PAD - D_IN))).astype(BF16)
    n_levels = max(1, int(math.log2(T // SSM_CHUNK)))
    ssm_tab = _ssm_tables(lam_re[l], lam_im[l], log_dt[l], b_re[l], b_im[l], c_re[l], c_im[l],
                          SSM_CHUNK, n_levels)
    cmp_tab = _compress_tables(phi_pe[l], phi_w1[l], phi_b1[l], phi_w2[l], phi_b2[l])
    w_post = dict(
        d_skip=d_skip[l].reshape(1, D_SSM), w_glu=w_glu[l].astype(BF16), b_glu=b_glu[l].reshape(1, D_SSM),
        gexp=jnp.asarray(_gate_expand_matrix()), w_out=w_out[l].astype(BF16),
        ln1_g=ln1_g[l].reshape(1, D), ln1_b=ln1_b[l].reshape(1, D),
        w_router=jnp.pad(w_router[l], ((0, 0), (0, LANE - N_EXPERTS))),
        b_router=jnp.pad(b_router[l], (0, LANE - N_EXPERTS)).reshape(1, LANE))

    u, q, kvc, kvs, kvw, g = _mixer_in(x_prompt, mod_p[0], mod_p[1], w_in_pad, tm=512)
    y_ssm, h_p = _ssm_prompt(u, ssm_tab)
    o_cmp, o_sel, o_win = _nsa_prompt(q, kvc, kvs, kvw, cmp_tab, rel_bias)
    x1_p, hm_p, te_p, tw_p = _post_mixer(y_ssm, u, o_cmp, o_sel, o_win, g, x_prompt,
                                         mod_p[2], mod_p[3], mod_p[4], w_post, tm=256)

    u_s, q_s, kvc_s, kvs_s, kvw_s, g_s = _mixer_in(x_sample.reshape(1, Bs, D), mod_s[0], mod_s[1],
                                                   w_in_pad, tm=Bs)
    y_s, h_s = _ssm_sample(u_s[0], state_ssm_re[l], state_ssm_im[l], ssm_tab, c_re[l], c_im[l])
    oc_s, os_s, ow_s = _nsa_sample(q_s[0].astype(F32), kvc_s[0], kvs_s[0], kvw_s[0], cache_cmp_kv[l],
                                   cache_sel_kv[l], state_win_kv[l], page_table, cmp_tab, rel_bias)
    x1_s, hm_s, te_s, tw_s = _post_mixer(y_s[None], u_s, oc_s[None], os_s[None], ow_s[None], g_s,
                                         x_sample.reshape(1, Bs, D), mod_s[2], mod_s[3], mod_s[4],
                                         w_post, tm=Bs)

    n_p = Bp * T
    n_all = n_p + Bs
    hm_all = jnp.concatenate([hm_p.reshape(n_p, D), hm_s.reshape(Bs, D)], 0)
    te_all = jnp.concatenate([te_p.reshape(n_p, LANE), te_s.reshape(Bs, LANE)], 0)[:, :TOP_K]
    row_tok, dest, items = _moe_dispatch(te_all, n_all)
    xb = jnp.concatenate([hm_all, jnp.zeros((1, D), F32)], 0)[row_tok]
    yb = _experts(xb, items, w_gate_up[l], b_gate_up[l], w_down[l], b_down[l])
    ys_p = [yb[dest[:n_p, k]].reshape(Bp, T, D) for k in range(TOP_K)]
    ys_s = [yb[dest[n_p:, k]].reshape(1, Bs, D) for k in range(TOP_K)]
    ln2g, ln2b = ln2_g[l].reshape(1, D), ln2_b[l].reshape(1, D)
    out_p = _final(x1_p, ys_p, tw_p, mod_p[5], ln2g, ln2b, tm=512)
    out_s = _final(x1_s, ys_s, tw_s, mod_s[5], ln2g, ln2b, tm=Bs)

    wlen = min(WINDOW, T)
    win_s = jnp.concatenate([state_win_kv[l], kvw_s[0].reshape(Bs, 1, *kv_tail)], 1)[:, -state_win_kv.shape[2]:]
    p_state = SSM_STATE
    return (out_p, out_s.reshape(Bs, 1, D),
            kvc.reshape(1, Bp, T, *kv_tail), kvc_s[0].reshape(1, Bs, 1, *kv_tail),
            kvs.reshape(1, Bp, T, *kv_tail), kvs_s[0].reshape(1, Bs, 1, *kv_tail),
            kvw[:, T - wlen:].reshape(1, Bp, wlen, *kv_tail), win_s[None],
            h_p[None, ..., :p_state], h_p[None, ..., p_state:],
            h_s[None, ..., :p_state], h_s[None, ..., p_state:])
```

```python
import functools
import math

import numpy as np
import jax
import jax.numpy as jnp
from jax import lax
from jax.experimental import pallas as pl
from jax.experimental.pallas import tpu as pltpu

D_MODEL = 1024
DEPTH = 1
PAST_LEN = 16384
PAGE_SIZE = 128
D_SSM = 512
SSM_GROUP = 16
N_SSM_GROUPS = D_SSM // SSM_GROUP
SSM_STATE = 64
N_HEADS = 8
HEAD_DIM = 64
N_KV_HEADS = 2
GQA = N_HEADS // N_KV_HEADS
D_ATT = N_HEADS * HEAD_DIM
D_KV = 2 * N_KV_HEADS * HEAD_DIM
CMP_STRIDE = 16
CMP_BLOCK = 2 * CMP_STRIDE
SEL_BLOCK = 64
N_SEL = 16
WINDOW = 512
NUM_BUCKETS = 32
REL_MAX_DIST = 1024
N_EXPERTS = 32
TOP_K = 4
D_FF = 1024
SWIGLU_LIMIT = 7.0
SWIGLU_ALPHA = 1.702
DN_ALPHA = (2 * DEPTH) ** 0.25
D_IN = D_SSM + D_ATT + 3 * D_KV + 3 * N_HEADS
NEG = -1e30
F32 = jnp.float32
BF16 = jnp.bfloat16
HIGHEST = lax.Precision.HIGHEST

LANE = 128
D_IN_PAD = 1920
GATE_COL = D_SSM + D_ATT + 3 * D_KV
SSM_CHUNK = 16
ATT_TQ = 128
ATT_TK = 128
SEL_CHAINS = 4
MOE_ROWS = 256
PAGES_PER_STEP = 32
VMEM_LIMIT = 48 * 1024 * 1024
LN_EPS = 1e-5


def _cparams(*sem):
    return pltpu.CompilerParams(dimension_semantics=sem, vmem_limit_bytes=VMEM_LIMIT)


def _nt_dot(a, b):
    return lax.dot_general(a, b, (((1,), (1,)), ((), ())), preferred_element_type=F32)


def _layer_norm(x):
    mu = jnp.mean(x, axis=-1, keepdims=True)
    xc = x - mu
    var = jnp.mean(xc * xc, axis=-1, keepdims=True)
    return xc * lax.rsqrt(var + LN_EPS)


def _adaln_kernel(c_ref, w_ref, b_ref, o_ref):
    c = c_ref[...]
    s = c * jax.nn.sigmoid(c)
    o_ref[...] = jnp.dot(s, w_ref[...], precision=HIGHEST, preferred_element_type=F32) + b_ref[...]


def _adaln(c, w, b):
    n, d = c.shape
    dout = w.shape[1]
    tn = 1024
    return pl.pallas_call(
        _adaln_kernel,
        out_shape=jax.ShapeDtypeStruct((n, dout), F32),
        grid=(dout // tn,),
        in_specs=[pl.BlockSpec((n, d), lambda j: (0, 0)),
                  pl.BlockSpec((d, tn), lambda j: (0, j)),
                  pl.BlockSpec((1, tn), lambda j: (0, j))],
        out_specs=pl.BlockSpec((n, tn), lambda j: (0, j)),
        compiler_params=_cparams("arbitrary"),
        name="adaln",
    )(c, w, b.reshape(1, dout))


def _mixer_in_kernel(x_ref, sh_ref, sc_ref, w_ref, u_ref, q_ref, kvc_ref, kvs_ref, kvw_ref, g_ref):
    h = _layer_norm(x_ref[0]) * (1.0 + sc_ref[0]) + sh_ref[0]
    z = jnp.dot(h.astype(BF16), w_ref[...], preferred_element_type=F32)
    c0 = D_SSM
    c1 = c0 + D_ATT
    c2 = c1 + D_KV
    c3 = c2 + D_KV
    c4 = c3 + D_KV
    u_ref[0] = z[:, :c0]
    q_ref[0] = z[:, c0:c1].astype(BF16)
    kvc_ref[0] = z[:, c1:c2]
    kvs_ref[0] = z[:, c2:c3]
    kvw_ref[0] = z[:, c3:c4]
    g_ref[0] = z[:, c4:c4 + LANE]


def _mixer_in(x, shift, scale, w_pad, tm):
    B, T, D = x.shape
    R = shift.shape[1]
    rb = 1 if R == 1 else tm
    mod_map = (lambda b, i: (b, 0, 0)) if R == 1 else (lambda b, i: (b, i, 0))
    row = lambda n: pl.BlockSpec((1, tm, n), lambda b, i: (b, i, 0))
    outs = (jax.ShapeDtypeStruct((B, T, D_SSM), F32), jax.ShapeDtypeStruct((B, T, D_ATT), BF16),
            jax.ShapeDtypeStruct((B, T, D_KV), F32), jax.ShapeDtypeStruct((B, T, D_KV), F32),
            jax.ShapeDtypeStruct((B, T, D_KV), F32), jax.ShapeDtypeStruct((B, T, LANE), F32))
    return pl.pallas_call(
        _mixer_in_kernel,
        out_shape=outs,
        grid=(B, T // tm),
        in_specs=[row(D), pl.BlockSpec((1, rb, D), mod_map), pl.BlockSpec((1, rb, D), mod_map),
                  pl.BlockSpec((D, D_IN_PAD), lambda b, i: (0, 0))],
        out_specs=(row(D_SSM), row(D_ATT), row(D_KV), row(D_KV), row(D_KV), row(LANE)),
        compiler_params=_cparams("parallel", "parallel"),
        name="mixer_in",
    )(x, shift, scale, w_pad)


def _ssm_tables(lam_re, lam_im, log_dt, b_re, b_im, c_re, c_im, L, n_levels):
    G, P = lam_re.shape
    C = b_re.shape[-1]
    dt = jnp.exp(log_dt.astype(F32))[:, None]
    er, ei = lam_re * dt, lam_im * dt

    def power(k):
        kk = k.astype(F32)[:, None, None]
        mag = jnp.exp(kk * er)
        return mag * jnp.cos(kk * ei), mag * jnp.sin(kk * ei)

    lb_re, lb_im = power(jnp.ones((1,), F32))
    nr, ni = lb_re[0] - 1.0, lb_im[0]
    den = lam_re * lam_re + lam_im * lam_im
    fr = (nr * lam_re + ni * lam_im) / den
    fi = (ni * lam_re - nr * lam_im) / den
    bbr = fr[:, :, None] * b_re - fi[:, :, None] * b_im
    bbi = fr[:, :, None] * b_im + fi[:, :, None] * b_re
    pr, pi = power(jnp.arange(L + 1))
    clr = c_re[None] * pr[:, :, None, :] - c_im[None] * pi[:, :, None, :]
    cli = c_re[None] * pi[:, :, None, :] + c_im[None] * pr[:, :, None, :]
    kern = (jnp.einsum('kgcp,gpd->kgcd', clr[:L], bbr, precision=HIGHEST)
            - jnp.einsum('kgcp,gpd->kgcd', cli[:L], bbi, precision=HIGHEST))
    kz = jnp.concatenate([kern, jnp.zeros((1,) + kern.shape[1:], F32)], 0)
    ts = np.arange(L)
    lag = ts[None, :] - ts[:, None]
    lag = np.where(lag >= 0, lag, L)
    toep = kz[lag]
    toep = jnp.transpose(toep, (2, 0, 4, 1, 3)).reshape(G, L * C, L * C)
    rev = L - 1 - ts
    wsr = pr[rev][:, :, :, None] * bbr[None] - pi[rev][:, :, :, None] * bbi[None]
    wsi = pr[rev][:, :, :, None] * bbi[None] + pi[rev][:, :, :, None] * bbr[None]
    ws = jnp.concatenate([jnp.transpose(wsr, (1, 0, 3, 2)), jnp.transpose(wsi, (1, 0, 3, 2))], -1)
    ws = ws.reshape(G, L * C, 2 * P)
    wy = jnp.concatenate([jnp.transpose(clr[1:], (1, 3, 0, 2)), -jnp.transpose(cli[1:], (1, 3, 0, 2))], 1)
    wy = wy.reshape(G, 2 * P, L * C)
    lr, li = power(L * (2 ** jnp.arange(n_levels)))
    ar = jnp.transpose(jnp.concatenate([lr, lr], -1), (1, 0, 2))
    ai = jnp.transpose(jnp.concatenate([-li, li], -1), (1, 0, 2))
    return toep.astype(BF16), ws.astype(BF16), wy.astype(BF16), ar, ai, (lb_re[0], lb_im[0], bbr, bbi)


def _ssm_kernel(u_ref, toep_ref, ws_ref, wy_ref, ar_ref, ai_ref, y_ref, hl_ref, *, nb, nc, n_levels):
    u = u_ref[0]
    y1 = jnp.dot(u, toep_ref[0], preferred_element_type=F32)
    s = jnp.dot(u, ws_ref[0], preferred_element_type=F32)
    p2 = s.shape[-1]
    rows = lax.broadcasted_iota(jnp.int32, (nc, p2), 0)
    prev = []
    for b in range(nb):
        h = s[b * nc:(b + 1) * nc]
        for k in range(n_levels):
            d = 1 << k
            sh = jnp.where(rows >= d, pltpu.roll(h, d, axis=0), 0.0)
            sw = pltpu.roll(sh, p2 // 2, axis=1)
            h = h + ar_ref[0, k:k + 1, :] * sh + ai_ref[0, k:k + 1, :] * sw
        hl_ref[0, b:b + 1, :] = h[nc - 1:nc, :]
        prev.append(jnp.where(rows >= 1, pltpu.roll(h, 1, axis=0), 0.0))
    hp = jnp.concatenate(prev, axis=0)
    y2 = jnp.dot(hp.astype(BF16), wy_ref[0], preferred_element_type=F32)
    y_ref[0] = y1 + y2


def _ssm_prompt(u, tables):
    toep, ws, wy, ar, ai, _ = tables
    B, T, _ = u.shape
    G, C, L = N_SSM_GROUPS, SSM_GROUP, SSM_CHUNK
    nc = T // L
    n_levels = ar.shape[1]
    ug = jnp.transpose(u.reshape(B, nc, L, G, C), (3, 0, 1, 2, 4)).reshape(G, B * nc, L * C).astype(BF16)
    grp = lambda r, c: pl.BlockSpec((1, r, c), lambda g: (g, 0, 0))
    y, hl = pl.pallas_call(
        functools.partial(_ssm_kernel, nb=B, nc=nc, n_levels=n_levels),
        out_shape=(jax.ShapeDtypeStruct((G, B * nc, L * C), F32),
                   jax.ShapeDtypeStruct((G, B, 2 * SSM_STATE), F32)),
        grid=(G,),
        in_specs=[grp(B * nc, L * C), grp(L * C, L * C), grp(L * C, 2 * SSM_STATE),
                  grp(2 * SSM_STATE, L * C), grp(n_levels, 2 * SSM_STATE), grp(n_levels, 2 * SSM_STATE)],
        out_specs=(grp(B * nc, L * C), grp(B, 2 * SSM_STATE)),
        compiler_params=_cparams("parallel"),
        name="ssm_prompt",
    )(ug, toep, ws, wy, ar, ai)
    y = jnp.transpose(y.reshape(G, B, nc, L, C), (1, 2, 3, 0, 4)).reshape(B, T, D_SSM)
    return y, jnp.transpose(hl, (1, 0, 2))


def _ssm_step_kernel(u_ref, h0_ref, bb_ref, lr_ref, li_ref, cy_ref, y_ref, h_ref):
    p = lr_ref.shape[-1] // 2
    bu = jnp.einsum('gbc,gcp->gbp', u_ref[...], bb_ref[...], preferred_element_type=F32)
    h0 = h0_ref[...]
    h0s = jnp.concatenate([h0[..., p:], h0[..., :p]], axis=-1)
    h = lr_ref[...] * h0 + li_ref[...] * h0s + bu
    h_ref[...] = h
    y_ref[...] = jnp.einsum('gbp,gpc->gbc', h.astype(BF16), cy_ref[...], preferred_element_type=F32)


def _ssm_sample(u, h0_re, h0_im, tables, c_re, c_im):
    lb_re, lb_im, bbr, bbi = tables[-1]
    B = u.shape[0]
    G, C, P = N_SSM_GROUPS, SSM_GROUP, SSM_STATE
    ug = jnp.transpose(u.reshape(B, G, C), (1, 0, 2)).astype(BF16)
    h0 = jnp.transpose(jnp.concatenate([h0_re, h0_im], -1), (1, 0, 2)).astype(F32)
    bb = jnp.concatenate([jnp.transpose(bbr, (0, 2, 1)), jnp.transpose(bbi, (0, 2, 1))], -1).astype(BF16)
    lr = jnp.concatenate([lb_re, lb_re], -1)[:, None, :]
    li = jnp.concatenate([-lb_im, lb_im], -1)[:, None, :]
    cy = jnp.concatenate([jnp.transpose(c_re, (0, 2, 1)), -jnp.transpose(c_im, (0, 2, 1))], 1).astype(BF16)
    y, h = pl.pallas_call(
        _ssm_step_kernel,
        out_shape=(jax.ShapeDtypeStruct((G, B, C), F32), jax.ShapeDtypeStruct((G, B, 2 * P), F32)),
        name="ssm_step",
    )(ug, h0, bb, lr, li, cy)
    return jnp.transpose(y, (1, 0, 2)).reshape(B, D_SSM), jnp.transpose(h, (1, 0, 2))


def _compress_tables(phi_pe, phi_w1, phi_b1, phi_w2, phi_b2):
    S, H, Dh = CMP_STRIDE, N_KV_HEADS, HEAD_DIM
    w1 = phi_w1.reshape(2, 2, S, Dh, Dh)
    eye_c = jnp.eye(2, dtype=F32)
    eye_h = jnp.eye(H, dtype=F32)
    wbig = jnp.einsum('cajde,xc,yh->jxydache', w1, eye_c, eye_h).reshape(S * 2 * H * Dh, 2 * 2 * H * Dh)
    pe = jnp.transpose(phi_pe.reshape(2, 2, S, Dh), (1, 2, 0, 3))
    pe_rows = jnp.broadcast_to(pe[:, :, :, None, :], (2, S, 2, H, Dh)).reshape(2, S * 2 * H * Dh)
    n = 2 * H * Dh
    pe_w = (jnp.dot(pe_rows[0], wbig[:, :n], precision=HIGHEST) + jnp.dot(pe_rows[1], wbig[:, n:], precision=HIGHEST))
    b1 = jnp.broadcast_to(phi_b1[:, None, :], (2, H, Dh)).reshape(1, n) + pe_w[None, :]
    w2 = jnp.einsum('cef,cx,hy->chexyf', phi_w2, eye_c, eye_h).reshape(n, n)
    b2 = jnp.broadcast_to(phi_b2[:, None, :], (2, H, Dh)).reshape(1, n)
    return wbig.astype(BF16), b1, w2.astype(BF16), b2


def _compress_in_kernel(x_ref, w_ref, z_ref):
    z_ref[0] = jnp.dot(x_ref[0].astype(BF16), w_ref[...], preferred_element_type=F32)


def _compress_in(x2, tables):
    wbig = tables[0]
    N2 = wbig.shape[1]
    B, n, K = x2.shape
    tr = math.gcd(n, 256)
    return pl.pallas_call(
        _compress_in_kernel,
        out_shape=jax.ShapeDtypeStruct((B, n, N2), F32),
        grid=(B, n // tr),
        in_specs=[pl.BlockSpec((1, tr, K), lambda b, i: (b, i, 0)),
                  pl.BlockSpec((K, N2), lambda b, i: (0, 0))],
        out_specs=pl.BlockSpec((1, tr, N2), lambda b, i: (b, i, 0)),
        compiler_params=_cparams("parallel", "parallel"),
        name="compress_in",
    )(x2, wbig)


def _compress_in_paged_kernel(pt_ref, *refs, n_pg):
    x_refs = refs[:n_pg]
    w_ref, z_ref = refs[n_pg:n_pg + 2]
    s_refs = refs[n_pg + 2:]
    for k in range(n_pg):
        t = x_refs[k][0].reshape(D_KV, PAGE_SIZE).T
        for c, s_ref in enumerate(s_refs):
            s_ref[k * PAGE_SIZE:(k + 1) * PAGE_SIZE, :] = t[:, c * LANE:(c + 1) * LANE]
    rows = n_pg * PAGE_SIZE // CMP_STRIDE
    z = jnp.zeros((rows, w_ref.shape[1]), F32)
    for j in range(CMP_STRIDE):
        xj = jnp.concatenate([s_ref[pl.ds(j, rows, stride=CMP_STRIDE), :] for s_ref in s_refs], axis=1)
        z = z + jnp.dot(xj.astype(BF16), w_ref[j * D_KV:(j + 1) * D_KV, :], preferred_element_type=F32)
    z_ref[0] = z


def _compress_in_paged(pool_t, page_table, tables):
    wbig = tables[0]
    N2 = wbig.shape[1]
    K = wbig.shape[0]
    B, n_pages = page_table.shape
    n_pg = math.gcd(n_pages, PAGES_PER_STEP)
    rows = n_pg * PAGE_SIZE // CMP_STRIDE
    page_spec = lambda k: pl.BlockSpec((1,) + pool_t.shape[1:],
                                       lambda b, i, pt, k=k: (pt[b, i * n_pg + k], 0, 0, 0, 0))
    grid_spec = pltpu.PrefetchScalarGridSpec(
        num_scalar_prefetch=1,
        grid=(B, n_pages // n_pg),
        in_specs=[page_spec(k) for k in range(n_pg)] + [pl.BlockSpec((K, N2), lambda b, i, pt: (0, 0))],
        out_specs=pl.BlockSpec((1, rows, N2), lambda b, i, pt: (b, i, 0)),
        scratch_shapes=[pltpu.VMEM((n_pg * PAGE_SIZE, LANE), F32) for _ in range(D_KV // LANE)],
    )
    return pl.pallas_call(
        functools.partial(_compress_in_paged_kernel, n_pg=n_pg),
        out_shape=jax.ShapeDtypeStruct((B, n_pages * PAGE_SIZE // CMP_STRIDE, N2), F32),
        grid_spec=grid_spec,
        compiler_params=_cparams("arbitrary", "arbitrary"),
        name="compress_in_paged",
    )(page_table, *([pool_t] * n_pg), wbig)


def _compress_out_kernel(z_ref, b1_ref, w2_ref, b2_ref, o_ref):
    z = z_ref[0]
    n = z.shape[-1] // 2
    rows = z.shape[0]
    second = pltpu.roll(z[:, n:], rows - 1, axis=0)
    hdn = jax.nn.gelu(z[:, :n] + second + b1_ref[...])
    o_ref[0, :rows, :] = jnp.dot(hdn.astype(BF16), w2_ref[...], preferred_element_type=F32) + b2_ref[...]
    if o_ref.shape[1] > rows:
        o_ref[0, rows:, :] = jnp.zeros((o_ref.shape[1] - rows, n), F32)


def _compress_out(z, tables, n_out):
    _, b1, w2, b2 = tables
    B, n, N2 = z.shape
    return pl.pallas_call(
        _compress_out_kernel,
        out_shape=jax.ShapeDtypeStruct((B, n_out, N2 // 2), F32),
        grid=(B,),
        in_specs=[pl.BlockSpec((1, n, N2), lambda b: (b, 0, 0)),
                  pl.BlockSpec((1, N2 // 2), lambda b: (0, 0)),
                  pl.BlockSpec((N2 // 2, N2 // 2), lambda b: (0, 0)),
                  pl.BlockSpec((1, N2 // 2), lambda b: (0, 0))],
        out_specs=pl.BlockSpec((1, n_out, N2 // 2), lambda b: (b, 0, 0)),
        compiler_params=_cparams("parallel"),
        name="compress_out",
    )(z, b1, w2, b2)


def _rel_bucket(dist):
    n = jnp.maximum(dist, 0)
    max_exact = NUM_BUCKETS // 2
    nf = jnp.maximum(n, 1).astype(F32)
    large = max_exact + (jnp.log(nf / max_exact) / math.log(REL_MAX_DIST / max_exact)
                         * (NUM_BUCKETS - max_exact)).astype(jnp.int32)
    large = jnp.minimum(large, NUM_BUCKETS - 1)
    return jnp.where(n < max_exact, n, large)


def _bias_by_distance(rel_bias, n_max):
    onehot = (_rel_bucket(jnp.arange(n_max))[None, :] == jnp.arange(NUM_BUCKETS)[:, None]).astype(F32)
    return jnp.dot(jnp.transpose(rel_bias.astype(F32)), onehot, precision=HIGHEST)


def _shifted_chunks(bias_n, pad, n_chunks, width):
    n = min(bias_n.shape[1], n_chunks * width - pad)
    ext = jnp.concatenate([jnp.broadcast_to(bias_n[:, :1], (N_HEADS, pad)), bias_n[:, :n],
                           jnp.zeros((N_HEADS, n_chunks * width - pad - n), F32)], axis=1)
    return ext.reshape(N_HEADS, n_chunks, width)


def _bias_tables_kernel(ed_ref, ec_ref, tzs_ref, tzw_ref, cmp_ref, *, tq, tk, n_qt):
    n_ds, n_dw, n_j = tzs_ref.shape[1] - 1, tzw_ref.shape[1] - 1, cmp_ref.shape[1] // 8
    tzs_ref[0, n_ds] = jnp.full((tk, tq), NEG, F32)
    tzw_ref[0, n_dw] = jnp.full((tk, tq), NEG, F32)
    w = tq + tk
    c = lax.broadcasted_iota(jnp.int32, (tk, tq), 0)
    r = lax.broadcasted_iota(jnp.int32, (tk, tq), 1)
    for d in range(n_ds):
        v = jnp.concatenate([ed_ref[0, d:d + 1, :], ed_ref[0, d + 1:d + 2, :]], axis=1)
        t = pltpu.roll(jnp.broadcast_to(v, (tk, w)), w - (tk - 1), axis=1, stride=1, stride_axis=0)[:, :tq]
        dist = d * tk + r - c
        tzs_ref[0, d] = jnp.where(dist >= 0, t, NEG)
        if d < n_dw:
            tzw_ref[0, d] = jnp.where((dist >= 0) & (dist <= WINDOW), t, NEG)
    for j in range(n_j):
        dd = n_qt - 1 - j
        c0, c1 = max(dd, 0), max(dd + 1, 0)
        v = jnp.concatenate([ec_ref[0, c0:c0 + 1, :], ec_ref[0, c1:c1 + 1, :]], axis=1)
        t = pltpu.roll(jnp.broadcast_to(v, (8, w)), w - 7 * CMP_STRIDE, axis=1, stride=CMP_STRIDE, stride_axis=0)
        cmp_ref[0, j * 8:(j + 1) * 8, :] = t[:, :tq]


def _bias_tables(bias_n, n_qt, n_rb, n_ds, n_dw, tq, tk):
    assert tq == tk == 8 * CMP_STRIDE and n_dw <= n_ds
    n_j = n_rb + n_qt - 1
    ed = _shifted_chunks(bias_n, tk - 1, n_ds + 1, tq)
    ec = _shifted_chunks(bias_n, 7 * CMP_STRIDE + CMP_BLOCK - 1, n_qt + 1, tq)
    head = lambda a: pl.BlockSpec((1,) + a.shape[1:], lambda h: (h,) + (0,) * (a.ndim - 1))
    outs = (jax.ShapeDtypeStruct((N_HEADS, n_ds + 1, tk, tq), F32),
            jax.ShapeDtypeStruct((N_HEADS, n_dw + 1, tk, tq), F32),
            jax.ShapeDtypeStruct((N_HEADS, n_j * 8, tq), F32))
    tzs, tzw, cmp = pl.pallas_call(
        functools.partial(_bias_tables_kernel, tq=tq, tk=tk, n_qt=n_qt),
        out_shape=outs,
        grid=(N_HEADS,),
        in_specs=[head(ed), head(ec)],
        out_specs=tuple(head(o) for o in outs),
        compiler_params=_cparams("parallel"),
        name="bias_tables",
    )(ed, ec)
    grp = lambda a: a.reshape((N_KV_HEADS, GQA) + a.shape[1:])
    return grp(tzs), grp(tzw), cmp


def _pool_matrix(n_cmp_pad, n_blk_pad):
    r = SEL_BLOCK // CMP_STRIDE
    i = np.arange(n_cmp_pad)[None, :]
    j = np.arange(n_blk_pad)[:, None]
    return ((i >= r * j - 1) & (i <= r * j + r - 1)).astype(np.float32)


def _cmp_select_kernel(q_ref, k_ref, vt_ref, bias_ref, pool_ref, o_ref, sel_ref, *, tq, n_cmp):
    qt = pl.program_id(2)
    n_qt = pl.num_programs(2)
    q = q_ref[0, 0].reshape(GQA * tq, HEAD_DIM)
    k = k_ref[0, 0]
    nc = k.shape[0]
    s = _nt_dot(k, q)
    row0 = pl.multiple_of((n_qt - 1 - qt) * 8, 8)
    s = s + jnp.concatenate([bias_ref[g, pl.ds(row0, nc), :] for g in range(GQA)], axis=-1)
    t_pos = qt * tq + (lax.broadcasted_iota(jnp.int32, (nc, GQA * tq), 1) % tq)
    ci = lax.broadcasted_iota(jnp.int32, (nc, GQA * tq), 0)
    mask = (ci * CMP_STRIDE + CMP_BLOCK - 1 <= t_pos) & (ci < n_cmp)
    s = jnp.where(mask, s, NEG)
    m = jnp.max(s, axis=0, keepdims=True)
    p = jnp.where(mask, jnp.exp(s - m), 0.0)
    p = p / jnp.maximum(jnp.sum(p, axis=0, keepdims=True), 1e-30)
    ot = jnp.dot(vt_ref[0, 0], p.astype(BF16), preferred_element_type=F32)
    o_ref[0] = jnp.concatenate([ot[:, g * tq:(g + 1) * tq].T for g in range(GQA)], axis=-1)
    imp = p[:, 0:tq]
    for g in range(1, GQA):
        imp = imp + p[:, g * tq:(g + 1) * tq]
    sb = jnp.dot(pool_ref[...], imp, precision=HIGHEST, preferred_element_type=F32)
    nb = sb.shape[0]
    blk = lax.broadcasted_iota(jnp.int32, (nb, tq), 0)
    cur = (qt * tq + lax.broadcasted_iota(jnp.int32, (nb, tq), 1)) // SEL_BLOCK
    causal = blk <= cur
    forced = (blk == 0) | (blk == cur) | (blk == cur - 1)
    sc = jnp.where(forced & causal, 1e4, jnp.where(causal, sb, -1.0))
    rank = jnp.zeros((nb, tq), jnp.int32)
    for i in range(nb):
        row = sc[i:i + 1, :]
        ahead = (row > sc) | ((row == sc) & (blk > i))
        rank = rank + ahead.astype(jnp.int32)
    sel_ref[0, 0] = jnp.where((rank < N_SEL) & causal, 0.0, NEG)


def _cmp_select_prompt(q5, kc, vct, bias_tab, pool, n_cmp):
    B, _, _, T, _ = q5.shape
    NC = kc.shape[2]
    NB = pool.shape[0]
    R = bias_tab.shape[1]
    tq = ATT_TQ
    return pl.pallas_call(
        functools.partial(_cmp_select_kernel, tq=tq, n_cmp=n_cmp),
        out_shape=(jax.ShapeDtypeStruct((B, T, D_ATT), F32),
                   jax.ShapeDtypeStruct((B, N_KV_HEADS, NB, T), F32)),
        grid=(B, N_KV_HEADS, T // tq),
        in_specs=[pl.BlockSpec((1, 1, GQA, tq, HEAD_DIM), lambda b, h, i: (b, h, 0, i, 0)),
                  pl.BlockSpec((1, 1, NC, HEAD_DIM), lambda b, h, i: (b, h, 0, 0)),
                  pl.BlockSpec((1, 1, HEAD_DIM, NC), lambda b, h, i: (b, h, 0, 0)),
                  pl.BlockSpec((GQA, R, tq), lambda b, h, i: (h, 0, 0)),
                  pl.BlockSpec((NB, NC), lambda b, h, i: (0, 0))],
        out_specs=(pl.BlockSpec((1, tq, GQA * HEAD_DIM), lambda b, h, i: (b, i, h)),
                   pl.BlockSpec((1, 1, NB, tq), lambda b, h, i: (b, h, 0, i))),
        compiler_params=_cparams("parallel", "parallel", "parallel"),
        name="cmp_select_prompt",
    )(q5, kc, vct, bias_tab, pool)


def _sel_win_kernel(q_ref, ks_ref, vst_ref, kw_ref, vwt_ref, sel_ref, tzs_ref, tzw_ref, os_ref, ow_ref, *, tq):
    tk = ATT_TK
    qt = pl.program_id(2)
    q = q_ref[0, 0].reshape(GQA * tq, HEAD_DIM)
    width = GQA * tq
    per_tile = tk // SEL_BLOCK

    def make_sweep(k_ref, vt_ref, tz_ref, use_sel, n_chains, single_trip):
        n_d = tz_ref.shape[2] - 1

        def scores(kt, hi):
            pad = kt > hi
            kt = jnp.minimum(kt, hi)
            off = pl.multiple_of(kt * tk, tk)
            k = k_ref[0, 0, pl.ds(off, tk), :]
            d = jnp.where(pad, n_d, jnp.minimum(qt - kt, n_d - 1))
            bias = [tz_ref[0, g, d] for g in range(GQA)]
            if use_sel:
                rows = sel_ref[0, 0, pl.ds(kt * per_tile, per_tile), :]
                selb = jnp.concatenate([jnp.broadcast_to(rows[i:i + 1], (SEL_BLOCK, tq))
                                        for i in range(per_tile)], axis=0)
                bias = [b + selb for b in bias]
            return _nt_dot(k, q) + jnp.concatenate(bias, axis=1)

        def values_t(kt, lo, hi):
            off = pl.multiple_of(jnp.clip(kt, lo, hi) * tk, tk)
            return vt_ref[0, 0, :, pl.ds(off, tk)]

        def sweep(lo, hi):
            n_trips = (hi - lo + n_chains) // n_chains
            chain0 = (jnp.full((1, width), 0.5 * NEG, F32), jnp.zeros((1, width), F32),
                      jnp.zeros((HEAD_DIM, width), F32), jnp.ones((1, width), F32), jnp.zeros((tk, width), BF16))

            def trip(i, chains):
                kt = lo + n_chains * i
                pv = [jnp.dot(values_t(kt - n_chains + c, lo, hi), chains[c][4], preferred_element_type=F32)
                      for c in range(n_chains)]
                ss = [scores(kt + c, hi) for c in range(n_chains)]
                out = []
                for c in range(n_chains):
                    m, l, acc, alpha_prev, _ = chains[c]
                    m_new = jnp.maximum(m, jnp.max(ss[c], axis=0, keepdims=True))
                    alpha = jnp.exp(m - m_new)
                    p = jnp.exp(ss[c] - m_new)
                    l = alpha * l + jnp.sum(p, axis=0, keepdims=True)
                    out.append((m_new, l, alpha_prev * acc + pv[c], alpha, p.astype(BF16)))
                return tuple(out)

            if single_trip:
                done = []
                for c in range(n_chains):
                    s = scores(lo + c, hi)
                    m = jnp.maximum(jnp.max(s, axis=0, keepdims=True), 0.5 * NEG)
                    p = jnp.exp(s - m)
                    done.append((m, jnp.sum(p, axis=0, keepdims=True),
                                 jnp.dot(values_t(lo + c, lo, hi), p.astype(BF16), preferred_element_type=F32)))
            else:
                chains = lax.fori_loop(0, n_trips, trip, (chain0,) * n_chains)
                kt_last = lo + n_chains * (n_trips - 1)
                done = []
                for c in range(n_chains):
                    m, l, acc, alpha, p = chains[c]
                    done.append((m, l, alpha * acc + jnp.dot(values_t(kt_last + c, lo, hi), p,
                                                              preferred_element_type=F32)))
            m_all = functools.reduce(jnp.maximum, [m for m, _, _ in done])
            num = den = 0.0
            for m, l, acc in done:
                e = jnp.exp(m - m_all)
                num = num + acc * e
                den = den + l * e
            o = num / jnp.maximum(den, 1e-30)
            return jnp.concatenate([o[:, g * tq:(g + 1) * tq].T for g in range(GQA)], axis=-1)
        return sweep

    n_win = tzw_ref.shape[2] - 1
    os_ref[0] = make_sweep(ks_ref, vst_ref, tzs_ref, True, SEL_CHAINS, False)(0, qt)
    ow_ref[0] = make_sweep(kw_ref, vwt_ref, tzw_ref, False, n_win, True)(jnp.maximum(qt - (n_win - 1), 0), qt)


def _sel_win_prompt(q5, ks, vst, kw, vwt, sel, tzs, tzw):
    B, _, _, T, _ = q5.shape
    NB = sel.shape[2]
    tq = ATT_TQ
    k_spec = pl.BlockSpec((1, 1, T, HEAD_DIM), lambda b, h, i: (b, h, 0, 0))
    vt_spec = pl.BlockSpec((1, 1, HEAD_DIM, T), lambda b, h, i: (b, h, 0, 0))
    tz_spec = lambda tz: pl.BlockSpec((1,) + tz.shape[1:], lambda b, h, i: (h, 0, 0, 0, 0))
    o_spec = pl.BlockSpec((1, tq, GQA * HEAD_DIM), lambda b, h, i: (b, i, h))
    return pl.pallas_call(
        functools.partial(_sel_win_kernel, tq=tq),
        out_shape=(jax.ShapeDtypeStruct((B, T, D_ATT), F32), jax.ShapeDtypeStruct((B, T, D_ATT), F32)),
        grid=(B, N_KV_HEADS, T // tq),
        in_specs=[pl.BlockSpec((1, 1, GQA, tq, HEAD_DIM), lambda b, h, i: (b, h, 0, i, 0)),
                  k_spec, vt_spec, k_spec, vt_spec,
                  pl.BlockSpec((1, 1, NB, tq), lambda b, h, i: (b, h, 0, i)),
                  tz_spec(tzs), tz_spec(tzw)],
        out_specs=(o_spec, o_spec),
        compiler_params=_cparams("parallel", "parallel", "parallel"),
        name="sel_win_prompt",
    )(q5, ks, vst, kw, vwt, sel, tzs, tzw)


def _gate_expand_matrix():
    m = np.zeros((3, 2 * LANE, D_ATT), np.float32)
    for r in range(3):
        for h in range(N_HEADS):
            m[r, h * 3 + r, h * HEAD_DIM:(h + 1) * HEAD_DIM] = 1.0
            m[r, LANE + h * 3 + r, h * HEAD_DIM:(h + 1) * HEAD_DIM] = 1.0
    return m


def _split_bf16(x):
    hi = x.astype(BF16)
    return hi, (x - hi.astype(F32)).astype(BF16)


def _post_mixer_kernel(y_ref, u_ref, oc_ref, os_ref, ow_ref, g_ref, x_ref, gate_ref, sh_ref, sc_ref,
                       dskip_ref, wglu_ref, bglu_ref, gexp_ref, wout_ref, lng_ref, lnb_ref,
                       wr_ref, br_ref, x1_ref, hm_ref, te_ref, tw_ref):
    y = y_ref[0] + dskip_ref[...] * u_ref[0]
    gl = jax.nn.gelu(y)
    ssm = gl * jax.nn.sigmoid(jnp.dot(gl.astype(BF16), wglu_ref[...], preferred_element_type=F32)
                              + bglu_ref[...])
    sg = jnp.concatenate(_split_bf16(jax.nn.sigmoid(g_ref[0])), axis=1)
    att = jnp.zeros_like(oc_ref[0])
    for r, o_ref in enumerate((oc_ref, os_ref, ow_ref)):
        att = att + jnp.dot(sg, gexp_ref[r], preferred_element_type=F32) * o_ref[0]
    h = (jnp.dot(ssm.astype(BF16), wout_ref[:D_SSM, :], preferred_element_type=F32)
         + jnp.dot(att.astype(BF16), wout_ref[D_SSM:, :], preferred_element_type=F32))
    z = DN_ALPHA * x_ref[0] + gate_ref[0] * h
    x1 = _layer_norm(z) * lng_ref[...] + lnb_ref[...]
    x1_ref[0] = x1
    hm = _layer_norm(x1) * (1.0 + sc_ref[0]) + sh_ref[0]
    hm_ref[0] = hm
    hm_hi, hm_lo = _split_bf16(hm)
    logits = (jnp.dot(hm_hi, wr_ref[0], preferred_element_type=F32)
              + jnp.dot(hm_lo, wr_ref[0], preferred_element_type=F32)
              + jnp.dot(hm_hi, wr_ref[1], preferred_element_type=F32)) + br_ref[...]
    lane = lax.broadcasted_iota(jnp.int32, logits.shape, 1)
    work = jnp.where(lane < N_EXPERTS, logits, -jnp.inf)
    te = jnp.zeros(logits.shape, jnp.int32)
    tv = jnp.zeros(logits.shape, F32)
    for k in range(TOP_K):
        best = jnp.max(work, axis=-1, keepdims=True)
        arg = jnp.min(jnp.where(work == best, lane, LANE), axis=-1, keepdims=True)
        te = jnp.where(lane == k, arg, te)
        tv = jnp.where(lane == k, best, tv)
        work = jnp.where(lane == arg, -jnp.inf, work)
    ex = jnp.where(lane < TOP_K, jnp.exp(tv - tv[:, 0:1]), 0.0)
    te_ref[0] = te
    tw_ref[0] = ex / jnp.sum(ex, axis=-1, keepdims=True)


def _post_mixer(y, u, oc, osel, ow, g, x, gate, shift, scale, w, tm):
    B, T, D = x.shape
    R = gate.shape[1]
    rb = 1 if R == 1 else tm
    mod_map = (lambda b, i: (b, 0, 0)) if R == 1 else (lambda b, i: (b, i, 0))
    row = lambda n: pl.BlockSpec((1, tm, n), lambda b, i: (b, i, 0))
    mod = pl.BlockSpec((1, rb, D), mod_map)
    full = lambda a: pl.BlockSpec(a.shape, lambda b, i: (0,) * a.ndim)
    consts = (w['d_skip'], w['w_glu'], w['b_glu'], w['gexp'], w['w_out'], w['ln1_g'], w['ln1_b'],
              w['w_router'], w['b_router'])
    return pl.pallas_call(
        _post_mixer_kernel,
        out_shape=(jax.ShapeDtypeStruct((B, T, D), F32), jax.ShapeDtypeStruct((B, T, D), F32),
                   jax.ShapeDtypeStruct((B, T, LANE), jnp.int32), jax.ShapeDtypeStruct((B, T, LANE), F32)),
        grid=(B, T // tm),
        in_specs=[row(D_SSM), row(D_SSM), row(D_ATT), row(D_ATT), row(D_ATT), row(LANE), row(D),
                  mod, mod, mod] + [full(a) for a in consts],
        out_specs=(row(D), row(D), row(LANE), row(LANE)),
        compiler_params=_cparams("parallel", "parallel"),
        name="post_mixer",
    )(y, u, oc, osel, ow, g, x, gate, shift, scale, *consts)


def _expert_kernel(e_ref, blk_ref, lo_ref, hi_ref, first_ref, x_ref, wgu_ref, bgu_ref, wd_ref, bd_ref, o_ref,
                   wgu_s, wd_s):
    i = pl.program_id(0)
    fresh = (i == 0) | (e_ref[i] != e_ref[jnp.maximum(i - 1, 0)])

    @pl.when(fresh)
    def _():
        wgu_s[...] = wgu_ref[0].astype(BF16)
        wd_s[...] = wd_ref[0].astype(BF16)

    @pl.when(first_ref[i] == 1)
    def _():
        o_ref[...] = jnp.zeros_like(o_ref)

    @pl.when(hi_ref[i] > lo_ref[i])
    def _():
        gu = jnp.dot(x_ref[...].astype(BF16), wgu_s[...], preferred_element_type=F32) + bgu_ref[0]
        gate = jnp.minimum(gu[:, :D_FF], SWIGLU_LIMIT)
        up = jnp.clip(gu[:, D_FF:], -SWIGLU_LIMIT, SWIGLU_LIMIT)
        hh = (up + 1.0) * gate * jax.nn.sigmoid(SWIGLU_ALPHA * gate)
        y = jnp.dot(hh.astype(BF16), wd_s[...], preferred_element_type=F32) + bd_ref[0]
        row = blk_ref[i] * MOE_ROWS + lax.broadcasted_iota(jnp.int32, (MOE_ROWS, 1), 0)
        o_ref[...] = jnp.where((row >= lo_ref[i]) & (row < hi_ref[i]), y, o_ref[...])


def _experts(xb, items, w_gate_up, b_gate_up, w_down, b_down):
    rows, D = xb.shape
    n_items = items[0].shape[0]
    wmap = lambda i, e, blk, lo, hi, first: (e[i], 0, 0)
    rmap = lambda i, e, blk, lo, hi, first: (blk[i], 0)
    grid_spec = pltpu.PrefetchScalarGridSpec(
        num_scalar_prefetch=5,
        grid=(n_items,),
        in_specs=[pl.BlockSpec((MOE_ROWS, D), rmap),
                  pl.BlockSpec((1, D, 2 * D_FF), wmap),
                  pl.BlockSpec((1, 1, 2 * D_FF), wmap),
                  pl.BlockSpec((1, D_FF, D), wmap),
                  pl.BlockSpec((1, 1, D), wmap)],
        out_specs=pl.BlockSpec((MOE_ROWS, D), rmap),
        scratch_shapes=[pltpu.VMEM((D, 2 * D_FF), BF16), pltpu.VMEM((D_FF, D), BF16)],
    )
    return pl.pallas_call(
        _expert_kernel,
        out_shape=jax.ShapeDtypeStruct((rows, D), F32),
        grid_spec=grid_spec,
        compiler_params=_cparams("arbitrary"),
        name="moe_experts",
    )(*items, xb, w_gate_up, b_gate_up.reshape(N_EXPERTS, 1, 2 * D_FF), w_down,
      b_down.reshape(N_EXPERTS, 1, D))


def _moe_dispatch(top_e, n):
    blk = MOE_ROWS
    nk = n * TOP_K
    cb = 128
    assert nk % cb == 0
    e = top_e.reshape(-1)
    onehot = (e[:, None] == jnp.arange(N_EXPERTS)[None, :]).astype(F32)
    oh3 = onehot.reshape(nk // cb, cb, N_EXPERTS)
    tri = jnp.asarray(np.tril(np.ones((cb, cb), np.float32), -1))
    within = jnp.einsum('ij,bje->bie', tri, oh3, precision=HIGHEST)
    blk_tot = jnp.sum(oh3, axis=1)
    blk_off = jnp.cumsum(blk_tot, axis=0) - blk_tot
    counts = jnp.sum(blk_tot, axis=0)
    start = jnp.cumsum(counts) - counts
    dest = jnp.sum((within + blk_off[:, None, :] + start[None, None, :]) * oh3, axis=-1)
    dest = dest.reshape(nk).astype(jnp.int32)
    order = jnp.argsort(dest)
    n_blk = -(-nk // blk)
    row_tok = jnp.concatenate([(order // TOP_K).astype(jnp.int32), jnp.full((n_blk * blk - nk,), n, jnp.int32)])
    counts_i, start_i = counts.astype(jnp.int32), start.astype(jnp.int32)
    first_b = start_i // blk
    last_b = (start_i + counts_i - 1) // blk
    n_it = jnp.where(counts_i > 0, last_b - first_b + 1, 0)
    it_end = jnp.cumsum(n_it)
    it_start = it_end - n_it
    n_items = n_blk + N_EXPERTS - 1
    i = jnp.arange(n_items)
    live = i < it_end[-1]
    it_e = jnp.minimum(jnp.sum(it_end[None, :] <= i[:, None], axis=1), N_EXPERTS - 1)
    it_blk = jnp.where(live, first_b[it_e] + i - it_start[it_e], n_blk - 1)
    it_lo = jnp.where(live, start_i[it_e], 0)
    it_hi = jnp.where(live, start_i[it_e] + counts_i[it_e], 0)
    it_first = jnp.concatenate([jnp.ones((1,), jnp.int32), (it_blk[1:] != it_blk[:-1]).astype(jnp.int32)])
    items = tuple(a.astype(jnp.int32) for a in (it_e, it_blk, it_lo, it_hi, it_first))
    return row_tok, dest.reshape(n, TOP_K), items


def _final_kernel(x_ref, y0_ref, y1_ref, y2_ref, y3_ref, tw_ref, gate_ref, lng_ref, lnb_ref, o_ref):
    tw = tw_ref[0]
    y = jnp.zeros_like(x_ref[0])
    for k, y_ref in enumerate((y0_ref, y1_ref, y2_ref, y3_ref)):
        y = y + tw[:, k:k + 1] * y_ref[0]
    z = DN_ALPHA * x_ref[0] + gate_ref[0] * y
    o_ref[0] = _layer_norm(z) * lng_ref[...] + lnb_ref[...]


def _final(x1, ys, tw, gate, ln_g, ln_b, tm):
    B, T, D = x1.shape
    R = gate.shape[1]
    rb = 1 if R == 1 else tm
    mod_map = (lambda b, i: (b, 0, 0)) if R == 1 else (lambda b, i: (b, i, 0))
    row = lambda n: pl.BlockSpec((1, tm, n), lambda b, i: (b, i, 0))
    vec = pl.BlockSpec((1, D), lambda b, i: (0, 0))
    return pl.pallas_call(
        _final_kernel,
        out_shape=jax.ShapeDtypeStruct((B, T, D), F32),
        grid=(B, T // tm),
        in_specs=[row(D), row(D), row(D), row(D), row(D), row(LANE),
                  pl.BlockSpec((1, rb, D), mod_map), vec, vec],
        out_specs=row(D),
        compiler_params=_cparams("parallel", "parallel"),
        name="moe_combine_ln",
    )(x1, *ys, tw, gate, ln_g, ln_b)


def _cmp_select_step_kernel(q_ref, kv_ref, bias_ref, pool_ref, o_ref, idx_ref, *, n_cmp, n_blk, q_pos):
    q = q_ref[0].astype(BF16)
    ncp = kv_ref.shape[1]
    nbp = pool_ref.shape[1]
    hd = HEAD_DIM
    kv = kv_ref[0]
    kb = [kv[:, h * hd:(h + 1) * hd].astype(BF16) for h in range(N_KV_HEADS)]
    vb = [kv[:, (N_KV_HEADS + h) * hd:(N_KV_HEADS + h + 1) * hd].astype(BF16) for h in range(N_KV_HEADS)]
    row = lax.broadcasted_iota(jnp.int32, (N_HEADS, 1), 0)
    first = row < GQA
    s = jnp.where(first, _nt_dot(q, kb[0]), _nt_dot(q, kb[1])) * (hd ** -0.5)
    s = s + bias_ref[...]
    ci = lax.broadcasted_iota(jnp.int32, (N_HEADS, ncp), 1)
    mask = (ci * CMP_STRIDE + CMP_BLOCK - 1 <= q_pos) & (ci < n_cmp)
    s = jnp.where(mask, s, NEG)
    m = jnp.max(s, axis=-1, keepdims=True)
    p = jnp.where(mask, jnp.exp(s - m), 0.0)
    p = p / jnp.maximum(jnp.sum(p, axis=-1, keepdims=True), 1e-30)
    pb = p.astype(BF16)
    o_ref[0] = jnp.where(first, jnp.dot(pb, vb[0], preferred_element_type=F32),
                         jnp.dot(pb, vb[1], preferred_element_type=F32))
    imp0 = jnp.sum(jnp.where(first, p, 0.0), axis=0, keepdims=True)
    imp1 = jnp.sum(jnp.where(first, 0.0, p), axis=0, keepdims=True)
    imp = jnp.where(first, imp0, imp1)
    sb = jnp.dot(imp, pool_ref[...], precision=HIGHEST, preferred_element_type=F32)
    cur = q_pos // SEL_BLOCK
    bi = lax.broadcasted_iota(jnp.int32, (nbp, nbp), 0)
    bj = lax.broadcasted_iota(jnp.int32, (nbp, nbp), 1)
    blk = lax.broadcasted_iota(jnp.int32, (1, nbp), 1)
    causal = blk <= cur
    forced = (blk == 0) | (blk == cur) | (blk == cur - 1)
    rsel = lax.broadcasted_iota(jnp.int32, (N_SEL, nbp), 0)
    for h in range(N_KV_HEADS):
        sc = jnp.where(forced & causal, 1e4, jnp.where(causal, sb[h * GQA:h * GQA + 1, :], -1.0))
        sc = jnp.where(blk < n_blk, sc, -2.0)
        scb = jnp.broadcast_to(sc, (nbp, nbp))
        col = jnp.sum(jnp.where(bi == bj, scb, 0.0), axis=1, keepdims=True)
        ahead = (col > scb) | ((col == scb) & (bi < bj))
        rank = jnp.sum(ahead.astype(jnp.int32), axis=0, keepdims=True)
        hit = jnp.broadcast_to(rank, (N_SEL, nbp)) == rsel
        idx = jnp.sum(jnp.where(hit, jnp.broadcast_to(blk, (N_SEL, nbp)), 0), axis=1, keepdims=True)
        idx_ref[0, h] = jnp.broadcast_to(idx, (N_SEL, LANE))


def _cmp_select_step(q, ckv, bias, pool, n_cmp, n_blk, q_pos):
    B = q.shape[0]
    NCp = ckv.shape[1]
    return pl.pallas_call(
        functools.partial(_cmp_select_step_kernel, n_cmp=n_cmp, n_blk=n_blk, q_pos=q_pos),
        out_shape=(jax.ShapeDtypeStruct((B, N_HEADS, HEAD_DIM), F32),
                   jax.ShapeDtypeStruct((B, N_KV_HEADS, N_SEL, LANE), jnp.int32)),
        grid=(B,),
        in_specs=[pl.BlockSpec((1, N_HEADS, HEAD_DIM), lambda b: (b, 0, 0)),
                  pl.BlockSpec((1, NCp, D_KV), lambda b: (b, 0, 0)),
                  pl.BlockSpec(bias.shape, lambda b: (0, 0)),
                  pl.BlockSpec(pool.shape, lambda b: (0, 0))],
        out_specs=(pl.BlockSpec((1, N_HEADS, HEAD_DIM), lambda b: (b, 0, 0)),
                   pl.BlockSpec((1, N_KV_HEADS, N_SEL, LANE), lambda b: (b, 0, 0, 0))),
        compiler_params=_cparams("parallel"),
        name="cmp_select_step",
    )(q, ckv, bias, pool)


def _sel_step_kernel(pg_ref, idx_ref, q_ref, *refs, n_past, q_pos):
    page_refs = refs[:N_SEL]
    new_ref, bias_ref, kpos_ref, o_ref = refs[N_SEL:]
    b, h = pl.program_id(0), pl.program_id(1)
    base = (b * N_KV_HEADS + h) * N_SEL
    kts, vts = [], []
    for j in range(N_SEL):
        is_new = idx_ref[base + j] >= n_past
        kts.append(jnp.where(is_new, new_ref[0, 0, 0], page_refs[j][0, 0, 0]))
        vts.append(jnp.where(is_new, new_ref[0, 1, 0], page_refs[j][0, 1, 0]))
    kt = jnp.concatenate(kts, axis=1).astype(BF16)
    vt = jnp.concatenate(vts, axis=1).astype(BF16)
    s = jnp.dot(q_ref[0].astype(BF16), kt, preferred_element_type=F32) * (HEAD_DIM ** -0.5) + bias_ref[0, 0]
    mask = kpos_ref[0, 0] <= q_pos
    s = jnp.where(mask, s, NEG)
    m = jnp.max(s, axis=-1, keepdims=True)
    p = jnp.where(mask, jnp.exp(s - m), 0.0)
    l = jnp.sum(p, axis=-1, keepdims=True)
    o_ref[0, 0] = _nt_dot(p.astype(BF16), vt) / jnp.maximum(l, 1e-30)


def _sel_step(q, pool_t, new_t, bias_sel, kpos, pages, idx_flat, n_past, q_pos):
    B = q.shape[0]
    nk = N_SEL * PAGE_SIZE
    slot = lambda b, h, j: (b * N_KV_HEADS + h) * N_SEL + j
    page_spec = lambda j: pl.BlockSpec((1, 2, 1, HEAD_DIM, PAGE_SIZE),
                                       lambda b, h, pg, ix, j=j: (pg[slot(b, h, j)], 0, h, 0, 0))
    grid_spec = pltpu.PrefetchScalarGridSpec(
        num_scalar_prefetch=2,
        grid=(B, N_KV_HEADS),
        in_specs=[pl.BlockSpec((1, N_HEADS, HEAD_DIM), lambda b, h, pg, ix: (b, 0, 0))]
        + [page_spec(j) for j in range(N_SEL)]
        + [pl.BlockSpec((1, 2, 1, HEAD_DIM, PAGE_SIZE), lambda b, h, pg, ix: (b, 0, h, 0, 0)),
           pl.BlockSpec((1, 1, N_HEADS, nk), lambda b, h, pg, ix: (b, h, 0, 0)),
           pl.BlockSpec((1, 1, 1, nk), lambda b, h, pg, ix: (b, h, 0, 0))],
        out_specs=pl.BlockSpec((1, 1, N_HEADS, HEAD_DIM), lambda b, h, pg, ix: (b, h, 0, 0)),
    )
    return pl.pallas_call(
        functools.partial(_sel_step_kernel, n_past=n_past, q_pos=q_pos),
        out_shape=jax.ShapeDtypeStruct((B, N_KV_HEADS, N_HEADS, HEAD_DIM), F32),
        grid_spec=grid_spec,
        compiler_params=_cparams("arbitrary", "arbitrary"),
        name="sel_step",
    )(pages, idx_flat, q, *([pool_t] * N_SEL), new_t, bias_sel, kpos)


def _win_step_kernel(q_ref, w_ref, new_ref, bias_ref, bias0_ref, o_ref):
    q = q_ref[0]
    qb = q.astype(BF16)
    row = lax.broadcasted_iota(jnp.int32, (N_HEADS, 1), 0)
    first = row < GQA
    w = w_ref[0]
    hd = HEAD_DIM
    kb = [w[:, h * hd:(h + 1) * hd].astype(BF16) for h in range(N_KV_HEADS)]
    vb = [w[:, (N_KV_HEADS + h) * hd:(N_KV_HEADS + h + 1) * hd].astype(BF16) for h in range(N_KV_HEADS)]
    s = jnp.where(first, _nt_dot(qb, kb[0]), _nt_dot(qb, kb[1])) * (hd ** -0.5) + bias_ref[...]
    new = new_ref[0]
    kn = jnp.where(first, new[:, 0:hd], new[:, hd:2 * hd])
    vn = jnp.where(first, new[:, 2 * hd:3 * hd], new[:, 3 * hd:])
    sn = jnp.sum(q * kn, axis=-1, keepdims=True) * (hd ** -0.5) + bias0_ref[...]
    m = jnp.maximum(jnp.max(s, axis=-1, keepdims=True), sn)
    p = jnp.exp(s - m)
    pn = jnp.exp(sn - m)
    l = jnp.sum(p, axis=-1, keepdims=True) + pn
    pb = p.astype(BF16)
    acc = jnp.where(first, jnp.dot(pb, vb[0], preferred_element_type=F32),
                    jnp.dot(pb, vb[1], preferred_element_type=F32)) + pn * vn
    o_ref[0] = acc / jnp.maximum(l, 1e-30)


def _win_step(q, win, new, bias, bias0):
    B, W, _ = win.shape
    return pl.pallas_call(
        _win_step_kernel,
        out_shape=jax.ShapeDtypeStruct((B, N_HEADS, HEAD_DIM), F32),
        grid=(B,),
        in_specs=[pl.BlockSpec((1, N_HEADS, HEAD_DIM), lambda b: (b, 0, 0)),
                  pl.BlockSpec((1, W, D_KV), lambda b: (b, 0, 0)),
                  pl.BlockSpec((1, 1, D_KV), lambda b: (b, 0, 0)),
                  pl.BlockSpec((N_HEADS, W), lambda b: (0, 0)),
                  pl.BlockSpec((N_HEADS, 1), lambda b: (0, 0))],
        out_specs=pl.BlockSpec((1, N_HEADS, HEAD_DIM), lambda b: (b, 0, 0)),
        compiler_params=_cparams("parallel"),
        name="win_step",
    )(q, win, new, bias, bias0)


def _split_heads(kv, dtype):
    B, L, _ = kv.shape
    kv5 = kv.reshape(B, L, 2, N_KV_HEADS, HEAD_DIM)
    return (jnp.transpose(kv5[:, :, 0], (0, 2, 1, 3)).astype(dtype),
            jnp.transpose(kv5[:, :, 1], (0, 2, 1, 3)).astype(dtype))


def _nsa_prompt(q, kvc, kvs, kvw, cmp_tab, rel_bias):
    B, T, _ = q.shape
    nc = T // CMP_STRIDE
    nb = T // SEL_BLOCK
    ckv = _compress_out(_compress_in(kvc.reshape(B, nc, CMP_STRIDE * D_KV), cmp_tab), cmp_tab, nc)
    kc, vc = _split_heads(ckv, BF16)
    vct = jnp.transpose(vc, (0, 1, 3, 2))
    bias_n = _bias_by_distance(rel_bias, T)
    n_qt, n_kt = T // ATT_TQ, T // ATT_TK
    n_ds = min(n_kt, -(-(REL_MAX_DIST + ATT_TK - 1) // ATT_TK) + 1)
    n_dw = min(n_kt, WINDOW // ATT_TK + 1)
    tzs, tzw, bias_tab = _bias_tables(bias_n, n_qt, nc // 8, n_ds, n_dw, ATT_TQ, ATT_TK)
    pool = jnp.asarray(_pool_matrix(nc, nb))
    scale = HEAD_DIM ** -0.5
    q5 = jnp.transpose((q * scale).reshape(B, T, N_KV_HEADS, GQA, HEAD_DIM), (0, 2, 3, 1, 4))
    o_cmp, sel = _cmp_select_prompt(q5, kc, vct, bias_tab, pool, nc - 1)
    ks, vs = _split_heads(kvs, BF16)
    kw, vw = _split_heads(kvw, BF16)
    o_sel, o_win = _sel_win_prompt(q5, ks, jnp.transpose(vs, (0, 1, 3, 2)), kw, jnp.transpose(vw, (0, 1, 3, 2)),
                                   sel, tzs, tzw)
    return o_cmp, o_sel, o_win


def _nsa_sample(q, kvc, kvs, kvw, pool_cmp, pool_sel, win_buf, page_table, cmp_tab, rel_bias):
    B = q.shape[0]
    n_pages = page_table.shape[1]
    past_len = n_pages * PAGE_SIZE
    q_pos = past_len
    lp = -(-(past_len + 1) // SEL_BLOCK) * SEL_BLOCK
    n_cmp = lp // CMP_STRIDE - 1
    n_blk = lp // SEL_BLOCK
    n_past_chunks = past_len // CMP_STRIDE
    n_tail = 8
    assert n_past_chunks + n_tail >= n_cmp + 1
    n_chunks = n_past_chunks + n_tail
    feature_major = lambda pool: jnp.transpose(pool, (0, 2, 3, 4, 1))
    z_past = _compress_in_paged(feature_major(pool_cmp), page_table, cmp_tab)
    tail = jnp.pad(kvc[:, None, :], ((0, 0), (0, n_tail * CMP_STRIDE - 1), (0, 0)))
    z_tail = _compress_in(tail.reshape(B, n_tail, CMP_STRIDE * D_KV), cmp_tab)
    ncp = -(-n_chunks // LANE) * LANE
    nbp = -(-n_blk // LANE) * LANE
    ckv = _compress_out(jnp.concatenate([z_past, z_tail], axis=1), cmp_tab, ncp)
    bias_n = _bias_by_distance(rel_bias, q_pos + 1)
    n_back = max((n_pages + 1) * PAGE_SIZE, ncp * CMP_STRIDE + CMP_BLOCK)
    back = jnp.concatenate([bias_n[:, ::-1], jnp.broadcast_to(bias_n[:, :1], (N_HEADS, n_back - q_pos - 1))], 1)
    bias_c = back[:, CMP_BLOCK - 1:CMP_BLOCK - 1 + ncp * CMP_STRIDE:CMP_STRIDE]
    pool = jnp.asarray(_pool_matrix(ncp, nbp).T)
    q3 = q.reshape(B, N_HEADS, HEAD_DIM)
    o_cmp, idx = _cmp_select_step(q3, ckv, bias_c, pool, n_cmp, n_blk, q_pos)
    idx = idx[..., 0]
    bpp = PAGE_SIZE // SEL_BLOCK
    n_past = n_pages * bpp
    lpage = idx // bpp
    pages = jnp.take_along_axis(page_table, jnp.minimum(lpage, n_pages - 1).reshape(B, -1), axis=1)
    new_t = jnp.pad(kvs.reshape(B, 2, N_KV_HEADS, HEAD_DIM, 1), ((0, 0),) * 4 + ((0, PAGE_SIZE - 1),))
    bias_page = jnp.transpose(back[:, :(n_pages + 1) * PAGE_SIZE].reshape(N_HEADS, n_pages + 1, PAGE_SIZE),
                              (1, 0, 2))
    bias_sel = jnp.transpose(bias_page[lpage], (0, 1, 3, 2, 4)).reshape(B, N_KV_HEADS, N_HEADS, -1)
    kpos = lpage[..., None] * PAGE_SIZE + jnp.arange(PAGE_SIZE)
    ok = (kpos // SEL_BLOCK == idx[..., None]) & (idx <= q_pos // SEL_BLOCK)[..., None]
    kpos = jnp.where(ok, kpos, q_pos + 1).reshape(B, N_KV_HEADS, 1, -1).astype(jnp.int32)
    o_sel = _sel_step(q3, feature_major(pool_sel), new_t, bias_sel, kpos, pages.reshape(-1).astype(jnp.int32),
                      idx.reshape(-1).astype(jnp.int32), n_past, q_pos)
    o_sel = jnp.concatenate([o_sel[:, h, h * GQA:(h + 1) * GQA] for h in range(N_KV_HEADS)], axis=1)
    wb = win_buf.shape[1]
    bias_w = bias_n[:, 1:wb + 1][:, ::-1]
    o_win = _win_step(q3, win_buf.reshape(B, wb, D_KV), kvw[:, None, :], bias_w, bias_n[:, 0:1])
    return o_cmp.reshape(B, D_ATT), o_sel.reshape(B, D_ATT), o_win.reshape(B, D_ATT)


def kernel(x_prompt, x_sample, cache_cmp_kv, cache_sel_kv, state_win_kv, state_ssm_re, state_ssm_im, page_table,
           c_prompt, c_sample, w_ada, b_ada, w_in, lam_re, lam_im, log_dt, b_re, b_im, c_re, c_im, d_skip,
           w_glu, b_glu, phi_pe, phi_w1, phi_b1, phi_w2, phi_b2, rel_bias, w_out, ln1_g, ln1_b,
           w_router, b_router, w_gate_up, b_gate_up, w_down, b_down, ln2_g, ln2_b):
    assert w_ada.shape[0] == DEPTH == 1
    l = 0
    Bp, T, D = x_prompt.shape
    Bs = x_sample.shape[0]
    kv_tail = (2, N_KV_HEADS, HEAD_DIM)

    n_c = Bp + Bs
    c_all = jnp.pad(jnp.concatenate([c_prompt, c_sample], 0), ((0, -n_c % 8), (0, 0)))
    m_all = _adaln(c_all, w_ada[l], b_ada[l])
    m_p = m_all[:Bp].reshape(Bp, 6, D)
    m_s = m_all[Bp:n_c].reshape(Bs, 6, D)
    mod_p = [m_p[:, i:i + 1, :] for i in range(6)]
    mod_s = [m_s[None, :, i, :] for i in range(6)]

    w_in_pad = jnp.pad(w_in[l], ((0, 0), (0, D_IN_PAD - D_IN))).astype(BF16)
    n_levels = max(1, int(math.log2(T // SSM_CHUNK)))
    ssm_tab = _ssm_tables(lam_re[l], lam_im[l], log_dt[l], b_re[l], b_im[l], c_re[l], c_im[l],
                          SSM_CHUNK, n_levels)
    cmp_tab = _compress_tables(phi_pe[l], phi_w1[l], phi_b1[l], phi_w2[l], phi_b2[l])
    w_post = dict(
        d_skip=d_skip[l].reshape(1, D_SSM), w_glu=w_glu[l].astype(BF16), b_glu=b_glu[l].reshape(1, D_SSM),
        gexp=jnp.asarray(_gate_expand_matrix(), dtype=BF16), w_out=w_out[l].astype(BF16),
        ln1_g=ln1_g[l].reshape(1, D), ln1_b=ln1_b[l].reshape(1, D),
        w_router=jnp.stack(_split_bf16(jnp.pad(w_router[l], ((0, 0), (0, LANE - N_EXPERTS))))),
        b_router=jnp.pad(b_router[l], (0, LANE - N_EXPERTS)).reshape(1, LANE))

    u, q, kvc, kvs, kvw, g = _mixer_in(x_prompt, mod_p[0], mod_p[1], w_in_pad, tm=512)
    y_ssm, h_p = _ssm_prompt(u, ssm_tab)
    o_cmp, o_sel, o_win = _nsa_prompt(q, kvc, kvs, kvw, cmp_tab, rel_bias)
    x1_p, hm_p, te_p, tw_p = _post_mixer(y_ssm, u, o_cmp, o_sel, o_win, g, x_prompt,
                                         mod_p[2], mod_p[3], mod_p[4], w_post, tm=256)

    u_s, q_s, kvc_s, kvs_s, kvw_s, g_s = _mixer_in(x_sample.reshape(1, Bs, D), mod_s[0], mod_s[1],
                                                   w_in_pad, tm=Bs)
    y_s, h_s = _ssm_sample(u_s[0], state_ssm_re[l], state_ssm_im[l], ssm_tab, c_re[l], c_im[l])
    oc_s, os_s, ow_s = _nsa_sample(q_s[0].astype(F32), kvc_s[0], kvs_s[0], kvw_s[0], cache_cmp_kv[l],
                                   cache_sel_kv[l], state_win_kv[l], page_table, cmp_tab, rel_bias)
    x1_s, hm_s, te_s, tw_s = _post_mixer(y_s[None], u_s, oc_s[None], os_s[None], ow_s[None], g_s,
                                         x_sample.reshape(1, Bs, D), mod_s[2], mod_s[3], mod_s[4],
                                         w_post, tm=Bs)

    n_p = Bp * T
    n_all = n_p + Bs
    hm_all = jnp.concatenate([hm_p.reshape(n_p, D), hm_s.reshape(Bs, D)], 0)
    te_all = jnp.concatenate([te_p.reshape(n_p, LANE), te_s.reshape(Bs, LANE)], 0)[:, :TOP_K]
    row_tok, dest, items = _moe_dispatch(te_all, n_all)
    xb = jnp.concatenate([hm_all, jnp.zeros((1, D), F32)], 0)[row_tok]
    yb = _experts(xb, items, w_gate_up[l], b_gate_up[l], w_down[l], b_down[l])
    ys_p = [yb[dest[:n_p, k]].reshape(Bp, T, D) for k in range(TOP_K)]
    ys_s = [yb[dest[n_p:, k]].reshape(1, Bs, D) for k in range(TOP_K)]
    ln2g, ln2b = ln2_g[l].reshape(1, D), ln2_b[l].reshape(1, D)
    out_p = _final(x1_p, ys_p, tw_p, mod_p[5], ln2g, ln2b, tm=512)
    out_s = _final(x1_s, ys_s, tw_s, mod_s[5], ln2g, ln2b, tm=Bs)

    wlen = min(WINDOW, T)
    win_s = jnp.concatenate([state_win_kv[l], kvw_s[0].reshape(Bs, 1, *kv_tail)], 1)[:, -state_win_kv.shape[2]:]
    p_state = SSM_STATE
    return (out_p, out_s.reshape(Bs, 1, D),
            kvc.reshape(1, Bp, T, *kv_tail), kvc_s[0].reshape(1, Bs, 1, *kv_tail),
            kvs.reshape(1, Bp, T, *kv_tail), kvs_s[0].reshape(1, Bs, 1, *kv_tail),
            kvw[:, T - wlen:].reshape(1, Bp, wlen, *kv_tail), win_s[None],
            h_p[None, ..., :p_state], h_p[None, ..., p_state:],
            h_s[None, ..., :p_state], h_s[None, ..., p_state:])
```

```python
import functools
import math

import numpy as np
import jax
import jax.numpy as jnp
from jax import lax
from jax.experimental import pallas as pl
from jax.experimental.pallas import tpu as pltpu

D_MODEL = 1024
DEPTH = 1
PAST_LEN = 16384
PAGE_SIZE = 128
D_SSM = 512
SSM_GROUP = 16
N_SSM_GROUPS = D_SSM // SSM_GROUP
SSM_STATE = 64
N_HEADS = 8
HEAD_DIM = 64
N_KV_HEADS = 2
GQA = N_HEADS // N_KV_HEADS
D_ATT = N_HEADS * HEAD_DIM
D_KV = 2 * N_KV_HEADS * HEAD_DIM
CMP_STRIDE = 16
CMP_BLOCK = 2 * CMP_STRIDE
SEL_BLOCK = 64
N_SEL = 16
WINDOW = 512
NUM_BUCKETS = 32
REL_MAX_DIST = 1024
N_EXPERTS = 32
TOP_K = 4
D_FF = 1024
SWIGLU_LIMIT = 7.0
SWIGLU_ALPHA = 1.702
DN_ALPHA = (2 * DEPTH) ** 0.25
D_IN = D_SSM + D_ATT + 3 * D_KV + 3 * N_HEADS
NEG = -1e30
F32 = jnp.float32
BF16 = jnp.bfloat16
HIGHEST = lax.Precision.HIGHEST

LANE = 128
D_IN_PAD = 1920
GATE_COL = D_SSM + D_ATT + 3 * D_KV
SSM_CHUNK = 16
ATT_TQ = 128
ATT_TK = 128
SEL_CHAINS = 4
MOE_ROWS = 256
PAGES_PER_STEP = 32
CHUNK_PITCH = 24
VMEM_LIMIT = 48 * 1024 * 1024
LN_EPS = 1e-5


def _cparams(*sem):
    return pltpu.CompilerParams(dimension_semantics=sem, vmem_limit_bytes=VMEM_LIMIT)


def _nt_dot(a, b):
    return lax.dot_general(a, b, (((1,), (1,)), ((), ())), preferred_element_type=F32)


def _layer_norm(x):
    mu = jnp.mean(x, axis=-1, keepdims=True)
    xc = x - mu
    var = jnp.mean(xc * xc, axis=-1, keepdims=True)
    return xc * lax.rsqrt(var + LN_EPS)


def _adaln_kernel(c_ref, w_ref, b_ref, o_ref):
    c = c_ref[...]
    s = c * jax.nn.sigmoid(c)
    o_ref[...] = jnp.dot(s, w_ref[...], precision=HIGHEST, preferred_element_type=F32) + b_ref[...]


def _adaln(c, w, b):
    n, d = c.shape
    dout = w.shape[1]
    tn = 1024
    return pl.pallas_call(
        _adaln_kernel,
        out_shape=jax.ShapeDtypeStruct((n, dout), F32),
        grid=(dout // tn,),
        in_specs=[pl.BlockSpec((n, d), lambda j: (0, 0)),
                  pl.BlockSpec((d, tn), lambda j: (0, j)),
                  pl.BlockSpec((1, tn), lambda j: (0, j))],
        out_specs=pl.BlockSpec((n, tn), lambda j: (0, j)),
        compiler_params=_cparams("arbitrary"),
        name="adaln",
    )(c, w, b.reshape(1, dout))


def _mixer_in_kernel(x_ref, sh_ref, sc_ref, w_ref, u_ref, q_ref, kvc_ref, kvs_ref, kvw_ref, g_ref, *att_refs):
    h = _layer_norm(x_ref[0]) * (1.0 + sc_ref[0]) + sh_ref[0]
    z = jnp.dot(h.astype(BF16), w_ref[...], preferred_element_type=F32)
    c0 = D_SSM
    c1 = c0 + D_ATT
    c2 = c1 + D_KV
    c3 = c2 + D_KV
    c4 = c3 + D_KV
    u_ref[0] = z[:, :c0]
    kvc_ref[0] = z[:, c1:c2]
    kvs_ref[0] = z[:, c2:c3]
    kvw_ref[0] = z[:, c3:c4]
    g_ref[0] = z[:, c4:c4 + LANE]
    if not att_refs:
        q_ref[0] = z[:, c0:c1].astype(BF16)
        return
    ks_ref, vst_ref, kw_ref, vwt_ref = att_refs
    hd, half = HEAD_DIM, N_KV_HEADS * HEAD_DIM
    for hq in range(N_HEADS):
        q_ref[0, hq // GQA, hq % GQA] = (z[:, c0 + hq * hd:c0 + (hq + 1) * hd] * (hd ** -0.5)).astype(BF16)
    for k_ref, vt_ref, base in ((ks_ref, vst_ref, c2), (kw_ref, vwt_ref, c3)):
        for hk in range(N_KV_HEADS):
            k_ref[0, hk] = z[:, base + hk * hd:base + (hk + 1) * hd].astype(BF16)
        vt = z[:, base + half:base + 2 * half].T
        vt_ref[0] = vt.reshape(N_KV_HEADS, hd, vt.shape[1]).astype(BF16)


def _mixer_in(x, shift, scale, w_pad, tm, attention_layouts):
    B, T, D = x.shape
    R = shift.shape[1]
    rb = 1 if R == 1 else tm
    mod_map = (lambda b, i: (b, 0, 0)) if R == 1 else (lambda b, i: (b, i, 0))
    row = lambda n: pl.BlockSpec((1, tm, n), lambda b, i: (b, i, 0))
    f32 = lambda n: jax.ShapeDtypeStruct((B, T, n), F32)
    if attention_layouts:
        q_shape = jax.ShapeDtypeStruct((B, N_KV_HEADS, GQA, T, HEAD_DIM), BF16)
        q_spec = pl.BlockSpec((1, N_KV_HEADS, GQA, tm, HEAD_DIM), lambda b, i: (b, 0, 0, i, 0))
        k_shape = jax.ShapeDtypeStruct((B, N_KV_HEADS, T, HEAD_DIM), BF16)
        k_spec = pl.BlockSpec((1, N_KV_HEADS, tm, HEAD_DIM), lambda b, i: (b, 0, i, 0))
        vt_shape = jax.ShapeDtypeStruct((B, N_KV_HEADS, HEAD_DIM, T), BF16)
        vt_spec = pl.BlockSpec((1, N_KV_HEADS, HEAD_DIM, tm), lambda b, i: (b, 0, 0, i))
        extra_shapes, extra_specs = (k_shape, vt_shape, k_shape, vt_shape), (k_spec, vt_spec, k_spec, vt_spec)
    else:
        q_shape, q_spec = jax.ShapeDtypeStruct((B, T, D_ATT), BF16), row(D_ATT)
        extra_shapes, extra_specs = (), ()
    return pl.pallas_call(
        _mixer_in_kernel,
        out_shape=(f32(D_SSM), q_shape, f32(D_KV), f32(D_KV), f32(D_KV), f32(LANE)) + extra_shapes,
        grid=(B, T // tm),
        in_specs=[row(D), pl.BlockSpec((1, rb, D), mod_map), pl.BlockSpec((1, rb, D), mod_map),
                  pl.BlockSpec((D, D_IN_PAD), lambda b, i: (0, 0))],
        out_specs=(row(D_SSM), q_spec, row(D_KV), row(D_KV), row(D_KV), row(LANE)) + extra_specs,
        compiler_params=_cparams("parallel", "parallel"),
        name="mixer_in",
    )(x, shift, scale, w_pad)


def _ssm_tables(lam_re, lam_im, log_dt, b_re, b_im, c_re, c_im, L, n_levels):
    G, P = lam_re.shape
    C = b_re.shape[-1]
    dt = jnp.exp(log_dt.astype(F32))[:, None]
    er, ei = lam_re * dt, lam_im * dt

    def power(k):
        kk = k.astype(F32)[:, None, None]
        mag = jnp.exp(kk * er)
        return mag * jnp.cos(kk * ei), mag * jnp.sin(kk * ei)

    lb_re, lb_im = power(jnp.ones((1,), F32))
    nr, ni = lb_re[0] - 1.0, lb_im[0]
    den = lam_re * lam_re + lam_im * lam_im
    fr = (nr * lam_re + ni * lam_im) / den
    fi = (ni * lam_re - nr * lam_im) / den
    bbr = fr[:, :, None] * b_re - fi[:, :, None] * b_im
    bbi = fr[:, :, None] * b_im + fi[:, :, None] * b_re
    pr, pi = power(jnp.arange(L + 1))
    clr = c_re[None] * pr[:, :, None, :] - c_im[None] * pi[:, :, None, :]
    cli = c_re[None] * pi[:, :, None, :] + c_im[None] * pr[:, :, None, :]
    kern = (jnp.einsum('kgcp,gpd->kgcd', clr[:L], bbr, precision=HIGHEST)
            - jnp.einsum('kgcp,gpd->kgcd', cli[:L], bbi, precision=HIGHEST))
    kz = jnp.concatenate([kern, jnp.zeros((1,) + kern.shape[1:], F32)], 0)
    ts = np.arange(L)
    lag = ts[None, :] - ts[:, None]
    lag = np.where(lag >= 0, lag, L)
    toep = kz[lag]
    toep = jnp.transpose(toep, (2, 0, 4, 1, 3)).reshape(G, L * C, L * C)
    rev = L - 1 - ts
    wsr = pr[rev][:, :, :, None] * bbr[None] - pi[rev][:, :, :, None] * bbi[None]
    wsi = pr[rev][:, :, :, None] * bbi[None] + pi[rev][:, :, :, None] * bbr[None]
    ws = jnp.concatenate([jnp.transpose(wsr, (1, 0, 3, 2)), jnp.transpose(wsi, (1, 0, 3, 2))], -1)
    ws = ws.reshape(G, L * C, 2 * P)
    wy = jnp.concatenate([jnp.transpose(clr[1:], (1, 3, 0, 2)), -jnp.transpose(cli[1:], (1, 3, 0, 2))], 1)
    wy = wy.reshape(G, 2 * P, L * C)
    lr, li = power(L * (2 ** jnp.arange(n_levels)))
    ar = jnp.transpose(jnp.concatenate([lr, lr], -1), (1, 0, 2))
    ai = jnp.transpose(jnp.concatenate([-li, li], -1), (1, 0, 2))
    return toep.astype(BF16), ws.astype(BF16), wy.astype(BF16), ar, ai, (lb_re[0], lb_im[0], bbr, bbi)


def _ssm_kernel(u_ref, toep_ref, ws_ref, wy_ref, ar_ref, ai_ref, y_ref, hl_ref, *, nb, nc, n_levels):
    u = u_ref[0]
    y1 = jnp.dot(u, toep_ref[0], preferred_element_type=F32)
    s = jnp.dot(u, ws_ref[0], preferred_element_type=F32)
    p2 = s.shape[-1]
    rows = lax.broadcasted_iota(jnp.int32, (nc, p2), 0)
    prev = []
    for b in range(nb):
        h = s[b * nc:(b + 1) * nc]
        for k in range(n_levels):
            d = 1 << k
            sh = jnp.where(rows >= d, pltpu.roll(h, d, axis=0), 0.0)
            sw = pltpu.roll(sh, p2 // 2, axis=1)
            h = h + ar_ref[0, k:k + 1, :] * sh + ai_ref[0, k:k + 1, :] * sw
        hl_ref[0, b:b + 1, :] = h[nc - 1:nc, :]
        prev.append(jnp.where(rows >= 1, pltpu.roll(h, 1, axis=0), 0.0))
    hp = jnp.concatenate(prev, axis=0)
    y2 = jnp.dot(hp.astype(BF16), wy_ref[0], preferred_element_type=F32)
    y_ref[0] = y1 + y2


def _ssm_prompt(u, tables):
    toep, ws, wy, ar, ai, _ = tables
    B, T, _ = u.shape
    G, C, L = N_SSM_GROUPS, SSM_GROUP, SSM_CHUNK
    nc = T // L
    n_levels = ar.shape[1]
    ug = jnp.transpose(u.reshape(B, nc, L, G, C), (3, 0, 1, 2, 4)).reshape(G, B * nc, L * C).astype(BF16)
    grp = lambda r, c: pl.BlockSpec((1, r, c), lambda g: (g, 0, 0))
    y, hl = pl.pallas_call(
        functools.partial(_ssm_kernel, nb=B, nc=nc, n_levels=n_levels),
        out_shape=(jax.ShapeDtypeStruct((G, B * nc, L * C), F32),
                   jax.ShapeDtypeStruct((G, B, 2 * SSM_STATE), F32)),
        grid=(G,),
        in_specs=[grp(B * nc, L * C), grp(L * C, L * C), grp(L * C, 2 * SSM_STATE),
                  grp(2 * SSM_STATE, L * C), grp(n_levels, 2 * SSM_STATE), grp(n_levels, 2 * SSM_STATE)],
        out_specs=(grp(B * nc, L * C), grp(B, 2 * SSM_STATE)),
        compiler_params=_cparams("parallel"),
        name="ssm_prompt",
    )(ug, toep, ws, wy, ar, ai)
    y = jnp.transpose(y.reshape(G, B, nc, L, C), (1, 2, 3, 0, 4)).reshape(B, T, D_SSM)
    return y, jnp.transpose(hl, (1, 0, 2))


def _ssm_step_kernel(u_ref, h0_ref, bb_ref, lr_ref, li_ref, cy_ref, y_ref, h_ref):
    p = lr_ref.shape[-1] // 2
    bu = jnp.einsum('gbc,gcp->gbp', u_ref[...], bb_ref[...], preferred_element_type=F32)
    h0 = h0_ref[...]
    h0s = jnp.concatenate([h0[..., p:], h0[..., :p]], axis=-1)
    h = lr_ref[...] * h0 + li_ref[...] * h0s + bu
    h_ref[...] = h
    y_ref[...] = jnp.einsum('gbp,gpc->gbc', h.astype(BF16), cy_ref[...], preferred_element_type=F32)


def _ssm_sample(u, h0_re, h0_im, tables, c_re, c_im):
    lb_re, lb_im, bbr, bbi = tables[-1]
    B = u.shape[0]
    G, C, P = N_SSM_GROUPS, SSM_GROUP, SSM_STATE
    ug = jnp.transpose(u.reshape(B, G, C), (1, 0, 2)).astype(BF16)
    h0 = jnp.transpose(jnp.concatenate([h0_re, h0_im], -1), (1, 0, 2)).astype(F32)
    bb = jnp.concatenate([jnp.transpose(bbr, (0, 2, 1)), jnp.transpose(bbi, (0, 2, 1))], -1).astype(BF16)
    lr = jnp.concatenate([lb_re, lb_re], -1)[:, None, :]
    li = jnp.concatenate([-lb_im, lb_im], -1)[:, None, :]
    cy = jnp.concatenate([jnp.transpose(c_re, (0, 2, 1)), -jnp.transpose(c_im, (0, 2, 1))], 1).astype(BF16)
    y, h = pl.pallas_call(
        _ssm_step_kernel,
        out_shape=(jax.ShapeDtypeStruct((G, B, C), F32), jax.ShapeDtypeStruct((G, B, 2 * P), F32)),
        name="ssm_step",
    )(ug, h0, bb, lr, li, cy)
    return jnp.transpose(y, (1, 0, 2)).reshape(B, D_SSM), jnp.transpose(h, (1, 0, 2))


def _compress_tables(phi_pe, phi_w1, phi_b1, phi_w2, phi_b2):
    S, H, Dh = CMP_STRIDE, N_KV_HEADS, HEAD_DIM
    w1 = phi_w1.reshape(2, 2, S, Dh, Dh)
    eye_c = jnp.eye(2, dtype=F32)
    eye_h = jnp.eye(H, dtype=F32)
    wbig = jnp.einsum('cajde,xc,yh->jxydache', w1, eye_c, eye_h).reshape(S * 2 * H * Dh, 2 * 2 * H * Dh)
    pe = jnp.transpose(phi_pe.reshape(2, 2, S, Dh), (1, 2, 0, 3))
    pe_rows = jnp.broadcast_to(pe[:, :, :, None, :], (2, S, 2, H, Dh)).reshape(2, S * 2 * H * Dh)
    n = 2 * H * Dh
    pe_w = (jnp.dot(pe_rows[0], wbig[:, :n], precision=HIGHEST) + jnp.dot(pe_rows[1], wbig[:, n:], precision=HIGHEST))
    b1 = jnp.broadcast_to(phi_b1[:, None, :], (2, H, Dh)).reshape(1, n) + pe_w[None, :]
    w2 = jnp.einsum('cef,cx,hy->chexyf', phi_w2, eye_c, eye_h).reshape(n, n)
    b2 = jnp.broadcast_to(phi_b2[:, None, :], (2, H, Dh)).reshape(1, n)
    return wbig.astype(BF16), b1, w2.astype(BF16), b2


def _compress_in_kernel(x_ref, w_ref, z_ref):
    z_ref[0] = jnp.dot(x_ref[0].astype(BF16), w_ref[...], preferred_element_type=F32)


def _compress_in(x2, tables):
    wbig = tables[0]
    N2 = wbig.shape[1]
    B, n, K = x2.shape
    tr = math.gcd(n, 256)
    return pl.pallas_call(
        _compress_in_kernel,
        out_shape=jax.ShapeDtypeStruct((B, n, N2), F32),
        grid=(B, n // tr),
        in_specs=[pl.BlockSpec((1, tr, K), lambda b, i: (b, i, 0)),
                  pl.BlockSpec((K, N2), lambda b, i: (0, 0))],
        out_specs=pl.BlockSpec((1, tr, N2), lambda b, i: (b, i, 0)),
        compiler_params=_cparams("parallel", "parallel"),
        name="compress_in",
    )(x2, wbig)


def _compress_in_paged_kernel(pt_ref, *refs, n_pg):
    x_refs = refs[:n_pg]
    w_ref, z_ref = refs[n_pg:n_pg + 2]
    s_refs = refs[n_pg + 2:]
    for k in range(n_pg):
        t = x_refs[k][0].reshape(D_KV, PAGE_SIZE).T
        for c, s_ref in enumerate(s_refs):
            for n in range(PAGE_SIZE // CMP_STRIDE):
                r0 = (k * (PAGE_SIZE // CMP_STRIDE) + n) * CHUNK_PITCH
                s_ref[r0:r0 + CMP_STRIDE, :] = t[n * CMP_STRIDE:(n + 1) * CMP_STRIDE, c * LANE:(c + 1) * LANE]
    rows = n_pg * PAGE_SIZE // CMP_STRIDE
    z = jnp.zeros((rows, w_ref.shape[1]), F32)
    for j in range(CMP_STRIDE):
        xj = jnp.concatenate([s_ref[pl.ds(j, rows, stride=CHUNK_PITCH), :] for s_ref in s_refs], axis=1)
        z = z + jnp.dot(xj.astype(BF16), w_ref[j * D_KV:(j + 1) * D_KV, :], preferred_element_type=F32)
    z_ref[0] = z


def _compress_in_paged(pool_t, page_table, tables):
    wbig = tables[0]
    N2 = wbig.shape[1]
    K = wbig.shape[0]
    B, n_pages = page_table.shape
    n_pg = math.gcd(n_pages, PAGES_PER_STEP)
    rows = n_pg * PAGE_SIZE // CMP_STRIDE
    page_spec = lambda k: pl.BlockSpec((1,) + pool_t.shape[1:],
                                       lambda b, i, pt, k=k: (pt[b, i * n_pg + k], 0, 0, 0, 0))
    grid_spec = pltpu.PrefetchScalarGridSpec(
        num_scalar_prefetch=1,
        grid=(B, n_pages // n_pg),
        in_specs=[page_spec(k) for k in range(n_pg)] + [pl.BlockSpec((K, N2), lambda b, i, pt: (0, 0))],
        out_specs=pl.BlockSpec((1, rows, N2), lambda b, i, pt: (b, i, 0)),
        scratch_shapes=[pltpu.VMEM((rows * CHUNK_PITCH, LANE), F32) for _ in range(D_KV // LANE)],
    )
    return pl.pallas_call(
        functools.partial(_compress_in_paged_kernel, n_pg=n_pg),
        out_shape=jax.ShapeDtypeStruct((B, n_pages * PAGE_SIZE // CMP_STRIDE, N2), F32),
        grid_spec=grid_spec,
        compiler_params=_cparams("arbitrary", "arbitrary"),
        name="compress_in_paged",
    )(page_table, *([pool_t] * n_pg), wbig)


def _compress_out_kernel(z_ref, b1_ref, w2_ref, b2_ref, o_ref):
    z = z_ref[0]
    n = z.shape[-1] // 2
    rows = z.shape[0]
    second = pltpu.roll(z[:, n:], rows - 1, axis=0)
    hdn = jax.nn.gelu(z[:, :n] + second + b1_ref[...])
    o_ref[0, :rows, :] = jnp.dot(hdn.astype(BF16), w2_ref[...], preferred_element_type=F32) + b2_ref[...]
    if o_ref.shape[1] > rows:
        o_ref[0, rows:, :] = jnp.zeros((o_ref.shape[1] - rows, n), F32)


def _compress_out(z, tables, n_out):
    _, b1, w2, b2 = tables
    B, n, N2 = z.shape
    return pl.pallas_call(
        _compress_out_kernel,
        out_shape=jax.ShapeDtypeStruct((B, n_out, N2 // 2), F32),
        grid=(B,),
        in_specs=[pl.BlockSpec((1, n, N2), lambda b: (b, 0, 0)),
                  pl.BlockSpec((1, N2 // 2), lambda b: (0, 0)),
                  pl.BlockSpec((N2 // 2, N2 // 2), lambda b: (0, 0)),
                  pl.BlockSpec((1, N2 // 2), lambda b: (0, 0))],
        out_specs=pl.BlockSpec((1, n_out, N2 // 2), lambda b: (b, 0, 0)),
        compiler_params=_cparams("parallel"),
        name="compress_out",
    )(z, b1, w2, b2)


def _rel_bucket(dist):
    n = jnp.maximum(dist, 0)
    max_exact = NUM_BUCKETS // 2
    nf = jnp.maximum(n, 1).astype(F32)
    large = max_exact + (jnp.log(nf / max_exact) / math.log(REL_MAX_DIST / max_exact)
                         * (NUM_BUCKETS - max_exact)).astype(jnp.int32)
    large = jnp.minimum(large, NUM_BUCKETS - 1)
    return jnp.where(n < max_exact, n, large)


def _bias_by_distance(rel_bias, n_max):
    onehot = (_rel_bucket(jnp.arange(n_max))[None, :] == jnp.arange(NUM_BUCKETS)[:, None]).astype(F32)
    return jnp.dot(jnp.transpose(rel_bias.astype(F32)), onehot, precision=HIGHEST)


def _shifted_chunks(bias_n, pad, n_chunks, width):
    n = min(bias_n.shape[1], n_chunks * width - pad)
    ext = jnp.concatenate([jnp.broadcast_to(bias_n[:, :1], (N_HEADS, pad)), bias_n[:, :n],
                           jnp.zeros((N_HEADS, n_chunks * width - pad - n), F32)], axis=1)
    return ext.reshape(N_HEADS, n_chunks, width)


def _bias_tables_kernel(ed_ref, ec_ref, tzs_ref, tzw_ref, cmp_ref, *, tq, tk, n_qt):
    n_ds, n_dw, n_j = tzs_ref.shape[1] - 1, tzw_ref.shape[1] - 1, cmp_ref.shape[1] // 8
    tzs_ref[0, n_ds] = jnp.full((tk, tq), NEG, F32)
    tzw_ref[0, n_dw] = jnp.full((tk, tq), NEG, F32)
    w = tq + tk
    c = lax.broadcasted_iota(jnp.int32, (tk, tq), 0)
    r = lax.broadcasted_iota(jnp.int32, (tk, tq), 1)
    for d in range(n_ds):
        v = jnp.concatenate([ed_ref[0, d:d + 1, :], ed_ref[0, d + 1:d + 2, :]], axis=1)
        t = pltpu.roll(jnp.broadcast_to(v, (tk, w)), w - (tk - 1), axis=1, stride=1, stride_axis=0)[:, :tq]
        dist = d * tk + r - c
        tzs_ref[0, d] = jnp.where(dist >= 0, t, NEG)
        if d < n_dw:
            tzw_ref[0, d] = jnp.where((dist >= 0) & (dist <= WINDOW), t, NEG)
    for j in range(n_j):
        dd = n_qt - 1 - j
        c0, c1 = max(dd, 0), max(dd + 1, 0)
        v = jnp.concatenate([ec_ref[0, c0:c0 + 1, :], ec_ref[0, c1:c1 + 1, :]], axis=1)
        t = pltpu.roll(jnp.broadcast_to(v, (8, w)), w - 7 * CMP_STRIDE, axis=1, stride=CMP_STRIDE, stride_axis=0)
        cmp_ref[0, j * 8:(j + 1) * 8, :] = t[:, :tq]


def _bias_tables(bias_n, n_qt, n_rb, n_ds, n_dw, tq, tk):
    assert tq == tk == 8 * CMP_STRIDE and n_dw <= n_ds
    n_j = n_rb + n_qt - 1
    ed = _shifted_chunks(bias_n, tk - 1, n_ds + 1, tq)
    ec = _shifted_chunks(bias_n, 7 * CMP_STRIDE + CMP_BLOCK - 1, n_qt + 1, tq)
    head = lambda a: pl.BlockSpec((1,) + a.shape[1:], lambda h: (h,) + (0,) * (a.ndim - 1))
    outs = (jax.ShapeDtypeStruct((N_HEADS, n_ds + 1, tk, tq), F32),
            jax.ShapeDtypeStruct((N_HEADS, n_dw + 1, tk, tq), F32),
            jax.ShapeDtypeStruct((N_HEADS, n_j * 8, tq), F32))
    tzs, tzw, cmp = pl.pallas_call(
        functools.partial(_bias_tables_kernel, tq=tq, tk=tk, n_qt=n_qt),
        out_shape=outs,
        grid=(N_HEADS,),
        in_specs=[head(ed), head(ec)],
        out_specs=tuple(head(o) for o in outs),
        compiler_params=_cparams("parallel"),
        name="bias_tables",
    )(ed, ec)
    grp = lambda a: a.reshape((N_KV_HEADS, GQA) + a.shape[1:])
    return grp(tzs), grp(tzw), cmp


def _pool_matrix(n_cmp_pad, n_blk_pad):
    r = SEL_BLOCK // CMP_STRIDE
    i = np.arange(n_cmp_pad)[None, :]
    j = np.arange(n_blk_pad)[:, None]
    return ((i >= r * j - 1) & (i <= r * j + r - 1)).astype(np.float32)


def _cmp_select_kernel(q_ref, k_ref, vt_ref, bias_ref, pool_ref, o_ref, sel_ref, *, tq, n_cmp):
    qt = pl.program_id(2)
    n_qt = pl.num_programs(2)
    q = q_ref[0, 0].reshape(GQA * tq, HEAD_DIM)
    k = k_ref[0, 0]
    nc = k.shape[0]
    s = _nt_dot(k, q)
    row0 = pl.multiple_of((n_qt - 1 - qt) * 8, 8)
    s = s + jnp.concatenate([bias_ref[g, pl.ds(row0, nc), :] for g in range(GQA)], axis=-1)
    t_pos = qt * tq + (lax.broadcasted_iota(jnp.int32, (nc, GQA * tq), 1) % tq)
    ci = lax.broadcasted_iota(jnp.int32, (nc, GQA * tq), 0)
    mask = (ci * CMP_STRIDE + CMP_BLOCK - 1 <= t_pos) & (ci < n_cmp)
    s = jnp.where(mask, s, NEG)
    m = jnp.max(s, axis=0, keepdims=True)
    p = jnp.where(mask, jnp.exp(s - m), 0.0)
    p = p / jnp.maximum(jnp.sum(p, axis=0, keepdims=True), 1e-30)
    ot = jnp.dot(vt_ref[0, 0], p.astype(BF16), preferred_element_type=F32)
    o_ref[0] = jnp.concatenate([ot[:, g * tq:(g + 1) * tq].T for g in range(GQA)], axis=-1)
    imp = p[:, 0:tq]
    for g in range(1, GQA):
        imp = imp + p[:, g * tq:(g + 1) * tq]
    sb = jnp.dot(pool_ref[...], imp, precision=HIGHEST, preferred_element_type=F32)
    nb = sb.shape[0]
    blk = lax.broadcasted_iota(jnp.int32, (nb, tq), 0)
    cur = (qt * tq + lax.broadcasted_iota(jnp.int32, (nb, tq), 1)) // SEL_BLOCK
    causal = blk <= cur
    forced = (blk == 0) | (blk == cur) | (blk == cur - 1)
    sc = jnp.where(forced & causal, 1e4, jnp.where(causal, sb, -1.0))
    rank = jnp.zeros((nb, tq), jnp.int32)
    for i in range(nb):
        row = sc[i:i + 1, :]
        ahead = (row > sc) | ((row == sc) & (blk > i))
        rank = rank + ahead.astype(jnp.int32)
    sel_ref[0, 0] = jnp.where((rank < N_SEL) & causal, 0.0, NEG)


def _cmp_select_prompt(q5, kc, vct, bias_tab, pool, n_cmp):
    B, _, _, T, _ = q5.shape
    NC = kc.shape[2]
    NB = pool.shape[0]
    R = bias_tab.shape[1]
    tq = ATT_TQ
    return pl.pallas_call(
        functools.partial(_cmp_select_kernel, tq=tq, n_cmp=n_cmp),
        out_shape=(jax.ShapeDtypeStruct((B, T, D_ATT), F32),
                   jax.ShapeDtypeStruct((B, N_KV_HEADS, NB, T), F32)),
        grid=(B, N_KV_HEADS, T // tq),
        in_specs=[pl.BlockSpec((1, 1, GQA, tq, HEAD_DIM), lambda b, h, i: (b, h, 0, i, 0)),
                  pl.BlockSpec((1, 1, NC, HEAD_DIM), lambda b, h, i: (b, h, 0, 0)),
                  pl.BlockSpec((1, 1, HEAD_DIM, NC), lambda b, h, i: (b, h, 0, 0)),
                  pl.BlockSpec((GQA, R, tq), lambda b, h, i: (h, 0, 0)),
                  pl.BlockSpec((NB, NC), lambda b, h, i: (0, 0))],
        out_specs=(pl.BlockSpec((1, tq, GQA * HEAD_DIM), lambda b, h, i: (b, i, h)),
                   pl.BlockSpec((1, 1, NB, tq), lambda b, h, i: (b, h, 0, i))),
        compiler_params=_cparams("parallel", "parallel", "parallel"),
        name="cmp_select_prompt",
    )(q5, kc, vct, bias_tab, pool)


def _sel_win_kernel(q_ref, ks_ref, vst_ref, kw_ref, vwt_ref, sel_ref, tzs_ref, tzw_ref, os_ref, ow_ref, *, tq):
    tk = ATT_TK
    qt = pl.program_id(2)
    q = q_ref[0, 0].reshape(GQA * tq, HEAD_DIM)
    width = GQA * tq
    per_tile = tk // SEL_BLOCK

    def make_sweep(k_ref, vt_ref, tz_ref, use_sel, n_chains, single_trip):
        n_d = tz_ref.shape[2] - 1

        def scores(kt, hi):
            pad = kt > hi
            kt = jnp.minimum(kt, hi)
            off = pl.multiple_of(kt * tk, tk)
            k = k_ref[0, 0, pl.ds(off, tk), :]
            d = jnp.where(pad, n_d, jnp.minimum(qt - kt, n_d - 1))
            bias = [tz_ref[0, g, d] for g in range(GQA)]
            if use_sel:
                rows = sel_ref[0, 0, pl.ds(kt * per_tile, per_tile), :]
                selb = jnp.concatenate([jnp.broadcast_to(rows[i:i + 1], (SEL_BLOCK, tq))
                                        for i in range(per_tile)], axis=0)
                bias = [b + selb for b in bias]
            return _nt_dot(k, q) + jnp.concatenate(bias, axis=1)

        def values_t(kt, lo, hi):
            off = pl.multiple_of(jnp.clip(kt, lo, hi) * tk, tk)
            return vt_ref[0, 0, :, pl.ds(off, tk)]

        def sweep(lo, hi):
            n_trips = (hi - lo + n_chains) // n_chains
            chain0 = (jnp.full((1, width), 0.5 * NEG, F32), jnp.zeros((1, width), F32),
                      jnp.zeros((HEAD_DIM, width), F32), jnp.ones((1, width), F32), jnp.zeros((tk, width), BF16))

            def trip(i, chains):
                kt = lo + n_chains * i
                pv = [jnp.dot(values_t(kt - n_chains + c, lo, hi), chains[c][4], preferred_element_type=F32)
                      for c in range(n_chains)]
                ss = [scores(kt + c, hi) for c in range(n_chains)]
                out = []
                for c in range(n_chains):
                    m, l, acc, alpha_prev, _ = chains[c]
                    m_new = jnp.maximum(m, jnp.max(ss[c], axis=0, keepdims=True))
                    alpha = jnp.exp(m - m_new)
                    p = jnp.exp(ss[c] - m_new)
                    l = alpha * l + jnp.sum(p, axis=0, keepdims=True)
                    out.append((m_new, l, alpha_prev * acc + pv[c], alpha, p.astype(BF16)))
                return tuple(out)

            if single_trip:
                done = []
                for c in range(n_chains):
                    s = scores(lo + c, hi)
                    m = jnp.maximum(jnp.max(s, axis=0, keepdims=True), 0.5 * NEG)
                    p = jnp.exp(s - m)
                    done.append((m, jnp.sum(p, axis=0, keepdims=True),
                                 jnp.dot(values_t(lo + c, lo, hi), p.astype(BF16), preferred_element_type=F32)))
            else:
                chains = lax.fori_loop(0, n_trips, trip, (chain0,) * n_chains)
                kt_last = lo + n_chains * (n_trips - 1)
                done = []
                for c in range(n_chains):
                    m, l, acc, alpha, p = chains[c]
                    done.append((m, l, alpha * acc + jnp.dot(values_t(kt_last + c, lo, hi), p,
                                                              preferred_element_type=F32)))
            m_all = functools.reduce(jnp.maximum, [m for m, _, _ in done])
            num = den = 0.0
            for m, l, acc in done:
                e = jnp.exp(m - m_all)
                num = num + acc * e
                den = den + l * e
            o = num / jnp.maximum(den, 1e-30)
            return jnp.concatenate([o[:, g * tq:(g + 1) * tq].T for g in range(GQA)], axis=-1)
        return sweep

    n_win = tzw_ref.shape[2] - 1
    os_ref[0] = make_sweep(ks_ref, vst_ref, tzs_ref, True, SEL_CHAINS, False)(0, qt)
    ow_ref[0] = make_sweep(kw_ref, vwt_ref, tzw_ref, False, n_win, True)(jnp.maximum(qt - (n_win - 1), 0), qt)


def _sel_win_prompt(q5, ks, vst, kw, vwt, sel, tzs, tzw):
    B, _, _, T, _ = q5.shape
    NB = sel.shape[2]
    tq = ATT_TQ
    k_spec = pl.BlockSpec((1, 1, T, HEAD_DIM), lambda b, h, i: (b, h, 0, 0))
    vt_spec = pl.BlockSpec((1, 1, HEAD_DIM, T), lambda b, h, i: (b, h, 0, 0))
    tz_spec = lambda tz: pl.BlockSpec((1,) + tz.shape[1:], lambda b, h, i: (h, 0, 0, 0, 0))
    o_spec = pl.BlockSpec((1, tq, GQA * HEAD_DIM), lambda b, h, i: (b, i, h))
    return pl.pallas_call(
        functools.partial(_sel_win_kernel, tq=tq),
        out_shape=(jax.ShapeDtypeStruct((B, T, D_ATT), F32), jax.ShapeDtypeStruct((B, T, D_ATT), F32)),
        grid=(B, N_KV_HEADS, T // tq),
        in_specs=[pl.BlockSpec((1, 1, GQA, tq, HEAD_DIM), lambda b, h, i: (b, h, 0, i, 0)),
                  k_spec, vt_spec, k_spec, vt_spec,
                  pl.BlockSpec((1, 1, NB, tq), lambda b, h, i: (b, h, 0, i)),
                  tz_spec(tzs), tz_spec(tzw)],
        out_specs=(o_spec, o_spec),
        compiler_params=_cparams("parallel", "parallel", "parallel"),
        name="sel_win_prompt",
    )(q5, ks, vst, kw, vwt, sel, tzs, tzw)


def _gate_expand_matrix():
    m = np.zeros((3, 2 * LANE, D_ATT), np.float32)
    for r in range(3):
        for h in range(N_HEADS):
            m[r, h * 3 + r, h * HEAD_DIM:(h + 1) * HEAD_DIM] = 1.0
            m[r, LANE + h * 3 + r, h * HEAD_DIM:(h + 1) * HEAD_DIM] = 1.0
    return m


def _split_bf16(x):
    hi = x.astype(BF16)
    return hi, (x - hi.astype(F32)).astype(BF16)


def _post_mixer_kernel(y_ref, u_ref, oc_ref, os_ref, ow_ref, g_ref, x_ref, gate_ref, sh_ref, sc_ref,
                       dskip_ref, wglu_ref, bglu_ref, gexp_ref, wout_ref, lng_ref, lnb_ref,
                       wr_ref, br_ref, x1_ref, hm_ref, te_ref, tw_ref):
    y = y_ref[0] + dskip_ref[...] * u_ref[0]
    gl = jax.nn.gelu(y)
    ssm = gl * jax.nn.sigmoid(jnp.dot(gl.astype(BF16), wglu_ref[...], preferred_element_type=F32)
                              + bglu_ref[...])
    sg = jnp.concatenate(_split_bf16(jax.nn.sigmoid(g_ref[0])), axis=1)
    att = jnp.zeros_like(oc_ref[0])
    for r, o_ref in enumerate((oc_ref, os_ref, ow_ref)):
        att = att + jnp.dot(sg, gexp_ref[r], preferred_element_type=F32) * o_ref[0]
    h = (jnp.dot(ssm.astype(BF16), wout_ref[:D_SSM, :], preferred_element_type=F32)
         + jnp.dot(att.astype(BF16), wout_ref[D_SSM:, :], preferred_element_type=F32))
    z = DN_ALPHA * x_ref[0] + gate_ref[0] * h
    x1 = _layer_norm(z) * lng_ref[...] + lnb_ref[...]
    x1_ref[0] = x1
    hm = _layer_norm(x1) * (1.0 + sc_ref[0]) + sh_ref[0]
    hm_ref[0] = hm
    hm_hi, hm_lo = _split_bf16(hm)
    logits = (jnp.dot(hm_hi, wr_ref[0], preferred_element_type=F32)
              + jnp.dot(hm_lo, wr_ref[0], preferred_element_type=F32)
              + jnp.dot(hm_hi, wr_ref[1], preferred_element_type=F32)) + br_ref[...]
    lane = lax.broadcasted_iota(jnp.int32, logits.shape, 1)
    work = jnp.where(lane < N_EXPERTS, logits, -jnp.inf)
    te = jnp.zeros(logits.shape, jnp.int32)
    tv = jnp.zeros(logits.shape, F32)
    for k in range(TOP_K):
        best = jnp.max(work, axis=-1, keepdims=True)
        arg = jnp.min(jnp.where(work == best, lane, LANE), axis=-1, keepdims=True)
        te = jnp.where(lane == k, arg, te)
        tv = jnp.where(lane == k, best, tv)
        work = jnp.where(lane == arg, -jnp.inf, work)
    ex = jnp.where(lane < TOP_K, jnp.exp(tv - tv[:, 0:1]), 0.0)
    te_ref[0] = te
    tw_ref[0] = ex / jnp.sum(ex, axis=-1, keepdims=True)


def _post_mixer(y, u, oc, osel, ow, g, x, gate, shift, scale, w, tm):
    B, T, D = x.shape
    R = gate.shape[1]
    rb = 1 if R == 1 else tm
    mod_map = (lambda b, i: (b, 0, 0)) if R == 1 else (lambda b, i: (b, i, 0))
    row = lambda n: pl.BlockSpec((1, tm, n), lambda b, i: (b, i, 0))
    mod = pl.BlockSpec((1, rb, D), mod_map)
    full = lambda a: pl.BlockSpec(a.shape, lambda b, i: (0,) * a.ndim)
    consts = (w['d_skip'], w['w_glu'], w['b_glu'], w['gexp'], w['w_out'], w['ln1_g'], w['ln1_b'],
              w['w_router'], w['b_router'])
    return pl.pallas_call(
        _post_mixer_kernel,
        out_shape=(jax.ShapeDtypeStruct((B, T, D), F32), jax.ShapeDtypeStruct((B, T, D), F32),
                   jax.ShapeDtypeStruct((B, T, LANE), jnp.int32), jax.ShapeDtypeStruct((B, T, LANE), F32)),
        grid=(B, T // tm),
        in_specs=[row(D_SSM), row(D_SSM), row(D_ATT), row(D_ATT), row(D_ATT), row(LANE), row(D),
                  mod, mod, mod] + [full(a) for a in consts],
        out_specs=(row(D), row(D), row(LANE), row(LANE)),
        compiler_params=_cparams("parallel", "parallel"),
        name="post_mixer",
    )(y, u, oc, osel, ow, g, x, gate, shift, scale, *consts)


def _expert_kernel(e_ref, blk_ref, lo_ref, hi_ref, first_ref, x_ref, wgu_ref, bgu_ref, wd_ref, bd_ref, o_ref,
                   wgu_s, wd_s):
    i = pl.program_id(0)
    fresh = (i == 0) | (e_ref[i] != e_ref[jnp.maximum(i - 1, 0)])

    @pl.when(fresh)
    def _():
        wgu_s[...] = wgu_ref[0].astype(BF16)
        wd_s[...] = wd_ref[0].astype(BF16)

    @pl.when(first_ref[i] == 1)
    def _():
        o_ref[...] = jnp.zeros_like(o_ref)

    @pl.when(hi_ref[i] > lo_ref[i])
    def _():
        gu = jnp.dot(x_ref[...].astype(BF16), wgu_s[...], preferred_element_type=F32) + bgu_ref[0]
        gate = jnp.minimum(gu[:, :D_FF], SWIGLU_LIMIT)
        up = jnp.clip(gu[:, D_FF:], -SWIGLU_LIMIT, SWIGLU_LIMIT)
        hh = (up + 1.0) * gate * jax.nn.sigmoid(SWIGLU_ALPHA * gate)
        y = jnp.dot(hh.astype(BF16), wd_s[...], preferred_element_type=F32) + bd_ref[0]
        row = blk_ref[i] * MOE_ROWS + lax.broadcasted_iota(jnp.int32, (MOE_ROWS, 1), 0)
        o_ref[...] = jnp.where((row >= lo_ref[i]) & (row < hi_ref[i]), y, o_ref[...])


def _experts(xb, items, w_gate_up, b_gate_up, w_down, b_down):
    rows, D = xb.shape
    n_items = items[0].shape[0]
    wmap = lambda i, e, blk, lo, hi, first: (e[i], 0, 0)
    rmap = lambda i, e, blk, lo, hi, first: (blk[i], 0)
    grid_spec = pltpu.PrefetchScalarGridSpec(
        num_scalar_prefetch=5,
        grid=(n_items,),
        in_specs=[pl.BlockSpec((MOE_ROWS, D), rmap),
                  pl.BlockSpec((1, D, 2 * D_FF), wmap),
                  pl.BlockSpec((1, 1, 2 * D_FF), wmap),
                  pl.BlockSpec((1, D_FF, D), wmap),
                  pl.BlockSpec((1, 1, D), wmap)],
        out_specs=pl.BlockSpec((MOE_ROWS, D), rmap),
        scratch_shapes=[pltpu.VMEM((D, 2 * D_FF), BF16), pltpu.VMEM((D_FF, D), BF16)],
    )
    return pl.pallas_call(
        _expert_kernel,
        out_shape=jax.ShapeDtypeStruct((rows, D), F32),
        grid_spec=grid_spec,
        compiler_params=_cparams("arbitrary"),
        name="moe_experts",
    )(*items, xb, w_gate_up, b_gate_up.reshape(N_EXPERTS, 1, 2 * D_FF), w_down,
      b_down.reshape(N_EXPERTS, 1, D))


def _moe_dispatch(top_e, n):
    blk = MOE_ROWS
    nk = n * TOP_K
    cb = 128
    assert nk % cb == 0
    e = top_e.reshape(-1)
    onehot = (e[:, None] == jnp.arange(N_EXPERTS)[None, :]).astype(F32)
    oh3 = onehot.reshape(nk // cb, cb, N_EXPERTS)
    tri = jnp.asarray(np.tril(np.ones((cb, cb), np.float32), -1))
    within = jnp.einsum('ij,bje->bie', tri, oh3, precision=HIGHEST)
    blk_tot = jnp.sum(oh3, axis=1)
    blk_off = jnp.cumsum(blk_tot, axis=0) - blk_tot
    counts = jnp.sum(blk_tot, axis=0)
    start = jnp.cumsum(counts) - counts
    dest = jnp.sum((within + blk_off[:, None, :] + start[None, None, :]) * oh3, axis=-1)
    dest = dest.reshape(nk).astype(jnp.int32)
    order = jnp.argsort(dest)
    n_blk = -(-nk // blk)
    row_tok = jnp.concatenate([(order // TOP_K).astype(jnp.int32), jnp.full((n_blk * blk - nk,), n, jnp.int32)])
    counts_i, start_i = counts.astype(jnp.int32), start.astype(jnp.int32)
    first_b = start_i // blk
    last_b = (start_i + counts_i - 1) // blk
    n_it = jnp.where(counts_i > 0, last_b - first_b + 1, 0)
    it_end = jnp.cumsum(n_it)
    it_start = it_end - n_it
    n_items = n_blk + N_EXPERTS - 1
    i = jnp.arange(n_items)
    live = i < it_end[-1]
    it_e = jnp.minimum(jnp.sum(it_end[None, :] <= i[:, None], axis=1), N_EXPERTS - 1)
    it_blk = jnp.where(live, first_b[it_e] + i - it_start[it_e], n_blk - 1)
    it_lo = jnp.where(live, start_i[it_e], 0)
    it_hi = jnp.where(live, start_i[it_e] + counts_i[it_e], 0)
    it_first = jnp.concatenate([jnp.ones((1,), jnp.int32), (it_blk[1:] != it_blk[:-1]).astype(jnp.int32)])
    items = tuple(a.astype(jnp.int32) for a in (it_e, it_blk, it_lo, it_hi, it_first))
    return row_tok, dest.reshape(n, TOP_K), items


def _final_kernel(x_ref, y0_ref, y1_ref, y2_ref, y3_ref, tw_ref, gate_ref, lng_ref, lnb_ref, o_ref):
    tw = tw_ref[0]
    y = jnp.zeros_like(x_ref[0])
    for k, y_ref in enumerate((y0_ref, y1_ref, y2_ref, y3_ref)):
        y = y + tw[:, k:k + 1] * y_ref[0]
    z = DN_ALPHA * x_ref[0] + gate_ref[0] * y
    o_ref[0] = _layer_norm(z) * lng_ref[...] + lnb_ref[...]


def _final(x1, ys, tw, gate, ln_g, ln_b, tm):
    B, T, D = x1.shape
    R = gate.shape[1]
    rb = 1 if R == 1 else tm
    mod_map = (lambda b, i: (b, 0, 0)) if R == 1 else (lambda b, i: (b, i, 0))
    row = lambda n: pl.BlockSpec((1, tm, n), lambda b, i: (b, i, 0))
    vec = pl.BlockSpec((1, D), lambda b, i: (0, 0))
    return pl.pallas_call(
        _final_kernel,
        out_shape=jax.ShapeDtypeStruct((B, T, D), F32),
        grid=(B, T // tm),
        in_specs=[row(D), row(D), row(D), row(D), row(D), row(LANE),
                  pl.BlockSpec((1, rb, D), mod_map), vec, vec],
        out_specs=row(D),
        compiler_params=_cparams("parallel", "parallel"),
        name="moe_combine_ln",
    )(x1, *ys, tw, gate, ln_g, ln_b)


def _cmp_select_step_kernel(q_ref, kv_ref, bias_ref, pool_ref, o_ref, idx_ref, *, n_cmp, n_blk, q_pos):
    q = q_ref[0].astype(BF16)
    ncp = kv_ref.shape[1]
    nbp = pool_ref.shape[1]
    hd = HEAD_DIM
    kv = kv_ref[0]
    kb = [kv[:, h * hd:(h + 1) * hd].astype(BF16) for h in range(N_KV_HEADS)]
    vb = [kv[:, (N_KV_HEADS + h) * hd:(N_KV_HEADS + h + 1) * hd].astype(BF16) for h in range(N_KV_HEADS)]
    row = lax.broadcasted_iota(jnp.int32, (N_HEADS, 1), 0)
    first = row < GQA
    s = jnp.where(first, _nt_dot(q, kb[0]), _nt_dot(q, kb[1])) * (hd ** -0.5)
    s = s + bias_ref[...]
    ci = lax.broadcasted_iota(jnp.int32, (N_HEADS, ncp), 1)
    mask = (ci * CMP_STRIDE + CMP_BLOCK - 1 <= q_pos) & (ci < n_cmp)
    s = jnp.where(mask, s, NEG)
    m = jnp.max(s, axis=-1, keepdims=True)
    p = jnp.where(mask, jnp.exp(s - m), 0.0)
    p = p / jnp.maximum(jnp.sum(p, axis=-1, keepdims=True), 1e-30)
    pb = p.astype(BF16)
    o_ref[0] = jnp.where(first, jnp.dot(pb, vb[0], preferred_element_type=F32),
                         jnp.dot(pb, vb[1], preferred_element_type=F32))
    imp0 = jnp.sum(jnp.where(first, p, 0.0), axis=0, keepdims=True)
    imp1 = jnp.sum(jnp.where(first, 0.0, p), axis=0, keepdims=True)
    imp = jnp.where(first, imp0, imp1)
    sb = jnp.dot(imp, pool_ref[...], precision=HIGHEST, preferred_element_type=F32)
    cur = q_pos // SEL_BLOCK
    bi = lax.broadcasted_iota(jnp.int32, (nbp, nbp), 0)
    bj = lax.broadcasted_iota(jnp.int32, (nbp, nbp), 1)
    blk = lax.broadcasted_iota(jnp.int32, (1, nbp), 1)
    causal = blk <= cur
    forced = (blk == 0) | (blk == cur) | (blk == cur - 1)
    rsel = lax.broadcasted_iota(jnp.int32, (N_SEL, nbp), 0)
    for h in range(N_KV_HEADS):
        sc = jnp.where(forced & causal, 1e4, jnp.where(causal, sb[h * GQA:h * GQA + 1, :], -1.0))
        sc = jnp.where(blk < n_blk, sc, -2.0)
        scb = jnp.broadcast_to(sc, (nbp, nbp))
        col = jnp.sum(jnp.where(bi == bj, scb, 0.0), axis=1, keepdims=True)
        ahead = (col > scb) | ((col == scb) & (bi < bj))
        rank = jnp.sum(ahead.astype(jnp.int32), axis=0, keepdims=True)
        hit = jnp.broadcast_to(rank, (N_SEL, nbp)) == rsel
        idx = jnp.sum(jnp.where(hit, jnp.broadcast_to(blk, (N_SEL, nbp)), 0), axis=1, keepdims=True)
        idx_ref[0, h] = jnp.broadcast_to(idx, (N_SEL, LANE))


def _cmp_select_step(q, ckv, bias, pool, n_cmp, n_blk, q_pos):
    B = q.shape[0]
    NCp = ckv.shape[1]
    return pl.pallas_call(
        functools.partial(_cmp_select_step_kernel, n_cmp=n_cmp, n_blk=n_blk, q_pos=q_pos),
        out_shape=(jax.ShapeDtypeStruct((B, N_HEADS, HEAD_DIM), F32),
                   jax.ShapeDtypeStruct((B, N_KV_HEADS, N_SEL, LANE), jnp.int32)),
        grid=(B,),
        in_specs=[pl.BlockSpec((1, N_HEADS, HEAD_DIM), lambda b: (b, 0, 0)),
                  pl.BlockSpec((1, NCp, D_KV), lambda b: (b, 0, 0)),
                  pl.BlockSpec(bias.shape, lambda b: (0, 0)),
                  pl.BlockSpec(pool.shape, lambda b: (0, 0))],
        out_specs=(pl.BlockSpec((1, N_HEADS, HEAD_DIM), lambda b: (b, 0, 0)),
                   pl.BlockSpec((1, N_KV_HEADS, N_SEL, LANE), lambda b: (b, 0, 0, 0))),
        compiler_params=_cparams("parallel"),
        name="cmp_select_step",
    )(q, ckv, bias, pool)


def _sel_step_kernel(pg_ref, idx_ref, q_ref, *refs, n_past, q_pos):
    page_refs = refs[:N_SEL]
    new_ref, bias_ref, kpos_ref, o_ref = refs[N_SEL:]
    b, h = pl.program_id(0), pl.program_id(1)
    base = (b * N_KV_HEADS + h) * N_SEL
    kts, vts = [], []
    for j in range(N_SEL):
        is_new = idx_ref[base + j] >= n_past
        kts.append(jnp.where(is_new, new_ref[0, 0, 0], page_refs[j][0, 0, 0]))
        vts.append(jnp.where(is_new, new_ref[0, 1, 0], page_refs[j][0, 1, 0]))
    kt = jnp.concatenate(kts, axis=1).astype(BF16)
    vt = jnp.concatenate(vts, axis=1).astype(BF16)
    s = jnp.dot(q_ref[0].astype(BF16), kt, preferred_element_type=F32) * (HEAD_DIM ** -0.5) + bias_ref[0, 0]
    mask = kpos_ref[0, 0] <= q_pos
    s = jnp.where(mask, s, NEG)
    m = jnp.max(s, axis=-1, keepdims=True)
    p = jnp.where(mask, jnp.exp(s - m), 0.0)
    l = jnp.sum(p, axis=-1, keepdims=True)
    o_ref[0, 0] = _nt_dot(p.astype(BF16), vt) / jnp.maximum(l, 1e-30)


def _sel_step(q, pool_t, new_t, bias_sel, kpos, pages, idx_flat, n_past, q_pos):
    B = q.shape[0]
    nk = N_SEL * PAGE_SIZE
    slot = lambda b, h, j: (b * N_KV_HEADS + h) * N_SEL + j
    page_spec = lambda j: pl.BlockSpec((1, 2, 1, HEAD_DIM, PAGE_SIZE),
                                       lambda b, h, pg, ix, j=j: (pg[slot(b, h, j)], 0, h, 0, 0))
    grid_spec = pltpu.PrefetchScalarGridSpec(
        num_scalar_prefetch=2,
        grid=(B, N_KV_HEADS),
        in_specs=[pl.BlockSpec((1, N_HEADS, HEAD_DIM), lambda b, h, pg, ix: (b, 0, 0))]
        + [page_spec(j) for j in range(N_SEL)]
        + [pl.BlockSpec((1, 2, 1, HEAD_DIM, PAGE_SIZE), lambda b, h, pg, ix: (b, 0, h, 0, 0)),
           pl.BlockSpec((1, 1, N_HEADS, nk), lambda b, h, pg, ix: (b, h, 0, 0)),
           pl.BlockSpec((1, 1, 1, nk), lambda b, h, pg, ix: (b, h, 0, 0))],
        out_specs=pl.BlockSpec((1, 1, N_HEADS, HEAD_DIM), lambda b, h, pg, ix: (b, h, 0, 0)),
    )
    return pl.pallas_call(
        functools.partial(_sel_step_kernel, n_past=n_past, q_pos=q_pos),
        out_shape=jax.ShapeDtypeStruct((B, N_KV_HEADS, N_HEADS, HEAD_DIM), F32),
        grid_spec=grid_spec,
        compiler_params=_cparams("arbitrary", "arbitrary"),
        name="sel_step",
    )(pages, idx_flat, q, *([pool_t] * N_SEL), new_t, bias_sel, kpos)


def _win_step_kernel(q_ref, w_ref, new_ref, bias_ref, bias0_ref, o_ref):
    q = q_ref[0]
    qb = q.astype(BF16)
    row = lax.broadcasted_iota(jnp.int32, (N_HEADS, 1), 0)
    first = row < GQA
    w = w_ref[0]
    hd = HEAD_DIM
    kb = [w[:, h * hd:(h + 1) * hd].astype(BF16) for h in range(N_KV_HEADS)]
    vb = [w[:, (N_KV_HEADS + h) * hd:(N_KV_HEADS + h + 1) * hd].astype(BF16) for h in range(N_KV_HEADS)]
    s = jnp.where(first, _nt_dot(qb, kb[0]), _nt_dot(qb, kb[1])) * (hd ** -0.5) + bias_ref[...]
    new = new_ref[0]
    kn = jnp.where(first, new[:, 0:hd], new[:, hd:2 * hd])
    vn = jnp.where(first, new[:, 2 * hd:3 * hd], new[:, 3 * hd:])
    sn = jnp.sum(q * kn, axis=-1, keepdims=True) * (hd ** -0.5) + bias0_ref[...]
    m = jnp.maximum(jnp.max(s, axis=-1, keepdims=True), sn)
    p = jnp.exp(s - m)
    pn = jnp.exp(sn - m)
    l = jnp.sum(p, axis=-1, keepdims=True) + pn
    pb = p.astype(BF16)
    acc = jnp.where(first, jnp.dot(pb, vb[0], preferred_element_type=F32),
                    jnp.dot(pb, vb[1], preferred_element_type=F32)) + pn * vn
    o_ref[0] = acc / jnp.maximum(l, 1e-30)


def _win_step(q, win, new, bias, bias0):
    B, W, _ = win.shape
    return pl.pallas_call(
        _win_step_kernel,
        out_shape=jax.ShapeDtypeStruct((B, N_HEADS, HEAD_DIM), F32),
        grid=(B,),
        in_specs=[pl.BlockSpec((1, N_HEADS, HEAD_DIM), lambda b: (b, 0, 0)),
                  pl.BlockSpec((1, W, D_KV), lambda b: (b, 0, 0)),
                  pl.BlockSpec((1, 1, D_KV), lambda b: (b, 0, 0)),
                  pl.BlockSpec((N_HEADS, W), lambda b: (0, 0)),
                  pl.BlockSpec((N_HEADS, 1), lambda b: (0, 0))],
        out_specs=pl.BlockSpec((1, N_HEADS, HEAD_DIM), lambda b: (b, 0, 0)),
        compiler_params=_cparams("parallel"),
        name="win_step",
    )(q, win, new, bias, bias0)


def _split_heads(kv, dtype):
    B, L, _ = kv.shape
    kv5 = kv.reshape(B, L, 2, N_KV_HEADS, HEAD_DIM)
    return (jnp.transpose(kv5[:, :, 0], (0, 2, 1, 3)).astype(dtype),
            jnp.transpose(kv5[:, :, 1], (0, 2, 1, 3)).astype(dtype))


def _nsa_prompt(q5, kvc, ks, vst, kw, vwt, cmp_tab, rel_bias):
    B, T, _ = kvc.shape
    nc = T // CMP_STRIDE
    nb = T // SEL_BLOCK
    ckv = _compress_out(_compress_in(kvc.reshape(B, nc, CMP_STRIDE * D_KV), cmp_tab), cmp_tab, nc)
    kc, vc = _split_heads(ckv, BF16)
    vct = jnp.transpose(vc, (0, 1, 3, 2))
    bias_n = _bias_by_distance(rel_bias, T)
    n_qt, n_kt = T // ATT_TQ, T // ATT_TK
    n_ds = min(n_kt, -(-(REL_MAX_DIST + ATT_TK - 1) // ATT_TK) + 1)
    n_dw = min(n_kt, WINDOW // ATT_TK + 1)
    tzs, tzw, bias_tab = _bias_tables(bias_n, n_qt, nc // 8, n_ds, n_dw, ATT_TQ, ATT_TK)
    pool = jnp.asarray(_pool_matrix(nc, nb))
    o_cmp, sel = _cmp_select_prompt(q5, kc, vct, bias_tab, pool, nc - 1)
    o_sel, o_win = _sel_win_prompt(q5, ks, vst, kw, vwt, sel, tzs, tzw)
    return o_cmp, o_sel, o_win


def _nsa_sample(q, kvc, kvs, kvw, pool_cmp, pool_sel, win_buf, page_table, cmp_tab, rel_bias):
    B = q.shape[0]
    n_pages = page_table.shape[1]
    past_len = n_pages * PAGE_SIZE
    q_pos = past_len
    lp = -(-(past_len + 1) // SEL_BLOCK) * SEL_BLOCK
    n_cmp = lp // CMP_STRIDE - 1
    n_blk = lp // SEL_BLOCK
    n_past_chunks = past_len // CMP_STRIDE
    n_tail = 8
    assert n_past_chunks + n_tail >= n_cmp + 1
    n_chunks = n_past_chunks + n_tail
    feature_major = lambda pool: jnp.transpose(pool, (0, 2, 3, 4, 1))
    z_past = _compress_in_paged(feature_major(pool_cmp), page_table, cmp_tab)
    tail = jnp.pad(kvc[:, None, :], ((0, 0), (0, n_tail * CMP_STRIDE - 1), (0, 0)))
    z_tail = _compress_in(tail.reshape(B, n_tail, CMP_STRIDE * D_KV), cmp_tab)
    ncp = -(-n_chunks // LANE) * LANE
    nbp = -(-n_blk // LANE) * LANE
    ckv = _compress_out(jnp.concatenate([z_past, z_tail], axis=1), cmp_tab, ncp)
    bias_n = _bias_by_distance(rel_bias, q_pos + 1)
    n_back = max((n_pages + 1) * PAGE_SIZE, ncp * CMP_STRIDE + CMP_BLOCK)
    back = jnp.concatenate([bias_n[:, ::-1], jnp.broadcast_to(bias_n[:, :1], (N_HEADS, n_back - q_pos - 1))], 1)
    bias_c = back[:, CMP_BLOCK - 1:CMP_BLOCK - 1 + ncp * CMP_STRIDE:CMP_STRIDE]
    pool = jnp.asarray(_pool_matrix(ncp, nbp).T)
    q3 = q.reshape(B, N_HEADS, HEAD_DIM)
    o_cmp, idx = _cmp_select_step(q3, ckv, bias_c, pool, n_cmp, n_blk, q_pos)
    idx = idx[..., 0]
    bpp = PAGE_SIZE // SEL_BLOCK
    n_past = n_pages * bpp
    lpage = idx // bpp
    pages = jnp.take_along_axis(page_table, jnp.minimum(lpage, n_pages - 1).reshape(B, -1), axis=1)
    new_t = jnp.pad(kvs.reshape(B, 2, N_KV_HEADS, HEAD_DIM, 1), ((0, 0),) * 4 + ((0, PAGE_SIZE - 1),))
    bias_page = jnp.transpose(back[:, :(n_pages + 1) * PAGE_SIZE].reshape(N_HEADS, n_pages + 1, PAGE_SIZE),
                              (1, 0, 2))
    bias_sel = jnp.transpose(bias_page[lpage], (0, 1, 3, 2, 4)).reshape(B, N_KV_HEADS, N_HEADS, -1)
    kpos = lpage[..., None] * PAGE_SIZE + jnp.arange(PAGE_SIZE)
    ok = (kpos // SEL_BLOCK == idx[..., None]) & (idx <= q_pos // SEL_BLOCK)[..., None]
    kpos = jnp.where(ok, kpos, q_pos + 1).reshape(B, N_KV_HEADS, 1, -1).astype(jnp.int32)
    o_sel = _sel_step(q3, feature_major(pool_sel), new_t, bias_sel, kpos, pages.reshape(-1).astype(jnp.int32),
                      idx.reshape(-1).astype(jnp.int32), n_past, q_pos)
    o_sel = jnp.concatenate([o_sel[:, h, h * GQA:(h + 1) * GQA] for h in range(N_KV_HEADS)], axis=1)
    wb = win_buf.shape[1]
    bias_w = bias_n[:, 1:wb + 1][:, ::-1]
    o_win = _win_step(q3, win_buf.reshape(B, wb, D_KV), kvw[:, None, :], bias_w, bias_n[:, 0:1])
    return o_cmp.reshape(B, D_ATT), o_sel.reshape(B, D_ATT), o_win.reshape(B, D_ATT)


def kernel(x_prompt, x_sample, cache_cmp_kv, cache_sel_kv, state_win_kv, state_ssm_re, state_ssm_im, page_table,
           c_prompt, c_sample, w_ada, b_ada, w_in, lam_re, lam_im, log_dt, b_re, b_im, c_re, c_im, d_skip,
           w_glu, b_glu, phi_pe, phi_w1, phi_b1, phi_w2, phi_b2, rel_bias, w_out, ln1_g, ln1_b,
           w_router, b_router, w_gate_up, b_gate_up, w_down, b_down, ln2_g, ln2_b):
    assert w_ada.shape[0] == DEPTH == 1
    l = 0
    Bp, T, D = x_prompt.shape
    Bs = x_sample.shape[0]
    kv_tail = (2, N_KV_HEADS, HEAD_DIM)

    n_c = Bp + Bs
    c_all = jnp.pad(jnp.concatenate([c_prompt, c_sample], 0), ((0, -n_c % 8), (0, 0)))
    m_all = _adaln(c_all, w_ada[l], b_ada[l])
    m_p = m_all[:Bp].reshape(Bp, 6, D)
    m_s = m_all[Bp:n_c].reshape(Bs, 6, D)
    mod_p = [m_p[:, i:i + 1, :] for i in range(6)]
    mod_s = [m_s[None, :, i, :] for i in range(6)]

    w_in_pad = jnp.pad(w_in[l], ((0, 0), (0, D_IN_PAD - D_IN))).astype(BF16)
    n_levels = max(1, int(math.log2(T // SSM_CHUNK)))
    ssm_tab = _ssm_tables(lam_re[l], lam_im[l], log_dt[l], b_re[l], b_im[l], c_re[l], c_im[l],
                          SSM_CHUNK, n_levels)
    cmp_tab = _compress_tables(phi_pe[l], phi_w1[l], phi_b1[l], phi_w2[l], phi_b2[l])
    w_post = dict(
        d_skip=d_skip[l].reshape(1, D_SSM), w_glu=w_glu[l].astype(BF16), b_glu=b_glu[l].reshape(1, D_SSM),
        gexp=jnp.asarray(_gate_expand_matrix(), dtype=BF16), w_out=w_out[l].astype(BF16),
        ln1_g=ln1_g[l].reshape(1, D), ln1_b=ln1_b[l].reshape(1, D),
        w_router=jnp.stack(_split_bf16(jnp.pad(w_router[l], ((0, 0), (0, LANE - N_EXPERTS))))),
        b_router=jnp.pad(b_router[l], (0, LANE - N_EXPERTS)).reshape(1, LANE))

    u, q5, kvc, kvs, kvw, g, ks, vst, kw, vwt = _mixer_in(x_prompt, mod_p[0], mod_p[1], w_in_pad, 512, True)
    y_ssm, h_p = _ssm_prompt(u, ssm_tab)
    o_cmp, o_sel, o_win = _nsa_prompt(q5, kvc, ks, vst, kw, vwt, cmp_tab, rel_bias)
    x1_p, hm_p, te_p, tw_p = _post_mixer(y_ssm, u, o_cmp, o_sel, o_win, g, x_prompt,
                                         mod_p[2], mod_p[3], mod_p[4], w_post, tm=256)

    u_s, q_s, kvc_s, kvs_s, kvw_s, g_s = _mixer_in(x_sample.reshape(1, Bs, D), mod_s[0], mod_s[1],
                                                   w_in_pad, Bs, False)
    y_s, h_s = _ssm_sample(u_s[0], state_ssm_re[l], state_ssm_im[l], ssm_tab, c_re[l], c_im[l])
    oc_s, os_s, ow_s = _nsa_sample(q_s[0].astype(F32), kvc_s[0], kvs_s[0], kvw_s[0], cache_cmp_kv[l],
                                   cache_sel_kv[l], state_win_kv[l], page_table, cmp_tab, rel_bias)
    x1_s, hm_s, te_s, tw_s = _post_mixer(y_s[None], u_s, oc_s[None], os_s[None], ow_s[None], g_s,
                                         x_sample.reshape(1, Bs, D), mod_s[2], mod_s[3], mod_s[4],
                                         w_post, tm=Bs)

    n_p = Bp * T
    n_all = n_p + Bs
    hm_all = jnp.concatenate([hm_p.reshape(n_p, D), hm_s.reshape(Bs, D)], 0)
    te_all = jnp.concatenate([te_p.reshape(n_p, LANE), te_s.reshape(Bs, LANE)], 0)[:, :TOP_K]
    row_tok, dest, items = _moe_dispatch(te_all, n_all)
    xb = jnp.concatenate([hm_all, jnp.zeros((1, D), F32)], 0)[row_tok]
    yb = _experts(xb, items, w_gate_up[l], b_gate_up[l], w_down[l], b_down[l])
    ys_p = [yb[dest[:n_p, k]].reshape(Bp, T, D) for k in range(TOP_K)]
    ys_s = [yb[dest[n_p:, k]].reshape(1, Bs, D) for k in range(TOP_K)]
    ln2g, ln2b = ln2_g[l].reshape(1, D), ln2_b[l].reshape(1, D)
    out_p = _final(x1_p, ys_p, tw_p, mod_p[5], ln2g, ln2b, tm=512)
    out_s = _final(x1_s, ys_s, tw_s, mod_s[5], ln2g, ln2b, tm=Bs)

    wlen = min(WINDOW, T)
    win_s = jnp.concatenate([state_win_kv[l], kvw_s[0].reshape(Bs, 1, *kv_tail)], 1)[:, -state_win_kv.shape[2]:]
    p_state = SSM_STATE
    return (out_p, out_s.reshape(Bs, 1, D),
            kvc.reshape(1, Bp, T, *kv_tail), kvc_s[0].reshape(1, Bs, 1, *kv_tail),
            kvs.reshape(1, Bp, T, *kv_tail), kvs_s[0].reshape(1, Bs, 1, *kv_tail),
            kvw[:, T - wlen:].reshape(1, Bp, wlen, *kv_tail), win_s[None],
            h_p[None, ..., :p_state], h_p[None, ..., p_state:],
            h_s[None, ..., :p_state], h_s[None, ..., p_state:])
```

```python
import functools
import math

import numpy as np
import jax
import jax.numpy as jnp
from jax import lax
from jax.experimental import pallas as pl
from jax.experimental.pallas import tpu as pltpu

D_MODEL = 1024
DEPTH = 1
PAST_LEN = 16384
PAGE_SIZE = 128
D_SSM = 512
SSM_GROUP = 16
N_SSM_GROUPS = D_SSM // SSM_GROUP
SSM_STATE = 64
N_HEADS = 8
HEAD_DIM = 64
N_KV_HEADS = 2
GQA = N_HEADS // N_KV_HEADS
D_ATT = N_HEADS * HEAD_DIM
D_KV = 2 * N_KV_HEADS * HEAD_DIM
CMP_STRIDE = 16
CMP_BLOCK = 2 * CMP_STRIDE
SEL_BLOCK = 64
N_SEL = 16
WINDOW = 512
NUM_BUCKETS = 32
REL_MAX_DIST = 1024
N_EXPERTS = 32
TOP_K = 4
D_FF = 1024
SWIGLU_LIMIT = 7.0
SWIGLU_ALPHA = 1.702
DN_ALPHA = (2 * DEPTH) ** 0.25
D_IN = D_SSM + D_ATT + 3 * D_KV + 3 * N_HEADS
NEG = -1e30
F32 = jnp.float32
BF16 = jnp.bfloat16
HIGHEST = lax.Precision.HIGHEST

LANE = 128
D_IN_PAD = 1920
GATE_COL = D_SSM + D_ATT + 3 * D_KV
SSM_CHUNK = 16
ATT_TQ = 128
ATT_TK = 128
SEL_CHAINS = 4
MOE_ROWS = 256
PAGES_PER_STEP = 32
CHUNK_PITCH = 24
VMEM_LIMIT = 48 * 1024 * 1024
LN_EPS = 1e-5


def _cparams(*sem):
    return pltpu.CompilerParams(dimension_semantics=sem, vmem_limit_bytes=VMEM_LIMIT)


def _nt_dot(a, b):
    return lax.dot_general(a, b, (((1,), (1,)), ((), ())), preferred_element_type=F32)


def _layer_norm(x):
    mu = jnp.mean(x, axis=-1, keepdims=True)
    xc = x - mu
    var = jnp.mean(xc * xc, axis=-1, keepdims=True)
    return xc * lax.rsqrt(var + LN_EPS)


def _adaln_kernel(c_ref, w_ref, b_ref, o_ref):
    c = c_ref[...]
    s = c * jax.nn.sigmoid(c)
    o_ref[...] = jnp.dot(s, w_ref[...], precision=HIGHEST, preferred_element_type=F32) + b_ref[...]


def _adaln(c, w, b):
    n, d = c.shape
    dout = w.shape[1]
    tn = 1024
    return pl.pallas_call(
        _adaln_kernel,
        out_shape=jax.ShapeDtypeStruct((n, dout), F32),
        grid=(dout // tn,),
        in_specs=[pl.BlockSpec((n, d), lambda j: (0, 0)),
                  pl.BlockSpec((d, tn), lambda j: (0, j)),
                  pl.BlockSpec((1, tn), lambda j: (0, j))],
        out_specs=pl.BlockSpec((n, tn), lambda j: (0, j)),
        compiler_params=_cparams("arbitrary"),
        name="adaln",
    )(c, w, b.reshape(1, dout))


def _mixer_in_kernel(x_ref, sh_ref, sc_ref, w_ref, u_ref, q_ref, kvc_ref, kvs_ref, kvw_ref, g_ref, *att_refs):
    h = _layer_norm(x_ref[0]) * (1.0 + sc_ref[0]) + sh_ref[0]
    z = jnp.dot(h.astype(BF16), w_ref[...], preferred_element_type=F32)
    c0 = D_SSM
    c1 = c0 + D_ATT
    c2 = c1 + D_KV
    c3 = c2 + D_KV
    c4 = c3 + D_KV
    u_ref[0] = z[:, :c0]
    kvc_ref[0] = z[:, c1:c2]
    kvs_ref[0] = z[:, c2:c3]
    kvw_ref[0] = z[:, c3:c4]
    g_ref[0] = z[:, c4:c4 + LANE]
    if not att_refs:
        q_ref[0] = z[:, c0:c1].astype(BF16)
        return
    ks_ref, vst_ref, kw_ref, vwt_ref = att_refs
    hd, half = HEAD_DIM, N_KV_HEADS * HEAD_DIM
    for hq in range(N_HEADS):
        q_ref[0, hq // GQA, hq % GQA] = (z[:, c0 + hq * hd:c0 + (hq + 1) * hd] * (hd ** -0.5)).astype(BF16)
    for k_ref, vt_ref, base in ((ks_ref, vst_ref, c2), (kw_ref, vwt_ref, c3)):
        for hk in range(N_KV_HEADS):
            k_ref[0, hk] = z[:, base + hk * hd:base + (hk + 1) * hd].astype(BF16)
        vt = z[:, base + half:base + 2 * half].T
        vt_ref[0] = vt.reshape(N_KV_HEADS, hd, vt.shape[1]).astype(BF16)


def _mixer_in(x, shift, scale, w_pad, tm, attention_layouts):
    B, T, D = x.shape
    R = shift.shape[1]
    rb = 1 if R == 1 else tm
    mod_map = (lambda b, i: (b, 0, 0)) if R == 1 else (lambda b, i: (b, i, 0))
    row = lambda n: pl.BlockSpec((1, tm, n), lambda b, i: (b, i, 0))
    f32 = lambda n: jax.ShapeDtypeStruct((B, T, n), F32)
    if attention_layouts:
        q_shape = jax.ShapeDtypeStruct((B, N_KV_HEADS, GQA, T, HEAD_DIM), BF16)
        q_spec = pl.BlockSpec((1, N_KV_HEADS, GQA, tm, HEAD_DIM), lambda b, i: (b, 0, 0, i, 0))
        k_shape = jax.ShapeDtypeStruct((B, N_KV_HEADS, T, HEAD_DIM), BF16)
        k_spec = pl.BlockSpec((1, N_KV_HEADS, tm, HEAD_DIM), lambda b, i: (b, 0, i, 0))
        vt_shape = jax.ShapeDtypeStruct((B, N_KV_HEADS, HEAD_DIM, T), BF16)
        vt_spec = pl.BlockSpec((1, N_KV_HEADS, HEAD_DIM, tm), lambda b, i: (b, 0, 0, i))
        extra_shapes, extra_specs = (k_shape, vt_shape, k_shape, vt_shape), (k_spec, vt_spec, k_spec, vt_spec)
    else:
        q_shape, q_spec = jax.ShapeDtypeStruct((B, T, D_ATT), BF16), row(D_ATT)
        extra_shapes, extra_specs = (), ()
    return pl.pallas_call(
        _mixer_in_kernel,
        out_shape=(f32(D_SSM), q_shape, f32(D_KV), f32(D_KV), f32(D_KV), f32(LANE)) + extra_shapes,
        grid=(B, T // tm),
        in_specs=[row(D), pl.BlockSpec((1, rb, D), mod_map), pl.BlockSpec((1, rb, D), mod_map),
                  pl.BlockSpec((D, D_IN_PAD), lambda b, i: (0, 0))],
        out_specs=(row(D_SSM), q_spec, row(D_KV), row(D_KV), row(D_KV), row(LANE)) + extra_specs,
        compiler_params=_cparams("parallel", "parallel"),
        name="mixer_in",
    )(x, shift, scale, w_pad)


def _ssm_tables(lam_re, lam_im, log_dt, b_re, b_im, c_re, c_im, L, n_levels):
    G, P = lam_re.shape
    C = b_re.shape[-1]
    dt = jnp.exp(log_dt.astype(F32))[:, None]
    er, ei = lam_re * dt, lam_im * dt

    def power(k):
        kk = k.astype(F32)[:, None, None]
        mag = jnp.exp(kk * er)
        return mag * jnp.cos(kk * ei), mag * jnp.sin(kk * ei)

    lb_re, lb_im = power(jnp.ones((1,), F32))
    nr, ni = lb_re[0] - 1.0, lb_im[0]
    den = lam_re * lam_re + lam_im * lam_im
    fr = (nr * lam_re + ni * lam_im) / den
    fi = (ni * lam_re - nr * lam_im) / den
    bbr = fr[:, :, None] * b_re - fi[:, :, None] * b_im
    bbi = fr[:, :, None] * b_im + fi[:, :, None] * b_re
    pr, pi = power(jnp.arange(L + 1))
    clr = c_re[None] * pr[:, :, None, :] - c_im[None] * pi[:, :, None, :]
    cli = c_re[None] * pi[:, :, None, :] + c_im[None] * pr[:, :, None, :]
    kern = (jnp.einsum('kgcp,gpd->kgcd', clr[:L], bbr, precision=HIGHEST)
            - jnp.einsum('kgcp,gpd->kgcd', cli[:L], bbi, precision=HIGHEST))
    kz = jnp.concatenate([kern, jnp.zeros((1,) + kern.shape[1:], F32)], 0)
    ts = np.arange(L)
    lag = ts[None, :] - ts[:, None]
    lag = np.where(lag >= 0, lag, L)
    toep = kz[lag]
    toep = jnp.transpose(toep, (2, 0, 4, 1, 3)).reshape(G, L * C, L * C)
    rev = L - 1 - ts
    wsr = pr[rev][:, :, :, None] * bbr[None] - pi[rev][:, :, :, None] * bbi[None]
    wsi = pr[rev][:, :, :, None] * bbi[None] + pi[rev][:, :, :, None] * bbr[None]
    ws = jnp.concatenate([jnp.transpose(wsr, (1, 0, 3, 2)), jnp.transpose(wsi, (1, 0, 3, 2))], -1)
    ws = ws.reshape(G, L * C, 2 * P)
    wy = jnp.concatenate([jnp.transpose(clr[1:], (1, 3, 0, 2)), -jnp.transpose(cli[1:], (1, 3, 0, 2))], 1)
    wy = wy.reshape(G, 2 * P, L * C)
    lr, li = power(L * (2 ** jnp.arange(n_levels)))
    ar = jnp.transpose(jnp.concatenate([lr, lr], -1), (1, 0, 2))
    ai = jnp.transpose(jnp.concatenate([-li, li], -1), (1, 0, 2))
    return toep.astype(BF16), ws.astype(BF16), wy.astype(BF16), ar, ai, (lb_re[0], lb_im[0], bbr, bbi)


def _ssm_kernel(u_ref, toep_ref, ws_ref, wy_ref, ar_ref, ai_ref, y_ref, hl_ref, *, nb, nc, n_levels):
    u = u_ref[0]
    y1 = jnp.dot(u, toep_ref[0], preferred_element_type=F32)
    s = jnp.dot(u, ws_ref[0], preferred_element_type=F32)
    p2 = s.shape[-1]
    rows = lax.broadcasted_iota(jnp.int32, (nc, p2), 0)
    prev = []
    for b in range(nb):
        h = s[b * nc:(b + 1) * nc]
        for k in range(n_levels):
            d = 1 << k
            sh = jnp.where(rows >= d, pltpu.roll(h, d, axis=0), 0.0)
            sw = pltpu.roll(sh, p2 // 2, axis=1)
            h = h + ar_ref[0, k:k + 1, :] * sh + ai_ref[0, k:k + 1, :] * sw
        hl_ref[0, b:b + 1, :] = h[nc - 1:nc, :]
        prev.append(jnp.where(rows >= 1, pltpu.roll(h, 1, axis=0), 0.0))
    hp = jnp.concatenate(prev, axis=0)
    y2 = jnp.dot(hp.astype(BF16), wy_ref[0], preferred_element_type=F32)
    y_ref[0] = (y1 + y2).astype(y_ref.dtype)


def _ssm_prompt(u, tables):
    toep, ws, wy, ar, ai, _ = tables
    B, T, _ = u.shape
    G, C, L = N_SSM_GROUPS, SSM_GROUP, SSM_CHUNK
    nc = T // L
    n_levels = ar.shape[1]
    ug = jnp.transpose(u.reshape(B, nc, L, G, C), (3, 0, 1, 2, 4)).reshape(G, B * nc, L * C).astype(BF16)
    grp = lambda r, c: pl.BlockSpec((1, r, c), lambda g: (g, 0, 0))
    y, hl = pl.pallas_call(
        functools.partial(_ssm_kernel, nb=B, nc=nc, n_levels=n_levels),
        out_shape=(jax.ShapeDtypeStruct((G, B * nc, L * C), BF16),
                   jax.ShapeDtypeStruct((G, B, 2 * SSM_STATE), F32)),
        grid=(G,),
        in_specs=[grp(B * nc, L * C), grp(L * C, L * C), grp(L * C, 2 * SSM_STATE),
                  grp(2 * SSM_STATE, L * C), grp(n_levels, 2 * SSM_STATE), grp(n_levels, 2 * SSM_STATE)],
        out_specs=(grp(B * nc, L * C), grp(B, 2 * SSM_STATE)),
        compiler_params=_cparams("parallel"),
        name="ssm_prompt",
    )(ug, toep, ws, wy, ar, ai)
    y = jnp.transpose(y.reshape(G, B, nc, L, C), (1, 2, 3, 0, 4)).reshape(B, T, D_SSM)
    return y, jnp.transpose(hl, (1, 0, 2))


def _ssm_step_kernel(u_ref, h0_ref, bb_ref, lr_ref, li_ref, cy_ref, y_ref, h_ref):
    p = lr_ref.shape[-1] // 2
    bu = jnp.einsum('gbc,gcp->gbp', u_ref[...], bb_ref[...], preferred_element_type=F32)
    h0 = h0_ref[...]
    h0s = jnp.concatenate([h0[..., p:], h0[..., :p]], axis=-1)
    h = lr_ref[...] * h0 + li_ref[...] * h0s + bu
    h_ref[...] = h
    y_ref[...] = jnp.einsum('gbp,gpc->gbc', h.astype(BF16), cy_ref[...], preferred_element_type=F32)


def _ssm_sample(u, h0_re, h0_im, tables, c_re, c_im):
    lb_re, lb_im, bbr, bbi = tables[-1]
    B = u.shape[0]
    G, C, P = N_SSM_GROUPS, SSM_GROUP, SSM_STATE
    ug = jnp.transpose(u.reshape(B, G, C), (1, 0, 2)).astype(BF16)
    h0 = jnp.transpose(jnp.concatenate([h0_re, h0_im], -1), (1, 0, 2)).astype(F32)
    bb = jnp.concatenate([jnp.transpose(bbr, (0, 2, 1)), jnp.transpose(bbi, (0, 2, 1))], -1).astype(BF16)
    lr = jnp.concatenate([lb_re, lb_re], -1)[:, None, :]
    li = jnp.concatenate([-lb_im, lb_im], -1)[:, None, :]
    cy = jnp.concatenate([jnp.transpose(c_re, (0, 2, 1)), -jnp.transpose(c_im, (0, 2, 1))], 1).astype(BF16)
    y, h = pl.pallas_call(
        _ssm_step_kernel,
        out_shape=(jax.ShapeDtypeStruct((G, B, C), F32), jax.ShapeDtypeStruct((G, B, 2 * P), F32)),
        name="ssm_step",
    )(ug, h0, bb, lr, li, cy)
    return jnp.transpose(y, (1, 0, 2)).reshape(B, D_SSM), jnp.transpose(h, (1, 0, 2))


def _compress_tables(phi_pe, phi_w1, phi_b1, phi_w2, phi_b2):
    S, H, Dh = CMP_STRIDE, N_KV_HEADS, HEAD_DIM
    w1 = phi_w1.reshape(2, 2, S, Dh, Dh)
    eye_c = jnp.eye(2, dtype=F32)
    eye_h = jnp.eye(H, dtype=F32)
    wbig = jnp.einsum('cajde,xc,yh->jxydache', w1, eye_c, eye_h).reshape(S * 2 * H * Dh, 2 * 2 * H * Dh)
    pe = jnp.transpose(phi_pe.reshape(2, 2, S, Dh), (1, 2, 0, 3))
    pe_rows = jnp.broadcast_to(pe[:, :, :, None, :], (2, S, 2, H, Dh)).reshape(2, S * 2 * H * Dh)
    n = 2 * H * Dh
    pe_w = (jnp.dot(pe_rows[0], wbig[:, :n], precision=HIGHEST) + jnp.dot(pe_rows[1], wbig[:, n:], precision=HIGHEST))
    b1 = jnp.broadcast_to(phi_b1[:, None, :], (2, H, Dh)).reshape(1, n) + pe_w[None, :]
    w2 = jnp.einsum('cef,cx,hy->chexyf', phi_w2, eye_c, eye_h).reshape(n, n)
    b2 = jnp.broadcast_to(phi_b2[:, None, :], (2, H, Dh)).reshape(1, n)
    return wbig.astype(BF16), b1, w2.astype(BF16), b2


def _compress_in_kernel(x_ref, w_ref, z_ref):
    z_ref[0] = jnp.dot(x_ref[0].astype(BF16), w_ref[...], preferred_element_type=F32)


def _compress_in(x2, tables):
    wbig = tables[0]
    N2 = wbig.shape[1]
    B, n, K = x2.shape
    tr = math.gcd(n, 256)
    return pl.pallas_call(
        _compress_in_kernel,
        out_shape=jax.ShapeDtypeStruct((B, n, N2), F32),
        grid=(B, n // tr),
        in_specs=[pl.BlockSpec((1, tr, K), lambda b, i: (b, i, 0)),
                  pl.BlockSpec((K, N2), lambda b, i: (0, 0))],
        out_specs=pl.BlockSpec((1, tr, N2), lambda b, i: (b, i, 0)),
        compiler_params=_cparams("parallel", "parallel"),
        name="compress_in",
    )(x2, wbig)


def _compress_in_paged_kernel(pt_ref, *refs, n_pg):
    x_refs = refs[:n_pg]
    w_ref, z_ref = refs[n_pg:n_pg + 2]
    s_refs = refs[n_pg + 2:]
    for k in range(n_pg):
        t = x_refs[k][0].reshape(D_KV, PAGE_SIZE).T
        for c, s_ref in enumerate(s_refs):
            for n in range(PAGE_SIZE // CMP_STRIDE):
                r0 = (k * (PAGE_SIZE // CMP_STRIDE) + n) * CHUNK_PITCH
                s_ref[r0:r0 + CMP_STRIDE, :] = t[n * CMP_STRIDE:(n + 1) * CMP_STRIDE, c * LANE:(c + 1) * LANE]
    rows = n_pg * PAGE_SIZE // CMP_STRIDE
    z = jnp.zeros((rows, w_ref.shape[1]), F32)
    for j in range(CMP_STRIDE):
        xj = jnp.concatenate([s_ref[pl.ds(j, rows, stride=CHUNK_PITCH), :] for s_ref in s_refs], axis=1)
        z = z + jnp.dot(xj.astype(BF16), w_ref[j * D_KV:(j + 1) * D_KV, :], preferred_element_type=F32)
    z_ref[0] = z


def _compress_in_paged(pool_t, page_table, tables):
    wbig = tables[0]
    N2 = wbig.shape[1]
    K = wbig.shape[0]
    B, n_pages = page_table.shape
    n_pg = math.gcd(n_pages, PAGES_PER_STEP)
    rows = n_pg * PAGE_SIZE // CMP_STRIDE
    page_spec = lambda k: pl.BlockSpec((1,) + pool_t.shape[1:],
                                       lambda b, i, pt, k=k: (pt[b, i * n_pg + k], 0, 0, 0, 0))
    grid_spec = pltpu.PrefetchScalarGridSpec(
        num_scalar_prefetch=1,
        grid=(B, n_pages // n_pg),
        in_specs=[page_spec(k) for k in range(n_pg)] + [pl.BlockSpec((K, N2), lambda b, i, pt: (0, 0))],
        out_specs=pl.BlockSpec((1, rows, N2), lambda b, i, pt: (b, i, 0)),
        scratch_shapes=[pltpu.VMEM((rows * CHUNK_PITCH, LANE), F32) for _ in range(D_KV // LANE)],
    )
    return pl.pallas_call(
        functools.partial(_compress_in_paged_kernel, n_pg=n_pg),
        out_shape=jax.ShapeDtypeStruct((B, n_pages * PAGE_SIZE // CMP_STRIDE, N2), F32),
        grid_spec=grid_spec,
        compiler_params=_cparams("arbitrary", "arbitrary"),
        name="compress_in_paged",
    )(page_table, *([pool_t] * n_pg), wbig)


def _compress_out_kernel(z_ref, b1_ref, w2_ref, b2_ref, o_ref):
    z = z_ref[0]
    n = z.shape[-1] // 2
    rows = z.shape[0]
    second = pltpu.roll(z[:, n:], rows - 1, axis=0)
    hdn = jax.nn.gelu(z[:, :n] + second + b1_ref[...])
    o_ref[0, :rows, :] = jnp.dot(hdn.astype(BF16), w2_ref[...], preferred_element_type=F32) + b2_ref[...]
    if o_ref.shape[1] > rows:
        o_ref[0, rows:, :] = jnp.zeros((o_ref.shape[1] - rows, n), F32)


def _compress_out(z, tables, n_out):
    _, b1, w2, b2 = tables
    B, n, N2 = z.shape
    return pl.pallas_call(
        _compress_out_kernel,
        out_shape=jax.ShapeDtypeStruct((B, n_out, N2 // 2), F32),
        grid=(B,),
        in_specs=[pl.BlockSpec((1, n, N2), lambda b: (b, 0, 0)),
                  pl.BlockSpec((1, N2 // 2), lambda b: (0, 0)),
                  pl.BlockSpec((N2 // 2, N2 // 2), lambda b: (0, 0)),
                  pl.BlockSpec((1, N2 // 2), lambda b: (0, 0))],
        out_specs=pl.BlockSpec((1, n_out, N2 // 2), lambda b: (b, 0, 0)),
        compiler_params=_cparams("parallel"),
        name="compress_out",
    )(z, b1, w2, b2)


def _rel_bucket(dist):
    n = jnp.maximum(dist, 0)
    max_exact = NUM_BUCKETS // 2
    nf = jnp.maximum(n, 1).astype(F32)
    large = max_exact + (jnp.log(nf / max_exact) / math.log(REL_MAX_DIST / max_exact)
                         * (NUM_BUCKETS - max_exact)).astype(jnp.int32)
    large = jnp.minimum(large, NUM_BUCKETS - 1)
    return jnp.where(n < max_exact, n, large)


def _bias_by_distance(rel_bias, n_max):
    onehot = (_rel_bucket(jnp.arange(n_max))[None, :] == jnp.arange(NUM_BUCKETS)[:, None]).astype(F32)
    return jnp.dot(jnp.transpose(rel_bias.astype(F32)), onehot, precision=HIGHEST)


def _shifted_chunks(bias_n, pad, n_chunks, width):
    n = min(bias_n.shape[1], n_chunks * width - pad)
    ext = jnp.concatenate([jnp.broadcast_to(bias_n[:, :1], (N_HEADS, pad)), bias_n[:, :n],
                           jnp.zeros((N_HEADS, n_chunks * width - pad - n), F32)], axis=1)
    return ext.reshape(N_HEADS, n_chunks, width)


def _bias_tables_kernel(ed_ref, ec_ref, tzs_ref, tzw_ref, cmp_ref, *, tq, tk, n_qt):
    n_ds, n_dw, n_j = tzs_ref.shape[1] - 1, tzw_ref.shape[1] - 1, cmp_ref.shape[1] // 8
    tzs_ref[0, n_ds] = jnp.full((tk, tq), NEG, F32)
    tzw_ref[0, n_dw] = jnp.full((tk, tq), NEG, F32)
    w = tq + tk
    c = lax.broadcasted_iota(jnp.int32, (tk, tq), 0)
    r = lax.broadcasted_iota(jnp.int32, (tk, tq), 1)
    for d in range(n_ds):
        v = jnp.concatenate([ed_ref[0, d:d + 1, :], ed_ref[0, d + 1:d + 2, :]], axis=1)
        t = pltpu.roll(jnp.broadcast_to(v, (tk, w)), w - (tk - 1), axis=1, stride=1, stride_axis=0)[:, :tq]
        dist = d * tk + r - c
        tzs_ref[0, d] = jnp.where(dist >= 0, t, NEG)
        if d < n_dw:
            tzw_ref[0, d] = jnp.where((dist >= 0) & (dist <= WINDOW), t, NEG)
    for j in range(n_j):
        dd = n_qt - 1 - j
        c0, c1 = max(dd, 0), max(dd + 1, 0)
        v = jnp.concatenate([ec_ref[0, c0:c0 + 1, :], ec_ref[0, c1:c1 + 1, :]], axis=1)
        t = pltpu.roll(jnp.broadcast_to(v, (8, w)), w - 7 * CMP_STRIDE, axis=1, stride=CMP_STRIDE, stride_axis=0)
        cmp_ref[0, j * 8:(j + 1) * 8, :] = t[:, :tq]


def _bias_tables(bias_n, n_qt, n_rb, n_ds, n_dw, tq, tk):
    assert tq == tk == 8 * CMP_STRIDE and n_dw <= n_ds
    n_j = n_rb + n_qt - 1
    ed = _shifted_chunks(bias_n, tk - 1, n_ds + 1, tq)
    ec = _shifted_chunks(bias_n, 7 * CMP_STRIDE + CMP_BLOCK - 1, n_qt + 1, tq)
    head = lambda a: pl.BlockSpec((1,) + a.shape[1:], lambda h: (h,) + (0,) * (a.ndim - 1))
    outs = (jax.ShapeDtypeStruct((N_HEADS, n_ds + 1, tk, tq), F32),
            jax.ShapeDtypeStruct((N_HEADS, n_dw + 1, tk, tq), F32),
            jax.ShapeDtypeStruct((N_HEADS, n_j * 8, tq), F32))
    tzs, tzw, cmp = pl.pallas_call(
        functools.partial(_bias_tables_kernel, tq=tq, tk=tk, n_qt=n_qt),
        out_shape=outs,
        grid=(N_HEADS,),
        in_specs=[head(ed), head(ec)],
        out_specs=tuple(head(o) for o in outs),
        compiler_params=_cparams("parallel"),
        name="bias_tables",
    )(ed, ec)
    grp = lambda a: a.reshape((N_KV_HEADS, GQA) + a.shape[1:])
    return grp(tzs), grp(tzw), cmp


def _pool_matrix(n_cmp_pad, n_blk_pad):
    r = SEL_BLOCK // CMP_STRIDE
    i = np.arange(n_cmp_pad)[None, :]
    j = np.arange(n_blk_pad)[:, None]
    return ((i >= r * j - 1) & (i <= r * j + r - 1)).astype(np.float32)


def _cmp_select_kernel(q_ref, k_ref, vt_ref, bias_ref, pool_ref, o_ref, sel_ref, *, tq, n_cmp):
    qt = pl.program_id(2)
    n_qt = pl.num_programs(2)
    q = q_ref[0, 0].reshape(GQA * tq, HEAD_DIM)
    k = k_ref[0, 0]
    nc = k.shape[0]
    s = _nt_dot(k, q)
    row0 = pl.multiple_of((n_qt - 1 - qt) * 8, 8)
    s = s + jnp.concatenate([bias_ref[g, pl.ds(row0, nc), :] for g in range(GQA)], axis=-1)
    t_pos = qt * tq + (lax.broadcasted_iota(jnp.int32, (nc, GQA * tq), 1) % tq)
    ci = lax.broadcasted_iota(jnp.int32, (nc, GQA * tq), 0)
    mask = (ci * CMP_STRIDE + CMP_BLOCK - 1 <= t_pos) & (ci < n_cmp)
    s = jnp.where(mask, s, NEG)
    m = jnp.max(s, axis=0, keepdims=True)
    p = jnp.where(mask, jnp.exp(s - m), 0.0)
    p = p / jnp.maximum(jnp.sum(p, axis=0, keepdims=True), 1e-30)
    ot = jnp.dot(vt_ref[0, 0], p.astype(BF16), preferred_element_type=F32)
    o_ref[0] = jnp.concatenate([ot[:, g * tq:(g + 1) * tq].T for g in range(GQA)], axis=-1)
    imp = p[:, 0:tq]
    for g in range(1, GQA):
        imp = imp + p[:, g * tq:(g + 1) * tq]
    sb = jnp.dot(pool_ref[...], imp, precision=HIGHEST, preferred_element_type=F32)
    nb = sb.shape[0]
    blk = lax.broadcasted_iota(jnp.int32, (nb, tq), 0)
    cur = (qt * tq + lax.broadcasted_iota(jnp.int32, (nb, tq), 1)) // SEL_BLOCK
    causal = blk <= cur
    forced = (blk == 0) | (blk == cur) | (blk == cur - 1)
    sc = jnp.where(forced & causal, 1e4, jnp.where(causal, sb, -1.0))
    groups = [sc[r:r + 8] for r in range(0, nb, 8)]
    sub = lax.broadcasted_iota(jnp.int32, (8, tq), 0)
    ranks = [jnp.zeros((8, tq), F32) for _ in groups]
    for i in range(nb):
        row = sc[i:i + 1, :]
        for gi, grp in enumerate(groups):
            if gi * 8 > i:
                ahead = row >= grp
            elif gi * 8 + 7 < i:
                ahead = row > grp
            else:
                ahead = (row > grp) | ((row == grp) & (sub > i - gi * 8))
            ranks[gi] = ranks[gi] + jnp.where(ahead, 1.0, 0.0)
    rank = jnp.concatenate(ranks, axis=0)
    sel_ref[0, 0] = jnp.where((rank < N_SEL) & causal, 0.0, NEG)


def _cmp_select_prompt(q5, kc, vct, bias_tab, pool, n_cmp):
    B, _, _, T, _ = q5.shape
    NC = kc.shape[2]
    NB = pool.shape[0]
    R = bias_tab.shape[1]
    tq = ATT_TQ
    return pl.pallas_call(
        functools.partial(_cmp_select_kernel, tq=tq, n_cmp=n_cmp),
        out_shape=(jax.ShapeDtypeStruct((B, T, D_ATT), F32),
                   jax.ShapeDtypeStruct((B, N_KV_HEADS, NB, T), F32)),
        grid=(B, N_KV_HEADS, T // tq),
        in_specs=[pl.BlockSpec((1, 1, GQA, tq, HEAD_DIM), lambda b, h, i: (b, h, 0, i, 0)),
                  pl.BlockSpec((1, 1, NC, HEAD_DIM), lambda b, h, i: (b, h, 0, 0)),
                  pl.BlockSpec((1, 1, HEAD_DIM, NC), lambda b, h, i: (b, h, 0, 0)),
                  pl.BlockSpec((GQA, R, tq), lambda b, h, i: (h, 0, 0)),
                  pl.BlockSpec((NB, NC), lambda b, h, i: (0, 0))],
        out_specs=(pl.BlockSpec((1, tq, GQA * HEAD_DIM), lambda b, h, i: (b, i, h)),
                   pl.BlockSpec((1, 1, NB, tq), lambda b, h, i: (b, h, 0, i))),
        compiler_params=_cparams("parallel", "parallel", "parallel"),
        name="cmp_select_prompt",
    )(q5, kc, vct, bias_tab, pool)


def _sel_win_kernel(q_ref, ks_ref, vst_ref, kw_ref, vwt_ref, sel_ref, tzs_ref, tzw_ref, os_ref, ow_ref, *, tq):
    tk = ATT_TK
    qt = pl.program_id(2)
    q = q_ref[0, 0].reshape(GQA * tq, HEAD_DIM)
    width = GQA * tq
    per_tile = tk // SEL_BLOCK

    def make_sweep(k_ref, vt_ref, tz_ref, use_sel, n_chains, single_trip):
        n_d = tz_ref.shape[2] - 1

        def scores(kt, hi):
            pad = kt > hi
            kt = jnp.minimum(kt, hi)
            off = pl.multiple_of(kt * tk, tk)
            k = k_ref[0, 0, pl.ds(off, tk), :]
            d = jnp.where(pad, n_d, jnp.minimum(qt - kt, n_d - 1))
            bias = [tz_ref[0, g, d] for g in range(GQA)]
            if use_sel:
                rows = sel_ref[0, 0, pl.ds(kt * per_tile, per_tile), :]
                selb = jnp.concatenate([jnp.broadcast_to(rows[i:i + 1], (SEL_BLOCK, tq))
                                        for i in range(per_tile)], axis=0)
                bias = [b + selb for b in bias]
            return _nt_dot(k, q) + jnp.concatenate(bias, axis=1)

        def values_t(kt, lo, hi):
            off = pl.multiple_of(jnp.clip(kt, lo, hi) * tk, tk)
            return vt_ref[0, 0, :, pl.ds(off, tk)]

        def sweep(lo, hi):
            n_trips = (hi - lo + n_chains) // n_chains
            chain0 = (jnp.full((1, width), 0.5 * NEG, F32), jnp.zeros((1, width), F32),
                      jnp.zeros((HEAD_DIM, width), F32), jnp.ones((1, width), F32), jnp.zeros((tk, width), BF16))

            def trip(i, chains):
                kt = lo + n_chains * i
                pv = [jnp.dot(values_t(kt - n_chains + c, lo, hi), chains[c][4], preferred_element_type=F32)
                      for c in range(n_chains)]
                ss = [scores(kt + c, hi) for c in range(n_chains)]
                out = []
                for c in range(n_chains):
                    m, l, acc, alpha_prev, _ = chains[c]
                    m_new = jnp.maximum(m, jnp.max(ss[c], axis=0, keepdims=True))
                    alpha = jnp.exp(m - m_new)
                    p = jnp.exp(ss[c] - m_new)
                    l = alpha * l + jnp.sum(p, axis=0, keepdims=True)
                    out.append((m_new, l, alpha_prev * acc + pv[c], alpha, p.astype(BF16)))
                return tuple(out)

            if single_trip:
                done = []
                for c in range(n_chains):
                    s = scores(lo + c, hi)
                    m = jnp.maximum(jnp.max(s, axis=0, keepdims=True), 0.5 * NEG)
                    p = jnp.exp(s - m)
                    done.append((m, jnp.sum(p, axis=0, keepdims=True),
                                 jnp.dot(values_t(lo + c, lo, hi), p.astype(BF16), preferred_element_type=F32)))
            else:
                chains = lax.fori_loop(0, n_trips, trip, (chain0,) * n_chains)
                kt_last = lo + n_chains * (n_trips - 1)
                done = []
                for c in range(n_chains):
                    m, l, acc, alpha, p = chains[c]
                    done.append((m, l, alpha * acc + jnp.dot(values_t(kt_last + c, lo, hi), p,
                                                              preferred_element_type=F32)))
            m_all = functools.reduce(jnp.maximum, [m for m, _, _ in done])
            num = den = 0.0
            for m, l, acc in done:
                e = jnp.exp(m - m_all)
                num = num + acc * e
                den = den + l * e
            o = num / jnp.maximum(den, 1e-30)
            return jnp.concatenate([o[:, g * tq:(g + 1) * tq].T for g in range(GQA)], axis=-1)
        return sweep

    n_win = tzw_ref.shape[2] - 1
    os_ref[0] = make_sweep(ks_ref, vst_ref, tzs_ref, True, SEL_CHAINS, False)(0, qt)
    ow_ref[0] = make_sweep(kw_ref, vwt_ref, tzw_ref, False, n_win, True)(jnp.maximum(qt - (n_win - 1), 0), qt)


def _sel_win_prompt(q5, ks, vst, kw, vwt, sel, tzs, tzw):
    B, _, _, T, _ = q5.shape
    NB = sel.shape[2]
    tq = ATT_TQ
    k_spec = pl.BlockSpec((1, 1, T, HEAD_DIM), lambda b, h, i: (b, h, 0, 0))
    vt_spec = pl.BlockSpec((1, 1, HEAD_DIM, T), lambda b, h, i: (b, h, 0, 0))
    tz_spec = lambda tz: pl.BlockSpec((1,) + tz.shape[1:], lambda b, h, i: (h, 0, 0, 0, 0))
    o_spec = pl.BlockSpec((1, tq, GQA * HEAD_DIM), lambda b, h, i: (b, i, h))
    return pl.pallas_call(
        functools.partial(_sel_win_kernel, tq=tq),
        out_shape=(jax.ShapeDtypeStruct((B, T, D_ATT), F32), jax.ShapeDtypeStruct((B, T, D_ATT), F32)),
        grid=(B, N_KV_HEADS, T // tq),
        in_specs=[pl.BlockSpec((1, 1, GQA, tq, HEAD_DIM), lambda b, h, i: (b, h, 0, i, 0)),
                  k_spec, vt_spec, k_spec, vt_spec,
                  pl.BlockSpec((1, 1, NB, tq), lambda b, h, i: (b, h, 0, i)),
                  tz_spec(tzs), tz_spec(tzw)],
        out_specs=(o_spec, o_spec),
        compiler_params=_cparams("parallel", "parallel", "parallel"),
        name="sel_win_prompt",
    )(q5, ks, vst, kw, vwt, sel, tzs, tzw)


def _gate_expand_matrix():
    m = np.zeros((3, 2 * LANE, D_ATT), np.float32)
    for r in range(3):
        for h in range(N_HEADS):
            m[r, h * 3 + r, h * HEAD_DIM:(h + 1) * HEAD_DIM] = 1.0
            m[r, LANE + h * 3 + r, h * HEAD_DIM:(h + 1) * HEAD_DIM] = 1.0
    return m


def _split_bf16(x):
    hi = x.astype(BF16)
    return hi, (x - hi.astype(F32)).astype(BF16)


def _post_mixer_kernel(y_ref, u_ref, oc_ref, os_ref, ow_ref, g_ref, x_ref, gate_ref, sh_ref, sc_ref,
                       dskip_ref, wglu_ref, bglu_ref, gexp_ref, wout_ref, lng_ref, lnb_ref,
                       wr_ref, br_ref, x1_ref, hm_ref, te_ref, tw_ref):
    y = y_ref[0].astype(F32) + dskip_ref[...] * u_ref[0]
    gl = jax.nn.gelu(y)
    ssm = gl * jax.nn.sigmoid(jnp.dot(gl.astype(BF16), wglu_ref[...], preferred_element_type=F32)
                              + bglu_ref[...])
    sg = jnp.concatenate(_split_bf16(jax.nn.sigmoid(g_ref[0])), axis=1)
    att = jnp.zeros_like(oc_ref[0])
    for r, o_ref in enumerate((oc_ref, os_ref, ow_ref)):
        att = att + jnp.dot(sg, gexp_ref[r], preferred_element_type=F32) * o_ref[0]
    h = (jnp.dot(ssm.astype(BF16), wout_ref[:D_SSM, :], preferred_element_type=F32)
         + jnp.dot(att.astype(BF16), wout_ref[D_SSM:, :], preferred_element_type=F32))
    z = DN_ALPHA * x_ref[0] + gate_ref[0] * h
    x1 = _layer_norm(z) * lng_ref[...] + lnb_ref[...]
    x1_ref[0] = x1
    hm = _layer_norm(x1) * (1.0 + sc_ref[0]) + sh_ref[0]
    hm_ref[0] = hm
    hm_hi, hm_lo = _split_bf16(hm)
    logits = (jnp.dot(hm_hi, wr_ref[0], preferred_element_type=F32)
              + jnp.dot(hm_lo, wr_ref[0], preferred_element_type=F32)
              + jnp.dot(hm_hi, wr_ref[1], preferred_element_type=F32)) + br_ref[...]
    lane = lax.broadcasted_iota(jnp.int32, logits.shape, 1)
    work = jnp.where(lane < N_EXPERTS, logits, -jnp.inf)
    te = jnp.zeros(logits.shape, jnp.int32)
    tv = jnp.zeros(logits.shape, F32)
    for k in range(TOP_K):
        best = jnp.max(work, axis=-1, keepdims=True)
        arg = jnp.min(jnp.where(work == best, lane, LANE), axis=-1, keepdims=True)
        te = jnp.where(lane == k, arg, te)
        tv = jnp.where(lane == k, best, tv)
        work = jnp.where(lane == arg, -jnp.inf, work)
    ex = jnp.where(lane < TOP_K, jnp.exp(tv - tv[:, 0:1]), 0.0)
    te_ref[0] = te
    tw_ref[0] = ex / jnp.sum(ex, axis=-1, keepdims=True)


def _post_mixer(y, u, oc, osel, ow, g, x, gate, shift, scale, w, tm):
    B, T, D = x.shape
    R = gate.shape[1]
    rb = 1 if R == 1 else tm
    mod_map = (lambda b, i: (b, 0, 0)) if R == 1 else (lambda b, i: (b, i, 0))
    row = lambda n: pl.BlockSpec((1, tm, n), lambda b, i: (b, i, 0))
    mod = pl.BlockSpec((1, rb, D), mod_map)
    full = lambda a: pl.BlockSpec(a.shape, lambda b, i: (0,) * a.ndim)
    consts = (w['d_skip'], w['w_glu'], w['b_glu'], w['gexp'], w['w_out'], w['ln1_g'], w['ln1_b'],
              w['w_router'], w['b_router'])
    return pl.pallas_call(
        _post_mixer_kernel,
        out_shape=(jax.ShapeDtypeStruct((B, T, D), F32), jax.ShapeDtypeStruct((B, T, D), F32),
                   jax.ShapeDtypeStruct((B, T, LANE), jnp.int32), jax.ShapeDtypeStruct((B, T, LANE), F32)),
        grid=(B, T // tm),
        in_specs=[row(D_SSM), row(D_SSM), row(D_ATT), row(D_ATT), row(D_ATT), row(LANE), row(D),
                  mod, mod, mod] + [full(a) for a in consts],
        out_specs=(row(D), row(D), row(LANE), row(LANE)),
        compiler_params=_cparams("parallel", "parallel"),
        name="post_mixer",
    )(y, u, oc, osel, ow, g, x, gate, shift, scale, *consts)


def _expert_kernel(e_ref, blk_ref, lo_ref, hi_ref, first_ref, x_ref, wgu_ref, bgu_ref, wd_ref, bd_ref, o_ref,
                   wgu_s, wd_s):
    i = pl.program_id(0)
    fresh = (i == 0) | (e_ref[i] != e_ref[jnp.maximum(i - 1, 0)])

    @pl.when(fresh)
    def _():
        wgu_s[...] = wgu_ref[0].astype(BF16)
        wd_s[...] = wd_ref[0].astype(BF16)

    @pl.when(first_ref[i] == 1)
    def _():
        o_ref[...] = jnp.zeros_like(o_ref)

    @pl.when(hi_ref[i] > lo_ref[i])
    def _():
        gu = jnp.dot(x_ref[...].astype(BF16), wgu_s[...], preferred_element_type=F32) + bgu_ref[0]
        gate = jnp.minimum(gu[:, :D_FF], SWIGLU_LIMIT)
        up = jnp.clip(gu[:, D_FF:], -SWIGLU_LIMIT, SWIGLU_LIMIT)
        hh = (up + 1.0) * gate * jax.nn.sigmoid(SWIGLU_ALPHA * gate)
        y = jnp.dot(hh.astype(BF16), wd_s[...], preferred_element_type=F32) + bd_ref[0]
        row = blk_ref[i] * MOE_ROWS + lax.broadcasted_iota(jnp.int32, (MOE_ROWS, 1), 0)
        o_ref[...] = jnp.where((row >= lo_ref[i]) & (row < hi_ref[i]), y, o_ref[...])


def _experts(xb, items, w_gate_up, b_gate_up, w_down, b_down):
    rows, D = xb.shape
    n_items = items[0].shape[0]
    wmap = lambda i, e, blk, lo, hi, first: (e[i], 0, 0)
    rmap = lambda i, e, blk, lo, hi, first: (blk[i], 0)
    grid_spec = pltpu.PrefetchScalarGridSpec(
        num_scalar_prefetch=5,
        grid=(n_items,),
        in_specs=[pl.BlockSpec((MOE_ROWS, D), rmap),
                  pl.BlockSpec((1, D, 2 * D_FF), wmap),
                  pl.BlockSpec((1, 1, 2 * D_FF), wmap),
                  pl.BlockSpec((1, D_FF, D), wmap),
                  pl.BlockSpec((1, 1, D), wmap)],
        out_specs=pl.BlockSpec((MOE_ROWS, D), rmap),
        scratch_shapes=[pltpu.VMEM((D, 2 * D_FF), BF16), pltpu.VMEM((D_FF, D), BF16)],
    )
    return pl.pallas_call(
        _expert_kernel,
        out_shape=jax.ShapeDtypeStruct((rows, D), F32),
        grid_spec=grid_spec,
        compiler_params=_cparams("arbitrary"),
        name="moe_experts",
    )(*items, xb, w_gate_up, b_gate_up.reshape(N_EXPERTS, 1, 2 * D_FF), w_down,
      b_down.reshape(N_EXPERTS, 1, D))


def _moe_dispatch(top_e, n):
    blk = MOE_ROWS
    nk = n * TOP_K
    cb = 128
    assert nk % cb == 0
    e = top_e.reshape(-1)
    onehot = (e[:, None] == jnp.arange(N_EXPERTS)[None, :]).astype(F32)
    oh3 = onehot.reshape(nk // cb, cb, N_EXPERTS)
    tri = jnp.asarray(np.tril(np.ones((cb, cb), np.float32), -1))
    within = jnp.einsum('ij,bje->bie', tri, oh3, precision=HIGHEST)
    blk_tot = jnp.sum(oh3, axis=1)
    blk_off = jnp.cumsum(blk_tot, axis=0) - blk_tot
    counts = jnp.sum(blk_tot, axis=0)
    start = jnp.cumsum(counts) - counts
    dest = jnp.sum((within + blk_off[:, None, :] + start[None, None, :]) * oh3, axis=-1)
    dest = dest.reshape(nk).astype(jnp.int32)
    order = jnp.argsort(dest)
    n_blk = -(-nk // blk)
    row_tok = jnp.concatenate([(order // TOP_K).astype(jnp.int32), jnp.full((n_blk * blk - nk,), n, jnp.int32)])
    counts_i, start_i = counts.astype(jnp.int32), start.astype(jnp.int32)
    first_b = start_i // blk
    last_b = (start_i + counts_i - 1) // blk
    n_it = jnp.where(counts_i > 0, last_b - first_b + 1, 0)
    it_end = jnp.cumsum(n_it)
    it_start = it_end - n_it
    n_items = n_blk + N_EXPERTS - 1
    i = jnp.arange(n_items)
    live = i < it_end[-1]
    it_e = jnp.minimum(jnp.sum(it_end[None, :] <= i[:, None], axis=1), N_EXPERTS - 1)
    it_blk = jnp.where(live, first_b[it_e] + i - it_start[it_e], n_blk - 1)
    it_lo = jnp.where(live, start_i[it_e], 0)
    it_hi = jnp.where(live, start_i[it_e] + counts_i[it_e], 0)
    it_first = jnp.concatenate([jnp.ones((1,), jnp.int32), (it_blk[1:] != it_blk[:-1]).astype(jnp.int32)])
    items = tuple(a.astype(jnp.int32) for a in (it_e, it_blk, it_lo, it_hi, it_first))
    return row_tok, dest.reshape(n, TOP_K), items


def _final_kernel(x_ref, y0_ref, y1_ref, y2_ref, y3_ref, tw_ref, gate_ref, lng_ref, lnb_ref, o_ref):
    tw = tw_ref[0]
    y = jnp.zeros_like(x_ref[0])
    for k, y_ref in enumerate((y0_ref, y1_ref, y2_ref, y3_ref)):
        y = y + tw[:, k:k + 1] * y_ref[0]
    z = DN_ALPHA * x_ref[0] + gate_ref[0] * y
    o_ref[0] = _layer_norm(z) * lng_ref[...] + lnb_ref[...]


def _final(x1, ys, tw, gate, ln_g, ln_b, tm):
    B, T, D = x1.shape
    R = gate.shape[1]
    rb = 1 if R == 1 else tm
    mod_map = (lambda b, i: (b, 0, 0)) if R == 1 else (lambda b, i: (b, i, 0))
    row = lambda n: pl.BlockSpec((1, tm, n), lambda b, i: (b, i, 0))
    vec = pl.BlockSpec((1, D), lambda b, i: (0, 0))
    return pl.pallas_call(
        _final_kernel,
        out_shape=jax.ShapeDtypeStruct((B, T, D), F32),
        grid=(B, T // tm),
        in_specs=[row(D), row(D), row(D), row(D), row(D), row(LANE),
                  pl.BlockSpec((1, rb, D), mod_map), vec, vec],
        out_specs=row(D),
        compiler_params=_cparams("parallel", "parallel"),
        name="moe_combine_ln",
    )(x1, *ys, tw, gate, ln_g, ln_b)


def _cmp_select_step_kernel(q_ref, kv_ref, bias_ref, pool_ref, o_ref, idx_ref, *, n_cmp, n_blk, q_pos):
    q = q_ref[0].astype(BF16)
    ncp = kv_ref.shape[1]
    nbp = pool_ref.shape[1]
    hd = HEAD_DIM
    kv = kv_ref[0]
    kb = [kv[:, h * hd:(h + 1) * hd].astype(BF16) for h in range(N_KV_HEADS)]
    vb = [kv[:, (N_KV_HEADS + h) * hd:(N_KV_HEADS + h + 1) * hd].astype(BF16) for h in range(N_KV_HEADS)]
    row = lax.broadcasted_iota(jnp.int32, (N_HEADS, 1), 0)
    first = row < GQA
    s = jnp.where(first, _nt_dot(q, kb[0]), _nt_dot(q, kb[1])) * (hd ** -0.5)
    s = s + bias_ref[...]
    ci = lax.broadcasted_iota(jnp.int32, (N_HEADS, ncp), 1)
    mask = (ci * CMP_STRIDE + CMP_BLOCK - 1 <= q_pos) & (ci < n_cmp)
    s = jnp.where(mask, s, NEG)
    m = jnp.max(s, axis=-1, keepdims=True)
    p = jnp.where(mask, jnp.exp(s - m), 0.0)
    p = p / jnp.maximum(jnp.sum(p, axis=-1, keepdims=True), 1e-30)
    pb = p.astype(BF16)
    o_ref[0] = jnp.where(first, jnp.dot(pb, vb[0], preferred_element_type=F32),
                         jnp.dot(pb, vb[1], preferred_element_type=F32))
    imp0 = jnp.sum(jnp.where(first, p, 0.0), axis=0, keepdims=True)
    imp1 = jnp.sum(jnp.where(first, 0.0, p), axis=0, keepdims=True)
    imp = jnp.where(first, imp0, imp1)
    sb = jnp.dot(imp, pool_ref[...], precision=HIGHEST, preferred_element_type=F32)
    cur = q_pos // SEL_BLOCK
    bi = lax.broadcasted_iota(jnp.int32, (nbp, nbp), 0)
    bj = lax.broadcasted_iota(jnp.int32, (nbp, nbp), 1)
    blk = lax.broadcasted_iota(jnp.int32, (1, nbp), 1)
    causal = blk <= cur
    forced = (blk == 0) | (blk == cur) | (blk == cur - 1)
    rsel = lax.broadcasted_iota(jnp.int32, (N_SEL, nbp), 0)
    for h in range(N_KV_HEADS):
        sc = jnp.where(forced & causal, 1e4, jnp.where(causal, sb[h * GQA:h * GQA + 1, :], -1.0))
        sc = jnp.where(blk < n_blk, sc, -2.0)
        scb = jnp.broadcast_to(sc, (nbp, nbp))
        col = jnp.sum(jnp.where(bi == bj, scb, 0.0), axis=1, keepdims=True)
        ahead = (col > scb) | ((col == scb) & (bi < bj))
        rank = jnp.sum(ahead.astype(jnp.int32), axis=0, keepdims=True)
        hit = jnp.broadcast_to(rank, (N_SEL, nbp)) == rsel
        idx = jnp.sum(jnp.where(hit, jnp.broadcast_to(blk, (N_SEL, nbp)), 0), axis=1, keepdims=True)
        idx_ref[0, h] = jnp.broadcast_to(idx, (N_SEL, LANE))


def _cmp_select_step(q, ckv, bias, pool, n_cmp, n_blk, q_pos):
    B = q.shape[0]
    NCp = ckv.shape[1]
    return pl.pallas_call(
        functools.partial(_cmp_select_step_kernel, n_cmp=n_cmp, n_blk=n_blk, q_pos=q_pos),
        out_shape=(jax.ShapeDtypeStruct((B, N_HEADS, HEAD_DIM), F32),
                   jax.ShapeDtypeStruct((B, N_KV_HEADS, N_SEL, LANE), jnp.int32)),
        grid=(B,),
        in_specs=[pl.BlockSpec((1, N_HEADS, HEAD_DIM), lambda b: (b, 0, 0)),
                  pl.BlockSpec((1, NCp, D_KV), lambda b: (b, 0, 0)),
                  pl.BlockSpec(bias.shape, lambda b: (0, 0)),
                  pl.BlockSpec(pool.shape, lambda b: (0, 0))],
        out_specs=(pl.BlockSpec((1, N_HEADS, HEAD_DIM), lambda b: (b, 0, 0)),
                   pl.BlockSpec((1, N_KV_HEADS, N_SEL, LANE), lambda b: (b, 0, 0, 0))),
        compiler_params=_cparams("parallel"),
        name="cmp_select_step",
    )(q, ckv, bias, pool)


def _sel_step_kernel(pg_ref, idx_ref, q_ref, *refs, n_past, q_pos):
    page_refs = refs[:N_SEL]
    new_ref, bias_ref, kpos_ref, o_ref = refs[N_SEL:]
    b, h = pl.program_id(0), pl.program_id(1)
    base = (b * N_KV_HEADS + h) * N_SEL
    kts, vts = [], []
    for j in range(N_SEL):
        is_new = idx_ref[base + j] >= n_past
        kts.append(jnp.where(is_new, new_ref[0, 0, 0], page_refs[j][0, 0, 0]))
        vts.append(jnp.where(is_new, new_ref[0, 1, 0], page_refs[j][0, 1, 0]))
    kt = jnp.concatenate(kts, axis=1).astype(BF16)
    vt = jnp.concatenate(vts, axis=1).astype(BF16)
    s = jnp.dot(q_ref[0].astype(BF16), kt, preferred_element_type=F32) * (HEAD_DIM ** -0.5) + bias_ref[0, 0]
    mask = kpos_ref[0, 0] <= q_pos
    s = jnp.where(mask, s, NEG)
    m = jnp.max(s, axis=-1, keepdims=True)
    p = jnp.where(mask, jnp.exp(s - m), 0.0)
    l = jnp.sum(p, axis=-1, keepdims=True)
    o_ref[0, 0] = _nt_dot(p.astype(BF16), vt) / jnp.maximum(l, 1e-30)


def _sel_step(q, pool_t, new_t, bias_sel, kpos, pages, idx_flat, n_past, q_pos):
    B = q.shape[0]
    nk = N_SEL * PAGE_SIZE
    slot = lambda b, h, j: (b * N_KV_HEADS + h) * N_SEL + j
    page_spec = lambda j: pl.BlockSpec((1, 2, 1, HEAD_DIM, PAGE_SIZE),
                                       lambda b, h, pg, ix, j=j: (pg[slot(b, h, j)], 0, h, 0, 0))
    grid_spec = pltpu.PrefetchScalarGridSpec(
        num_scalar_prefetch=2,
        grid=(B, N_KV_HEADS),
        in_specs=[pl.BlockSpec((1, N_HEADS, HEAD_DIM), lambda b, h, pg, ix: (b, 0, 0))]
        + [page_spec(j) for j in range(N_SEL)]
        + [pl.BlockSpec((1, 2, 1, HEAD_DIM, PAGE_SIZE), lambda b, h, pg, ix: (b, 0, h, 0, 0)),
           pl.BlockSpec((1, 1, N_HEADS, nk), lambda b, h, pg, ix: (b, h, 0, 0)),
           pl.BlockSpec((1, 1, 1, nk), lambda b, h, pg, ix: (b, h, 0, 0))],
        out_specs=pl.BlockSpec((1, 1, N_HEADS, HEAD_DIM), lambda b, h, pg, ix: (b, h, 0, 0)),
    )
    return pl.pallas_call(
        functools.partial(_sel_step_kernel, n_past=n_past, q_pos=q_pos),
        out_shape=jax.ShapeDtypeStruct((B, N_KV_HEADS, N_HEADS, HEAD_DIM), F32),
        grid_spec=grid_spec,
        compiler_params=_cparams("arbitrary", "arbitrary"),
        name="sel_step",
    )(pages, idx_flat, q, *([pool_t] * N_SEL), new_t, bias_sel, kpos)


def _win_step_kernel(q_ref, w_ref, new_ref, bias_ref, bias0_ref, o_ref):
    q = q_ref[0]
    qb = q.astype(BF16)
    row = lax.broadcasted_iota(jnp.int32, (N_HEADS, 1), 0)
    first = row < GQA
    w = w_ref[0]
    hd = HEAD_DIM
    kb = [w[:, h * hd:(h + 1) * hd].astype(BF16) for h in range(N_KV_HEADS)]
    vb = [w[:, (N_KV_HEADS + h) * hd:(N_KV_HEADS + h + 1) * hd].astype(BF16) for h in range(N_KV_HEADS)]
    s = jnp.where(first, _nt_dot(qb, kb[0]), _nt_dot(qb, kb[1])) * (hd ** -0.5) + bias_ref[...]
    new = new_ref[0]
    kn = jnp.where(first, new[:, 0:hd], new[:, hd:2 * hd])
    vn = jnp.where(first, new[:, 2 * hd:3 * hd], new[:, 3 * hd:])
    sn = jnp.sum(q * kn, axis=-1, keepdims=True) * (hd ** -0.5) + bias0_ref[...]
    m = jnp.maximum(jnp.max(s, axis=-1, keepdims=True), sn)
    p = jnp.exp(s - m)
    pn = jnp.exp(sn - m)
    l = jnp.sum(p, axis=-1, keepdims=True) + pn
    pb = p.astype(BF16)
    acc = jnp.where(first, jnp.dot(pb, vb[0], preferred_element_type=F32),
                    jnp.dot(pb, vb[1], preferred_element_type=F32)) + pn * vn
    o_ref[0] = acc / jnp.maximum(l, 1e-30)


def _win_step(q, win, new, bias, bias0):
    B, W, _ = win.shape
    return pl.pallas_call(
        _win_step_kernel,
        out_shape=jax.ShapeDtypeStruct((B, N_HEADS, HEAD_DIM), F32),
        grid=(B,),
        in_specs=[pl.BlockSpec((1, N_HEADS, HEAD_DIM), lambda b: (b, 0, 0)),
                  pl.BlockSpec((1, W, D_KV), lambda b: (b, 0, 0)),
                  pl.BlockSpec((1, 1, D_KV), lambda b: (b, 0, 0)),
                  pl.BlockSpec((N_HEADS, W), lambda b: (0, 0)),
                  pl.BlockSpec((N_HEADS, 1), lambda b: (0, 0))],
        out_specs=pl.BlockSpec((1, N_HEADS, HEAD_DIM), lambda b: (b, 0, 0)),
        compiler_params=_cparams("parallel"),
        name="win_step",
    )(q, win, new, bias, bias0)


def _split_heads(kv, dtype):
    B, L, _ = kv.shape
    kv5 = kv.reshape(B, L, 2, N_KV_HEADS, HEAD_DIM)
    return (jnp.transpose(kv5[:, :, 0], (0, 2, 1, 3)).astype(dtype),
            jnp.transpose(kv5[:, :, 1], (0, 2, 1, 3)).astype(dtype))


def _nsa_prompt(q5, kvc, ks, vst, kw, vwt, cmp_tab, rel_bias):
    B, T, _ = kvc.shape
    nc = T // CMP_STRIDE
    nb = T // SEL_BLOCK
    ckv = _compress_out(_compress_in(kvc.reshape(B, nc, CMP_STRIDE * D_KV), cmp_tab), cmp_tab, nc)
    kc, vc = _split_heads(ckv, BF16)
    vct = jnp.transpose(vc, (0, 1, 3, 2))
    bias_n = _bias_by_distance(rel_bias, T)
    n_qt, n_kt = T // ATT_TQ, T // ATT_TK
    n_ds = min(n_kt, -(-(REL_MAX_DIST + ATT_TK - 1) // ATT_TK) + 1)
    n_dw = min(n_kt, WINDOW // ATT_TK + 1)
    tzs, tzw, bias_tab = _bias_tables(bias_n, n_qt, nc // 8, n_ds, n_dw, ATT_TQ, ATT_TK)
    pool = jnp.asarray(_pool_matrix(nc, nb))
    o_cmp, sel = _cmp_select_prompt(q5, kc, vct, bias_tab, pool, nc - 1)
    o_sel, o_win = _sel_win_prompt(q5, ks, vst, kw, vwt, sel, tzs, tzw)
    return o_cmp, o_sel, o_win


def _nsa_sample(q, kvc, kvs, kvw, pool_cmp, pool_sel, win_buf, page_table, cmp_tab, rel_bias):
    B = q.shape[0]
    n_pages = page_table.shape[1]
    past_len = n_pages * PAGE_SIZE
    q_pos = past_len
    lp = -(-(past_len + 1) // SEL_BLOCK) * SEL_BLOCK
    n_cmp = lp // CMP_STRIDE - 1
    n_blk = lp // SEL_BLOCK
    n_past_chunks = past_len // CMP_STRIDE
    n_tail = 8
    assert n_past_chunks + n_tail >= n_cmp + 1
    n_chunks = n_past_chunks + n_tail
    feature_major = lambda pool: jnp.transpose(pool, (0, 2, 3, 4, 1))
    z_past = _compress_in_paged(feature_major(pool_cmp), page_table, cmp_tab)
    tail = jnp.pad(kvc[:, None, :], ((0, 0), (0, n_tail * CMP_STRIDE - 1), (0, 0)))
    z_tail = _compress_in(tail.reshape(B, n_tail, CMP_STRIDE * D_KV), cmp_tab)
    ncp = -(-n_chunks // LANE) * LANE
    nbp = -(-n_blk // LANE) * LANE
    ckv = _compress_out(jnp.concatenate([z_past, z_tail], axis=1), cmp_tab, ncp)
    bias_n = _bias_by_distance(rel_bias, q_pos + 1)
    n_back = max((n_pages + 1) * PAGE_SIZE, ncp * CMP_STRIDE + CMP_BLOCK)
    back = jnp.concatenate([bias_n[:, ::-1], jnp.broadcast_to(bias_n[:, :1], (N_HEADS, n_back - q_pos - 1))], 1)
    bias_c = back[:, CMP_BLOCK - 1:CMP_BLOCK - 1 + ncp * CMP_STRIDE:CMP_STRIDE]
    pool = jnp.asarray(_pool_matrix(ncp, nbp).T)
    q3 = q.reshape(B, N_HEADS, HEAD_DIM)
    o_cmp, idx = _cmp_select_step(q3, ckv, bias_c, pool, n_cmp, n_blk, q_pos)
    idx = idx[..., 0]
    bpp = PAGE_SIZE // SEL_BLOCK
    n_past = n_pages * bpp
    lpage = idx // bpp
    pages = jnp.take_along_axis(page_table, jnp.minimum(lpage, n_pages - 1).reshape(B, -1), axis=1)
    new_t = jnp.pad(kvs.reshape(B, 2, N_KV_HEADS, HEAD_DIM, 1), ((0, 0),) * 4 + ((0, PAGE_SIZE - 1),))
    bias_page = jnp.transpose(back[:, :(n_pages + 1) * PAGE_SIZE].reshape(N_HEADS, n_pages + 1, PAGE_SIZE),
                              (1, 0, 2))
    bias_sel = jnp.transpose(bias_page[lpage], (0, 1, 3, 2, 4)).reshape(B, N_KV_HEADS, N_HEADS, -1)
    kpos = lpage[..., None] * PAGE_SIZE + jnp.arange(PAGE_SIZE)
    ok = (kpos // SEL_BLOCK == idx[..., None]) & (idx <= q_pos // SEL_BLOCK)[..., None]
    kpos = jnp.where(ok, kpos, q_pos + 1).reshape(B, N_KV_HEADS, 1, -1).astype(jnp.int32)
    o_sel = _sel_step(q3, feature_major(pool_sel), new_t, bias_sel, kpos, pages.reshape(-1).astype(jnp.int32),
                      idx.reshape(-1).astype(jnp.int32), n_past, q_pos)
    o_sel = jnp.concatenate([o_sel[:, h, h * GQA:(h + 1) * GQA] for h in range(N_KV_HEADS)], axis=1)
    wb = win_buf.shape[1]
    bias_w = bias_n[:, 1:wb + 1][:, ::-1]
    o_win = _win_step(q3, win_buf.reshape(B, wb, D_KV), kvw[:, None, :], bias_w, bias_n[:, 0:1])
    return o_cmp.reshape(B, D_ATT), o_sel.reshape(B, D_ATT), o_win.reshape(B, D_ATT)


def kernel(x_prompt, x_sample, cache_cmp_kv, cache_sel_kv, state_win_kv, state_ssm_re, state_ssm_im, page_table,
           c_prompt, c_sample, w_ada, b_ada, w_in, lam_re, lam_im, log_dt, b_re, b_im, c_re, c_im, d_skip,
           w_glu, b_glu, phi_pe, phi_w1, phi_b1, phi_w2, phi_b2, rel_bias, w_out, ln1_g, ln1_b,
           w_router, b_router, w_gate_up, b_gate_up, w_down, b_down, ln2_g, ln2_b):
    assert w_ada.shape[0] == DEPTH == 1
    l = 0
    Bp, T, D = x_prompt.shape
    Bs = x_sample.shape[0]
    kv_tail = (2, N_KV_HEADS, HEAD_DIM)

    n_c = Bp + Bs
    c_all = jnp.pad(jnp.concatenate([c_prompt, c_sample], 0), ((0, -n_c % 8), (0, 0)))
    m_all = _adaln(c_all, w_ada[l], b_ada[l])
    m_p = m_all[:Bp].reshape(Bp, 6, D)
    m_s = m_all[Bp:n_c].reshape(Bs, 6, D)
    mod_p = [m_p[:, i:i + 1, :] for i in range(6)]
    mod_s = [m_s[None, :, i, :] for i in range(6)]

    w_in_pad = jnp.pad(w_in[l], ((0, 0), (0, D_IN_PAD - D_IN))).astype(BF16)
    n_levels = max(1, int(math.log2(T // SSM_CHUNK)))
    ssm_tab = _ssm_tables(lam_re[l], lam_im[l], log_dt[l], b_re[l], b_im[l], c_re[l], c_im[l],
                          SSM_CHUNK, n_levels)
    cmp_tab = _compress_tables(phi_pe[l], phi_w1[l], phi_b1[l], phi_w2[l], phi_b2[l])
    w_post = dict(
        d_skip=d_skip[l].reshape(1, D_SSM), w_glu=w_glu[l].astype(BF16), b_glu=b_glu[l].reshape(1, D_SSM),
        gexp=jnp.asarray(_gate_expand_matrix(), dtype=BF16), w_out=w_out[l].astype(BF16),
        ln1_g=ln1_g[l].reshape(1, D), ln1_b=ln1_b[l].reshape(1, D),
        w_router=jnp.stack(_split_bf16(jnp.pad(w_router[l], ((0, 0), (0, LANE - N_EXPERTS))))),
        b_router=jnp.pad(b_router[l], (0, LANE - N_EXPERTS)).reshape(1, LANE))

    u, q5, kvc, kvs, kvw, g, ks, vst, kw, vwt = _mixer_in(x_prompt, mod_p[0], mod_p[1], w_in_pad, 512, True)
    y_ssm, h_p = _ssm_prompt(u, ssm_tab)
    o_cmp, o_sel, o_win = _nsa_prompt(q5, kvc, ks, vst, kw, vwt, cmp_tab, rel_bias)
    x1_p, hm_p, te_p, tw_p = _post_mixer(y_ssm, u, o_cmp, o_sel, o_win, g, x_prompt,
                                         mod_p[2], mod_p[3], mod_p[4], w_post, tm=256)

    u_s, q_s, kvc_s, kvs_s, kvw_s, g_s = _mixer_in(x_sample.reshape(1, Bs, D), mod_s[0], mod_s[1],
                                                   w_in_pad, Bs, False)
    y_s, h_s = _ssm_sample(u_s[0], state_ssm_re[l], state_ssm_im[l], ssm_tab, c_re[l], c_im[l])
    oc_s, os_s, ow_s = _nsa_sample(q_s[0].astype(F32), kvc_s[0], kvs_s[0], kvw_s[0], cache_cmp_kv[l],
                                   cache_sel_kv[l], state_win_kv[l], page_table, cmp_tab, rel_bias)
    x1_s, hm_s, te_s, tw_s = _post_mixer(y_s[None], u_s, oc_s[None], os_s[None], ow_s[None], g_s,
                                         x_sample.reshape(1, Bs, D), mod_s[2], mod_s[3], mod_s[4],
                                         w_post, tm=Bs)

    n_p = Bp * T
    n_all = n_p + Bs
    hm_all = jnp.concatenate([hm_p.reshape(n_p, D), hm_s.reshape(Bs, D)], 0)
    te_all = jnp.concatenate([te_p.reshape(n_p, LANE), te_s.reshape(Bs, LANE)], 0)[:, :TOP_K]
    row_tok, dest, items = _moe_dispatch(te_all, n_all)
    xb = jnp.concatenate([hm_all, jnp.zeros((1, D), F32)], 0)[row_tok]
    yb = _experts(xb, items, w_gate_up[l], b_gate_up[l], w_down[l], b_down[l])
    ys_p = [yb[dest[:n_p, k]].reshape(Bp, T, D) for k in range(TOP_K)]
    ys_s = [yb[dest[n_p:, k]].reshape(1, Bs, D) for k in range(TOP_K)]
    ln2g, ln2b = ln2_g[l].reshape(1, D), ln2_b[l].reshape(1, D)
    out_p = _final(x1_p, ys_p, tw_p, mod_p[5], ln2g, ln2b, tm=512)
    out_s = _final(x1_s, ys_s, tw_s, mod_s[5], ln2g, ln2b, tm=Bs)

    wlen = min(WINDOW, T)
    win_s = jnp.concatenate([state_win_kv[l], kvw_s[0].reshape(Bs, 1, *kv_tail)], 1)[:, -state_win_kv.shape[2]:]
    p_state = SSM_STATE
    return (out_p, out_s.reshape(Bs, 1, D),
            kvc.reshape(1, Bp, T, *kv_tail), kvc_s[0].reshape(1, Bs, 1, *kv_tail),
            kvs.reshape(1, Bp, T, *kv_tail), kvs_s[0].reshape(1, Bs, 1, *kv_tail),
            kvw[:, T - wlen:].reshape(1, Bp, wlen, *kv_tail), win_s[None],
            h_p[None, ..., :p_state], h_p[None, ..., p_state:],
            h_s[None, ..., :p_state], h_s[None, ..., p_state:])
```

```python
import functools
import math

import numpy as np
import jax
import jax.numpy as jnp
from jax import lax
from jax.experimental import pallas as pl
from jax.experimental.pallas import tpu as pltpu

D_MODEL = 1024
DEPTH = 1
PAST_LEN = 16384
PAGE_SIZE = 128
D_SSM = 512
SSM_GROUP = 16
N_SSM_GROUPS = D_SSM // SSM_GROUP
SSM_STATE = 64
N_HEADS = 8
HEAD_DIM = 64
N_KV_HEADS = 2
GQA = N_HEADS // N_KV_HEADS
D_ATT = N_HEADS * HEAD_DIM
D_KV = 2 * N_KV_HEADS * HEAD_DIM
CMP_STRIDE = 16
CMP_BLOCK = 2 * CMP_STRIDE
SEL_BLOCK = 64
N_SEL = 16
WINDOW = 512
NUM_BUCKETS = 32
REL_MAX_DIST = 1024
N_EXPERTS = 32
TOP_K = 4
D_FF = 1024
SWIGLU_LIMIT = 7.0
SWIGLU_ALPHA = 1.702
DN_ALPHA = (2 * DEPTH) ** 0.25
D_IN = D_SSM + D_ATT + 3 * D_KV + 3 * N_HEADS
NEG = -1e30
F32 = jnp.float32
BF16 = jnp.bfloat16
HIGHEST = lax.Precision.HIGHEST

LANE = 128
D_IN_PAD = 1920
GATE_COL = D_SSM + D_ATT + 3 * D_KV
SSM_CHUNK = 8
ATT_TQ = 128
ATT_TK = 128
SEL_CHAINS = 4
MOE_ROWS = 256
PAGES_PER_STEP = 32
CHUNK_PITCH = 24
VMEM_LIMIT = 48 * 1024 * 1024
LN_EPS = 1e-5


def _cparams(*sem):
    return pltpu.CompilerParams(dimension_semantics=sem, vmem_limit_bytes=VMEM_LIMIT)


def _nt_dot(a, b):
    return lax.dot_general(a, b, (((1,), (1,)), ((), ())), preferred_element_type=F32)


def _layer_norm(x):
    mu = jnp.mean(x, axis=-1, keepdims=True)
    xc = x - mu
    var = jnp.mean(xc * xc, axis=-1, keepdims=True)
    return xc * lax.rsqrt(var + LN_EPS)


def _adaln_kernel(c_ref, w_ref, b_ref, o_ref):
    c = c_ref[...]
    s = c * jax.nn.sigmoid(c)
    o_ref[...] = jnp.dot(s, w_ref[...], precision=HIGHEST, preferred_element_type=F32) + b_ref[...]


def _adaln(c, w, b):
    n, d = c.shape
    dout = w.shape[1]
    tn = 1024
    return pl.pallas_call(
        _adaln_kernel,
        out_shape=jax.ShapeDtypeStruct((n, dout), F32),
        grid=(dout // tn,),
        in_specs=[pl.BlockSpec((n, d), lambda j: (0, 0)),
                  pl.BlockSpec((d, tn), lambda j: (0, j)),
                  pl.BlockSpec((1, tn), lambda j: (0, j))],
        out_specs=pl.BlockSpec((n, tn), lambda j: (0, j)),
        compiler_params=_cparams("arbitrary"),
        name="adaln",
    )(c, w, b.reshape(1, dout))


def _mixer_in_kernel(x_ref, sh_ref, sc_ref, w_ref, u_ref, q_ref, kvc_ref, kvs_ref, kvw_ref, g_ref, *att_refs):
    h = _layer_norm(x_ref[0]) * (1.0 + sc_ref[0]) + sh_ref[0]
    z = jnp.dot(h.astype(BF16), w_ref[...], preferred_element_type=F32)
    c0 = D_SSM
    c1 = c0 + D_ATT
    c2 = c1 + D_KV
    c3 = c2 + D_KV
    c4 = c3 + D_KV
    u_ref[0] = z[:, :c0]
    kvc_ref[0] = z[:, c1:c2]
    kvs_ref[0] = z[:, c2:c3]
    kvw_ref[0] = z[:, c3:c4]
    g_ref[0] = z[:, c4:c4 + LANE]
    if not att_refs:
        q_ref[0] = z[:, c0:c1].astype(BF16)
        return
    ks_ref, vst_ref, kw_ref, vwt_ref = att_refs
    hd, half = HEAD_DIM, N_KV_HEADS * HEAD_DIM
    for hq in range(N_HEADS):
        q_ref[0, hq // GQA, hq % GQA] = (z[:, c0 + hq * hd:c0 + (hq + 1) * hd] * (hd ** -0.5)).astype(BF16)
    for k_ref, vt_ref, base in ((ks_ref, vst_ref, c2), (kw_ref, vwt_ref, c3)):
        for hk in range(N_KV_HEADS):
            k_ref[0, hk] = z[:, base + hk * hd:base + (hk + 1) * hd].astype(BF16)
        vt = z[:, base + half:base + 2 * half].T
        vt_ref[0] = vt.reshape(N_KV_HEADS, hd, vt.shape[1]).astype(BF16)


def _mixer_in(x, shift, scale, w_pad, tm, attention_layouts):
    B, T, D = x.shape
    R = shift.shape[1]
    rb = 1 if R == 1 else tm
    mod_map = (lambda b, i: (b, 0, 0)) if R == 1 else (lambda b, i: (b, i, 0))
    row = lambda n: pl.BlockSpec((1, tm, n), lambda b, i: (b, i, 0))
    f32 = lambda n: jax.ShapeDtypeStruct((B, T, n), F32)
    if attention_layouts:
        q_shape = jax.ShapeDtypeStruct((B, N_KV_HEADS, GQA, T, HEAD_DIM), BF16)
        q_spec = pl.BlockSpec((1, N_KV_HEADS, GQA, tm, HEAD_DIM), lambda b, i: (b, 0, 0, i, 0))
        k_shape = jax.ShapeDtypeStruct((B, N_KV_HEADS, T, HEAD_DIM), BF16)
        k_spec = pl.BlockSpec((1, N_KV_HEADS, tm, HEAD_DIM), lambda b, i: (b, 0, i, 0))
        vt_shape = jax.ShapeDtypeStruct((B, N_KV_HEADS, HEAD_DIM, T), BF16)
        vt_spec = pl.BlockSpec((1, N_KV_HEADS, HEAD_DIM, tm), lambda b, i: (b, 0, 0, i))
        extra_shapes, extra_specs = (k_shape, vt_shape, k_shape, vt_shape), (k_spec, vt_spec, k_spec, vt_spec)
    else:
        q_shape, q_spec = jax.ShapeDtypeStruct((B, T, D_ATT), BF16), row(D_ATT)
        extra_shapes, extra_specs = (), ()
    return pl.pallas_call(
        _mixer_in_kernel,
        out_shape=(f32(D_SSM), q_shape, f32(D_KV), f32(D_KV), f32(D_KV), f32(LANE)) + extra_shapes,
        grid=(B, T // tm),
        in_specs=[row(D), pl.BlockSpec((1, rb, D), mod_map), pl.BlockSpec((1, rb, D), mod_map),
                  pl.BlockSpec((D, D_IN_PAD), lambda b, i: (0, 0))],
        out_specs=(row(D_SSM), q_spec, row(D_KV), row(D_KV), row(D_KV), row(LANE)) + extra_specs,
        compiler_params=_cparams("parallel", "parallel"),
        name="mixer_in",
    )(x, shift, scale, w_pad)


def _ssm_tables(lam_re, lam_im, log_dt, b_re, b_im, c_re, c_im, L, n_levels):
    G, P = lam_re.shape
    C = b_re.shape[-1]
    dt = jnp.exp(log_dt.astype(F32))[:, None]
    er, ei = lam_re * dt, lam_im * dt

    def power(k):
        kk = k.astype(F32)[:, None, None]
        mag = jnp.exp(kk * er)
        return mag * jnp.cos(kk * ei), mag * jnp.sin(kk * ei)

    lb_re, lb_im = power(jnp.ones((1,), F32))
    nr, ni = lb_re[0] - 1.0, lb_im[0]
    den = lam_re * lam_re + lam_im * lam_im
    fr = (nr * lam_re + ni * lam_im) / den
    fi = (ni * lam_re - nr * lam_im) / den
    bbr = fr[:, :, None] * b_re - fi[:, :, None] * b_im
    bbi = fr[:, :, None] * b_im + fi[:, :, None] * b_re
    pr, pi = power(jnp.arange(L + 1))
    clr = c_re[None] * pr[:, :, None, :] - c_im[None] * pi[:, :, None, :]
    cli = c_re[None] * pi[:, :, None, :] + c_im[None] * pr[:, :, None, :]
    kern = (jnp.einsum('kgcp,gpd->kgcd', clr[:L], bbr, precision=HIGHEST)
            - jnp.einsum('kgcp,gpd->kgcd', cli[:L], bbi, precision=HIGHEST))
    GP = LANE // C
    X = G // GP
    eye = jnp.eye(GP, dtype=F32)
    ts = np.arange(L)
    place = jnp.asarray((ts[None, :, None] - ts[:, None, None] == ts[None, None, :]).astype(np.float32))
    toep = jnp.einsum('stk,kxhcd,hj->xshdtjc', place, kern.reshape(L, X, GP, C, C), eye, precision=HIGHEST)
    toep = toep.reshape(X, L * LANE, L * LANE)
    prr, pir = pr[:L][::-1], pi[:L][::-1]
    ws2 = jnp.stack([prr[..., None] * bbr[None] - pir[..., None] * bbi[None],
                     prr[..., None] * bbi[None] + pir[..., None] * bbr[None]])
    ws = jnp.einsum('rsxhpd,hj->xshdrjp', ws2.reshape(2, L, X, GP, P, C), eye, precision=HIGHEST)
    ws = ws.reshape(X, L * LANE, 2 * GP * P)
    wy2 = jnp.stack([clr[1:], -cli[1:]])
    wy = jnp.einsum('rtxhcp,hj->xrhptjc', wy2.reshape(2, L, X, GP, C, P), eye, precision=HIGHEST)
    wy = wy.reshape(X, 2 * GP * P, L * LANE)
    lr, li = power(L * (2 ** jnp.arange(n_levels)))
    lr, li = lr.reshape(n_levels, X, GP * P), li.reshape(n_levels, X, GP * P)
    ar = jnp.transpose(jnp.concatenate([lr, lr], -1), (1, 0, 2))
    ai = jnp.transpose(jnp.concatenate([-li, li], -1), (1, 0, 2))
    return toep.astype(BF16), ws.astype(BF16), wy.astype(BF16), ar, ai, (lb_re[0], lb_im[0], bbr, bbi)


def _ssm_kernel(u_ref, toep_ref, ws_ref, wy_ref, ar_ref, ai_ref, y_ref, hl_ref, *, L, nc, n_levels):
    u = jnp.concatenate([u_ref[0, pl.ds(t, nc, stride=L), :] for t in range(L)], axis=1).astype(BF16)
    y1 = jnp.dot(u, toep_ref[0], preferred_element_type=F32)
    h = jnp.dot(u, ws_ref[0], preferred_element_type=F32)
    w2 = h.shape[-1]
    rows = lax.broadcasted_iota(jnp.int32, (nc, w2), 0)
    for k in range(n_levels):
        d = 1 << k
        sh = jnp.where(rows >= d, pltpu.roll(h, d, axis=0), 0.0)
        sw = pltpu.roll(sh, w2 // 2, axis=1)
        h = h + ar_ref[0, k:k + 1, :] * sh + ai_ref[0, k:k + 1, :] * sw
    hl_ref[0, 0] = h[nc - 1:nc, :]
    hp = jnp.where(rows >= 1, pltpu.roll(h, 1, axis=0), 0.0)
    y = y1 + jnp.dot(hp.astype(BF16), wy_ref[0], preferred_element_type=F32)
    for t in range(L):
        y_ref[0, pl.ds(t, nc, stride=L), :] = y[:, t * LANE:(t + 1) * LANE]


def _ssm_prompt(u, tables):
    toep, ws, wy, ar, ai, _ = tables
    B, T, _ = u.shape
    L, P = SSM_CHUNK, SSM_STATE
    X, n_levels, w2 = ar.shape
    GP = w2 // (2 * P)
    nc = T // L
    tab = lambda a: pl.BlockSpec((1,) + a.shape[1:], lambda x, b: (x, 0, 0))
    seq = pl.BlockSpec((1, T, LANE), lambda x, b: (b, 0, x))
    y, hl = pl.pallas_call(
        functools.partial(_ssm_kernel, L=L, nc=nc, n_levels=n_levels),
        out_shape=(jax.ShapeDtypeStruct((B, T, D_SSM), F32), jax.ShapeDtypeStruct((X, B, 1, w2), F32)),
        grid=(X, B),
        in_specs=[seq, tab(toep), tab(ws), tab(wy), tab(ar), tab(ai)],
        out_specs=(seq, pl.BlockSpec((1, 1, 1, w2), lambda x, b: (x, b, 0, 0))),
        compiler_params=_cparams("parallel", "parallel"),
        name="ssm_prompt",
    )(u, toep, ws, wy, ar, ai)
    hl = jnp.transpose(hl.reshape(X, B, 2, GP, P), (1, 0, 3, 2, 4))
    return y, hl.reshape(B, X * GP, 2 * P)


def _ssm_step_kernel(u_ref, h0_ref, bb_ref, lr_ref, li_ref, cy_ref, y_ref, h_ref):
    p = lr_ref.shape[-1] // 2
    bu = jnp.einsum('gbc,gcp->gbp', u_ref[...], bb_ref[...], preferred_element_type=F32)
    h0 = h0_ref[...]
    h0s = jnp.concatenate([h0[..., p:], h0[..., :p]], axis=-1)
    h = lr_ref[...] * h0 + li_ref[...] * h0s + bu
    h_ref[...] = h
    y_ref[...] = jnp.einsum('gbp,gpc->gbc', h.astype(BF16), cy_ref[...], preferred_element_type=F32)


def _ssm_sample(u, h0_re, h0_im, tables, c_re, c_im):
    lb_re, lb_im, bbr, bbi = tables[-1]
    B = u.shape[0]
    G, C, P = N_SSM_GROUPS, SSM_GROUP, SSM_STATE
    ug = jnp.transpose(u.reshape(B, G, C), (1, 0, 2)).astype(BF16)
    h0 = jnp.transpose(jnp.concatenate([h0_re, h0_im], -1), (1, 0, 2)).astype(F32)
    bb = jnp.concatenate([jnp.transpose(bbr, (0, 2, 1)), jnp.transpose(bbi, (0, 2, 1))], -1).astype(BF16)
    lr = jnp.concatenate([lb_re, lb_re], -1)[:, None, :]
    li = jnp.concatenate([-lb_im, lb_im], -1)[:, None, :]
    cy = jnp.concatenate([jnp.transpose(c_re, (0, 2, 1)), -jnp.transpose(c_im, (0, 2, 1))], 1).astype(BF16)
    y, h = pl.pallas_call(
        _ssm_step_kernel,
        out_shape=(jax.ShapeDtypeStruct((G, B, C), F32), jax.ShapeDtypeStruct((G, B, 2 * P), F32)),
        name="ssm_step",
    )(ug, h0, bb, lr, li, cy)
    return jnp.transpose(y, (1, 0, 2)).reshape(B, D_SSM), jnp.transpose(h, (1, 0, 2))


def _compress_tables(phi_pe, phi_w1, phi_b1, phi_w2, phi_b2):
    S, H, Dh = CMP_STRIDE, N_KV_HEADS, HEAD_DIM
    w1 = phi_w1.reshape(2, 2, S, Dh, Dh)
    eye_c = jnp.eye(2, dtype=F32)
    eye_h = jnp.eye(H, dtype=F32)
    wbig = jnp.einsum('cajde,xc,yh->jxydache', w1, eye_c, eye_h).reshape(S * 2 * H * Dh, 2 * 2 * H * Dh)
    pe = jnp.transpose(phi_pe.reshape(2, 2, S, Dh), (1, 2, 0, 3))
    pe_rows = jnp.broadcast_to(pe[:, :, :, None, :], (2, S, 2, H, Dh)).reshape(2, S * 2 * H * Dh)
    n = 2 * H * Dh
    pe_w = (jnp.dot(pe_rows[0], wbig[:, :n], precision=HIGHEST) + jnp.dot(pe_rows[1], wbig[:, n:], precision=HIGHEST))
    b1 = jnp.broadcast_to(phi_b1[:, None, :], (2, H, Dh)).reshape(1, n) + pe_w[None, :]
    w2 = jnp.einsum('cef,cx,hy->chexyf', phi_w2, eye_c, eye_h).reshape(n, n)
    b2 = jnp.broadcast_to(phi_b2[:, None, :], (2, H, Dh)).reshape(1, n)
    return wbig.astype(BF16), b1, w2.astype(BF16), b2


def _compress_in_kernel(x_ref, w_ref, z_ref):
    z_ref[0] = jnp.dot(x_ref[0].astype(BF16), w_ref[...], preferred_element_type=F32)


def _compress_in(x2, tables):
    wbig = tables[0]
    N2 = wbig.shape[1]
    B, n, K = x2.shape
    tr = math.gcd(n, 256)
    return pl.pallas_call(
        _compress_in_kernel,
        out_shape=jax.ShapeDtypeStruct((B, n, N2), F32),
        grid=(B, n // tr),
        in_specs=[pl.BlockSpec((1, tr, K), lambda b, i: (b, i, 0)),
                  pl.BlockSpec((K, N2), lambda b, i: (0, 0))],
        out_specs=pl.BlockSpec((1, tr, N2), lambda b, i: (b, i, 0)),
        compiler_params=_cparams("parallel", "parallel"),
        name="compress_in",
    )(x2, wbig)


def _compress_in_paged_kernel(pt_ref, *refs, n_pg):
    x_refs = refs[:n_pg]
    w_ref, z_ref = refs[n_pg:n_pg + 2]
    s_refs = refs[n_pg + 2:]
    for k in range(n_pg):
        t = x_refs[k][0].reshape(D_KV, PAGE_SIZE).T
        for c, s_ref in enumerate(s_refs):
            for n in range(PAGE_SIZE // CMP_STRIDE):
                r0 = (k * (PAGE_SIZE // CMP_STRIDE) + n) * CHUNK_PITCH
                s_ref[r0:r0 + CMP_STRIDE, :] = t[n * CMP_STRIDE:(n + 1) * CMP_STRIDE, c * LANE:(c + 1) * LANE]
    rows = n_pg * PAGE_SIZE // CMP_STRIDE
    z = jnp.zeros((rows, w_ref.shape[1]), F32)
    for j in range(CMP_STRIDE):
        xj = jnp.concatenate([s_ref[pl.ds(j, rows, stride=CHUNK_PITCH), :] for s_ref in s_refs], axis=1)
        z = z + jnp.dot(xj.astype(BF16), w_ref[j * D_KV:(j + 1) * D_KV, :], preferred_element_type=F32)
    z_ref[0] = z


def _compress_in_paged(pool_t, page_table, tables):
    wbig = tables[0]
    N2 = wbig.shape[1]
    K = wbig.shape[0]
    B, n_pages = page_table.shape
    n_pg = math.gcd(n_pages, PAGES_PER_STEP)
    rows = n_pg * PAGE_SIZE // CMP_STRIDE
    page_spec = lambda k: pl.BlockSpec((1,) + pool_t.shape[1:],
                                       lambda b, i, pt, k=k: (pt[b, i * n_pg + k], 0, 0, 0, 0))
    grid_spec = pltpu.PrefetchScalarGridSpec(
        num_scalar_prefetch=1,
        grid=(B, n_pages // n_pg),
        in_specs=[page_spec(k) for k in range(n_pg)] + [pl.BlockSpec((K, N2), lambda b, i, pt: (0, 0))],
        out_specs=pl.BlockSpec((1, rows, N2), lambda b, i, pt: (b, i, 0)),
        scratch_shapes=[pltpu.VMEM((rows * CHUNK_PITCH, LANE), F32) for _ in range(D_KV // LANE)],
    )
    return pl.pallas_call(
        functools.partial(_compress_in_paged_kernel, n_pg=n_pg),
        out_shape=jax.ShapeDtypeStruct((B, n_pages * PAGE_SIZE // CMP_STRIDE, N2), F32),
        grid_spec=grid_spec,
        compiler_params=_cparams("arbitrary", "arbitrary"),
        name="compress_in_paged",
    )(page_table, *([pool_t] * n_pg), wbig)


def _compress_out_kernel(*refs):
    z_refs, (b1_ref, w2_ref, b2_ref, o_ref) = refs[:-4], refs[-4:]
    z = jnp.concatenate([z_ref[0] for z_ref in z_refs], axis=0)
    n = z.shape[-1] // 2
    rows = z.shape[0]
    second = pltpu.roll(z[:, n:], rows - 1, axis=0)
    hdn = jax.nn.gelu(z[:, :n] + second + b1_ref[...])
    o_ref[0, :rows, :] = jnp.dot(hdn.astype(BF16), w2_ref[...], preferred_element_type=F32) + b2_ref[...]
    if o_ref.shape[1] > rows:
        o_ref[0, rows:, :] = jnp.zeros((o_ref.shape[1] - rows, n), F32)


def _compress_out(zs, tables, n_out):
    _, b1, w2, b2 = tables
    B, _, N2 = zs[0].shape
    return pl.pallas_call(
        _compress_out_kernel,
        out_shape=jax.ShapeDtypeStruct((B, n_out, N2 // 2), F32),
        grid=(B,),
        in_specs=[pl.BlockSpec((1, z.shape[1], N2), lambda b: (b, 0, 0)) for z in zs] + [
                  pl.BlockSpec((1, N2 // 2), lambda b: (0, 0)),
                  pl.BlockSpec((N2 // 2, N2 // 2), lambda b: (0, 0)),
                  pl.BlockSpec((1, N2 // 2), lambda b: (0, 0))],
        out_specs=pl.BlockSpec((1, n_out, N2 // 2), lambda b: (b, 0, 0)),
        compiler_params=_cparams("parallel"),
        name="compress_out",
    )(*zs, b1, w2, b2)


def _rel_bucket(dist):
    n = jnp.maximum(dist, 0)
    max_exact = NUM_BUCKETS // 2
    nf = jnp.maximum(n, 1).astype(F32)
    large = max_exact + (jnp.log(nf / max_exact) / math.log(REL_MAX_DIST / max_exact)
                         * (NUM_BUCKETS - max_exact)).astype(jnp.int32)
    large = jnp.minimum(large, NUM_BUCKETS - 1)
    return jnp.where(n < max_exact, n, large)


def _bias_by_distance(rel_bias, n_max):
    onehot = (_rel_bucket(jnp.arange(n_max))[None, :] == jnp.arange(NUM_BUCKETS)[:, None]).astype(F32)
    return jnp.dot(jnp.transpose(rel_bias.astype(F32)), onehot, precision=HIGHEST)


def _shifted_chunks(bias_n, pad, n_chunks, width):
    n = min(bias_n.shape[1], n_chunks * width - pad)
    ext = jnp.concatenate([jnp.broadcast_to(bias_n[:, :1], (N_HEADS, pad)), bias_n[:, :n],
                           jnp.zeros((N_HEADS, n_chunks * width - pad - n), F32)], axis=1)
    return ext.reshape(N_HEADS, n_chunks, width)


def _bias_tables_kernel(ed_ref, ec_ref, tzs_ref, tzw_ref, cmp_ref, *, tq, tk, n_qt):
    n_ds, n_dw, n_j = tzs_ref.shape[1] - 1, tzw_ref.shape[1] - 1, cmp_ref.shape[1] // 8
    tzs_ref[0, n_ds] = jnp.full((tk, tq), NEG, F32)
    tzw_ref[0, n_dw] = jnp.full((tk, tq), NEG, F32)
    w = tq + tk
    c = lax.broadcasted_iota(jnp.int32, (tk, tq), 0)
    r = lax.broadcasted_iota(jnp.int32, (tk, tq), 1)
    for d in range(n_ds):
        v = jnp.concatenate([ed_ref[0, d:d + 1, :], ed_ref[0, d + 1:d + 2, :]], axis=1)
        t = pltpu.roll(jnp.broadcast_to(v, (tk, w)), w - (tk - 1), axis=1, stride=1, stride_axis=0)[:, :tq]
        dist = d * tk + r - c
        tzs_ref[0, d] = jnp.where(dist >= 0, t, NEG)
        if d < n_dw:
            tzw_ref[0, d] = jnp.where((dist >= 0) & (dist <= WINDOW), t, NEG)
    for j in range(n_j):
        dd = n_qt - 1 - j
        c0, c1 = max(dd, 0), max(dd + 1, 0)
        v = jnp.concatenate([ec_ref[0, c0:c0 + 1, :], ec_ref[0, c1:c1 + 1, :]], axis=1)
        t = pltpu.roll(jnp.broadcast_to(v, (8, w)), w - 7 * CMP_STRIDE, axis=1, stride=CMP_STRIDE, stride_axis=0)
        cmp_ref[0, j * 8:(j + 1) * 8, :] = t[:, :tq]


def _bias_tables(bias_n, n_qt, n_rb, n_ds, n_dw, tq, tk):
    assert tq == tk == 8 * CMP_STRIDE and n_dw <= n_ds
    n_j = n_rb + n_qt - 1
    ed = _shifted_chunks(bias_n, tk - 1, n_ds + 1, tq)
    ec = _shifted_chunks(bias_n, 7 * CMP_STRIDE + CMP_BLOCK - 1, n_qt + 1, tq)
    head = lambda a: pl.BlockSpec((1,) + a.shape[1:], lambda h: (h,) + (0,) * (a.ndim - 1))
    outs = (jax.ShapeDtypeStruct((N_HEADS, n_ds + 1, tk, tq), F32),
            jax.ShapeDtypeStruct((N_HEADS, n_dw + 1, tk, tq), F32),
            jax.ShapeDtypeStruct((N_HEADS, n_j * 8, tq), F32))
    tzs, tzw, cmp = pl.pallas_call(
        functools.partial(_bias_tables_kernel, tq=tq, tk=tk, n_qt=n_qt),
        out_shape=outs,
        grid=(N_HEADS,),
        in_specs=[head(ed), head(ec)],
        out_specs=tuple(head(o) for o in outs),
        compiler_params=_cparams("parallel"),
        name="bias_tables",
    )(ed, ec)
    grp = lambda a: a.reshape((N_KV_HEADS, GQA) + a.shape[1:])
    return grp(tzs), grp(tzw), cmp


def _pool_matrix(n_cmp_pad, n_blk_pad):
    r = SEL_BLOCK // CMP_STRIDE
    i = np.arange(n_cmp_pad)[None, :]
    j = np.arange(n_blk_pad)[:, None]
    return ((i >= r * j - 1) & (i <= r * j + r - 1)).astype(np.float32)


def _cmp_select_kernel(q_ref, k_ref, vt_ref, bias_ref, pool_ref, o_ref, sel_ref, *, tq, n_cmp):
    qt = pl.program_id(2)
    n_qt = pl.num_programs(2)
    q = q_ref[0, 0].reshape(GQA * tq, HEAD_DIM)
    k = k_ref[0, 0]
    nc = k.shape[0]
    s = _nt_dot(k, q)
    row0 = pl.multiple_of((n_qt - 1 - qt) * 8, 8)
    s = s + jnp.concatenate([bias_ref[g, pl.ds(row0, nc), :] for g in range(GQA)], axis=-1)
    t_pos = qt * tq + (lax.broadcasted_iota(jnp.int32, (nc, GQA * tq), 1) % tq)
    ci = lax.broadcasted_iota(jnp.int32, (nc, GQA * tq), 0)
    mask = (ci * CMP_STRIDE + CMP_BLOCK - 1 <= t_pos) & (ci < n_cmp)
    s = jnp.where(mask, s, NEG)
    m = jnp.max(s, axis=0, keepdims=True)
    p = jnp.where(mask, jnp.exp(s - m), 0.0)
    p = p / jnp.maximum(jnp.sum(p, axis=0, keepdims=True), 1e-30)
    ot = jnp.dot(vt_ref[0, 0], p.astype(BF16), preferred_element_type=F32)
    o_ref[0] = jnp.concatenate([ot[:, g * tq:(g + 1) * tq].T for g in range(GQA)], axis=-1)
    imp = p[:, 0:tq]
    for g in range(1, GQA):
        imp = imp + p[:, g * tq:(g + 1) * tq]
    sb = jnp.dot(pool_ref[...], imp, precision=HIGHEST, preferred_element_type=F32)
    nb = sb.shape[0]
    blk = lax.broadcasted_iota(jnp.int32, (nb, tq), 0)
    cur = (qt * tq + lax.broadcasted_iota(jnp.int32, (nb, tq), 1)) // SEL_BLOCK
    causal = blk <= cur
    forced = (blk == 0) | (blk == cur) | (blk == cur - 1)
    sc = jnp.where(forced & causal, 1e4, jnp.where(causal, sb, -1.0))
    groups = [sc[r:r + 8] for r in range(0, nb, 8)]
    sub = lax.broadcasted_iota(jnp.int32, (8, tq), 0)
    ranks = [jnp.zeros((8, tq), F32) for _ in groups]
    for i in range(nb):
        row = sc[i:i + 1, :]
        for gi, grp in enumerate(groups):
            if gi * 8 > i:
                ahead = row >= grp
            elif gi * 8 + 7 < i:
                ahead = row > grp
            else:
                ahead = (row > grp) | ((row == grp) & (sub > i - gi * 8))
            ranks[gi] = ranks[gi] + jnp.where(ahead, 1.0, 0.0)
    rank = jnp.concatenate(ranks, axis=0)
    sel_ref[0, 0] = jnp.where((rank < N_SEL) & causal, 0.0, NEG)


def _cmp_select_prompt(q5, kc, vct, bias_tab, pool, n_cmp):
    B, _, _, T, _ = q5.shape
    NC = kc.shape[2]
    NB = pool.shape[0]
    R = bias_tab.shape[1]
    tq = ATT_TQ
    return pl.pallas_call(
        functools.partial(_cmp_select_kernel, tq=tq, n_cmp=n_cmp),
        out_shape=(jax.ShapeDtypeStruct((B, T, D_ATT), F32),
                   jax.ShapeDtypeStruct((B, N_KV_HEADS, NB, T), F32)),
        grid=(B, N_KV_HEADS, T // tq),
        in_specs=[pl.BlockSpec((1, 1, GQA, tq, HEAD_DIM), lambda b, h, i: (b, h, 0, i, 0)),
                  pl.BlockSpec((1, 1, NC, HEAD_DIM), lambda b, h, i: (b, h, 0, 0)),
                  pl.BlockSpec((1, 1, HEAD_DIM, NC), lambda b, h, i: (b, h, 0, 0)),
                  pl.BlockSpec((GQA, R, tq), lambda b, h, i: (h, 0, 0)),
                  pl.BlockSpec((NB, NC), lambda b, h, i: (0, 0))],
        out_specs=(pl.BlockSpec((1, tq, GQA * HEAD_DIM), lambda b, h, i: (b, i, h)),
                   pl.BlockSpec((1, 1, NB, tq), lambda b, h, i: (b, h, 0, i))),
        compiler_params=_cparams("parallel", "parallel", "parallel"),
        name="cmp_select_prompt",
    )(q5, kc, vct, bias_tab, pool)


def _sel_win_kernel(q_ref, ks_ref, vst_ref, kw_ref, vwt_ref, sel_ref, tzs_ref, tzw_ref, os_ref, ow_ref, *, tq):
    tk = ATT_TK
    qt = pl.program_id(2)
    q = q_ref[0, 0].reshape(GQA * tq, HEAD_DIM)
    width = GQA * tq
    per_tile = tk // SEL_BLOCK

    def make_sweep(k_ref, vt_ref, tz_ref, use_sel, n_chains, single_trip):
        n_d = tz_ref.shape[2] - 1

        def scores(kt, hi):
            pad = kt > hi
            kt = jnp.minimum(kt, hi)
            off = pl.multiple_of(kt * tk, tk)
            k = k_ref[0, 0, pl.ds(off, tk), :]
            d = jnp.where(pad, n_d, jnp.minimum(qt - kt, n_d - 1))
            bias = [tz_ref[0, g, d] for g in range(GQA)]
            if use_sel:
                rows = sel_ref[0, 0, pl.ds(kt * per_tile, per_tile), :]
                selb = jnp.concatenate([jnp.broadcast_to(rows[i:i + 1], (SEL_BLOCK, tq))
                                        for i in range(per_tile)], axis=0)
                bias = [b + selb for b in bias]
            return _nt_dot(k, q) + jnp.concatenate(bias, axis=1)

        def values_t(kt, lo, hi):
            off = pl.multiple_of(jnp.clip(kt, lo, hi) * tk, tk)
            return vt_ref[0, 0, :, pl.ds(off, tk)]

        def sweep(lo, hi):
            n_trips = (hi - lo + n_chains) // n_chains
            chain0 = (jnp.full((1, width), 0.5 * NEG, F32), jnp.zeros((1, width), F32),
                      jnp.zeros((HEAD_DIM, width), F32), jnp.ones((1, width), F32), jnp.zeros((tk, width), BF16))

            def trip(i, chains):
                kt = lo + n_chains * i
                pv = [jnp.dot(values_t(kt - n_chains + c, lo, hi), chains[c][4], preferred_element_type=F32)
                      for c in range(n_chains)]
                ss = [scores(kt + c, hi) for c in range(n_chains)]
                out = []
                for c in range(n_chains):
                    m, l, acc, alpha_prev, _ = chains[c]
                    m_new = jnp.maximum(m, jnp.max(ss[c], axis=0, keepdims=True))
                    alpha = jnp.exp(m - m_new)
                    p = jnp.exp(ss[c] - m_new)
                    l = alpha * l + jnp.sum(p, axis=0, keepdims=True)
                    out.append((m_new, l, alpha_prev * acc + pv[c], alpha, p.astype(BF16)))
                return tuple(out)

            if single_trip:
                done = []
                for c in range(n_chains):
                    s = scores(lo + c, hi)
                    m = jnp.maximum(jnp.max(s, axis=0, keepdims=True), 0.5 * NEG)
                    p = jnp.exp(s - m)
                    done.append((m, jnp.sum(p, axis=0, keepdims=True),
                                 jnp.dot(values_t(lo + c, lo, hi), p.astype(BF16), preferred_element_type=F32)))
            else:
                chains = lax.fori_loop(0, n_trips, trip, (chain0,) * n_chains)
                kt_last = lo + n_chains * (n_trips - 1)
                done = []
                for c in range(n_chains):
                    m, l, acc, alpha, p = chains[c]
                    done.append((m, l, alpha * acc + jnp.dot(values_t(kt_last + c, lo, hi), p,
                                                              preferred_element_type=F32)))
            m_all = functools.reduce(jnp.maximum, [m for m, _, _ in done])
            num = den = 0.0
            for m, l, acc in done:
                e = jnp.exp(m - m_all)
                num = num + acc * e
                den = den + l * e
            o = num / jnp.maximum(den, 1e-30)
            return jnp.concatenate([o[:, g * tq:(g + 1) * tq].T for g in range(GQA)], axis=-1)
        return sweep

    n_win = tzw_ref.shape[2] - 1
    os_ref[0] = make_sweep(ks_ref, vst_ref, tzs_ref, True, SEL_CHAINS, False)(0, qt)
    ow_ref[0] = make_sweep(kw_ref, vwt_ref, tzw_ref, False, n_win, True)(jnp.maximum(qt - (n_win - 1), 0), qt)


def _sel_win_prompt(q5, ks, vst, kw, vwt, sel, tzs, tzw):
    B, _, _, T, _ = q5.shape
    NB = sel.shape[2]
    tq = ATT_TQ
    k_spec = pl.BlockSpec((1, 1, T, HEAD_DIM), lambda b, h, i: (b, h, 0, 0))
    vt_spec = pl.BlockSpec((1, 1, HEAD_DIM, T), lambda b, h, i: (b, h, 0, 0))
    tz_spec = lambda tz: pl.BlockSpec((1,) + tz.shape[1:], lambda b, h, i: (h, 0, 0, 0, 0))
    o_spec = pl.BlockSpec((1, tq, GQA * HEAD_DIM), lambda b, h, i: (b, i, h))
    return pl.pallas_call(
        functools.partial(_sel_win_kernel, tq=tq),
        out_shape=(jax.ShapeDtypeStruct((B, T, D_ATT), F32), jax.ShapeDtypeStruct((B, T, D_ATT), F32)),
        grid=(B, N_KV_HEADS, T // tq),
        in_specs=[pl.BlockSpec((1, 1, GQA, tq, HEAD_DIM), lambda b, h, i: (b, h, 0, i, 0)),
                  k_spec, vt_spec, k_spec, vt_spec,
                  pl.BlockSpec((1, 1, NB, tq), lambda b, h, i: (b, h, 0, i)),
                  tz_spec(tzs), tz_spec(tzw)],
        out_specs=(o_spec, o_spec),
        compiler_params=_cparams("parallel", "parallel", "parallel"),
        name="sel_win_prompt",
    )(q5, ks, vst, kw, vwt, sel, tzs, tzw)


def _gate_expand_matrix():
    m = np.zeros((3, 2 * LANE, D_ATT), np.float32)
    for r in range(3):
        for h in range(N_HEADS):
            m[r, h * 3 + r, h * HEAD_DIM:(h + 1) * HEAD_DIM] = 1.0
            m[r, LANE + h * 3 + r, h * HEAD_DIM:(h + 1) * HEAD_DIM] = 1.0
    return m


def _split_bf16(x):
    hi = x.astype(BF16)
    return hi, (x - hi.astype(F32)).astype(BF16)


def _post_mixer_kernel(y_ref, u_ref, oc_ref, os_ref, ow_ref, g_ref, x_ref, gate_ref, sh_ref, sc_ref,
                       dskip_ref, wglu_ref, bglu_ref, gexp_ref, wout_ref, lng_ref, lnb_ref,
                       wr_ref, br_ref, x1_ref, hm_ref, te_ref, tw_ref):
    y = y_ref[0] + dskip_ref[...] * u_ref[0]
    gl = jax.nn.gelu(y)
    ssm = gl * jax.nn.sigmoid(jnp.dot(gl.astype(BF16), wglu_ref[...], preferred_element_type=F32)
                              + bglu_ref[...])
    sg = jnp.concatenate(_split_bf16(jax.nn.sigmoid(g_ref[0])), axis=1)
    att = jnp.zeros_like(oc_ref[0])
    for r, o_ref in enumerate((oc_ref, os_ref, ow_ref)):
        att = att + jnp.dot(sg, gexp_ref[r], preferred_element_type=F32) * o_ref[0]
    h = (jnp.dot(ssm.astype(BF16), wout_ref[:D_SSM, :], preferred_element_type=F32)
         + jnp.dot(att.astype(BF16), wout_ref[D_SSM:, :], preferred_element_type=F32))
    z = DN_ALPHA * x_ref[0] + gate_ref[0] * h
    x1 = _layer_norm(z) * lng_ref[...] + lnb_ref[...]
    x1_ref[0] = x1
    hm = _layer_norm(x1) * (1.0 + sc_ref[0]) + sh_ref[0]
    hm_ref[0] = hm
    hm_hi, hm_lo = _split_bf16(hm)
    logits = (jnp.dot(hm_hi, wr_ref[0], preferred_element_type=F32)
              + jnp.dot(hm_lo, wr_ref[0], preferred_element_type=F32)
              + jnp.dot(hm_hi, wr_ref[1], preferred_element_type=F32)) + br_ref[...]
    lane = lax.broadcasted_iota(jnp.int32, logits.shape, 1)
    work = jnp.where(lane < N_EXPERTS, logits, -jnp.inf)
    te = jnp.zeros(logits.shape, jnp.int32)
    tv = jnp.zeros(logits.shape, F32)
    for k in range(TOP_K):
        best = jnp.max(work, axis=-1, keepdims=True)
        arg = jnp.min(jnp.where(work == best, lane, LANE), axis=-1, keepdims=True)
        te = jnp.where(lane == k, arg, te)
        tv = jnp.where(lane == k, best, tv)
        work = jnp.where(lane == arg, -jnp.inf, work)
    ex = jnp.where(lane < TOP_K, jnp.exp(tv - tv[:, 0:1]), 0.0)
    te_ref[0] = te
    tw_ref[0] = ex / jnp.sum(ex, axis=-1, keepdims=True)


def _post_mixer(y, u, oc, osel, ow, g, x, gate, shift, scale, w, tm):
    B, T, D = x.shape
    R = gate.shape[1]
    rb = 1 if R == 1 else tm
    mod_map = (lambda b, i: (b, 0, 0)) if R == 1 else (lambda b, i: (b, i, 0))
    row = lambda n: pl.BlockSpec((1, tm, n), lambda b, i: (b, i, 0))
    mod = pl.BlockSpec((1, rb, D), mod_map)
    full = lambda a: pl.BlockSpec(a.shape, lambda b, i: (0,) * a.ndim)
    consts = (w['d_skip'], w['w_glu'], w['b_glu'], w['gexp'], w['w_out'], w['ln1_g'], w['ln1_b'],
              w['w_router'], w['b_router'])
    return pl.pallas_call(
        _post_mixer_kernel,
        out_shape=(jax.ShapeDtypeStruct((B, T, D), F32), jax.ShapeDtypeStruct((B, T, D), F32),
                   jax.ShapeDtypeStruct((B, T, LANE), jnp.int32), jax.ShapeDtypeStruct((B, T, LANE), F32)),
        grid=(B, T // tm),
        in_specs=[row(D_SSM), row(D_SSM), row(D_ATT), row(D_ATT), row(D_ATT), row(LANE), row(D),
                  mod, mod, mod] + [full(a) for a in consts],
        out_specs=(row(D), row(D), row(LANE), row(LANE)),
        compiler_params=_cparams("parallel", "parallel"),
        name="post_mixer",
    )(y, u, oc, osel, ow, g, x, gate, shift, scale, *consts)


def _expert_kernel(e_ref, blk_ref, lo_ref, hi_ref, first_ref, x_ref, wgu_ref, bgu_ref, wd_ref, bd_ref, o_ref,
                   wgu_s, wd_s):
    i = pl.program_id(0)
    fresh = (i == 0) | (e_ref[i] != e_ref[jnp.maximum(i - 1, 0)])

    @pl.when(fresh)
    def _():
        wgu_s[...] = wgu_ref[0].astype(BF16)
        wd_s[...] = wd_ref[0].astype(BF16)

    @pl.when(first_ref[i] == 1)
    def _():
        o_ref[...] = jnp.zeros_like(o_ref)

    @pl.when(hi_ref[i] > lo_ref[i])
    def _():
        gu = jnp.dot(x_ref[...].astype(BF16), wgu_s[...], preferred_element_type=F32) + bgu_ref[0]
        gate = jnp.minimum(gu[:, :D_FF], SWIGLU_LIMIT)
        up = jnp.clip(gu[:, D_FF:], -SWIGLU_LIMIT, SWIGLU_LIMIT)
        hh = (up + 1.0) * gate * jax.nn.sigmoid(SWIGLU_ALPHA * gate)
        y = jnp.dot(hh.astype(BF16), wd_s[...], preferred_element_type=F32) + bd_ref[0]
        row = blk_ref[i] * MOE_ROWS + lax.broadcasted_iota(jnp.int32, (MOE_ROWS, 1), 0)
        o_ref[...] = jnp.where((row >= lo_ref[i]) & (row < hi_ref[i]), y, o_ref[...])


def _experts(xb, items, w_gate_up, b_gate_up, w_down, b_down):
    rows, D = xb.shape
    n_items = items[0].shape[0]
    wmap = lambda i, e, blk, lo, hi, first: (e[i], 0, 0)
    rmap = lambda i, e, blk, lo, hi, first: (blk[i], 0)
    grid_spec = pltpu.PrefetchScalarGridSpec(
        num_scalar_prefetch=5,
        grid=(n_items,),
        in_specs=[pl.BlockSpec((MOE_ROWS, D), rmap),
                  pl.BlockSpec((1, D, 2 * D_FF), wmap),
                  pl.BlockSpec((1, 1, 2 * D_FF), wmap),
                  pl.BlockSpec((1, D_FF, D), wmap),
                  pl.BlockSpec((1, 1, D), wmap)],
        out_specs=pl.BlockSpec((MOE_ROWS, D), rmap),
        scratch_shapes=[pltpu.VMEM((D, 2 * D_FF), BF16), pltpu.VMEM((D_FF, D), BF16)],
    )
    return pl.pallas_call(
        _expert_kernel,
        out_shape=jax.ShapeDtypeStruct((rows, D), F32),
        grid_spec=grid_spec,
        compiler_params=_cparams("arbitrary"),
        name="moe_experts",
    )(*items, xb, w_gate_up, b_gate_up.reshape(N_EXPERTS, 1, 2 * D_FF), w_down,
      b_down.reshape(N_EXPERTS, 1, D))


def _moe_dispatch(top_e, n):
    blk = MOE_ROWS
    nk = n * TOP_K
    cb = 128
    assert nk % cb == 0
    e = top_e.reshape(-1)
    onehot = (e[:, None] == jnp.arange(N_EXPERTS)[None, :]).astype(F32)
    oh3 = onehot.reshape(nk // cb, cb, N_EXPERTS)
    tri = jnp.asarray(np.tril(np.ones((cb, cb), np.float32), -1))
    within = jnp.einsum('ij,bje->bie', tri, oh3, precision=HIGHEST)
    blk_tot = jnp.sum(oh3, axis=1)
    blk_off = jnp.cumsum(blk_tot, axis=0) - blk_tot
    counts = jnp.sum(blk_tot, axis=0)
    start = jnp.cumsum(counts) - counts
    dest = jnp.sum((within + blk_off[:, None, :] + start[None, None, :]) * oh3, axis=-1)
    dest = dest.reshape(nk).astype(jnp.int32)
    order = jnp.argsort(dest)
    n_blk = -(-nk // blk)
    row_tok = jnp.concatenate([(order // TOP_K).astype(jnp.int32), jnp.full((n_blk * blk - nk,), n, jnp.int32)])
    counts_i, start_i = counts.astype(jnp.int32), start.astype(jnp.int32)
    first_b = start_i // blk
    last_b = (start_i + counts_i - 1) // blk
    n_it = jnp.where(counts_i > 0, last_b - first_b + 1, 0)
    it_end = jnp.cumsum(n_it)
    it_start = it_end - n_it
    n_items = n_blk + N_EXPERTS - 1
    i = jnp.arange(n_items)
    live = i < it_end[-1]
    it_e = jnp.minimum(jnp.sum(it_end[None, :] <= i[:, None], axis=1), N_EXPERTS - 1)
    it_blk = jnp.where(live, first_b[it_e] + i - it_start[it_e], n_blk - 1)
    it_lo = jnp.where(live, start_i[it_e], 0)
    it_hi = jnp.where(live, start_i[it_e] + counts_i[it_e], 0)
    it_first = jnp.concatenate([jnp.ones((1,), jnp.int32), (it_blk[1:] != it_blk[:-1]).astype(jnp.int32)])
    items = tuple(a.astype(jnp.int32) for a in (it_e, it_blk, it_lo, it_hi, it_first))
    return row_tok, dest.reshape(n, TOP_K), items


def _final_kernel(x_ref, y0_ref, y1_ref, y2_ref, y3_ref, tw_ref, gate_ref, lng_ref, lnb_ref, o_ref):
    tw = tw_ref[0]
    y = jnp.zeros_like(x_ref[0])
    for k, y_ref in enumerate((y0_ref, y1_ref, y2_ref, y3_ref)):
        y = y + tw[:, k:k + 1] * y_ref[0]
    z = DN_ALPHA * x_ref[0] + gate_ref[0] * y
    o_ref[0] = _layer_norm(z) * lng_ref[...] + lnb_ref[...]


def _final(x1, ys, tw, gate, ln_g, ln_b, tm):
    B, T, D = x1.shape
    R = gate.shape[1]
    rb = 1 if R == 1 else tm
    mod_map = (lambda b, i: (b, 0, 0)) if R == 1 else (lambda b, i: (b, i, 0))
    row = lambda n: pl.BlockSpec((1, tm, n), lambda b, i: (b, i, 0))
    vec = pl.BlockSpec((1, D), lambda b, i: (0, 0))
    return pl.pallas_call(
        _final_kernel,
        out_shape=jax.ShapeDtypeStruct((B, T, D), F32),
        grid=(B, T // tm),
        in_specs=[row(D), row(D), row(D), row(D), row(D), row(LANE),
                  pl.BlockSpec((1, rb, D), mod_map), vec, vec],
        out_specs=row(D),
        compiler_params=_cparams("parallel", "parallel"),
        name="moe_combine_ln",
    )(x1, *ys, tw, gate, ln_g, ln_b)


def _cmp_select_step_kernel(q_ref, kv_ref, bias_ref, pool_ref, o_ref, idx_ref, *, n_cmp, n_blk, q_pos):
    q = q_ref[0].astype(BF16)
    ncp = kv_ref.shape[1]
    nbp = pool_ref.shape[1]
    hd = HEAD_DIM
    kv = kv_ref[0]
    kb = [kv[:, h * hd:(h + 1) * hd].astype(BF16) for h in range(N_KV_HEADS)]
    vb = [kv[:, (N_KV_HEADS + h) * hd:(N_KV_HEADS + h + 1) * hd].astype(BF16) for h in range(N_KV_HEADS)]
    row = lax.broadcasted_iota(jnp.int32, (N_HEADS, 1), 0)
    first = row < GQA
    s = jnp.where(first, _nt_dot(q, kb[0]), _nt_dot(q, kb[1])) * (hd ** -0.5)
    s = s + bias_ref[...]
    ci = lax.broadcasted_iota(jnp.int32, (N_HEADS, ncp), 1)
    mask = (ci * CMP_STRIDE + CMP_BLOCK - 1 <= q_pos) & (ci < n_cmp)
    s = jnp.where(mask, s, NEG)
    m = jnp.max(s, axis=-1, keepdims=True)
    p = jnp.where(mask, jnp.exp(s - m), 0.0)
    p = p / jnp.maximum(jnp.sum(p, axis=-1, keepdims=True), 1e-30)
    pb = p.astype(BF16)
    o_ref[0] = jnp.where(first, jnp.dot(pb, vb[0], preferred_element_type=F32),
                         jnp.dot(pb, vb[1], preferred_element_type=F32))
    imp0 = jnp.sum(jnp.where(first, p, 0.0), axis=0, keepdims=True)
    imp1 = jnp.sum(jnp.where(first, 0.0, p), axis=0, keepdims=True)
    imp = jnp.where(first, imp0, imp1)
    sb = jnp.dot(imp, pool_ref[...], precision=HIGHEST, preferred_element_type=F32)
    cur = q_pos // SEL_BLOCK
    bi = lax.broadcasted_iota(jnp.int32, (nbp, nbp), 0)
    bj = lax.broadcasted_iota(jnp.int32, (nbp, nbp), 1)
    blk = lax.broadcasted_iota(jnp.int32, (1, nbp), 1)
    causal = blk <= cur
    forced = (blk == 0) | (blk == cur) | (blk == cur - 1)
    rsel = lax.broadcasted_iota(jnp.int32, (N_SEL, nbp), 0)
    for h in range(N_KV_HEADS):
        sc = jnp.where(forced & causal, 1e4, jnp.where(causal, sb[h * GQA:h * GQA + 1, :], -1.0))
        sc = jnp.where(blk < n_blk, sc, -2.0)
        scb = jnp.broadcast_to(sc, (nbp, nbp))
        col = jnp.sum(jnp.where(bi == bj, scb, 0.0), axis=1, keepdims=True)
        ahead = (col > scb) | ((col == scb) & (bi < bj))
        rank = jnp.sum(ahead.astype(jnp.int32), axis=0, keepdims=True)
        hit = jnp.broadcast_to(rank, (N_SEL, nbp)) == rsel
        idx = jnp.sum(jnp.where(hit, jnp.broadcast_to(blk, (N_SEL, nbp)), 0), axis=1, keepdims=True)
        idx_ref[0, h] = jnp.broadcast_to(idx, (N_SEL, LANE))


def _cmp_select_step(q, ckv, bias, pool, n_cmp, n_blk, q_pos):
    B = q.shape[0]
    NCp = ckv.shape[1]
    return pl.pallas_call(
        functools.partial(_cmp_select_step_kernel, n_cmp=n_cmp, n_blk=n_blk, q_pos=q_pos),
        out_shape=(jax.ShapeDtypeStruct((B, N_HEADS, HEAD_DIM), F32),
                   jax.ShapeDtypeStruct((B, N_KV_HEADS, N_SEL, LANE), jnp.int32)),
        grid=(B,),
        in_specs=[pl.BlockSpec((1, N_HEADS, HEAD_DIM), lambda b: (b, 0, 0)),
                  pl.BlockSpec((1, NCp, D_KV), lambda b: (b, 0, 0)),
                  pl.BlockSpec(bias.shape, lambda b: (0, 0)),
                  pl.BlockSpec(pool.shape, lambda b: (0, 0))],
        out_specs=(pl.BlockSpec((1, N_HEADS, HEAD_DIM), lambda b: (b, 0, 0)),
                   pl.BlockSpec((1, N_KV_HEADS, N_SEL, LANE), lambda b: (b, 0, 0, 0))),
        compiler_params=_cparams("parallel"),
        name="cmp_select_step",
    )(q, ckv, bias, pool)


def _sel_step_kernel(pg_ref, idx_ref, q_ref, *refs, n_past, q_pos):
    page_refs = refs[:N_SEL]
    new_ref, bias_ref, kpos_ref, o_ref = refs[N_SEL:]
    b, h = pl.program_id(0), pl.program_id(1)
    base = (b * N_KV_HEADS + h) * N_SEL
    kts, vts = [], []
    for j in range(N_SEL):
        is_new = idx_ref[base + j] >= n_past
        kts.append(jnp.where(is_new, new_ref[0, 0, 0], page_refs[j][0, 0, 0]))
        vts.append(jnp.where(is_new, new_ref[0, 1, 0], page_refs[j][0, 1, 0]))
    kt = jnp.concatenate(kts, axis=1).astype(BF16)
    vt = jnp.concatenate(vts, axis=1).astype(BF16)
    s = jnp.dot(q_ref[0].astype(BF16), kt, preferred_element_type=F32) * (HEAD_DIM ** -0.5) + bias_ref[0, 0]
    mask = kpos_ref[0, 0] <= q_pos
    s = jnp.where(mask, s, NEG)
    m = jnp.max(s, axis=-1, keepdims=True)
    p = jnp.where(mask, jnp.exp(s - m), 0.0)
    l = jnp.sum(p, axis=-1, keepdims=True)
    o_ref[0, 0] = _nt_dot(p.astype(BF16), vt) / jnp.maximum(l, 1e-30)


def _sel_step(q, pool_t, new_t, bias_sel, kpos, pages, idx_flat, n_past, q_pos):
    B = q.shape[0]
    nk = N_SEL * PAGE_SIZE
    slot = lambda b, h, j: (b * N_KV_HEADS + h) * N_SEL + j
    page_spec = lambda j: pl.BlockSpec((1, 2, 1, HEAD_DIM, PAGE_SIZE),
                                       lambda b, h, pg, ix, j=j: (pg[slot(b, h, j)], 0, h, 0, 0))
    grid_spec = pltpu.PrefetchScalarGridSpec(
        num_scalar_prefetch=2,
        grid=(B, N_KV_HEADS),
        in_specs=[pl.BlockSpec((1, N_HEADS, HEAD_DIM), lambda b, h, pg, ix: (b, 0, 0))]
        + [page_spec(j) for j in range(N_SEL)]
        + [pl.BlockSpec((1, 2, 1, HEAD_DIM, PAGE_SIZE), lambda b, h, pg, ix: (b, 0, h, 0, 0)),
           pl.BlockSpec((1, 1, N_HEADS, nk), lambda b, h, pg, ix: (b, h, 0, 0)),
           pl.BlockSpec((1, 1, 1, nk), lambda b, h, pg, ix: (b, h, 0, 0))],
        out_specs=pl.BlockSpec((1, 1, N_HEADS, HEAD_DIM), lambda b, h, pg, ix: (b, h, 0, 0)),
    )
    return pl.pallas_call(
        functools.partial(_sel_step_kernel, n_past=n_past, q_pos=q_pos),
        out_shape=jax.ShapeDtypeStruct((B, N_KV_HEADS, N_HEADS, HEAD_DIM), F32),
        grid_spec=grid_spec,
        compiler_params=_cparams("arbitrary", "arbitrary"),
        name="sel_step",
    )(pages, idx_flat, q, *([pool_t] * N_SEL), new_t, bias_sel, kpos)


def _win_step_kernel(q_ref, w_ref, new_ref, bias_ref, bias0_ref, o_ref):
    q = q_ref[0]
    qb = q.astype(BF16)
    row = lax.broadcasted_iota(jnp.int32, (N_HEADS, 1), 0)
    first = row < GQA
    w = w_ref[0]
    hd = HEAD_DIM
    kb = [w[:, h * hd:(h + 1) * hd].astype(BF16) for h in range(N_KV_HEADS)]
    vb = [w[:, (N_KV_HEADS + h) * hd:(N_KV_HEADS + h + 1) * hd].astype(BF16) for h in range(N_KV_HEADS)]
    s = jnp.where(first, _nt_dot(qb, kb[0]), _nt_dot(qb, kb[1])) * (hd ** -0.5) + bias_ref[...]
    new = new_ref[0]
    kn = jnp.where(first, new[:, 0:hd], new[:, hd:2 * hd])
    vn = jnp.where(first, new[:, 2 * hd:3 * hd], new[:, 3 * hd:])
    sn = jnp.sum(q * kn, axis=-1, keepdims=True) * (hd ** -0.5) + bias0_ref[...]
    m = jnp.maximum(jnp.max(s, axis=-1, keepdims=True), sn)
    p = jnp.exp(s - m)
    pn = jnp.exp(sn - m)
    l = jnp.sum(p, axis=-1, keepdims=True) + pn
    pb = p.astype(BF16)
    acc = jnp.where(first, jnp.dot(pb, vb[0], preferred_element_type=F32),
                    jnp.dot(pb, vb[1], preferred_element_type=F32)) + pn * vn
    o_ref[0] = acc / jnp.maximum(l, 1e-30)


def _win_step(q, win, new, bias, bias0):
    B, W, _ = win.shape
    return pl.pallas_call(
        _win_step_kernel,
        out_shape=jax.ShapeDtypeStruct((B, N_HEADS, HEAD_DIM), F32),
        grid=(B,),
        in_specs=[pl.BlockSpec((1, N_HEADS, HEAD_DIM), lambda b: (b, 0, 0)),
                  pl.BlockSpec((1, W, D_KV), lambda b: (b, 0, 0)),
                  pl.BlockSpec((1, 1, D_KV), lambda b: (b, 0, 0)),
                  pl.BlockSpec((N_HEADS, W), lambda b: (0, 0)),
                  pl.BlockSpec((N_HEADS, 1), lambda b: (0, 0))],
        out_specs=pl.BlockSpec((1, N_HEADS, HEAD_DIM), lambda b: (b, 0, 0)),
        compiler_params=_cparams("parallel"),
        name="win_step",
    )(q, win, new, bias, bias0)


def _split_heads(kv, dtype):
    B, L, _ = kv.shape
    kv5 = kv.reshape(B, L, 2, N_KV_HEADS, HEAD_DIM)
    return (jnp.transpose(kv5[:, :, 0], (0, 2, 1, 3)).astype(dtype),
            jnp.transpose(kv5[:, :, 1], (0, 2, 1, 3)).astype(dtype))


def _nsa_prompt(q5, kvc, ks, vst, kw, vwt, cmp_tab, rel_bias):
    B, T, _ = kvc.shape
    nc = T // CMP_STRIDE
    nb = T // SEL_BLOCK
    ckv = _compress_out([_compress_in(kvc.reshape(B, nc, CMP_STRIDE * D_KV), cmp_tab)], cmp_tab, nc)
    kc, vc = _split_heads(ckv, BF16)
    vct = jnp.transpose(vc, (0, 1, 3, 2))
    bias_n = _bias_by_distance(rel_bias, T)
    n_qt, n_kt = T // ATT_TQ, T // ATT_TK
    n_ds = min(n_kt, -(-(REL_MAX_DIST + ATT_TK - 1) // ATT_TK) + 1)
    n_dw = min(n_kt, WINDOW // ATT_TK + 1)
    tzs, tzw, bias_tab = _bias_tables(bias_n, n_qt, nc // 8, n_ds, n_dw, ATT_TQ, ATT_TK)
    pool = jnp.asarray(_pool_matrix(nc, nb))
    o_cmp, sel = _cmp_select_prompt(q5, kc, vct, bias_tab, pool, nc - 1)
    o_sel, o_win = _sel_win_prompt(q5, ks, vst, kw, vwt, sel, tzs, tzw)
    return o_cmp, o_sel, o_win


def _nsa_sample(q, kvc, kvs, kvw, pool_cmp, pool_sel, win_buf, page_table, cmp_tab, rel_bias):
    B = q.shape[0]
    n_pages = page_table.shape[1]
    past_len = n_pages * PAGE_SIZE
    q_pos = past_len
    lp = -(-(past_len + 1) // SEL_BLOCK) * SEL_BLOCK
    n_cmp = lp // CMP_STRIDE - 1
    n_blk = lp // SEL_BLOCK
    n_past_chunks = past_len // CMP_STRIDE
    n_tail = 8
    assert n_past_chunks + n_tail >= n_cmp + 1
    n_chunks = n_past_chunks + n_tail
    feature_major = lambda pool: jnp.transpose(pool, (0, 2, 3, 4, 1))
    z_past = _compress_in_paged(feature_major(pool_cmp), page_table, cmp_tab)
    tail = jnp.pad(kvc[:, None, :], ((0, 0), (0, n_tail * CMP_STRIDE - 1), (0, 0)))
    z_tail = _compress_in(tail.reshape(B, n_tail, CMP_STRIDE * D_KV), cmp_tab)
    ncp = -(-n_chunks // LANE) * LANE
    nbp = -(-n_blk // LANE) * LANE
    ckv = _compress_out([z_past, z_tail], cmp_tab, ncp)
    bias_n = _bias_by_distance(rel_bias, q_pos + 1)
    n_back = max((n_pages + 1) * PAGE_SIZE, ncp * CMP_STRIDE + CMP_BLOCK)
    back = jnp.concatenate([bias_n[:, ::-1], jnp.broadcast_to(bias_n[:, :1], (N_HEADS, n_back - q_pos - 1))], 1)
    bias_c = back[:, CMP_BLOCK - 1:CMP_BLOCK - 1 + ncp * CMP_STRIDE:CMP_STRIDE]
    pool = jnp.asarray(_pool_matrix(ncp, nbp).T)
    q3 = q.reshape(B, N_HEADS, HEAD_DIM)
    o_cmp, idx = _cmp_select_step(q3, ckv, bias_c, pool, n_cmp, n_blk, q_pos)
    idx = idx[..., 0]
    bpp = PAGE_SIZE // SEL_BLOCK
    n_past = n_pages * bpp
    lpage = idx // bpp
    pages = jnp.take_along_axis(page_table, jnp.minimum(lpage, n_pages - 1).reshape(B, -1), axis=1)
    new_t = jnp.pad(kvs.reshape(B, 2, N_KV_HEADS, HEAD_DIM, 1), ((0, 0),) * 4 + ((0, PAGE_SIZE - 1),))
    bias_page = jnp.transpose(back[:, :(n_pages + 1) * PAGE_SIZE].reshape(N_HEADS, n_pages + 1, PAGE_SIZE),
                              (1, 0, 2))
    bias_sel = jnp.transpose(bias_page[lpage], (0, 1, 3, 2, 4)).reshape(B, N_KV_HEADS, N_HEADS, -1)
    kpos = lpage[..., None] * PAGE_SIZE + jnp.arange(PAGE_SIZE)
    ok = (kpos // SEL_BLOCK == idx[..., None]) & (idx <= q_pos // SEL_BLOCK)[..., None]
    kpos = jnp.where(ok, kpos, q_pos + 1).reshape(B, N_KV_HEADS, 1, -1).astype(jnp.int32)
    o_sel = _sel_step(q3, feature_major(pool_sel), new_t, bias_sel, kpos, pages.reshape(-1).astype(jnp.int32),
                      idx.reshape(-1).astype(jnp.int32), n_past, q_pos)
    o_sel = jnp.concatenate([o_sel[:, h, h * GQA:(h + 1) * GQA] for h in range(N_KV_HEADS)], axis=1)
    wb = win_buf.shape[1]
    bias_w = bias_n[:, 1:wb + 1][:, ::-1]
    o_win = _win_step(q3, win_buf.reshape(B, wb, D_KV), kvw[:, None, :], bias_w, bias_n[:, 0:1])
    return o_cmp.reshape(B, D_ATT), o_sel.reshape(B, D_ATT), o_win.reshape(B, D_ATT)


def kernel(x_prompt, x_sample, cache_cmp_kv, cache_sel_kv, state_win_kv, state_ssm_re, state_ssm_im, page_table,
           c_prompt, c_sample, w_ada, b_ada, w_in, lam_re, lam_im, log_dt, b_re, b_im, c_re, c_im, d_skip,
           w_glu, b_glu, phi_pe, phi_w1, phi_b1, phi_w2, phi_b2, rel_bias, w_out, ln1_g, ln1_b,
           w_router, b_router, w_gate_up, b_gate_up, w_down, b_down, ln2_g, ln2_b):
    assert w_ada.shape[0] == DEPTH == 1
    l = 0
    Bp, T, D = x_prompt.shape
    Bs = x_sample.shape[0]
    kv_tail = (2, N_KV_HEADS, HEAD_DIM)

    n_c = Bp + Bs
    c_all = jnp.pad(jnp.concatenate([c_prompt, c_sample], 0), ((0, -n_c % 8), (0, 0)))
    m_all = _adaln(c_all, w_ada[l], b_ada[l])
    m_p = m_all[:Bp].reshape(Bp, 6, D)
    m_s = m_all[Bp:n_c].reshape(Bs, 6, D)
    mod_p = [m_p[:, i:i + 1, :] for i in range(6)]
    mod_s = [m_s[None, :, i, :] for i in range(6)]

    w_in_pad = jnp.pad(w_in[l], ((0, 0), (0, D_IN_PAD - D_IN))).astype(BF16)
    n_levels = max(1, int(math.log2(T // SSM_CHUNK)))
    ssm_tab = _ssm_tables(lam_re[l], lam_im[l], log_dt[l], b_re[l], b_im[l], c_re[l], c_im[l],
                          SSM_CHUNK, n_levels)
    cmp_tab = _compress_tables(phi_pe[l], phi_w1[l], phi_b1[l], phi_w2[l], phi_b2[l])
    w_post = dict(
        d_skip=d_skip[l].reshape(1, D_SSM), w_glu=w_glu[l].astype(BF16), b_glu=b_glu[l].reshape(1, D_SSM),
        gexp=jnp.asarray(_gate_expand_matrix(), dtype=BF16), w_out=w_out[l].astype(BF16),
        ln1_g=ln1_g[l].reshape(1, D), ln1_b=ln1_b[l].reshape(1, D),
        w_router=jnp.stack(_split_bf16(jnp.pad(w_router[l], ((0, 0), (0, LANE - N_EXPERTS))))),
        b_router=jnp.pad(b_router[l], (0, LANE - N_EXPERTS)).reshape(1, LANE))

    u, q5, kvc, kvs, kvw, g, ks, vst, kw, vwt = _mixer_in(x_prompt, mod_p[0], mod_p[1], w_in_pad, 512, True)
    y_ssm, h_p = _ssm_prompt(u, ssm_tab)
    o_cmp, o_sel, o_win = _nsa_prompt(q5, kvc, ks, vst, kw, vwt, cmp_tab, rel_bias)
    x1_p, hm_p, te_p, tw_p = _post_mixer(y_ssm, u, o_cmp, o_sel, o_win, g, x_prompt,
                                         mod_p[2], mod_p[3], mod_p[4], w_post, tm=256)

    u_s, q_s, kvc_s, kvs_s, kvw_s, g_s = _mixer_in(x_sample.reshape(1, Bs, D), mod_s[0], mod_s[1],
                                                   w_in_pad, Bs, False)
    y_s, h_s = _ssm_sample(u_s[0], state_ssm_re[l], state_ssm_im[l], ssm_tab, c_re[l], c_im[l])
    oc_s, os_s, ow_s = _nsa_sample(q_s[0].astype(F32), kvc_s[0], kvs_s[0], kvw_s[0], cache_cmp_kv[l],
                                   cache_sel_kv[l], state_win_kv[l], page_table, cmp_tab, rel_bias)
    x1_s, hm_s, te_s, tw_s = _post_mixer(y_s[None], u_s, oc_s[None], os_s[None], ow_s[None], g_s,
                                         x_sample.reshape(1, Bs, D), mod_s[2], mod_s[3], mod_s[4],
                                         w_post, tm=Bs)

    n_p = Bp * T
    n_all = n_p + Bs
    hm_all = jnp.concatenate([hm_p.reshape(n_p, D), hm_s.reshape(Bs, D)], 0)
    te_all = jnp.concatenate([te_p.reshape(n_p, LANE), te_s.reshape(Bs, LANE)], 0)[:, :TOP_K]
    row_tok, dest, items = _moe_dispatch(te_all, n_all)
    xb = jnp.concatenate([hm_all, jnp.zeros((1, D), F32)], 0)[row_tok]
    yb = _experts(xb, items, w_gate_up[l], b_gate_up[l], w_down[l], b_down[l])
    ys_p = [yb[dest[:n_p, k]].reshape(Bp, T, D) for k in range(TOP_K)]
    ys_s = [yb[dest[n_p:, k]].reshape(1, Bs, D) for k in range(TOP_K)]
    ln2g, ln2b = ln2_g[l].reshape(1, D), ln2_b[l].reshape(1, D)
    out_p = _final(x1_p, ys_p, tw_p, mod_p[5], ln2g, ln2b, tm=512)
    out_s = _final(x1_s, ys_s, tw_s, mod_s[5], ln2g, ln2b, tm=Bs)

    wlen = min(WINDOW, T)
    win_s = jnp.concatenate([state_win_kv[l], kvw_s[0].reshape(Bs, 1, *kv_tail)], 1)[:, -state_win_kv.shape[2]:]
    p_state = SSM_STATE
    return (out_p, out_s.reshape(Bs, 1, D),
            kvc.reshape(1, Bp, T, *kv_tail), kvc_s[0].reshape(1, Bs, 1, *kv_tail),
            kvs.reshape(1, Bp, T, *kv_tail), kvs_s[0].reshape(1, Bs, 1, *kv_tail),
            kvw[:, T - wlen:].reshape(1, Bp, wlen, *kv_tail), win_s[None],
            h_p[None, ..., :p_state], h_p[None, ..., p_state:],
            h_s[None, ..., :p_state], h_s[None, ..., p_state:])
```

```python
import functools
import math

import numpy as np
import jax
import jax.numpy as jnp
from jax import lax
from jax.experimental import pallas as pl
from jax.experimental.pallas import tpu as pltpu

D_MODEL = 1024
DEPTH = 1
PAST_LEN = 16384
PAGE_SIZE = 128
D_SSM = 512
SSM_GROUP = 16
N_SSM_GROUPS = D_SSM // SSM_GROUP
SSM_STATE = 64
N_HEADS = 8
HEAD_DIM = 64
N_KV_HEADS = 2
GQA = N_HEADS // N_KV_HEADS
D_ATT = N_HEADS * HEAD_DIM
D_KV = 2 * N_KV_HEADS * HEAD_DIM
CMP_STRIDE = 16
CMP_BLOCK = 2 * CMP_STRIDE
SEL_BLOCK = 64
N_SEL = 16
WINDOW = 512
NUM_BUCKETS = 32
REL_MAX_DIST = 1024
N_EXPERTS = 32
TOP_K = 4
D_FF = 1024
SWIGLU_LIMIT = 7.0
SWIGLU_ALPHA = 1.702
DN_ALPHA = (2 * DEPTH) ** 0.25
D_IN = D_SSM + D_ATT + 3 * D_KV + 3 * N_HEADS
NEG = -1e30
F32 = jnp.float32
BF16 = jnp.bfloat16
HIGHEST = lax.Precision.HIGHEST

LANE = 128
D_IN_PAD = 1920
GATE_COL = D_SSM + D_ATT + 3 * D_KV
SSM_CHUNK = 8
ATT_TQ = 128
ATT_TK = 128
SEL_CHAINS = 4
MOE_ROWS = 256
PAGES_PER_STEP = 32
PAGE_PARTS = 2
CHUNK_PITCH = 24
VMEM_LIMIT = 48 * 1024 * 1024
LN_EPS = 1e-5


def _cparams(*sem):
    return pltpu.CompilerParams(dimension_semantics=sem, vmem_limit_bytes=VMEM_LIMIT)


def _nt_dot(a, b):
    return lax.dot_general(a, b, (((1,), (1,)), ((), ())), preferred_element_type=F32)


def _layer_norm(x):
    mu = jnp.mean(x, axis=-1, keepdims=True)
    xc = x - mu
    var = jnp.mean(xc * xc, axis=-1, keepdims=True)
    return xc * lax.rsqrt(var + LN_EPS)


def _adaln_kernel(c_ref, w_ref, b_ref, o_ref):
    c = c_ref[...]
    s = c * jax.nn.sigmoid(c)
    o_ref[...] = jnp.dot(s, w_ref[...], precision=HIGHEST, preferred_element_type=F32) + b_ref[...]


def _adaln(c, w, b):
    n, d = c.shape
    dout = w.shape[1]
    tn = 1024
    return pl.pallas_call(
        _adaln_kernel,
        out_shape=jax.ShapeDtypeStruct((n, dout), F32),
        grid=(dout // tn,),
        in_specs=[pl.BlockSpec((n, d), lambda j: (0, 0)),
                  pl.BlockSpec((d, tn), lambda j: (0, j)),
                  pl.BlockSpec((1, tn), lambda j: (0, j))],
        out_specs=pl.BlockSpec((n, tn), lambda j: (0, j)),
        compiler_params=_cparams("arbitrary"),
        name="adaln",
    )(c, w, b.reshape(1, dout))


def _mixer_in_kernel(x_ref, sh_ref, sc_ref, w_ref, u_ref, q_ref, kvc_ref, kvs_ref, kvw_ref, g_ref, *att_refs):
    h = _layer_norm(x_ref[0]) * (1.0 + sc_ref[0]) + sh_ref[0]
    z = jnp.dot(h.astype(BF16), w_ref[...], preferred_element_type=F32)
    c0 = D_SSM
    c1 = c0 + D_ATT
    c2 = c1 + D_KV
    c3 = c2 + D_KV
    c4 = c3 + D_KV
    u_ref[0] = z[:, :c0]
    kvc_ref[0] = z[:, c1:c2]
    kvs_ref[0] = z[:, c2:c3]
    kvw_ref[0] = z[:, c3:c4]
    g_ref[0] = z[:, c4:c4 + LANE]
    if not att_refs:
        q_ref[0] = z[:, c0:c1].astype(BF16)
        return
    ks_ref, vst_ref, kw_ref, vwt_ref = att_refs
    hd, half = HEAD_DIM, N_KV_HEADS * HEAD_DIM
    for hq in range(N_HEADS):
        q_ref[0, hq // GQA, hq % GQA] = (z[:, c0 + hq * hd:c0 + (hq + 1) * hd] * (hd ** -0.5)).astype(BF16)
    for k_ref, vt_ref, base in ((ks_ref, vst_ref, c2), (kw_ref, vwt_ref, c3)):
        for hk in range(N_KV_HEADS):
            k_ref[0, hk] = z[:, base + hk * hd:base + (hk + 1) * hd].astype(BF16)
        vt = z[:, base + half:base + 2 * half].T
        vt_ref[0] = vt.reshape(N_KV_HEADS, hd, vt.shape[1]).astype(BF16)


def _mixer_in(x, shift, scale, w_pad, tm, attention_layouts):
    B, T, D = x.shape
    R = shift.shape[1]
    rb = 1 if R == 1 else tm
    mod_map = (lambda b, i: (b, 0, 0)) if R == 1 else (lambda b, i: (b, i, 0))
    row = lambda n: pl.BlockSpec((1, tm, n), lambda b, i: (b, i, 0))
    f32 = lambda n: jax.ShapeDtypeStruct((B, T, n), F32)
    if attention_layouts:
        q_shape = jax.ShapeDtypeStruct((B, N_KV_HEADS, GQA, T, HEAD_DIM), BF16)
        q_spec = pl.BlockSpec((1, N_KV_HEADS, GQA, tm, HEAD_DIM), lambda b, i: (b, 0, 0, i, 0))
        k_shape = jax.ShapeDtypeStruct((B, N_KV_HEADS, T, HEAD_DIM), BF16)
        k_spec = pl.BlockSpec((1, N_KV_HEADS, tm, HEAD_DIM), lambda b, i: (b, 0, i, 0))
        vt_shape = jax.ShapeDtypeStruct((B, N_KV_HEADS, HEAD_DIM, T), BF16)
        vt_spec = pl.BlockSpec((1, N_KV_HEADS, HEAD_DIM, tm), lambda b, i: (b, 0, 0, i))
        extra_shapes, extra_specs = (k_shape, vt_shape, k_shape, vt_shape), (k_spec, vt_spec, k_spec, vt_spec)
    else:
        q_shape, q_spec = jax.ShapeDtypeStruct((B, T, D_ATT), BF16), row(D_ATT)
        extra_shapes, extra_specs = (), ()
    return pl.pallas_call(
        _mixer_in_kernel,
        out_shape=(f32(D_SSM), q_shape, f32(D_KV), f32(D_KV), f32(D_KV), f32(LANE)) + extra_shapes,
        grid=(B, T // tm),
        in_specs=[row(D), pl.BlockSpec((1, rb, D), mod_map), pl.BlockSpec((1, rb, D), mod_map),
                  pl.BlockSpec((D, D_IN_PAD), lambda b, i: (0, 0))],
        out_specs=(row(D_SSM), q_spec, row(D_KV), row(D_KV), row(D_KV), row(LANE)) + extra_specs,
        compiler_params=_cparams("parallel", "parallel"),
        name="mixer_in",
    )(x, shift, scale, w_pad)


def _ssm_tables(lam_re, lam_im, log_dt, b_re, b_im, c_re, c_im, L, n_levels):
    G, P = lam_re.shape
    C = b_re.shape[-1]
    dt = jnp.exp(log_dt.astype(F32))[:, None]
    er, ei = lam_re * dt, lam_im * dt

    def power(k):
        kk = k.astype(F32)[:, None, None]
        mag = jnp.exp(kk * er)
        return mag * jnp.cos(kk * ei), mag * jnp.sin(kk * ei)

    lb_re, lb_im = power(jnp.ones((1,), F32))
    nr, ni = lb_re[0] - 1.0, lb_im[0]
    den = lam_re * lam_re + lam_im * lam_im
    fr = (nr * lam_re + ni * lam_im) / den
    fi = (ni * lam_re - nr * lam_im) / den
    bbr = fr[:, :, None] * b_re - fi[:, :, None] * b_im
    bbi = fr[:, :, None] * b_im + fi[:, :, None] * b_re
    pr, pi = power(jnp.arange(L + 1))
    clr = c_re[None] * pr[:, :, None, :] - c_im[None] * pi[:, :, None, :]
    cli = c_re[None] * pi[:, :, None, :] + c_im[None] * pr[:, :, None, :]
    kern = (jnp.einsum('kgcp,gpd->kgcd', clr[:L], bbr, precision=HIGHEST)
            - jnp.einsum('kgcp,gpd->kgcd', cli[:L], bbi, precision=HIGHEST))
    GP = LANE // C
    X = G // GP
    eye = jnp.eye(GP, dtype=BF16)
    ts = np.arange(L)
    place = jnp.asarray(ts[None, :, None] - ts[:, None, None] == ts[None, None, :], dtype=BF16)
    place_einsum = functools.partial(jnp.einsum, preferred_element_type=BF16)
    toep = place_einsum('stk,kxhcd,hj->xshdtjc', place, kern.astype(BF16).reshape(L, X, GP, C, C), eye)
    toep = toep.reshape(X, L * LANE, L * LANE)
    prr, pir = pr[:L][::-1], pi[:L][::-1]
    ws2 = jnp.stack([prr[..., None] * bbr[None] - pir[..., None] * bbi[None],
                     prr[..., None] * bbi[None] + pir[..., None] * bbr[None]])
    ws = place_einsum('rsxhpd,hj->xshdrjp', ws2.astype(BF16).reshape(2, L, X, GP, P, C), eye)
    ws = ws.reshape(X, L * LANE, 2 * GP * P)
    wy2 = jnp.stack([clr[1:], -cli[1:]])
    wy = place_einsum('rtxhcp,hj->xrhptjc', wy2.astype(BF16).reshape(2, L, X, GP, C, P), eye)
    wy = wy.reshape(X, 2 * GP * P, L * LANE)
    lr, li = power(L * (2 ** jnp.arange(n_levels)))
    lr, li = lr.reshape(n_levels, X, GP * P), li.reshape(n_levels, X, GP * P)
    ar = jnp.transpose(jnp.concatenate([lr, lr], -1), (1, 0, 2))
    ai = jnp.transpose(jnp.concatenate([-li, li], -1), (1, 0, 2))
    return toep, ws, wy, ar, ai, (lb_re[0], lb_im[0], bbr, bbi)


def _ssm_kernel(u_ref, toep_ref, ws_ref, wy_ref, ar_ref, ai_ref, y_ref, hl_ref, *, L, nc, n_levels):
    u = jnp.concatenate([u_ref[0, pl.ds(t, nc, stride=L), :] for t in range(L)], axis=1).astype(BF16)
    y1 = jnp.dot(u, toep_ref[0], preferred_element_type=F32)
    h = jnp.dot(u, ws_ref[0], preferred_element_type=F32)
    w2 = h.shape[-1]
    rows = lax.broadcasted_iota(jnp.int32, (nc, w2), 0)
    for k in range(n_levels):
        d = 1 << k
        sh = jnp.where(rows >= d, pltpu.roll(h, d, axis=0), 0.0)
        sw = pltpu.roll(sh, w2 // 2, axis=1)
        h = h + ar_ref[0, k:k + 1, :] * sh + ai_ref[0, k:k + 1, :] * sw
    hl_ref[0, 0] = h[nc - 1:nc, :]
    hp = jnp.where(rows >= 1, pltpu.roll(h, 1, axis=0), 0.0)
    y = y1 + jnp.dot(hp.astype(BF16), wy_ref[0], preferred_element_type=F32)
    for t in range(L):
        y_ref[0, pl.ds(t, nc, stride=L), :] = y[:, t * LANE:(t + 1) * LANE]


def _ssm_prompt(u, tables):
    toep, ws, wy, ar, ai, _ = tables
    B, T, _ = u.shape
    L, P = SSM_CHUNK, SSM_STATE
    X, n_levels, w2 = ar.shape
    GP = w2 // (2 * P)
    nc = T // L
    tab = lambda a: pl.BlockSpec((1,) + a.shape[1:], lambda x, b: (x, 0, 0))
    seq = pl.BlockSpec((1, T, LANE), lambda x, b: (b, 0, x))
    y, hl = pl.pallas_call(
        functools.partial(_ssm_kernel, L=L, nc=nc, n_levels=n_levels),
        out_shape=(jax.ShapeDtypeStruct((B, T, D_SSM), F32), jax.ShapeDtypeStruct((X, B, 1, w2), F32)),
        grid=(X, B),
        in_specs=[seq, tab(toep), tab(ws), tab(wy), tab(ar), tab(ai)],
        out_specs=(seq, pl.BlockSpec((1, 1, 1, w2), lambda x, b: (x, b, 0, 0))),
        compiler_params=_cparams("parallel", "parallel"),
        name="ssm_prompt",
    )(u, toep, ws, wy, ar, ai)
    hl = jnp.transpose(hl.reshape(X, B, 2, GP, P), (1, 0, 3, 2, 4))
    return y, hl.reshape(B, X * GP, 2 * P)


def _ssm_step_kernel(u_ref, h0_ref, bb_ref, lr_ref, li_ref, cy_ref, y_ref, h_ref):
    p = lr_ref.shape[-1] // 2
    bu = jnp.einsum('gbc,gcp->gbp', u_ref[...], bb_ref[...], preferred_element_type=F32)
    h0 = h0_ref[...]
    h0s = jnp.concatenate([h0[..., p:], h0[..., :p]], axis=-1)
    h = lr_ref[...] * h0 + li_ref[...] * h0s + bu
    h_ref[...] = h
    y_ref[...] = jnp.einsum('gbp,gpc->gbc', h.astype(BF16), cy_ref[...], preferred_element_type=F32)


def _ssm_sample(u, h0_re, h0_im, tables, c_re, c_im):
    lb_re, lb_im, bbr, bbi = tables[-1]
    B = u.shape[0]
    G, C, P = N_SSM_GROUPS, SSM_GROUP, SSM_STATE
    ug = jnp.transpose(u.reshape(B, G, C), (1, 0, 2)).astype(BF16)
    h0 = jnp.transpose(jnp.concatenate([h0_re, h0_im], -1), (1, 0, 2)).astype(F32)
    bb = jnp.concatenate([jnp.transpose(bbr, (0, 2, 1)), jnp.transpose(bbi, (0, 2, 1))], -1).astype(BF16)
    lr = jnp.concatenate([lb_re, lb_re], -1)[:, None, :]
    li = jnp.concatenate([-lb_im, lb_im], -1)[:, None, :]
    cy = jnp.concatenate([jnp.transpose(c_re, (0, 2, 1)), -jnp.transpose(c_im, (0, 2, 1))], 1).astype(BF16)
    y, h = pl.pallas_call(
        _ssm_step_kernel,
        out_shape=(jax.ShapeDtypeStruct((G, B, C), F32), jax.ShapeDtypeStruct((G, B, 2 * P), F32)),
        name="ssm_step",
    )(ug, h0, bb, lr, li, cy)
    return jnp.transpose(y, (1, 0, 2)).reshape(B, D_SSM), jnp.transpose(h, (1, 0, 2))


def _compress_tables(phi_pe, phi_w1, phi_b1, phi_w2, phi_b2):
    S, H, Dh = CMP_STRIDE, N_KV_HEADS, HEAD_DIM
    w1 = phi_w1.reshape(2, 2, S, Dh, Dh)
    eye_c = jnp.eye(2, dtype=F32)
    eye_h = jnp.eye(H, dtype=F32)
    wbig = jnp.einsum('cajde,xc,yh->jxydache', w1, eye_c, eye_h).reshape(S * 2 * H * Dh, 2 * 2 * H * Dh)
    pe = jnp.transpose(phi_pe.reshape(2, 2, S, Dh), (1, 2, 0, 3))
    pe_rows = jnp.broadcast_to(pe[:, :, :, None, :], (2, S, 2, H, Dh)).reshape(2, S * 2 * H * Dh)
    n = 2 * H * Dh
    pe_w = (jnp.dot(pe_rows[0], wbig[:, :n], precision=HIGHEST) + jnp.dot(pe_rows[1], wbig[:, n:], precision=HIGHEST))
    b1 = jnp.broadcast_to(phi_b1[:, None, :], (2, H, Dh)).reshape(1, n) + pe_w[None, :]
    w2 = jnp.einsum('cef,cx,hy->chexyf', phi_w2, eye_c, eye_h).reshape(n, n)
    b2 = jnp.broadcast_to(phi_b2[:, None, :], (2, H, Dh)).reshape(1, n)
    return wbig.astype(BF16), b1, w2.astype(BF16), b2


def _compress_in_kernel(x_ref, w_ref, z_ref):
    z_ref[0] = jnp.dot(x_ref[0].astype(BF16), w_ref[...], preferred_element_type=F32)


def _compress_in(x2, tables):
    wbig = tables[0]
    N2 = wbig.shape[1]
    B, n, K = x2.shape
    tr = math.gcd(n, 256)
    return pl.pallas_call(
        _compress_in_kernel,
        out_shape=jax.ShapeDtypeStruct((B, n, N2), F32),
        grid=(B, n // tr),
        in_specs=[pl.BlockSpec((1, tr, K), lambda b, i: (b, i, 0)),
                  pl.BlockSpec((K, N2), lambda b, i: (0, 0))],
        out_specs=pl.BlockSpec((1, tr, N2), lambda b, i: (b, i, 0)),
        compiler_params=_cparams("parallel", "parallel"),
        name="compress_in",
    )(x2, wbig)


def _compress_in_paged_kernel(pt_ref, *refs, n_pg):
    x_refs = refs[:n_pg]
    w_ref, z_ref = refs[n_pg:n_pg + 2]
    scratch = refs[n_pg + 2:]
    n_slab = D_KV // LANE
    pg_part = n_pg // PAGE_PARTS
    cpp = PAGE_SIZE // CMP_STRIDE
    rows = pg_part * cpp
    for part in range(PAGE_PARTS):
        s_refs = scratch[part * n_slab:(part + 1) * n_slab]
        for k in range(pg_part):
            t = x_refs[part * pg_part + k][0].reshape(D_KV, PAGE_SIZE).T
            for c, s_ref in enumerate(s_refs):
                for n in range(cpp):
                    r0 = (k * cpp + n) * CHUNK_PITCH
                    s_ref[r0:r0 + CMP_STRIDE, :] = t[n * CMP_STRIDE:(n + 1) * CMP_STRIDE, c * LANE:(c + 1) * LANE]
        z = jnp.zeros((rows, w_ref.shape[1]), F32)
        for j in range(CMP_STRIDE):
            xj = jnp.concatenate([s_ref[pl.ds(j, rows, stride=CHUNK_PITCH), :] for s_ref in s_refs], axis=1)
            z = z + jnp.dot(xj.astype(BF16), w_ref[j * D_KV:(j + 1) * D_KV, :], preferred_element_type=F32)
        z_ref[0, part * rows:(part + 1) * rows, :] = z


def _compress_in_paged(pool_t, page_table, tables):
    wbig = tables[0]
    N2 = wbig.shape[1]
    K = wbig.shape[0]
    B, n_pages = page_table.shape
    n_pg = math.gcd(n_pages, PAGES_PER_STEP)
    rows = n_pg * PAGE_SIZE // CMP_STRIDE
    page_spec = lambda k: pl.BlockSpec((1,) + pool_t.shape[1:],
                                       lambda b, i, pt, k=k: (pt[b, i * n_pg + k], 0, 0, 0, 0))
    grid_spec = pltpu.PrefetchScalarGridSpec(
        num_scalar_prefetch=1,
        grid=(B, n_pages // n_pg),
        in_specs=[page_spec(k) for k in range(n_pg)] + [pl.BlockSpec((K, N2), lambda b, i, pt: (0, 0))],
        out_specs=pl.BlockSpec((1, rows, N2), lambda b, i, pt: (b, i, 0)),
        scratch_shapes=[pltpu.VMEM((rows // PAGE_PARTS * CHUNK_PITCH, LANE), F32)
                        for _ in range(PAGE_PARTS * (D_KV // LANE))],
    )
    return pl.pallas_call(
        functools.partial(_compress_in_paged_kernel, n_pg=n_pg),
        out_shape=jax.ShapeDtypeStruct((B, n_pages * PAGE_SIZE // CMP_STRIDE, N2), F32),
        grid_spec=grid_spec,
        compiler_params=_cparams("arbitrary", "arbitrary"),
        name="compress_in_paged",
    )(page_table, *([pool_t] * n_pg), wbig)


def _compress_out_kernel(*refs):
    z_refs, (b1_ref, w2_ref, b2_ref, o_ref) = refs[:-4], refs[-4:]
    z = jnp.concatenate([z_ref[0] for z_ref in z_refs], axis=0)
    n = z.shape[-1] // 2
    rows = z.shape[0]
    second = pltpu.roll(z[:, n:], rows - 1, axis=0)
    hdn = jax.nn.gelu(z[:, :n] + second + b1_ref[...])
    o_ref[0, :rows, :] = jnp.dot(hdn.astype(BF16), w2_ref[...], preferred_element_type=F32) + b2_ref[...]
    if o_ref.shape[1] > rows:
        o_ref[0, rows:, :] = jnp.zeros((o_ref.shape[1] - rows, n), F32)


def _compress_out(zs, tables, n_out):
    _, b1, w2, b2 = tables
    B, _, N2 = zs[0].shape
    return pl.pallas_call(
        _compress_out_kernel,
        out_shape=jax.ShapeDtypeStruct((B, n_out, N2 // 2), F32),
        grid=(B,),
        in_specs=[pl.BlockSpec((1, z.shape[1], N2), lambda b: (b, 0, 0)) for z in zs] + [
                  pl.BlockSpec((1, N2 // 2), lambda b: (0, 0)),
                  pl.BlockSpec((N2 // 2, N2 // 2), lambda b: (0, 0)),
                  pl.BlockSpec((1, N2 // 2), lambda b: (0, 0))],
        out_specs=pl.BlockSpec((1, n_out, N2 // 2), lambda b: (b, 0, 0)),
        compiler_params=_cparams("parallel"),
        name="compress_out",
    )(*zs, b1, w2, b2)


def _rel_bucket(dist):
    n = jnp.maximum(dist, 0)
    max_exact = NUM_BUCKETS // 2
    nf = jnp.maximum(n, 1).astype(F32)
    large = max_exact + (jnp.log(nf / max_exact) / math.log(REL_MAX_DIST / max_exact)
                         * (NUM_BUCKETS - max_exact)).astype(jnp.int32)
    large = jnp.minimum(large, NUM_BUCKETS - 1)
    return jnp.where(n < max_exact, n, large)


def _bias_by_distance(rel_bias, n_max):
    onehot = (_rel_bucket(jnp.arange(n_max))[None, :] == jnp.arange(NUM_BUCKETS)[:, None]).astype(F32)
    return jnp.dot(jnp.transpose(rel_bias.astype(F32)), onehot, precision=HIGHEST)


def _shifted_chunks(bias_n, pad, n_chunks, width):
    n = min(bias_n.shape[1], n_chunks * width - pad)
    ext = jnp.concatenate([jnp.broadcast_to(bias_n[:, :1], (N_HEADS, pad)), bias_n[:, :n],
                           jnp.zeros((N_HEADS, n_chunks * width - pad - n), F32)], axis=1)
    return ext.reshape(N_HEADS, n_chunks, width)


def _bias_tables_kernel(ed_ref, ec_ref, tzs_ref, tzw_ref, cmp_ref, *, tq, tk, n_qt):
    n_ds, n_dw, n_j = tzs_ref.shape[1] - 1, tzw_ref.shape[1] - 1, cmp_ref.shape[1] // 8
    tzs_ref[0, n_ds] = jnp.full((tk, tq), NEG, F32)
    tzw_ref[0, n_dw] = jnp.full((tk, tq), NEG, F32)
    w = tq + tk
    c = lax.broadcasted_iota(jnp.int32, (tk, tq), 0)
    r = lax.broadcasted_iota(jnp.int32, (tk, tq), 1)
    for d in range(n_ds):
        v = jnp.concatenate([ed_ref[0, d:d + 1, :], ed_ref[0, d + 1:d + 2, :]], axis=1)
        t = pltpu.roll(jnp.broadcast_to(v, (tk, w)), w - (tk - 1), axis=1, stride=1, stride_axis=0)[:, :tq]
        dist = d * tk + r - c
        tzs_ref[0, d] = jnp.where(dist >= 0, t, NEG)
        if d < n_dw:
            tzw_ref[0, d] = jnp.where((dist >= 0) & (dist <= WINDOW), t, NEG)
    for j in range(n_j):
        dd = n_qt - 1 - j
        c0, c1 = max(dd, 0), max(dd + 1, 0)
        v = jnp.concatenate([ec_ref[0, c0:c0 + 1, :], ec_ref[0, c1:c1 + 1, :]], axis=1)
        t = pltpu.roll(jnp.broadcast_to(v, (8, w)), w - 7 * CMP_STRIDE, axis=1, stride=CMP_STRIDE, stride_axis=0)
        cmp_ref[0, j * 8:(j + 1) * 8, :] = t[:, :tq]


def _bias_tables(bias_n, n_qt, n_rb, n_ds, n_dw, tq, tk):
    assert tq == tk == 8 * CMP_STRIDE and n_dw <= n_ds
    n_j = n_rb + n_qt - 1
    ed = _shifted_chunks(bias_n, tk - 1, n_ds + 1, tq)
    ec = _shifted_chunks(bias_n, 7 * CMP_STRIDE + CMP_BLOCK - 1, n_qt + 1, tq)
    head = lambda a: pl.BlockSpec((1,) + a.shape[1:], lambda h: (h,) + (0,) * (a.ndim - 1))
    outs = (jax.ShapeDtypeStruct((N_HEADS, n_ds + 1, tk, tq), F32),
            jax.ShapeDtypeStruct((N_HEADS, n_dw + 1, tk, tq), F32),
            jax.ShapeDtypeStruct((N_HEADS, n_j * 8, tq), F32))
    tzs, tzw, cmp = pl.pallas_call(
        functools.partial(_bias_tables_kernel, tq=tq, tk=tk, n_qt=n_qt),
        out_shape=outs,
        grid=(N_HEADS,),
        in_specs=[head(ed), head(ec)],
        out_specs=tuple(head(o) for o in outs),
        compiler_params=_cparams("parallel"),
        name="bias_tables",
    )(ed, ec)
    grp = lambda a: a.reshape((N_KV_HEADS, GQA) + a.shape[1:])
    return grp(tzs), grp(tzw), cmp


def _pool_matrix(n_cmp_pad, n_blk_pad):
    r = SEL_BLOCK // CMP_STRIDE
    i = np.arange(n_cmp_pad)[None, :]
    j = np.arange(n_blk_pad)[:, None]
    return ((i >= r * j - 1) & (i <= r * j + r - 1)).astype(np.float32)


def _cmp_select_kernel(q_ref, k_ref, vt_ref, bias_ref, pool_ref, o_ref, sel_ref, *, tq, n_cmp):
    qt = pl.program_id(2)
    n_qt = pl.num_programs(2)
    q = q_ref[0, 0].reshape(GQA * tq, HEAD_DIM)
    k = k_ref[0, 0]
    nc = k.shape[0]
    s = _nt_dot(k, q)
    row0 = pl.multiple_of((n_qt - 1 - qt) * 8, 8)
    s = s + jnp.concatenate([bias_ref[g, pl.ds(row0, nc), :] for g in range(GQA)], axis=-1)
    t_pos = qt * tq + (lax.broadcasted_iota(jnp.int32, (nc, GQA * tq), 1) % tq)
    ci = lax.broadcasted_iota(jnp.int32, (nc, GQA * tq), 0)
    mask = (ci * CMP_STRIDE + CMP_BLOCK - 1 <= t_pos) & (ci < n_cmp)
    s = jnp.where(mask, s, NEG)
    m = jnp.max(s, axis=0, keepdims=True)
    p = jnp.where(mask, jnp.exp(s - m), 0.0)
    p = p / jnp.maximum(jnp.sum(p, axis=0, keepdims=True), 1e-30)
    ot = jnp.dot(vt_ref[0, 0], p.astype(BF16), preferred_element_type=F32)
    o_ref[0] = jnp.concatenate([ot[:, g * tq:(g + 1) * tq].T for g in range(GQA)], axis=-1)
    imp = p[:, 0:tq]
    for g in range(1, GQA):
        imp = imp + p[:, g * tq:(g + 1) * tq]
    sb = jnp.dot(pool_ref[...], imp, precision=HIGHEST, preferred_element_type=F32)
    nb = sb.shape[0]
    blk = lax.broadcasted_iota(jnp.int32, (nb, tq), 0)
    cur = (qt * tq + lax.broadcasted_iota(jnp.int32, (nb, tq), 1)) // SEL_BLOCK
    causal = blk <= cur
    forced = (blk == 0) | (blk == cur) | (blk == cur - 1)
    sc = jnp.where(forced & causal, 1e4, jnp.where(causal, sb, -1.0))
    groups = [sc[r:r + 8] for r in range(0, nb, 8)]
    sub = lax.broadcasted_iota(jnp.int32, (8, tq), 0)
    ranks = [jnp.zeros((8, tq), F32) for _ in groups]
    for i in range(nb):
        row = sc[i:i + 1, :]
        for gi, grp in enumerate(groups):
            if gi * 8 > i:
                ahead = row >= grp
            elif gi * 8 + 7 < i:
                ahead = row > grp
            else:
                ahead = (row > grp) | ((row == grp) & (sub > i - gi * 8))
            ranks[gi] = ranks[gi] + jnp.where(ahead, 1.0, 0.0)
    rank = jnp.concatenate(ranks, axis=0)
    sel_ref[0, 0] = jnp.where((rank < N_SEL) & causal, 0.0, NEG)


def _cmp_select_prompt(q5, kc, vct, bias_tab, pool, n_cmp):
    B, _, _, T, _ = q5.shape
    NC = kc.shape[2]
    NB = pool.shape[0]
    R = bias_tab.shape[1]
    tq = ATT_TQ
    return pl.pallas_call(
        functools.partial(_cmp_select_kernel, tq=tq, n_cmp=n_cmp),
        out_shape=(jax.ShapeDtypeStruct((B, T, D_ATT), F32),
                   jax.ShapeDtypeStruct((B, N_KV_HEADS, NB, T), F32)),
        grid=(B, N_KV_HEADS, T // tq),
        in_specs=[pl.BlockSpec((1, 1, GQA, tq, HEAD_DIM), lambda b, h, i: (b, h, 0, i, 0)),
                  pl.BlockSpec((1, 1, NC, HEAD_DIM), lambda b, h, i: (b, h, 0, 0)),
                  pl.BlockSpec((1, 1, HEAD_DIM, NC), lambda b, h, i: (b, h, 0, 0)),
                  pl.BlockSpec((GQA, R, tq), lambda b, h, i: (h, 0, 0)),
                  pl.BlockSpec((NB, NC), lambda b, h, i: (0, 0))],
        out_specs=(pl.BlockSpec((1, tq, GQA * HEAD_DIM), lambda b, h, i: (b, i, h)),
                   pl.BlockSpec((1, 1, NB, tq), lambda b, h, i: (b, h, 0, i))),
        compiler_params=_cparams("parallel", "parallel", "parallel"),
        name="cmp_select_prompt",
    )(q5, kc, vct, bias_tab, pool)


def _sel_win_kernel(q_ref, ks_ref, vst_ref, kw_ref, vwt_ref, sel_ref, tzs_ref, tzw_ref, os_ref, ow_ref, *, tq):
    tk = ATT_TK
    qt = pl.program_id(2)
    q = q_ref[0, 0].reshape(GQA * tq, HEAD_DIM)
    width = GQA * tq
    per_tile = tk // SEL_BLOCK

    def make_sweep(k_ref, vt_ref, tz_ref, use_sel, n_chains, single_trip):
        n_d = tz_ref.shape[2] - 1

        def scores(kt, hi):
            pad = kt > hi
            kt = jnp.minimum(kt, hi)
            off = pl.multiple_of(kt * tk, tk)
            k = k_ref[0, 0, pl.ds(off, tk), :]
            d = jnp.where(pad, n_d, jnp.minimum(qt - kt, n_d - 1))
            bias = [tz_ref[0, g, d] for g in range(GQA)]
            if use_sel:
                rows = sel_ref[0, 0, pl.ds(kt * per_tile, per_tile), :]
                selb = jnp.concatenate([jnp.broadcast_to(rows[i:i + 1], (SEL_BLOCK, tq))
                                        for i in range(per_tile)], axis=0)
                bias = [b + selb for b in bias]
            return _nt_dot(k, q) + jnp.concatenate(bias, axis=1)

        def values_t(kt, lo, hi):
            off = pl.multiple_of(jnp.clip(kt, lo, hi) * tk, tk)
            return vt_ref[0, 0, :, pl.ds(off, tk)]

        def sweep(lo, hi):
            n_trips = (hi - lo + n_chains) // n_chains
            chain0 = (jnp.full((1, width), 0.5 * NEG, F32), jnp.zeros((1, width), F32),
                      jnp.zeros((HEAD_DIM, width), F32), jnp.ones((1, width), F32), jnp.zeros((tk, width), BF16))

            def trip(i, chains):
                kt = lo + n_chains * i
                pv = [jnp.dot(values_t(kt - n_chains + c, lo, hi), chains[c][4], preferred_element_type=F32)
                      for c in range(n_chains)]
                ss = [scores(kt + c, hi) for c in range(n_chains)]
                out = []
                for c in range(n_chains):
                    m, l, acc, alpha_prev, _ = chains[c]
                    m_new = jnp.maximum(m, jnp.max(ss[c], axis=0, keepdims=True))
                    alpha = jnp.exp(m - m_new)
                    p = jnp.exp(ss[c] - m_new)
                    l = alpha * l + jnp.sum(p, axis=0, keepdims=True)
                    out.append((m_new, l, alpha_prev * acc + pv[c], alpha, p.astype(BF16)))
                return tuple(out)

            if single_trip:
                done = []
                for c in range(n_chains):
                    s = scores(lo + c, hi)
                    m = jnp.maximum(jnp.max(s, axis=0, keepdims=True), 0.5 * NEG)
                    p = jnp.exp(s - m)
                    done.append((m, jnp.sum(p, axis=0, keepdims=True),
                                 jnp.dot(values_t(lo + c, lo, hi), p.astype(BF16), preferred_element_type=F32)))
            else:
                chains = lax.fori_loop(0, n_trips, trip, (chain0,) * n_chains)
                kt_last = lo + n_chains * (n_trips - 1)
                done = []
                for c in range(n_chains):
                    m, l, acc, alpha, p = chains[c]
                    done.append((m, l, alpha * acc + jnp.dot(values_t(kt_last + c, lo, hi), p,
                                                              preferred_element_type=F32)))
            m_all = functools.reduce(jnp.maximum, [m for m, _, _ in done])
            num = den = 0.0
            for m, l, acc in done:
                e = jnp.exp(m - m_all)
                num = num + acc * e
                den = den + l * e
            o = num / jnp.maximum(den, 1e-30)
            return jnp.concatenate([o[:, g * tq:(g + 1) * tq].T for g in range(GQA)], axis=-1)
        return sweep

    n_win = tzw_ref.shape[2] - 1
    os_ref[0] = make_sweep(ks_ref, vst_ref, tzs_ref, True, SEL_CHAINS, False)(0, qt)
    ow_ref[0] = make_sweep(kw_ref, vwt_ref, tzw_ref, False, n_win, True)(jnp.maximum(qt - (n_win - 1), 0), qt)


def _sel_win_prompt(q5, ks, vst, kw, vwt, sel, tzs, tzw):
    B, _, _, T, _ = q5.shape
    NB = sel.shape[2]
    tq = ATT_TQ
    k_spec = pl.BlockSpec((1, 1, T, HEAD_DIM), lambda b, h, i: (b, h, 0, 0))
    vt_spec = pl.BlockSpec((1, 1, HEAD_DIM, T), lambda b, h, i: (b, h, 0, 0))
    tz_spec = lambda tz: pl.BlockSpec((1,) + tz.shape[1:], lambda b, h, i: (h, 0, 0, 0, 0))
    o_spec = pl.BlockSpec((1, tq, GQA * HEAD_DIM), lambda b, h, i: (b, i, h))
    return pl.pallas_call(
        functools.partial(_sel_win_kernel, tq=tq),
        out_shape=(jax.ShapeDtypeStruct((B, T, D_ATT), F32), jax.ShapeDtypeStruct((B, T, D_ATT), F32)),
        grid=(B, N_KV_HEADS, T // tq),
        in_specs=[pl.BlockSpec((1, 1, GQA, tq, HEAD_DIM), lambda b, h, i: (b, h, 0, i, 0)),
                  k_spec, vt_spec, k_spec, vt_spec,
                  pl.BlockSpec((1, 1, NB, tq), lambda b, h, i: (b, h, 0, i)),
                  tz_spec(tzs), tz_spec(tzw)],
        out_specs=(o_spec, o_spec),
        compiler_params=_cparams("parallel", "parallel", "parallel"),
        name="sel_win_prompt",
    )(q5, ks, vst, kw, vwt, sel, tzs, tzw)


def _gate_expand_matrix():
    m = np.zeros((3, 2 * LANE, D_ATT), np.float32)
    for r in range(3):
        for h in range(N_HEADS):
            m[r, h * 3 + r, h * HEAD_DIM:(h + 1) * HEAD_DIM] = 1.0
            m[r, LANE + h * 3 + r, h * HEAD_DIM:(h + 1) * HEAD_DIM] = 1.0
    return m


def _split_bf16(x):
    hi = x.astype(BF16)
    return hi, (x - hi.astype(F32)).astype(BF16)


def _post_mixer_kernel(y_ref, u_ref, oc_ref, os_ref, ow_ref, g_ref, x_ref, gate_ref, sh_ref, sc_ref,
                       dskip_ref, wglu_ref, bglu_ref, gexp_ref, wout_ref, lng_ref, lnb_ref,
                       wr_ref, br_ref, x1_ref, hm_ref, te_ref, tw_ref):
    y = y_ref[0] + dskip_ref[...] * u_ref[0]
    gl = jax.nn.gelu(y)
    ssm = gl * jax.nn.sigmoid(jnp.dot(gl.astype(BF16), wglu_ref[...], preferred_element_type=F32)
                              + bglu_ref[...])
    sg = jnp.concatenate(_split_bf16(jax.nn.sigmoid(g_ref[0])), axis=1)
    att = jnp.zeros_like(oc_ref[0])
    for r, o_ref in enumerate((oc_ref, os_ref, ow_ref)):
        att = att + jnp.dot(sg, gexp_ref[r], preferred_element_type=F32) * o_ref[0]
    h = (jnp.dot(ssm.astype(BF16), wout_ref[:D_SSM, :], preferred_element_type=F32)
         + jnp.dot(att.astype(BF16), wout_ref[D_SSM:, :], preferred_element_type=F32))
    z = DN_ALPHA * x_ref[0] + gate_ref[0] * h
    x1 = _layer_norm(z) * lng_ref[...] + lnb_ref[...]
    x1_ref[0] = x1
    hm = _layer_norm(x1) * (1.0 + sc_ref[0]) + sh_ref[0]
    hm_ref[0] = hm
    hm_hi, hm_lo = _split_bf16(hm)
    logits = (jnp.dot(hm_hi, wr_ref[0], preferred_element_type=F32)
              + jnp.dot(hm_lo, wr_ref[0], preferred_element_type=F32)
              + jnp.dot(hm_hi, wr_ref[1], preferred_element_type=F32)) + br_ref[...]
    lane = lax.broadcasted_iota(jnp.int32, logits.shape, 1)
    work = jnp.where(lane < N_EXPERTS, logits, -jnp.inf)
    te = jnp.zeros(logits.shape, jnp.int32)
    tv = jnp.zeros(logits.shape, F32)
    for k in range(TOP_K):
        best = jnp.max(work, axis=-1, keepdims=True)
        arg = jnp.min(jnp.where(work == best, lane, LANE), axis=-1, keepdims=True)
        te = jnp.where(lane == k, arg, te)
        tv = jnp.where(lane == k, best, tv)
        work = jnp.where(lane == arg, -jnp.inf, work)
    ex = jnp.where(lane < TOP_K, jnp.exp(tv - tv[:, 0:1]), 0.0)
    te_ref[0] = te
    tw_ref[0] = ex / jnp.sum(ex, axis=-1, keepdims=True)


def _post_mixer(y, u, oc, osel, ow, g, x, gate, shift, scale, w, tm):
    B, T, D = x.shape
    R = gate.shape[1]
    rb = 1 if R == 1 else tm
    mod_map = (lambda b, i: (b, 0, 0)) if R == 1 else (lambda b, i: (b, i, 0))
    row = lambda n: pl.BlockSpec((1, tm, n), lambda b, i: (b, i, 0))
    mod = pl.BlockSpec((1, rb, D), mod_map)
    full = lambda a: pl.BlockSpec(a.shape, lambda b, i: (0,) * a.ndim)
    consts = (w['d_skip'], w['w_glu'], w['b_glu'], w['gexp'], w['w_out'], w['ln1_g'], w['ln1_b'],
              w['w_router'], w['b_router'])
    return pl.pallas_call(
        _post_mixer_kernel,
        out_shape=(jax.ShapeDtypeStruct((B, T, D), F32), jax.ShapeDtypeStruct((B, T, D), F32),
                   jax.ShapeDtypeStruct((B, T, LANE), jnp.int32), jax.ShapeDtypeStruct((B, T, LANE), F32)),
        grid=(B, T // tm),
        in_specs=[row(D_SSM), row(D_SSM), row(D_ATT), row(D_ATT), row(D_ATT), row(LANE), row(D),
                  mod, mod, mod] + [full(a) for a in consts],
        out_specs=(row(D), row(D), row(LANE), row(LANE)),
        compiler_params=_cparams("parallel", "parallel"),
        name="post_mixer",
    )(y, u, oc, osel, ow, g, x, gate, shift, scale, *consts)


def _expert_kernel(e_ref, blk_ref, lo_ref, hi_ref, first_ref, x_ref, wgu_ref, bgu_ref, wd_ref, bd_ref, o_ref,
                   wgu_s, wd_s):
    i = pl.program_id(0)
    fresh = (i == 0) | (e_ref[i] != e_ref[jnp.maximum(i - 1, 0)])

    @pl.when(fresh)
    def _():
        wgu_s[...] = wgu_ref[0].astype(BF16)
        wd_s[...] = wd_ref[0].astype(BF16)

    @pl.when(first_ref[i] == 1)
    def _():
        o_ref[...] = jnp.zeros_like(o_ref)

    @pl.when(hi_ref[i] > lo_ref[i])
    def _():
        gu = jnp.dot(x_ref[...].astype(BF16), wgu_s[...], preferred_element_type=F32) + bgu_ref[0]
        gate = jnp.minimum(gu[:, :D_FF], SWIGLU_LIMIT)
        up = jnp.clip(gu[:, D_FF:], -SWIGLU_LIMIT, SWIGLU_LIMIT)
        hh = (up + 1.0) * gate * jax.nn.sigmoid(SWIGLU_ALPHA * gate)
        y = jnp.dot(hh.astype(BF16), wd_s[...], preferred_element_type=F32) + bd_ref[0]
        row = blk_ref[i] * MOE_ROWS + lax.broadcasted_iota(jnp.int32, (MOE_ROWS, 1), 0)
        o_ref[...] = jnp.where((row >= lo_ref[i]) & (row < hi_ref[i]), y, o_ref[...])


def _experts(xb, items, w_gate_up, b_gate_up, w_down, b_down):
    rows, D = xb.shape
    n_items = items[0].shape[0]
    wmap = lambda i, e, blk, lo, hi, first: (e[i], 0, 0)
    rmap = lambda i, e, blk, lo, hi, first: (blk[i], 0)
    grid_spec = pltpu.PrefetchScalarGridSpec(
        num_scalar_prefetch=5,
        grid=(n_items,),
        in_specs=[pl.BlockSpec((MOE_ROWS, D), rmap),
                  pl.BlockSpec((1, D, 2 * D_FF), wmap),
                  pl.BlockSpec((1, 1, 2 * D_FF), wmap),
                  pl.BlockSpec((1, D_FF, D), wmap),
                  pl.BlockSpec((1, 1, D), wmap)],
        out_specs=pl.BlockSpec((MOE_ROWS, D), rmap),
        scratch_shapes=[pltpu.VMEM((D, 2 * D_FF), BF16), pltpu.VMEM((D_FF, D), BF16)],
    )
    return pl.pallas_call(
        _expert_kernel,
        out_shape=jax.ShapeDtypeStruct((rows, D), F32),
        grid_spec=grid_spec,
        compiler_params=_cparams("arbitrary"),
        name="moe_experts",
    )(*items, xb, w_gate_up, b_gate_up.reshape(N_EXPERTS, 1, 2 * D_FF), w_down,
      b_down.reshape(N_EXPERTS, 1, D))


def _moe_dispatch(top_e, n):
    blk = MOE_ROWS
    nk = n * TOP_K
    cb = 128
    assert nk % cb == 0
    e = top_e.reshape(-1)
    onehot = (e[:, None] == jnp.arange(N_EXPERTS)[None, :]).astype(F32)
    oh3 = onehot.reshape(nk // cb, cb, N_EXPERTS)
    tri = jnp.asarray(np.tril(np.ones((cb, cb), np.float32), -1))
    within = jnp.einsum('ij,bje->bie', tri, oh3, precision=HIGHEST)
    blk_tot = jnp.sum(oh3, axis=1)
    blk_off = jnp.cumsum(blk_tot, axis=0) - blk_tot
    counts = jnp.sum(blk_tot, axis=0)
    start = jnp.cumsum(counts) - counts
    dest = jnp.sum((within + blk_off[:, None, :] + start[None, None, :]) * oh3, axis=-1)
    dest = dest.reshape(nk).astype(jnp.int32)
    order = jnp.argsort(dest)
    n_blk = -(-nk // blk)
    row_tok = jnp.concatenate([(order // TOP_K).astype(jnp.int32), jnp.full((n_blk * blk - nk,), n, jnp.int32)])
    counts_i, start_i = counts.astype(jnp.int32), start.astype(jnp.int32)
    first_b = start_i // blk
    last_b = (start_i + counts_i - 1) // blk
    n_it = jnp.where(counts_i > 0, last_b - first_b + 1, 0)
    it_end = jnp.cumsum(n_it)
    it_start = it_end - n_it
    n_items = n_blk + N_EXPERTS - 1
    i = jnp.arange(n_items)
    live = i < it_end[-1]
    it_e = jnp.minimum(jnp.sum(it_end[None, :] <= i[:, None], axis=1), N_EXPERTS - 1)
    it_blk = jnp.where(live, first_b[it_e] + i - it_start[it_e], n_blk - 1)
    it_lo = jnp.where(live, start_i[it_e], 0)
    it_hi = jnp.where(live, start_i[it_e] + counts_i[it_e], 0)
    it_first = jnp.concatenate([jnp.ones((1,), jnp.int32), (it_blk[1:] != it_blk[:-1]).astype(jnp.int32)])
    items = tuple(a.astype(jnp.int32) for a in (it_e, it_blk, it_lo, it_hi, it_first))
    return row_tok, dest.reshape(n, TOP_K), items


def _final_kernel(x_ref, y0_ref, y1_ref, y2_ref, y3_ref, tw_ref, gate_ref, lng_ref, lnb_ref, o_ref):
    tw = tw_ref[0]
    y = jnp.zeros_like(x_ref[0])
    for k, y_ref in enumerate((y0_ref, y1_ref, y2_ref, y3_ref)):
        y = y + tw[:, k:k + 1] * y_ref[0]
    z = DN_ALPHA * x_ref[0] + gate_ref[0] * y
    o_ref[0] = _layer_norm(z) * lng_ref[...] + lnb_ref[...]


def _final(x1, ys, tw, gate, ln_g, ln_b, tm):
    B, T, D = x1.shape
    R = gate.shape[1]
    rb = 1 if R == 1 else tm
    mod_map = (lambda b, i: (b, 0, 0)) if R == 1 else (lambda b, i: (b, i, 0))
    row = lambda n: pl.BlockSpec((1, tm, n), lambda b, i: (b, i, 0))
    vec = pl.BlockSpec((1, D), lambda b, i: (0, 0))
    return pl.pallas_call(
        _final_kernel,
        out_shape=jax.ShapeDtypeStruct((B, T, D), F32),
        grid=(B, T // tm),
        in_specs=[row(D), row(D), row(D), row(D), row(D), row(LANE),
                  pl.BlockSpec((1, rb, D), mod_map), vec, vec],
        out_specs=row(D),
        compiler_params=_cparams("parallel", "parallel"),
        name="moe_combine_ln",
    )(x1, *ys, tw, gate, ln_g, ln_b)


def _cmp_select_step_kernel(q_ref, kv_ref, bias_ref, pool_ref, o_ref, idx_ref, *, n_cmp, n_blk, q_pos):
    q = q_ref[0].astype(BF16)
    ncp = kv_ref.shape[1]
    nbp = pool_ref.shape[1]
    hd = HEAD_DIM
    kv = kv_ref[0]
    kb = [kv[:, h * hd:(h + 1) * hd].astype(BF16) for h in range(N_KV_HEADS)]
    vb = [kv[:, (N_KV_HEADS + h) * hd:(N_KV_HEADS + h + 1) * hd].astype(BF16) for h in range(N_KV_HEADS)]
    row = lax.broadcasted_iota(jnp.int32, (N_HEADS, 1), 0)
    first = row < GQA
    s = jnp.where(first, _nt_dot(q, kb[0]), _nt_dot(q, kb[1])) * (hd ** -0.5)
    s = s + bias_ref[...]
    ci = lax.broadcasted_iota(jnp.int32, (N_HEADS, ncp), 1)
    mask = (ci * CMP_STRIDE + CMP_BLOCK - 1 <= q_pos) & (ci < n_cmp)
    s = jnp.where(mask, s, NEG)
    m = jnp.max(s, axis=-1, keepdims=True)
    p = jnp.where(mask, jnp.exp(s - m), 0.0)
    p = p / jnp.maximum(jnp.sum(p, axis=-1, keepdims=True), 1e-30)
    pb = p.astype(BF16)
    o_ref[0] = jnp.where(first, jnp.dot(pb, vb[0], preferred_element_type=F32),
                         jnp.dot(pb, vb[1], preferred_element_type=F32))
    imp0 = jnp.sum(jnp.where(first, p, 0.0), axis=0, keepdims=True)
    imp1 = jnp.sum(jnp.where(first, 0.0, p), axis=0, keepdims=True)
    imp = jnp.where(first, imp0, imp1)
    sb = jnp.dot(imp, pool_ref[...], precision=HIGHEST, preferred_element_type=F32)
    cur = q_pos // SEL_BLOCK
    bi = lax.broadcasted_iota(jnp.int32, (nbp, nbp), 0)
    bj = lax.broadcasted_iota(jnp.int32, (nbp, nbp), 1)
    blk = lax.broadcasted_iota(jnp.int32, (1, nbp), 1)
    causal = blk <= cur
    forced = (blk == 0) | (blk == cur) | (blk == cur - 1)
    rsel = lax.broadcasted_iota(jnp.int32, (N_SEL, nbp), 0)
    for h in range(N_KV_HEADS):
        sc = jnp.where(forced & causal, 1e4, jnp.where(causal, sb[h * GQA:h * GQA + 1, :], -1.0))
        sc = jnp.where(blk < n_blk, sc, -2.0)
        scb = jnp.broadcast_to(sc, (nbp, nbp))
        col = jnp.sum(jnp.where(bi == bj, scb, 0.0), axis=1, keepdims=True)
        ahead = (col > scb) | ((col == scb) & (bi < bj))
        rank = jnp.sum(ahead.astype(jnp.int32), axis=0, keepdims=True)
        hit = jnp.broadcast_to(rank, (N_SEL, nbp)) == rsel
        idx = jnp.sum(jnp.where(hit, jnp.broadcast_to(blk, (N_SEL, nbp)), 0), axis=1, keepdims=True)
        idx_ref[0, h] = jnp.broadcast_to(idx, (N_SEL, LANE))


def _cmp_select_step(q, ckv, bias, pool, n_cmp, n_blk, q_pos):
    B = q.shape[0]
    NCp = ckv.shape[1]
    return pl.pallas_call(
        functools.partial(_cmp_select_step_kernel, n_cmp=n_cmp, n_blk=n_blk, q_pos=q_pos),
        out_shape=(jax.ShapeDtypeStruct((B, N_HEADS, HEAD_DIM), F32),
                   jax.ShapeDtypeStruct((B, N_KV_HEADS, N_SEL, LANE), jnp.int32)),
        grid=(B,),
        in_specs=[pl.BlockSpec((1, N_HEADS, HEAD_DIM), lambda b: (b, 0, 0)),
                  pl.BlockSpec((1, NCp, D_KV), lambda b: (b, 0, 0)),
                  pl.BlockSpec(bias.shape, lambda b: (0, 0)),
                  pl.BlockSpec(pool.shape, lambda b: (0, 0))],
        out_specs=(pl.BlockSpec((1, N_HEADS, HEAD_DIM), lambda b: (b, 0, 0)),
                   pl.BlockSpec((1, N_KV_HEADS, N_SEL, LANE), lambda b: (b, 0, 0, 0))),
        compiler_params=_cparams("parallel"),
        name="cmp_select_step",
    )(q, ckv, bias, pool)


def _sel_step_kernel(pg_ref, idx_ref, q_ref, *refs, n_past, q_pos):
    page_refs = refs[:N_SEL]
    new_ref, bias_ref, kpos_ref, o_ref = refs[N_SEL:]
    b, h = pl.program_id(0), pl.program_id(1)
    base = (b * N_KV_HEADS + h) * N_SEL
    kts, vts = [], []
    for j in range(N_SEL):
        is_new = idx_ref[base + j] >= n_past
        kts.append(jnp.where(is_new, new_ref[0, 0, 0], page_refs[j][0, 0, 0]))
        vts.append(jnp.where(is_new, new_ref[0, 1, 0], page_refs[j][0, 1, 0]))
    kt = jnp.concatenate(kts, axis=1).astype(BF16)
    vt = jnp.concatenate(vts, axis=1).astype(BF16)
    s = jnp.dot(q_ref[0].astype(BF16), kt, preferred_element_type=F32) * (HEAD_DIM ** -0.5) + bias_ref[0, 0]
    mask = kpos_ref[0, 0] <= q_pos
    s = jnp.where(mask, s, NEG)
    m = jnp.max(s, axis=-1, keepdims=True)
    p = jnp.where(mask, jnp.exp(s - m), 0.0)
    l = jnp.sum(p, axis=-1, keepdims=True)
    o_ref[0, 0] = _nt_dot(p.astype(BF16), vt) / jnp.maximum(l, 1e-30)


def _sel_step(q, pool_t, new_t, bias_sel, kpos, pages, idx_flat, n_past, q_pos):
    B = q.shape[0]
    nk = N_SEL * PAGE_SIZE
    slot = lambda b, h, j: (b * N_KV_HEADS + h) * N_SEL + j
    page_spec = lambda j: pl.BlockSpec((1, 2, 1, HEAD_DIM, PAGE_SIZE),
                                       lambda b, h, pg, ix, j=j: (pg[slot(b, h, j)], 0, h, 0, 0))
    grid_spec = pltpu.PrefetchScalarGridSpec(
        num_scalar_prefetch=2,
        grid=(B, N_KV_HEADS),
        in_specs=[pl.BlockSpec((1, N_HEADS, HEAD_DIM), lambda b, h, pg, ix: (b, 0, 0))]
        + [page_spec(j) for j in range(N_SEL)]
        + [pl.BlockSpec((1, 2, 1, HEAD_DIM, PAGE_SIZE), lambda b, h, pg, ix: (b, 0, h, 0, 0)),
           pl.BlockSpec((1, 1, N_HEADS, nk), lambda b, h, pg, ix: (b, h, 0, 0)),
           pl.BlockSpec((1, 1, 1, nk), lambda b, h, pg, ix: (b, h, 0, 0))],
        out_specs=pl.BlockSpec((1, 1, N_HEADS, HEAD_DIM), lambda b, h, pg, ix: (b, h, 0, 0)),
    )
    return pl.pallas_call(
        functools.partial(_sel_step_kernel, n_past=n_past, q_pos=q_pos),
        out_shape=jax.ShapeDtypeStruct((B, N_KV_HEADS, N_HEADS, HEAD_DIM), F32),
        grid_spec=grid_spec,
        compiler_params=_cparams("arbitrary", "arbitrary"),
        name="sel_step",
    )(pages, idx_flat, q, *([pool_t] * N_SEL), new_t, bias_sel, kpos)


def _win_step_kernel(q_ref, w_ref, new_ref, bias_ref, bias0_ref, o_ref):
    q = q_ref[0]
    qb = q.astype(BF16)
    row = lax.broadcasted_iota(jnp.int32, (N_HEADS, 1), 0)
    first = row < GQA
    w = w_ref[0]
    hd = HEAD_DIM
    kb = [w[:, h * hd:(h + 1) * hd].astype(BF16) for h in range(N_KV_HEADS)]
    vb = [w[:, (N_KV_HEADS + h) * hd:(N_KV_HEADS + h + 1) * hd].astype(BF16) for h in range(N_KV_HEADS)]
    s = jnp.where(first, _nt_dot(qb, kb[0]), _nt_dot(qb, kb[1])) * (hd ** -0.5) + bias_ref[...]
    new = new_ref[0]
    kn = jnp.where(first, new[:, 0:hd], new[:, hd:2 * hd])
    vn = jnp.where(first, new[:, 2 * hd:3 * hd], new[:, 3 * hd:])
    sn = jnp.sum(q * kn, axis=-1, keepdims=True) * (hd ** -0.5) + bias0_ref[...]
    m = jnp.maximum(jnp.max(s, axis=-1, keepdims=True), sn)
    p = jnp.exp(s - m)
    pn = jnp.exp(sn - m)
    l = jnp.sum(p, axis=-1, keepdims=True) + pn
    pb = p.astype(BF16)
    acc = jnp.where(first, jnp.dot(pb, vb[0], preferred_element_type=F32),
                    jnp.dot(pb, vb[1], preferred_element_type=F32)) + pn * vn
    o_ref[0] = acc / jnp.maximum(l, 1e-30)


def _win_step(q, win, new, bias, bias0):
    B, W, _ = win.shape
    return pl.pallas_call(
        _win_step_kernel,
        out_shape=jax.ShapeDtypeStruct((B, N_HEADS, HEAD_DIM), F32),
        grid=(B,),
        in_specs=[pl.BlockSpec((1, N_HEADS, HEAD_DIM), lambda b: (b, 0, 0)),
                  pl.BlockSpec((1, W, D_KV), lambda b: (b, 0, 0)),
                  pl.BlockSpec((1, 1, D_KV), lambda b: (b, 0, 0)),
                  pl.BlockSpec((N_HEADS, W), lambda b: (0, 0)),
                  pl.BlockSpec((N_HEADS, 1), lambda b: (0, 0))],
        out_specs=pl.BlockSpec((1, N_HEADS, HEAD_DIM), lambda b: (b, 0, 0)),
        compiler_params=_cparams("parallel"),
        name="win_step",
    )(q, win, new, bias, bias0)


def _split_heads(kv, dtype):
    B, L, _ = kv.shape
    kv5 = kv.reshape(B, L, 2, N_KV_HEADS, HEAD_DIM)
    return (jnp.transpose(kv5[:, :, 0], (0, 2, 1, 3)).astype(dtype),
            jnp.transpose(kv5[:, :, 1], (0, 2, 1, 3)).astype(dtype))


def _nsa_prompt(q5, kvc, ks, vst, kw, vwt, cmp_tab, rel_bias):
    B, T, _ = kvc.shape
    nc = T // CMP_STRIDE
    nb = T // SEL_BLOCK
    ckv = _compress_out([_compress_in(kvc.reshape(B, nc, CMP_STRIDE * D_KV), cmp_tab)], cmp_tab, nc)
    kc, vc = _split_heads(ckv, BF16)
    vct = jnp.transpose(vc, (0, 1, 3, 2))
    bias_n = _bias_by_distance(rel_bias, T)
    n_qt, n_kt = T // ATT_TQ, T // ATT_TK
    n_ds = min(n_kt, -(-(REL_MAX_DIST + ATT_TK - 1) // ATT_TK) + 1)
    n_dw = min(n_kt, WINDOW // ATT_TK + 1)
    tzs, tzw, bias_tab = _bias_tables(bias_n, n_qt, nc // 8, n_ds, n_dw, ATT_TQ, ATT_TK)
    pool = jnp.asarray(_pool_matrix(nc, nb))
    o_cmp, sel = _cmp_select_prompt(q5, kc, vct, bias_tab, pool, nc - 1)
    o_sel, o_win = _sel_win_prompt(q5, ks, vst, kw, vwt, sel, tzs, tzw)
    return o_cmp, o_sel, o_win


def _nsa_sample(q, kvc, kvs, kvw, pool_cmp, pool_sel, win_buf, page_table, cmp_tab, rel_bias):
    B = q.shape[0]
    n_pages = page_table.shape[1]
    past_len = n_pages * PAGE_SIZE
    q_pos = past_len
    lp = -(-(past_len + 1) // SEL_BLOCK) * SEL_BLOCK
    n_cmp = lp // CMP_STRIDE - 1
    n_blk = lp // SEL_BLOCK
    n_past_chunks = past_len // CMP_STRIDE
    n_tail = 8
    assert n_past_chunks + n_tail >= n_cmp + 1
    n_chunks = n_past_chunks + n_tail
    feature_major = lambda pool: jnp.transpose(pool, (0, 2, 3, 4, 1))
    z_past = _compress_in_paged(feature_major(pool_cmp), page_table, cmp_tab)
    tail = jnp.pad(kvc[:, None, :], ((0, 0), (0, n_tail * CMP_STRIDE - 1), (0, 0)))
    z_tail = _compress_in(tail.reshape(B, n_tail, CMP_STRIDE * D_KV), cmp_tab)
    ncp = -(-n_chunks // LANE) * LANE
    nbp = -(-n_blk // LANE) * LANE
    ckv = _compress_out([z_past, z_tail], cmp_tab, ncp)
    bias_n = _bias_by_distance(rel_bias, q_pos + 1)
    n_back = max((n_pages + 1) * PAGE_SIZE, ncp * CMP_STRIDE + CMP_BLOCK)
    back = jnp.concatenate([bias_n[:, ::-1], jnp.broadcast_to(bias_n[:, :1], (N_HEADS, n_back - q_pos - 1))], 1)
    bias_c = back[:, CMP_BLOCK - 1:CMP_BLOCK - 1 + ncp * CMP_STRIDE:CMP_STRIDE]
    pool = jnp.asarray(_pool_matrix(ncp, nbp).T)
    q3 = q.reshape(B, N_HEADS, HEAD_DIM)
    o_cmp, idx = _cmp_select_step(q3, ckv, bias_c, pool, n_cmp, n_blk, q_pos)
    idx = idx[..., 0]
    bpp = PAGE_SIZE // SEL_BLOCK
    n_past = n_pages * bpp
    lpage = idx // bpp
    pages = jnp.take_along_axis(page_table, jnp.minimum(lpage, n_pages - 1).reshape(B, -1), axis=1)
    new_t = jnp.pad(kvs.reshape(B, 2, N_KV_HEADS, HEAD_DIM, 1), ((0, 0),) * 4 + ((0, PAGE_SIZE - 1),))
    bias_page = jnp.transpose(back[:, :(n_pages + 1) * PAGE_SIZE].reshape(N_HEADS, n_pages + 1, PAGE_SIZE),
                              (1, 0, 2))
    bias_sel = jnp.transpose(bias_page[lpage], (0, 1, 3, 2, 4)).reshape(B, N_KV_HEADS, N_HEADS, -1)
    kpos = lpage[..., None] * PAGE_SIZE + jnp.arange(PAGE_SIZE)
    ok = (kpos // SEL_BLOCK == idx[..., None]) & (idx <= q_pos // SEL_BLOCK)[..., None]
    kpos = jnp.where(ok, kpos, q_pos + 1).reshape(B, N_KV_HEADS, 1, -1).astype(jnp.int32)
    o_sel = _sel_step(q3, feature_major(pool_sel), new_t, bias_sel, kpos, pages.reshape(-1).astype(jnp.int32),
                      idx.reshape(-1).astype(jnp.int32), n_past, q_pos)
    o_sel = jnp.concatenate([o_sel[:, h, h * GQA:(h + 1) * GQA] for h in range(N_KV_HEADS)], axis=1)
    wb = win_buf.shape[1]
    bias_w = bias_n[:, 1:wb + 1][:, ::-1]
    o_win = _win_step(q3, win_buf.reshape(B, wb, D_KV), kvw[:, None, :], bias_w, bias_n[:, 0:1])
    return o_cmp.reshape(B, D_ATT), o_sel.reshape(B, D_ATT), o_win.reshape(B, D_ATT)


def kernel(x_prompt, x_sample, cache_cmp_kv, cache_sel_kv, state_win_kv, state_ssm_re, state_ssm_im, page_table,
           c_prompt, c_sample, w_ada, b_ada, w_in, lam_re, lam_im, log_dt, b_re, b_im, c_re, c_im, d_skip,
           w_glu, b_glu, phi_pe, phi_w1, phi_b1, phi_w2, phi_b2, rel_bias, w_out, ln1_g, ln1_b,
           w_router, b_router, w_gate_up, b_gate_up, w_down, b_down, ln2_g, ln2_b):
    assert w_ada.shape[0] == DEPTH == 1
    l = 0
    Bp, T, D = x_prompt.shape
    Bs = x_sample.shape[0]
    kv_tail = (2, N_KV_HEADS, HEAD_DIM)

    n_c = Bp + Bs
    c_all = jnp.pad(jnp.concatenate([c_prompt, c_sample], 0), ((0, -n_c % 8), (0, 0)))
    m_all = _adaln(c_all, w_ada[l], b_ada[l])
    m_p = m_all[:Bp].reshape(Bp, 6, D)
    m_s = m_all[Bp:n_c].reshape(Bs, 6, D)
    mod_p = [m_p[:, i:i + 1, :] for i in range(6)]
    mod_s = [m_s[None, :, i, :] for i in range(6)]

    w_in_pad = jnp.pad(w_in[l], ((0, 0), (0, D_IN_PAD - D_IN))).astype(BF16)
    n_levels = max(1, int(math.log2(T // SSM_CHUNK)))
    ssm_tab = _ssm_tables(lam_re[l], lam_im[l], log_dt[l], b_re[l], b_im[l], c_re[l], c_im[l],
                          SSM_CHUNK, n_levels)
    cmp_tab = _compress_tables(phi_pe[l], phi_w1[l], phi_b1[l], phi_w2[l], phi_b2[l])
    w_post = dict(
        d_skip=d_skip[l].reshape(1, D_SSM), w_glu=w_glu[l].astype(BF16), b_glu=b_glu[l].reshape(1, D_SSM),
        gexp=jnp.asarray(_gate_expand_matrix(), dtype=BF16), w_out=w_out[l].astype(BF16),
        ln1_g=ln1_g[l].reshape(1, D), ln1_b=ln1_b[l].reshape(1, D),
        w_router=jnp.stack(_split_bf16(jnp.pad(w_router[l], ((0, 0), (0, LANE - N_EXPERTS))))),
        b_router=jnp.pad(b_router[l], (0, LANE - N_EXPERTS)).reshape(1, LANE))

    u, q5, kvc, kvs, kvw, g, ks, vst, kw, vwt = _mixer_in(x_prompt, mod_p[0], mod_p[1], w_in_pad, 512, True)
    y_ssm, h_p = _ssm_prompt(u, ssm_tab)
    o_cmp, o_sel, o_win = _nsa_prompt(q5, kvc, ks, vst, kw, vwt, cmp_tab, rel_bias)
    x1_p, hm_p, te_p, tw_p = _post_mixer(y_ssm, u, o_cmp, o_sel, o_win, g, x_prompt,
                                         mod_p[2], mod_p[3], mod_p[4], w_post, tm=256)

    u_s, q_s, kvc_s, kvs_s, kvw_s, g_s = _mixer_in(x_sample.reshape(1, Bs, D), mod_s[0], mod_s[1],
                                                   w_in_pad, Bs, False)
    y_s, h_s = _ssm_sample(u_s[0], state_ssm_re[l], state_ssm_im[l], ssm_tab, c_re[l], c_im[l])
    oc_s, os_s, ow_s = _nsa_sample(q_s[0].astype(F32), kvc_s[0], kvs_s[0], kvw_s[0], cache_cmp_kv[l],
                                   cache_sel_kv[l], state_win_kv[l], page_table, cmp_tab, rel_bias)
    x1_s, hm_s, te_s, tw_s = _post_mixer(y_s[None], u_s, oc_s[None], os_s[None], ow_s[None], g_s,
                                         x_sample.reshape(1, Bs, D), mod_s[2], mod_s[3], mod_s[4],
                                         w_post, tm=Bs)

    n_p = Bp * T
    n_all = n_p + Bs
    hm_all = jnp.concatenate([hm_p.reshape(n_p, D), hm_s.reshape(Bs, D)], 0)
    te_all = jnp.concatenate([te_p.reshape(n_p, LANE), te_s.reshape(Bs, LANE)], 0)[:, :TOP_K]
    row_tok, dest, items = _moe_dispatch(te_all, n_all)
    xb = jnp.concatenate([hm_all, jnp.zeros((1, D), F32)], 0)[row_tok]
    yb = _experts(xb, items, w_gate_up[l], b_gate_up[l], w_down[l], b_down[l])
    ys_p = [yb[dest[:n_p, k]].reshape(Bp, T, D) for k in range(TOP_K)]
    ys_s = [yb[dest[n_p:, k]].reshape(1, Bs, D) for k in range(TOP_K)]
    ln2g, ln2b = ln2_g[l].reshape(1, D), ln2_b[l].reshape(1, D)
    out_p = _final(x1_p, ys_p, tw_p, mod_p[5], ln2g, ln2b, tm=512)
    out_s = _final(x1_s, ys_s, tw_s, mod_s[5], ln2g, ln2b, tm=Bs)

    wlen = min(WINDOW, T)
    win_s = jnp.concatenate([state_win_kv[l], kvw_s[0].reshape(Bs, 1, *kv_tail)], 1)[:, -state_win_kv.shape[2]:]
    p_state = SSM_STATE
    return (out_p, out_s.reshape(Bs, 1, D),
            kvc.reshape(1, Bp, T, *kv_tail), kvc_s[0].reshape(1, Bs, 1, *kv_tail),
            kvs.reshape(1, Bp, T, *kv_tail), kvs_s[0].reshape(1, Bs, 1, *kv_tail),
            kvw[:, T - wlen:].reshape(1, Bp, wlen, *kv_tail), win_s[None],
            h_p[None, ..., :p_state], h_p[None, ..., p_state:],
            h_s[None, ..., :p_state], h_s[None, ..., p_state:])
```

```python
import functools
import math

import numpy as np
import jax
import jax.numpy as jnp
from jax import lax
from jax.experimental import pallas as pl
from jax.experimental.pallas import tpu as pltpu

DEPTH = 1
PAGE_SIZE = 128
D_SSM = 512
SSM_GROUP = 16
N_SSM_GROUPS = D_SSM // SSM_GROUP
SSM_STATE = 64
N_HEADS = 8
HEAD_DIM = 64
N_KV_HEADS = 2
GQA = N_HEADS // N_KV_HEADS
D_ATT = N_HEADS * HEAD_DIM
D_KV = 2 * N_KV_HEADS * HEAD_DIM
CMP_STRIDE = 16
CMP_BLOCK = 2 * CMP_STRIDE
SEL_BLOCK = 64
N_SEL = 16
WINDOW = 512
NUM_BUCKETS = 32
REL_MAX_DIST = 1024
N_EXPERTS = 32
TOP_K = 4
D_FF = 1024
SWIGLU_LIMIT = 7.0
SWIGLU_ALPHA = 1.702
DN_ALPHA = (2 * DEPTH) ** 0.25
D_IN = D_SSM + D_ATT + 3 * D_KV + 3 * N_HEADS
NEG = -1e30
F32 = jnp.float32
BF16 = jnp.bfloat16
HIGHEST = lax.Precision.HIGHEST

LANE = 128
D_IN_PAD = -(-D_IN // LANE) * LANE
SSM_CHUNK = 8
ATT_TQ = 128
ATT_TK = 128
SEL_CHAINS = 4
MOE_ROWS = 256
PAGES_PER_STEP = 32
PAGE_PARTS = 2
CHUNK_PITCH = 24
VMEM_LIMIT = 48 * 1024 * 1024
LN_EPS = 1e-5


def _cparams(*sem):
    return pltpu.CompilerParams(dimension_semantics=sem, vmem_limit_bytes=VMEM_LIMIT)


def _nt_dot(a, b):
    return lax.dot_general(a, b, (((1,), (1,)), ((), ())), preferred_element_type=F32)


def _layer_norm(x):
    mu = jnp.mean(x, axis=-1, keepdims=True)
    xc = x - mu
    var = jnp.mean(xc * xc, axis=-1, keepdims=True)
    return xc * lax.rsqrt(var + LN_EPS)


def _adaln_kernel(c_ref, w_ref, b_ref, o_ref):
    c = c_ref[...]
    s = c * jax.nn.sigmoid(c)
    o_ref[...] = jnp.dot(s, w_ref[...], precision=HIGHEST, preferred_element_type=F32) + b_ref[...]


def _adaln(c, w, b):
    n, d = c.shape
    dout = w.shape[1]
    tn = 1024
    return pl.pallas_call(
        _adaln_kernel,
        out_shape=jax.ShapeDtypeStruct((n, dout), F32),
        grid=(dout // tn,),
        in_specs=[pl.BlockSpec((n, d), lambda j: (0, 0)),
                  pl.BlockSpec((d, tn), lambda j: (0, j)),
                  pl.BlockSpec((1, tn), lambda j: (0, j))],
        out_specs=pl.BlockSpec((n, tn), lambda j: (0, j)),
        compiler_params=_cparams("arbitrary"),
        name="adaln",
    )(c, w, b.reshape(1, dout))


def _mixer_in_kernel(x_ref, sh_ref, sc_ref, w_ref, u_ref, q_ref, kvc_ref, kvs_ref, kvw_ref, g_ref, *att_refs):
    h = _layer_norm(x_ref[0]) * (1.0 + sc_ref[0]) + sh_ref[0]
    z = jnp.dot(h.astype(BF16), w_ref[...], preferred_element_type=F32)
    c0 = D_SSM
    c1 = c0 + D_ATT
    c2 = c1 + D_KV
    c3 = c2 + D_KV
    c4 = c3 + D_KV
    u_ref[0] = z[:, :c0]
    kvc_ref[0] = z[:, c1:c2]
    kvs_ref[0] = z[:, c2:c3]
    kvw_ref[0] = z[:, c3:c4]
    g_ref[0] = z[:, c4:c4 + LANE]
    if not att_refs:
        q_ref[0] = z[:, c0:c1].astype(BF16)
        return
    ks_ref, vst_ref, kw_ref, vwt_ref = att_refs
    hd, half = HEAD_DIM, N_KV_HEADS * HEAD_DIM
    for hq in range(N_HEADS):
        q_ref[0, hq // GQA, hq % GQA] = (z[:, c0 + hq * hd:c0 + (hq + 1) * hd] * (hd ** -0.5)).astype(BF16)
    for k_ref, vt_ref, base in ((ks_ref, vst_ref, c2), (kw_ref, vwt_ref, c3)):
        for hk in range(N_KV_HEADS):
            k_ref[0, hk] = z[:, base + hk * hd:base + (hk + 1) * hd].astype(BF16)
        vt = z[:, base + half:base + 2 * half].T
        vt_ref[0] = vt.reshape(N_KV_HEADS, hd, vt.shape[1]).astype(BF16)


def _mixer_in(x, shift, scale, w_pad, tm, attention_layouts):
    B, T, D = x.shape
    R = shift.shape[1]
    rb = 1 if R == 1 else tm
    mod_map = (lambda b, i: (b, 0, 0)) if R == 1 else (lambda b, i: (b, i, 0))
    row = lambda n: pl.BlockSpec((1, tm, n), lambda b, i: (b, i, 0))
    f32 = lambda n: jax.ShapeDtypeStruct((B, T, n), F32)
    if attention_layouts:
        q_shape = jax.ShapeDtypeStruct((B, N_KV_HEADS, GQA, T, HEAD_DIM), BF16)
        q_spec = pl.BlockSpec((1, N_KV_HEADS, GQA, tm, HEAD_DIM), lambda b, i: (b, 0, 0, i, 0))
        k_shape = jax.ShapeDtypeStruct((B, N_KV_HEADS, T, HEAD_DIM), BF16)
        k_spec = pl.BlockSpec((1, N_KV_HEADS, tm, HEAD_DIM), lambda b, i: (b, 0, i, 0))
        vt_shape = jax.ShapeDtypeStruct((B, N_KV_HEADS, HEAD_DIM, T), BF16)
        vt_spec = pl.BlockSpec((1, N_KV_HEADS, HEAD_DIM, tm), lambda b, i: (b, 0, 0, i))
        extra_shapes, extra_specs = (k_shape, vt_shape, k_shape, vt_shape), (k_spec, vt_spec, k_spec, vt_spec)
    else:
        q_shape, q_spec = jax.ShapeDtypeStruct((B, T, D_ATT), BF16), row(D_ATT)
        extra_shapes, extra_specs = (), ()
    return pl.pallas_call(
        _mixer_in_kernel,
        out_shape=(f32(D_SSM), q_shape, f32(D_KV), f32(D_KV), f32(D_KV), f32(LANE)) + extra_shapes,
        grid=(B, T // tm),
        in_specs=[row(D), pl.BlockSpec((1, rb, D), mod_map), pl.BlockSpec((1, rb, D), mod_map),
                  pl.BlockSpec((D, D_IN_PAD), lambda b, i: (0, 0))],
        out_specs=(row(D_SSM), q_spec, row(D_KV), row(D_KV), row(D_KV), row(LANE)) + extra_specs,
        compiler_params=_cparams("parallel", "parallel"),
        name="mixer_in",
    )(x, shift, scale, w_pad)


def _ssm_tables(lam_re, lam_im, log_dt, b_re, b_im, c_re, c_im, L, n_levels):
    G, P = lam_re.shape
    C = b_re.shape[-1]
    dt = jnp.exp(log_dt.astype(F32))[:, None]
    er, ei = lam_re * dt, lam_im * dt

    def power(k):
        kk = k.astype(F32)[:, None, None]
        mag = jnp.exp(kk * er)
        return mag * jnp.cos(kk * ei), mag * jnp.sin(kk * ei)

    lb_re, lb_im = power(jnp.ones((1,), F32))
    nr, ni = lb_re[0] - 1.0, lb_im[0]
    den = lam_re * lam_re + lam_im * lam_im
    fr = (nr * lam_re + ni * lam_im) / den
    fi = (ni * lam_re - nr * lam_im) / den
    bbr = fr[:, :, None] * b_re - fi[:, :, None] * b_im
    bbi = fr[:, :, None] * b_im + fi[:, :, None] * b_re
    pr, pi = power(jnp.arange(L + 1))
    clr = c_re[None] * pr[:, :, None, :] - c_im[None] * pi[:, :, None, :]
    cli = c_re[None] * pi[:, :, None, :] + c_im[None] * pr[:, :, None, :]
    kern = (jnp.einsum('kgcp,gpd->kgcd', clr[:L], bbr, precision=HIGHEST)
            - jnp.einsum('kgcp,gpd->kgcd', cli[:L], bbi, precision=HIGHEST))
    GP = LANE // C
    X = G // GP
    eye = jnp.eye(GP, dtype=BF16)
    place_einsum = functools.partial(jnp.einsum, preferred_element_type=BF16)
    kblk = place_einsum('kxhcd,hj->xkhdjc', kern.astype(BF16).reshape(L, X, GP, C, C), eye)
    kblk = kblk.reshape(X, L, LANE, LANE)
    prr, pir = pr[:L][::-1], pi[:L][::-1]
    ws2 = jnp.stack([prr[..., None] * bbr[None] - pir[..., None] * bbi[None],
                     prr[..., None] * bbi[None] + pir[..., None] * bbr[None]])
    ws = place_einsum('rsxhpd,hj->xshdrjp', ws2.astype(BF16).reshape(2, L, X, GP, P, C), eye)
    ws = ws.reshape(X, L * LANE, 2 * GP * P)
    wy2 = jnp.stack([clr[1:], -cli[1:]])
    wy = place_einsum('rtxhcp,hj->xrhptjc', wy2.astype(BF16).reshape(2, L, X, GP, C, P), eye)
    wy = wy.reshape(X, 2 * GP * P, L * LANE)
    lr, li = power(L * (2 ** jnp.arange(n_levels)))
    lr, li = lr.reshape(n_levels, X, GP * P), li.reshape(n_levels, X, GP * P)
    ar = jnp.transpose(jnp.concatenate([lr, lr], -1), (1, 0, 2))
    ai = jnp.transpose(jnp.concatenate([-li, li], -1), (1, 0, 2))
    return kblk, ws, wy, ar, ai, (lb_re[0], lb_im[0], bbr, bbi)


def _ssm_kernel(u_ref, kblk_ref, ws_ref, wy_ref, ar_ref, ai_ref, y_ref, hl_ref, toep_ref, *, L, nc, n_levels):
    for s in range(L):
        for t in range(L):
            blk = kblk_ref[0, t - s] if t >= s else jnp.zeros((LANE, LANE), BF16)
            toep_ref[s * LANE:(s + 1) * LANE, t * LANE:(t + 1) * LANE] = blk
    u = jnp.concatenate([u_ref[0, pl.ds(t, nc, stride=L), :] for t in range(L)], axis=1).astype(BF16)
    y1 = jnp.dot(u, toep_ref[...], preferred_element_type=F32)
    h = jnp.dot(u, ws_ref[0], preferred_element_type=F32)
    w2 = h.shape[-1]
    rows = lax.broadcasted_iota(jnp.int32, (nc, w2), 0)
    for k in range(n_levels):
        d = 1 << k
        sh = jnp.where(rows >= d, pltpu.roll(h, d, axis=0), 0.0)
        sw = pltpu.roll(sh, w2 // 2, axis=1)
        h = h + ar_ref[0, k:k + 1, :] * sh + ai_ref[0, k:k + 1, :] * sw
    hl_ref[0, 0] = h[nc - 1:nc, :]
    hp = jnp.where(rows >= 1, pltpu.roll(h, 1, axis=0), 0.0)
    y = y1 + jnp.dot(hp.astype(BF16), wy_ref[0], preferred_element_type=F32)
    for t in range(L):
        y_ref[0, pl.ds(t, nc, stride=L), :] = y[:, t * LANE:(t + 1) * LANE]


def _ssm_prompt(u, tables):
    kblk, ws, wy, ar, ai, _ = tables
    B, T, _ = u.shape
    L, P = SSM_CHUNK, SSM_STATE
    X, n_levels, w2 = ar.shape
    GP = w2 // (2 * P)
    nc = T // L
    tab = lambda a: pl.BlockSpec((1,) + a.shape[1:], lambda x, b: (x,) + (0,) * (a.ndim - 1))
    seq = pl.BlockSpec((1, T, LANE), lambda x, b: (b, 0, x))
    y, hl = pl.pallas_call(
        functools.partial(_ssm_kernel, L=L, nc=nc, n_levels=n_levels),
        out_shape=(jax.ShapeDtypeStruct((B, T, D_SSM), F32), jax.ShapeDtypeStruct((X, B, 1, w2), F32)),
        grid=(X, B),
        in_specs=[seq, tab(kblk), tab(ws), tab(wy), tab(ar), tab(ai)],
        out_specs=(seq, pl.BlockSpec((1, 1, 1, w2), lambda x, b: (x, b, 0, 0))),
        scratch_shapes=[pltpu.VMEM((L * LANE, L * LANE), BF16)],
        compiler_params=_cparams("parallel", "parallel"),
        name="ssm_prompt",
    )(u, kblk, ws, wy, ar, ai)
    hl = jnp.transpose(hl.reshape(X, B, 2, GP, P), (1, 0, 3, 2, 4))
    return y, hl.reshape(B, X * GP, 2 * P)


def _ssm_step_kernel(u_ref, h0_ref, bb_ref, lr_ref, li_ref, cy_ref, y_ref, h_ref):
    p = lr_ref.shape[-1] // 2
    bu = jnp.einsum('gbc,gcp->gbp', u_ref[...], bb_ref[...], preferred_element_type=F32)
    h0 = h0_ref[...]
    h0s = jnp.concatenate([h0[..., p:], h0[..., :p]], axis=-1)
    h = lr_ref[...] * h0 + li_ref[...] * h0s + bu
    h_ref[...] = h
    y_ref[...] = jnp.einsum('gbp,gpc->gbc', h.astype(BF16), cy_ref[...], preferred_element_type=F32)


def _ssm_sample(u, h0_re, h0_im, tables, c_re, c_im):
    lb_re, lb_im, bbr, bbi = tables[-1]
    B = u.shape[0]
    G, C, P = N_SSM_GROUPS, SSM_GROUP, SSM_STATE
    ug = jnp.transpose(u.reshape(B, G, C), (1, 0, 2)).astype(BF16)
    h0 = jnp.transpose(jnp.concatenate([h0_re, h0_im], -1), (1, 0, 2)).astype(F32)
    bb = jnp.concatenate([jnp.transpose(bbr, (0, 2, 1)), jnp.transpose(bbi, (0, 2, 1))], -1).astype(BF16)
    lr = jnp.concatenate([lb_re, lb_re], -1)[:, None, :]
    li = jnp.concatenate([-lb_im, lb_im], -1)[:, None, :]
    cy = jnp.concatenate([jnp.transpose(c_re, (0, 2, 1)), -jnp.transpose(c_im, (0, 2, 1))], 1).astype(BF16)
    y, h = pl.pallas_call(
        _ssm_step_kernel,
        out_shape=(jax.ShapeDtypeStruct((G, B, C), F32), jax.ShapeDtypeStruct((G, B, 2 * P), F32)),
        name="ssm_step",
    )(ug, h0, bb, lr, li, cy)
    return jnp.transpose(y, (1, 0, 2)).reshape(B, D_SSM), jnp.transpose(h, (1, 0, 2))


def _compress_tables(phi_pe, phi_w1, phi_b1, phi_w2, phi_b2):
    S, H, Dh = CMP_STRIDE, N_KV_HEADS, HEAD_DIM
    w1 = phi_w1.reshape(2, 2, S, Dh, Dh)
    eye_c = jnp.eye(2, dtype=F32)
    eye_h = jnp.eye(H, dtype=F32)
    wbig = jnp.einsum('cajde,xc,yh->jxydache', w1, eye_c, eye_h).reshape(S * 2 * H * Dh, 2 * 2 * H * Dh)
    pe = jnp.transpose(phi_pe.reshape(2, 2, S, Dh), (1, 2, 0, 3))
    pe_rows = jnp.broadcast_to(pe[:, :, :, None, :], (2, S, 2, H, Dh)).reshape(2, S * 2 * H * Dh)
    n = 2 * H * Dh
    pe_w = (jnp.dot(pe_rows[0], wbig[:, :n], precision=HIGHEST) + jnp.dot(pe_rows[1], wbig[:, n:], precision=HIGHEST))
    b1 = jnp.broadcast_to(phi_b1[:, None, :], (2, H, Dh)).reshape(1, n) + pe_w[None, :]
    w2 = jnp.einsum('cef,cx,hy->chexyf', phi_w2, eye_c, eye_h).reshape(n, n)
    b2 = jnp.broadcast_to(phi_b2[:, None, :], (2, H, Dh)).reshape(1, n)
    return wbig.astype(BF16), b1, w2.astype(BF16), b2


def _compress_in_kernel(x_ref, w_ref, z_ref):
    z_ref[0] = jnp.dot(x_ref[0].astype(BF16), w_ref[...], preferred_element_type=F32)


def _compress_in(x2, tables):
    wbig = tables[0]
    N2 = wbig.shape[1]
    B, n, K = x2.shape
    tr = math.gcd(n, 256)
    return pl.pallas_call(
        _compress_in_kernel,
        out_shape=jax.ShapeDtypeStruct((B, n, N2), F32),
        grid=(B, n // tr),
        in_specs=[pl.BlockSpec((1, tr, K), lambda b, i: (b, i, 0)),
                  pl.BlockSpec((K, N2), lambda b, i: (0, 0))],
        out_specs=pl.BlockSpec((1, tr, N2), lambda b, i: (b, i, 0)),
        compiler_params=_cparams("parallel", "parallel"),
        name="compress_in",
    )(x2, wbig)


def _compress_in_paged_kernel(pt_ref, *refs, n_pg):
    x_refs = refs[:n_pg]
    w_ref, z_ref = refs[n_pg:n_pg + 2]
    scratch = refs[n_pg + 2:]
    n_slab = D_KV // LANE
    pg_part = n_pg // PAGE_PARTS
    cpp = PAGE_SIZE // CMP_STRIDE
    rows = pg_part * cpp
    for part in range(PAGE_PARTS):
        s_refs = scratch[part * n_slab:(part + 1) * n_slab]
        for k in range(pg_part):
            t = x_refs[part * pg_part + k][0].reshape(D_KV, PAGE_SIZE).T
            for c, s_ref in enumerate(s_refs):
                for n in range(cpp):
                    r0 = (k * cpp + n) * CHUNK_PITCH
                    s_ref[r0:r0 + CMP_STRIDE, :] = t[n * CMP_STRIDE:(n + 1) * CMP_STRIDE, c * LANE:(c + 1) * LANE]
        z = jnp.zeros((rows, w_ref.shape[1]), F32)
        for j in range(CMP_STRIDE):
            xj = jnp.concatenate([s_ref[pl.ds(j, rows, stride=CHUNK_PITCH), :] for s_ref in s_refs], axis=1)
            z = z + jnp.dot(xj.astype(BF16), w_ref[j * D_KV:(j + 1) * D_KV, :], preferred_element_type=F32)
        z_ref[0, part * rows:(part + 1) * rows, :] = z


def _compress_in_paged(pool_t, page_table, tables):
    wbig = tables[0]
    N2 = wbig.shape[1]
    K = wbig.shape[0]
    B, n_pages = page_table.shape
    n_pg = math.gcd(n_pages, PAGES_PER_STEP)
    rows = n_pg * PAGE_SIZE // CMP_STRIDE
    page_spec = lambda k: pl.BlockSpec((1,) + pool_t.shape[1:],
                                       lambda b, i, pt, k=k: (pt[b, i * n_pg + k], 0, 0, 0, 0))
    grid_spec = pltpu.PrefetchScalarGridSpec(
        num_scalar_prefetch=1,
        grid=(B, n_pages // n_pg),
        in_specs=[page_spec(k) for k in range(n_pg)] + [pl.BlockSpec((K, N2), lambda b, i, pt: (0, 0))],
        out_specs=pl.BlockSpec((1, rows, N2), lambda b, i, pt: (b, i, 0)),
        scratch_shapes=[pltpu.VMEM((rows // PAGE_PARTS * CHUNK_PITCH, LANE), F32)
                        for _ in range(PAGE_PARTS * (D_KV // LANE))],
    )
    return pl.pallas_call(
        functools.partial(_compress_in_paged_kernel, n_pg=n_pg),
        out_shape=jax.ShapeDtypeStruct((B, n_pages * PAGE_SIZE // CMP_STRIDE, N2), F32),
        grid_spec=grid_spec,
        compiler_params=_cparams("arbitrary", "arbitrary"),
        name="compress_in_paged",
    )(page_table, *([pool_t] * n_pg), wbig)


def _compress_out_kernel(*refs):
    z_refs, (b1_ref, w2_ref, b2_ref, o_ref) = refs[:-4], refs[-4:]
    z = jnp.concatenate([z_ref[0] for z_ref in z_refs], axis=0)
    n = z.shape[-1] // 2
    rows = z.shape[0]
    second = pltpu.roll(z[:, n:], rows - 1, axis=0)
    hdn = jax.nn.gelu(z[:, :n] + second + b1_ref[...])
    o_ref[0, :rows, :] = jnp.dot(hdn.astype(BF16), w2_ref[...], preferred_element_type=F32) + b2_ref[...]
    if o_ref.shape[1] > rows:
        o_ref[0, rows:, :] = jnp.zeros((o_ref.shape[1] - rows, n), F32)


def _compress_out(zs, tables, n_out):
    _, b1, w2, b2 = tables
    B, _, N2 = zs[0].shape
    return pl.pallas_call(
        _compress_out_kernel,
        out_shape=jax.ShapeDtypeStruct((B, n_out, N2 // 2), F32),
        grid=(B,),
        in_specs=[pl.BlockSpec((1, z.shape[1], N2), lambda b: (b, 0, 0)) for z in zs] + [
                  pl.BlockSpec((1, N2 // 2), lambda b: (0, 0)),
                  pl.BlockSpec((N2 // 2, N2 // 2), lambda b: (0, 0)),
                  pl.BlockSpec((1, N2 // 2), lambda b: (0, 0))],
        out_specs=pl.BlockSpec((1, n_out, N2 // 2), lambda b: (b, 0, 0)),
        compiler_params=_cparams("parallel"),
        name="compress_out",
    )(*zs, b1, w2, b2)


def _rel_bucket(dist):
    n = jnp.maximum(dist, 0)
    max_exact = NUM_BUCKETS // 2
    nf = jnp.maximum(n, 1).astype(F32)
    large = max_exact + (jnp.log(nf / max_exact) / math.log(REL_MAX_DIST / max_exact)
                         * (NUM_BUCKETS - max_exact)).astype(jnp.int32)
    large = jnp.minimum(large, NUM_BUCKETS - 1)
    return jnp.where(n < max_exact, n, large)


def _bias_by_distance(rel_bias, n_max):
    onehot = (_rel_bucket(jnp.arange(n_max))[None, :] == jnp.arange(NUM_BUCKETS)[:, None]).astype(F32)
    return jnp.dot(jnp.transpose(rel_bias.astype(F32)), onehot, precision=HIGHEST)


def _shifted_chunks(bias_n, pad, n_chunks, width):
    n = min(bias_n.shape[1], n_chunks * width - pad)
    ext = jnp.concatenate([jnp.broadcast_to(bias_n[:, :1], (N_HEADS, pad)), bias_n[:, :n],
                           jnp.zeros((N_HEADS, n_chunks * width - pad - n), F32)], axis=1)
    return ext.reshape(N_HEADS, n_chunks, width)


def _bias_tables_kernel(ed_ref, ec_ref, tzs_ref, tzw_ref, cmp_ref, *, tq, tk, n_qt):
    n_ds, n_dw, n_j = tzs_ref.shape[1] - 1, tzw_ref.shape[1] - 1, cmp_ref.shape[1] // 8
    tzs_ref[0, n_ds] = jnp.full((tk, tq), NEG, F32)
    tzw_ref[0, n_dw] = jnp.full((tk, tq), NEG, F32)
    w = tq + tk
    c = lax.broadcasted_iota(jnp.int32, (tk, tq), 0)
    r = lax.broadcasted_iota(jnp.int32, (tk, tq), 1)
    for d in range(n_ds):
        v = jnp.concatenate([ed_ref[0, d:d + 1, :], ed_ref[0, d + 1:d + 2, :]], axis=1)
        t = pltpu.roll(jnp.broadcast_to(v, (tk, w)), w - (tk - 1), axis=1, stride=1, stride_axis=0)[:, :tq]
        dist = d * tk + r - c
        tzs_ref[0, d] = jnp.where(dist >= 0, t, NEG)
        if d < n_dw:
            tzw_ref[0, d] = jnp.where((dist >= 0) & (dist <= WINDOW), t, NEG)
    for j in range(n_j):
        dd = n_qt - 1 - j
        c0, c1 = max(dd, 0), max(dd + 1, 0)
        v = jnp.concatenate([ec_ref[0, c0:c0 + 1, :], ec_ref[0, c1:c1 + 1, :]], axis=1)
        t = pltpu.roll(jnp.broadcast_to(v, (8, w)), w - 7 * CMP_STRIDE, axis=1, stride=CMP_STRIDE, stride_axis=0)
        cmp_ref[0, j * 8:(j + 1) * 8, :] = t[:, :tq]


def _bias_tables(bias_n, n_qt, n_rb, n_ds, n_dw, tq, tk):
    assert tq == tk == 8 * CMP_STRIDE and n_dw <= n_ds
    n_j = n_rb + n_qt - 1
    ed = _shifted_chunks(bias_n, tk - 1, n_ds + 1, tq)
    ec = _shifted_chunks(bias_n, 7 * CMP_STRIDE + CMP_BLOCK - 1, n_qt + 1, tq)
    head = lambda a: pl.BlockSpec((1,) + a.shape[1:], lambda h: (h,) + (0,) * (a.ndim - 1))
    outs = (jax.ShapeDtypeStruct((N_HEADS, n_ds + 1, tk, tq), F32),
            jax.ShapeDtypeStruct((N_HEADS, n_dw + 1, tk, tq), F32),
            jax.ShapeDtypeStruct((N_HEADS, n_j * 8, tq), F32))
    tzs, tzw, cmp = pl.pallas_call(
        functools.partial(_bias_tables_kernel, tq=tq, tk=tk, n_qt=n_qt),
        out_shape=outs,
        grid=(N_HEADS,),
        in_specs=[head(ed), head(ec)],
        out_specs=tuple(head(o) for o in outs),
        compiler_params=_cparams("parallel"),
        name="bias_tables",
    )(ed, ec)
    grp = lambda a: a.reshape((N_KV_HEADS, GQA) + a.shape[1:])
    return grp(tzs), grp(tzw), cmp


def _pool_matrix(n_cmp_pad, n_blk_pad):
    r = SEL_BLOCK // CMP_STRIDE
    i = np.arange(n_cmp_pad)[None, :]
    j = np.arange(n_blk_pad)[:, None]
    return ((i >= r * j - 1) & (i <= r * j + r - 1)).astype(np.float32)


def _cmp_select_kernel(q_ref, k_ref, vt_ref, bias_ref, pool_ref, o_ref, sel_ref, *, tq, n_cmp):
    qt = pl.program_id(2)
    n_qt = pl.num_programs(2)
    q = q_ref[0, 0].reshape(GQA * tq, HEAD_DIM)
    k = k_ref[0, 0]
    nc = k.shape[0]
    s = _nt_dot(k, q)
    row0 = pl.multiple_of((n_qt - 1 - qt) * 8, 8)
    s = s + jnp.concatenate([bias_ref[g, pl.ds(row0, nc), :] for g in range(GQA)], axis=-1)
    t_pos = qt * tq + (lax.broadcasted_iota(jnp.int32, (nc, GQA * tq), 1) % tq)
    ci = lax.broadcasted_iota(jnp.int32, (nc, GQA * tq), 0)
    mask = (ci * CMP_STRIDE + CMP_BLOCK - 1 <= t_pos) & (ci < n_cmp)
    s = jnp.where(mask, s, NEG)
    m = jnp.max(s, axis=0, keepdims=True)
    p = jnp.where(mask, jnp.exp(s - m), 0.0)
    p = p / jnp.maximum(jnp.sum(p, axis=0, keepdims=True), 1e-30)
    ot = jnp.dot(vt_ref[0, 0], p.astype(BF16), preferred_element_type=F32)
    o_ref[0] = jnp.concatenate([ot[:, g * tq:(g + 1) * tq].T for g in range(GQA)], axis=-1)
    imp = p[:, 0:tq]
    for g in range(1, GQA):
        imp = imp + p[:, g * tq:(g + 1) * tq]
    sb = jnp.dot(pool_ref[...], imp, precision=HIGHEST, preferred_element_type=F32)
    nb = sb.shape[0]
    blk = lax.broadcasted_iota(jnp.int32, (nb, tq), 0)
    cur = (qt * tq + lax.broadcasted_iota(jnp.int32, (nb, tq), 1)) // SEL_BLOCK
    causal = blk <= cur
    forced = (blk == 0) | (blk == cur) | (blk == cur - 1)
    sc = jnp.where(forced & causal, 1e4, jnp.where(causal, sb, -1.0))
    groups = [sc[r:r + 8] for r in range(0, nb, 8)]
    sub = lax.broadcasted_iota(jnp.int32, (8, tq), 0)
    ranks = [jnp.zeros((8, tq), F32) for _ in groups]
    for i in range(nb):
        row = sc[i:i + 1, :]
        for gi, grp in enumerate(groups):
            if gi * 8 > i:
                ahead = row >= grp
            elif gi * 8 + 7 < i:
                ahead = row > grp
            else:
                ahead = (row > grp) | ((row == grp) & (sub > i - gi * 8))
            ranks[gi] = ranks[gi] + jnp.where(ahead, 1.0, 0.0)
    rank = jnp.concatenate(ranks, axis=0)
    sel_ref[0, 0] = jnp.where((rank < N_SEL) & causal, 0.0, NEG)


def _cmp_select_prompt(q5, kc, vct, bias_tab, pool, n_cmp):
    B, _, _, T, _ = q5.shape
    NC = kc.shape[2]
    NB = pool.shape[0]
    R = bias_tab.shape[1]
    tq = ATT_TQ
    return pl.pallas_call(
        functools.partial(_cmp_select_kernel, tq=tq, n_cmp=n_cmp),
        out_shape=(jax.ShapeDtypeStruct((B, T, D_ATT), F32),
                   jax.ShapeDtypeStruct((B, N_KV_HEADS, NB, T), F32)),
        grid=(B, N_KV_HEADS, T // tq),
        in_specs=[pl.BlockSpec((1, 1, GQA, tq, HEAD_DIM), lambda b, h, i: (b, h, 0, i, 0)),
                  pl.BlockSpec((1, 1, NC, HEAD_DIM), lambda b, h, i: (b, h, 0, 0)),
                  pl.BlockSpec((1, 1, HEAD_DIM, NC), lambda b, h, i: (b, h, 0, 0)),
                  pl.BlockSpec((GQA, R, tq), lambda b, h, i: (h, 0, 0)),
                  pl.BlockSpec((NB, NC), lambda b, h, i: (0, 0))],
        out_specs=(pl.BlockSpec((1, tq, GQA * HEAD_DIM), lambda b, h, i: (b, i, h)),
                   pl.BlockSpec((1, 1, NB, tq), lambda b, h, i: (b, h, 0, i))),
        compiler_params=_cparams("parallel", "parallel", "parallel"),
        name="cmp_select_prompt",
    )(q5, kc, vct, bias_tab, pool)


def _sel_win_kernel(q_ref, ks_ref, vst_ref, kw_ref, vwt_ref, sel_ref, tzs_ref, tzw_ref, os_ref, ow_ref, *, tq):
    tk = ATT_TK
    qt = pl.program_id(2)
    q = q_ref[0, 0].reshape(GQA * tq, HEAD_DIM)
    width = GQA * tq
    per_tile = tk // SEL_BLOCK

    def make_sweep(k_ref, vt_ref, tz_ref, use_sel, n_chains, single_trip):
        n_d = tz_ref.shape[2] - 1

        def scores(kt, hi):
            pad = kt > hi
            kt = jnp.minimum(kt, hi)
            off = pl.multiple_of(kt * tk, tk)
            k = k_ref[0, 0, pl.ds(off, tk), :]
            d = jnp.where(pad, n_d, jnp.minimum(qt - kt, n_d - 1))
            bias = [tz_ref[0, g, d] for g in range(GQA)]
            if use_sel:
                rows = sel_ref[0, 0, pl.ds(kt * per_tile, per_tile), :]
                selb = jnp.concatenate([jnp.broadcast_to(rows[i:i + 1], (SEL_BLOCK, tq))
                                        for i in range(per_tile)], axis=0)
                bias = [b + selb for b in bias]
            return _nt_dot(k, q) + jnp.concatenate(bias, axis=1)

        def values_t(kt, lo, hi):
            off = pl.multiple_of(jnp.clip(kt, lo, hi) * tk, tk)
            return vt_ref[0, 0, :, pl.ds(off, tk)]

        def sweep(lo, hi):
            n_trips = (hi - lo + n_chains) // n_chains
            chain0 = (jnp.full((1, width), 0.5 * NEG, F32), jnp.zeros((1, width), F32),
                      jnp.zeros((HEAD_DIM, width), F32), jnp.ones((1, width), F32), jnp.zeros((tk, width), BF16))

            def trip(i, chains):
                kt = lo + n_chains * i
                pv = [jnp.dot(values_t(kt - n_chains + c, lo, hi), chains[c][4], preferred_element_type=F32)
                      for c in range(n_chains)]
                ss = [scores(kt + c, hi) for c in range(n_chains)]
                out = []
                for c in range(n_chains):
                    m, l, acc, alpha_prev, _ = chains[c]
                    m_new = jnp.maximum(m, jnp.max(ss[c], axis=0, keepdims=True))
                    alpha = jnp.exp(m - m_new)
                    p = jnp.exp(ss[c] - m_new)
                    l = alpha * l + jnp.sum(p, axis=0, keepdims=True)
                    out.append((m_new, l, alpha_prev * acc + pv[c], alpha, p.astype(BF16)))
                return tuple(out)

            if single_trip:
                done = []
                for c in range(n_chains):
                    s = scores(lo + c, hi)
                    m = jnp.maximum(jnp.max(s, axis=0, keepdims=True), 0.5 * NEG)
                    p = jnp.exp(s - m)
                    done.append((m, jnp.sum(p, axis=0, keepdims=True),
                                 jnp.dot(values_t(lo + c, lo, hi), p.astype(BF16), preferred_element_type=F32)))
            else:
                chains = lax.fori_loop(0, n_trips, trip, (chain0,) * n_chains)
                kt_last = lo + n_chains * (n_trips - 1)
                done = []
                for c in range(n_chains):
                    m, l, acc, alpha, p = chains[c]
                    done.append((m, l, alpha * acc + jnp.dot(values_t(kt_last + c, lo, hi), p,
                                                              preferred_element_type=F32)))
            m_all = functools.reduce(jnp.maximum, [m for m, _, _ in done])
            num = den = 0.0
            for m, l, acc in done:
                e = jnp.exp(m - m_all)
                num = num + acc * e
                den = den + l * e
            o = num / jnp.maximum(den, 1e-30)
            return jnp.concatenate([o[:, g * tq:(g + 1) * tq].T for g in range(GQA)], axis=-1)
        return sweep

    n_win = tzw_ref.shape[2] - 1
    os_ref[0] = make_sweep(ks_ref, vst_ref, tzs_ref, True, SEL_CHAINS, False)(0, qt)
    ow_ref[0] = make_sweep(kw_ref, vwt_ref, tzw_ref, False, n_win, True)(jnp.maximum(qt - (n_win - 1), 0), qt)


def _sel_win_prompt(q5, ks, vst, kw, vwt, sel, tzs, tzw):
    B, _, _, T, _ = q5.shape
    NB = sel.shape[2]
    tq = ATT_TQ
    k_spec = pl.BlockSpec((1, 1, T, HEAD_DIM), lambda b, h, i: (b, h, 0, 0))
    vt_spec = pl.BlockSpec((1, 1, HEAD_DIM, T), lambda b, h, i: (b, h, 0, 0))
    tz_spec = lambda tz: pl.BlockSpec((1,) + tz.shape[1:], lambda b, h, i: (h, 0, 0, 0, 0))
    o_spec = pl.BlockSpec((1, tq, GQA * HEAD_DIM), lambda b, h, i: (b, i, h))
    return pl.pallas_call(
        functools.partial(_sel_win_kernel, tq=tq),
        out_shape=(jax.ShapeDtypeStruct((B, T, D_ATT), F32), jax.ShapeDtypeStruct((B, T, D_ATT), F32)),
        grid=(B, N_KV_HEADS, T // tq),
        in_specs=[pl.BlockSpec((1, 1, GQA, tq, HEAD_DIM), lambda b, h, i: (b, h, 0, i, 0)),
                  k_spec, vt_spec, k_spec, vt_spec,
                  pl.BlockSpec((1, 1, NB, tq), lambda b, h, i: (b, h, 0, i)),
                  tz_spec(tzs), tz_spec(tzw)],
        out_specs=(o_spec, o_spec),
        compiler_params=_cparams("parallel", "parallel", "parallel"),
        name="sel_win_prompt",
    )(q5, ks, vst, kw, vwt, sel, tzs, tzw)


def _gate_expand_matrix():
    m = np.zeros((3, 2 * LANE, D_ATT), np.float32)
    for r in range(3):
        for h in range(N_HEADS):
            m[r, h * 3 + r, h * HEAD_DIM:(h + 1) * HEAD_DIM] = 1.0
            m[r, LANE + h * 3 + r, h * HEAD_DIM:(h + 1) * HEAD_DIM] = 1.0
    return m


def _split_bf16(x):
    hi = x.astype(BF16)
    return hi, (x - hi.astype(F32)).astype(BF16)


def _post_mixer_kernel(y_ref, u_ref, oc_ref, os_ref, ow_ref, g_ref, x_ref, gate_ref, sh_ref, sc_ref,
                       dskip_ref, wglu_ref, bglu_ref, gexp_ref, wout_ref, lng_ref, lnb_ref,
                       wr_ref, br_ref, x1_ref, hm_ref, te_ref, tw_ref):
    y = y_ref[0] + dskip_ref[...] * u_ref[0]
    gl = jax.nn.gelu(y)
    ssm = gl * jax.nn.sigmoid(jnp.dot(gl.astype(BF16), wglu_ref[...], preferred_element_type=F32)
                              + bglu_ref[...])
    sg = jnp.concatenate(_split_bf16(jax.nn.sigmoid(g_ref[0])), axis=1)
    att = jnp.zeros_like(oc_ref[0])
    for r, o_ref in enumerate((oc_ref, os_ref, ow_ref)):
        att = att + jnp.dot(sg, gexp_ref[r], preferred_element_type=F32) * o_ref[0]
    h = (jnp.dot(ssm.astype(BF16), wout_ref[:D_SSM, :], preferred_element_type=F32)
         + jnp.dot(att.astype(BF16), wout_ref[D_SSM:, :], preferred_element_type=F32))
    z = DN_ALPHA * x_ref[0] + gate_ref[0] * h
    x1 = _layer_norm(z) * lng_ref[...] + lnb_ref[...]
    x1_ref[0] = x1
    hm = _layer_norm(x1) * (1.0 + sc_ref[0]) + sh_ref[0]
    hm_ref[0] = hm
    hm_hi, hm_lo = _split_bf16(hm)
    logits = (jnp.dot(hm_hi, wr_ref[0], preferred_element_type=F32)
              + jnp.dot(hm_lo, wr_ref[0], preferred_element_type=F32)
              + jnp.dot(hm_hi, wr_ref[1], preferred_element_type=F32)) + br_ref[...]
    lane = lax.broadcasted_iota(jnp.int32, logits.shape, 1)
    work = jnp.where(lane < N_EXPERTS, logits, -jnp.inf)
    te = jnp.zeros(logits.shape, jnp.int32)
    tv = jnp.zeros(logits.shape, F32)
    for k in range(TOP_K):
        best = jnp.max(work, axis=-1, keepdims=True)
        arg = jnp.min(jnp.where(work == best, lane, LANE), axis=-1, keepdims=True)
        te = jnp.where(lane == k, arg, te)
        tv = jnp.where(lane == k, best, tv)
        work = jnp.where(lane == arg, -jnp.inf, work)
    ex = jnp.where(lane < TOP_K, jnp.exp(tv - tv[:, 0:1]), 0.0)
    te_ref[0] = te
    tw_ref[0] = ex / jnp.sum(ex, axis=-1, keepdims=True)


def _post_mixer(y, u, oc, osel, ow, g, x, gate, shift, scale, w, tm):
    B, T, D = x.shape
    R = gate.shape[1]
    rb = 1 if R == 1 else tm
    mod_map = (lambda b, i: (b, 0, 0)) if R == 1 else (lambda b, i: (b, i, 0))
    row = lambda n: pl.BlockSpec((1, tm, n), lambda b, i: (b, i, 0))
    mod = pl.BlockSpec((1, rb, D), mod_map)
    full = lambda a: pl.BlockSpec(a.shape, lambda b, i: (0,) * a.ndim)
    consts = (w['d_skip'], w['w_glu'], w['b_glu'], w['gexp'], w['w_out'], w['ln1_g'], w['ln1_b'],
              w['w_router'], w['b_router'])
    return pl.pallas_call(
        _post_mixer_kernel,
        out_shape=(jax.ShapeDtypeStruct((B, T, D), F32), jax.ShapeDtypeStruct((B, T, D), F32),
                   jax.ShapeDtypeStruct((B, T, LANE), jnp.int32), jax.ShapeDtypeStruct((B, T, LANE), F32)),
        grid=(B, T // tm),
        in_specs=[row(D_SSM), row(D_SSM), row(D_ATT), row(D_ATT), row(D_ATT), row(LANE), row(D),
                  mod, mod, mod] + [full(a) for a in consts],
        out_specs=(row(D), row(D), row(LANE), row(LANE)),
        compiler_params=_cparams("parallel", "parallel"),
        name="post_mixer",
    )(y, u, oc, osel, ow, g, x, gate, shift, scale, *consts)


def _expert_kernel(e_ref, blk_ref, lo_ref, hi_ref, first_ref, x_ref, wgu_ref, bgu_ref, wd_ref, bd_ref, o_ref,
                   wgu_s, wd_s):
    i = pl.program_id(0)
    fresh = (i == 0) | (e_ref[i] != e_ref[jnp.maximum(i - 1, 0)])

    @pl.when(fresh)
    def _():
        wgu_s[...] = wgu_ref[0].astype(BF16)
        wd_s[...] = wd_ref[0].astype(BF16)

    @pl.when(first_ref[i] == 1)
    def _():
        o_ref[...] = jnp.zeros_like(o_ref)

    @pl.when(hi_ref[i] > lo_ref[i])
    def _():
        gu = jnp.dot(x_ref[...].astype(BF16), wgu_s[...], preferred_element_type=F32) + bgu_ref[0]
        gate = jnp.minimum(gu[:, :D_FF], SWIGLU_LIMIT)
        up = jnp.clip(gu[:, D_FF:], -SWIGLU_LIMIT, SWIGLU_LIMIT)
        hh = (up + 1.0) * gate * jax.nn.sigmoid(SWIGLU_ALPHA * gate)
        y = jnp.dot(hh.astype(BF16), wd_s[...], preferred_element_type=F32) + bd_ref[0]
        row = blk_ref[i] * MOE_ROWS + lax.broadcasted_iota(jnp.int32, (MOE_ROWS, 1), 0)
        o_ref[...] = jnp.where((row >= lo_ref[i]) & (row < hi_ref[i]), y, o_ref[...])


def _experts(xb, items, w_gate_up, b_gate_up, w_down, b_down):
    rows, D = xb.shape
    n_items = items[0].shape[0]
    wmap = lambda i, e, blk, lo, hi, first: (e[i], 0, 0)
    rmap = lambda i, e, blk, lo, hi, first: (blk[i], 0)
    grid_spec = pltpu.PrefetchScalarGridSpec(
        num_scalar_prefetch=5,
        grid=(n_items,),
        in_specs=[pl.BlockSpec((MOE_ROWS, D), rmap),
                  pl.BlockSpec((1, D, 2 * D_FF), wmap),
                  pl.BlockSpec((1, 1, 2 * D_FF), wmap),
                  pl.BlockSpec((1, D_FF, D), wmap),
                  pl.BlockSpec((1, 1, D), wmap)],
        out_specs=pl.BlockSpec((MOE_ROWS, D), rmap),
        scratch_shapes=[pltpu.VMEM((D, 2 * D_FF), BF16), pltpu.VMEM((D_FF, D), BF16)],
    )
    return pl.pallas_call(
        _expert_kernel,
        out_shape=jax.ShapeDtypeStruct((rows, D), F32),
        grid_spec=grid_spec,
        compiler_params=_cparams("arbitrary"),
        name="moe_experts",
    )(*items, xb, w_gate_up, b_gate_up.reshape(N_EXPERTS, 1, 2 * D_FF), w_down,
      b_down.reshape(N_EXPERTS, 1, D))


def _moe_dispatch(top_e, n):
    blk = MOE_ROWS
    nk = n * TOP_K
    cb = 128
    assert nk % cb == 0
    e = top_e.reshape(-1)
    onehot = (e[:, None] == jnp.arange(N_EXPERTS)[None, :]).astype(F32)
    oh3 = onehot.reshape(nk // cb, cb, N_EXPERTS)
    tri = jnp.asarray(np.tril(np.ones((cb, cb), np.float32), -1), dtype=BF16)
    within = jnp.einsum('ij,bje->bie', tri, oh3.astype(BF16), preferred_element_type=F32)
    blk_tot = jnp.sum(oh3, axis=1)
    blk_off = jnp.cumsum(blk_tot, axis=0) - blk_tot
    counts = jnp.sum(blk_tot, axis=0)
    start = jnp.cumsum(counts) - counts
    dest = jnp.sum((within + blk_off[:, None, :] + start[None, None, :]) * oh3, axis=-1)
    dest = dest.reshape(nk).astype(jnp.int32)
    order = jnp.argsort(dest)
    n_blk = -(-nk // blk)
    row_tok = jnp.concatenate([(order // TOP_K).astype(jnp.int32), jnp.full((n_blk * blk - nk,), n, jnp.int32)])
    counts_i, start_i = counts.astype(jnp.int32), start.astype(jnp.int32)
    first_b = start_i // blk
    last_b = (start_i + counts_i - 1) // blk
    n_it = jnp.where(counts_i > 0, last_b - first_b + 1, 0)
    it_end = jnp.cumsum(n_it)
    it_start = it_end - n_it
    n_items = n_blk + N_EXPERTS - 1
    i = jnp.arange(n_items)
    live = i < it_end[-1]
    it_e = jnp.minimum(jnp.sum(it_end[None, :] <= i[:, None], axis=1), N_EXPERTS - 1)
    it_blk = jnp.where(live, first_b[it_e] + i - it_start[it_e], n_blk - 1)
    it_lo = jnp.where(live, start_i[it_e], 0)
    it_hi = jnp.where(live, start_i[it_e] + counts_i[it_e], 0)
    it_first = jnp.concatenate([jnp.ones((1,), jnp.int32), (it_blk[1:] != it_blk[:-1]).astype(jnp.int32)])
    items = tuple(a.astype(jnp.int32) for a in (it_e, it_blk, it_lo, it_hi, it_first))
    return row_tok, dest.reshape(n, TOP_K), items


def _final_kernel(x_ref, y0_ref, y1_ref, y2_ref, y3_ref, tw_ref, gate_ref, lng_ref, lnb_ref, o_ref):
    tw = tw_ref[0]
    y = jnp.zeros_like(x_ref[0])
    for k, y_ref in enumerate((y0_ref, y1_ref, y2_ref, y3_ref)):
        y = y + tw[:, k:k + 1] * y_ref[0]
    z = DN_ALPHA * x_ref[0] + gate_ref[0] * y
    o_ref[0] = _layer_norm(z) * lng_ref[...] + lnb_ref[...]


def _final(x1, ys, tw, gate, ln_g, ln_b, tm):
    B, T, D = x1.shape
    R = gate.shape[1]
    rb = 1 if R == 1 else tm
    mod_map = (lambda b, i: (b, 0, 0)) if R == 1 else (lambda b, i: (b, i, 0))
    row = lambda n: pl.BlockSpec((1, tm, n), lambda b, i: (b, i, 0))
    vec = pl.BlockSpec((1, D), lambda b, i: (0, 0))
    return pl.pallas_call(
        _final_kernel,
        out_shape=jax.ShapeDtypeStruct((B, T, D), F32),
        grid=(B, T // tm),
        in_specs=[row(D), row(D), row(D), row(D), row(D), row(LANE),
                  pl.BlockSpec((1, rb, D), mod_map), vec, vec],
        out_specs=row(D),
        compiler_params=_cparams("parallel", "parallel"),
        name="moe_combine_ln",
    )(x1, *ys, tw, gate, ln_g, ln_b)


def _cmp_select_step_kernel(q_ref, kv_ref, bias_ref, pool_ref, o_ref, idx_ref, *, n_cmp, n_blk, q_pos):
    q = q_ref[0].astype(BF16)
    ncp = kv_ref.shape[1]
    nbp = pool_ref.shape[1]
    hd = HEAD_DIM
    kv = kv_ref[0]
    kb = [kv[:, h * hd:(h + 1) * hd].astype(BF16) for h in range(N_KV_HEADS)]
    vb = [kv[:, (N_KV_HEADS + h) * hd:(N_KV_HEADS + h + 1) * hd].astype(BF16) for h in range(N_KV_HEADS)]
    row = lax.broadcasted_iota(jnp.int32, (N_HEADS, 1), 0)
    first = row < GQA
    s = jnp.where(first, _nt_dot(q, kb[0]), _nt_dot(q, kb[1])) * (hd ** -0.5)
    s = s + bias_ref[...]
    ci = lax.broadcasted_iota(jnp.int32, (N_HEADS, ncp), 1)
    mask = (ci * CMP_STRIDE + CMP_BLOCK - 1 <= q_pos) & (ci < n_cmp)
    s = jnp.where(mask, s, NEG)
    m = jnp.max(s, axis=-1, keepdims=True)
    p = jnp.where(mask, jnp.exp(s - m), 0.0)
    p = p / jnp.maximum(jnp.sum(p, axis=-1, keepdims=True), 1e-30)
    pb = p.astype(BF16)
    o_ref[0] = jnp.where(first, jnp.dot(pb, vb[0], preferred_element_type=F32),
                         jnp.dot(pb, vb[1], preferred_element_type=F32))
    imp0 = jnp.sum(jnp.where(first, p, 0.0), axis=0, keepdims=True)
    imp1 = jnp.sum(jnp.where(first, 0.0, p), axis=0, keepdims=True)
    imp = jnp.where(first, imp0, imp1)
    sb = jnp.dot(imp, pool_ref[...], precision=HIGHEST, preferred_element_type=F32)
    cur = q_pos // SEL_BLOCK
    bi = lax.broadcasted_iota(jnp.int32, (nbp, nbp), 0)
    bj = lax.broadcasted_iota(jnp.int32, (nbp, nbp), 1)
    blk = lax.broadcasted_iota(jnp.int32, (1, nbp), 1)
    causal = blk <= cur
    forced = (blk == 0) | (blk == cur) | (blk == cur - 1)
    rsel = lax.broadcasted_iota(jnp.int32, (N_SEL, nbp), 0)
    for h in range(N_KV_HEADS):
        sc = jnp.where(forced & causal, 1e4, jnp.where(causal, sb[h * GQA:h * GQA + 1, :], -1.0))
        sc = jnp.where(blk < n_blk, sc, -2.0)
        scb = jnp.broadcast_to(sc, (nbp, nbp))
        col = jnp.sum(jnp.where(bi == bj, scb, 0.0), axis=1, keepdims=True)
        ahead = (col > scb) | ((col == scb) & (bi < bj))
        rank = jnp.sum(ahead.astype(jnp.int32), axis=0, keepdims=True)
        hit = jnp.broadcast_to(rank, (N_SEL, nbp)) == rsel
        idx = jnp.sum(jnp.where(hit, jnp.broadcast_to(blk, (N_SEL, nbp)), 0), axis=1, keepdims=True)
        idx_ref[0, h] = jnp.broadcast_to(idx, (N_SEL, LANE))


def _cmp_select_step(q, ckv, bias, pool, n_cmp, n_blk, q_pos):
    B = q.shape[0]
    NCp = ckv.shape[1]
    return pl.pallas_call(
        functools.partial(_cmp_select_step_kernel, n_cmp=n_cmp, n_blk=n_blk, q_pos=q_pos),
        out_shape=(jax.ShapeDtypeStruct((B, N_HEADS, HEAD_DIM), F32),
                   jax.ShapeDtypeStruct((B, N_KV_HEADS, N_SEL, LANE), jnp.int32)),
        grid=(B,),
        in_specs=[pl.BlockSpec((1, N_HEADS, HEAD_DIM), lambda b: (b, 0, 0)),
                  pl.BlockSpec((1, NCp, D_KV), lambda b: (b, 0, 0)),
                  pl.BlockSpec(bias.shape, lambda b: (0, 0)),
                  pl.BlockSpec(pool.shape, lambda b: (0, 0))],
        out_specs=(pl.BlockSpec((1, N_HEADS, HEAD_DIM), lambda b: (b, 0, 0)),
                   pl.BlockSpec((1, N_KV_HEADS, N_SEL, LANE), lambda b: (b, 0, 0, 0))),
        compiler_params=_cparams("parallel"),
        name="cmp_select_step",
    )(q, ckv, bias, pool)


def _sel_step_kernel(pg_ref, idx_ref, q_ref, *refs, n_past, q_pos):
    page_refs = refs[:N_SEL]
    new_ref, bias_ref, kpos_ref, o_ref = refs[N_SEL:]
    b, h = pl.program_id(0), pl.program_id(1)
    base = (b * N_KV_HEADS + h) * N_SEL
    kts, vts = [], []
    for j in range(N_SEL):
        is_new = idx_ref[base + j] >= n_past
        kts.append(jnp.where(is_new, new_ref[0, 0, 0], page_refs[j][0, 0, 0]))
        vts.append(jnp.where(is_new, new_ref[0, 1, 0], page_refs[j][0, 1, 0]))
    kt = jnp.concatenate(kts, axis=1).astype(BF16)
    vt = jnp.concatenate(vts, axis=1).astype(BF16)
    s = jnp.dot(q_ref[0].astype(BF16), kt, preferred_element_type=F32) * (HEAD_DIM ** -0.5) + bias_ref[0, 0]
    mask = kpos_ref[0, 0] <= q_pos
    s = jnp.where(mask, s, NEG)
    m = jnp.max(s, axis=-1, keepdims=True)
    p = jnp.where(mask, jnp.exp(s - m), 0.0)
    l = jnp.sum(p, axis=-1, keepdims=True)
    o_ref[0, 0] = _nt_dot(p.astype(BF16), vt) / jnp.maximum(l, 1e-30)


def _sel_step(q, pool_t, new_t, bias_sel, kpos, pages, idx_flat, n_past, q_pos):
    B = q.shape[0]
    nk = N_SEL * PAGE_SIZE
    slot = lambda b, h, j: (b * N_KV_HEADS + h) * N_SEL + j
    page_spec = lambda j: pl.BlockSpec((1, 2, 1, HEAD_DIM, PAGE_SIZE),
                                       lambda b, h, pg, ix, j=j: (pg[slot(b, h, j)], 0, h, 0, 0))
    grid_spec = pltpu.PrefetchScalarGridSpec(
        num_scalar_prefetch=2,
        grid=(B, N_KV_HEADS),
        in_specs=[pl.BlockSpec((1, N_HEADS, HEAD_DIM), lambda b, h, pg, ix: (b, 0, 0))]
        + [page_spec(j) for j in range(N_SEL)]
        + [pl.BlockSpec((1, 2, 1, HEAD_DIM, PAGE_SIZE), lambda b, h, pg, ix: (b, 0, h, 0, 0)),
           pl.BlockSpec((1, 1, N_HEADS, nk), lambda b, h, pg, ix: (b, h, 0, 0)),
           pl.BlockSpec((1, 1, 1, nk), lambda b, h, pg, ix: (b, h, 0, 0))],
        out_specs=pl.BlockSpec((1, 1, N_HEADS, HEAD_DIM), lambda b, h, pg, ix: (b, h, 0, 0)),
    )
    return pl.pallas_call(
        functools.partial(_sel_step_kernel, n_past=n_past, q_pos=q_pos),
        out_shape=jax.ShapeDtypeStruct((B, N_KV_HEADS, N_HEADS, HEAD_DIM), F32),
        grid_spec=grid_spec,
        compiler_params=_cparams("arbitrary", "arbitrary"),
        name="sel_step",
    )(pages, idx_flat, q, *([pool_t] * N_SEL), new_t, bias_sel, kpos)


def _win_step_kernel(q_ref, w_ref, new_ref, bias_ref, bias0_ref, o_ref):
    q = q_ref[0]
    qb = q.astype(BF16)
    row = lax.broadcasted_iota(jnp.int32, (N_HEADS, 1), 0)
    first = row < GQA
    w = w_ref[0]
    hd = HEAD_DIM
    kb = [w[:, h * hd:(h + 1) * hd].astype(BF16) for h in range(N_KV_HEADS)]
    vb = [w[:, (N_KV_HEADS + h) * hd:(N_KV_HEADS + h + 1) * hd].astype(BF16) for h in range(N_KV_HEADS)]
    s = jnp.where(first, _nt_dot(qb, kb[0]), _nt_dot(qb, kb[1])) * (hd ** -0.5) + bias_ref[...]
    new = new_ref[0]
    kn = jnp.where(first, new[:, 0:hd], new[:, hd:2 * hd])
    vn = jnp.where(first, new[:, 2 * hd:3 * hd], new[:, 3 * hd:])
    sn = jnp.sum(q * kn, axis=-1, keepdims=True) * (hd ** -0.5) + bias0_ref[...]
    m = jnp.maximum(jnp.max(s, axis=-1, keepdims=True), sn)
    p = jnp.exp(s - m)
    pn = jnp.exp(sn - m)
    l = jnp.sum(p, axis=-1, keepdims=True) + pn
    pb = p.astype(BF16)
    acc = jnp.where(first, jnp.dot(pb, vb[0], preferred_element_type=F32),
                    jnp.dot(pb, vb[1], preferred_element_type=F32)) + pn * vn
    o_ref[0] = acc / jnp.maximum(l, 1e-30)


def _win_step(q, win, new, bias, bias0):
    B, W, _ = win.shape
    return pl.pallas_call(
        _win_step_kernel,
        out_shape=jax.ShapeDtypeStruct((B, N_HEADS, HEAD_DIM), F32),
        grid=(B,),
        in_specs=[pl.BlockSpec((1, N_HEADS, HEAD_DIM), lambda b: (b, 0, 0)),
                  pl.BlockSpec((1, W, D_KV), lambda b: (b, 0, 0)),
                  pl.BlockSpec((1, 1, D_KV), lambda b: (b, 0, 0)),
                  pl.BlockSpec((N_HEADS, W), lambda b: (0, 0)),
                  pl.BlockSpec((N_HEADS, 1), lambda b: (0, 0))],
        out_specs=pl.BlockSpec((1, N_HEADS, HEAD_DIM), lambda b: (b, 0, 0)),
        compiler_params=_cparams("parallel"),
        name="win_step",
    )(q, win, new, bias, bias0)


def _split_heads(kv, dtype):
    B, L, _ = kv.shape
    kv5 = kv.reshape(B, L, 2, N_KV_HEADS, HEAD_DIM)
    return (jnp.transpose(kv5[:, :, 0], (0, 2, 1, 3)).astype(dtype),
            jnp.transpose(kv5[:, :, 1], (0, 2, 1, 3)).astype(dtype))


def _nsa_prompt(q5, kvc, ks, vst, kw, vwt, cmp_tab, rel_bias):
    B, T, _ = kvc.shape
    nc = T // CMP_STRIDE
    nb = T // SEL_BLOCK
    ckv = _compress_out([_compress_in(kvc.reshape(B, nc, CMP_STRIDE * D_KV), cmp_tab)], cmp_tab, nc)
    kc, vc = _split_heads(ckv, BF16)
    vct = jnp.transpose(vc, (0, 1, 3, 2))
    bias_n = _bias_by_distance(rel_bias, T)
    n_qt, n_kt = T // ATT_TQ, T // ATT_TK
    n_ds = min(n_kt, -(-(REL_MAX_DIST + ATT_TK - 1) // ATT_TK) + 1)
    n_dw = min(n_kt, WINDOW // ATT_TK + 1)
    tzs, tzw, bias_tab = _bias_tables(bias_n, n_qt, nc // 8, n_ds, n_dw, ATT_TQ, ATT_TK)
    pool = jnp.asarray(_pool_matrix(nc, nb))
    o_cmp, sel = _cmp_select_prompt(q5, kc, vct, bias_tab, pool, nc - 1)
    o_sel, o_win = _sel_win_prompt(q5, ks, vst, kw, vwt, sel, tzs, tzw)
    return o_cmp, o_sel, o_win


def _nsa_sample(q, kvc, kvs, kvw, pool_cmp, pool_sel, win_buf, page_table, cmp_tab, rel_bias):
    B = q.shape[0]
    n_pages = page_table.shape[1]
    past_len = n_pages * PAGE_SIZE
    q_pos = past_len
    lp = -(-(past_len + 1) // SEL_BLOCK) * SEL_BLOCK
    n_cmp = lp // CMP_STRIDE - 1
    n_blk = lp // SEL_BLOCK
    n_past_chunks = past_len // CMP_STRIDE
    n_tail = 8
    assert n_past_chunks + n_tail >= n_cmp + 1
    n_chunks = n_past_chunks + n_tail
    feature_major = lambda pool: jnp.transpose(pool, (0, 2, 3, 4, 1))
    z_past = _compress_in_paged(feature_major(pool_cmp), page_table, cmp_tab)
    tail = jnp.pad(kvc[:, None, :], ((0, 0), (0, n_tail * CMP_STRIDE - 1), (0, 0)))
    z_tail = _compress_in(tail.reshape(B, n_tail, CMP_STRIDE * D_KV), cmp_tab)
    ncp = -(-n_chunks // LANE) * LANE
    nbp = -(-n_blk // LANE) * LANE
    ckv = _compress_out([z_past, z_tail], cmp_tab, ncp)
    bias_n = _bias_by_distance(rel_bias, q_pos + 1)
    n_back = max((n_pages + 1) * PAGE_SIZE, ncp * CMP_STRIDE + CMP_BLOCK)
    back = jnp.concatenate([bias_n[:, ::-1], jnp.broadcast_to(bias_n[:, :1], (N_HEADS, n_back - q_pos - 1))], 1)
    bias_c = back[:, CMP_BLOCK - 1:CMP_BLOCK - 1 + ncp * CMP_STRIDE:CMP_STRIDE]
    pool = jnp.asarray(_pool_matrix(ncp, nbp).T)
    q3 = q.reshape(B, N_HEADS, HEAD_DIM)
    o_cmp, idx = _cmp_select_step(q3, ckv, bias_c, pool, n_cmp, n_blk, q_pos)
    idx = idx[..., 0]
    bpp = PAGE_SIZE // SEL_BLOCK
    n_past = n_pages * bpp
    lpage = idx // bpp
    pages = jnp.take_along_axis(page_table, jnp.minimum(lpage, n_pages - 1).reshape(B, -1), axis=1)
    new_t = jnp.pad(kvs.reshape(B, 2, N_KV_HEADS, HEAD_DIM, 1), ((0, 0),) * 4 + ((0, PAGE_SIZE - 1),))
    bias_page = jnp.transpose(back[:, :(n_pages + 1) * PAGE_SIZE].reshape(N_HEADS, n_pages + 1, PAGE_SIZE),
                              (1, 0, 2))
    bias_sel = jnp.transpose(bias_page[lpage], (0, 1, 3, 2, 4)).reshape(B, N_KV_HEADS, N_HEADS, -1)
    kpos = lpage[..., None] * PAGE_SIZE + jnp.arange(PAGE_SIZE)
    ok = (kpos // SEL_BLOCK == idx[..., None]) & (idx <= q_pos // SEL_BLOCK)[..., None]
    kpos = jnp.where(ok, kpos, q_pos + 1).reshape(B, N_KV_HEADS, 1, -1).astype(jnp.int32)
    o_sel = _sel_step(q3, feature_major(pool_sel), new_t, bias_sel, kpos, pages.reshape(-1).astype(jnp.int32),
                      idx.reshape(-1).astype(jnp.int32), n_past, q_pos)
    o_sel = jnp.concatenate([o_sel[:, h, h * GQA:(h + 1) * GQA] for h in range(N_KV_HEADS)], axis=1)
    wb = win_buf.shape[1]
    bias_w = bias_n[:, 1:wb + 1][:, ::-1]
    o_win = _win_step(q3, win_buf.reshape(B, wb, D_KV), kvw[:, None, :], bias_w, bias_n[:, 0:1])
    return o_cmp.reshape(B, D_ATT), o_sel.reshape(B, D_ATT), o_win.reshape(B, D_ATT)


def kernel(x_prompt, x_sample, cache_cmp_kv, cache_sel_kv, state_win_kv, state_ssm_re, state_ssm_im, page_table,
           c_prompt, c_sample, w_ada, b_ada, w_in, lam_re, lam_im, log_dt, b_re, b_im, c_re, c_im, d_skip,
           w_glu, b_glu, phi_pe, phi_w1, phi_b1, phi_w2, phi_b2, rel_bias, w_out, ln1_g, ln1_b,
           w_router, b_router, w_gate_up, b_gate_up, w_down, b_down, ln2_g, ln2_b):
    assert w_ada.shape[0] == DEPTH == 1
    l = 0
    Bp, T, D = x_prompt.shape
    Bs = x_sample.shape[0]
    kv_tail = (2, N_KV_HEADS, HEAD_DIM)

    n_c = Bp + Bs
    c_all = jnp.pad(jnp.concatenate([c_prompt, c_sample], 0), ((0, -n_c % 8), (0, 0)))
    m_all = _adaln(c_all, w_ada[l], b_ada[l])
    m_p = m_all[:Bp].reshape(Bp, 6, D)
    m_s = m_all[Bp:n_c].reshape(Bs, 6, D)
    mod_p = [m_p[:, i:i + 1, :] for i in range(6)]
    mod_s = [m_s[None, :, i, :] for i in range(6)]

    w_in_pad = jnp.pad(w_in[l], ((0, 0), (0, D_IN_PAD - D_IN))).astype(BF16)
    n_levels = max(1, int(math.log2(T // SSM_CHUNK)))
    ssm_tab = _ssm_tables(lam_re[l], lam_im[l], log_dt[l], b_re[l], b_im[l], c_re[l], c_im[l],
                          SSM_CHUNK, n_levels)
    cmp_tab = _compress_tables(phi_pe[l], phi_w1[l], phi_b1[l], phi_w2[l], phi_b2[l])
    w_post = dict(
        d_skip=d_skip[l].reshape(1, D_SSM), w_glu=w_glu[l].astype(BF16), b_glu=b_glu[l].reshape(1, D_SSM),
        gexp=jnp.asarray(_gate_expand_matrix(), dtype=BF16), w_out=w_out[l].astype(BF16),
        ln1_g=ln1_g[l].reshape(1, D), ln1_b=ln1_b[l].reshape(1, D),
        w_router=jnp.stack(_split_bf16(jnp.pad(w_router[l], ((0, 0), (0, LANE - N_EXPERTS))))),
        b_router=jnp.pad(b_router[l], (0, LANE - N_EXPERTS)).reshape(1, LANE))

    u, q5, kvc, kvs, kvw, g, ks, vst, kw, vwt = _mixer_in(x_prompt, mod_p[0], mod_p[1], w_in_pad, 512, True)
    y_ssm, h_p = _ssm_prompt(u, ssm_tab)
    o_cmp, o_sel, o_win = _nsa_prompt(q5, kvc, ks, vst, kw, vwt, cmp_tab, rel_bias)
    x1_p, hm_p, te_p, tw_p = _post_mixer(y_ssm, u, o_cmp, o_sel, o_win, g, x_prompt,
                                         mod_p[2], mod_p[3], mod_p[4], w_post, tm=256)

    u_s, q_s, kvc_s, kvs_s, kvw_s, g_s = _mixer_in(x_sample.reshape(1, Bs, D), mod_s[0], mod_s[1],
                                                   w_in_pad, Bs, False)
    y_s, h_s = _ssm_sample(u_s[0], state_ssm_re[l], state_ssm_im[l], ssm_tab, c_re[l], c_im[l])
    oc_s, os_s, ow_s = _nsa_sample(q_s[0].astype(F32), kvc_s[0], kvs_s[0], kvw_s[0], cache_cmp_kv[l],
                                   cache_sel_kv[l], state_win_kv[l], page_table, cmp_tab, rel_bias)
    x1_s, hm_s, te_s, tw_s = _post_mixer(y_s[None], u_s, oc_s[None], os_s[None], ow_s[None], g_s,
                                         x_sample.reshape(1, Bs, D), mod_s[2], mod_s[3], mod_s[4],
                                         w_post, tm=Bs)

    n_p = Bp * T
    n_all = n_p + Bs
    hm_all = jnp.concatenate([hm_p.reshape(n_p, D), hm_s.reshape(Bs, D)], 0)
    te_all = jnp.concatenate([te_p.reshape(n_p, LANE), te_s.reshape(Bs, LANE)], 0)[:, :TOP_K]
    row_tok, dest, items = _moe_dispatch(te_all, n_all)
    xb = jnp.concatenate([hm_all, jnp.zeros((1, D), F32)], 0)[row_tok]
    yb = _experts(xb, items, w_gate_up[l], b_gate_up[l], w_down[l], b_down[l])
    ys_p = [yb[dest[:n_p, k]].reshape(Bp, T, D) for k in range(TOP_K)]
    ys_s = [yb[dest[n_p:, k]].reshape(1, Bs, D) for k in range(TOP_K)]
    ln2g, ln2b = ln2_g[l].reshape(1, D), ln2_b[l].reshape(1, D)
    out_p = _final(x1_p, ys_p, tw_p, mod_p[5], ln2g, ln2b, tm=512)
    out_s = _final(x1_s, ys_s, tw_s, mod_s[5], ln2g, ln2b, tm=Bs)

    wlen = min(WINDOW, T)
    win_s = jnp.concatenate([state_win_kv[l], kvw_s[0].reshape(Bs, 1, *kv_tail)], 1)[:, -state_win_kv.shape[2]:]
    p_state = SSM_STATE
    return (out_p, out_s.reshape(Bs, 1, D),
            kvc.reshape(1, Bp, T, *kv_tail), kvc_s[0].reshape(1, Bs, 1, *kv_tail),
            kvs.reshape(1, Bp, T, *kv_tail), kvs_s[0].reshape(1, Bs, 1, *kv_tail),
            kvw[:, T - wlen:].reshape(1, Bp, wlen, *kv_tail), win_s[None],
            h_p[None, ..., :p_state], h_p[None, ..., p_state:],
            h_s[None, ..., :p_state], h_s[None, ..., p_state:])
```

```python
import functools
import math

import numpy as np
import jax
import jax.numpy as jnp
from jax import lax
from jax.experimental import pallas as pl
from jax.experimental.pallas import tpu as pltpu

DEPTH = 1
PAGE_SIZE = 128
D_SSM = 512
SSM_GROUP = 16
N_SSM_GROUPS = D_SSM // SSM_GROUP
SSM_STATE = 64
N_HEADS = 8
HEAD_DIM = 64
N_KV_HEADS = 2
GQA = N_HEADS // N_KV_HEADS
D_ATT = N_HEADS * HEAD_DIM
D_KV = 2 * N_KV_HEADS * HEAD_DIM
CMP_STRIDE = 16
CMP_BLOCK = 2 * CMP_STRIDE
SEL_BLOCK = 64
N_SEL = 16
WINDOW = 512
NUM_BUCKETS = 32
REL_MAX_DIST = 1024
N_EXPERTS = 32
TOP_K = 4
D_FF = 1024
SWIGLU_LIMIT = 7.0
SWIGLU_ALPHA = 1.702
DN_ALPHA = (2 * DEPTH) ** 0.25
D_IN = D_SSM + D_ATT + 3 * D_KV + 3 * N_HEADS
NEG = -1e30
F32 = jnp.float32
BF16 = jnp.bfloat16
HIGHEST = lax.Precision.HIGHEST

LANE = 128
D_IN_PAD = -(-D_IN // LANE) * LANE
SSM_CHUNK = 8
ATT_TQ = 128
ATT_TK = 128
SEL_CHAINS = 4
MOE_ROWS = 256
PAGES_PER_STEP = 32
PAGE_PARTS = 2
CHUNK_PITCH = 24
VMEM_LIMIT = 48 * 1024 * 1024
LN_EPS = 1e-5


def _cparams(*sem):
    return pltpu.CompilerParams(dimension_semantics=sem, vmem_limit_bytes=VMEM_LIMIT)


def _nt_dot(a, b):
    return lax.dot_general(a, b, (((1,), (1,)), ((), ())), preferred_element_type=F32)


def _layer_norm(x):
    mu = jnp.mean(x, axis=-1, keepdims=True)
    xc = x - mu
    var = jnp.mean(xc * xc, axis=-1, keepdims=True)
    return xc * lax.rsqrt(var + LN_EPS)


def _adaln_kernel(c_ref, w_ref, b_ref, o_ref):
    c = c_ref[...]
    s = c * jax.nn.sigmoid(c)
    o_ref[...] = jnp.dot(s, w_ref[...], precision=HIGHEST, preferred_element_type=F32) + b_ref[...]


def _adaln(c, w, b):
    n, d = c.shape
    dout = w.shape[1]
    tn = 1024
    return pl.pallas_call(
        _adaln_kernel,
        out_shape=jax.ShapeDtypeStruct((n, dout), F32),
        grid=(dout // tn,),
        in_specs=[pl.BlockSpec((n, d), lambda j: (0, 0)),
                  pl.BlockSpec((d, tn), lambda j: (0, j)),
                  pl.BlockSpec((1, tn), lambda j: (0, j))],
        out_specs=pl.BlockSpec((n, tn), lambda j: (0, j)),
        compiler_params=_cparams("arbitrary"),
        name="adaln",
    )(c, w, b.reshape(1, dout))


def _mixer_in_kernel(x_ref, sh_ref, sc_ref, w_ref, u_ref, q_ref, kvc_ref, kvs_ref, kvw_ref, g_ref, *att_refs):
    h = _layer_norm(x_ref[0]) * (1.0 + sc_ref[0]) + sh_ref[0]
    z = jnp.dot(h.astype(BF16), w_ref[...], preferred_element_type=F32)
    c0 = D_SSM
    c1 = c0 + D_ATT
    c2 = c1 + D_KV
    c3 = c2 + D_KV
    c4 = c3 + D_KV
    u_ref[0] = z[:, :c0]
    kvc_ref[0] = z[:, c1:c2]
    kvs_ref[0] = z[:, c2:c3]
    kvw_ref[0] = z[:, c3:c4]
    g_ref[0] = z[:, c4:c4 + LANE]
    if not att_refs:
        q_ref[0] = z[:, c0:c1].astype(BF16)
        return
    ks_ref, vst_ref, kw_ref, vwt_ref = att_refs
    hd, half = HEAD_DIM, N_KV_HEADS * HEAD_DIM
    for hq in range(N_HEADS):
        q_ref[0, hq // GQA, hq % GQA] = (z[:, c0 + hq * hd:c0 + (hq + 1) * hd] * (hd ** -0.5)).astype(BF16)
    for k_ref, vt_ref, base in ((ks_ref, vst_ref, c2), (kw_ref, vwt_ref, c3)):
        for hk in range(N_KV_HEADS):
            k_ref[0, hk] = z[:, base + hk * hd:base + (hk + 1) * hd].astype(BF16)
        vt = z[:, base + half:base + 2 * half].T
        vt_ref[0] = vt.reshape(N_KV_HEADS, hd, vt.shape[1]).astype(BF16)


def _mixer_in(x, shift, scale, w_pad, tm, attention_layouts):
    B, T, D = x.shape
    R = shift.shape[1]
    rb = 1 if R == 1 else tm
    mod_map = (lambda b, i: (b, 0, 0)) if R == 1 else (lambda b, i: (b, i, 0))
    row = lambda n: pl.BlockSpec((1, tm, n), lambda b, i: (b, i, 0))
    f32 = lambda n: jax.ShapeDtypeStruct((B, T, n), F32)
    if attention_layouts:
        q_shape = jax.ShapeDtypeStruct((B, N_KV_HEADS, GQA, T, HEAD_DIM), BF16)
        q_spec = pl.BlockSpec((1, N_KV_HEADS, GQA, tm, HEAD_DIM), lambda b, i: (b, 0, 0, i, 0))
        k_shape = jax.ShapeDtypeStruct((B, N_KV_HEADS, T, HEAD_DIM), BF16)
        k_spec = pl.BlockSpec((1, N_KV_HEADS, tm, HEAD_DIM), lambda b, i: (b, 0, i, 0))
        vt_shape = jax.ShapeDtypeStruct((B, N_KV_HEADS, HEAD_DIM, T), BF16)
        vt_spec = pl.BlockSpec((1, N_KV_HEADS, HEAD_DIM, tm), lambda b, i: (b, 0, 0, i))
        extra_shapes, extra_specs = (k_shape, vt_shape, k_shape, vt_shape), (k_spec, vt_spec, k_spec, vt_spec)
    else:
        q_shape, q_spec = jax.ShapeDtypeStruct((B, T, D_ATT), BF16), row(D_ATT)
        extra_shapes, extra_specs = (), ()
    return pl.pallas_call(
        _mixer_in_kernel,
        out_shape=(f32(D_SSM), q_shape, f32(D_KV), f32(D_KV), f32(D_KV), f32(LANE)) + extra_shapes,
        grid=(B, T // tm),
        in_specs=[row(D), pl.BlockSpec((1, rb, D), mod_map), pl.BlockSpec((1, rb, D), mod_map),
                  pl.BlockSpec((D, D_IN_PAD), lambda b, i: (0, 0))],
        out_specs=(row(D_SSM), q_spec, row(D_KV), row(D_KV), row(D_KV), row(LANE)) + extra_specs,
        compiler_params=_cparams("parallel", "parallel"),
        name="mixer_in",
    )(x, shift, scale, w_pad)


def _ssm_tables(lam_re, lam_im, log_dt, b_re, b_im, c_re, c_im, L, n_levels):
    G, P = lam_re.shape
    C = b_re.shape[-1]
    dt = jnp.exp(log_dt.astype(F32))[:, None]
    er, ei = lam_re * dt, lam_im * dt

    def power(k):
        kk = k.astype(F32)[:, None, None]
        mag = jnp.exp(kk * er)
        return mag * jnp.cos(kk * ei), mag * jnp.sin(kk * ei)

    lb_re, lb_im = power(jnp.ones((1,), F32))
    nr, ni = lb_re[0] - 1.0, lb_im[0]
    den = lam_re * lam_re + lam_im * lam_im
    fr = (nr * lam_re + ni * lam_im) / den
    fi = (ni * lam_re - nr * lam_im) / den
    bbr = fr[:, :, None] * b_re - fi[:, :, None] * b_im
    bbi = fr[:, :, None] * b_im + fi[:, :, None] * b_re
    pr, pi = power(jnp.arange(L + 1))
    clr = c_re[None] * pr[:, :, None, :] - c_im[None] * pi[:, :, None, :]
    cli = c_re[None] * pi[:, :, None, :] + c_im[None] * pr[:, :, None, :]
    kern = (jnp.einsum('kgcp,gpd->kgcd', clr[:L], bbr, precision=HIGHEST)
            - jnp.einsum('kgcp,gpd->kgcd', cli[:L], bbi, precision=HIGHEST))
    GP = LANE // C
    X = G // GP
    eye = jnp.eye(GP, dtype=BF16)
    place_einsum = functools.partial(jnp.einsum, preferred_element_type=BF16)
    kblk = place_einsum('kxhcd,hj->xkhdjc', kern.astype(BF16).reshape(L, X, GP, C, C), eye)
    kblk = kblk.reshape(X, L, LANE, LANE)
    prr, pir = pr[:L][::-1], pi[:L][::-1]
    ws2 = jnp.stack([prr[..., None] * bbr[None] - pir[..., None] * bbi[None],
                     prr[..., None] * bbi[None] + pir[..., None] * bbr[None]])
    ws = place_einsum('rsxhpd,hj->xshdrjp', ws2.astype(BF16).reshape(2, L, X, GP, P, C), eye)
    ws = ws.reshape(X, L * LANE, 2 * GP * P)
    wy2 = jnp.stack([clr[1:], -cli[1:]])
    wy = place_einsum('rtxhcp,hj->xrhptjc', wy2.astype(BF16).reshape(2, L, X, GP, C, P), eye)
    wy = wy.reshape(X, 2 * GP * P, L * LANE)
    lr, li = power(L * (2 ** jnp.arange(n_levels)))
    lr, li = lr.reshape(n_levels, X, GP * P), li.reshape(n_levels, X, GP * P)
    ar = jnp.transpose(jnp.concatenate([lr, lr], -1), (1, 0, 2))
    ai = jnp.transpose(jnp.concatenate([-li, li], -1), (1, 0, 2))
    return kblk, ws, wy, ar, ai, (lb_re[0], lb_im[0], bbr, bbi)


def _ssm_kernel(u_ref, kblk_ref, ws_ref, wy_ref, ar_ref, ai_ref, y_ref, hl_ref, toep_ref, *, L, nc, n_levels):
    for s in range(L):
        for t in range(L):
            blk = kblk_ref[0, t - s] if t >= s else jnp.zeros((LANE, LANE), BF16)
            toep_ref[s * LANE:(s + 1) * LANE, t * LANE:(t + 1) * LANE] = blk
    u = jnp.concatenate([u_ref[0, pl.ds(t, nc, stride=L), :] for t in range(L)], axis=1).astype(BF16)
    y1 = jnp.dot(u, toep_ref[...], preferred_element_type=F32)
    h = jnp.dot(u, ws_ref[0], preferred_element_type=F32)
    w2 = h.shape[-1]
    rows = lax.broadcasted_iota(jnp.int32, (nc, w2), 0)
    for k in range(n_levels):
        d = 1 << k
        sh = jnp.where(rows >= d, pltpu.roll(h, d, axis=0), 0.0)
        sw = pltpu.roll(sh, w2 // 2, axis=1)
        h = h + ar_ref[0, k:k + 1, :] * sh + ai_ref[0, k:k + 1, :] * sw
    hl_ref[0, 0] = h[nc - 1:nc, :]
    hp = jnp.where(rows >= 1, pltpu.roll(h, 1, axis=0), 0.0)
    y = y1 + jnp.dot(hp.astype(BF16), wy_ref[0], preferred_element_type=F32)
    for t in range(L):
        y_ref[0, pl.ds(t, nc, stride=L), :] = y[:, t * LANE:(t + 1) * LANE]


def _ssm_prompt(u, tables):
    kblk, ws, wy, ar, ai, _ = tables
    B, T, _ = u.shape
    L, P = SSM_CHUNK, SSM_STATE
    X, n_levels, w2 = ar.shape
    GP = w2 // (2 * P)
    nc = T // L
    tab = lambda a: pl.BlockSpec((1,) + a.shape[1:], lambda x, b: (x,) + (0,) * (a.ndim - 1))
    seq = pl.BlockSpec((1, T, LANE), lambda x, b: (b, 0, x))
    y, hl = pl.pallas_call(
        functools.partial(_ssm_kernel, L=L, nc=nc, n_levels=n_levels),
        out_shape=(jax.ShapeDtypeStruct((B, T, D_SSM), F32), jax.ShapeDtypeStruct((X, B, 1, w2), F32)),
        grid=(X, B),
        in_specs=[seq, tab(kblk), tab(ws), tab(wy), tab(ar), tab(ai)],
        out_specs=(seq, pl.BlockSpec((1, 1, 1, w2), lambda x, b: (x, b, 0, 0))),
        scratch_shapes=[pltpu.VMEM((L * LANE, L * LANE), BF16)],
        compiler_params=_cparams("parallel", "parallel"),
        name="ssm_prompt",
    )(u, kblk, ws, wy, ar, ai)
    hl = jnp.transpose(hl.reshape(X, B, 2, GP, P), (1, 0, 3, 2, 4))
    return y, hl.reshape(B, X * GP, 2 * P)


def _ssm_step_kernel(u_ref, h0_ref, bb_ref, lr_ref, li_ref, cy_ref, y_ref, h_ref):
    p = lr_ref.shape[-1] // 2
    bu = jnp.einsum('gbc,gcp->gbp', u_ref[...], bb_ref[...], preferred_element_type=F32)
    h0 = h0_ref[...]
    h0s = jnp.concatenate([h0[..., p:], h0[..., :p]], axis=-1)
    h = lr_ref[...] * h0 + li_ref[...] * h0s + bu
    h_ref[...] = h
    y_ref[...] = jnp.einsum('gbp,gpc->gbc', h.astype(BF16), cy_ref[...], preferred_element_type=F32)


def _ssm_sample(u, h0_re, h0_im, tables, c_re, c_im):
    lb_re, lb_im, bbr, bbi = tables[-1]
    B = u.shape[0]
    G, C, P = N_SSM_GROUPS, SSM_GROUP, SSM_STATE
    ug = jnp.transpose(u.reshape(B, G, C), (1, 0, 2)).astype(BF16)
    h0 = jnp.transpose(jnp.concatenate([h0_re, h0_im], -1), (1, 0, 2)).astype(F32)
    bb = jnp.concatenate([jnp.transpose(bbr, (0, 2, 1)), jnp.transpose(bbi, (0, 2, 1))], -1).astype(BF16)
    lr = jnp.concatenate([lb_re, lb_re], -1)[:, None, :]
    li = jnp.concatenate([-lb_im, lb_im], -1)[:, None, :]
    cy = jnp.concatenate([jnp.transpose(c_re, (0, 2, 1)), -jnp.transpose(c_im, (0, 2, 1))], 1).astype(BF16)
    y, h = pl.pallas_call(
        _ssm_step_kernel,
        out_shape=(jax.ShapeDtypeStruct((G, B, C), F32), jax.ShapeDtypeStruct((G, B, 2 * P), F32)),
        name="ssm_step",
    )(ug, h0, bb, lr, li, cy)
    return jnp.transpose(y, (1, 0, 2)).reshape(B, D_SSM), jnp.transpose(h, (1, 0, 2))


def _compress_tables(phi_pe, phi_w1, phi_b1, phi_w2, phi_b2):
    S, H, Dh = CMP_STRIDE, N_KV_HEADS, HEAD_DIM
    w1 = phi_w1.reshape(2, 2, S, Dh, Dh)
    eye_c = jnp.eye(2, dtype=F32)
    eye_h = jnp.eye(H, dtype=F32)
    wbig = jnp.einsum('cajde,xc,yh->jxydache', w1, eye_c, eye_h).reshape(S * 2 * H * Dh, 2 * 2 * H * Dh)
    pe = jnp.transpose(phi_pe.reshape(2, 2, S, Dh), (1, 2, 0, 3))
    pe_rows = jnp.broadcast_to(pe[:, :, :, None, :], (2, S, 2, H, Dh)).reshape(2, S * 2 * H * Dh)
    n = 2 * H * Dh
    pe_w = (jnp.dot(pe_rows[0], wbig[:, :n], precision=HIGHEST) + jnp.dot(pe_rows[1], wbig[:, n:], precision=HIGHEST))
    b1 = jnp.broadcast_to(phi_b1[:, None, :], (2, H, Dh)).reshape(1, n) + pe_w[None, :]
    w2 = jnp.einsum('cef,cx,hy->chexyf', phi_w2, eye_c, eye_h).reshape(n, n)
    b2 = jnp.broadcast_to(phi_b2[:, None, :], (2, H, Dh)).reshape(1, n)
    return wbig.astype(BF16), b1, w2.astype(BF16), b2


def _compress_in_kernel(x_ref, w_ref, z_ref):
    z_ref[0] = jnp.dot(x_ref[0].astype(BF16), w_ref[...], preferred_element_type=F32)


def _compress_in(x2, tables):
    wbig = tables[0]
    N2 = wbig.shape[1]
    B, n, K = x2.shape
    tr = math.gcd(n, 256)
    return pl.pallas_call(
        _compress_in_kernel,
        out_shape=jax.ShapeDtypeStruct((B, n, N2), F32),
        grid=(B, n // tr),
        in_specs=[pl.BlockSpec((1, tr, K), lambda b, i: (b, i, 0)),
                  pl.BlockSpec((K, N2), lambda b, i: (0, 0))],
        out_specs=pl.BlockSpec((1, tr, N2), lambda b, i: (b, i, 0)),
        compiler_params=_cparams("parallel", "parallel"),
        name="compress_in",
    )(x2, wbig)


def _compress_in_paged_kernel(pt_ref, *refs, n_pg):
    x_refs = refs[:n_pg]
    w_ref, z_ref = refs[n_pg:n_pg + 2]
    scratch = refs[n_pg + 2:]
    n_slab = D_KV // LANE
    pg_part = n_pg // PAGE_PARTS
    cpp = PAGE_SIZE // CMP_STRIDE
    rows = pg_part * cpp
    for part in range(PAGE_PARTS):
        s_refs = scratch[part * n_slab:(part + 1) * n_slab]
        for k in range(pg_part):
            t = x_refs[part * pg_part + k][0].reshape(D_KV, PAGE_SIZE).T
            for c, s_ref in enumerate(s_refs):
                for n in range(cpp):
                    r0 = (k * cpp + n) * CHUNK_PITCH
                    s_ref[r0:r0 + CMP_STRIDE, :] = t[n * CMP_STRIDE:(n + 1) * CMP_STRIDE, c * LANE:(c + 1) * LANE]
        z = jnp.zeros((rows, w_ref.shape[1]), F32)
        for j in range(CMP_STRIDE):
            xj = jnp.concatenate([s_ref[pl.ds(j, rows, stride=CHUNK_PITCH), :] for s_ref in s_refs], axis=1)
            z = z + jnp.dot(xj.astype(BF16), w_ref[j * D_KV:(j + 1) * D_KV, :], preferred_element_type=F32)
        z_ref[0, part * rows:(part + 1) * rows, :] = z


def _compress_in_paged(pool_t, page_table, tables):
    wbig = tables[0]
    N2 = wbig.shape[1]
    K = wbig.shape[0]
    B, n_pages = page_table.shape
    n_pg = math.gcd(n_pages, PAGES_PER_STEP)
    rows = n_pg * PAGE_SIZE // CMP_STRIDE
    page_spec = lambda k: pl.BlockSpec((1,) + pool_t.shape[1:],
                                       lambda b, i, pt, k=k: (pt[b, i * n_pg + k], 0, 0, 0, 0))
    grid_spec = pltpu.PrefetchScalarGridSpec(
        num_scalar_prefetch=1,
        grid=(B, n_pages // n_pg),
        in_specs=[page_spec(k) for k in range(n_pg)] + [pl.BlockSpec((K, N2), lambda b, i, pt: (0, 0))],
        out_specs=pl.BlockSpec((1, rows, N2), lambda b, i, pt: (b, i, 0)),
        scratch_shapes=[pltpu.VMEM((rows // PAGE_PARTS * CHUNK_PITCH, LANE), F32)
                        for _ in range(PAGE_PARTS * (D_KV // LANE))],
    )
    return pl.pallas_call(
        functools.partial(_compress_in_paged_kernel, n_pg=n_pg),
        out_shape=jax.ShapeDtypeStruct((B, n_pages * PAGE_SIZE // CMP_STRIDE, N2), F32),
        grid_spec=grid_spec,
        compiler_params=_cparams("arbitrary", "arbitrary"),
        name="compress_in_paged",
    )(page_table, *([pool_t] * n_pg), wbig)


def _compress_out_kernel(*refs):
    z_refs, (b1_ref, w2_ref, b2_ref, o_ref) = refs[:-4], refs[-4:]
    z = jnp.concatenate([z_ref[0] for z_ref in z_refs], axis=0)
    n = z.shape[-1] // 2
    rows = z.shape[0]
    second = pltpu.roll(z[:, n:], rows - 1, axis=0)
    hdn = jax.nn.gelu(z[:, :n] + second + b1_ref[...])
    o_ref[0, :rows, :] = jnp.dot(hdn.astype(BF16), w2_ref[...], preferred_element_type=F32) + b2_ref[...]
    if o_ref.shape[1] > rows:
        o_ref[0, rows:, :] = jnp.zeros((o_ref.shape[1] - rows, n), F32)


def _compress_out(zs, tables, n_out):
    _, b1, w2, b2 = tables
    B, _, N2 = zs[0].shape
    return pl.pallas_call(
        _compress_out_kernel,
        out_shape=jax.ShapeDtypeStruct((B, n_out, N2 // 2), F32),
        grid=(B,),
        in_specs=[pl.BlockSpec((1, z.shape[1], N2), lambda b: (b, 0, 0)) for z in zs] + [
                  pl.BlockSpec((1, N2 // 2), lambda b: (0, 0)),
                  pl.BlockSpec((N2 // 2, N2 // 2), lambda b: (0, 0)),
                  pl.BlockSpec((1, N2 // 2), lambda b: (0, 0))],
        out_specs=pl.BlockSpec((1, n_out, N2 // 2), lambda b: (b, 0, 0)),
        compiler_params=_cparams("parallel"),
        name="compress_out",
    )(*zs, b1, w2, b2)


def _rel_bucket(dist):
    n = jnp.maximum(dist, 0)
    max_exact = NUM_BUCKETS // 2
    nf = jnp.maximum(n, 1).astype(F32)
    large = max_exact + (jnp.log(nf / max_exact) / math.log(REL_MAX_DIST / max_exact)
                         * (NUM_BUCKETS - max_exact)).astype(jnp.int32)
    large = jnp.minimum(large, NUM_BUCKETS - 1)
    return jnp.where(n < max_exact, n, large)


def _bias_by_distance(rel_bias, n_max):
    onehot = (_rel_bucket(jnp.arange(n_max))[None, :] == jnp.arange(NUM_BUCKETS)[:, None]).astype(F32)
    return jnp.dot(jnp.transpose(rel_bias.astype(F32)), onehot, precision=HIGHEST)


def _shifted_chunks(bias_n, pad, n_chunks, width):
    n = min(bias_n.shape[1], n_chunks * width - pad)
    ext = jnp.concatenate([jnp.broadcast_to(bias_n[:, :1], (N_HEADS, pad)), bias_n[:, :n],
                           jnp.zeros((N_HEADS, n_chunks * width - pad - n), F32)], axis=1)
    return ext.reshape(N_HEADS, n_chunks, width)


def _bias_tables_kernel(ed_ref, ec_ref, tzs_ref, tzw_ref, cmp_ref, *, tq, tk, n_qt):
    n_ds, n_dw, n_j = tzs_ref.shape[1] - 1, tzw_ref.shape[1] - 1, cmp_ref.shape[1] // 8
    tzs_ref[0, n_ds] = jnp.full((tk, tq), NEG, F32)
    tzw_ref[0, n_dw] = jnp.full((tk, tq), NEG, F32)
    w = tq + tk
    c = lax.broadcasted_iota(jnp.int32, (tk, tq), 0)
    r = lax.broadcasted_iota(jnp.int32, (tk, tq), 1)
    for d in range(n_ds):
        v = jnp.concatenate([ed_ref[0, d:d + 1, :], ed_ref[0, d + 1:d + 2, :]], axis=1)
        t = pltpu.roll(jnp.broadcast_to(v, (tk, w)), w - (tk - 1), axis=1, stride=1, stride_axis=0)[:, :tq]
        dist = d * tk + r - c
        tzs_ref[0, d] = jnp.where(dist >= 0, t, NEG)
        if d < n_dw:
            tzw_ref[0, d] = jnp.where((dist >= 0) & (dist <= WINDOW), t, NEG)
    for j in range(n_j):
        dd = n_qt - 1 - j
        c0, c1 = max(dd, 0), max(dd + 1, 0)
        v = jnp.concatenate([ec_ref[0, c0:c0 + 1, :], ec_ref[0, c1:c1 + 1, :]], axis=1)
        t = pltpu.roll(jnp.broadcast_to(v, (8, w)), w - 7 * CMP_STRIDE, axis=1, stride=CMP_STRIDE, stride_axis=0)
        cmp_ref[0, j * 8:(j + 1) * 8, :] = t[:, :tq]


def _bias_tables(bias_n, n_qt, n_rb, n_ds, n_dw, tq, tk):
    assert tq == tk == 8 * CMP_STRIDE and n_dw <= n_ds
    n_j = n_rb + n_qt - 1
    ed = _shifted_chunks(bias_n, tk - 1, n_ds + 1, tq)
    ec = _shifted_chunks(bias_n, 7 * CMP_STRIDE + CMP_BLOCK - 1, n_qt + 1, tq)
    head = lambda a: pl.BlockSpec((1,) + a.shape[1:], lambda h: (h,) + (0,) * (a.ndim - 1))
    outs = (jax.ShapeDtypeStruct((N_HEADS, n_ds + 1, tk, tq), F32),
            jax.ShapeDtypeStruct((N_HEADS, n_dw + 1, tk, tq), F32),
            jax.ShapeDtypeStruct((N_HEADS, n_j * 8, tq), F32))
    tzs, tzw, cmp = pl.pallas_call(
        functools.partial(_bias_tables_kernel, tq=tq, tk=tk, n_qt=n_qt),
        out_shape=outs,
        grid=(N_HEADS,),
        in_specs=[head(ed), head(ec)],
        out_specs=tuple(head(o) for o in outs),
        compiler_params=_cparams("parallel"),
        name="bias_tables",
    )(ed, ec)
    grp = lambda a: a.reshape((N_KV_HEADS, GQA) + a.shape[1:])
    return grp(tzs), grp(tzw), cmp


def _pool_matrix(n_cmp_pad, n_blk_pad):
    r = SEL_BLOCK // CMP_STRIDE
    i = np.arange(n_cmp_pad)[None, :]
    j = np.arange(n_blk_pad)[:, None]
    return ((i >= r * j - 1) & (i <= r * j + r - 1)).astype(np.float32)


def _cmp_select_kernel(q_ref, k_ref, vt_ref, bias_ref, pool_ref, o_ref, sel_ref, *, tq, n_cmp):
    qt = pl.program_id(2)
    n_qt = pl.num_programs(2)
    q = q_ref[0, 0].reshape(GQA * tq, HEAD_DIM)
    k = k_ref[0, 0]
    nc = k.shape[0]
    s = _nt_dot(k, q)
    row0 = pl.multiple_of((n_qt - 1 - qt) * 8, 8)
    s = s + jnp.concatenate([bias_ref[g, pl.ds(row0, nc), :] for g in range(GQA)], axis=-1)
    t_pos = qt * tq + (lax.broadcasted_iota(jnp.int32, (nc, GQA * tq), 1) % tq)
    ci = lax.broadcasted_iota(jnp.int32, (nc, GQA * tq), 0)
    mask = (ci * CMP_STRIDE + CMP_BLOCK - 1 <= t_pos) & (ci < n_cmp)
    s = jnp.where(mask, s, NEG)
    m = jnp.max(s, axis=0, keepdims=True)
    p = jnp.where(mask, jnp.exp(s - m), 0.0)
    p = p / jnp.maximum(jnp.sum(p, axis=0, keepdims=True), 1e-30)
    ot = jnp.dot(vt_ref[0, 0], p.astype(BF16), preferred_element_type=F32)
    o_ref[0] = jnp.concatenate([ot[:, g * tq:(g + 1) * tq].T for g in range(GQA)], axis=-1)
    imp = p[:, 0:tq]
    for g in range(1, GQA):
        imp = imp + p[:, g * tq:(g + 1) * tq]
    sb = jnp.dot(pool_ref[...], imp, precision=HIGHEST, preferred_element_type=F32)
    nb = sb.shape[0]
    blk = lax.broadcasted_iota(jnp.int32, (nb, tq), 0)
    cur = (qt * tq + lax.broadcasted_iota(jnp.int32, (nb, tq), 1)) // SEL_BLOCK
    causal = blk <= cur
    forced = (blk == 0) | (blk == cur) | (blk == cur - 1)
    sc = jnp.where(forced & causal, 1e4, jnp.where(causal, sb, -1.0))
    groups = [sc[r:r + 8] for r in range(0, nb, 8)]
    sub = lax.broadcasted_iota(jnp.int32, (8, tq), 0)
    ranks = [jnp.zeros((8, tq), F32) for _ in groups]
    for i in range(nb):
        row = sc[i:i + 1, :]
        for gi, grp in enumerate(groups):
            if gi * 8 > i:
                ahead = row >= grp
            elif gi * 8 + 7 < i:
                ahead = row > grp
            else:
                ahead = (row > grp) | ((row == grp) & (sub > i - gi * 8))
            ranks[gi] = ranks[gi] + jnp.where(ahead, 1.0, 0.0)
    rank = jnp.concatenate(ranks, axis=0)
    sel_ref[0, 0] = jnp.where((rank < N_SEL) & causal, 0.0, NEG)


def _cmp_select_prompt(q5, kc, vct, bias_tab, pool, n_cmp):
    B, _, _, T, _ = q5.shape
    NC = kc.shape[2]
    NB = pool.shape[0]
    R = bias_tab.shape[1]
    tq = ATT_TQ
    return pl.pallas_call(
        functools.partial(_cmp_select_kernel, tq=tq, n_cmp=n_cmp),
        out_shape=(jax.ShapeDtypeStruct((B, T, D_ATT), F32),
                   jax.ShapeDtypeStruct((B, N_KV_HEADS, NB, T), F32)),
        grid=(B, N_KV_HEADS, T // tq),
        in_specs=[pl.BlockSpec((1, 1, GQA, tq, HEAD_DIM), lambda b, h, i: (b, h, 0, i, 0)),
                  pl.BlockSpec((1, 1, NC, HEAD_DIM), lambda b, h, i: (b, h, 0, 0)),
                  pl.BlockSpec((1, 1, HEAD_DIM, NC), lambda b, h, i: (b, h, 0, 0)),
                  pl.BlockSpec((GQA, R, tq), lambda b, h, i: (h, 0, 0)),
                  pl.BlockSpec((NB, NC), lambda b, h, i: (0, 0))],
        out_specs=(pl.BlockSpec((1, tq, GQA * HEAD_DIM), lambda b, h, i: (b, i, h)),
                   pl.BlockSpec((1, 1, NB, tq), lambda b, h, i: (b, h, 0, i))),
        compiler_params=_cparams("parallel", "parallel", "parallel"),
        name="cmp_select_prompt",
    )(q5, kc, vct, bias_tab, pool)


def _sel_win_kernel(q_ref, ks_ref, vst_ref, kw_ref, vwt_ref, sel_ref, tzs_ref, tzw_ref, os_ref, ow_ref, *, tq):
    tk = ATT_TK
    qt = pl.program_id(2)
    q = q_ref[0, 0].reshape(GQA * tq, HEAD_DIM)
    width = GQA * tq
    per_tile = tk // SEL_BLOCK

    def make_sweep(k_ref, vt_ref, tz_ref, use_sel, n_chains, single_trip):
        n_d = tz_ref.shape[2] - 1

        def scores(kt, hi):
            pad = kt > hi
            kt = jnp.minimum(kt, hi)
            off = pl.multiple_of(kt * tk, tk)
            k = k_ref[0, 0, pl.ds(off, tk), :]
            d = jnp.where(pad, n_d, jnp.minimum(qt - kt, n_d - 1))
            bias = [tz_ref[0, g, d] for g in range(GQA)]
            if use_sel:
                rows = sel_ref[0, 0, pl.ds(kt * per_tile, per_tile), :]
                selb = jnp.concatenate([jnp.broadcast_to(rows[i:i + 1], (SEL_BLOCK, tq))
                                        for i in range(per_tile)], axis=0)
                bias = [b + selb for b in bias]
            return _nt_dot(k, q) + jnp.concatenate(bias, axis=1)

        def values_t(kt, lo, hi):
            off = pl.multiple_of(jnp.clip(kt, lo, hi) * tk, tk)
            return vt_ref[0, 0, :, pl.ds(off, tk)]

        def sweep(lo, hi):
            n_trips = (hi - lo + n_chains) // n_chains
            chain0 = (jnp.full((1, width), 0.5 * NEG, F32), jnp.zeros((1, width), F32),
                      jnp.zeros((HEAD_DIM, width), F32), jnp.ones((1, width), F32), jnp.zeros((tk, width), BF16))

            def trip(i, chains):
                kt = lo + n_chains * i
                pv = [jnp.dot(values_t(kt - n_chains + c, lo, hi), chains[c][4], preferred_element_type=F32)
                      for c in range(n_chains)]
                ss = [scores(kt + c, hi) for c in range(n_chains)]
                out = []
                for c in range(n_chains):
                    m, l, acc, alpha_prev, _ = chains[c]
                    m_new = jnp.maximum(m, jnp.max(ss[c], axis=0, keepdims=True))
                    alpha = jnp.exp(m - m_new)
                    p = jnp.exp(ss[c] - m_new)
                    l = alpha * l + jnp.sum(p, axis=0, keepdims=True)
                    out.append((m_new, l, alpha_prev * acc + pv[c], alpha, p.astype(BF16)))
                return tuple(out)

            if single_trip:
                done = []
                for c in range(n_chains):
                    s = scores(lo + c, hi)
                    m = jnp.maximum(jnp.max(s, axis=0, keepdims=True), 0.5 * NEG)
                    p = jnp.exp(s - m)
                    done.append((m, jnp.sum(p, axis=0, keepdims=True),
                                 jnp.dot(values_t(lo + c, lo, hi), p.astype(BF16), preferred_element_type=F32)))
            else:
                chains = lax.fori_loop(0, n_trips, trip, (chain0,) * n_chains)
                kt_last = lo + n_chains * (n_trips - 1)
                done = []
                for c in range(n_chains):
                    m, l, acc, alpha, p = chains[c]
                    done.append((m, l, alpha * acc + jnp.dot(values_t(kt_last + c, lo, hi), p,
                                                              preferred_element_type=F32)))
            m_all = functools.reduce(jnp.maximum, [m for m, _, _ in done])
            num = den = 0.0
            for m, l, acc in done:
                e = jnp.exp(m - m_all)
                num = num + acc * e
                den = den + l * e
            o = num / jnp.maximum(den, 1e-30)
            return jnp.concatenate([o[:, g * tq:(g + 1) * tq].T for g in range(GQA)], axis=-1)
        return sweep

    n_win = tzw_ref.shape[2] - 1
    os_ref[0] = make_sweep(ks_ref, vst_ref, tzs_ref, True, SEL_CHAINS, False)(0, qt)
    ow_ref[0] = make_sweep(kw_ref, vwt_ref, tzw_ref, False, n_win, True)(jnp.maximum(qt - (n_win - 1), 0), qt)


def _sel_win_prompt(q5, ks, vst, kw, vwt, sel, tzs, tzw):
    B, _, _, T, _ = q5.shape
    NB = sel.shape[2]
    tq = ATT_TQ
    k_spec = pl.BlockSpec((1, 1, T, HEAD_DIM), lambda b, h, i: (b, h, 0, 0))
    vt_spec = pl.BlockSpec((1, 1, HEAD_DIM, T), lambda b, h, i: (b, h, 0, 0))
    tz_spec = lambda tz: pl.BlockSpec((1,) + tz.shape[1:], lambda b, h, i: (h, 0, 0, 0, 0))
    o_spec = pl.BlockSpec((1, tq, GQA * HEAD_DIM), lambda b, h, i: (b, i, h))
    return pl.pallas_call(
        functools.partial(_sel_win_kernel, tq=tq),
        out_shape=(jax.ShapeDtypeStruct((B, T, D_ATT), F32), jax.ShapeDtypeStruct((B, T, D_ATT), F32)),
        grid=(B, N_KV_HEADS, T // tq),
        in_specs=[pl.BlockSpec((1, 1, GQA, tq, HEAD_DIM), lambda b, h, i: (b, h, 0, i, 0)),
                  k_spec, vt_spec, k_spec, vt_spec,
                  pl.BlockSpec((1, 1, NB, tq), lambda b, h, i: (b, h, 0, i)),
                  tz_spec(tzs), tz_spec(tzw)],
        out_specs=(o_spec, o_spec),
        compiler_params=_cparams("parallel", "parallel", "parallel"),
        name="sel_win_prompt",
    )(q5, ks, vst, kw, vwt, sel, tzs, tzw)


def _gate_expand_matrix():
    m = np.zeros((3, 2 * LANE, D_ATT), np.float32)
    for r in range(3):
        for h in range(N_HEADS):
            m[r, h * 3 + r, h * HEAD_DIM:(h + 1) * HEAD_DIM] = 1.0
            m[r, LANE + h * 3 + r, h * HEAD_DIM:(h + 1) * HEAD_DIM] = 1.0
    return m


def _split_bf16(x):
    hi = x.astype(BF16)
    return hi, (x - hi.astype(F32)).astype(BF16)


def _post_mixer_kernel(y_ref, u_ref, oc_ref, os_ref, ow_ref, g_ref, x_ref, gate_ref, sh_ref, sc_ref,
                       dskip_ref, wglu_ref, bglu_ref, gexp_ref, wout_ref, lng_ref, lnb_ref,
                       wr_ref, br_ref, x1_ref, hm_ref, te_ref, tw_ref):
    y = y_ref[0] + dskip_ref[...] * u_ref[0]
    gl = jax.nn.gelu(y)
    ssm = gl * jax.nn.sigmoid(jnp.dot(gl.astype(BF16), wglu_ref[...], preferred_element_type=F32)
                              + bglu_ref[...])
    sg = jnp.concatenate(_split_bf16(jax.nn.sigmoid(g_ref[0])), axis=1)
    att = jnp.zeros_like(oc_ref[0])
    for r, o_ref in enumerate((oc_ref, os_ref, ow_ref)):
        att = att + jnp.dot(sg, gexp_ref[r], preferred_element_type=F32) * o_ref[0]
    h = (jnp.dot(ssm.astype(BF16), wout_ref[:D_SSM, :], preferred_element_type=F32)
         + jnp.dot(att.astype(BF16), wout_ref[D_SSM:, :], preferred_element_type=F32))
    z = DN_ALPHA * x_ref[0] + gate_ref[0] * h
    x1 = _layer_norm(z) * lng_ref[...] + lnb_ref[...]
    x1_ref[0] = x1
    hm = _layer_norm(x1) * (1.0 + sc_ref[0]) + sh_ref[0]
    hm_ref[0] = hm
    hm_hi, hm_lo = _split_bf16(hm)
    logits = (jnp.dot(hm_hi, wr_ref[0], preferred_element_type=F32)
              + jnp.dot(hm_lo, wr_ref[0], preferred_element_type=F32)
              + jnp.dot(hm_hi, wr_ref[1], preferred_element_type=F32)) + br_ref[...]
    lane = lax.broadcasted_iota(jnp.int32, logits.shape, 1)
    work = jnp.where(lane < N_EXPERTS, logits, -jnp.inf)
    te = jnp.zeros(logits.shape, jnp.int32)
    tv = jnp.zeros(logits.shape, F32)
    for k in range(TOP_K):
        best = jnp.max(work, axis=-1, keepdims=True)
        arg = jnp.min(jnp.where(work == best, lane, LANE), axis=-1, keepdims=True)
        te = jnp.where(lane == k, arg, te)
        tv = jnp.where(lane == k, best, tv)
        work = jnp.where(lane == arg, -jnp.inf, work)
    ex = jnp.where(lane < TOP_K, jnp.exp(tv - tv[:, 0:1]), 0.0)
    te_ref[0] = te
    tw_ref[0] = ex / jnp.sum(ex, axis=-1, keepdims=True)


def _post_mixer(y, u, oc, osel, ow, g, x, gate, shift, scale, w, tm):
    B, T, D = x.shape
    R = gate.shape[1]
    rb = 1 if R == 1 else tm
    mod_map = (lambda b, i: (b, 0, 0)) if R == 1 else (lambda b, i: (b, i, 0))
    row = lambda n: pl.BlockSpec((1, tm, n), lambda b, i: (b, i, 0))
    mod = pl.BlockSpec((1, rb, D), mod_map)
    full = lambda a: pl.BlockSpec(a.shape, lambda b, i: (0,) * a.ndim)
    consts = (w['d_skip'], w['w_glu'], w['b_glu'], w['gexp'], w['w_out'], w['ln1_g'], w['ln1_b'],
              w['w_router'], w['b_router'])
    return pl.pallas_call(
        _post_mixer_kernel,
        out_shape=(jax.ShapeDtypeStruct((B, T, D), F32), jax.ShapeDtypeStruct((B, T, D), F32),
                   jax.ShapeDtypeStruct((B, T, LANE), jnp.int32), jax.ShapeDtypeStruct((B, T, LANE), F32)),
        grid=(B, T // tm),
        in_specs=[row(D_SSM), row(D_SSM), row(D_ATT), row(D_ATT), row(D_ATT), row(LANE), row(D),
                  mod, mod, mod] + [full(a) for a in consts],
        out_specs=(row(D), row(D), row(LANE), row(LANE)),
        compiler_params=_cparams("parallel", "parallel"),
        name="post_mixer",
    )(y, u, oc, osel, ow, g, x, gate, shift, scale, *consts)


def _expert_kernel(e_ref, blk_ref, lo_ref, hi_ref, first_ref, x_ref, wgu_ref, bgu_ref, wd_ref, bd_ref, o_ref,
                   wgu_s, wd_s):
    i = pl.program_id(0)
    fresh = (i == 0) | (e_ref[i] != e_ref[jnp.maximum(i - 1, 0)])

    @pl.when(fresh)
    def _():
        wgu_s[...] = wgu_ref[0].astype(BF16)
        wd_s[...] = wd_ref[0].astype(BF16)

    @pl.when(first_ref[i] == 1)
    def _():
        o_ref[...] = jnp.zeros_like(o_ref)

    @pl.when(hi_ref[i] > lo_ref[i])
    def _():
        gu = jnp.dot(x_ref[...].astype(BF16), wgu_s[...], preferred_element_type=F32) + bgu_ref[0]
        gate = jnp.minimum(gu[:, :D_FF], SWIGLU_LIMIT)
        up = jnp.clip(gu[:, D_FF:], -SWIGLU_LIMIT, SWIGLU_LIMIT)
        hh = (up + 1.0) * gate * jax.nn.sigmoid(SWIGLU_ALPHA * gate)
        y = jnp.dot(hh.astype(BF16), wd_s[...], preferred_element_type=F32) + bd_ref[0]
        row = blk_ref[i] * MOE_ROWS + lax.broadcasted_iota(jnp.int32, (MOE_ROWS, 1), 0)
        o_ref[...] = jnp.where((row >= lo_ref[i]) & (row < hi_ref[i]), y, o_ref[...])


def _experts(xb, items, w_gate_up, b_gate_up, w_down, b_down):
    rows, D = xb.shape
    n_items = items[0].shape[0]
    wmap = lambda i, e, blk, lo, hi, first: (e[i], 0, 0)
    rmap = lambda i, e, blk, lo, hi, first: (blk[i], 0)
    grid_spec = pltpu.PrefetchScalarGridSpec(
        num_scalar_prefetch=5,
        grid=(n_items,),
        in_specs=[pl.BlockSpec((MOE_ROWS, D), rmap),
                  pl.BlockSpec((1, D, 2 * D_FF), wmap),
                  pl.BlockSpec((1, 1, 2 * D_FF), wmap),
                  pl.BlockSpec((1, D_FF, D), wmap),
                  pl.BlockSpec((1, 1, D), wmap)],
        out_specs=pl.BlockSpec((MOE_ROWS, D), rmap),
        scratch_shapes=[pltpu.VMEM((D, 2 * D_FF), BF16), pltpu.VMEM((D_FF, D), BF16)],
    )
    return pl.pallas_call(
        _expert_kernel,
        out_shape=jax.ShapeDtypeStruct((rows, D), F32),
        grid_spec=grid_spec,
        compiler_params=_cparams("arbitrary"),
        name="moe_experts",
    )(*items, xb, w_gate_up, b_gate_up.reshape(N_EXPERTS, 1, 2 * D_FF), w_down,
      b_down.reshape(N_EXPERTS, 1, D))


def _moe_dispatch(top_e, n):
    blk = MOE_ROWS
    nk = n * TOP_K
    cb = 128
    assert nk % cb == 0
    e = top_e.reshape(-1)
    onehot = (e[:, None] == jnp.arange(N_EXPERTS)[None, :]).astype(F32)
    oh3 = onehot.reshape(nk // cb, cb, N_EXPERTS)
    tri = jnp.asarray(np.tril(np.ones((cb, cb), np.float32), -1), dtype=BF16)
    within = jnp.einsum('ij,bje->bie', tri, oh3.astype(BF16), preferred_element_type=F32)
    blk_tot = jnp.sum(oh3, axis=1)
    blk_off = jnp.cumsum(blk_tot, axis=0) - blk_tot
    counts = jnp.sum(blk_tot, axis=0)
    start = jnp.cumsum(counts) - counts
    dest = jnp.sum((within + blk_off[:, None, :] + start[None, None, :]) * oh3, axis=-1)
    dest = dest.reshape(nk).astype(jnp.int32)
    order = jnp.argsort(dest)
    n_blk = -(-nk // blk)
    row_tok = jnp.concatenate([(order // TOP_K).astype(jnp.int32), jnp.full((n_blk * blk - nk,), n, jnp.int32)])
    counts_i, start_i = counts.astype(jnp.int32), start.astype(jnp.int32)
    first_b = start_i // blk
    last_b = (start_i + counts_i - 1) // blk
    n_it = jnp.where(counts_i > 0, last_b - first_b + 1, 0)
    it_end = jnp.cumsum(n_it)
    it_start = it_end - n_it
    n_items = n_blk + N_EXPERTS - 1
    i = jnp.arange(n_items)
    live = i < it_end[-1]
    it_e = jnp.minimum(jnp.sum(it_end[None, :] <= i[:, None], axis=1), N_EXPERTS - 1)
    it_blk = jnp.where(live, first_b[it_e] + i - it_start[it_e], n_blk - 1)
    it_lo = jnp.where(live, start_i[it_e], 0)
    it_hi = jnp.where(live, start_i[it_e] + counts_i[it_e], 0)
    it_first = jnp.concatenate([jnp.ones((1,), jnp.int32), (it_blk[1:] != it_blk[:-1]).astype(jnp.int32)])
    items = tuple(a.astype(jnp.int32) for a in (it_e, it_blk, it_lo, it_hi, it_first))
    return row_tok, dest.reshape(n, TOP_K), items


def _final_kernel(x_ref, y0_ref, y1_ref, y2_ref, y3_ref, tw_ref, gate_ref, lng_ref, lnb_ref, o_ref):
    tw = tw_ref[0]
    y = jnp.zeros_like(x_ref[0])
    for k, y_ref in enumerate((y0_ref, y1_ref, y2_ref, y3_ref)):
        y = y + tw[:, k:k + 1] * y_ref[0]
    z = DN_ALPHA * x_ref[0] + gate_ref[0] * y
    o_ref[0] = _layer_norm(z) * lng_ref[...] + lnb_ref[...]


def _final(x1, ys, tw, gate, ln_g, ln_b, tm):
    B, T, D = x1.shape
    R = gate.shape[1]
    rb = 1 if R == 1 else tm
    mod_map = (lambda b, i: (b, 0, 0)) if R == 1 else (lambda b, i: (b, i, 0))
    row = lambda n: pl.BlockSpec((1, tm, n), lambda b, i: (b, i, 0))
    vec = pl.BlockSpec((1, D), lambda b, i: (0, 0))
    return pl.pallas_call(
        _final_kernel,
        out_shape=jax.ShapeDtypeStruct((B, T, D), F32),
        grid=(B, T // tm),
        in_specs=[row(D), row(D), row(D), row(D), row(D), row(LANE),
                  pl.BlockSpec((1, rb, D), mod_map), vec, vec],
        out_specs=row(D),
        compiler_params=_cparams("parallel", "parallel"),
        name="moe_combine_ln",
    )(x1, *ys, tw, gate, ln_g, ln_b)


def _cmp_select_step_kernel(q_ref, kv_ref, bias_ref, pool_ref, o_ref, idx_ref, *, n_cmp, n_blk, q_pos):
    q = q_ref[0].astype(BF16)
    ncp = kv_ref.shape[1]
    nbp = pool_ref.shape[1]
    hd = HEAD_DIM
    kv = kv_ref[0]
    kb = [kv[:, h * hd:(h + 1) * hd].astype(BF16) for h in range(N_KV_HEADS)]
    vb = [kv[:, (N_KV_HEADS + h) * hd:(N_KV_HEADS + h + 1) * hd].astype(BF16) for h in range(N_KV_HEADS)]
    row = lax.broadcasted_iota(jnp.int32, (N_HEADS, 1), 0)
    first = row < GQA
    s = jnp.where(first, _nt_dot(q, kb[0]), _nt_dot(q, kb[1])) * (hd ** -0.5)
    s = s + bias_ref[...]
    ci = lax.broadcasted_iota(jnp.int32, (N_HEADS, ncp), 1)
    mask = (ci * CMP_STRIDE + CMP_BLOCK - 1 <= q_pos) & (ci < n_cmp)
    s = jnp.where(mask, s, NEG)
    m = jnp.max(s, axis=-1, keepdims=True)
    p = jnp.where(mask, jnp.exp(s - m), 0.0)
    p = p / jnp.maximum(jnp.sum(p, axis=-1, keepdims=True), 1e-30)
    pb = p.astype(BF16)
    o_ref[0] = jnp.where(first, jnp.dot(pb, vb[0], preferred_element_type=F32),
                         jnp.dot(pb, vb[1], preferred_element_type=F32))
    imp0 = jnp.sum(jnp.where(first, p, 0.0), axis=0, keepdims=True)
    imp1 = jnp.sum(jnp.where(first, 0.0, p), axis=0, keepdims=True)
    imp = jnp.where(first, imp0, imp1)
    sb = jnp.dot(imp, pool_ref[...], precision=HIGHEST, preferred_element_type=F32)
    cur = q_pos // SEL_BLOCK
    bi = lax.broadcasted_iota(jnp.int32, (nbp, nbp), 0)
    bj = lax.broadcasted_iota(jnp.int32, (nbp, nbp), 1)
    blk = lax.broadcasted_iota(jnp.int32, (1, nbp), 1)
    causal = blk <= cur
    forced = (blk == 0) | (blk == cur) | (blk == cur - 1)
    rsel = lax.broadcasted_iota(jnp.int32, (N_SEL, nbp), 0)
    for h in range(N_KV_HEADS):
        sc = jnp.where(forced & causal, 1e4, jnp.where(causal, sb[h * GQA:h * GQA + 1, :], -1.0))
        sc = jnp.where(blk < n_blk, sc, -2.0)
        scb = jnp.broadcast_to(sc, (nbp, nbp))
        col = jnp.sum(jnp.where(bi == bj, scb, 0.0), axis=1, keepdims=True)
        ahead = (col > scb) | ((col == scb) & (bi < bj))
        rank = jnp.sum(ahead.astype(jnp.int32), axis=0, keepdims=True)
        hit = jnp.broadcast_to(rank, (N_SEL, nbp)) == rsel
        idx = jnp.sum(jnp.where(hit, jnp.broadcast_to(blk, (N_SEL, nbp)), 0), axis=1, keepdims=True)
        idx_ref[0, h] = jnp.broadcast_to(idx, (N_SEL, LANE))


def _cmp_select_step(q, ckv, bias, pool, n_cmp, n_blk, q_pos):
    B = q.shape[0]
    NCp = ckv.shape[1]
    return pl.pallas_call(
        functools.partial(_cmp_select_step_kernel, n_cmp=n_cmp, n_blk=n_blk, q_pos=q_pos),
        out_shape=(jax.ShapeDtypeStruct((B, N_HEADS, HEAD_DIM), F32),
                   jax.ShapeDtypeStruct((B, N_KV_HEADS, N_SEL, LANE), jnp.int32)),
        grid=(B,),
        in_specs=[pl.BlockSpec((1, N_HEADS, HEAD_DIM), lambda b: (b, 0, 0)),
                  pl.BlockSpec((1, NCp, D_KV), lambda b: (b, 0, 0)),
                  pl.BlockSpec(bias.shape, lambda b: (0, 0)),
                  pl.BlockSpec(pool.shape, lambda b: (0, 0))],
        out_specs=(pl.BlockSpec((1, N_HEADS, HEAD_DIM), lambda b: (b, 0, 0)),
                   pl.BlockSpec((1, N_KV_HEADS, N_SEL, LANE), lambda b: (b, 0, 0, 0))),
        compiler_params=_cparams("parallel"),
        name="cmp_select_step",
    )(q, ckv, bias, pool)


def _sel_step_kernel(pg_ref, idx_ref, q_ref, *refs, n_past, q_pos):
    page_refs = refs[:N_SEL]
    new_ref, bias_ref, kpos_ref, o_ref = refs[N_SEL:]
    b, h = pl.program_id(0), pl.program_id(1)
    base = (b * N_KV_HEADS + h) * N_SEL
    kts, vts = [], []
    for j in range(N_SEL):
        is_new = idx_ref[base + j] >= n_past
        kts.append(jnp.where(is_new, new_ref[0, 0, 0], page_refs[j][0, 0, 0]))
        vts.append(jnp.where(is_new, new_ref[0, 1, 0], page_refs[j][0, 1, 0]))
    kt = jnp.concatenate(kts, axis=1).astype(BF16)
    vt = jnp.concatenate(vts, axis=1).astype(BF16)
    s = jnp.dot(q_ref[0].astype(BF16), kt, preferred_element_type=F32) * (HEAD_DIM ** -0.5) + bias_ref[0, 0]
    mask = kpos_ref[0, 0] <= q_pos
    s = jnp.where(mask, s, NEG)
    m = jnp.max(s, axis=-1, keepdims=True)
    p = jnp.where(mask, jnp.exp(s - m), 0.0)
    l = jnp.sum(p, axis=-1, keepdims=True)
    o_ref[0, 0] = _nt_dot(p.astype(BF16), vt) / jnp.maximum(l, 1e-30)


def _sel_step(q, pool_t, new_t, bias_sel, kpos, pages, idx_flat, n_past, q_pos):
    B = q.shape[0]
    nk = N_SEL * PAGE_SIZE
    slot = lambda b, h, j: (b * N_KV_HEADS + h) * N_SEL + j
    page_spec = lambda j: pl.BlockSpec((1, 2, 1, HEAD_DIM, PAGE_SIZE),
                                       lambda b, h, pg, ix, j=j: (pg[slot(b, h, j)], 0, h, 0, 0))
    grid_spec = pltpu.PrefetchScalarGridSpec(
        num_scalar_prefetch=2,
        grid=(B, N_KV_HEADS),
        in_specs=[pl.BlockSpec((1, N_HEADS, HEAD_DIM), lambda b, h, pg, ix: (b, 0, 0))]
        + [page_spec(j) for j in range(N_SEL)]
        + [pl.BlockSpec((1, 2, 1, HEAD_DIM, PAGE_SIZE), lambda b, h, pg, ix: (b, 0, h, 0, 0)),
           pl.BlockSpec((1, 1, N_HEADS, nk), lambda b, h, pg, ix: (b, h, 0, 0)),
           pl.BlockSpec((1, 1, 1, nk), lambda b, h, pg, ix: (b, h, 0, 0))],
        out_specs=pl.BlockSpec((1, 1, N_HEADS, HEAD_DIM), lambda b, h, pg, ix: (b, h, 0, 0)),
    )
    return pl.pallas_call(
        functools.partial(_sel_step_kernel, n_past=n_past, q_pos=q_pos),
        out_shape=jax.ShapeDtypeStruct((B, N_KV_HEADS, N_HEADS, HEAD_DIM), F32),
        grid_spec=grid_spec,
        compiler_params=_cparams("arbitrary", "arbitrary"),
        name="sel_step",
    )(pages, idx_flat, q, *([pool_t] * N_SEL), new_t, bias_sel, kpos)


def _win_step_kernel(q_ref, w_ref, new_ref, bias_ref, bias0_ref, o_ref):
    q = q_ref[0]
    qb = q.astype(BF16)
    row = lax.broadcasted_iota(jnp.int32, (N_HEADS, 1), 0)
    first = row < GQA
    w = w_ref[0]
    hd = HEAD_DIM
    kb = [w[:, h * hd:(h + 1) * hd].astype(BF16) for h in range(N_KV_HEADS)]
    vb = [w[:, (N_KV_HEADS + h) * hd:(N_KV_HEADS + h + 1) * hd].astype(BF16) for h in range(N_KV_HEADS)]
    s = jnp.where(first, _nt_dot(qb, kb[0]), _nt_dot(qb, kb[1])) * (hd ** -0.5) + bias_ref[...]
    new = new_ref[0]
    kn = jnp.where(first, new[:, 0:hd], new[:, hd:2 * hd])
    vn = jnp.where(first, new[:, 2 * hd:3 * hd], new[:, 3 * hd:])
    sn = jnp.sum(q * kn, axis=-1, keepdims=True) * (hd ** -0.5) + bias0_ref[...]
    m = jnp.maximum(jnp.max(s, axis=-1, keepdims=True), sn)
    p = jnp.exp(s - m)
    pn = jnp.exp(sn - m)
    l = jnp.sum(p, axis=-1, keepdims=True) + pn
    pb = p.astype(BF16)
    acc = jnp.where(first, jnp.dot(pb, vb[0], preferred_element_type=F32),
                    jnp.dot(pb, vb[1], preferred_element_type=F32)) + pn * vn
    o_ref[0] = acc / jnp.maximum(l, 1e-30)


def _win_step(q, win, new, bias, bias0):
    B, W, _ = win.shape
    return pl.pallas_call(
        _win_step_kernel,
        out_shape=jax.ShapeDtypeStruct((B, N_HEADS, HEAD_DIM), F32),
        grid=(B,),
        in_specs=[pl.BlockSpec((1, N_HEADS, HEAD_DIM), lambda b: (b, 0, 0)),
                  pl.BlockSpec((1, W, D_KV), lambda b: (b, 0, 0)),
                  pl.BlockSpec((1, 1, D_KV), lambda b: (b, 0, 0)),
                  pl.BlockSpec((N_HEADS, W), lambda b: (0, 0)),
                  pl.BlockSpec((N_HEADS, 1), lambda b: (0, 0))],
        out_specs=pl.BlockSpec((1, N_HEADS, HEAD_DIM), lambda b: (b, 0, 0)),
        compiler_params=_cparams("parallel"),
        name="win_step",
    )(q, win, new, bias, bias0)


def _split_heads(kv, dtype):
    B, L, _ = kv.shape
    kv5 = kv.reshape(B, L, 2, N_KV_HEADS, HEAD_DIM)
    return (jnp.transpose(kv5[:, :, 0], (0, 2, 1, 3)).astype(dtype),
            jnp.transpose(kv5[:, :, 1], (0, 2, 1, 3)).astype(dtype))


def _nsa_prompt(q5, kvc, ks, vst, kw, vwt, cmp_tab, rel_bias):
    B, T, _ = kvc.shape
    nc = T // CMP_STRIDE
    nb = T // SEL_BLOCK
    ckv = _compress_out([_compress_in(kvc.reshape(B, nc, CMP_STRIDE * D_KV), cmp_tab)], cmp_tab, nc)
    kc, vc = _split_heads(ckv, BF16)
    vct = jnp.transpose(vc, (0, 1, 3, 2))
    bias_n = _bias_by_distance(rel_bias, T)
    n_qt, n_kt = T // ATT_TQ, T // ATT_TK
    n_ds = min(n_kt, -(-(REL_MAX_DIST + ATT_TK - 1) // ATT_TK) + 1)
    n_dw = min(n_kt, WINDOW // ATT_TK + 1)
    tzs, tzw, bias_tab = _bias_tables(bias_n, n_qt, nc // 8, n_ds, n_dw, ATT_TQ, ATT_TK)
    pool = jnp.asarray(_pool_matrix(nc, nb))
    o_cmp, sel = _cmp_select_prompt(q5, kc, vct, bias_tab, pool, nc - 1)
    o_sel, o_win = _sel_win_prompt(q5, ks, vst, kw, vwt, sel, tzs, tzw)
    return o_cmp, o_sel, o_win


def _nsa_sample(q, kvc, kvs, kvw, pool_cmp, pool_sel, win_buf, page_table, cmp_tab, rel_bias):
    B = q.shape[0]
    n_pages = page_table.shape[1]
    past_len = n_pages * PAGE_SIZE
    q_pos = past_len
    lp = -(-(past_len + 1) // SEL_BLOCK) * SEL_BLOCK
    n_cmp = lp // CMP_STRIDE - 1
    n_blk = lp // SEL_BLOCK
    n_past_chunks = past_len // CMP_STRIDE
    n_tail = 8
    assert n_past_chunks + n_tail >= n_cmp + 1
    n_chunks = n_past_chunks + n_tail
    feature_major = lambda pool: jnp.transpose(pool, (0, 2, 3, 4, 1))
    z_past = _compress_in_paged(feature_major(pool_cmp), page_table, cmp_tab)
    tail = jnp.pad(kvc[:, None, :], ((0, 0), (0, n_tail * CMP_STRIDE - 1), (0, 0)))
    z_tail = _compress_in(tail.reshape(B, n_tail, CMP_STRIDE * D_KV), cmp_tab)
    ncp = -(-n_chunks // LANE) * LANE
    nbp = -(-n_blk // LANE) * LANE
    ckv = _compress_out([z_past, z_tail], cmp_tab, ncp)
    bias_n = _bias_by_distance(rel_bias, q_pos + 1)
    n_back = max((n_pages + 1) * PAGE_SIZE, ncp * CMP_STRIDE + CMP_BLOCK)
    back = jnp.concatenate([bias_n[:, ::-1], jnp.broadcast_to(bias_n[:, :1], (N_HEADS, n_back - q_pos - 1))], 1)
    bias_c = back[:, CMP_BLOCK - 1:CMP_BLOCK - 1 + ncp * CMP_STRIDE:CMP_STRIDE]
    pool = jnp.asarray(_pool_matrix(ncp, nbp).T)
    q3 = q.reshape(B, N_HEADS, HEAD_DIM)
    o_cmp, idx = _cmp_select_step(q3, ckv, bias_c, pool, n_cmp, n_blk, q_pos)
    idx = idx[..., 0]
    bpp = PAGE_SIZE // SEL_BLOCK
    n_past = n_pages * bpp
    lpage = idx // bpp
    pages = jnp.take_along_axis(page_table, jnp.minimum(lpage, n_pages - 1).reshape(B, -1), axis=1)
    new_t = jnp.pad(kvs.reshape(B, 2, N_KV_HEADS, HEAD_DIM, 1), ((0, 0),) * 4 + ((0, PAGE_SIZE - 1),))
    bias_page = jnp.transpose(back[:, :(n_pages + 1) * PAGE_SIZE].reshape(N_HEADS, n_pages + 1, PAGE_SIZE),
                              (1, 0, 2))
    bias_sel = jnp.transpose(bias_page[lpage], (0, 1, 3, 2, 4)).reshape(B, N_KV_HEADS, N_HEADS, -1)
    kpos = lpage[..., None] * PAGE_SIZE + jnp.arange(PAGE_SIZE)
    ok = (kpos // SEL_BLOCK == idx[..., None]) & (idx <= q_pos // SEL_BLOCK)[..., None]
    kpos = jnp.where(ok, kpos, q_pos + 1).reshape(B, N_KV_HEADS, 1, -1).astype(jnp.int32)
    o_sel = _sel_step(q3, feature_major(pool_sel), new_t, bias_sel, kpos, pages.reshape(-1).astype(jnp.int32),
                      idx.reshape(-1).astype(jnp.int32), n_past, q_pos)
    o_sel = jnp.concatenate([o_sel[:, h, h * GQA:(h + 1) * GQA] for h in range(N_KV_HEADS)], axis=1)
    wb = win_buf.shape[1]
    bias_w = bias_n[:, 1:wb + 1][:, ::-1]
    o_win = _win_step(q3, win_buf.reshape(B, wb, D_KV), kvw[:, None, :], bias_w, bias_n[:, 0:1])
    return o_cmp.reshape(B, D_ATT), o_sel.reshape(B, D_ATT), o_win.reshape(B, D_ATT)


def kernel(x_prompt, x_sample, cache_cmp_kv, cache_sel_kv, state_win_kv, state_ssm_re, state_ssm_im, page_table,
           c_prompt, c_sample, w_ada, b_ada, w_in, lam_re, lam_im, log_dt, b_re, b_im, c_re, c_im, d_skip,
           w_glu, b_glu, phi_pe, phi_w1, phi_b1, phi_w2, phi_b2, rel_bias, w_out, ln1_g, ln1_b,
           w_router, b_router, w_gate_up, b_gate_up, w_down, b_down, ln2_g, ln2_b):
    assert w_ada.shape[0] == DEPTH == 1
    l = 0
    Bp, T, D = x_prompt.shape
    Bs = x_sample.shape[0]
    kv_tail = (2, N_KV_HEADS, HEAD_DIM)

    n_c = Bp + Bs
    c_all = jnp.pad(jnp.concatenate([c_prompt, c_sample], 0), ((0, -n_c % 8), (0, 0)))
    m_all = _adaln(c_all, w_ada[l], b_ada[l])
    m_p = m_all[:Bp].reshape(Bp, 6, D)
    m_s = m_all[Bp:n_c].reshape(Bs, 6, D)
    mod_p = [m_p[:, i:i + 1, :] for i in range(6)]
    mod_s = [m_s[None, :, i, :] for i in range(6)]

    w_in_pad = jnp.pad(w_in[l], ((0, 0), (0, D_IN_PAD - D_IN))).astype(BF16)
    n_levels = max(1, int(math.log2(T // SSM_CHUNK)))
    ssm_tab = _ssm_tables(lam_re[l], lam_im[l], log_dt[l], b_re[l], b_im[l], c_re[l], c_im[l],
                          SSM_CHUNK, n_levels)
    cmp_tab = _compress_tables(phi_pe[l], phi_w1[l], phi_b1[l], phi_w2[l], phi_b2[l])
    w_post = dict(
        d_skip=d_skip[l].reshape(1, D_SSM), w_glu=w_glu[l].astype(BF16), b_glu=b_glu[l].reshape(1, D_SSM),
        gexp=jnp.asarray(_gate_expand_matrix(), dtype=BF16), w_out=w_out[l].astype(BF16),
        ln1_g=ln1_g[l].reshape(1, D), ln1_b=ln1_b[l].reshape(1, D),
        w_router=jnp.stack(_split_bf16(jnp.pad(w_router[l], ((0, 0), (0, LANE - N_EXPERTS))))),
        b_router=jnp.pad(b_router[l], (0, LANE - N_EXPERTS)).reshape(1, LANE))

    u, q5, kvc, kvs, kvw, g, ks, vst, kw, vwt = _mixer_in(x_prompt, mod_p[0], mod_p[1], w_in_pad, 512, True)
    y_ssm, h_p = _ssm_prompt(u, ssm_tab)
    o_cmp, o_sel, o_win = _nsa_prompt(q5, kvc, ks, vst, kw, vwt, cmp_tab, rel_bias)
    x1_p, hm_p, te_p, tw_p = _post_mixer(y_ssm, u, o_cmp, o_sel, o_win, g, x_prompt,
                                         mod_p[2], mod_p[3], mod_p[4], w_post, tm=512)

    u_s, q_s, kvc_s, kvs_s, kvw_s, g_s = _mixer_in(x_sample.reshape(1, Bs, D), mod_s[0], mod_s[1],
                                                   w_in_pad, Bs, False)
    y_s, h_s = _ssm_sample(u_s[0], state_ssm_re[l], state_ssm_im[l], ssm_tab, c_re[l], c_im[l])
    oc_s, os_s, ow_s = _nsa_sample(q_s[0].astype(F32), kvc_s[0], kvs_s[0], kvw_s[0], cache_cmp_kv[l],
                                   cache_sel_kv[l], state_win_kv[l], page_table, cmp_tab, rel_bias)
    x1_s, hm_s, te_s, tw_s = _post_mixer(y_s[None], u_s, oc_s[None], os_s[None], ow_s[None], g_s,
                                         x_sample.reshape(1, Bs, D), mod_s[2], mod_s[3], mod_s[4],
                                         w_post, tm=Bs)

    n_p = Bp * T
    n_all = n_p + Bs
    hm_all = jnp.concatenate([hm_p.reshape(n_p, D), hm_s.reshape(Bs, D)], 0)
    te_all = jnp.concatenate([te_p.reshape(n_p, LANE), te_s.reshape(Bs, LANE)], 0)[:, :TOP_K]
    row_tok, dest, items = _moe_dispatch(te_all, n_all)
    xb = jnp.concatenate([hm_all, jnp.zeros((1, D), F32)], 0)[row_tok]
    yb = _experts(xb, items, w_gate_up[l], b_gate_up[l], w_down[l], b_down[l])
    ys_p = [yb[dest[:n_p, k]].reshape(Bp, T, D) for k in range(TOP_K)]
    ys_s = [yb[dest[n_p:, k]].reshape(1, Bs, D) for k in range(TOP_K)]
    ln2g, ln2b = ln2_g[l].reshape(1, D), ln2_b[l].reshape(1, D)
    out_p = _final(x1_p, ys_p, tw_p, mod_p[5], ln2g, ln2b, tm=512)
    out_s = _final(x1_s, ys_s, tw_s, mod_s[5], ln2g, ln2b, tm=Bs)

    wlen = min(WINDOW, T)
    win_s = jnp.concatenate([state_win_kv[l], kvw_s[0].reshape(Bs, 1, *kv_tail)], 1)[:, -state_win_kv.shape[2]:]
    p_state = SSM_STATE
    return (out_p, out_s.reshape(Bs, 1, D),
            kvc.reshape(1, Bp, T, *kv_tail), kvc_s[0].reshape(1, Bs, 1, *kv_tail),
            kvs.reshape(1, Bp, T, *kv_tail), kvs_s[0].reshape(1, Bs, 1, *kv_tail),
            kvw[:, T - wlen:].reshape(1, Bp, wlen, *kv_tail), win_s[None],
            h_p[None, ..., :p_state], h_p[None, ..., p_state:],
            h_s[None, ..., :p_state], h_s[None, ..., p_state:])
```

```python
import functools
import math

import numpy as np
import jax
import jax.numpy as jnp
from jax import lax
from jax.experimental import pallas as pl
from jax.experimental.pallas import tpu as pltpu

DEPTH = 1
PAGE_SIZE = 128
D_SSM = 512
SSM_GROUP = 16
N_SSM_GROUPS = D_SSM // SSM_GROUP
SSM_STATE = 64
N_HEADS = 8
HEAD_DIM = 64
N_KV_HEADS = 2
GQA = N_HEADS // N_KV_HEADS
D_ATT = N_HEADS * HEAD_DIM
D_KV = 2 * N_KV_HEADS * HEAD_DIM
CMP_STRIDE = 16
CMP_BLOCK = 2 * CMP_STRIDE
SEL_BLOCK = 64
N_SEL = 16
WINDOW = 512
NUM_BUCKETS = 32
REL_MAX_DIST = 1024
N_EXPERTS = 32
TOP_K = 4
D_FF = 1024
SWIGLU_LIMIT = 7.0
SWIGLU_ALPHA = 1.702
DN_ALPHA = (2 * DEPTH) ** 0.25
D_IN = D_SSM + D_ATT + 3 * D_KV + 3 * N_HEADS
NEG = -1e30
F32 = jnp.float32
BF16 = jnp.bfloat16
HIGHEST = lax.Precision.HIGHEST

LANE = 128
D_IN_PAD = -(-D_IN // LANE) * LANE
SSM_CHUNK = 8
ATT_TQ = 128
ATT_TK = 128
SEL_CHAINS = 4
MOE_ROWS = 256
PAGES_PER_STEP = 32
PAGE_PARTS = 2
CHUNK_PITCH = 24
VMEM_LIMIT = 48 * 1024 * 1024
LN_EPS = 1e-5


def _cparams(*sem):
    return pltpu.CompilerParams(dimension_semantics=sem, vmem_limit_bytes=VMEM_LIMIT)


def _nt_dot(a, b):
    return lax.dot_general(a, b, (((1,), (1,)), ((), ())), preferred_element_type=F32)


def _layer_norm(x):
    mu = jnp.mean(x, axis=-1, keepdims=True)
    xc = x - mu
    var = jnp.mean(xc * xc, axis=-1, keepdims=True)
    return xc * lax.rsqrt(var + LN_EPS)


def _adaln_kernel(c_ref, w_ref, b_ref, o_ref):
    c = c_ref[...]
    s = c * jax.nn.sigmoid(c)
    o_ref[...] = jnp.dot(s, w_ref[...], precision=HIGHEST, preferred_element_type=F32) + b_ref[...]


def _adaln(c, w, b):
    n, d = c.shape
    dout = w.shape[1]
    tn = 1024
    return pl.pallas_call(
        _adaln_kernel,
        out_shape=jax.ShapeDtypeStruct((n, dout), F32),
        grid=(dout // tn,),
        in_specs=[pl.BlockSpec((n, d), lambda j: (0, 0)),
                  pl.BlockSpec((d, tn), lambda j: (0, j)),
                  pl.BlockSpec((1, tn), lambda j: (0, j))],
        out_specs=pl.BlockSpec((n, tn), lambda j: (0, j)),
        compiler_params=_cparams("arbitrary"),
        name="adaln",
    )(c, w, b.reshape(1, dout))


def _mixer_in_kernel(x_ref, sh_ref, sc_ref, w_ref, u_ref, q_ref, kvc_ref, kvs_ref, kvw_ref, g_ref, *att_refs):
    h = _layer_norm(x_ref[0]) * (1.0 + sc_ref[0]) + sh_ref[0]
    z = jnp.dot(h.astype(BF16), w_ref[...], preferred_element_type=F32)
    c0 = D_SSM
    c1 = c0 + D_ATT
    c2 = c1 + D_KV
    c3 = c2 + D_KV
    c4 = c3 + D_KV
    u_ref[0] = z[:, :c0]
    kvc_ref[0] = z[:, c1:c2]
    kvs_ref[0] = z[:, c2:c3]
    kvw_ref[0] = z[:, c3:c4]
    g_ref[0] = z[:, c4:c4 + LANE]
    if not att_refs:
        q_ref[0] = z[:, c0:c1].astype(BF16)
        return
    ks_ref, vst_ref, kw_ref, vwt_ref = att_refs
    hd, half = HEAD_DIM, N_KV_HEADS * HEAD_DIM
    for hq in range(N_HEADS):
        q_ref[0, hq // GQA, hq % GQA] = (z[:, c0 + hq * hd:c0 + (hq + 1) * hd] * (hd ** -0.5)).astype(BF16)
    for k_ref, vt_ref, base in ((ks_ref, vst_ref, c2), (kw_ref, vwt_ref, c3)):
        for hk in range(N_KV_HEADS):
            k_ref[0, hk] = z[:, base + hk * hd:base + (hk + 1) * hd].astype(BF16)
        vt = z[:, base + half:base + 2 * half].T
        vt_ref[0] = vt.reshape(N_KV_HEADS, hd, vt.shape[1]).astype(BF16)


def _mixer_in(x, shift, scale, w_pad, tm, attention_layouts):
    B, T, D = x.shape
    R = shift.shape[1]
    rb = 1 if R == 1 else tm
    mod_map = (lambda b, i: (b, 0, 0)) if R == 1 else (lambda b, i: (b, i, 0))
    row = lambda n: pl.BlockSpec((1, tm, n), lambda b, i: (b, i, 0))
    f32 = lambda n: jax.ShapeDtypeStruct((B, T, n), F32)
    if attention_layouts:
        q_shape = jax.ShapeDtypeStruct((B, N_KV_HEADS, GQA, T, HEAD_DIM), BF16)
        q_spec = pl.BlockSpec((1, N_KV_HEADS, GQA, tm, HEAD_DIM), lambda b, i: (b, 0, 0, i, 0))
        k_shape = jax.ShapeDtypeStruct((B, N_KV_HEADS, T, HEAD_DIM), BF16)
        k_spec = pl.BlockSpec((1, N_KV_HEADS, tm, HEAD_DIM), lambda b, i: (b, 0, i, 0))
        vt_shape = jax.ShapeDtypeStruct((B, N_KV_HEADS, HEAD_DIM, T), BF16)
        vt_spec = pl.BlockSpec((1, N_KV_HEADS, HEAD_DIM, tm), lambda b, i: (b, 0, 0, i))
        extra_shapes, extra_specs = (k_shape, vt_shape, k_shape, vt_shape), (k_spec, vt_spec, k_spec, vt_spec)
    else:
        q_shape, q_spec = jax.ShapeDtypeStruct((B, T, D_ATT), BF16), row(D_ATT)
        extra_shapes, extra_specs = (), ()
    return pl.pallas_call(
        _mixer_in_kernel,
        out_shape=(f32(D_SSM), q_shape, f32(D_KV), f32(D_KV), f32(D_KV), f32(LANE)) + extra_shapes,
        grid=(B, T // tm),
        in_specs=[row(D), pl.BlockSpec((1, rb, D), mod_map), pl.BlockSpec((1, rb, D), mod_map),
                  pl.BlockSpec((D, D_IN_PAD), lambda b, i: (0, 0))],
        out_specs=(row(D_SSM), q_spec, row(D_KV), row(D_KV), row(D_KV), row(LANE)) + extra_specs,
        compiler_params=_cparams("parallel", "parallel"),
        name="mixer_in",
    )(x, shift, scale, w_pad)


def _ssm_tables(lam_re, lam_im, log_dt, b_re, b_im, c_re, c_im, L, n_levels):
    G, P = lam_re.shape
    C = b_re.shape[-1]
    dt = jnp.exp(log_dt.astype(F32))[:, None]
    er, ei = lam_re * dt, lam_im * dt

    def power(k):
        kk = k.astype(F32)[:, None, None]
        mag = jnp.exp(kk * er)
        return mag * jnp.cos(kk * ei), mag * jnp.sin(kk * ei)

    lb_re, lb_im = power(jnp.ones((1,), F32))
    nr, ni = lb_re[0] - 1.0, lb_im[0]
    den = lam_re * lam_re + lam_im * lam_im
    fr = (nr * lam_re + ni * lam_im) / den
    fi = (ni * lam_re - nr * lam_im) / den
    bbr = fr[:, :, None] * b_re - fi[:, :, None] * b_im
    bbi = fr[:, :, None] * b_im + fi[:, :, None] * b_re
    pr, pi = power(jnp.arange(L + 1))
    clr = c_re[None] * pr[:, :, None, :] - c_im[None] * pi[:, :, None, :]
    cli = c_re[None] * pi[:, :, None, :] + c_im[None] * pr[:, :, None, :]
    kern = (jnp.einsum('kgcp,gpd->kgcd', clr[:L], bbr, precision=HIGHEST)
            - jnp.einsum('kgcp,gpd->kgcd', cli[:L], bbi, precision=HIGHEST))
    GP = LANE // C
    X = G // GP
    eye = jnp.eye(GP, dtype=BF16)
    place_einsum = functools.partial(jnp.einsum, preferred_element_type=BF16)
    kblk = place_einsum('kxhcd,hj->xkhdjc', kern.astype(BF16).reshape(L, X, GP, C, C), eye)
    kblk = kblk.reshape(X, L, LANE, LANE)
    prr, pir = pr[:L][::-1], pi[:L][::-1]
    ws2 = jnp.stack([prr[..., None] * bbr[None] - pir[..., None] * bbi[None],
                     prr[..., None] * bbi[None] + pir[..., None] * bbr[None]])
    ws = place_einsum('rsxhpd,hj->xshdrjp', ws2.astype(BF16).reshape(2, L, X, GP, P, C), eye)
    ws = ws.reshape(X, L * LANE, 2 * GP * P)
    wy2 = jnp.stack([clr[1:], -cli[1:]])
    wy = place_einsum('rtxhcp,hj->xrhptjc', wy2.astype(BF16).reshape(2, L, X, GP, C, P), eye)
    wy = wy.reshape(X, 2 * GP * P, L * LANE)
    lr, li = power(L * (2 ** jnp.arange(n_levels)))
    lr, li = lr.reshape(n_levels, X, GP * P), li.reshape(n_levels, X, GP * P)
    ar = jnp.transpose(jnp.concatenate([lr, lr], -1), (1, 0, 2))
    ai = jnp.transpose(jnp.concatenate([-li, li], -1), (1, 0, 2))
    return kblk, ws, wy, ar, ai, (lb_re[0], lb_im[0], bbr, bbi)


def _ssm_kernel(u_ref, kblk_ref, ws_ref, wy_ref, ar_ref, ai_ref, y_ref, hl_ref, toep_ref, *, L, nc, n_levels):
    for s in range(L):
        for t in range(L):
            blk = kblk_ref[0, t - s] if t >= s else jnp.zeros((LANE, LANE), BF16)
            toep_ref[s * LANE:(s + 1) * LANE, t * LANE:(t + 1) * LANE] = blk
    u = jnp.concatenate([u_ref[0, pl.ds(t, nc, stride=L), :] for t in range(L)], axis=1).astype(BF16)
    y1 = jnp.dot(u, toep_ref[...], preferred_element_type=F32)
    h = jnp.dot(u, ws_ref[0], preferred_element_type=F32)
    w2 = h.shape[-1]
    rows = lax.broadcasted_iota(jnp.int32, (nc, w2), 0)
    for k in range(n_levels):
        d = 1 << k
        sh = jnp.where(rows >= d, pltpu.roll(h, d, axis=0), 0.0)
        sw = pltpu.roll(sh, w2 // 2, axis=1)
        h = h + ar_ref[0, k:k + 1, :] * sh + ai_ref[0, k:k + 1, :] * sw
    hl_ref[0, 0] = h[nc - 1:nc, :]
    hp = jnp.where(rows >= 1, pltpu.roll(h, 1, axis=0), 0.0)
    y = y1 + jnp.dot(hp.astype(BF16), wy_ref[0], preferred_element_type=F32)
    for t in range(L):
        y_ref[0, pl.ds(t, nc, stride=L), :] = y[:, t * LANE:(t + 1) * LANE]


def _ssm_prompt(u, tables):
    kblk, ws, wy, ar, ai, _ = tables
    B, T, _ = u.shape
    L, P = SSM_CHUNK, SSM_STATE
    X, n_levels, w2 = ar.shape
    GP = w2 // (2 * P)
    nc = T // L
    tab = lambda a: pl.BlockSpec((1,) + a.shape[1:], lambda x, b: (x,) + (0,) * (a.ndim - 1))
    seq = pl.BlockSpec((1, T, LANE), lambda x, b: (b, 0, x))
    y, hl = pl.pallas_call(
        functools.partial(_ssm_kernel, L=L, nc=nc, n_levels=n_levels),
        out_shape=(jax.ShapeDtypeStruct((B, T, D_SSM), F32), jax.ShapeDtypeStruct((X, B, 1, w2), F32)),
        grid=(X, B),
        in_specs=[seq, tab(kblk), tab(ws), tab(wy), tab(ar), tab(ai)],
        out_specs=(seq, pl.BlockSpec((1, 1, 1, w2), lambda x, b: (x, b, 0, 0))),
        scratch_shapes=[pltpu.VMEM((L * LANE, L * LANE), BF16)],
        compiler_params=_cparams("parallel", "parallel"),
        name="ssm_prompt",
    )(u, kblk, ws, wy, ar, ai)
    hl = jnp.transpose(hl.reshape(X, B, 2, GP, P), (1, 0, 3, 2, 4))
    return y, hl.reshape(B, X * GP, 2 * P)


def _ssm_step_kernel(u_ref, h0_ref, bb_ref, lr_ref, li_ref, cy_ref, y_ref, h_ref):
    p = lr_ref.shape[-1] // 2
    bu = jnp.einsum('gbc,gcp->gbp', u_ref[...], bb_ref[...], preferred_element_type=F32)
    h0 = h0_ref[...]
    h0s = jnp.concatenate([h0[..., p:], h0[..., :p]], axis=-1)
    h = lr_ref[...] * h0 + li_ref[...] * h0s + bu
    h_ref[...] = h
    y_ref[...] = jnp.einsum('gbp,gpc->gbc', h.astype(BF16), cy_ref[...], preferred_element_type=F32)


def _ssm_sample(u, h0_re, h0_im, tables, c_re, c_im):
    lb_re, lb_im, bbr, bbi = tables[-1]
    B = u.shape[0]
    G, C, P = N_SSM_GROUPS, SSM_GROUP, SSM_STATE
    ug = jnp.transpose(u.reshape(B, G, C), (1, 0, 2)).astype(BF16)
    h0 = jnp.transpose(jnp.concatenate([h0_re, h0_im], -1), (1, 0, 2)).astype(F32)
    bb = jnp.concatenate([jnp.transpose(bbr, (0, 2, 1)), jnp.transpose(bbi, (0, 2, 1))], -1).astype(BF16)
    lr = jnp.concatenate([lb_re, lb_re], -1)[:, None, :]
    li = jnp.concatenate([-lb_im, lb_im], -1)[:, None, :]
    cy = jnp.concatenate([jnp.transpose(c_re, (0, 2, 1)), -jnp.transpose(c_im, (0, 2, 1))], 1).astype(BF16)
    y, h = pl.pallas_call(
        _ssm_step_kernel,
        out_shape=(jax.ShapeDtypeStruct((G, B, C), F32), jax.ShapeDtypeStruct((G, B, 2 * P), F32)),
        name="ssm_step",
    )(ug, h0, bb, lr, li, cy)
    return jnp.transpose(y, (1, 0, 2)).reshape(B, D_SSM), jnp.transpose(h, (1, 0, 2))


def _compress_tables(phi_pe, phi_w1, phi_b1, phi_w2, phi_b2):
    S, H, Dh = CMP_STRIDE, N_KV_HEADS, HEAD_DIM
    w1 = phi_w1.reshape(2, 2, S, Dh, Dh)
    eye_c = jnp.eye(2, dtype=F32)
    eye_h = jnp.eye(H, dtype=F32)
    wbig = jnp.einsum('cajde,xc,yh->jxydache', w1, eye_c, eye_h).reshape(S * 2 * H * Dh, 2 * 2 * H * Dh)
    pe = jnp.transpose(phi_pe.reshape(2, 2, S, Dh), (1, 2, 0, 3))
    pe_rows = jnp.broadcast_to(pe[:, :, :, None, :], (2, S, 2, H, Dh)).reshape(2, S * 2 * H * Dh)
    n = 2 * H * Dh
    pe_w = (jnp.dot(pe_rows[0], wbig[:, :n], precision=HIGHEST) + jnp.dot(pe_rows[1], wbig[:, n:], precision=HIGHEST))
    b1 = jnp.broadcast_to(phi_b1[:, None, :], (2, H, Dh)).reshape(1, n) + pe_w[None, :]
    w2 = jnp.einsum('cef,cx,hy->chexyf', phi_w2, eye_c, eye_h).reshape(n, n)
    b2 = jnp.broadcast_to(phi_b2[:, None, :], (2, H, Dh)).reshape(1, n)
    return wbig.astype(BF16), b1, w2.astype(BF16), b2


def _compress_in_kernel(x_ref, w_ref, z_ref):
    z_ref[0] = jnp.dot(x_ref[0].astype(BF16), w_ref[...], preferred_element_type=F32)


def _compress_in(x2, tables):
    wbig = tables[0]
    N2 = wbig.shape[1]
    B, n, K = x2.shape
    tr = math.gcd(n, 256)
    return pl.pallas_call(
        _compress_in_kernel,
        out_shape=jax.ShapeDtypeStruct((B, n, N2), F32),
        grid=(B, n // tr),
        in_specs=[pl.BlockSpec((1, tr, K), lambda b, i: (b, i, 0)),
                  pl.BlockSpec((K, N2), lambda b, i: (0, 0))],
        out_specs=pl.BlockSpec((1, tr, N2), lambda b, i: (b, i, 0)),
        compiler_params=_cparams("parallel", "parallel"),
        name="compress_in",
    )(x2, wbig)


def _compress_in_paged_kernel(pt_ref, *refs, n_pg):
    x_refs = refs[:n_pg]
    w_ref, z_ref = refs[n_pg:n_pg + 2]
    scratch = refs[n_pg + 2:]
    n_slab = D_KV // LANE
    pg_part = n_pg // PAGE_PARTS
    cpp = PAGE_SIZE // CMP_STRIDE
    rows = pg_part * cpp
    for part in range(PAGE_PARTS):
        s_refs = scratch[part * n_slab:(part + 1) * n_slab]
        for k in range(pg_part):
            t = x_refs[part * pg_part + k][0].reshape(D_KV, PAGE_SIZE).T
            for c, s_ref in enumerate(s_refs):
                for n in range(cpp):
                    r0 = (k * cpp + n) * CHUNK_PITCH
                    s_ref[r0:r0 + CMP_STRIDE, :] = t[n * CMP_STRIDE:(n + 1) * CMP_STRIDE, c * LANE:(c + 1) * LANE]
        z = jnp.zeros((rows, w_ref.shape[1]), F32)
        for j in range(CMP_STRIDE):
            xj = jnp.concatenate([s_ref[pl.ds(j, rows, stride=CHUNK_PITCH), :] for s_ref in s_refs], axis=1)
            z = z + jnp.dot(xj.astype(BF16), w_ref[j * D_KV:(j + 1) * D_KV, :], preferred_element_type=F32)
        z_ref[0, part * rows:(part + 1) * rows, :] = z


def _compress_in_paged(pool_t, page_table, tables):
    wbig = tables[0]
    N2 = wbig.shape[1]
    K = wbig.shape[0]
    B, n_pages = page_table.shape
    n_pg = math.gcd(n_pages, PAGES_PER_STEP)
    rows = n_pg * PAGE_SIZE // CMP_STRIDE
    page_spec = lambda k: pl.BlockSpec((1,) + pool_t.shape[1:],
                                       lambda b, i, pt, k=k: (pt[b, i * n_pg + k], 0, 0, 0, 0))
    grid_spec = pltpu.PrefetchScalarGridSpec(
        num_scalar_prefetch=1,
        grid=(B, n_pages // n_pg),
        in_specs=[page_spec(k) for k in range(n_pg)] + [pl.BlockSpec((K, N2), lambda b, i, pt: (0, 0))],
        out_specs=pl.BlockSpec((1, rows, N2), lambda b, i, pt: (b, i, 0)),
        scratch_shapes=[pltpu.VMEM((rows // PAGE_PARTS * CHUNK_PITCH, LANE), F32)
                        for _ in range(PAGE_PARTS * (D_KV // LANE))],
    )
    return pl.pallas_call(
        functools.partial(_compress_in_paged_kernel, n_pg=n_pg),
        out_shape=jax.ShapeDtypeStruct((B, n_pages * PAGE_SIZE // CMP_STRIDE, N2), F32),
        grid_spec=grid_spec,
        compiler_params=_cparams("arbitrary", "arbitrary"),
        name="compress_in_paged",
    )(page_table, *([pool_t] * n_pg), wbig)


def _compress_out_kernel(*refs):
    z_refs, (b1_ref, w2_ref, b2_ref, o_ref) = refs[:-4], refs[-4:]
    z = jnp.concatenate([z_ref[0] for z_ref in z_refs], axis=0)
    n = z.shape[-1] // 2
    rows = z.shape[0]
    second = pltpu.roll(z[:, n:], rows - 1, axis=0)
    hdn = jax.nn.gelu(z[:, :n] + second + b1_ref[...])
    o_ref[0, :rows, :] = jnp.dot(hdn.astype(BF16), w2_ref[...], preferred_element_type=F32) + b2_ref[...]
    if o_ref.shape[1] > rows:
        o_ref[0, rows:, :] = jnp.zeros((o_ref.shape[1] - rows, n), F32)


def _compress_out(zs, tables, n_out):
    _, b1, w2, b2 = tables
    B, _, N2 = zs[0].shape
    return pl.pallas_call(
        _compress_out_kernel,
        out_shape=jax.ShapeDtypeStruct((B, n_out, N2 // 2), F32),
        grid=(B,),
        in_specs=[pl.BlockSpec((1, z.shape[1], N2), lambda b: (b, 0, 0)) for z in zs] + [
                  pl.BlockSpec((1, N2 // 2), lambda b: (0, 0)),
                  pl.BlockSpec((N2 // 2, N2 // 2), lambda b: (0, 0)),
                  pl.BlockSpec((1, N2 // 2), lambda b: (0, 0))],
        out_specs=pl.BlockSpec((1, n_out, N2 // 2), lambda b: (b, 0, 0)),
        compiler_params=_cparams("parallel"),
        name="compress_out",
    )(*zs, b1, w2, b2)


def _rel_bucket(dist):
    n = jnp.maximum(dist, 0)
    max_exact = NUM_BUCKETS // 2
    nf = jnp.maximum(n, 1).astype(F32)
    large = max_exact + (jnp.log(nf / max_exact) / math.log(REL_MAX_DIST / max_exact)
                         * (NUM_BUCKETS - max_exact)).astype(jnp.int32)
    large = jnp.minimum(large, NUM_BUCKETS - 1)
    return jnp.where(n < max_exact, n, large)


def _bias_by_distance(rel_bias, n_max):
    onehot = (_rel_bucket(jnp.arange(n_max))[None, :] == jnp.arange(NUM_BUCKETS)[:, None]).astype(F32)
    return jnp.dot(jnp.transpose(rel_bias.astype(F32)), onehot, precision=HIGHEST)


def _shifted_chunks(bias_n, pad, n_chunks, width):
    n = min(bias_n.shape[1], n_chunks * width - pad)
    ext = jnp.concatenate([jnp.broadcast_to(bias_n[:, :1], (N_HEADS, pad)), bias_n[:, :n],
                           jnp.zeros((N_HEADS, n_chunks * width - pad - n), F32)], axis=1)
    return ext.reshape(N_HEADS, n_chunks, width)


def _bias_tables_kernel(ed_ref, ec_ref, tzs_ref, tzw_ref, cmp_ref, *, tq, tk, n_qt):
    n_ds, n_dw, n_j = tzs_ref.shape[1] - 1, tzw_ref.shape[1] - 1, cmp_ref.shape[1] // 8
    tzs_ref[0, n_ds] = jnp.full((tk, tq), NEG, F32)
    tzw_ref[0, n_dw] = jnp.full((tk, tq), NEG, F32)
    w = tq + tk
    c = lax.broadcasted_iota(jnp.int32, (tk, tq), 0)
    r = lax.broadcasted_iota(jnp.int32, (tk, tq), 1)
    for d in range(n_ds):
        v = jnp.concatenate([ed_ref[0, d:d + 1, :], ed_ref[0, d + 1:d + 2, :]], axis=1)
        t = pltpu.roll(jnp.broadcast_to(v, (tk, w)), w - (tk - 1), axis=1, stride=1, stride_axis=0)[:, :tq]
        dist = d * tk + r - c
        tzs_ref[0, d] = jnp.where(dist >= 0, t, NEG)
        if d < n_dw:
            tzw_ref[0, d] = jnp.where((dist >= 0) & (dist <= WINDOW), t, NEG)
    for j in range(n_j):
        dd = n_qt - 1 - j
        c0, c1 = max(dd, 0), max(dd + 1, 0)
        v = jnp.concatenate([ec_ref[0, c0:c0 + 1, :], ec_ref[0, c1:c1 + 1, :]], axis=1)
        t = pltpu.roll(jnp.broadcast_to(v, (8, w)), w - 7 * CMP_STRIDE, axis=1, stride=CMP_STRIDE, stride_axis=0)
        cmp_ref[0, j * 8:(j + 1) * 8, :] = t[:, :tq]


def _bias_tables(bias_n, n_qt, n_rb, n_ds, n_dw, tq, tk):
    assert tq == tk == 8 * CMP_STRIDE and n_dw <= n_ds
    n_j = n_rb + n_qt - 1
    ed = _shifted_chunks(bias_n, tk - 1, n_ds + 1, tq)
    ec = _shifted_chunks(bias_n, 7 * CMP_STRIDE + CMP_BLOCK - 1, n_qt + 1, tq)
    head = lambda a: pl.BlockSpec((1,) + a.shape[1:], lambda h: (h,) + (0,) * (a.ndim - 1))
    outs = (jax.ShapeDtypeStruct((N_HEADS, n_ds + 1, tk, tq), F32),
            jax.ShapeDtypeStruct((N_HEADS, n_dw + 1, tk, tq), F32),
            jax.ShapeDtypeStruct((N_HEADS, n_j * 8, tq), F32))
    tzs, tzw, cmp = pl.pallas_call(
        functools.partial(_bias_tables_kernel, tq=tq, tk=tk, n_qt=n_qt),
        out_shape=outs,
        grid=(N_HEADS,),
        in_specs=[head(ed), head(ec)],
        out_specs=tuple(head(o) for o in outs),
        compiler_params=_cparams("parallel"),
        name="bias_tables",
    )(ed, ec)
    grp = lambda a: a.reshape((N_KV_HEADS, GQA) + a.shape[1:])
    return grp(tzs), grp(tzw), cmp


def _pool_matrix(n_cmp_pad, n_blk_pad):
    r = SEL_BLOCK // CMP_STRIDE
    i = np.arange(n_cmp_pad)[None, :]
    j = np.arange(n_blk_pad)[:, None]
    return ((i >= r * j - 1) & (i <= r * j + r - 1)).astype(np.float32)


def _cmp_select_kernel(q_ref, k_ref, vt_ref, bias_ref, pool_ref, o_ref, sel_ref, *, tq, n_cmp):
    qt = pl.program_id(2)
    n_qt = pl.num_programs(2)
    q = q_ref[0, 0].reshape(GQA * tq, HEAD_DIM)
    k = k_ref[0, 0]
    nc = k.shape[0]
    s = _nt_dot(k, q)
    row0 = pl.multiple_of((n_qt - 1 - qt) * 8, 8)
    s = s + jnp.concatenate([bias_ref[g, pl.ds(row0, nc), :] for g in range(GQA)], axis=-1)
    t_pos = qt * tq + (lax.broadcasted_iota(jnp.int32, (nc, GQA * tq), 1) % tq)
    ci = lax.broadcasted_iota(jnp.int32, (nc, GQA * tq), 0)
    mask = (ci * CMP_STRIDE + CMP_BLOCK - 1 <= t_pos) & (ci < n_cmp)
    s = jnp.where(mask, s, NEG)
    m = jnp.max(s, axis=0, keepdims=True)
    p = jnp.where(mask, jnp.exp(s - m), 0.0)
    p = p / jnp.maximum(jnp.sum(p, axis=0, keepdims=True), 1e-30)
    ot = jnp.dot(vt_ref[0, 0], p.astype(BF16), preferred_element_type=F32)
    o_ref[0] = jnp.concatenate([ot[:, g * tq:(g + 1) * tq].T for g in range(GQA)], axis=-1)
    imp = p[:, 0:tq]
    for g in range(1, GQA):
        imp = imp + p[:, g * tq:(g + 1) * tq]
    sb = jnp.dot(pool_ref[...], imp, precision=HIGHEST, preferred_element_type=F32)
    nb = sb.shape[0]
    blk = lax.broadcasted_iota(jnp.int32, (nb, tq), 0)
    cur = (qt * tq + lax.broadcasted_iota(jnp.int32, (nb, tq), 1)) // SEL_BLOCK
    causal = blk <= cur
    forced = (blk == 0) | (blk == cur) | (blk == cur - 1)
    sc = jnp.where(forced & causal, 1e4, jnp.where(causal, sb, -1.0))
    groups = [sc[r:r + 8] for r in range(0, nb, 8)]
    sub = lax.broadcasted_iota(jnp.int32, (8, tq), 0)
    ranks = [jnp.zeros((8, tq), F32) for _ in groups]
    for i in range(nb):
        row = sc[i:i + 1, :]
        for gi, grp in enumerate(groups):
            if gi * 8 > i:
                ahead = row >= grp
            elif gi * 8 + 7 < i:
                ahead = row > grp
            else:
                ahead = (row > grp) | ((row == grp) & (sub > i - gi * 8))
            ranks[gi] = ranks[gi] + jnp.where(ahead, 1.0, 0.0)
    rank = jnp.concatenate(ranks, axis=0)
    sel_ref[0, 0] = jnp.where((rank < N_SEL) & causal, 0.0, NEG)


def _cmp_select_prompt(q5, kc, vct, bias_tab, pool, n_cmp):
    B, _, _, T, _ = q5.shape
    NC = kc.shape[2]
    NB = pool.shape[0]
    R = bias_tab.shape[1]
    tq = ATT_TQ
    return pl.pallas_call(
        functools.partial(_cmp_select_kernel, tq=tq, n_cmp=n_cmp),
        out_shape=(jax.ShapeDtypeStruct((B, T, D_ATT), F32),
                   jax.ShapeDtypeStruct((B, N_KV_HEADS, NB, T), F32)),
        grid=(B, N_KV_HEADS, T // tq),
        in_specs=[pl.BlockSpec((1, 1, GQA, tq, HEAD_DIM), lambda b, h, i: (b, h, 0, i, 0)),
                  pl.BlockSpec((1, 1, NC, HEAD_DIM), lambda b, h, i: (b, h, 0, 0)),
                  pl.BlockSpec((1, 1, HEAD_DIM, NC), lambda b, h, i: (b, h, 0, 0)),
                  pl.BlockSpec((GQA, R, tq), lambda b, h, i: (h, 0, 0)),
                  pl.BlockSpec((NB, NC), lambda b, h, i: (0, 0))],
        out_specs=(pl.BlockSpec((1, tq, GQA * HEAD_DIM), lambda b, h, i: (b, i, h)),
                   pl.BlockSpec((1, 1, NB, tq), lambda b, h, i: (b, h, 0, i))),
        compiler_params=_cparams("parallel", "parallel", "parallel"),
        name="cmp_select_prompt",
    )(q5, kc, vct, bias_tab, pool)


def _sel_win_kernel(q_ref, ks_ref, vst_ref, kw_ref, vwt_ref, sel_ref, tzs_ref, tzw_ref, os_ref, ow_ref, *, tq):
    tk = ATT_TK
    qt = pl.program_id(2)
    q = q_ref[0, 0].reshape(GQA * tq, HEAD_DIM)
    width = GQA * tq
    per_tile = tk // SEL_BLOCK

    def make_sweep(k_ref, vt_ref, tz_ref, use_sel, n_chains, single_trip):
        n_d = tz_ref.shape[2] - 1

        def scores(kt, hi):
            pad = kt > hi
            kt = jnp.minimum(kt, hi)
            off = pl.multiple_of(kt * tk, tk)
            k = k_ref[0, 0, pl.ds(off, tk), :]
            d = jnp.where(pad, n_d, jnp.minimum(qt - kt, n_d - 1))
            bias = [tz_ref[0, g, d] for g in range(GQA)]
            if use_sel:
                rows = sel_ref[0, 0, pl.ds(kt * per_tile, per_tile), :]
                selb = jnp.concatenate([jnp.broadcast_to(rows[i:i + 1], (SEL_BLOCK, tq))
                                        for i in range(per_tile)], axis=0)
                bias = [b + selb for b in bias]
            return _nt_dot(k, q) + jnp.concatenate(bias, axis=1)

        def values_t(kt, lo, hi):
            off = pl.multiple_of(jnp.clip(kt, lo, hi) * tk, tk)
            return vt_ref[0, 0, :, pl.ds(off, tk)]

        def sweep(lo, hi):
            n_trips = (hi - lo + n_chains) // n_chains
            chain0 = (jnp.full((1, width), 0.5 * NEG, F32), jnp.zeros((1, width), F32),
                      jnp.zeros((HEAD_DIM, width), F32), jnp.ones((1, width), F32), jnp.zeros((tk, width), BF16))

            def trip(i, chains):
                kt = lo + n_chains * i
                pv = [jnp.dot(values_t(kt - n_chains + c, lo, hi), chains[c][4], preferred_element_type=F32)
                      for c in range(n_chains)]
                ss = [scores(kt + c, hi) for c in range(n_chains)]
                out = []
                for c in range(n_chains):
                    m, l, acc, alpha_prev, _ = chains[c]
                    m_new = jnp.maximum(m, jnp.max(ss[c], axis=0, keepdims=True))
                    alpha = jnp.exp(m - m_new)
                    p = jnp.exp(ss[c] - m_new)
                    l = alpha * l + jnp.sum(p, axis=0, keepdims=True)
                    out.append((m_new, l, alpha_prev * acc + pv[c], alpha, p.astype(BF16)))
                return tuple(out)

            if single_trip:
                done = []
                for c in range(n_chains):
                    s = scores(lo + c, hi)
                    m = jnp.maximum(jnp.max(s, axis=0, keepdims=True), 0.5 * NEG)
                    p = jnp.exp(s - m)
                    done.append((m, jnp.sum(p, axis=0, keepdims=True),
                                 jnp.dot(values_t(lo + c, lo, hi), p.astype(BF16), preferred_element_type=F32)))
            else:
                chains = lax.fori_loop(0, n_trips, trip, (chain0,) * n_chains)
                kt_last = lo + n_chains * (n_trips - 1)
                done = []
                for c in range(n_chains):
                    m, l, acc, alpha, p = chains[c]
                    done.append((m, l, alpha * acc + jnp.dot(values_t(kt_last + c, lo, hi), p,
                                                              preferred_element_type=F32)))
            m_all = functools.reduce(jnp.maximum, [m for m, _, _ in done])
            num = den = 0.0
            for m, l, acc in done:
                e = jnp.exp(m - m_all)
                num = num + acc * e
                den = den + l * e
            o = num / jnp.maximum(den, 1e-30)
            return jnp.concatenate([o[:, g * tq:(g + 1) * tq].T for g in range(GQA)], axis=-1)
        return sweep

    n_win = tzw_ref.shape[2] - 1
    os_ref[0] = make_sweep(ks_ref, vst_ref, tzs_ref, True, SEL_CHAINS, False)(0, qt)
    ow_ref[0] = make_sweep(kw_ref, vwt_ref, tzw_ref, False, n_win, True)(jnp.maximum(qt - (n_win - 1), 0), qt)


def _sel_win_prompt(q5, ks, vst, kw, vwt, sel, tzs, tzw):
    B, _, _, T, _ = q5.shape
    NB = sel.shape[2]
    tq = ATT_TQ
    k_spec = pl.BlockSpec((1, 1, T, HEAD_DIM), lambda b, h, i: (b, h, 0, 0))
    vt_spec = pl.BlockSpec((1, 1, HEAD_DIM, T), lambda b, h, i: (b, h, 0, 0))
    tz_spec = lambda tz: pl.BlockSpec((1,) + tz.shape[1:], lambda b, h, i: (h, 0, 0, 0, 0))
    o_spec = pl.BlockSpec((1, tq, GQA * HEAD_DIM), lambda b, h, i: (b, i, h))
    return pl.pallas_call(
        functools.partial(_sel_win_kernel, tq=tq),
        out_shape=(jax.ShapeDtypeStruct((B, T, D_ATT), F32), jax.ShapeDtypeStruct((B, T, D_ATT), F32)),
        grid=(B, N_KV_HEADS, T // tq),
        in_specs=[pl.BlockSpec((1, 1, GQA, tq, HEAD_DIM), lambda b, h, i: (b, h, 0, i, 0)),
                  k_spec, vt_spec, k_spec, vt_spec,
                  pl.BlockSpec((1, 1, NB, tq), lambda b, h, i: (b, h, 0, i)),
                  tz_spec(tzs), tz_spec(tzw)],
        out_specs=(o_spec, o_spec),
        compiler_params=_cparams("parallel", "parallel", "parallel"),
        name="sel_win_prompt",
    )(q5, ks, vst, kw, vwt, sel, tzs, tzw)


def _gate_expand_matrix():
    m = np.zeros((3, 2 * LANE, D_ATT), np.float32)
    for r in range(3):
        for h in range(N_HEADS):
            m[r, h * 3 + r, h * HEAD_DIM:(h + 1) * HEAD_DIM] = 1.0
            m[r, LANE + h * 3 + r, h * HEAD_DIM:(h + 1) * HEAD_DIM] = 1.0
    return m


def _split_bf16(x):
    hi = x.astype(BF16)
    return hi, (x - hi.astype(F32)).astype(BF16)


def _post_mixer_kernel(y_ref, u_ref, oc_ref, os_ref, ow_ref, g_ref, x_ref, gate_ref, sh_ref, sc_ref,
                       dskip_ref, wglu_ref, bglu_ref, gexp_ref, wout_ref, lng_ref, lnb_ref,
                       wr_ref, br_ref, x1_ref, hm_ref, te_ref, tw_ref):
    y = y_ref[0] + dskip_ref[...] * u_ref[0]
    gl = jax.nn.gelu(y)
    ssm = gl * jax.nn.sigmoid(jnp.dot(gl.astype(BF16), wglu_ref[...], preferred_element_type=F32)
                              + bglu_ref[...])
    sg = jnp.concatenate(_split_bf16(jax.nn.sigmoid(g_ref[0])), axis=1)
    att = jnp.zeros_like(oc_ref[0])
    for r, o_ref in enumerate((oc_ref, os_ref, ow_ref)):
        att = att + jnp.dot(sg, gexp_ref[r], preferred_element_type=F32) * o_ref[0]
    h = (jnp.dot(ssm.astype(BF16), wout_ref[:D_SSM, :], preferred_element_type=F32)
         + jnp.dot(att.astype(BF16), wout_ref[D_SSM:, :], preferred_element_type=F32))
    z = DN_ALPHA * x_ref[0] + gate_ref[0] * h
    x1 = _layer_norm(z) * lng_ref[...] + lnb_ref[...]
    x1_ref[0] = x1
    hm = _layer_norm(x1) * (1.0 + sc_ref[0]) + sh_ref[0]
    hm_ref[0] = hm
    hm_hi, hm_lo = _split_bf16(hm)
    logits = (jnp.dot(hm_hi, wr_ref[0], preferred_element_type=F32)
              + jnp.dot(hm_lo, wr_ref[0], preferred_element_type=F32)
              + jnp.dot(hm_hi, wr_ref[1], preferred_element_type=F32)) + br_ref[...]
    lane = lax.broadcasted_iota(jnp.int32, logits.shape, 1)
    work = jnp.where(lane < N_EXPERTS, logits, -jnp.inf)
    te = jnp.zeros(logits.shape, jnp.int32)
    tv = jnp.zeros(logits.shape, F32)
    for k in range(TOP_K):
        best = jnp.max(work, axis=-1, keepdims=True)
        arg = jnp.min(jnp.where(work == best, lane, LANE), axis=-1, keepdims=True)
        te = jnp.where(lane == k, arg, te)
        tv = jnp.where(lane == k, best, tv)
        work = jnp.where(lane == arg, -jnp.inf, work)
    ex = jnp.where(lane < TOP_K, jnp.exp(tv - tv[:, 0:1]), 0.0)
    te_ref[0] = te
    tw_ref[0] = ex / jnp.sum(ex, axis=-1, keepdims=True)


def _post_mixer(y, u, oc, osel, ow, g, x, gate, shift, scale, w, tm):
    B, T, D = x.shape
    R = gate.shape[1]
    rb = 1 if R == 1 else tm
    mod_map = (lambda b, i: (b, 0, 0)) if R == 1 else (lambda b, i: (b, i, 0))
    row = lambda n: pl.BlockSpec((1, tm, n), lambda b, i: (b, i, 0))
    mod = pl.BlockSpec((1, rb, D), mod_map)
    full = lambda a: pl.BlockSpec(a.shape, lambda b, i: (0,) * a.ndim)
    consts = (w['d_skip'], w['w_glu'], w['b_glu'], w['gexp'], w['w_out'], w['ln1_g'], w['ln1_b'],
              w['w_router'], w['b_router'])
    return pl.pallas_call(
        _post_mixer_kernel,
        out_shape=(jax.ShapeDtypeStruct((B, T, D), F32), jax.ShapeDtypeStruct((B, T, D), F32),
                   jax.ShapeDtypeStruct((B, T, LANE), jnp.int32), jax.ShapeDtypeStruct((B, T, LANE), F32)),
        grid=(B, T // tm),
        in_specs=[row(D_SSM), row(D_SSM), row(D_ATT), row(D_ATT), row(D_ATT), row(LANE), row(D),
                  mod, mod, mod] + [full(a) for a in consts],
        out_specs=(row(D), row(D), row(LANE), row(LANE)),
        compiler_params=_cparams("parallel", "parallel"),
        name="post_mixer",
    )(y, u, oc, osel, ow, g, x, gate, shift, scale, *consts)


def _expert_kernel(e_ref, blk_ref, lo_ref, hi_ref, first_ref, x_ref, wgu_ref, bgu_ref, wd_ref, bd_ref, o_ref,
                   wgu_s, wd_s):
    i = pl.program_id(0)
    fresh = (i == 0) | (e_ref[i] != e_ref[jnp.maximum(i - 1, 0)])

    @pl.when(fresh)
    def _():
        wgu_s[...] = wgu_ref[0].astype(BF16)
        wd_s[...] = wd_ref[0].astype(BF16)

    @pl.when(first_ref[i] == 1)
    def _():
        o_ref[...] = jnp.zeros_like(o_ref)

    @pl.when(hi_ref[i] > lo_ref[i])
    def _():
        gu = jnp.dot(x_ref[...].astype(BF16), wgu_s[...], preferred_element_type=F32) + bgu_ref[0]
        gate = jnp.minimum(gu[:, :D_FF], SWIGLU_LIMIT)
        up = jnp.clip(gu[:, D_FF:], -SWIGLU_LIMIT, SWIGLU_LIMIT)
        hh = (up + 1.0) * gate * jax.nn.sigmoid(SWIGLU_ALPHA * gate)
        y = jnp.dot(hh.astype(BF16), wd_s[...], preferred_element_type=F32) + bd_ref[0]
        row = blk_ref[i] * MOE_ROWS + lax.broadcasted_iota(jnp.int32, (MOE_ROWS, 1), 0)
        o_ref[...] = jnp.where((row >= lo_ref[i]) & (row < hi_ref[i]), y, o_ref[...])


def _experts(xb, items, w_gate_up, b_gate_up, w_down, b_down):
    rows, D = xb.shape
    n_items = items[0].shape[0]
    wmap = lambda i, e, blk, lo, hi, first: (e[i], 0, 0)
    rmap = lambda i, e, blk, lo, hi, first: (blk[i], 0)
    grid_spec = pltpu.PrefetchScalarGridSpec(
        num_scalar_prefetch=5,
        grid=(n_items,),
        in_specs=[pl.BlockSpec((MOE_ROWS, D), rmap),
                  pl.BlockSpec((1, D, 2 * D_FF), wmap),
                  pl.BlockSpec((1, 1, 2 * D_FF), wmap),
                  pl.BlockSpec((1, D_FF, D), wmap),
                  pl.BlockSpec((1, 1, D), wmap)],
        out_specs=pl.BlockSpec((MOE_ROWS, D), rmap),
        scratch_shapes=[pltpu.VMEM((D, 2 * D_FF), BF16), pltpu.VMEM((D_FF, D), BF16)],
    )
    return pl.pallas_call(
        _expert_kernel,
        out_shape=jax.ShapeDtypeStruct((rows, D), F32),
        grid_spec=grid_spec,
        compiler_params=_cparams("arbitrary"),
        name="moe_experts",
    )(*items, xb, w_gate_up, b_gate_up.reshape(N_EXPERTS, 1, 2 * D_FF), w_down,
      b_down.reshape(N_EXPERTS, 1, D))


def _moe_dispatch(top_e, n):
    blk = MOE_ROWS
    nk = n * TOP_K
    cb = 128
    assert nk % cb == 0
    e = top_e.reshape(-1)
    oh = (jnp.arange(N_EXPERTS)[:, None] == e[None, :]).astype(BF16).reshape(N_EXPERTS, nk // cb, cb)
    before = jnp.asarray(np.triu(np.ones((cb, cb), np.float32), 1), dtype=BF16)
    within = jnp.einsum('ebj,ji->ebi', oh, before, preferred_element_type=F32)
    blk_tot = jnp.sum(oh.astype(F32), axis=2)
    blk_off = jnp.cumsum(blk_tot, axis=1) - blk_tot
    counts = jnp.sum(blk_tot, axis=1)
    start = jnp.cumsum(counts) - counts
    dest = jnp.sum((within + (blk_off + start[:, None])[:, :, None]) * oh.astype(F32), axis=0)
    dest = dest.reshape(nk).astype(jnp.int32)
    order = jnp.argsort(dest)
    n_blk = -(-nk // blk)
    row_tok = jnp.concatenate([(order // TOP_K).astype(jnp.int32), jnp.full((n_blk * blk - nk,), n, jnp.int32)])
    counts_i, start_i = counts.astype(jnp.int32), start.astype(jnp.int32)
    first_b = start_i // blk
    last_b = (start_i + counts_i - 1) // blk
    n_it = jnp.where(counts_i > 0, last_b - first_b + 1, 0)
    it_end = jnp.cumsum(n_it)
    it_start = it_end - n_it
    n_items = n_blk + N_EXPERTS - 1
    i = jnp.arange(n_items)
    live = i < it_end[-1]
    it_e = jnp.minimum(jnp.sum(it_end[None, :] <= i[:, None], axis=1), N_EXPERTS - 1)
    it_blk = jnp.where(live, first_b[it_e] + i - it_start[it_e], n_blk - 1)
    it_lo = jnp.where(live, start_i[it_e], 0)
    it_hi = jnp.where(live, start_i[it_e] + counts_i[it_e], 0)
    it_first = jnp.concatenate([jnp.ones((1,), jnp.int32), (it_blk[1:] != it_blk[:-1]).astype(jnp.int32)])
    items = tuple(a.astype(jnp.int32) for a in (it_e, it_blk, it_lo, it_hi, it_first))
    return row_tok, dest.reshape(n, TOP_K), items


def _final_kernel(x_ref, y0_ref, y1_ref, y2_ref, y3_ref, tw_ref, gate_ref, lng_ref, lnb_ref, o_ref):
    tw = tw_ref[0]
    y = jnp.zeros_like(x_ref[0])
    for k, y_ref in enumerate((y0_ref, y1_ref, y2_ref, y3_ref)):
        y = y + tw[:, k:k + 1] * y_ref[0]
    z = DN_ALPHA * x_ref[0] + gate_ref[0] * y
    o_ref[0] = _layer_norm(z) * lng_ref[...] + lnb_ref[...]


def _final(x1, ys, tw, gate, ln_g, ln_b, tm):
    B, T, D = x1.shape
    R = gate.shape[1]
    rb = 1 if R == 1 else tm
    mod_map = (lambda b, i: (b, 0, 0)) if R == 1 else (lambda b, i: (b, i, 0))
    row = lambda n: pl.BlockSpec((1, tm, n), lambda b, i: (b, i, 0))
    vec = pl.BlockSpec((1, D), lambda b, i: (0, 0))
    return pl.pallas_call(
        _final_kernel,
        out_shape=jax.ShapeDtypeStruct((B, T, D), F32),
        grid=(B, T // tm),
        in_specs=[row(D), row(D), row(D), row(D), row(D), row(LANE),
                  pl.BlockSpec((1, rb, D), mod_map), vec, vec],
        out_specs=row(D),
        compiler_params=_cparams("parallel", "parallel"),
        name="moe_combine_ln",
    )(x1, *ys, tw, gate, ln_g, ln_b)


def _cmp_select_step_kernel(q_ref, kv_ref, bias_ref, pool_ref, o_ref, idx_ref, *, n_cmp, n_blk, q_pos):
    q = q_ref[0].astype(BF16)
    ncp = kv_ref.shape[1]
    nbp = pool_ref.shape[1]
    hd = HEAD_DIM
    kv = kv_ref[0]
    kb = [kv[:, h * hd:(h + 1) * hd].astype(BF16) for h in range(N_KV_HEADS)]
    vb = [kv[:, (N_KV_HEADS + h) * hd:(N_KV_HEADS + h + 1) * hd].astype(BF16) for h in range(N_KV_HEADS)]
    row = lax.broadcasted_iota(jnp.int32, (N_HEADS, 1), 0)
    first = row < GQA
    s = jnp.where(first, _nt_dot(q, kb[0]), _nt_dot(q, kb[1])) * (hd ** -0.5)
    s = s + bias_ref[...]
    ci = lax.broadcasted_iota(jnp.int32, (N_HEADS, ncp), 1)
    mask = (ci * CMP_STRIDE + CMP_BLOCK - 1 <= q_pos) & (ci < n_cmp)
    s = jnp.where(mask, s, NEG)
    m = jnp.max(s, axis=-1, keepdims=True)
    p = jnp.where(mask, jnp.exp(s - m), 0.0)
    p = p / jnp.maximum(jnp.sum(p, axis=-1, keepdims=True), 1e-30)
    pb = p.astype(BF16)
    o_ref[0] = jnp.where(first, jnp.dot(pb, vb[0], preferred_element_type=F32),
                         jnp.dot(pb, vb[1], preferred_element_type=F32))
    imp0 = jnp.sum(jnp.where(first, p, 0.0), axis=0, keepdims=True)
    imp1 = jnp.sum(jnp.where(first, 0.0, p), axis=0, keepdims=True)
    imp = jnp.where(first, imp0, imp1)
    sb = jnp.dot(imp, pool_ref[...], precision=HIGHEST, preferred_element_type=F32)
    cur = q_pos // SEL_BLOCK
    bi = lax.broadcasted_iota(jnp.int32, (nbp, nbp), 0)
    bj = lax.broadcasted_iota(jnp.int32, (nbp, nbp), 1)
    blk = lax.broadcasted_iota(jnp.int32, (1, nbp), 1)
    causal = blk <= cur
    forced = (blk == 0) | (blk == cur) | (blk == cur - 1)
    rsel = lax.broadcasted_iota(jnp.int32, (N_SEL, nbp), 0)
    for h in range(N_KV_HEADS):
        sc = jnp.where(forced & causal, 1e4, jnp.where(causal, sb[h * GQA:h * GQA + 1, :], -1.0))
        sc = jnp.where(blk < n_blk, sc, -2.0)
        scb = jnp.broadcast_to(sc, (nbp, nbp))
        col = jnp.sum(jnp.where(bi == bj, scb, 0.0), axis=1, keepdims=True)
        ahead = (col > scb) | ((col == scb) & (bi < bj))
        rank = jnp.sum(ahead.astype(jnp.int32), axis=0, keepdims=True)
        hit = jnp.broadcast_to(rank, (N_SEL, nbp)) == rsel
        idx = jnp.sum(jnp.where(hit, jnp.broadcast_to(blk, (N_SEL, nbp)), 0), axis=1, keepdims=True)
        idx_ref[0, h] = jnp.broadcast_to(idx, (N_SEL, LANE))


def _cmp_select_step(q, ckv, bias, pool, n_cmp, n_blk, q_pos):
    B = q.shape[0]
    NCp = ckv.shape[1]
    return pl.pallas_call(
        functools.partial(_cmp_select_step_kernel, n_cmp=n_cmp, n_blk=n_blk, q_pos=q_pos),
        out_shape=(jax.ShapeDtypeStruct((B, N_HEADS, HEAD_DIM), F32),
                   jax.ShapeDtypeStruct((B, N_KV_HEADS, N_SEL, LANE), jnp.int32)),
        grid=(B,),
        in_specs=[pl.BlockSpec((1, N_HEADS, HEAD_DIM), lambda b: (b, 0, 0)),
                  pl.BlockSpec((1, NCp, D_KV), lambda b: (b, 0, 0)),
                  pl.BlockSpec(bias.shape, lambda b: (0, 0)),
                  pl.BlockSpec(pool.shape, lambda b: (0, 0))],
        out_specs=(pl.BlockSpec((1, N_HEADS, HEAD_DIM), lambda b: (b, 0, 0)),
                   pl.BlockSpec((1, N_KV_HEADS, N_SEL, LANE), lambda b: (b, 0, 0, 0))),
        compiler_params=_cparams("parallel"),
        name="cmp_select_step",
    )(q, ckv, bias, pool)


def _sel_step_kernel(pg_ref, idx_ref, q_ref, *refs, n_past, q_pos):
    page_refs = refs[:N_SEL]
    new_ref, bias_ref, kpos_ref, o_ref = refs[N_SEL:]
    b, h = pl.program_id(0), pl.program_id(1)
    base = (b * N_KV_HEADS + h) * N_SEL
    kts, vts = [], []
    for j in range(N_SEL):
        is_new = idx_ref[base + j] >= n_past
        kts.append(jnp.where(is_new, new_ref[0, 0, 0], page_refs[j][0, 0, 0]))
        vts.append(jnp.where(is_new, new_ref[0, 1, 0], page_refs[j][0, 1, 0]))
    kt = jnp.concatenate(kts, axis=1).astype(BF16)
    vt = jnp.concatenate(vts, axis=1).astype(BF16)
    s = jnp.dot(q_ref[0].astype(BF16), kt, preferred_element_type=F32) * (HEAD_DIM ** -0.5) + bias_ref[0, 0]
    mask = kpos_ref[0, 0] <= q_pos
    s = jnp.where(mask, s, NEG)
    m = jnp.max(s, axis=-1, keepdims=True)
    p = jnp.where(mask, jnp.exp(s - m), 0.0)
    l = jnp.sum(p, axis=-1, keepdims=True)
    o_ref[0, 0] = _nt_dot(p.astype(BF16), vt) / jnp.maximum(l, 1e-30)


def _sel_step(q, pool_t, new_t, bias_sel, kpos, pages, idx_flat, n_past, q_pos):
    B = q.shape[0]
    nk = N_SEL * PAGE_SIZE
    slot = lambda b, h, j: (b * N_KV_HEADS + h) * N_SEL + j
    page_spec = lambda j: pl.BlockSpec((1, 2, 1, HEAD_DIM, PAGE_SIZE),
                                       lambda b, h, pg, ix, j=j: (pg[slot(b, h, j)], 0, h, 0, 0))
    grid_spec = pltpu.PrefetchScalarGridSpec(
        num_scalar_prefetch=2,
        grid=(B, N_KV_HEADS),
        in_specs=[pl.BlockSpec((1, N_HEADS, HEAD_DIM), lambda b, h, pg, ix: (b, 0, 0))]
        + [page_spec(j) for j in range(N_SEL)]
        + [pl.BlockSpec((1, 2, 1, HEAD_DIM, PAGE_SIZE), lambda b, h, pg, ix: (b, 0, h, 0, 0)),
           pl.BlockSpec((1, 1, N_HEADS, nk), lambda b, h, pg, ix: (b, h, 0, 0)),
           pl.BlockSpec((1, 1, 1, nk), lambda b, h, pg, ix: (b, h, 0, 0))],
        out_specs=pl.BlockSpec((1, 1, N_HEADS, HEAD_DIM), lambda b, h, pg, ix: (b, h, 0, 0)),
    )
    return pl.pallas_call(
        functools.partial(_sel_step_kernel, n_past=n_past, q_pos=q_pos),
        out_shape=jax.ShapeDtypeStruct((B, N_KV_HEADS, N_HEADS, HEAD_DIM), F32),
        grid_spec=grid_spec,
        compiler_params=_cparams("arbitrary", "arbitrary"),
        name="sel_step",
    )(pages, idx_flat, q, *([pool_t] * N_SEL), new_t, bias_sel, kpos)


def _win_step_kernel(q_ref, w_ref, new_ref, bias_ref, bias0_ref, o_ref):
    q = q_ref[0]
    qb = q.astype(BF16)
    row = lax.broadcasted_iota(jnp.int32, (N_HEADS, 1), 0)
    first = row < GQA
    hd = HEAD_DIM
    kt = [w_ref[0, 0, h].astype(BF16) for h in range(N_KV_HEADS)]
    vt = [w_ref[0, 1, h].astype(BF16) for h in range(N_KV_HEADS)]
    dots = [jnp.dot(qb, kt[h], preferred_element_type=F32) for h in range(N_KV_HEADS)]
    s = jnp.where(first, dots[0], dots[1]) * (hd ** -0.5) + bias_ref[...]
    new = new_ref[0]
    kn = jnp.where(first, new[:, 0:hd], new[:, hd:2 * hd])
    vn = jnp.where(first, new[:, 2 * hd:3 * hd], new[:, 3 * hd:])
    sn = jnp.sum(q * kn, axis=-1, keepdims=True) * (hd ** -0.5) + bias0_ref[...]
    m = jnp.maximum(jnp.max(s, axis=-1, keepdims=True), sn)
    p = jnp.exp(s - m)
    pn = jnp.exp(sn - m)
    l = jnp.sum(p, axis=-1, keepdims=True) + pn
    pb = p.astype(BF16)
    acc = jnp.where(first, _nt_dot(pb, vt[0]), _nt_dot(pb, vt[1])) + pn * vn
    o_ref[0] = acc / jnp.maximum(l, 1e-30)


def _win_step(q, win_t, new, bias, bias0):
    B, W = win_t.shape[0], win_t.shape[-1]
    return pl.pallas_call(
        _win_step_kernel,
        out_shape=jax.ShapeDtypeStruct((B, N_HEADS, HEAD_DIM), F32),
        grid=(B,),
        in_specs=[pl.BlockSpec((1, N_HEADS, HEAD_DIM), lambda b: (b, 0, 0)),
                  pl.BlockSpec((1,) + win_t.shape[1:], lambda b: (b, 0, 0, 0, 0)),
                  pl.BlockSpec((1, 1, D_KV), lambda b: (b, 0, 0)),
                  pl.BlockSpec((N_HEADS, W), lambda b: (0, 0)),
                  pl.BlockSpec((N_HEADS, 1), lambda b: (0, 0))],
        out_specs=pl.BlockSpec((1, N_HEADS, HEAD_DIM), lambda b: (b, 0, 0)),
        compiler_params=_cparams("parallel"),
        name="win_step",
    )(q, win_t, new, bias, bias0)


def _split_heads(kv, dtype):
    B, L, _ = kv.shape
    kv5 = kv.reshape(B, L, 2, N_KV_HEADS, HEAD_DIM)
    return (jnp.transpose(kv5[:, :, 0], (0, 2, 1, 3)).astype(dtype),
            jnp.transpose(kv5[:, :, 1], (0, 2, 1, 3)).astype(dtype))


def _nsa_prompt(q5, kvc, ks, vst, kw, vwt, cmp_tab, rel_bias):
    B, T, _ = kvc.shape
    nc = T // CMP_STRIDE
    nb = T // SEL_BLOCK
    ckv = _compress_out([_compress_in(kvc.reshape(B, nc, CMP_STRIDE * D_KV), cmp_tab)], cmp_tab, nc)
    kc, vc = _split_heads(ckv, BF16)
    vct = jnp.transpose(vc, (0, 1, 3, 2))
    bias_n = _bias_by_distance(rel_bias, T)
    n_qt, n_kt = T // ATT_TQ, T // ATT_TK
    n_ds = min(n_kt, -(-(REL_MAX_DIST + ATT_TK - 1) // ATT_TK) + 1)
    n_dw = min(n_kt, WINDOW // ATT_TK + 1)
    tzs, tzw, bias_tab = _bias_tables(bias_n, n_qt, nc // 8, n_ds, n_dw, ATT_TQ, ATT_TK)
    pool = jnp.asarray(_pool_matrix(nc, nb))
    o_cmp, sel = _cmp_select_prompt(q5, kc, vct, bias_tab, pool, nc - 1)
    o_sel, o_win = _sel_win_prompt(q5, ks, vst, kw, vwt, sel, tzs, tzw)
    return o_cmp, o_sel, o_win


def _nsa_sample(q, kvc, kvs, kvw, pool_cmp, pool_sel, win_buf, page_table, cmp_tab, rel_bias):
    B = q.shape[0]
    n_pages = page_table.shape[1]
    past_len = n_pages * PAGE_SIZE
    q_pos = past_len
    lp = -(-(past_len + 1) // SEL_BLOCK) * SEL_BLOCK
    n_cmp = lp // CMP_STRIDE - 1
    n_blk = lp // SEL_BLOCK
    n_past_chunks = past_len // CMP_STRIDE
    n_tail = 8
    assert n_past_chunks + n_tail >= n_cmp + 1
    n_chunks = n_past_chunks + n_tail
    feature_major = lambda pool: jnp.transpose(pool, (0, 2, 3, 4, 1))
    z_past = _compress_in_paged(feature_major(pool_cmp), page_table, cmp_tab)
    tail = jnp.pad(kvc[:, None, :], ((0, 0), (0, n_tail * CMP_STRIDE - 1), (0, 0)))
    z_tail = _compress_in(tail.reshape(B, n_tail, CMP_STRIDE * D_KV), cmp_tab)
    ncp = -(-n_chunks // LANE) * LANE
    nbp = -(-n_blk // LANE) * LANE
    ckv = _compress_out([z_past, z_tail], cmp_tab, ncp)
    bias_n = _bias_by_distance(rel_bias, q_pos + 1)
    n_back = max((n_pages + 1) * PAGE_SIZE, ncp * CMP_STRIDE + CMP_BLOCK)
    back = jnp.concatenate([bias_n[:, ::-1], jnp.broadcast_to(bias_n[:, :1], (N_HEADS, n_back - q_pos - 1))], 1)
    bias_c = back[:, CMP_BLOCK - 1:CMP_BLOCK - 1 + ncp * CMP_STRIDE:CMP_STRIDE]
    pool = jnp.asarray(_pool_matrix(ncp, nbp).T)
    q3 = q.reshape(B, N_HEADS, HEAD_DIM)
    o_cmp, idx = _cmp_select_step(q3, ckv, bias_c, pool, n_cmp, n_blk, q_pos)
    idx = idx[..., 0]
    bpp = PAGE_SIZE // SEL_BLOCK
    n_past = n_pages * bpp
    lpage = idx // bpp
    pages = jnp.take_along_axis(page_table, jnp.minimum(lpage, n_pages - 1).reshape(B, -1), axis=1)
    new_t = jnp.pad(kvs.reshape(B, 2, N_KV_HEADS, HEAD_DIM, 1), ((0, 0),) * 4 + ((0, PAGE_SIZE - 1),))
    bias_page = jnp.transpose(back[:, :(n_pages + 1) * PAGE_SIZE].reshape(N_HEADS, n_pages + 1, PAGE_SIZE),
                              (1, 0, 2))
    bias_sel = jnp.transpose(bias_page[lpage], (0, 1, 3, 2, 4)).reshape(B, N_KV_HEADS, N_HEADS, -1)
    kpos = lpage[..., None] * PAGE_SIZE + jnp.arange(PAGE_SIZE)
    ok = (kpos // SEL_BLOCK == idx[..., None]) & (idx <= q_pos // SEL_BLOCK)[..., None]
    kpos = jnp.where(ok, kpos, q_pos + 1).reshape(B, N_KV_HEADS, 1, -1).astype(jnp.int32)
    o_sel = _sel_step(q3, feature_major(pool_sel), new_t, bias_sel, kpos, pages.reshape(-1).astype(jnp.int32),
                      idx.reshape(-1).astype(jnp.int32), n_past, q_pos)
    o_sel = jnp.concatenate([o_sel[:, h, h * GQA:(h + 1) * GQA] for h in range(N_KV_HEADS)], axis=1)
    wb = win_buf.shape[1]
    bias_w = bias_n[:, 1:wb + 1][:, ::-1]
    o_win = _win_step(q3, feature_major(win_buf), kvw[:, None, :], bias_w, bias_n[:, 0:1])
    return o_cmp.reshape(B, D_ATT), o_sel.reshape(B, D_ATT), o_win.reshape(B, D_ATT)


def kernel(x_prompt, x_sample, cache_cmp_kv, cache_sel_kv, state_win_kv, state_ssm_re, state_ssm_im, page_table,
           c_prompt, c_sample, w_ada, b_ada, w_in, lam_re, lam_im, log_dt, b_re, b_im, c_re, c_im, d_skip,
           w_glu, b_glu, phi_pe, phi_w1, phi_b1, phi_w2, phi_b2, rel_bias, w_out, ln1_g, ln1_b,
           w_router, b_router, w_gate_up, b_gate_up, w_down, b_down, ln2_g, ln2_b):
    assert w_ada.shape[0] == DEPTH == 1
    l = 0
    Bp, T, D = x_prompt.shape
    Bs = x_sample.shape[0]
    kv_tail = (2, N_KV_HEADS, HEAD_DIM)

    n_c = Bp + Bs
    c_all = jnp.pad(jnp.concatenate([c_prompt, c_sample], 0), ((0, -n_c % 8), (0, 0)))
    m_all = _adaln(c_all, w_ada[l], b_ada[l])
    m_p = m_all[:Bp].reshape(Bp, 6, D)
    m_s = m_all[Bp:n_c].reshape(Bs, 6, D)
    mod_p = [m_p[:, i:i + 1, :] for i in range(6)]
    mod_s = [m_s[None, :, i, :] for i in range(6)]

    w_in_pad = jnp.pad(w_in[l], ((0, 0), (0, D_IN_PAD - D_IN))).astype(BF16)
    n_levels = max(1, int(math.log2(T // SSM_CHUNK)))
    ssm_tab = _ssm_tables(lam_re[l], lam_im[l], log_dt[l], b_re[l], b_im[l], c_re[l], c_im[l],
                          SSM_CHUNK, n_levels)
    cmp_tab = _compress_tables(phi_pe[l], phi_w1[l], phi_b1[l], phi_w2[l], phi_b2[l])
    w_post = dict(
        d_skip=d_skip[l].reshape(1, D_SSM), w_glu=w_glu[l].astype(BF16), b_glu=b_glu[l].reshape(1, D_SSM),
        gexp=jnp.asarray(_gate_expand_matrix(), dtype=BF16), w_out=w_out[l].astype(BF16),
        ln1_g=ln1_g[l].reshape(1, D), ln1_b=ln1_b[l].reshape(1, D),
        w_router=jnp.stack(_split_bf16(jnp.pad(w_router[l], ((0, 0), (0, LANE - N_EXPERTS))))),
        b_router=jnp.pad(b_router[l], (0, LANE - N_EXPERTS)).reshape(1, LANE))

    u, q5, kvc, kvs, kvw, g, ks, vst, kw, vwt = _mixer_in(x_prompt, mod_p[0], mod_p[1], w_in_pad, 512, True)
    y_ssm, h_p = _ssm_prompt(u, ssm_tab)
    o_cmp, o_sel, o_win = _nsa_prompt(q5, kvc, ks, vst, kw, vwt, cmp_tab, rel_bias)
    x1_p, hm_p, te_p, tw_p = _post_mixer(y_ssm, u, o_cmp, o_sel, o_win, g, x_prompt,
                                         mod_p[2], mod_p[3], mod_p[4], w_post, tm=512)

    u_s, q_s, kvc_s, kvs_s, kvw_s, g_s = _mixer_in(x_sample.reshape(1, Bs, D), mod_s[0], mod_s[1],
                                                   w_in_pad, Bs, False)
    y_s, h_s = _ssm_sample(u_s[0], state_ssm_re[l], state_ssm_im[l], ssm_tab, c_re[l], c_im[l])
    oc_s, os_s, ow_s = _nsa_sample(q_s[0].astype(F32), kvc_s[0], kvs_s[0], kvw_s[0], cache_cmp_kv[l],
                                   cache_sel_kv[l], state_win_kv[l], page_table, cmp_tab, rel_bias)
    x1_s, hm_s, te_s, tw_s = _post_mixer(y_s[None], u_s, oc_s[None], os_s[None], ow_s[None], g_s,
                                         x_sample.reshape(1, Bs, D), mod_s[2], mod_s[3], mod_s[4],
                                         w_post, tm=Bs)

    n_p = Bp * T
    n_all = n_p + Bs
    hm_all = jnp.concatenate([hm_p.reshape(n_p, D), hm_s.reshape(Bs, D)], 0)
    te_all = jnp.concatenate([te_p.reshape(n_p, LANE), te_s.reshape(Bs, LANE)], 0)[:, :TOP_K]
    row_tok, dest, items = _moe_dispatch(te_all, n_all)
    xb = jnp.concatenate([hm_all, jnp.zeros((1, D), F32)], 0)[row_tok]
    yb = _experts(xb, items, w_gate_up[l], b_gate_up[l], w_down[l], b_down[l])
    ys_p = [yb[dest[:n_p, k]].reshape(Bp, T, D) for k in range(TOP_K)]
    ys_s = [yb[dest[n_p:, k]].reshape(1, Bs, D) for k in range(TOP_K)]
    ln2g, ln2b = ln2_g[l].reshape(1, D), ln2_b[l].reshape(1, D)
    out_p = _final(x1_p, ys_p, tw_p, mod_p[5], ln2g, ln2b, tm=512)
    out_s = _final(x1_s, ys_s, tw_s, mod_s[5], ln2g, ln2b, tm=Bs)

    wlen = min(WINDOW, T)
    win_s = jnp.concatenate([state_win_kv[l], kvw_s[0].reshape(Bs, 1, *kv_tail)], 1)[:, -state_win_kv.shape[2]:]
    p_state = SSM_STATE
    return (out_p, out_s.reshape(Bs, 1, D),
            kvc.reshape(1, Bp, T, *kv_tail), kvc_s[0].reshape(1, Bs, 1, *kv_tail),
            kvs.reshape(1, Bp, T, *kv_tail), kvs_s[0].reshape(1, Bs, 1, *kv_tail),
            kvw[:, T - wlen:].reshape(1, Bp, wlen, *kv_tail), win_s[None],
            h_p[None, ..., :p_state], h_p[None, ..., p_state:],
            h_s[None, ..., :p_state], h_s[None, ..., p_state:])
```

```python
import functools
import math

import numpy as np
import jax
import jax.numpy as jnp
from jax import lax
from jax.experimental import pallas as pl
from jax.experimental.pallas import tpu as pltpu

DEPTH = 1
PAGE_SIZE = 128
D_SSM = 512
SSM_GROUP = 16
N_SSM_GROUPS = D_SSM // SSM_GROUP
SSM_STATE = 64
N_HEADS = 8
HEAD_DIM = 64
N_KV_HEADS = 2
GQA = N_HEADS // N_KV_HEADS
D_ATT = N_HEADS * HEAD_DIM
D_KV = 2 * N_KV_HEADS * HEAD_DIM
CMP_STRIDE = 16
CMP_BLOCK = 2 * CMP_STRIDE
SEL_BLOCK = 64
N_SEL = 16
WINDOW = 512
NUM_BUCKETS = 32
REL_MAX_DIST = 1024
N_EXPERTS = 32
TOP_K = 4
D_FF = 1024
SWIGLU_LIMIT = 7.0
SWIGLU_ALPHA = 1.702
DN_ALPHA = (2 * DEPTH) ** 0.25
D_IN = D_SSM + D_ATT + 3 * D_KV + 3 * N_HEADS
NEG = -1e30
F32 = jnp.float32
BF16 = jnp.bfloat16
HIGHEST = lax.Precision.HIGHEST

LANE = 128
D_IN_PAD = -(-D_IN // LANE) * LANE
SSM_CHUNK = 8
ATT_TQ = 128
ATT_TK = 128
SEL_CHAINS = 4
MOE_ROWS = 256
PAGES_PER_STEP = 32
PAGE_PARTS = 2
CHUNK_PITCH = 24
VMEM_LIMIT = 48 * 1024 * 1024
LN_EPS = 1e-5


def _cparams(*sem):
    return pltpu.CompilerParams(dimension_semantics=sem, vmem_limit_bytes=VMEM_LIMIT)


def _nt_dot(a, b):
    return lax.dot_general(a, b, (((1,), (1,)), ((), ())), preferred_element_type=F32)


def _layer_norm(x):
    mu = jnp.mean(x, axis=-1, keepdims=True)
    xc = x - mu
    var = jnp.mean(xc * xc, axis=-1, keepdims=True)
    return xc * lax.rsqrt(var + LN_EPS)


def _adaln_kernel(c_ref, w_ref, b_ref, o_ref):
    c = c_ref[...]
    s = c * jax.nn.sigmoid(c)
    o_ref[...] = jnp.dot(s, w_ref[...], precision=HIGHEST, preferred_element_type=F32) + b_ref[...]


def _adaln(c, w, b):
    n, d = c.shape
    dout = w.shape[1]
    tn = 1024
    return pl.pallas_call(
        _adaln_kernel,
        out_shape=jax.ShapeDtypeStruct((n, dout), F32),
        grid=(dout // tn,),
        in_specs=[pl.BlockSpec((n, d), lambda j: (0, 0)),
                  pl.BlockSpec((d, tn), lambda j: (0, j)),
                  pl.BlockSpec((1, tn), lambda j: (0, j))],
        out_specs=pl.BlockSpec((n, tn), lambda j: (0, j)),
        compiler_params=_cparams("arbitrary"),
        name="adaln",
    )(c, w, b.reshape(1, dout))


def _mixer_in_kernel(x_ref, sh_ref, sc_ref, w_ref, u_ref, q_ref, kvc_ref, kvs_ref, kvw_ref, g_ref, *att_refs):
    h = _layer_norm(x_ref[0]) * (1.0 + sc_ref[0]) + sh_ref[0]
    z = jnp.dot(h.astype(BF16), w_ref[...], preferred_element_type=F32)
    c0 = D_SSM
    c1 = c0 + D_ATT
    c2 = c1 + D_KV
    c3 = c2 + D_KV
    c4 = c3 + D_KV
    u_ref[0] = z[:, :c0]
    kvc_ref[0] = z[:, c1:c2]
    kvs_ref[0] = z[:, c2:c3]
    kvw_ref[0] = z[:, c3:c4]
    g_ref[0] = z[:, c4:c4 + LANE]
    if not att_refs:
        q_ref[0] = z[:, c0:c1].astype(BF16)
        return
    ks_ref, vst_ref, kw_ref, vwt_ref = att_refs
    hd, half = HEAD_DIM, N_KV_HEADS * HEAD_DIM
    for hq in range(N_HEADS):
        q_ref[0, hq // GQA, hq % GQA] = (z[:, c0 + hq * hd:c0 + (hq + 1) * hd] * (hd ** -0.5)).astype(BF16)
    for k_ref, vt_ref, base in ((ks_ref, vst_ref, c2), (kw_ref, vwt_ref, c3)):
        for hk in range(N_KV_HEADS):
            k_ref[0, hk] = z[:, base + hk * hd:base + (hk + 1) * hd].astype(BF16)
        vt = z[:, base + half:base + 2 * half].T
        vt_ref[0] = vt.reshape(N_KV_HEADS, hd, vt.shape[1]).astype(BF16)


def _mixer_in(x, shift, scale, w_pad, tm, attention_layouts):
    B, T, D = x.shape
    R = shift.shape[1]
    rb = 1 if R == 1 else tm
    mod_map = (lambda b, i: (b, 0, 0)) if R == 1 else (lambda b, i: (b, i, 0))
    row = lambda n: pl.BlockSpec((1, tm, n), lambda b, i: (b, i, 0))
    f32 = lambda n: jax.ShapeDtypeStruct((B, T, n), F32)
    if attention_layouts:
        q_shape = jax.ShapeDtypeStruct((B, N_KV_HEADS, GQA, T, HEAD_DIM), BF16)
        q_spec = pl.BlockSpec((1, N_KV_HEADS, GQA, tm, HEAD_DIM), lambda b, i: (b, 0, 0, i, 0))
        k_shape = jax.ShapeDtypeStruct((B, N_KV_HEADS, T, HEAD_DIM), BF16)
        k_spec = pl.BlockSpec((1, N_KV_HEADS, tm, HEAD_DIM), lambda b, i: (b, 0, i, 0))
        vt_shape = jax.ShapeDtypeStruct((B, N_KV_HEADS, HEAD_DIM, T), BF16)
        vt_spec = pl.BlockSpec((1, N_KV_HEADS, HEAD_DIM, tm), lambda b, i: (b, 0, 0, i))
        extra_shapes, extra_specs = (k_shape, vt_shape, k_shape, vt_shape), (k_spec, vt_spec, k_spec, vt_spec)
    else:
        q_shape, q_spec = jax.ShapeDtypeStruct((B, T, D_ATT), BF16), row(D_ATT)
        extra_shapes, extra_specs = (), ()
    return pl.pallas_call(
        _mixer_in_kernel,
        out_shape=(f32(D_SSM), q_shape, f32(D_KV), f32(D_KV), f32(D_KV), f32(LANE)) + extra_shapes,
        grid=(B, T // tm),
        in_specs=[row(D), pl.BlockSpec((1, rb, D), mod_map), pl.BlockSpec((1, rb, D), mod_map),
                  pl.BlockSpec((D, D_IN_PAD), lambda b, i: (0, 0))],
        out_specs=(row(D_SSM), q_spec, row(D_KV), row(D_KV), row(D_KV), row(LANE)) + extra_specs,
        compiler_params=_cparams("parallel", "parallel"),
        name="mixer_in",
    )(x, shift, scale, w_pad)


def _ssm_tables(lam_re, lam_im, log_dt, b_re, b_im, c_re, c_im, L, n_levels):
    G, P = lam_re.shape
    C = b_re.shape[-1]
    dt = jnp.exp(log_dt.astype(F32))[:, None]
    er, ei = lam_re * dt, lam_im * dt

    def power(k):
        kk = k.astype(F32)[:, None, None]
        mag = jnp.exp(kk * er)
        return mag * jnp.cos(kk * ei), mag * jnp.sin(kk * ei)

    lb_re, lb_im = power(jnp.ones((1,), F32))
    nr, ni = lb_re[0] - 1.0, lb_im[0]
    den = lam_re * lam_re + lam_im * lam_im
    fr = (nr * lam_re + ni * lam_im) / den
    fi = (ni * lam_re - nr * lam_im) / den
    bbr = fr[:, :, None] * b_re - fi[:, :, None] * b_im
    bbi = fr[:, :, None] * b_im + fi[:, :, None] * b_re
    pr, pi = power(jnp.arange(L + 1))
    clr = c_re[None] * pr[:, :, None, :] - c_im[None] * pi[:, :, None, :]
    cli = c_re[None] * pi[:, :, None, :] + c_im[None] * pr[:, :, None, :]
    kern = (jnp.einsum('kgcp,gpd->kgcd', clr[:L], bbr, precision=HIGHEST)
            - jnp.einsum('kgcp,gpd->kgcd', cli[:L], bbi, precision=HIGHEST))
    GP = LANE // C
    X = G // GP
    eye = jnp.eye(GP, dtype=BF16)
    place_einsum = functools.partial(jnp.einsum, preferred_element_type=BF16)
    kblk = place_einsum('kxhcd,hj->xkhdjc', kern.astype(BF16).reshape(L, X, GP, C, C), eye)
    kblk = kblk.reshape(X, L, LANE, LANE)
    prr, pir = pr[:L][::-1], pi[:L][::-1]
    ws2 = jnp.stack([prr[..., None] * bbr[None] - pir[..., None] * bbi[None],
                     prr[..., None] * bbi[None] + pir[..., None] * bbr[None]])
    ws = place_einsum('rsxhpd,hj->xshdrjp', ws2.astype(BF16).reshape(2, L, X, GP, P, C), eye)
    ws = ws.reshape(X, L * LANE, 2 * GP * P)
    wy2 = jnp.stack([clr[1:], -cli[1:]])
    wy = place_einsum('rtxhcp,hj->xrhptjc', wy2.astype(BF16).reshape(2, L, X, GP, C, P), eye)
    wy = wy.reshape(X, 2 * GP * P, L * LANE)
    lr, li = power(L * (2 ** jnp.arange(n_levels)))
    lr, li = lr.reshape(n_levels, X, GP * P), li.reshape(n_levels, X, GP * P)
    ar = jnp.transpose(jnp.concatenate([lr, lr], -1), (1, 0, 2))
    ai = jnp.transpose(jnp.concatenate([-li, li], -1), (1, 0, 2))
    return kblk, ws, wy, ar, ai, (lb_re[0], lb_im[0], bbr, bbi)


def _ssm_kernel(u_ref, kblk_ref, ws_ref, wy_ref, ar_ref, ai_ref, y_ref, hl_ref, toep_ref, *, L, nc, n_levels):
    for s in range(L):
        for t in range(L):
            blk = kblk_ref[0, t - s] if t >= s else jnp.zeros((LANE, LANE), BF16)
            toep_ref[s * LANE:(s + 1) * LANE, t * LANE:(t + 1) * LANE] = blk
    u = jnp.concatenate([u_ref[0, pl.ds(t, nc, stride=L), :] for t in range(L)], axis=1).astype(BF16)
    y1 = jnp.dot(u, toep_ref[...], preferred_element_type=F32)
    h = jnp.dot(u, ws_ref[0], preferred_element_type=F32)
    w2 = h.shape[-1]
    rows = lax.broadcasted_iota(jnp.int32, (nc, w2), 0)
    for k in range(n_levels):
        d = 1 << k
        sh = jnp.where(rows >= d, pltpu.roll(h, d, axis=0), 0.0)
        sw = pltpu.roll(sh, w2 // 2, axis=1)
        h = h + ar_ref[0, k:k + 1, :] * sh + ai_ref[0, k:k + 1, :] * sw
    hl_ref[0, 0] = h[nc - 1:nc, :]
    hp = jnp.where(rows >= 1, pltpu.roll(h, 1, axis=0), 0.0)
    y = y1 + jnp.dot(hp.astype(BF16), wy_ref[0], preferred_element_type=F32)
    for t in range(L):
        y_ref[0, pl.ds(t, nc, stride=L), :] = y[:, t * LANE:(t + 1) * LANE]


def _ssm_prompt(u, tables):
    kblk, ws, wy, ar, ai, _ = tables
    B, T, _ = u.shape
    L, P = SSM_CHUNK, SSM_STATE
    X, n_levels, w2 = ar.shape
    GP = w2 // (2 * P)
    nc = T // L
    tab = lambda a: pl.BlockSpec((1,) + a.shape[1:], lambda x, b: (x,) + (0,) * (a.ndim - 1))
    seq = pl.BlockSpec((1, T, LANE), lambda x, b: (b, 0, x))
    y, hl = pl.pallas_call(
        functools.partial(_ssm_kernel, L=L, nc=nc, n_levels=n_levels),
        out_shape=(jax.ShapeDtypeStruct((B, T, D_SSM), F32), jax.ShapeDtypeStruct((X, B, 1, w2), F32)),
        grid=(X, B),
        in_specs=[seq, tab(kblk), tab(ws), tab(wy), tab(ar), tab(ai)],
        out_specs=(seq, pl.BlockSpec((1, 1, 1, w2), lambda x, b: (x, b, 0, 0))),
        scratch_shapes=[pltpu.VMEM((L * LANE, L * LANE), BF16)],
        compiler_params=_cparams("parallel", "parallel"),
        name="ssm_prompt",
    )(u, kblk, ws, wy, ar, ai)
    hl = jnp.transpose(hl.reshape(X, B, 2, GP, P), (1, 0, 3, 2, 4))
    return y, hl.reshape(B, X * GP, 2 * P)


def _ssm_step_kernel(u_ref, h0_ref, bb_ref, lr_ref, li_ref, cy_ref, y_ref, h_ref):
    p = lr_ref.shape[-1] // 2
    bu = jnp.einsum('gbc,gcp->gbp', u_ref[...], bb_ref[...], preferred_element_type=F32)
    h0 = h0_ref[...]
    h0s = jnp.concatenate([h0[..., p:], h0[..., :p]], axis=-1)
    h = lr_ref[...] * h0 + li_ref[...] * h0s + bu
    h_ref[...] = h
    y_ref[...] = jnp.einsum('gbp,gpc->gbc', h.astype(BF16), cy_ref[...], preferred_element_type=F32)


def _ssm_sample(u, h0_re, h0_im, tables, c_re, c_im):
    lb_re, lb_im, bbr, bbi = tables[-1]
    B = u.shape[0]
    G, C, P = N_SSM_GROUPS, SSM_GROUP, SSM_STATE
    ug = jnp.transpose(u.reshape(B, G, C), (1, 0, 2)).astype(BF16)
    h0 = jnp.transpose(jnp.concatenate([h0_re, h0_im], -1), (1, 0, 2)).astype(F32)
    bb = jnp.concatenate([jnp.transpose(bbr, (0, 2, 1)), jnp.transpose(bbi, (0, 2, 1))], -1).astype(BF16)
    lr = jnp.concatenate([lb_re, lb_re], -1)[:, None, :]
    li = jnp.concatenate([-lb_im, lb_im], -1)[:, None, :]
    cy = jnp.concatenate([jnp.transpose(c_re, (0, 2, 1)), -jnp.transpose(c_im, (0, 2, 1))], 1).astype(BF16)
    y, h = pl.pallas_call(
        _ssm_step_kernel,
        out_shape=(jax.ShapeDtypeStruct((G, B, C), F32), jax.ShapeDtypeStruct((G, B, 2 * P), F32)),
        name="ssm_step",
    )(ug, h0, bb, lr, li, cy)
    return jnp.transpose(y, (1, 0, 2)).reshape(B, D_SSM), jnp.transpose(h, (1, 0, 2))


def _compress_tables(phi_pe, phi_w1, phi_b1, phi_w2, phi_b2):
    S, H, Dh = CMP_STRIDE, N_KV_HEADS, HEAD_DIM
    w1 = phi_w1.reshape(2, 2, S, Dh, Dh)
    eye_c = jnp.eye(2, dtype=F32)
    eye_h = jnp.eye(H, dtype=F32)
    wbig = jnp.einsum('cajde,xc,yh->jxydache', w1, eye_c, eye_h).reshape(S * 2 * H * Dh, 2 * 2 * H * Dh)
    pe = jnp.transpose(phi_pe.reshape(2, 2, S, Dh), (1, 2, 0, 3))
    pe_rows = jnp.broadcast_to(pe[:, :, :, None, :], (2, S, 2, H, Dh)).reshape(2, S * 2 * H * Dh)
    n = 2 * H * Dh
    pe_w = (jnp.dot(pe_rows[0], wbig[:, :n], precision=HIGHEST) + jnp.dot(pe_rows[1], wbig[:, n:], precision=HIGHEST))
    b1 = jnp.broadcast_to(phi_b1[:, None, :], (2, H, Dh)).reshape(1, n) + pe_w[None, :]
    w2 = jnp.einsum('cef,cx,hy->chexyf', phi_w2, eye_c, eye_h).reshape(n, n)
    b2 = jnp.broadcast_to(phi_b2[:, None, :], (2, H, Dh)).reshape(1, n)
    return wbig.astype(BF16), b1, w2.astype(BF16), b2


def _compress_in_kernel(x_ref, w_ref, z_ref):
    z_ref[0] = jnp.dot(x_ref[0].astype(BF16), w_ref[...], preferred_element_type=F32)


def _compress_in(x2, tables):
    wbig = tables[0]
    N2 = wbig.shape[1]
    B, n, K = x2.shape
    tr = math.gcd(n, 256)
    return pl.pallas_call(
        _compress_in_kernel,
        out_shape=jax.ShapeDtypeStruct((B, n, N2), F32),
        grid=(B, n // tr),
        in_specs=[pl.BlockSpec((1, tr, K), lambda b, i: (b, i, 0)),
                  pl.BlockSpec((K, N2), lambda b, i: (0, 0))],
        out_specs=pl.BlockSpec((1, tr, N2), lambda b, i: (b, i, 0)),
        compiler_params=_cparams("parallel", "parallel"),
        name="compress_in",
    )(x2, wbig)


def _compress_in_paged_kernel(pt_ref, *refs, n_pg):
    x_refs = refs[:n_pg]
    w_ref, z_ref = refs[n_pg:n_pg + 2]
    scratch = refs[n_pg + 2:]
    n_slab = D_KV // LANE
    pg_part = n_pg // PAGE_PARTS
    cpp = PAGE_SIZE // CMP_STRIDE
    rows = pg_part * cpp
    for part in range(PAGE_PARTS):
        s_refs = scratch[part * n_slab:(part + 1) * n_slab]
        for k in range(pg_part):
            t = x_refs[part * pg_part + k][0].reshape(D_KV, PAGE_SIZE).T
            for c, s_ref in enumerate(s_refs):
                for n in range(cpp):
                    r0 = (k * cpp + n) * CHUNK_PITCH
                    s_ref[r0:r0 + CMP_STRIDE, :] = t[n * CMP_STRIDE:(n + 1) * CMP_STRIDE, c * LANE:(c + 1) * LANE]
        z = jnp.zeros((rows, w_ref.shape[1]), F32)
        for j in range(CMP_STRIDE):
            xj = jnp.concatenate([s_ref[pl.ds(j, rows, stride=CHUNK_PITCH), :] for s_ref in s_refs], axis=1)
            z = z + jnp.dot(xj.astype(BF16), w_ref[j * D_KV:(j + 1) * D_KV, :], preferred_element_type=F32)
        z_ref[0, part * rows:(part + 1) * rows, :] = z


def _compress_in_paged(pool_t, page_table, tables):
    wbig = tables[0]
    N2 = wbig.shape[1]
    K = wbig.shape[0]
    B, n_pages = page_table.shape
    n_pg = math.gcd(n_pages, PAGES_PER_STEP)
    rows = n_pg * PAGE_SIZE // CMP_STRIDE
    page_spec = lambda k: pl.BlockSpec((1,) + pool_t.shape[1:],
                                       lambda b, i, pt, k=k: (pt[b, i * n_pg + k], 0, 0, 0, 0))
    grid_spec = pltpu.PrefetchScalarGridSpec(
        num_scalar_prefetch=1,
        grid=(B, n_pages // n_pg),
        in_specs=[page_spec(k) for k in range(n_pg)] + [pl.BlockSpec((K, N2), lambda b, i, pt: (0, 0))],
        out_specs=pl.BlockSpec((1, rows, N2), lambda b, i, pt: (b, i, 0)),
        scratch_shapes=[pltpu.VMEM((rows // PAGE_PARTS * CHUNK_PITCH, LANE), F32)
                        for _ in range(PAGE_PARTS * (D_KV // LANE))],
    )
    return pl.pallas_call(
        functools.partial(_compress_in_paged_kernel, n_pg=n_pg),
        out_shape=jax.ShapeDtypeStruct((B, n_pages * PAGE_SIZE // CMP_STRIDE, N2), F32),
        grid_spec=grid_spec,
        compiler_params=_cparams("arbitrary", "arbitrary"),
        name="compress_in_paged",
    )(page_table, *([pool_t] * n_pg), wbig)


def _compress_out_kernel(*refs):
    z_refs, (b1_ref, w2_ref, b2_ref, o_ref) = refs[:-4], refs[-4:]
    z = jnp.concatenate([z_ref[0] for z_ref in z_refs], axis=0)
    n = z.shape[-1] // 2
    rows = z.shape[0]
    second = pltpu.roll(z[:, n:], rows - 1, axis=0)
    hdn = jax.nn.gelu(z[:, :n] + second + b1_ref[...])
    o_ref[0, :rows, :] = jnp.dot(hdn.astype(BF16), w2_ref[...], preferred_element_type=F32) + b2_ref[...]
    if o_ref.shape[1] > rows:
        o_ref[0, rows:, :] = jnp.zeros((o_ref.shape[1] - rows, n), F32)


def _compress_out(zs, tables, n_out):
    _, b1, w2, b2 = tables
    B, _, N2 = zs[0].shape
    return pl.pallas_call(
        _compress_out_kernel,
        out_shape=jax.ShapeDtypeStruct((B, n_out, N2 // 2), F32),
        grid=(B,),
        in_specs=[pl.BlockSpec((1, z.shape[1], N2), lambda b: (b, 0, 0)) for z in zs] + [
                  pl.BlockSpec((1, N2 // 2), lambda b: (0, 0)),
                  pl.BlockSpec((N2 // 2, N2 // 2), lambda b: (0, 0)),
                  pl.BlockSpec((1, N2 // 2), lambda b: (0, 0))],
        out_specs=pl.BlockSpec((1, n_out, N2 // 2), lambda b: (b, 0, 0)),
        compiler_params=_cparams("parallel"),
        name="compress_out",
    )(*zs, b1, w2, b2)


def _rel_bucket(dist):
    n = jnp.maximum(dist, 0)
    max_exact = NUM_BUCKETS // 2
    nf = jnp.maximum(n, 1).astype(F32)
    large = max_exact + (jnp.log(nf / max_exact) / math.log(REL_MAX_DIST / max_exact)
                         * (NUM_BUCKETS - max_exact)).astype(jnp.int32)
    large = jnp.minimum(large, NUM_BUCKETS - 1)
    return jnp.where(n < max_exact, n, large)


def _bias_by_distance(rel_bias, n_max):
    onehot = (_rel_bucket(jnp.arange(n_max))[None, :] == jnp.arange(NUM_BUCKETS)[:, None]).astype(F32)
    return jnp.dot(jnp.transpose(rel_bias.astype(F32)), onehot, precision=HIGHEST)


def _shifted_chunks(bias_n, pad, n_chunks, width):
    n = min(bias_n.shape[1], n_chunks * width - pad)
    ext = jnp.concatenate([jnp.broadcast_to(bias_n[:, :1], (N_HEADS, pad)), bias_n[:, :n],
                           jnp.zeros((N_HEADS, n_chunks * width - pad - n), F32)], axis=1)
    return ext.reshape(N_HEADS, n_chunks, width)


def _bias_tables_kernel(ed_ref, ec_ref, tzs_ref, tzw_ref, cmp_ref, *, tq, tk, n_qt):
    n_ds, n_dw, n_j = tzs_ref.shape[1] - 1, tzw_ref.shape[1] - 1, cmp_ref.shape[1] // 8
    tzs_ref[0, n_ds] = jnp.full((tk, tq), NEG, F32)
    tzw_ref[0, n_dw] = jnp.full((tk, tq), NEG, F32)
    w = tq + tk
    c = lax.broadcasted_iota(jnp.int32, (tk, tq), 0)
    r = lax.broadcasted_iota(jnp.int32, (tk, tq), 1)
    for d in range(n_ds):
        v = jnp.concatenate([ed_ref[0, d:d + 1, :], ed_ref[0, d + 1:d + 2, :]], axis=1)
        t = pltpu.roll(jnp.broadcast_to(v, (tk, w)), w - (tk - 1), axis=1, stride=1, stride_axis=0)[:, :tq]
        dist = d * tk + r - c
        tzs_ref[0, d] = jnp.where(dist >= 0, t, NEG)
        if d < n_dw:
            tzw_ref[0, d] = jnp.where((dist >= 0) & (dist <= WINDOW), t, NEG)
    for j in range(n_j):
        dd = n_qt - 1 - j
        c0, c1 = max(dd, 0), max(dd + 1, 0)
        v = jnp.concatenate([ec_ref[0, c0:c0 + 1, :], ec_ref[0, c1:c1 + 1, :]], axis=1)
        t = pltpu.roll(jnp.broadcast_to(v, (8, w)), w - 7 * CMP_STRIDE, axis=1, stride=CMP_STRIDE, stride_axis=0)
        dist = tq * dd + r[:8] - CMP_STRIDE * c[:8] - (CMP_BLOCK - 1)
        cmp_ref[0, j * 8:(j + 1) * 8, :] = jnp.where(dist >= 0, t[:, :tq], NEG)


def _bias_tables(bias_n, n_qt, n_rb, n_ds, n_dw, tq, tk):
    assert tq == tk == 8 * CMP_STRIDE and n_dw <= n_ds
    n_j = n_rb + n_qt - 1
    ed = _shifted_chunks(bias_n, tk - 1, n_ds + 1, tq)
    ec = _shifted_chunks(bias_n, 7 * CMP_STRIDE + CMP_BLOCK - 1, n_qt + 1, tq)
    head = lambda a: pl.BlockSpec((1,) + a.shape[1:], lambda h: (h,) + (0,) * (a.ndim - 1))
    outs = (jax.ShapeDtypeStruct((N_HEADS, n_ds + 1, tk, tq), F32),
            jax.ShapeDtypeStruct((N_HEADS, n_dw + 1, tk, tq), F32),
            jax.ShapeDtypeStruct((N_HEADS, n_j * 8, tq), F32))
    tzs, tzw, cmp = pl.pallas_call(
        functools.partial(_bias_tables_kernel, tq=tq, tk=tk, n_qt=n_qt),
        out_shape=outs,
        grid=(N_HEADS,),
        in_specs=[head(ed), head(ec)],
        out_specs=tuple(head(o) for o in outs),
        compiler_params=_cparams("parallel"),
        name="bias_tables",
    )(ed, ec)
    grp = lambda a: a.reshape((N_KV_HEADS, GQA) + a.shape[1:])
    return grp(tzs), grp(tzw), cmp


def _pool_matrix(n_cmp_pad, n_blk_pad):
    r = SEL_BLOCK // CMP_STRIDE
    i = np.arange(n_cmp_pad)[None, :]
    j = np.arange(n_blk_pad)[:, None]
    return ((i >= r * j - 1) & (i <= r * j + r - 1)).astype(np.float32)


def _cmp_select_kernel(q_ref, k_ref, vt_ref, bias_ref, pool_ref, o_ref, sel_ref, *, tq):
    qt = pl.program_id(2)
    n_qt = pl.num_programs(2)
    q = q_ref[0, 0].reshape(GQA * tq, HEAD_DIM)
    k = k_ref[0, 0]
    nc = k.shape[0]
    s = _nt_dot(k, q)
    row0 = pl.multiple_of((n_qt - 1 - qt) * 8, 8)
    s = s + jnp.concatenate([bias_ref[g, pl.ds(row0, nc), :] for g in range(GQA)], axis=-1)
    m = jnp.maximum(jnp.max(s, axis=0, keepdims=True), 0.5 * NEG)
    p = jnp.exp(s - m)
    p = p * (1.0 / jnp.maximum(jnp.sum(p, axis=0, keepdims=True), 1e-30))
    ot = jnp.dot(vt_ref[0, 0], p.astype(BF16), preferred_element_type=F32)
    o_ref[0] = jnp.concatenate([ot[:, g * tq:(g + 1) * tq].T for g in range(GQA)], axis=-1)
    imp = p[:, 0:tq]
    for g in range(1, GQA):
        imp = imp + p[:, g * tq:(g + 1) * tq]
    sb = jnp.dot(pool_ref[...], imp, precision=HIGHEST, preferred_element_type=F32)
    nb = sb.shape[0]
    blk = lax.broadcasted_iota(jnp.int32, (nb, tq), 0)
    cur = (qt * tq + lax.broadcasted_iota(jnp.int32, (nb, tq), 1)) // SEL_BLOCK
    causal = blk <= cur
    forced = (blk == 0) | (blk == cur) | (blk == cur - 1)
    sc = jnp.where(forced & causal, 1e4, jnp.where(causal, sb, -1.0))
    groups = [sc[r:r + 8] for r in range(0, nb, 8)]
    sub = lax.broadcasted_iota(jnp.int32, (8, tq), 0)
    ranks = [jnp.zeros((8, tq), F32) for _ in groups]
    for i in range(nb):
        row = sc[i:i + 1, :]
        for gi, grp in enumerate(groups):
            if gi * 8 > i:
                ahead = row >= grp
            elif gi * 8 + 7 < i:
                ahead = row > grp
            else:
                ahead = (row > grp) | ((row == grp) & (sub > i - gi * 8))
            ranks[gi] = ranks[gi] + jnp.where(ahead, 1.0, 0.0)
    rank = jnp.concatenate(ranks, axis=0)
    sel_ref[0, 0] = jnp.where((rank < N_SEL) & causal, 0.0, NEG)


def _cmp_select_prompt(q5, kc, vct, bias_tab, pool):
    B, _, _, T, _ = q5.shape
    NC = kc.shape[2]
    NB = pool.shape[0]
    R = bias_tab.shape[1]
    tq = ATT_TQ
    return pl.pallas_call(
        functools.partial(_cmp_select_kernel, tq=tq),
        out_shape=(jax.ShapeDtypeStruct((B, T, D_ATT), F32),
                   jax.ShapeDtypeStruct((B, N_KV_HEADS, NB, T), F32)),
        grid=(B, N_KV_HEADS, T // tq),
        in_specs=[pl.BlockSpec((1, 1, GQA, tq, HEAD_DIM), lambda b, h, i: (b, h, 0, i, 0)),
                  pl.BlockSpec((1, 1, NC, HEAD_DIM), lambda b, h, i: (b, h, 0, 0)),
                  pl.BlockSpec((1, 1, HEAD_DIM, NC), lambda b, h, i: (b, h, 0, 0)),
                  pl.BlockSpec((GQA, R, tq), lambda b, h, i: (h, 0, 0)),
                  pl.BlockSpec((NB, NC), lambda b, h, i: (0, 0))],
        out_specs=(pl.BlockSpec((1, tq, GQA * HEAD_DIM), lambda b, h, i: (b, i, h)),
                   pl.BlockSpec((1, 1, NB, tq), lambda b, h, i: (b, h, 0, i))),
        compiler_params=_cparams("parallel", "parallel", "parallel"),
        name="cmp_select_prompt",
    )(q5, kc, vct, bias_tab, pool)


def _sel_win_kernel(q_ref, ks_ref, vst_ref, kw_ref, vwt_ref, sel_ref, tzs_ref, tzw_ref, os_ref, ow_ref, *, tq):
    tk = ATT_TK
    qt = pl.program_id(2)
    q = q_ref[0, 0].reshape(GQA * tq, HEAD_DIM)
    width = GQA * tq
    per_tile = tk // SEL_BLOCK

    def make_sweep(k_ref, vt_ref, tz_ref, use_sel, n_chains, single_trip):
        n_d = tz_ref.shape[2] - 1

        def scores(kt, hi):
            pad = kt > hi
            kt = jnp.minimum(kt, hi)
            off = pl.multiple_of(kt * tk, tk)
            k = k_ref[0, 0, pl.ds(off, tk), :]
            d = jnp.where(pad, n_d, jnp.minimum(qt - kt, n_d - 1))
            bias = [tz_ref[0, g, d] for g in range(GQA)]
            if use_sel:
                rows = sel_ref[0, 0, pl.ds(kt * per_tile, per_tile), :]
                selb = jnp.concatenate([jnp.broadcast_to(rows[i:i + 1], (SEL_BLOCK, tq))
                                        for i in range(per_tile)], axis=0)
                bias = [b + selb for b in bias]
            return _nt_dot(k, q) + jnp.concatenate(bias, axis=1)

        def values_t(kt, lo, hi):
            off = pl.multiple_of(jnp.clip(kt, lo, hi) * tk, tk)
            return vt_ref[0, 0, :, pl.ds(off, tk)]

        def sweep(lo, hi):
            n_trips = (hi - lo + n_chains) // n_chains
            chain0 = (jnp.full((1, width), 0.5 * NEG, F32), jnp.zeros((1, width), F32),
                      jnp.zeros((HEAD_DIM, width), F32), jnp.ones((1, width), F32), jnp.zeros((tk, width), BF16))

            def trip(i, chains):
                kt = lo + n_chains * i
                pv = [jnp.dot(values_t(kt - n_chains + c, lo, hi), chains[c][4], preferred_element_type=F32)
                      for c in range(n_chains)]
                ss = [scores(kt + c, hi) for c in range(n_chains)]
                out = []
                for c in range(n_chains):
                    m, l, acc, alpha_prev, _ = chains[c]
                    m_new = jnp.maximum(m, jnp.max(ss[c], axis=0, keepdims=True))
                    alpha = jnp.exp(m - m_new)
                    p = jnp.exp(ss[c] - m_new)
                    l = alpha * l + jnp.sum(p, axis=0, keepdims=True)
                    out.append((m_new, l, alpha_prev * acc + pv[c], alpha, p.astype(BF16)))
                return tuple(out)

            if single_trip:
                done = []
                for c in range(n_chains):
                    s = scores(lo + c, hi)
                    m = jnp.maximum(jnp.max(s, axis=0, keepdims=True), 0.5 * NEG)
                    p = jnp.exp(s - m)
                    done.append((m, jnp.sum(p, axis=0, keepdims=True),
                                 jnp.dot(values_t(lo + c, lo, hi), p.astype(BF16), preferred_element_type=F32)))
            else:
                chains = lax.fori_loop(0, n_trips, trip, (chain0,) * n_chains)
                kt_last = lo + n_chains * (n_trips - 1)
                done = []
                for c in range(n_chains):
                    m, l, acc, alpha, p = chains[c]
                    done.append((m, l, alpha * acc + jnp.dot(values_t(kt_last + c, lo, hi), p,
                                                              preferred_element_type=F32)))
            m_all = functools.reduce(jnp.maximum, [m for m, _, _ in done])
            num = den = 0.0
            for m, l, acc in done:
                e = jnp.exp(m - m_all)
                num = num + acc * e
                den = den + l * e
            o = num / jnp.maximum(den, 1e-30)
            return jnp.concatenate([o[:, g * tq:(g + 1) * tq].T for g in range(GQA)], axis=-1)
        return sweep

    n_win = tzw_ref.shape[2] - 1
    os_ref[0] = make_sweep(ks_ref, vst_ref, tzs_ref, True, SEL_CHAINS, False)(0, qt)
    ow_ref[0] = make_sweep(kw_ref, vwt_ref, tzw_ref, False, n_win, True)(jnp.maximum(qt - (n_win - 1), 0), qt)


def _sel_win_prompt(q5, ks, vst, kw, vwt, sel, tzs, tzw):
    B, _, _, T, _ = q5.shape
    NB = sel.shape[2]
    tq = ATT_TQ
    k_spec = pl.BlockSpec((1, 1, T, HEAD_DIM), lambda b, h, i: (b, h, 0, 0))
    vt_spec = pl.BlockSpec((1, 1, HEAD_DIM, T), lambda b, h, i: (b, h, 0, 0))
    tz_spec = lambda tz: pl.BlockSpec((1,) + tz.shape[1:], lambda b, h, i: (h, 0, 0, 0, 0))
    o_spec = pl.BlockSpec((1, tq, GQA * HEAD_DIM), lambda b, h, i: (b, i, h))
    return pl.pallas_call(
        functools.partial(_sel_win_kernel, tq=tq),
        out_shape=(jax.ShapeDtypeStruct((B, T, D_ATT), F32), jax.ShapeDtypeStruct((B, T, D_ATT), F32)),
        grid=(B, N_KV_HEADS, T // tq),
        in_specs=[pl.BlockSpec((1, 1, GQA, tq, HEAD_DIM), lambda b, h, i: (b, h, 0, i, 0)),
                  k_spec, vt_spec, k_spec, vt_spec,
                  pl.BlockSpec((1, 1, NB, tq), lambda b, h, i: (b, h, 0, i)),
                  tz_spec(tzs), tz_spec(tzw)],
        out_specs=(o_spec, o_spec),
        compiler_params=_cparams("parallel", "parallel", "parallel"),
        name="sel_win_prompt",
    )(q5, ks, vst, kw, vwt, sel, tzs, tzw)


def _gate_expand_matrix():
    m = np.zeros((3, 2 * LANE, D_ATT), np.float32)
    for r in range(3):
        for h in range(N_HEADS):
            m[r, h * 3 + r, h * HEAD_DIM:(h + 1) * HEAD_DIM] = 1.0
            m[r, LANE + h * 3 + r, h * HEAD_DIM:(h + 1) * HEAD_DIM] = 1.0
    return m


def _split_bf16(x):
    hi = x.astype(BF16)
    return hi, (x - hi.astype(F32)).astype(BF16)


def _post_mixer_kernel(y_ref, u_ref, oc_ref, os_ref, ow_ref, g_ref, x_ref, gate_ref, sh_ref, sc_ref,
                       dskip_ref, wglu_ref, bglu_ref, gexp_ref, wout_ref, lng_ref, lnb_ref,
                       wr_ref, br_ref, x1_ref, hm_ref, te_ref, tw_ref):
    y = y_ref[0] + dskip_ref[...] * u_ref[0]
    gl = jax.nn.gelu(y)
    ssm = gl * jax.nn.sigmoid(jnp.dot(gl.astype(BF16), wglu_ref[...], preferred_element_type=F32)
                              + bglu_ref[...])
    sg = jnp.concatenate(_split_bf16(jax.nn.sigmoid(g_ref[0])), axis=1)
    att = jnp.zeros_like(oc_ref[0])
    for r, o_ref in enumerate((oc_ref, os_ref, ow_ref)):
        att = att + jnp.dot(sg, gexp_ref[r], preferred_element_type=F32) * o_ref[0]
    h = (jnp.dot(ssm.astype(BF16), wout_ref[:D_SSM, :], preferred_element_type=F32)
         + jnp.dot(att.astype(BF16), wout_ref[D_SSM:, :], preferred_element_type=F32))
    z = DN_ALPHA * x_ref[0] + gate_ref[0] * h
    x1 = _layer_norm(z) * lng_ref[...] + lnb_ref[...]
    x1_ref[0] = x1
    hm = _layer_norm(x1) * (1.0 + sc_ref[0]) + sh_ref[0]
    hm_ref[0] = hm
    hm_hi, hm_lo = _split_bf16(hm)
    logits = (jnp.dot(hm_hi, wr_ref[0], preferred_element_type=F32)
              + jnp.dot(hm_lo, wr_ref[0], preferred_element_type=F32)
              + jnp.dot(hm_hi, wr_ref[1], preferred_element_type=F32)) + br_ref[...]
    lane = lax.broadcasted_iota(jnp.int32, logits.shape, 1)
    work = jnp.where(lane < N_EXPERTS, logits, -jnp.inf)
    te = jnp.zeros(logits.shape, jnp.int32)
    tv = jnp.zeros(logits.shape, F32)
    for k in range(TOP_K):
        best = jnp.max(work, axis=-1, keepdims=True)
        arg = jnp.min(jnp.where(work == best, lane, LANE), axis=-1, keepdims=True)
        te = jnp.where(lane == k, arg, te)
        tv = jnp.where(lane == k, best, tv)
        work = jnp.where(lane == arg, -jnp.inf, work)
    ex = jnp.where(lane < TOP_K, jnp.exp(tv - tv[:, 0:1]), 0.0)
    te_ref[0] = te
    tw_ref[0] = ex / jnp.sum(ex, axis=-1, keepdims=True)


def _post_mixer(y, u, oc, osel, ow, g, x, gate, shift, scale, w, tm):
    B, T, D = x.shape
    R = gate.shape[1]
    rb = 1 if R == 1 else tm
    mod_map = (lambda b, i: (b, 0, 0)) if R == 1 else (lambda b, i: (b, i, 0))
    row = lambda n: pl.BlockSpec((1, tm, n), lambda b, i: (b, i, 0))
    mod = pl.BlockSpec((1, rb, D), mod_map)
    full = lambda a: pl.BlockSpec(a.shape, lambda b, i: (0,) * a.ndim)
    consts = (w['d_skip'], w['w_glu'], w['b_glu'], w['gexp'], w['w_out'], w['ln1_g'], w['ln1_b'],
              w['w_router'], w['b_router'])
    return pl.pallas_call(
        _post_mixer_kernel,
        out_shape=(jax.ShapeDtypeStruct((B, T, D), F32), jax.ShapeDtypeStruct((B, T, D), F32),
                   jax.ShapeDtypeStruct((B, T, LANE), jnp.int32), jax.ShapeDtypeStruct((B, T, LANE), F32)),
        grid=(B, T // tm),
        in_specs=[row(D_SSM), row(D_SSM), row(D_ATT), row(D_ATT), row(D_ATT), row(LANE), row(D),
                  mod, mod, mod] + [full(a) for a in consts],
        out_specs=(row(D), row(D), row(LANE), row(LANE)),
        compiler_params=_cparams("parallel", "parallel"),
        name="post_mixer",
    )(y, u, oc, osel, ow, g, x, gate, shift, scale, *consts)


def _expert_kernel(e_ref, blk_ref, lo_ref, hi_ref, first_ref, x_ref, wgu_ref, bgu_ref, wd_ref, bd_ref, o_ref,
                   wgu_s, wd_s):
    i = pl.program_id(0)
    fresh = (i == 0) | (e_ref[i] != e_ref[jnp.maximum(i - 1, 0)])

    @pl.when(fresh)
    def _():
        wgu_s[...] = wgu_ref[0].astype(BF16)
        wd_s[...] = wd_ref[0].astype(BF16)

    @pl.when(first_ref[i] == 1)
    def _():
        o_ref[...] = jnp.zeros_like(o_ref)

    @pl.when(hi_ref[i] > lo_ref[i])
    def _():
        gu = jnp.dot(x_ref[...].astype(BF16), wgu_s[...], preferred_element_type=F32) + bgu_ref[0]
        gate = jnp.minimum(gu[:, :D_FF], SWIGLU_LIMIT)
        up = jnp.clip(gu[:, D_FF:], -SWIGLU_LIMIT, SWIGLU_LIMIT)
        hh = (up + 1.0) * gate * jax.nn.sigmoid(SWIGLU_ALPHA * gate)
        y = jnp.dot(hh.astype(BF16), wd_s[...], preferred_element_type=F32) + bd_ref[0]
        row = blk_ref[i] * MOE_ROWS + lax.broadcasted_iota(jnp.int32, (MOE_ROWS, 1), 0)
        o_ref[...] = jnp.where((row >= lo_ref[i]) & (row < hi_ref[i]), y, o_ref[...])


def _experts(xb, items, w_gate_up, b_gate_up, w_down, b_down):
    rows, D = xb.shape
    n_items = items[0].shape[0]
    wmap = lambda i, e, blk, lo, hi, first: (e[i], 0, 0)
    rmap = lambda i, e, blk, lo, hi, first: (blk[i], 0)
    grid_spec = pltpu.PrefetchScalarGridSpec(
        num_scalar_prefetch=5,
        grid=(n_items,),
        in_specs=[pl.BlockSpec((MOE_ROWS, D), rmap),
                  pl.BlockSpec((1, D, 2 * D_FF), wmap),
                  pl.BlockSpec((1, 1, 2 * D_FF), wmap),
                  pl.BlockSpec((1, D_FF, D), wmap),
                  pl.BlockSpec((1, 1, D), wmap)],
        out_specs=pl.BlockSpec((MOE_ROWS, D), rmap),
        scratch_shapes=[pltpu.VMEM((D, 2 * D_FF), BF16), pltpu.VMEM((D_FF, D), BF16)],
    )
    return pl.pallas_call(
        _expert_kernel,
        out_shape=jax.ShapeDtypeStruct((rows, D), F32),
        grid_spec=grid_spec,
        compiler_params=_cparams("arbitrary"),
        name="moe_experts",
    )(*items, xb, w_gate_up, b_gate_up.reshape(N_EXPERTS, 1, 2 * D_FF), w_down,
      b_down.reshape(N_EXPERTS, 1, D))


def _moe_dispatch(top_e, n):
    blk = MOE_ROWS
    nk = n * TOP_K
    cb = 128
    assert nk % cb == 0
    e = top_e.reshape(-1)
    oh = (jnp.arange(N_EXPERTS)[:, None] == e[None, :]).astype(BF16).reshape(N_EXPERTS, nk // cb, cb)
    before = jnp.asarray(np.triu(np.ones((cb, cb), np.float32), 1), dtype=BF16)
    within = jnp.einsum('ebj,ji->ebi', oh, before, preferred_element_type=F32)
    blk_tot = jnp.sum(oh.astype(F32), axis=2)
    blk_off = jnp.cumsum(blk_tot, axis=1) - blk_tot
    counts = jnp.sum(blk_tot, axis=1)
    start = jnp.cumsum(counts) - counts
    dest = jnp.sum((within + (blk_off + start[:, None])[:, :, None]) * oh.astype(F32), axis=0)
    dest = dest.reshape(nk).astype(jnp.int32)
    order = jnp.argsort(dest)
    n_blk = -(-nk // blk)
    row_tok = jnp.concatenate([(order // TOP_K).astype(jnp.int32), jnp.full((n_blk * blk - nk,), n, jnp.int32)])
    counts_i, start_i = counts.astype(jnp.int32), start.astype(jnp.int32)
    first_b = start_i // blk
    last_b = (start_i + counts_i - 1) // blk
    n_it = jnp.where(counts_i > 0, last_b - first_b + 1, 0)
    it_end = jnp.cumsum(n_it)
    it_start = it_end - n_it
    n_items = n_blk + N_EXPERTS - 1
    i = jnp.arange(n_items)
    live = i < it_end[-1]
    it_e = jnp.minimum(jnp.sum(it_end[None, :] <= i[:, None], axis=1), N_EXPERTS - 1)
    it_blk = jnp.where(live, first_b[it_e] + i - it_start[it_e], n_blk - 1)
    it_lo = jnp.where(live, start_i[it_e], 0)
    it_hi = jnp.where(live, start_i[it_e] + counts_i[it_e], 0)
    it_first = jnp.concatenate([jnp.ones((1,), jnp.int32), (it_blk[1:] != it_blk[:-1]).astype(jnp.int32)])
    items = tuple(a.astype(jnp.int32) for a in (it_e, it_blk, it_lo, it_hi, it_first))
    return row_tok, dest.reshape(n, TOP_K), items


def _final_kernel(x_ref, y0_ref, y1_ref, y2_ref, y3_ref, tw_ref, gate_ref, lng_ref, lnb_ref, o_ref):
    tw = tw_ref[0]
    y = jnp.zeros_like(x_ref[0])
    for k, y_ref in enumerate((y0_ref, y1_ref, y2_ref, y3_ref)):
        y = y + tw[:, k:k + 1] * y_ref[0]
    z = DN_ALPHA * x_ref[0] + gate_ref[0] * y
    o_ref[0] = _layer_norm(z) * lng_ref[...] + lnb_ref[...]


def _final(x1, ys, tw, gate, ln_g, ln_b, tm):
    B, T, D = x1.shape
    R = gate.shape[1]
    rb = 1 if R == 1 else tm
    mod_map = (lambda b, i: (b, 0, 0)) if R == 1 else (lambda b, i: (b, i, 0))
    row = lambda n: pl.BlockSpec((1, tm, n), lambda b, i: (b, i, 0))
    vec = pl.BlockSpec((1, D), lambda b, i: (0, 0))
    return pl.pallas_call(
        _final_kernel,
        out_shape=jax.ShapeDtypeStruct((B, T, D), F32),
        grid=(B, T // tm),
        in_specs=[row(D), row(D), row(D), row(D), row(D), row(LANE),
                  pl.BlockSpec((1, rb, D), mod_map), vec, vec],
        out_specs=row(D),
        compiler_params=_cparams("parallel", "parallel"),
        name="moe_combine_ln",
    )(x1, *ys, tw, gate, ln_g, ln_b)


def _cmp_select_step_kernel(q_ref, kv_ref, bias_ref, pool_ref, o_ref, idx_ref, *, n_cmp, n_blk, q_pos):
    q = q_ref[0].astype(BF16)
    ncp = kv_ref.shape[1]
    nbp = pool_ref.shape[1]
    hd = HEAD_DIM
    kv = kv_ref[0]
    kb = [kv[:, h * hd:(h + 1) * hd].astype(BF16) for h in range(N_KV_HEADS)]
    vb = [kv[:, (N_KV_HEADS + h) * hd:(N_KV_HEADS + h + 1) * hd].astype(BF16) for h in range(N_KV_HEADS)]
    row = lax.broadcasted_iota(jnp.int32, (N_HEADS, 1), 0)
    first = row < GQA
    s = jnp.where(first, _nt_dot(q, kb[0]), _nt_dot(q, kb[1])) * (hd ** -0.5)
    s = s + bias_ref[...]
    ci = lax.broadcasted_iota(jnp.int32, (N_HEADS, ncp), 1)
    mask = (ci * CMP_STRIDE + CMP_BLOCK - 1 <= q_pos) & (ci < n_cmp)
    s = jnp.where(mask, s, NEG)
    m = jnp.max(s, axis=-1, keepdims=True)
    p = jnp.where(mask, jnp.exp(s - m), 0.0)
    p = p / jnp.maximum(jnp.sum(p, axis=-1, keepdims=True), 1e-30)
    pb = p.astype(BF16)
    o_ref[0] = jnp.where(first, jnp.dot(pb, vb[0], preferred_element_type=F32),
                         jnp.dot(pb, vb[1], preferred_element_type=F32))
    imp0 = jnp.sum(jnp.where(first, p, 0.0), axis=0, keepdims=True)
    imp1 = jnp.sum(jnp.where(first, 0.0, p), axis=0, keepdims=True)
    imp = jnp.where(first, imp0, imp1)
    sb = jnp.dot(imp, pool_ref[...], precision=HIGHEST, preferred_element_type=F32)
    cur = q_pos // SEL_BLOCK
    bi = lax.broadcasted_iota(jnp.int32, (nbp, nbp), 0)
    bj = lax.broadcasted_iota(jnp.int32, (nbp, nbp), 1)
    blk = lax.broadcasted_iota(jnp.int32, (1, nbp), 1)
    causal = blk <= cur
    forced = (blk == 0) | (blk == cur) | (blk == cur - 1)
    rsel = lax.broadcasted_iota(jnp.int32, (N_SEL, nbp), 0)
    for h in range(N_KV_HEADS):
        sc = jnp.where(forced & causal, 1e4, jnp.where(causal, sb[h * GQA:h * GQA + 1, :], -1.0))
        sc = jnp.where(blk < n_blk, sc, -2.0)
        scb = jnp.broadcast_to(sc, (nbp, nbp))
        col = jnp.sum(jnp.where(bi == bj, scb, 0.0), axis=1, keepdims=True)
        ahead = (col > scb) | ((col == scb) & (bi < bj))
        rank = jnp.sum(ahead.astype(jnp.int32), axis=0, keepdims=True)
        hit = jnp.broadcast_to(rank, (N_SEL, nbp)) == rsel
        idx = jnp.sum(jnp.where(hit, jnp.broadcast_to(blk, (N_SEL, nbp)), 0), axis=1, keepdims=True)
        idx_ref[0, h] = jnp.broadcast_to(idx, (N_SEL, LANE))


def _cmp_select_step(q, ckv, bias, pool, n_cmp, n_blk, q_pos):
    B = q.shape[0]
    NCp = ckv.shape[1]
    return pl.pallas_call(
        functools.partial(_cmp_select_step_kernel, n_cmp=n_cmp, n_blk=n_blk, q_pos=q_pos),
        out_shape=(jax.ShapeDtypeStruct((B, N_HEADS, HEAD_DIM), F32),
                   jax.ShapeDtypeStruct((B, N_KV_HEADS, N_SEL, LANE), jnp.int32)),
        grid=(B,),
        in_specs=[pl.BlockSpec((1, N_HEADS, HEAD_DIM), lambda b: (b, 0, 0)),
                  pl.BlockSpec((1, NCp, D_KV), lambda b: (b, 0, 0)),
                  pl.BlockSpec(bias.shape, lambda b: (0, 0)),
                  pl.BlockSpec(pool.shape, lambda b: (0, 0))],
        out_specs=(pl.BlockSpec((1, N_HEADS, HEAD_DIM), lambda b: (b, 0, 0)),
                   pl.BlockSpec((1, N_KV_HEADS, N_SEL, LANE), lambda b: (b, 0, 0, 0))),
        compiler_params=_cparams("parallel"),
        name="cmp_select_step",
    )(q, ckv, bias, pool)


def _sel_step_kernel(pg_ref, idx_ref, q_ref, *refs, n_past, q_pos):
    page_refs = refs[:N_SEL]
    new_ref, bias_ref, kpos_ref, o_ref = refs[N_SEL:]
    b, h = pl.program_id(0), pl.program_id(1)
    base = (b * N_KV_HEADS + h) * N_SEL
    kts, vts = [], []
    for j in range(N_SEL):
        is_new = idx_ref[base + j] >= n_past
        kts.append(jnp.where(is_new, new_ref[0, 0, 0], page_refs[j][0, 0, 0]))
        vts.append(jnp.where(is_new, new_ref[0, 1, 0], page_refs[j][0, 1, 0]))
    kt = jnp.concatenate(kts, axis=1).astype(BF16)
    vt = jnp.concatenate(vts, axis=1).astype(BF16)
    s = jnp.dot(q_ref[0].astype(BF16), kt, preferred_element_type=F32) * (HEAD_DIM ** -0.5) + bias_ref[0, 0]
    mask = kpos_ref[0, 0] <= q_pos
    s = jnp.where(mask, s, NEG)
    m = jnp.max(s, axis=-1, keepdims=True)
    p = jnp.where(mask, jnp.exp(s - m), 0.0)
    l = jnp.sum(p, axis=-1, keepdims=True)
    o_ref[0, 0] = _nt_dot(p.astype(BF16), vt) / jnp.maximum(l, 1e-30)


def _sel_step(q, pool_t, new_t, bias_sel, kpos, pages, idx_flat, n_past, q_pos):
    B = q.shape[0]
    nk = N_SEL * PAGE_SIZE
    slot = lambda b, h, j: (b * N_KV_HEADS + h) * N_SEL + j
    page_spec = lambda j: pl.BlockSpec((1, 2, 1, HEAD_DIM, PAGE_SIZE),
                                       lambda b, h, pg, ix, j=j: (pg[slot(b, h, j)], 0, h, 0, 0))
    grid_spec = pltpu.PrefetchScalarGridSpec(
        num_scalar_prefetch=2,
        grid=(B, N_KV_HEADS),
        in_specs=[pl.BlockSpec((1, N_HEADS, HEAD_DIM), lambda b, h, pg, ix: (b, 0, 0))]
        + [page_spec(j) for j in range(N_SEL)]
        + [pl.BlockSpec((1, 2, 1, HEAD_DIM, PAGE_SIZE), lambda b, h, pg, ix: (b, 0, h, 0, 0)),
           pl.BlockSpec((1, 1, N_HEADS, nk), lambda b, h, pg, ix: (b, h, 0, 0)),
           pl.BlockSpec((1, 1, 1, nk), lambda b, h, pg, ix: (b, h, 0, 0))],
        out_specs=pl.BlockSpec((1, 1, N_HEADS, HEAD_DIM), lambda b, h, pg, ix: (b, h, 0, 0)),
    )
    return pl.pallas_call(
        functools.partial(_sel_step_kernel, n_past=n_past, q_pos=q_pos),
        out_shape=jax.ShapeDtypeStruct((B, N_KV_HEADS, N_HEADS, HEAD_DIM), F32),
        grid_spec=grid_spec,
        compiler_params=_cparams("arbitrary", "arbitrary"),
        name="sel_step",
    )(pages, idx_flat, q, *([pool_t] * N_SEL), new_t, bias_sel, kpos)


def _win_step_kernel(q_ref, w_ref, new_ref, bias_ref, bias0_ref, o_ref):
    q = q_ref[0]
    qb = q.astype(BF16)
    row = lax.broadcasted_iota(jnp.int32, (N_HEADS, 1), 0)
    first = row < GQA
    hd = HEAD_DIM
    kt = [w_ref[0, 0, h].astype(BF16) for h in range(N_KV_HEADS)]
    vt = [w_ref[0, 1, h].astype(BF16) for h in range(N_KV_HEADS)]
    dots = [jnp.dot(qb, kt[h], preferred_element_type=F32) for h in range(N_KV_HEADS)]
    s = jnp.where(first, dots[0], dots[1]) * (hd ** -0.5) + bias_ref[...]
    new = new_ref[0]
    kn = jnp.where(first, new[:, 0:hd], new[:, hd:2 * hd])
    vn = jnp.where(first, new[:, 2 * hd:3 * hd], new[:, 3 * hd:])
    sn = jnp.sum(q * kn, axis=-1, keepdims=True) * (hd ** -0.5) + bias0_ref[...]
    m = jnp.maximum(jnp.max(s, axis=-1, keepdims=True), sn)
    p = jnp.exp(s - m)
    pn = jnp.exp(sn - m)
    l = jnp.sum(p, axis=-1, keepdims=True) + pn
    pb = p.astype(BF16)
    acc = jnp.where(first, _nt_dot(pb, vt[0]), _nt_dot(pb, vt[1])) + pn * vn
    o_ref[0] = acc / jnp.maximum(l, 1e-30)


def _win_step(q, win_t, new, bias, bias0):
    B, W = win_t.shape[0], win_t.shape[-1]
    return pl.pallas_call(
        _win_step_kernel,
        out_shape=jax.ShapeDtypeStruct((B, N_HEADS, HEAD_DIM), F32),
        grid=(B,),
        in_specs=[pl.BlockSpec((1, N_HEADS, HEAD_DIM), lambda b: (b, 0, 0)),
                  pl.BlockSpec((1,) + win_t.shape[1:], lambda b: (b, 0, 0, 0, 0)),
                  pl.BlockSpec((1, 1, D_KV), lambda b: (b, 0, 0)),
                  pl.BlockSpec((N_HEADS, W), lambda b: (0, 0)),
                  pl.BlockSpec((N_HEADS, 1), lambda b: (0, 0))],
        out_specs=pl.BlockSpec((1, N_HEADS, HEAD_DIM), lambda b: (b, 0, 0)),
        compiler_params=_cparams("parallel"),
        name="win_step",
    )(q, win_t, new, bias, bias0)


def _split_heads(kv, dtype):
    B, L, _ = kv.shape
    kv5 = kv.reshape(B, L, 2, N_KV_HEADS, HEAD_DIM)
    return (jnp.transpose(kv5[:, :, 0], (0, 2, 1, 3)).astype(dtype),
            jnp.transpose(kv5[:, :, 1], (0, 2, 1, 3)).astype(dtype))


def _nsa_prompt(q5, kvc, ks, vst, kw, vwt, cmp_tab, rel_bias):
    B, T, _ = kvc.shape
    nc = T // CMP_STRIDE
    nb = T // SEL_BLOCK
    ckv = _compress_out([_compress_in(kvc.reshape(B, nc, CMP_STRIDE * D_KV), cmp_tab)], cmp_tab, nc)
    kc, vc = _split_heads(ckv, BF16)
    vct = jnp.transpose(vc, (0, 1, 3, 2))
    bias_n = _bias_by_distance(rel_bias, T)
    n_qt, n_kt = T // ATT_TQ, T // ATT_TK
    n_ds = min(n_kt, -(-(REL_MAX_DIST + ATT_TK - 1) // ATT_TK) + 1)
    n_dw = min(n_kt, WINDOW // ATT_TK + 1)
    tzs, tzw, bias_tab = _bias_tables(bias_n, n_qt, nc // 8, n_ds, n_dw, ATT_TQ, ATT_TK)
    pool = jnp.asarray(_pool_matrix(nc, nb))
    o_cmp, sel = _cmp_select_prompt(q5, kc, vct, bias_tab, pool)
    o_sel, o_win = _sel_win_prompt(q5, ks, vst, kw, vwt, sel, tzs, tzw)
    return o_cmp, o_sel, o_win


def _nsa_sample(q, kvc, kvs, kvw, pool_cmp, pool_sel, win_buf, page_table, cmp_tab, rel_bias):
    B = q.shape[0]
    n_pages = page_table.shape[1]
    past_len = n_pages * PAGE_SIZE
    q_pos = past_len
    lp = -(-(past_len + 1) // SEL_BLOCK) * SEL_BLOCK
    n_cmp = lp // CMP_STRIDE - 1
    n_blk = lp // SEL_BLOCK
    n_past_chunks = past_len // CMP_STRIDE
    n_tail = 8
    assert n_past_chunks + n_tail >= n_cmp + 1
    n_chunks = n_past_chunks + n_tail
    feature_major = lambda pool: jnp.transpose(pool, (0, 2, 3, 4, 1))
    z_past = _compress_in_paged(feature_major(pool_cmp), page_table, cmp_tab)
    tail = jnp.pad(kvc[:, None, :], ((0, 0), (0, n_tail * CMP_STRIDE - 1), (0, 0)))
    z_tail = _compress_in(tail.reshape(B, n_tail, CMP_STRIDE * D_KV), cmp_tab)
    ncp = -(-n_chunks // LANE) * LANE
    nbp = -(-n_blk // LANE) * LANE
    ckv = _compress_out([z_past, z_tail], cmp_tab, ncp)
    bias_n = _bias_by_distance(rel_bias, q_pos + 1)
    n_back = max((n_pages + 1) * PAGE_SIZE, ncp * CMP_STRIDE + CMP_BLOCK)
    back = jnp.concatenate([bias_n[:, ::-1], jnp.broadcast_to(bias_n[:, :1], (N_HEADS, n_back - q_pos - 1))], 1)
    bias_c = back[:, CMP_BLOCK - 1:CMP_BLOCK - 1 + ncp * CMP_STRIDE:CMP_STRIDE]
    pool = jnp.asarray(_pool_matrix(ncp, nbp).T)
    q3 = q.reshape(B, N_HEADS, HEAD_DIM)
    o_cmp, idx = _cmp_select_step(q3, ckv, bias_c, pool, n_cmp, n_blk, q_pos)
    idx = idx[..., 0]
    bpp = PAGE_SIZE // SEL_BLOCK
    n_past = n_pages * bpp
    lpage = idx // bpp
    pages = jnp.take_along_axis(page_table, jnp.minimum(lpage, n_pages - 1).reshape(B, -1), axis=1)
    new_t = jnp.pad(kvs.reshape(B, 2, N_KV_HEADS, HEAD_DIM, 1), ((0, 0),) * 4 + ((0, PAGE_SIZE - 1),))
    bias_page = jnp.transpose(back[:, :(n_pages + 1) * PAGE_SIZE].reshape(N_HEADS, n_pages + 1, PAGE_SIZE),
                              (1, 0, 2))
    bias_sel = jnp.transpose(bias_page[lpage], (0, 1, 3, 2, 4)).reshape(B, N_KV_HEADS, N_HEADS, -1)
    kpos = lpage[..., None] * PAGE_SIZE + jnp.arange(PAGE_SIZE)
    ok = (kpos // SEL_BLOCK == idx[..., None]) & (idx <= q_pos // SEL_BLOCK)[..., None]
    kpos = jnp.where(ok, kpos, q_pos + 1).reshape(B, N_KV_HEADS, 1, -1).astype(jnp.int32)
    o_sel = _sel_step(q3, feature_major(pool_sel), new_t, bias_sel, kpos, pages.reshape(-1).astype(jnp.int32),
                      idx.reshape(-1).astype(jnp.int32), n_past, q_pos)
    o_sel = jnp.concatenate([o_sel[:, h, h * GQA:(h + 1) * GQA] for h in range(N_KV_HEADS)], axis=1)
    wb = win_buf.shape[1]
    bias_w = bias_n[:, 1:wb + 1][:, ::-1]
    o_win = _win_step(q3, feature_major(win_buf), kvw[:, None, :], bias_w, bias_n[:, 0:1])
    return o_cmp.reshape(B, D_ATT), o_sel.reshape(B, D_ATT), o_win.reshape(B, D_ATT)


def kernel(x_prompt, x_sample, cache_cmp_kv, cache_sel_kv, state_win_kv, state_ssm_re, state_ssm_im, page_table,
           c_prompt, c_sample, w_ada, b_ada, w_in, lam_re, lam_im, log_dt, b_re, b_im, c_re, c_im, d_skip,
           w_glu, b_glu, phi_pe, phi_w1, phi_b1, phi_w2, phi_b2, rel_bias, w_out, ln1_g, ln1_b,
           w_router, b_router, w_gate_up, b_gate_up, w_down, b_down, ln2_g, ln2_b):
    assert w_ada.shape[0] == DEPTH == 1
    l = 0
    Bp, T, D = x_prompt.shape
    Bs = x_sample.shape[0]
    kv_tail = (2, N_KV_HEADS, HEAD_DIM)

    n_c = Bp + Bs
    c_all = jnp.pad(jnp.concatenate([c_prompt, c_sample], 0), ((0, -n_c % 8), (0, 0)))
    m_all = _adaln(c_all, w_ada[l], b_ada[l])
    m_p = m_all[:Bp].reshape(Bp, 6, D)
    m_s = m_all[Bp:n_c].reshape(Bs, 6, D)
    mod_p = [m_p[:, i:i + 1, :] for i in range(6)]
    mod_s = [m_s[None, :, i, :] for i in range(6)]

    w_in_pad = jnp.pad(w_in[l], ((0, 0), (0, D_IN_PAD - D_IN))).astype(BF16)
    n_levels = max(1, int(math.log2(T // SSM_CHUNK)))
    ssm_tab = _ssm_tables(lam_re[l], lam_im[l], log_dt[l], b_re[l], b_im[l], c_re[l], c_im[l],
                          SSM_CHUNK, n_levels)
    cmp_tab = _compress_tables(phi_pe[l], phi_w1[l], phi_b1[l], phi_w2[l], phi_b2[l])
    w_post = dict(
        d_skip=d_skip[l].reshape(1, D_SSM), w_glu=w_glu[l].astype(BF16), b_glu=b_glu[l].reshape(1, D_SSM),
        gexp=jnp.asarray(_gate_expand_matrix(), dtype=BF16), w_out=w_out[l].astype(BF16),
        ln1_g=ln1_g[l].reshape(1, D), ln1_b=ln1_b[l].reshape(1, D),
        w_router=jnp.stack(_split_bf16(jnp.pad(w_router[l], ((0, 0), (0, LANE - N_EXPERTS))))),
        b_router=jnp.pad(b_router[l], (0, LANE - N_EXPERTS)).reshape(1, LANE))

    u, q5, kvc, kvs, kvw, g, ks, vst, kw, vwt = _mixer_in(x_prompt, mod_p[0], mod_p[1], w_in_pad, 512, True)
    y_ssm, h_p = _ssm_prompt(u, ssm_tab)
    o_cmp, o_sel, o_win = _nsa_prompt(q5, kvc, ks, vst, kw, vwt, cmp_tab, rel_bias)
    x1_p, hm_p, te_p, tw_p = _post_mixer(y_ssm, u, o_cmp, o_sel, o_win, g, x_prompt,
                                         mod_p[2], mod_p[3], mod_p[4], w_post, tm=512)

    u_s, q_s, kvc_s, kvs_s, kvw_s, g_s = _mixer_in(x_sample.reshape(1, Bs, D), mod_s[0], mod_s[1],
                                                   w_in_pad, Bs, False)
    y_s, h_s = _ssm_sample(u_s[0], state_ssm_re[l], state_ssm_im[l], ssm_tab, c_re[l], c_im[l])
    oc_s, os_s, ow_s = _nsa_sample(q_s[0].astype(F32), kvc_s[0], kvs_s[0], kvw_s[0], cache_cmp_kv[l],
                                   cache_sel_kv[l], state_win_kv[l], page_table, cmp_tab, rel_bias)
    x1_s, hm_s, te_s, tw_s = _post_mixer(y_s[None], u_s, oc_s[None], os_s[None], ow_s[None], g_s,
                                         x_sample.reshape(1, Bs, D), mod_s[2], mod_s[3], mod_s[4],
                                         w_post, tm=Bs)

    n_p = Bp * T
    n_all = n_p + Bs
    hm_all = jnp.concatenate([hm_p.reshape(n_p, D), hm_s.reshape(Bs, D)], 0)
    te_all = jnp.concatenate([te_p.reshape(n_p, LANE), te_s.reshape(Bs, LANE)], 0)[:, :TOP_K]
    row_tok, dest, items = _moe_dispatch(te_all, n_all)
    xb = jnp.concatenate([hm_all, jnp.zeros((1, D), F32)], 0)[row_tok]
    yb = _experts(xb, items, w_gate_up[l], b_gate_up[l], w_down[l], b_down[l])
    ys_p = [yb[dest[:n_p, k]].reshape(Bp, T, D) for k in range(TOP_K)]
    ys_s = [yb[dest[n_p:, k]].reshape(1, Bs, D) for k in range(TOP_K)]
    ln2g, ln2b = ln2_g[l].reshape(1, D), ln2_b[l].reshape(1, D)
    out_p = _final(x1_p, ys_p, tw_p, mod_p[5], ln2g, ln2b, tm=512)
    out_s = _final(x1_s, ys_s, tw_s, mod_s[5], ln2g, ln2b, tm=Bs)

    wlen = min(WINDOW, T)
    win_s = jnp.concatenate([state_win_kv[l], kvw_s[0].reshape(Bs, 1, *kv_tail)], 1)[:, -state_win_kv.shape[2]:]
    p_state = SSM_STATE
    return (out_p, out_s.reshape(Bs, 1, D),
            kvc.reshape(1, Bp, T, *kv_tail), kvc_s[0].reshape(1, Bs, 1, *kv_tail),
            kvs.reshape(1, Bp, T, *kv_tail), kvs_s[0].reshape(1, Bs, 1, *kv_tail),
            kvw[:, T - wlen:].reshape(1, Bp, wlen, *kv_tail), win_s[None],
            h_p[None, ..., :p_state], h_p[None, ..., p_state:],
            h_s[None, ..., :p_state], h_s[None, ..., p_state:])
```

```python
import functools
import math

import numpy as np
import jax
import jax.numpy as jnp
from jax import lax
from jax.experimental import pallas as pl
from jax.experimental.pallas import tpu as pltpu

DEPTH = 1
PAGE_SIZE = 128
D_SSM = 512
SSM_GROUP = 16
N_SSM_GROUPS = D_SSM // SSM_GROUP
SSM_STATE = 64
N_HEADS = 8
HEAD_DIM = 64
N_KV_HEADS = 2
GQA = N_HEADS // N_KV_HEADS
D_ATT = N_HEADS * HEAD_DIM
D_KV = 2 * N_KV_HEADS * HEAD_DIM
CMP_STRIDE = 16
CMP_BLOCK = 2 * CMP_STRIDE
SEL_BLOCK = 64
N_SEL = 16
WINDOW = 512
NUM_BUCKETS = 32
REL_MAX_DIST = 1024
N_EXPERTS = 32
TOP_K = 4
D_FF = 1024
SWIGLU_LIMIT = 7.0
SWIGLU_ALPHA = 1.702
DN_ALPHA = (2 * DEPTH) ** 0.25
D_IN = D_SSM + D_ATT + 3 * D_KV + 3 * N_HEADS
NEG = -1e30
F32 = jnp.float32
BF16 = jnp.bfloat16
HIGHEST = lax.Precision.HIGHEST

LANE = 128
D_IN_PAD = -(-D_IN // LANE) * LANE
SSM_CHUNK = 8
ATT_TQ = 128
ATT_TK = 128
SEL_CHAINS = 4
MOE_ROWS = 256
PAGES_PER_STEP = 32
PAGE_PARTS = 2
CHUNK_PITCH = 24
VMEM_LIMIT = 48 * 1024 * 1024
LN_EPS = 1e-5


def _cparams(*sem):
    return pltpu.CompilerParams(dimension_semantics=sem, vmem_limit_bytes=VMEM_LIMIT)


def _nt_dot(a, b):
    return lax.dot_general(a, b, (((1,), (1,)), ((), ())), preferred_element_type=F32)


def _layer_norm(x):
    mu = jnp.mean(x, axis=-1, keepdims=True)
    xc = x - mu
    var = jnp.mean(xc * xc, axis=-1, keepdims=True)
    return xc * lax.rsqrt(var + LN_EPS)


def _adaln_kernel(c_ref, w_ref, b_ref, o_ref):
    c = c_ref[...]
    s = c * jax.nn.sigmoid(c)
    o_ref[...] = jnp.dot(s, w_ref[...], precision=HIGHEST, preferred_element_type=F32) + b_ref[...]


def _adaln(c, w, b):
    n, d = c.shape
    dout = w.shape[1]
    tn = 1024
    return pl.pallas_call(
        _adaln_kernel,
        out_shape=jax.ShapeDtypeStruct((n, dout), F32),
        grid=(dout // tn,),
        in_specs=[pl.BlockSpec((n, d), lambda j: (0, 0)),
                  pl.BlockSpec((d, tn), lambda j: (0, j)),
                  pl.BlockSpec((1, tn), lambda j: (0, j))],
        out_specs=pl.BlockSpec((n, tn), lambda j: (0, j)),
        compiler_params=_cparams("arbitrary"),
        name="adaln",
    )(c, w, b.reshape(1, dout))


def _mixer_in_kernel(x_ref, sh_ref, sc_ref, w_ref, u_ref, q_ref, kvc_ref, kvs_ref, kvw_ref, g_ref, *att_refs):
    h = _layer_norm(x_ref[0]) * (1.0 + sc_ref[0]) + sh_ref[0]
    z = jnp.dot(h.astype(BF16), w_ref[...], preferred_element_type=F32)
    c0 = D_SSM
    c1 = c0 + D_ATT
    c2 = c1 + D_KV
    c3 = c2 + D_KV
    c4 = c3 + D_KV
    u_ref[0] = z[:, :c0]
    kvc_ref[0] = z[:, c1:c2]
    kvs_ref[0] = z[:, c2:c3]
    kvw_ref[0] = z[:, c3:c4]
    g_ref[0] = z[:, c4:c4 + LANE]
    if not att_refs:
        q_ref[0] = z[:, c0:c1].astype(BF16)
        return
    ks_ref, vst_ref, kw_ref, vwt_ref = att_refs
    hd, half = HEAD_DIM, N_KV_HEADS * HEAD_DIM
    for hq in range(N_HEADS):
        q_ref[0, hq // GQA, hq % GQA] = (z[:, c0 + hq * hd:c0 + (hq + 1) * hd] * (hd ** -0.5)).astype(BF16)
    for k_ref, vt_ref, base in ((ks_ref, vst_ref, c2), (kw_ref, vwt_ref, c3)):
        for hk in range(N_KV_HEADS):
            k_ref[0, hk] = z[:, base + hk * hd:base + (hk + 1) * hd].astype(BF16)
        vt = z[:, base + half:base + 2 * half].T
        vt_ref[0] = vt.reshape(N_KV_HEADS, hd, vt.shape[1]).astype(BF16)


def _mixer_in(x, shift, scale, w_pad, tm, attention_layouts):
    B, T, D = x.shape
    R = shift.shape[1]
    rb = 1 if R == 1 else tm
    mod_map = (lambda b, i: (b, 0, 0)) if R == 1 else (lambda b, i: (b, i, 0))
    row = lambda n: pl.BlockSpec((1, tm, n), lambda b, i: (b, i, 0))
    f32 = lambda n: jax.ShapeDtypeStruct((B, T, n), F32)
    if attention_layouts:
        q_shape = jax.ShapeDtypeStruct((B, N_KV_HEADS, GQA, T, HEAD_DIM), BF16)
        q_spec = pl.BlockSpec((1, N_KV_HEADS, GQA, tm, HEAD_DIM), lambda b, i: (b, 0, 0, i, 0))
        k_shape = jax.ShapeDtypeStruct((B, N_KV_HEADS, T, HEAD_DIM), BF16)
        k_spec = pl.BlockSpec((1, N_KV_HEADS, tm, HEAD_DIM), lambda b, i: (b, 0, i, 0))
        vt_shape = jax.ShapeDtypeStruct((B, N_KV_HEADS, HEAD_DIM, T), BF16)
        vt_spec = pl.BlockSpec((1, N_KV_HEADS, HEAD_DIM, tm), lambda b, i: (b, 0, 0, i))
        extra_shapes, extra_specs = (k_shape, vt_shape, k_shape, vt_shape), (k_spec, vt_spec, k_spec, vt_spec)
    else:
        q_shape, q_spec = jax.ShapeDtypeStruct((B, T, D_ATT), BF16), row(D_ATT)
        extra_shapes, extra_specs = (), ()
    return pl.pallas_call(
        _mixer_in_kernel,
        out_shape=(f32(D_SSM), q_shape, f32(D_KV), f32(D_KV), f32(D_KV), f32(LANE)) + extra_shapes,
        grid=(B, T // tm),
        in_specs=[row(D), pl.BlockSpec((1, rb, D), mod_map), pl.BlockSpec((1, rb, D), mod_map),
                  pl.BlockSpec((D, D_IN_PAD), lambda b, i: (0, 0))],
        out_specs=(row(D_SSM), q_spec, row(D_KV), row(D_KV), row(D_KV), row(LANE)) + extra_specs,
        compiler_params=_cparams("parallel", "parallel"),
        name="mixer_in",
    )(x, shift, scale, w_pad)


def _ssm_tables(lam_re, lam_im, log_dt, b_re, b_im, c_re, c_im, L, n_levels):
    G, P = lam_re.shape
    C = b_re.shape[-1]
    dt = jnp.exp(log_dt.astype(F32))[:, None]
    er, ei = lam_re * dt, lam_im * dt

    def power(k):
        kk = k.astype(F32)[:, None, None]
        mag = jnp.exp(kk * er)
        return mag * jnp.cos(kk * ei), mag * jnp.sin(kk * ei)

    lb_re, lb_im = power(jnp.ones((1,), F32))
    nr, ni = lb_re[0] - 1.0, lb_im[0]
    den = lam_re * lam_re + lam_im * lam_im
    fr = (nr * lam_re + ni * lam_im) / den
    fi = (ni * lam_re - nr * lam_im) / den
    bbr = fr[:, :, None] * b_re - fi[:, :, None] * b_im
    bbi = fr[:, :, None] * b_im + fi[:, :, None] * b_re
    pr, pi = power(jnp.arange(L + 1))
    clr = c_re[None] * pr[:, :, None, :] - c_im[None] * pi[:, :, None, :]
    cli = c_re[None] * pi[:, :, None, :] + c_im[None] * pr[:, :, None, :]
    kern = (jnp.einsum('kgcp,gpd->kgcd', clr[:L], bbr, precision=HIGHEST)
            - jnp.einsum('kgcp,gpd->kgcd', cli[:L], bbi, precision=HIGHEST))
    GP = LANE // C
    X = G // GP
    eye = jnp.eye(GP, dtype=BF16)
    place_einsum = functools.partial(jnp.einsum, preferred_element_type=BF16)
    kblk = place_einsum('kxhcd,hj->xkhdjc', kern.astype(BF16).reshape(L, X, GP, C, C), eye)
    kblk = kblk.reshape(X, L, LANE, LANE)
    prr, pir = pr[:L][::-1], pi[:L][::-1]
    ws2 = jnp.stack([prr[..., None] * bbr[None] - pir[..., None] * bbi[None],
                     prr[..., None] * bbi[None] + pir[..., None] * bbr[None]])
    ws = place_einsum('rsxhpd,hj->xshdrjp', ws2.astype(BF16).reshape(2, L, X, GP, P, C), eye)
    ws = ws.reshape(X, L * LANE, 2 * GP * P)
    wy2 = jnp.stack([clr[1:], -cli[1:]])
    wy = place_einsum('rtxhcp,hj->xrhptjc', wy2.astype(BF16).reshape(2, L, X, GP, C, P), eye)
    wy = wy.reshape(X, 2 * GP * P, L * LANE)
    lr, li = power(L * (2 ** jnp.arange(n_levels)))
    lr, li = lr.reshape(n_levels, X, GP * P), li.reshape(n_levels, X, GP * P)
    ar = jnp.transpose(jnp.concatenate([lr, lr], -1), (1, 0, 2))
    ai = jnp.transpose(jnp.concatenate([-li, li], -1), (1, 0, 2))
    return kblk, ws, wy, ar, ai, (lb_re[0], lb_im[0], bbr, bbi)


def _ssm_kernel(u_ref, kblk_ref, ws_ref, wy_ref, ar_ref, ai_ref, y_ref, hl_ref, toep_ref, *, L, nc, n_levels):
    for s in range(L):
        for t in range(L):
            blk = kblk_ref[0, t - s] if t >= s else jnp.zeros((LANE, LANE), BF16)
            toep_ref[s * LANE:(s + 1) * LANE, t * LANE:(t + 1) * LANE] = blk
    u = jnp.concatenate([u_ref[0, pl.ds(t, nc, stride=L), :] for t in range(L)], axis=1).astype(BF16)
    y1 = jnp.dot(u, toep_ref[...], preferred_element_type=F32)
    h = jnp.dot(u, ws_ref[0], preferred_element_type=F32)
    w2 = h.shape[-1]
    rows = lax.broadcasted_iota(jnp.int32, (nc, w2), 0)
    for k in range(n_levels):
        d = 1 << k
        sh = jnp.where(rows >= d, pltpu.roll(h, d, axis=0), 0.0)
        sw = pltpu.roll(sh, w2 // 2, axis=1)
        h = h + ar_ref[0, k:k + 1, :] * sh + ai_ref[0, k:k + 1, :] * sw
    hl_ref[0, 0] = h[nc - 1:nc, :]
    hp = jnp.where(rows >= 1, pltpu.roll(h, 1, axis=0), 0.0)
    y = y1 + jnp.dot(hp.astype(BF16), wy_ref[0], preferred_element_type=F32)
    for t in range(L):
        y_ref[0, pl.ds(t, nc, stride=L), :] = y[:, t * LANE:(t + 1) * LANE]


def _ssm_prompt(u, tables):
    kblk, ws, wy, ar, ai, _ = tables
    B, T, _ = u.shape
    L, P = SSM_CHUNK, SSM_STATE
    X, n_levels, w2 = ar.shape
    GP = w2 // (2 * P)
    nc = T // L
    tab = lambda a: pl.BlockSpec((1,) + a.shape[1:], lambda x, b: (x,) + (0,) * (a.ndim - 1))
    seq = pl.BlockSpec((1, T, LANE), lambda x, b: (b, 0, x))
    y, hl = pl.pallas_call(
        functools.partial(_ssm_kernel, L=L, nc=nc, n_levels=n_levels),
        out_shape=(jax.ShapeDtypeStruct((B, T, D_SSM), F32), jax.ShapeDtypeStruct((X, B, 1, w2), F32)),
        grid=(X, B),
        in_specs=[seq, tab(kblk), tab(ws), tab(wy), tab(ar), tab(ai)],
        out_specs=(seq, pl.BlockSpec((1, 1, 1, w2), lambda x, b: (x, b, 0, 0))),
        scratch_shapes=[pltpu.VMEM((L * LANE, L * LANE), BF16)],
        compiler_params=_cparams("parallel", "parallel"),
        name="ssm_prompt",
    )(u, kblk, ws, wy, ar, ai)
    hl = jnp.transpose(hl.reshape(X, B, 2, GP, P), (1, 0, 3, 2, 4))
    return y, hl.reshape(B, X * GP, 2 * P)


def _ssm_step_kernel(u_ref, h0_ref, bb_ref, lr_ref, li_ref, cy_ref, y_ref, h_ref):
    p = lr_ref.shape[-1] // 2
    bu = jnp.einsum('gbc,gcp->gbp', u_ref[...], bb_ref[...], preferred_element_type=F32)
    h0 = h0_ref[...]
    h0s = jnp.concatenate([h0[..., p:], h0[..., :p]], axis=-1)
    h = lr_ref[...] * h0 + li_ref[...] * h0s + bu
    h_ref[...] = h
    y_ref[...] = jnp.einsum('gbp,gpc->gbc', h.astype(BF16), cy_ref[...], preferred_element_type=F32)


def _ssm_sample(u, h0_re, h0_im, tables, c_re, c_im):
    lb_re, lb_im, bbr, bbi = tables[-1]
    B = u.shape[0]
    G, C, P = N_SSM_GROUPS, SSM_GROUP, SSM_STATE
    ug = jnp.transpose(u.reshape(B, G, C), (1, 0, 2)).astype(BF16)
    h0 = jnp.transpose(jnp.concatenate([h0_re, h0_im], -1), (1, 0, 2)).astype(F32)
    bb = jnp.concatenate([jnp.transpose(bbr, (0, 2, 1)), jnp.transpose(bbi, (0, 2, 1))], -1).astype(BF16)
    lr = jnp.concatenate([lb_re, lb_re], -1)[:, None, :]
    li = jnp.concatenate([-lb_im, lb_im], -1)[:, None, :]
    cy = jnp.concatenate([jnp.transpose(c_re, (0, 2, 1)), -jnp.transpose(c_im, (0, 2, 1))], 1).astype(BF16)
    y, h = pl.pallas_call(
        _ssm_step_kernel,
        out_shape=(jax.ShapeDtypeStruct((G, B, C), F32), jax.ShapeDtypeStruct((G, B, 2 * P), F32)),
        name="ssm_step",
    )(ug, h0, bb, lr, li, cy)
    return jnp.transpose(y, (1, 0, 2)).reshape(B, D_SSM), jnp.transpose(h, (1, 0, 2))


def _compress_tables(phi_pe, phi_w1, phi_b1, phi_w2, phi_b2):
    S, H, Dh = CMP_STRIDE, N_KV_HEADS, HEAD_DIM
    w1 = phi_w1.reshape(2, 2, S, Dh, Dh)
    eye_c = jnp.eye(2, dtype=F32)
    eye_h = jnp.eye(H, dtype=F32)
    wbig = jnp.einsum('cajde,xc,yh->jxydache', w1, eye_c, eye_h).reshape(S * 2 * H * Dh, 2 * 2 * H * Dh)
    pe = jnp.transpose(phi_pe.reshape(2, 2, S, Dh), (1, 2, 0, 3))
    pe_rows = jnp.broadcast_to(pe[:, :, :, None, :], (2, S, 2, H, Dh)).reshape(2, S * 2 * H * Dh)
    n = 2 * H * Dh
    pe_w = (jnp.dot(pe_rows[0], wbig[:, :n], precision=HIGHEST) + jnp.dot(pe_rows[1], wbig[:, n:], precision=HIGHEST))
    b1 = jnp.broadcast_to(phi_b1[:, None, :], (2, H, Dh)).reshape(1, n) + pe_w[None, :]
    w2 = jnp.einsum('cef,cx,hy->chexyf', phi_w2, eye_c, eye_h).reshape(n, n)
    b2 = jnp.broadcast_to(phi_b2[:, None, :], (2, H, Dh)).reshape(1, n)
    return wbig.astype(BF16), b1, w2.astype(BF16), b2


def _compress_in_kernel(x_ref, w_ref, z_ref):
    z_ref[0] = jnp.dot(x_ref[0].astype(BF16), w_ref[...], preferred_element_type=F32)


def _compress_in(x2, tables):
    wbig = tables[0]
    N2 = wbig.shape[1]
    B, n, K = x2.shape
    tr = math.gcd(n, 256)
    return pl.pallas_call(
        _compress_in_kernel,
        out_shape=jax.ShapeDtypeStruct((B, n, N2), F32),
        grid=(B, n // tr),
        in_specs=[pl.BlockSpec((1, tr, K), lambda b, i: (b, i, 0)),
                  pl.BlockSpec((K, N2), lambda b, i: (0, 0))],
        out_specs=pl.BlockSpec((1, tr, N2), lambda b, i: (b, i, 0)),
        compiler_params=_cparams("parallel", "parallel"),
        name="compress_in",
    )(x2, wbig)


def _compress_in_paged_kernel(pt_ref, *refs, n_pg):
    x_refs = refs[:n_pg]
    w_ref, z_ref = refs[n_pg:n_pg + 2]
    scratch = refs[n_pg + 2:]
    n_slab = D_KV // LANE
    pg_part = n_pg // PAGE_PARTS
    cpp = PAGE_SIZE // CMP_STRIDE
    rows = pg_part * cpp
    for part in range(PAGE_PARTS):
        s_refs = scratch[part * n_slab:(part + 1) * n_slab]
        for k in range(pg_part):
            t = x_refs[part * pg_part + k][0].reshape(D_KV, PAGE_SIZE).T
            for c, s_ref in enumerate(s_refs):
                for n in range(cpp):
                    r0 = (k * cpp + n) * CHUNK_PITCH
                    s_ref[r0:r0 + CMP_STRIDE, :] = t[n * CMP_STRIDE:(n + 1) * CMP_STRIDE, c * LANE:(c + 1) * LANE]
        z = jnp.zeros((rows, w_ref.shape[1]), F32)
        for j in range(CMP_STRIDE):
            xj = jnp.concatenate([s_ref[pl.ds(j, rows, stride=CHUNK_PITCH), :] for s_ref in s_refs], axis=1)
            z = z + jnp.dot(xj.astype(BF16), w_ref[j * D_KV:(j + 1) * D_KV, :], preferred_element_type=F32)
        z_ref[0, part * rows:(part + 1) * rows, :] = z


def _compress_in_paged(pool_t, page_table, tables):
    wbig = tables[0]
    N2 = wbig.shape[1]
    K = wbig.shape[0]
    B, n_pages = page_table.shape
    n_pg = math.gcd(n_pages, PAGES_PER_STEP)
    rows = n_pg * PAGE_SIZE // CMP_STRIDE
    page_spec = lambda k: pl.BlockSpec((1,) + pool_t.shape[1:],
                                       lambda b, i, pt, k=k: (pt[b, i * n_pg + k], 0, 0, 0, 0))
    grid_spec = pltpu.PrefetchScalarGridSpec(
        num_scalar_prefetch=1,
        grid=(B, n_pages // n_pg),
        in_specs=[page_spec(k) for k in range(n_pg)] + [pl.BlockSpec((K, N2), lambda b, i, pt: (0, 0))],
        out_specs=pl.BlockSpec((1, rows, N2), lambda b, i, pt: (b, i, 0)),
        scratch_shapes=[pltpu.VMEM((rows // PAGE_PARTS * CHUNK_PITCH, LANE), F32)
                        for _ in range(PAGE_PARTS * (D_KV // LANE))],
    )
    return pl.pallas_call(
        functools.partial(_compress_in_paged_kernel, n_pg=n_pg),
        out_shape=jax.ShapeDtypeStruct((B, n_pages * PAGE_SIZE // CMP_STRIDE, N2), F32),
        grid_spec=grid_spec,
        compiler_params=_cparams("arbitrary", "arbitrary"),
        name="compress_in_paged",
    )(page_table, *([pool_t] * n_pg), wbig)


def _compress_out_kernel(*refs):
    z_refs, (b1_ref, w2_ref, b2_ref, o_ref) = refs[:-4], refs[-4:]
    z = jnp.concatenate([z_ref[0] for z_ref in z_refs], axis=0)
    n = z.shape[-1] // 2
    rows = z.shape[0]
    second = pltpu.roll(z[:, n:], rows - 1, axis=0)
    hdn = jax.nn.gelu(z[:, :n] + second + b1_ref[...])
    o_ref[0, :rows, :] = jnp.dot(hdn.astype(BF16), w2_ref[...], preferred_element_type=F32) + b2_ref[...]
    if o_ref.shape[1] > rows:
        o_ref[0, rows:, :] = jnp.zeros((o_ref.shape[1] - rows, n), F32)


def _compress_out(zs, tables, n_out):
    _, b1, w2, b2 = tables
    B, _, N2 = zs[0].shape
    return pl.pallas_call(
        _compress_out_kernel,
        out_shape=jax.ShapeDtypeStruct((B, n_out, N2 // 2), F32),
        grid=(B,),
        in_specs=[pl.BlockSpec((1, z.shape[1], N2), lambda b: (b, 0, 0)) for z in zs] + [
                  pl.BlockSpec((1, N2 // 2), lambda b: (0, 0)),
                  pl.BlockSpec((N2 // 2, N2 // 2), lambda b: (0, 0)),
                  pl.BlockSpec((1, N2 // 2), lambda b: (0, 0))],
        out_specs=pl.BlockSpec((1, n_out, N2 // 2), lambda b: (b, 0, 0)),
        compiler_params=_cparams("parallel"),
        name="compress_out",
    )(*zs, b1, w2, b2)


def _rel_bucket(dist):
    n = jnp.maximum(dist, 0)
    max_exact = NUM_BUCKETS // 2
    nf = jnp.maximum(n, 1).astype(F32)
    large = max_exact + (jnp.log(nf / max_exact) / math.log(REL_MAX_DIST / max_exact)
                         * (NUM_BUCKETS - max_exact)).astype(jnp.int32)
    large = jnp.minimum(large, NUM_BUCKETS - 1)
    return jnp.where(n < max_exact, n, large)


def _bias_by_distance(rel_bias, n_max):
    onehot = (_rel_bucket(jnp.arange(n_max))[None, :] == jnp.arange(NUM_BUCKETS)[:, None]).astype(F32)
    return jnp.dot(jnp.transpose(rel_bias.astype(F32)), onehot, precision=HIGHEST)


def _shifted_chunks(bias_n, pad, n_chunks, width):
    n = min(bias_n.shape[1], n_chunks * width - pad)
    ext = jnp.concatenate([jnp.broadcast_to(bias_n[:, :1], (N_HEADS, pad)), bias_n[:, :n],
                           jnp.zeros((N_HEADS, n_chunks * width - pad - n), F32)], axis=1)
    return ext.reshape(N_HEADS, n_chunks, width)


def _bias_tables_kernel(ed_ref, ec_ref, tzs_ref, tzw_ref, cmp_ref, *, tq, tk, n_qt):
    n_ds, n_dw, n_j = tzs_ref.shape[1] - 1, tzw_ref.shape[1] - 1, cmp_ref.shape[1] // 8
    tzs_ref[0, n_ds] = jnp.full((tk, tq), NEG, F32)
    tzw_ref[0, n_dw] = jnp.full((tk, tq), NEG, F32)
    w = tq + tk
    c = lax.broadcasted_iota(jnp.int32, (tk, tq), 0)
    r = lax.broadcasted_iota(jnp.int32, (tk, tq), 1)
    for d in range(n_ds):
        v = jnp.concatenate([ed_ref[0, d:d + 1, :], ed_ref[0, d + 1:d + 2, :]], axis=1)
        t = pltpu.roll(jnp.broadcast_to(v, (tk, w)), w - (tk - 1), axis=1, stride=1, stride_axis=0)[:, :tq]
        dist = d * tk + r - c
        tzs_ref[0, d] = jnp.where(dist >= 0, t, NEG)
        if d < n_dw:
            tzw_ref[0, d] = jnp.where((dist >= 0) & (dist <= WINDOW), t, NEG)
    for j in range(n_j):
        dd = n_qt - 1 - j
        c0, c1 = max(dd, 0), max(dd + 1, 0)
        v = jnp.concatenate([ec_ref[0, c0:c0 + 1, :], ec_ref[0, c1:c1 + 1, :]], axis=1)
        t = pltpu.roll(jnp.broadcast_to(v, (8, w)), w - 7 * CMP_STRIDE, axis=1, stride=CMP_STRIDE, stride_axis=0)
        dist = tq * dd + r[:8] - CMP_STRIDE * c[:8] - (CMP_BLOCK - 1)
        cmp_ref[0, j * 8:(j + 1) * 8, :] = jnp.where(dist >= 0, t[:, :tq], NEG)


def _bias_tables(bias_n, n_qt, n_rb, n_ds, n_dw, tq, tk):
    assert tq == tk == 8 * CMP_STRIDE and n_dw <= n_ds
    n_j = n_rb + n_qt - 1
    ed = _shifted_chunks(bias_n, tk - 1, n_ds + 1, tq)
    ec = _shifted_chunks(bias_n, 7 * CMP_STRIDE + CMP_BLOCK - 1, n_qt + 1, tq)
    head = lambda a: pl.BlockSpec((1,) + a.shape[1:], lambda h: (h,) + (0,) * (a.ndim - 1))
    outs = (jax.ShapeDtypeStruct((N_HEADS, n_ds + 1, tk, tq), F32),
            jax.ShapeDtypeStruct((N_HEADS, n_dw + 1, tk, tq), F32),
            jax.ShapeDtypeStruct((N_HEADS, n_j * 8, tq), F32))
    tzs, tzw, cmp = pl.pallas_call(
        functools.partial(_bias_tables_kernel, tq=tq, tk=tk, n_qt=n_qt),
        out_shape=outs,
        grid=(N_HEADS,),
        in_specs=[head(ed), head(ec)],
        out_specs=tuple(head(o) for o in outs),
        compiler_params=_cparams("parallel"),
        name="bias_tables",
    )(ed, ec)
    grp = lambda a: a.reshape((N_KV_HEADS, GQA) + a.shape[1:])
    return grp(tzs), grp(tzw), cmp


def _pool_matrix(n_cmp_pad, n_blk_pad):
    r = SEL_BLOCK // CMP_STRIDE
    i = np.arange(n_cmp_pad)[None, :]
    j = np.arange(n_blk_pad)[:, None]
    return ((i >= r * j - 1) & (i <= r * j + r - 1)).astype(np.float32)


def _cmp_select_kernel(q_ref, k_ref, vt_ref, bias_ref, pool_ref, o_ref, sel_ref, *, tq):
    qt = pl.program_id(2)
    n_qt = pl.num_programs(2)
    q = q_ref[0, 0].reshape(GQA * tq, HEAD_DIM)
    k = k_ref[0, 0]
    nc = k.shape[0]
    s = _nt_dot(k, q)
    row0 = pl.multiple_of((n_qt - 1 - qt) * 8, 8)
    s = s + jnp.concatenate([bias_ref[g, pl.ds(row0, nc), :] for g in range(GQA)], axis=-1)
    m = jnp.maximum(jnp.max(s, axis=0, keepdims=True), 0.5 * NEG)
    p = jnp.exp(s - m)
    p = p * (1.0 / jnp.maximum(jnp.sum(p, axis=0, keepdims=True), 1e-30))
    ot = jnp.dot(vt_ref[0, 0], p.astype(BF16), preferred_element_type=F32)
    o_ref[0] = jnp.concatenate([ot[:, g * tq:(g + 1) * tq].T for g in range(GQA)], axis=-1)
    imp = p[:, 0:tq]
    for g in range(1, GQA):
        imp = imp + p[:, g * tq:(g + 1) * tq]
    sb = jnp.dot(pool_ref[...], imp, precision=HIGHEST, preferred_element_type=F32)
    nb = sb.shape[0]
    blk = lax.broadcasted_iota(jnp.int32, (nb, tq), 0)
    cur = (qt * tq + lax.broadcasted_iota(jnp.int32, (nb, tq), 1)) // SEL_BLOCK
    causal = blk <= cur
    forced = (blk == 0) | (blk == cur) | (blk == cur - 1)
    sc = jnp.where(forced & causal, 1e4, jnp.where(causal, sb, -1.0))
    groups = [sc[r:r + 8] for r in range(0, nb, 8)]
    sub = lax.broadcasted_iota(jnp.int32, (8, tq), 0)
    ranks = [jnp.zeros((8, tq), F32) for _ in groups]
    for i in range(nb):
        row = sc[i:i + 1, :]
        for gi, grp in enumerate(groups):
            if gi * 8 > i:
                ahead = row >= grp
            elif gi * 8 + 7 < i:
                ahead = row > grp
            else:
                ahead = (row > grp) | ((row == grp) & (sub > i - gi * 8))
            ranks[gi] = ranks[gi] + jnp.where(ahead, 1.0, 0.0)
    rank = jnp.concatenate(ranks, axis=0)
    sel_ref[0, 0] = jnp.where((rank < N_SEL) & causal, 0.0, NEG)


def _cmp_select_prompt(q5, kc, vct, bias_tab, pool):
    B, _, _, T, _ = q5.shape
    NC = kc.shape[2]
    NB = pool.shape[0]
    R = bias_tab.shape[1]
    tq = ATT_TQ
    return pl.pallas_call(
        functools.partial(_cmp_select_kernel, tq=tq),
        out_shape=(jax.ShapeDtypeStruct((B, T, D_ATT), F32),
                   jax.ShapeDtypeStruct((B, N_KV_HEADS, NB, T), F32)),
        grid=(B, N_KV_HEADS, T // tq),
        in_specs=[pl.BlockSpec((1, 1, GQA, tq, HEAD_DIM), lambda b, h, i: (b, h, 0, i, 0)),
                  pl.BlockSpec((1, 1, NC, HEAD_DIM), lambda b, h, i: (b, h, 0, 0)),
                  pl.BlockSpec((1, 1, HEAD_DIM, NC), lambda b, h, i: (b, h, 0, 0)),
                  pl.BlockSpec((GQA, R, tq), lambda b, h, i: (h, 0, 0)),
                  pl.BlockSpec((NB, NC), lambda b, h, i: (0, 0))],
        out_specs=(pl.BlockSpec((1, tq, GQA * HEAD_DIM), lambda b, h, i: (b, i, h)),
                   pl.BlockSpec((1, 1, NB, tq), lambda b, h, i: (b, h, 0, i))),
        compiler_params=_cparams("parallel", "parallel", "parallel"),
        name="cmp_select_prompt",
    )(q5, kc, vct, bias_tab, pool)


def _sel_win_kernel(q_ref, ks_ref, vst_ref, kw_ref, vwt_ref, sel_ref, tzs_ref, tzw_ref, os_ref, ow_ref, *, tq):
    tk = ATT_TK
    qt = pl.program_id(2)
    q = q_ref[0, 0].reshape(GQA * tq, HEAD_DIM)
    width = GQA * tq
    per_tile = tk // SEL_BLOCK

    def make_sweep(k_ref, vt_ref, tz_ref, use_sel, n_chains, single_trip):
        n_d = tz_ref.shape[2] - 1

        def scores(kt, hi):
            pad = kt > hi
            kt = jnp.minimum(kt, hi)
            off = pl.multiple_of(kt * tk, tk)
            k = k_ref[0, 0, pl.ds(off, tk), :]
            d = jnp.where(pad, n_d, jnp.minimum(qt - kt, n_d - 1))
            bias = [tz_ref[0, g, d] for g in range(GQA)]
            if use_sel:
                rows = sel_ref[0, 0, pl.ds(kt * per_tile, per_tile), :]
                selb = jnp.concatenate([jnp.broadcast_to(rows[i:i + 1], (SEL_BLOCK, tq))
                                        for i in range(per_tile)], axis=0)
                bias = [b + selb for b in bias]
            return _nt_dot(k, q) + jnp.concatenate(bias, axis=1)

        def values_t(kt, lo, hi):
            off = pl.multiple_of(jnp.clip(kt, lo, hi) * tk, tk)
            return vt_ref[0, 0, :, pl.ds(off, tk)]

        def sweep(lo, hi):
            n_trips = (hi - lo + n_chains) // n_chains

            def first_trip():
                out = []
                for c in range(n_chains):
                    s = scores(lo + c, hi)
                    m = jnp.maximum(jnp.max(s, axis=0, keepdims=True), 0.5 * NEG)
                    p = jnp.exp(s - m)
                    out.append((m, jnp.sum(p, axis=0, keepdims=True), jnp.zeros((HEAD_DIM, width), F32),
                                jnp.ones((1, width), F32), p.astype(BF16)))
                return tuple(out)

            def trip(i, chains):
                kt = lo + n_chains * i
                pv = [jnp.dot(values_t(kt - n_chains + c, lo, hi), chains[c][4], preferred_element_type=F32)
                      for c in range(n_chains)]
                ss = [scores(kt + c, hi) for c in range(n_chains)]
                out = []
                for c in range(n_chains):
                    m, l, acc, alpha_prev, _ = chains[c]
                    m_new = jnp.maximum(m, jnp.max(ss[c], axis=0, keepdims=True))
                    alpha = jnp.exp(m - m_new)
                    p = jnp.exp(ss[c] - m_new)
                    l = alpha * l + jnp.sum(p, axis=0, keepdims=True)
                    out.append((m_new, l, alpha_prev * acc + pv[c], alpha, p.astype(BF16)))
                return tuple(out)

            if single_trip:
                done = []
                for c in range(n_chains):
                    s = scores(lo + c, hi)
                    m = jnp.maximum(jnp.max(s, axis=0, keepdims=True), 0.5 * NEG)
                    p = jnp.exp(s - m)
                    done.append((m, jnp.sum(p, axis=0, keepdims=True),
                                 jnp.dot(values_t(lo + c, lo, hi), p.astype(BF16), preferred_element_type=F32)))
            else:
                chains = lax.fori_loop(1, n_trips, trip, first_trip())
                kt_last = lo + n_chains * (n_trips - 1)
                done = []
                for c in range(n_chains):
                    m, l, acc, alpha, p = chains[c]
                    done.append((m, l, alpha * acc + jnp.dot(values_t(kt_last + c, lo, hi), p,
                                                              preferred_element_type=F32)))
            m_all = functools.reduce(jnp.maximum, [m for m, _, _ in done])
            num = den = 0.0
            for m, l, acc in done:
                e = jnp.exp(m - m_all)
                num = num + acc * e
                den = den + l * e
            o = num / jnp.maximum(den, 1e-30)
            return jnp.concatenate([o[:, g * tq:(g + 1) * tq].T for g in range(GQA)], axis=-1)
        return sweep

    n_win = tzw_ref.shape[2] - 1
    os_ref[0] = make_sweep(ks_ref, vst_ref, tzs_ref, True, SEL_CHAINS, False)(0, qt)
    ow_ref[0] = make_sweep(kw_ref, vwt_ref, tzw_ref, False, n_win, True)(jnp.maximum(qt - (n_win - 1), 0), qt)


def _sel_win_prompt(q5, ks, vst, kw, vwt, sel, tzs, tzw):
    B, _, _, T, _ = q5.shape
    NB = sel.shape[2]
    tq = ATT_TQ
    k_spec = pl.BlockSpec((1, 1, T, HEAD_DIM), lambda b, h, i: (b, h, 0, 0))
    vt_spec = pl.BlockSpec((1, 1, HEAD_DIM, T), lambda b, h, i: (b, h, 0, 0))
    tz_spec = lambda tz: pl.BlockSpec((1,) + tz.shape[1:], lambda b, h, i: (h, 0, 0, 0, 0))
    o_spec = pl.BlockSpec((1, tq, GQA * HEAD_DIM), lambda b, h, i: (b, i, h))
    return pl.pallas_call(
        functools.partial(_sel_win_kernel, tq=tq),
        out_shape=(jax.ShapeDtypeStruct((B, T, D_ATT), F32), jax.ShapeDtypeStruct((B, T, D_ATT), F32)),
        grid=(B, N_KV_HEADS, T // tq),
        in_specs=[pl.BlockSpec((1, 1, GQA, tq, HEAD_DIM), lambda b, h, i: (b, h, 0, i, 0)),
                  k_spec, vt_spec, k_spec, vt_spec,
                  pl.BlockSpec((1, 1, NB, tq), lambda b, h, i: (b, h, 0, i)),
                  tz_spec(tzs), tz_spec(tzw)],
        out_specs=(o_spec, o_spec),
        compiler_params=_cparams("parallel", "parallel", "parallel"),
        name="sel_win_prompt",
    )(q5, ks, vst, kw, vwt, sel, tzs, tzw)


def _gate_expand_matrix():
    m = np.zeros((3, 2 * LANE, D_ATT), np.float32)
    for r in range(3):
        for h in range(N_HEADS):
            m[r, h * 3 + r, h * HEAD_DIM:(h + 1) * HEAD_DIM] = 1.0
            m[r, LANE + h * 3 + r, h * HEAD_DIM:(h + 1) * HEAD_DIM] = 1.0
    return m


def _split_bf16(x):
    hi = x.astype(BF16)
    return hi, (x - hi.astype(F32)).astype(BF16)


def _post_mixer_kernel(y_ref, u_ref, oc_ref, os_ref, ow_ref, g_ref, x_ref, gate_ref, sh_ref, sc_ref,
                       dskip_ref, wglu_ref, bglu_ref, gexp_ref, wout_ref, lng_ref, lnb_ref,
                       wr_ref, br_ref, x1_ref, hm_ref, te_ref, tw_ref):
    y = y_ref[0] + dskip_ref[...] * u_ref[0]
    gl = jax.nn.gelu(y)
    ssm = gl * jax.nn.sigmoid(jnp.dot(gl.astype(BF16), wglu_ref[...], preferred_element_type=F32)
                              + bglu_ref[...])
    sg = jnp.concatenate(_split_bf16(jax.nn.sigmoid(g_ref[0])), axis=1)
    att = jnp.zeros_like(oc_ref[0])
    for r, o_ref in enumerate((oc_ref, os_ref, ow_ref)):
        att = att + jnp.dot(sg, gexp_ref[r], preferred_element_type=F32) * o_ref[0]
    h = (jnp.dot(ssm.astype(BF16), wout_ref[:D_SSM, :], preferred_element_type=F32)
         + jnp.dot(att.astype(BF16), wout_ref[D_SSM:, :], preferred_element_type=F32))
    z = DN_ALPHA * x_ref[0] + gate_ref[0] * h
    x1 = _layer_norm(z) * lng_ref[...] + lnb_ref[...]
    x1_ref[0] = x1
    hm = _layer_norm(x1) * (1.0 + sc_ref[0]) + sh_ref[0]
    hm_ref[0] = hm
    hm_hi, hm_lo = _split_bf16(hm)
    logits = (jnp.dot(hm_hi, wr_ref[0], preferred_element_type=F32)
              + jnp.dot(hm_lo, wr_ref[0], preferred_element_type=F32)
              + jnp.dot(hm_hi, wr_ref[1], preferred_element_type=F32)) + br_ref[...]
    lane = lax.broadcasted_iota(jnp.int32, logits.shape, 1)
    work = jnp.where(lane < N_EXPERTS, logits, -jnp.inf)
    te = jnp.zeros(logits.shape, jnp.int32)
    tv = jnp.zeros(logits.shape, F32)
    for k in range(TOP_K):
        best = jnp.max(work, axis=-1, keepdims=True)
        arg = jnp.min(jnp.where(work == best, lane, LANE), axis=-1, keepdims=True)
        te = jnp.where(lane == k, arg, te)
        tv = jnp.where(lane == k, best, tv)
        work = jnp.where(lane == arg, -jnp.inf, work)
    ex = jnp.where(lane < TOP_K, jnp.exp(tv - tv[:, 0:1]), 0.0)
    te_ref[0] = te
    tw_ref[0] = ex / jnp.sum(ex, axis=-1, keepdims=True)


def _post_mixer(y, u, oc, osel, ow, g, x, gate, shift, scale, w, tm):
    B, T, D = x.shape
    R = gate.shape[1]
    rb = 1 if R == 1 else tm
    mod_map = (lambda b, i: (b, 0, 0)) if R == 1 else (lambda b, i: (b, i, 0))
    row = lambda n: pl.BlockSpec((1, tm, n), lambda b, i: (b, i, 0))
    mod = pl.BlockSpec((1, rb, D), mod_map)
    full = lambda a: pl.BlockSpec(a.shape, lambda b, i: (0,) * a.ndim)
    consts = (w['d_skip'], w['w_glu'], w['b_glu'], w['gexp'], w['w_out'], w['ln1_g'], w['ln1_b'],
              w['w_router'], w['b_router'])
    return pl.pallas_call(
        _post_mixer_kernel,
        out_shape=(jax.ShapeDtypeStruct((B, T, D), F32), jax.ShapeDtypeStruct((B, T, D), F32),
                   jax.ShapeDtypeStruct((B, T, LANE), jnp.int32), jax.ShapeDtypeStruct((B, T, LANE), F32)),
        grid=(B, T // tm),
        in_specs=[row(D_SSM), row(D_SSM), row(D_ATT), row(D_ATT), row(D_ATT), row(LANE), row(D),
                  mod, mod, mod] + [full(a) for a in consts],
        out_specs=(row(D), row(D), row(LANE), row(LANE)),
        compiler_params=_cparams("parallel", "parallel"),
        name="post_mixer",
    )(y, u, oc, osel, ow, g, x, gate, shift, scale, *consts)


def _expert_kernel(e_ref, blk_ref, lo_ref, hi_ref, first_ref, x_ref, wgu_ref, bgu_ref, wd_ref, bd_ref, o_ref,
                   wgu_s, wd_s):
    i = pl.program_id(0)
    fresh = (i == 0) | (e_ref[i] != e_ref[jnp.maximum(i - 1, 0)])

    @pl.when(fresh)
    def _():
        wgu_s[...] = wgu_ref[0].astype(BF16)
        wd_s[...] = wd_ref[0].astype(BF16)

    @pl.when(first_ref[i] == 1)
    def _():
        o_ref[...] = jnp.zeros_like(o_ref)

    @pl.when(hi_ref[i] > lo_ref[i])
    def _():
        gu = jnp.dot(x_ref[...].astype(BF16), wgu_s[...], preferred_element_type=F32) + bgu_ref[0]
        gate = jnp.minimum(gu[:, :D_FF], SWIGLU_LIMIT)
        up = jnp.clip(gu[:, D_FF:], -SWIGLU_LIMIT, SWIGLU_LIMIT)
        hh = (up + 1.0) * gate * jax.nn.sigmoid(SWIGLU_ALPHA * gate)
        y = jnp.dot(hh.astype(BF16), wd_s[...], preferred_element_type=F32) + bd_ref[0]
        row = blk_ref[i] * MOE_ROWS + lax.broadcasted_iota(jnp.int32, (MOE_ROWS, 1), 0)
        o_ref[...] = jnp.where((row >= lo_ref[i]) & (row < hi_ref[i]), y, o_ref[...])


def _experts(xb, items, w_gate_up, b_gate_up, w_down, b_down):
    rows, D = xb.shape
    n_items = items[0].shape[0]
    wmap = lambda i, e, blk, lo, hi, first: (e[i], 0, 0)
    rmap = lambda i, e, blk, lo, hi, first: (blk[i], 0)
    grid_spec = pltpu.PrefetchScalarGridSpec(
        num_scalar_prefetch=5,
        grid=(n_items,),
        in_specs=[pl.BlockSpec((MOE_ROWS, D), rmap),
                  pl.BlockSpec((1, D, 2 * D_FF), wmap),
                  pl.BlockSpec((1, 1, 2 * D_FF), wmap),
                  pl.BlockSpec((1, D_FF, D), wmap),
                  pl.BlockSpec((1, 1, D), wmap)],
        out_specs=pl.BlockSpec((MOE_ROWS, D), rmap),
        scratch_shapes=[pltpu.VMEM((D, 2 * D_FF), BF16), pltpu.VMEM((D_FF, D), BF16)],
    )
    return pl.pallas_call(
        _expert_kernel,
        out_shape=jax.ShapeDtypeStruct((rows, D), F32),
        grid_spec=grid_spec,
        compiler_params=_cparams("arbitrary"),
        name="moe_experts",
    )(*items, xb, w_gate_up, b_gate_up.reshape(N_EXPERTS, 1, 2 * D_FF), w_down,
      b_down.reshape(N_EXPERTS, 1, D))


def _moe_dispatch(top_e, n):
    blk = MOE_ROWS
    nk = n * TOP_K
    cb = 128
    assert nk % cb == 0
    e = top_e.reshape(-1)
    oh = (jnp.arange(N_EXPERTS)[:, None] == e[None, :]).astype(BF16).reshape(N_EXPERTS, nk // cb, cb)
    before = jnp.asarray(np.triu(np.ones((cb, cb), np.float32), 1), dtype=BF16)
    within = jnp.einsum('ebj,ji->ebi', oh, before, preferred_element_type=F32)
    blk_tot = jnp.sum(oh.astype(F32), axis=2)
    blk_off = jnp.cumsum(blk_tot, axis=1) - blk_tot
    counts = jnp.sum(blk_tot, axis=1)
    start = jnp.cumsum(counts) - counts
    dest = jnp.sum((within + (blk_off + start[:, None])[:, :, None]) * oh.astype(F32), axis=0)
    dest = dest.reshape(nk).astype(jnp.int32)
    order = jnp.argsort(dest)
    n_blk = -(-nk // blk)
    row_tok = jnp.concatenate([(order // TOP_K).astype(jnp.int32), jnp.full((n_blk * blk - nk,), n, jnp.int32)])
    counts_i, start_i = counts.astype(jnp.int32), start.astype(jnp.int32)
    first_b = start_i // blk
    last_b = (start_i + counts_i - 1) // blk
    n_it = jnp.where(counts_i > 0, last_b - first_b + 1, 0)
    it_end = jnp.cumsum(n_it)
    it_start = it_end - n_it
    n_items = n_blk + N_EXPERTS - 1
    i = jnp.arange(n_items)
    live = i < it_end[-1]
    it_e = jnp.minimum(jnp.sum(it_end[None, :] <= i[:, None], axis=1), N_EXPERTS - 1)
    it_blk = jnp.where(live, first_b[it_e] + i - it_start[it_e], n_blk - 1)
    it_lo = jnp.where(live, start_i[it_e], 0)
    it_hi = jnp.where(live, start_i[it_e] + counts_i[it_e], 0)
    it_first = jnp.concatenate([jnp.ones((1,), jnp.int32), (it_blk[1:] != it_blk[:-1]).astype(jnp.int32)])
    items = tuple(a.astype(jnp.int32) for a in (it_e, it_blk, it_lo, it_hi, it_first))
    return row_tok, dest.reshape(n, TOP_K), items


def _final_kernel(x_ref, y0_ref, y1_ref, y2_ref, y3_ref, tw_ref, gate_ref, lng_ref, lnb_ref, o_ref):
    tw = tw_ref[0]
    y = jnp.zeros_like(x_ref[0])
    for k, y_ref in enumerate((y0_ref, y1_ref, y2_ref, y3_ref)):
        y = y + tw[:, k:k + 1] * y_ref[0]
    z = DN_ALPHA * x_ref[0] + gate_ref[0] * y
    o_ref[0] = _layer_norm(z) * lng_ref[...] + lnb_ref[...]


def _final(x1, ys, tw, gate, ln_g, ln_b, tm):
    B, T, D = x1.shape
    R = gate.shape[1]
    rb = 1 if R == 1 else tm
    mod_map = (lambda b, i: (b, 0, 0)) if R == 1 else (lambda b, i: (b, i, 0))
    row = lambda n: pl.BlockSpec((1, tm, n), lambda b, i: (b, i, 0))
    vec = pl.BlockSpec((1, D), lambda b, i: (0, 0))
    return pl.pallas_call(
        _final_kernel,
        out_shape=jax.ShapeDtypeStruct((B, T, D), F32),
        grid=(B, T // tm),
        in_specs=[row(D), row(D), row(D), row(D), row(D), row(LANE),
                  pl.BlockSpec((1, rb, D), mod_map), vec, vec],
        out_specs=row(D),
        compiler_params=_cparams("parallel", "parallel"),
        name="moe_combine_ln",
    )(x1, *ys, tw, gate, ln_g, ln_b)


def _cmp_select_step_kernel(q_ref, kv_ref, bias_ref, pool_ref, o_ref, idx_ref, *, n_cmp, n_blk, q_pos):
    q = q_ref[0].astype(BF16)
    ncp = kv_ref.shape[1]
    nbp = pool_ref.shape[1]
    hd = HEAD_DIM
    kv = kv_ref[0]
    kb = [kv[:, h * hd:(h + 1) * hd].astype(BF16) for h in range(N_KV_HEADS)]
    vb = [kv[:, (N_KV_HEADS + h) * hd:(N_KV_HEADS + h + 1) * hd].astype(BF16) for h in range(N_KV_HEADS)]
    row = lax.broadcasted_iota(jnp.int32, (N_HEADS, 1), 0)
    first = row < GQA
    s = jnp.where(first, _nt_dot(q, kb[0]), _nt_dot(q, kb[1])) * (hd ** -0.5)
    s = s + bias_ref[...]
    ci = lax.broadcasted_iota(jnp.int32, (N_HEADS, ncp), 1)
    mask = (ci * CMP_STRIDE + CMP_BLOCK - 1 <= q_pos) & (ci < n_cmp)
    s = jnp.where(mask, s, NEG)
    m = jnp.max(s, axis=-1, keepdims=True)
    p = jnp.where(mask, jnp.exp(s - m), 0.0)
    p = p / jnp.maximum(jnp.sum(p, axis=-1, keepdims=True), 1e-30)
    pb = p.astype(BF16)
    o_ref[0] = jnp.where(first, jnp.dot(pb, vb[0], preferred_element_type=F32),
                         jnp.dot(pb, vb[1], preferred_element_type=F32))
    imp0 = jnp.sum(jnp.where(first, p, 0.0), axis=0, keepdims=True)
    imp1 = jnp.sum(jnp.where(first, 0.0, p), axis=0, keepdims=True)
    imp = jnp.where(first, imp0, imp1)
    sb = jnp.dot(imp, pool_ref[...], precision=HIGHEST, preferred_element_type=F32)
    cur = q_pos // SEL_BLOCK
    bi = lax.broadcasted_iota(jnp.int32, (nbp, nbp), 0)
    bj = lax.broadcasted_iota(jnp.int32, (nbp, nbp), 1)
    blk = lax.broadcasted_iota(jnp.int32, (1, nbp), 1)
    causal = blk <= cur
    forced = (blk == 0) | (blk == cur) | (blk == cur - 1)
    rsel = lax.broadcasted_iota(jnp.int32, (N_SEL, nbp), 0)
    for h in range(N_KV_HEADS):
        sc = jnp.where(forced & causal, 1e4, jnp.where(causal, sb[h * GQA:h * GQA + 1, :], -1.0))
        sc = jnp.where(blk < n_blk, sc, -2.0)
        scb = jnp.broadcast_to(sc, (nbp, nbp))
        col = jnp.sum(jnp.where(bi == bj, scb, 0.0), axis=1, keepdims=True)
        ahead = (col > scb) | ((col == scb) & (bi < bj))
        rank = jnp.sum(ahead.astype(jnp.int32), axis=0, keepdims=True)
        hit = jnp.broadcast_to(rank, (N_SEL, nbp)) == rsel
        idx = jnp.sum(jnp.where(hit, jnp.broadcast_to(blk, (N_SEL, nbp)), 0), axis=1, keepdims=True)
        idx_ref[0, h] = jnp.broadcast_to(idx, (N_SEL, LANE))


def _cmp_select_step(q, ckv, bias, pool, n_cmp, n_blk, q_pos):
    B = q.shape[0]
    NCp = ckv.shape[1]
    return pl.pallas_call(
        functools.partial(_cmp_select_step_kernel, n_cmp=n_cmp, n_blk=n_blk, q_pos=q_pos),
        out_shape=(jax.ShapeDtypeStruct((B, N_HEADS, HEAD_DIM), F32),
                   jax.ShapeDtypeStruct((B, N_KV_HEADS, N_SEL, LANE), jnp.int32)),
        grid=(B,),
        in_specs=[pl.BlockSpec((1, N_HEADS, HEAD_DIM), lambda b: (b, 0, 0)),
                  pl.BlockSpec((1, NCp, D_KV), lambda b: (b, 0, 0)),
                  pl.BlockSpec(bias.shape, lambda b: (0, 0)),
                  pl.BlockSpec(pool.shape, lambda b: (0, 0))],
        out_specs=(pl.BlockSpec((1, N_HEADS, HEAD_DIM), lambda b: (b, 0, 0)),
                   pl.BlockSpec((1, N_KV_HEADS, N_SEL, LANE), lambda b: (b, 0, 0, 0))),
        compiler_params=_cparams("parallel"),
        name="cmp_select_step",
    )(q, ckv, bias, pool)


def _sel_step_kernel(pg_ref, idx_ref, q_ref, *refs, n_past, q_pos):
    page_refs = refs[:N_SEL]
    new_ref, bias_ref, kpos_ref, o_ref = refs[N_SEL:]
    b, h = pl.program_id(0), pl.program_id(1)
    base = (b * N_KV_HEADS + h) * N_SEL
    kts, vts = [], []
    for j in range(N_SEL):
        is_new = idx_ref[base + j] >= n_past
        kts.append(jnp.where(is_new, new_ref[0, 0, 0], page_refs[j][0, 0, 0]))
        vts.append(jnp.where(is_new, new_ref[0, 1, 0], page_refs[j][0, 1, 0]))
    kt = jnp.concatenate(kts, axis=1).astype(BF16)
    vt = jnp.concatenate(vts, axis=1).astype(BF16)
    s = jnp.dot(q_ref[0].astype(BF16), kt, preferred_element_type=F32) * (HEAD_DIM ** -0.5) + bias_ref[0, 0]
    mask = kpos_ref[0, 0] <= q_pos
    s = jnp.where(mask, s, NEG)
    m = jnp.max(s, axis=-1, keepdims=True)
    p = jnp.where(mask, jnp.exp(s - m), 0.0)
    l = jnp.sum(p, axis=-1, keepdims=True)
    o_ref[0, 0] = _nt_dot(p.astype(BF16), vt) / jnp.maximum(l, 1e-30)


def _sel_step(q, pool_t, new_t, bias_sel, kpos, pages, idx_flat, n_past, q_pos):
    B = q.shape[0]
    nk = N_SEL * PAGE_SIZE
    slot = lambda b, h, j: (b * N_KV_HEADS + h) * N_SEL + j
    page_spec = lambda j: pl.BlockSpec((1, 2, 1, HEAD_DIM, PAGE_SIZE),
                                       lambda b, h, pg, ix, j=j: (pg[slot(b, h, j)], 0, h, 0, 0))
    grid_spec = pltpu.PrefetchScalarGridSpec(
        num_scalar_prefetch=2,
        grid=(B, N_KV_HEADS),
        in_specs=[pl.BlockSpec((1, N_HEADS, HEAD_DIM), lambda b, h, pg, ix: (b, 0, 0))]
        + [page_spec(j) for j in range(N_SEL)]
        + [pl.BlockSpec((1, 2, 1, HEAD_DIM, PAGE_SIZE), lambda b, h, pg, ix: (b, 0, h, 0, 0)),
           pl.BlockSpec((1, 1, N_HEADS, nk), lambda b, h, pg, ix: (b, h, 0, 0)),
           pl.BlockSpec((1, 1, 1, nk), lambda b, h, pg, ix: (b, h, 0, 0))],
        out_specs=pl.BlockSpec((1, 1, N_HEADS, HEAD_DIM), lambda b, h, pg, ix: (b, h, 0, 0)),
    )
    return pl.pallas_call(
        functools.partial(_sel_step_kernel, n_past=n_past, q_pos=q_pos),
        out_shape=jax.ShapeDtypeStruct((B, N_KV_HEADS, N_HEADS, HEAD_DIM), F32),
        grid_spec=grid_spec,
        compiler_params=_cparams("arbitrary", "arbitrary"),
        name="sel_step",
    )(pages, idx_flat, q, *([pool_t] * N_SEL), new_t, bias_sel, kpos)


def _win_step_kernel(q_ref, w_ref, new_ref, bias_ref, bias0_ref, o_ref):
    q = q_ref[0]
    qb = q.astype(BF16)
    row = lax.broadcasted_iota(jnp.int32, (N_HEADS, 1), 0)
    first = row < GQA
    hd = HEAD_DIM
    kt = [w_ref[0, 0, h].astype(BF16) for h in range(N_KV_HEADS)]
    vt = [w_ref[0, 1, h].astype(BF16) for h in range(N_KV_HEADS)]
    dots = [jnp.dot(qb, kt[h], preferred_element_type=F32) for h in range(N_KV_HEADS)]
    s = jnp.where(first, dots[0], dots[1]) * (hd ** -0.5) + bias_ref[...]
    new = new_ref[0]
    kn = jnp.where(first, new[:, 0:hd], new[:, hd:2 * hd])
    vn = jnp.where(first, new[:, 2 * hd:3 * hd], new[:, 3 * hd:])
    sn = jnp.sum(q * kn, axis=-1, keepdims=True) * (hd ** -0.5) + bias0_ref[...]
    m = jnp.maximum(jnp.max(s, axis=-1, keepdims=True), sn)
    p = jnp.exp(s - m)
    pn = jnp.exp(sn - m)
    l = jnp.sum(p, axis=-1, keepdims=True) + pn
    pb = p.astype(BF16)
    acc = jnp.where(first, _nt_dot(pb, vt[0]), _nt_dot(pb, vt[1])) + pn * vn
    o_ref[0] = acc / jnp.maximum(l, 1e-30)


def _win_step(q, win_t, new, bias, bias0):
    B, W = win_t.shape[0], win_t.shape[-1]
    return pl.pallas_call(
        _win_step_kernel,
        out_shape=jax.ShapeDtypeStruct((B, N_HEADS, HEAD_DIM), F32),
        grid=(B,),
        in_specs=[pl.BlockSpec((1, N_HEADS, HEAD_DIM), lambda b: (b, 0, 0)),
                  pl.BlockSpec((1,) + win_t.shape[1:], lambda b: (b, 0, 0, 0, 0)),
                  pl.BlockSpec((1, 1, D_KV), lambda b: (b, 0, 0)),
                  pl.BlockSpec((N_HEADS, W), lambda b: (0, 0)),
                  pl.BlockSpec((N_HEADS, 1), lambda b: (0, 0))],
        out_specs=pl.BlockSpec((1, N_HEADS, HEAD_DIM), lambda b: (b, 0, 0)),
        compiler_params=_cparams("parallel"),
        name="win_step",
    )(q, win_t, new, bias, bias0)


def _split_heads(kv, dtype):
    B, L, _ = kv.shape
    kv5 = kv.reshape(B, L, 2, N_KV_HEADS, HEAD_DIM)
    return (jnp.transpose(kv5[:, :, 0], (0, 2, 1, 3)).astype(dtype),
            jnp.transpose(kv5[:, :, 1], (0, 2, 1, 3)).astype(dtype))


def _nsa_prompt(q5, kvc, ks, vst, kw, vwt, cmp_tab, rel_bias):
    B, T, _ = kvc.shape
    nc = T // CMP_STRIDE
    nb = T // SEL_BLOCK
    ckv = _compress_out([_compress_in(kvc.reshape(B, nc, CMP_STRIDE * D_KV), cmp_tab)], cmp_tab, nc)
    kc, vc = _split_heads(ckv, BF16)
    vct = jnp.transpose(vc, (0, 1, 3, 2))
    bias_n = _bias_by_distance(rel_bias, T)
    n_qt, n_kt = T // ATT_TQ, T // ATT_TK
    n_ds = min(n_kt, -(-(REL_MAX_DIST + ATT_TK - 1) // ATT_TK) + 1)
    n_dw = min(n_kt, WINDOW // ATT_TK + 1)
    tzs, tzw, bias_tab = _bias_tables(bias_n, n_qt, nc // 8, n_ds, n_dw, ATT_TQ, ATT_TK)
    pool = jnp.asarray(_pool_matrix(nc, nb))
    o_cmp, sel = _cmp_select_prompt(q5, kc, vct, bias_tab, pool)
    o_sel, o_win = _sel_win_prompt(q5, ks, vst, kw, vwt, sel, tzs, tzw)
    return o_cmp, o_sel, o_win


def _nsa_sample(q, kvc, kvs, kvw, pool_cmp, pool_sel, win_buf, page_table, cmp_tab, rel_bias):
    B = q.shape[0]
    n_pages = page_table.shape[1]
    past_len = n_pages * PAGE_SIZE
    q_pos = past_len
    lp = -(-(past_len + 1) // SEL_BLOCK) * SEL_BLOCK
    n_cmp = lp // CMP_STRIDE - 1
    n_blk = lp // SEL_BLOCK
    n_past_chunks = past_len // CMP_STRIDE
    n_tail = 8
    assert n_past_chunks + n_tail >= n_cmp + 1
    n_chunks = n_past_chunks + n_tail
    feature_major = lambda pool: jnp.transpose(pool, (0, 2, 3, 4, 1))
    z_past = _compress_in_paged(feature_major(pool_cmp), page_table, cmp_tab)
    tail = jnp.pad(kvc[:, None, :], ((0, 0), (0, n_tail * CMP_STRIDE - 1), (0, 0)))
    z_tail = _compress_in(tail.reshape(B, n_tail, CMP_STRIDE * D_KV), cmp_tab)
    ncp = -(-n_chunks // LANE) * LANE
    nbp = -(-n_blk // LANE) * LANE
    ckv = _compress_out([z_past, z_tail], cmp_tab, ncp)
    bias_n = _bias_by_distance(rel_bias, q_pos + 1)
    n_back = max((n_pages + 1) * PAGE_SIZE, ncp * CMP_STRIDE + CMP_BLOCK)
    back = jnp.concatenate([bias_n[:, ::-1], jnp.broadcast_to(bias_n[:, :1], (N_HEADS, n_back - q_pos - 1))], 1)
    bias_c = back[:, CMP_BLOCK - 1:CMP_BLOCK - 1 + ncp * CMP_STRIDE:CMP_STRIDE]
    pool = jnp.asarray(_pool_matrix(ncp, nbp).T)
    q3 = q.reshape(B, N_HEADS, HEAD_DIM)
    o_cmp, idx = _cmp_select_step(q3, ckv, bias_c, pool, n_cmp, n_blk, q_pos)
    idx = idx[..., 0]
    bpp = PAGE_SIZE // SEL_BLOCK
    n_past = n_pages * bpp
    lpage = idx // bpp
    pages = jnp.take_along_axis(page_table, jnp.minimum(lpage, n_pages - 1).reshape(B, -1), axis=1)
    new_t = jnp.pad(kvs.reshape(B, 2, N_KV_HEADS, HEAD_DIM, 1), ((0, 0),) * 4 + ((0, PAGE_SIZE - 1),))
    bias_page = jnp.transpose(back[:, :(n_pages + 1) * PAGE_SIZE].reshape(N_HEADS, n_pages + 1, PAGE_SIZE),
                              (1, 0, 2))
    bias_sel = jnp.transpose(bias_page[lpage], (0, 1, 3, 2, 4)).reshape(B, N_KV_HEADS, N_HEADS, -1)
    kpos = lpage[..., None] * PAGE_SIZE + jnp.arange(PAGE_SIZE)
    ok = (kpos // SEL_BLOCK == idx[..., None]) & (idx <= q_pos // SEL_BLOCK)[..., None]
    kpos = jnp.where(ok, kpos, q_pos + 1).reshape(B, N_KV_HEADS, 1, -1).astype(jnp.int32)
    o_sel = _sel_step(q3, feature_major(pool_sel), new_t, bias_sel, kpos, pages.reshape(-1).astype(jnp.int32),
                      idx.reshape(-1).astype(jnp.int32), n_past, q_pos)
    o_sel = jnp.concatenate([o_sel[:, h, h * GQA:(h + 1) * GQA] for h in range(N_KV_HEADS)], axis=1)
    wb = win_buf.shape[1]
    bias_w = bias_n[:, 1:wb + 1][:, ::-1]
    o_win = _win_step(q3, feature_major(win_buf), kvw[:, None, :], bias_w, bias_n[:, 0:1])
    return o_cmp.reshape(B, D_ATT), o_sel.reshape(B, D_ATT), o_win.reshape(B, D_ATT)


def kernel(x_prompt, x_sample, cache_cmp_kv, cache_sel_kv, state_win_kv, state_ssm_re, state_ssm_im, page_table,
           c_prompt, c_sample, w_ada, b_ada, w_in, lam_re, lam_im, log_dt, b_re, b_im, c_re, c_im, d_skip,
           w_glu, b_glu, phi_pe, phi_w1, phi_b1, phi_w2, phi_b2, rel_bias, w_out, ln1_g, ln1_b,
           w_router, b_router, w_gate_up, b_gate_up, w_down, b_down, ln2_g, ln2_b):
    assert w_ada.shape[0] == DEPTH == 1
    l = 0
    Bp, T, D = x_prompt.shape
    Bs = x_sample.shape[0]
    kv_tail = (2, N_KV_HEADS, HEAD_DIM)

    n_c = Bp + Bs
    c_all = jnp.pad(jnp.concatenate([c_prompt, c_sample], 0), ((0, -n_c % 8), (0, 0)))
    m_all = _adaln(c_all, w_ada[l], b_ada[l])
    m_p = m_all[:Bp].reshape(Bp, 6, D)
    m_s = m_all[Bp:n_c].reshape(Bs, 6, D)
    mod_p = [m_p[:, i:i + 1, :] for i in range(6)]
    mod_s = [m_s[None, :, i, :] for i in range(6)]

    w_in_pad = jnp.pad(w_in[l], ((0, 0), (0, D_IN_PAD - D_IN))).astype(BF16)
    n_levels = max(1, int(math.log2(T // SSM_CHUNK)))
    ssm_tab = _ssm_tables(lam_re[l], lam_im[l], log_dt[l], b_re[l], b_im[l], c_re[l], c_im[l],
                          SSM_CHUNK, n_levels)
    cmp_tab = _compress_tables(phi_pe[l], phi_w1[l], phi_b1[l], phi_w2[l], phi_b2[l])
    w_post = dict(
        d_skip=d_skip[l].reshape(1, D_SSM), w_glu=w_glu[l].astype(BF16), b_glu=b_glu[l].reshape(1, D_SSM),
        gexp=jnp.asarray(_gate_expand_matrix(), dtype=BF16), w_out=w_out[l].astype(BF16),
        ln1_g=ln1_g[l].reshape(1, D), ln1_b=ln1_b[l].reshape(1, D),
        w_router=jnp.stack(_split_bf16(jnp.pad(w_router[l], ((0, 0), (0, LANE - N_EXPERTS))))),
        b_router=jnp.pad(b_router[l], (0, LANE - N_EXPERTS)).reshape(1, LANE))

    u, q5, kvc, kvs, kvw, g, ks, vst, kw, vwt = _mixer_in(x_prompt, mod_p[0], mod_p[1], w_in_pad, 512, True)
    y_ssm, h_p = _ssm_prompt(u, ssm_tab)
    o_cmp, o_sel, o_win = _nsa_prompt(q5, kvc, ks, vst, kw, vwt, cmp_tab, rel_bias)
    x1_p, hm_p, te_p, tw_p = _post_mixer(y_ssm, u, o_cmp, o_sel, o_win, g, x_prompt,
                                         mod_p[2], mod_p[3], mod_p[4], w_post, tm=512)

    u_s, q_s, kvc_s, kvs_s, kvw_s, g_s = _mixer_in(x_sample.reshape(1, Bs, D), mod_s[0], mod_s[1],
                                                   w_in_pad, Bs, False)
    y_s, h_s = _ssm_sample(u_s[0], state_ssm_re[l], state_ssm_im[l], ssm_tab, c_re[l], c_im[l])
    oc_s, os_s, ow_s = _nsa_sample(q_s[0].astype(F32), kvc_s[0], kvs_s[0], kvw_s[0], cache_cmp_kv[l],
                                   cache_sel_kv[l], state_win_kv[l], page_table, cmp_tab, rel_bias)
    x1_s, hm_s, te_s, tw_s = _post_mixer(y_s[None], u_s, oc_s[None], os_s[None], ow_s[None], g_s,
                                         x_sample.reshape(1, Bs, D), mod_s[2], mod_s[3], mod_s[4],
                                         w_post, tm=Bs)

    n_p = Bp * T
    n_all = n_p + Bs
    hm_all = jnp.concatenate([hm_p.reshape(n_p, D), hm_s.reshape(Bs, D)], 0)
    te_all = jnp.concatenate([te_p.reshape(n_p, LANE), te_s.reshape(Bs, LANE)], 0)[:, :TOP_K]
    row_tok, dest, items = _moe_dispatch(te_all, n_all)
    xb = jnp.concatenate([hm_all, jnp.zeros((1, D), F32)], 0)[row_tok]
    yb = _experts(xb, items, w_gate_up[l], b_gate_up[l], w_down[l], b_down[l])
    ys_p = [yb[dest[:n_p, k]].reshape(Bp, T, D) for k in range(TOP_K)]
    ys_s = [yb[dest[n_p:, k]].reshape(1, Bs, D) for k in range(TOP_K)]
    ln2g, ln2b = ln2_g[l].reshape(1, D), ln2_b[l].reshape(1, D)
    out_p = _final(x1_p, ys_p, tw_p, mod_p[5], ln2g, ln2b, tm=512)
    out_s = _final(x1_s, ys_s, tw_s, mod_s[5], ln2g, ln2b, tm=Bs)

    wlen = min(WINDOW, T)
    win_s = jnp.concatenate([state_win_kv[l], kvw_s[0].reshape(Bs, 1, *kv_tail)], 1)[:, -state_win_kv.shape[2]:]
    p_state = SSM_STATE
    return (out_p, out_s.reshape(Bs, 1, D),
            kvc.reshape(1, Bp, T, *kv_tail), kvc_s[0].reshape(1, Bs, 1, *kv_tail),
            kvs.reshape(1, Bp, T, *kv_tail), kvs_s[0].reshape(1, Bs, 1, *kv_tail),
            kvw[:, T - wlen:].reshape(1, Bp, wlen, *kv_tail), win_s[None],
            h_p[None, ..., :p_state], h_p[None, ..., p_state:],
            h_s[None, ..., :p_state], h_s[None, ..., p_state:])
```

```python
import functools
import math

import numpy as np
import jax
import jax.numpy as jnp
from jax import lax
from jax.experimental import pallas as pl
from jax.experimental.pallas import tpu as pltpu

DEPTH = 1
PAGE_SIZE = 128
D_SSM = 512
SSM_GROUP = 16
N_SSM_GROUPS = D_SSM // SSM_GROUP
SSM_STATE = 64
N_HEADS = 8
HEAD_DIM = 64
N_KV_HEADS = 2
GQA = N_HEADS // N_KV_HEADS
D_ATT = N_HEADS * HEAD_DIM
D_KV = 2 * N_KV_HEADS * HEAD_DIM
CMP_STRIDE = 16
CMP_BLOCK = 2 * CMP_STRIDE
SEL_BLOCK = 64
N_SEL = 16
WINDOW = 512
NUM_BUCKETS = 32
REL_MAX_DIST = 1024
N_EXPERTS = 32
TOP_K = 4
D_FF = 1024
SWIGLU_LIMIT = 7.0
SWIGLU_ALPHA = 1.702
DN_ALPHA = (2 * DEPTH) ** 0.25
D_IN = D_SSM + D_ATT + 3 * D_KV + 3 * N_HEADS
NEG = -1e30
F32 = jnp.float32
BF16 = jnp.bfloat16
HIGHEST = lax.Precision.HIGHEST

LANE = 128
D_IN_PAD = -(-D_IN // LANE) * LANE
SSM_CHUNK = 8
ATT_TQ = 128
ATT_TK = 128
SEL_CHAINS = 4
MOE_ROWS = 256
PAGES_PER_STEP = 32
PAGE_PARTS = 2
CHUNK_PITCH = 24
VMEM_LIMIT = 48 * 1024 * 1024
LN_EPS = 1e-5


def _cparams(*sem):
    return pltpu.CompilerParams(dimension_semantics=sem, vmem_limit_bytes=VMEM_LIMIT)


def _nt_dot(a, b):
    return lax.dot_general(a, b, (((1,), (1,)), ((), ())), preferred_element_type=F32)


def _layer_norm(x):
    mu = jnp.mean(x, axis=-1, keepdims=True)
    xc = x - mu
    var = jnp.mean(xc * xc, axis=-1, keepdims=True)
    return xc * lax.rsqrt(var + LN_EPS)


def _adaln_kernel(c_ref, w_ref, b_ref, o_ref):
    c = c_ref[...]
    s = c * jax.nn.sigmoid(c)
    o_ref[...] = jnp.dot(s, w_ref[...], precision=HIGHEST, preferred_element_type=F32) + b_ref[...]


def _adaln(c, w, b):
    n, d = c.shape
    dout = w.shape[1]
    tn = 1024
    return pl.pallas_call(
        _adaln_kernel,
        out_shape=jax.ShapeDtypeStruct((n, dout), F32),
        grid=(dout // tn,),
        in_specs=[pl.BlockSpec((n, d), lambda j: (0, 0)),
                  pl.BlockSpec((d, tn), lambda j: (0, j)),
                  pl.BlockSpec((1, tn), lambda j: (0, j))],
        out_specs=pl.BlockSpec((n, tn), lambda j: (0, j)),
        compiler_params=_cparams("arbitrary"),
        name="adaln",
    )(c, w, b.reshape(1, dout))


def _mixer_in_kernel(x_ref, sh_ref, sc_ref, w_ref, u_ref, q_ref, kvc_ref, kvs_ref, kvw_ref, g_ref, *att_refs):
    h = _layer_norm(x_ref[0]) * (1.0 + sc_ref[0]) + sh_ref[0]
    z = jnp.dot(h.astype(BF16), w_ref[...], preferred_element_type=F32)
    c0 = D_SSM
    c1 = c0 + D_ATT
    c2 = c1 + D_KV
    c3 = c2 + D_KV
    c4 = c3 + D_KV
    u_ref[0] = z[:, :c0]
    kvc_ref[0] = z[:, c1:c2]
    kvs_ref[0] = z[:, c2:c3]
    kvw_ref[0] = z[:, c3:c4]
    g_ref[0] = z[:, c4:c4 + LANE]
    if not att_refs:
        q_ref[0] = z[:, c0:c1].astype(BF16)
        return
    ks_ref, vst_ref, kw_ref, vwt_ref = att_refs
    hd, half = HEAD_DIM, N_KV_HEADS * HEAD_DIM
    for hq in range(N_HEADS):
        q_ref[0, hq // GQA, hq % GQA] = (z[:, c0 + hq * hd:c0 + (hq + 1) * hd] * (hd ** -0.5)).astype(BF16)
    for k_ref, vt_ref, base in ((ks_ref, vst_ref, c2), (kw_ref, vwt_ref, c3)):
        for hk in range(N_KV_HEADS):
            k_ref[0, hk] = z[:, base + hk * hd:base + (hk + 1) * hd].astype(BF16)
        vt = z[:, base + half:base + 2 * half].T
        vt_ref[0] = vt.reshape(N_KV_HEADS, hd, vt.shape[1]).astype(BF16)


def _mixer_in(x, shift, scale, w_pad, tm, attention_layouts):
    B, T, D = x.shape
    R = shift.shape[1]
    rb = 1 if R == 1 else tm
    mod_map = (lambda b, i: (b, 0, 0)) if R == 1 else (lambda b, i: (b, i, 0))
    row = lambda n: pl.BlockSpec((1, tm, n), lambda b, i: (b, i, 0))
    f32 = lambda n: jax.ShapeDtypeStruct((B, T, n), F32)
    if attention_layouts:
        q_shape = jax.ShapeDtypeStruct((B, N_KV_HEADS, GQA, T, HEAD_DIM), BF16)
        q_spec = pl.BlockSpec((1, N_KV_HEADS, GQA, tm, HEAD_DIM), lambda b, i: (b, 0, 0, i, 0))
        k_shape = jax.ShapeDtypeStruct((B, N_KV_HEADS, T, HEAD_DIM), BF16)
        k_spec = pl.BlockSpec((1, N_KV_HEADS, tm, HEAD_DIM), lambda b, i: (b, 0, i, 0))
        vt_shape = jax.ShapeDtypeStruct((B, N_KV_HEADS, HEAD_DIM, T), BF16)
        vt_spec = pl.BlockSpec((1, N_KV_HEADS, HEAD_DIM, tm), lambda b, i: (b, 0, 0, i))
        extra_shapes, extra_specs = (k_shape, vt_shape, k_shape, vt_shape), (k_spec, vt_spec, k_spec, vt_spec)
    else:
        q_shape, q_spec = jax.ShapeDtypeStruct((B, T, D_ATT), BF16), row(D_ATT)
        extra_shapes, extra_specs = (), ()
    return pl.pallas_call(
        _mixer_in_kernel,
        out_shape=(f32(D_SSM), q_shape, f32(D_KV), f32(D_KV), f32(D_KV), f32(LANE)) + extra_shapes,
        grid=(B, T // tm),
        in_specs=[row(D), pl.BlockSpec((1, rb, D), mod_map), pl.BlockSpec((1, rb, D), mod_map),
                  pl.BlockSpec((D, D_IN_PAD), lambda b, i: (0, 0))],
        out_specs=(row(D_SSM), q_spec, row(D_KV), row(D_KV), row(D_KV), row(LANE)) + extra_specs,
        compiler_params=_cparams("parallel", "parallel"),
        name="mixer_in",
    )(x, shift, scale, w_pad)


def _ssm_tables(lam_re, lam_im, log_dt, b_re, b_im, c_re, c_im, L, n_levels):
    G, P = lam_re.shape
    C = b_re.shape[-1]
    dt = jnp.exp(log_dt.astype(F32))[:, None]
    er, ei = lam_re * dt, lam_im * dt

    def power(k):
        kk = k.astype(F32)[:, None, None]
        mag = jnp.exp(kk * er)
        return mag * jnp.cos(kk * ei), mag * jnp.sin(kk * ei)

    lb_re, lb_im = power(jnp.ones((1,), F32))
    nr, ni = lb_re[0] - 1.0, lb_im[0]
    den = lam_re * lam_re + lam_im * lam_im
    fr = (nr * lam_re + ni * lam_im) / den
    fi = (ni * lam_re - nr * lam_im) / den
    bbr = fr[:, :, None] * b_re - fi[:, :, None] * b_im
    bbi = fr[:, :, None] * b_im + fi[:, :, None] * b_re
    pr, pi = power(jnp.arange(L + 1))
    clr = c_re[None] * pr[:, :, None, :] - c_im[None] * pi[:, :, None, :]
    cli = c_re[None] * pi[:, :, None, :] + c_im[None] * pr[:, :, None, :]
    kern = (jnp.einsum('kgcp,gpd->kgcd', clr[:L], bbr, precision=HIGHEST)
            - jnp.einsum('kgcp,gpd->kgcd', cli[:L], bbi, precision=HIGHEST))
    GP = LANE // C
    X = G // GP
    eye = jnp.eye(GP, dtype=BF16)
    place_einsum = functools.partial(jnp.einsum, preferred_element_type=BF16)
    kblk = place_einsum('kxhcd,hj->xkhdjc', kern.astype(BF16).reshape(L, X, GP, C, C), eye)
    kblk = kblk.reshape(X, L, LANE, LANE)
    prr, pir = pr[:L][::-1], pi[:L][::-1]
    ws2 = jnp.stack([prr[..., None] * bbr[None] - pir[..., None] * bbi[None],
                     prr[..., None] * bbi[None] + pir[..., None] * bbr[None]])
    ws = place_einsum('rsxhpd,hj->xshdrjp', ws2.astype(BF16).reshape(2, L, X, GP, P, C), eye)
    ws = ws.reshape(X, L * LANE, 2 * GP * P)
    wy2 = jnp.stack([clr[1:], -cli[1:]])
    wy = place_einsum('rtxhcp,hj->xrhptjc', wy2.astype(BF16).reshape(2, L, X, GP, C, P), eye)
    wy = wy.reshape(X, 2 * GP * P, L * LANE)
    lr, li = power(L * (2 ** jnp.arange(n_levels)))
    lr, li = lr.reshape(n_levels, X, GP * P), li.reshape(n_levels, X, GP * P)
    ar = jnp.transpose(jnp.concatenate([lr, lr], -1), (1, 0, 2))
    ai = jnp.transpose(jnp.concatenate([-li, li], -1), (1, 0, 2))
    return kblk, ws, wy, ar, ai, (lb_re[0], lb_im[0], bbr, bbi)


def _ssm_kernel(u_ref, kblk_ref, ws_ref, wy_ref, ar_ref, ai_ref, y_ref, hl_ref, toep_ref, *, L, nc, n_levels):
    for s in range(L):
        for t in range(L):
            blk = kblk_ref[0, t - s] if t >= s else jnp.zeros((LANE, LANE), BF16)
            toep_ref[s * LANE:(s + 1) * LANE, t * LANE:(t + 1) * LANE] = blk
    u = jnp.concatenate([u_ref[0, pl.ds(t, nc, stride=L), :] for t in range(L)], axis=1).astype(BF16)
    y1 = jnp.dot(u, toep_ref[...], preferred_element_type=F32)
    h = jnp.dot(u, ws_ref[0], preferred_element_type=F32)
    w2 = h.shape[-1]
    rows = lax.broadcasted_iota(jnp.int32, (nc, w2), 0)
    for k in range(n_levels):
        d = 1 << k
        sh = jnp.where(rows >= d, pltpu.roll(h, d, axis=0), 0.0)
        sw = pltpu.roll(sh, w2 // 2, axis=1)
        h = h + ar_ref[0, k:k + 1, :] * sh + ai_ref[0, k:k + 1, :] * sw
    hl_ref[0, 0] = h[nc - 1:nc, :]
    hp = jnp.where(rows >= 1, pltpu.roll(h, 1, axis=0), 0.0)
    y = y1 + jnp.dot(hp.astype(BF16), wy_ref[0], preferred_element_type=F32)
    for t in range(L):
        y_ref[0, pl.ds(t, nc, stride=L), :] = y[:, t * LANE:(t + 1) * LANE]


def _ssm_prompt(u, tables):
    kblk, ws, wy, ar, ai, _ = tables
    B, T, _ = u.shape
    L, P = SSM_CHUNK, SSM_STATE
    X, n_levels, w2 = ar.shape
    GP = w2 // (2 * P)
    nc = T // L
    tab = lambda a: pl.BlockSpec((1,) + a.shape[1:], lambda x, b: (x,) + (0,) * (a.ndim - 1))
    seq = pl.BlockSpec((1, T, LANE), lambda x, b: (b, 0, x))
    y, hl = pl.pallas_call(
        functools.partial(_ssm_kernel, L=L, nc=nc, n_levels=n_levels),
        out_shape=(jax.ShapeDtypeStruct((B, T, D_SSM), F32), jax.ShapeDtypeStruct((X, B, 1, w2), F32)),
        grid=(X, B),
        in_specs=[seq, tab(kblk), tab(ws), tab(wy), tab(ar), tab(ai)],
        out_specs=(seq, pl.BlockSpec((1, 1, 1, w2), lambda x, b: (x, b, 0, 0))),
        scratch_shapes=[pltpu.VMEM((L * LANE, L * LANE), BF16)],
        compiler_params=_cparams("parallel", "parallel"),
        name="ssm_prompt",
    )(u, kblk, ws, wy, ar, ai)
    hl = jnp.transpose(hl.reshape(X, B, 2, GP, P), (1, 0, 3, 2, 4))
    return y, hl.reshape(B, X * GP, 2 * P)


def _ssm_step_kernel(u_ref, h0_ref, bb_ref, lr_ref, li_ref, cy_ref, y_ref, h_ref):
    p = lr_ref.shape[-1] // 2
    bu = jnp.einsum('gbc,gcp->gbp', u_ref[...], bb_ref[...], preferred_element_type=F32)
    h0 = h0_ref[...]
    h0s = jnp.concatenate([h0[..., p:], h0[..., :p]], axis=-1)
    h = lr_ref[...] * h0 + li_ref[...] * h0s + bu
    h_ref[...] = h
    y_ref[...] = jnp.einsum('gbp,gpc->gbc', h.astype(BF16), cy_ref[...], preferred_element_type=F32)


def _ssm_sample(u, h0_re, h0_im, tables, c_re, c_im):
    lb_re, lb_im, bbr, bbi = tables[-1]
    B = u.shape[0]
    G, C, P = N_SSM_GROUPS, SSM_GROUP, SSM_STATE
    ug = jnp.transpose(u.reshape(B, G, C), (1, 0, 2)).astype(BF16)
    h0 = jnp.transpose(jnp.concatenate([h0_re, h0_im], -1), (1, 0, 2)).astype(F32)
    bb = jnp.concatenate([jnp.transpose(bbr, (0, 2, 1)), jnp.transpose(bbi, (0, 2, 1))], -1).astype(BF16)
    lr = jnp.concatenate([lb_re, lb_re], -1)[:, None, :]
    li = jnp.concatenate([-lb_im, lb_im], -1)[:, None, :]
    cy = jnp.concatenate([jnp.transpose(c_re, (0, 2, 1)), -jnp.transpose(c_im, (0, 2, 1))], 1).astype(BF16)
    y, h = pl.pallas_call(
        _ssm_step_kernel,
        out_shape=(jax.ShapeDtypeStruct((G, B, C), F32), jax.ShapeDtypeStruct((G, B, 2 * P), F32)),
        name="ssm_step",
    )(ug, h0, bb, lr, li, cy)
    return jnp.transpose(y, (1, 0, 2)).reshape(B, D_SSM), jnp.transpose(h, (1, 0, 2))


def _compress_tables(phi_pe, phi_w1, phi_b1, phi_w2, phi_b2):
    S, H, Dh = CMP_STRIDE, N_KV_HEADS, HEAD_DIM
    w1 = phi_w1.reshape(2, 2, S, Dh, Dh)
    eye_c = jnp.eye(2, dtype=F32)
    eye_h = jnp.eye(H, dtype=F32)
    wbig = jnp.einsum('cajde,xc,yh->jxydache', w1, eye_c, eye_h).reshape(S * 2 * H * Dh, 2 * 2 * H * Dh)
    pe = jnp.transpose(phi_pe.reshape(2, 2, S, Dh), (1, 2, 0, 3))
    pe_rows = jnp.broadcast_to(pe[:, :, :, None, :], (2, S, 2, H, Dh)).reshape(2, S * 2 * H * Dh)
    n = 2 * H * Dh
    pe_w = (jnp.dot(pe_rows[0], wbig[:, :n], precision=HIGHEST) + jnp.dot(pe_rows[1], wbig[:, n:], precision=HIGHEST))
    b1 = jnp.broadcast_to(phi_b1[:, None, :], (2, H, Dh)).reshape(1, n) + pe_w[None, :]
    w2 = jnp.einsum('cef,cx,hy->chexyf', phi_w2, eye_c, eye_h).reshape(n, n)
    b2 = jnp.broadcast_to(phi_b2[:, None, :], (2, H, Dh)).reshape(1, n)
    return wbig.astype(BF16), b1, w2.astype(BF16), b2


def _compress_in_kernel(x_ref, w_ref, z_ref):
    z_ref[0] = jnp.dot(x_ref[0].astype(BF16), w_ref[...], preferred_element_type=F32)


def _compress_in(x2, tables):
    wbig = tables[0]
    N2 = wbig.shape[1]
    B, n, K = x2.shape
    tr = math.gcd(n, 256)
    return pl.pallas_call(
        _compress_in_kernel,
        out_shape=jax.ShapeDtypeStruct((B, n, N2), F32),
        grid=(B, n // tr),
        in_specs=[pl.BlockSpec((1, tr, K), lambda b, i: (b, i, 0)),
                  pl.BlockSpec((K, N2), lambda b, i: (0, 0))],
        out_specs=pl.BlockSpec((1, tr, N2), lambda b, i: (b, i, 0)),
        compiler_params=_cparams("parallel", "parallel"),
        name="compress_in",
    )(x2, wbig)


def _compress_in_paged_kernel(pt_ref, *refs, n_pg):
    x_refs = refs[:n_pg]
    w_ref, z_ref = refs[n_pg:n_pg + 2]
    scratch = refs[n_pg + 2:]
    n_slab = D_KV // LANE
    pg_part = n_pg // PAGE_PARTS
    cpp = PAGE_SIZE // CMP_STRIDE
    rows = pg_part * cpp
    for part in range(PAGE_PARTS):
        s_refs = scratch[part * n_slab:(part + 1) * n_slab]
        for k in range(pg_part):
            t = x_refs[part * pg_part + k][0].reshape(D_KV, PAGE_SIZE).T
            for c, s_ref in enumerate(s_refs):
                for n in range(cpp):
                    r0 = (k * cpp + n) * CHUNK_PITCH
                    s_ref[r0:r0 + CMP_STRIDE, :] = t[n * CMP_STRIDE:(n + 1) * CMP_STRIDE, c * LANE:(c + 1) * LANE]
        z = jnp.zeros((rows, w_ref.shape[1]), F32)
        for j in range(CMP_STRIDE):
            xj = jnp.concatenate([s_ref[pl.ds(j, rows, stride=CHUNK_PITCH), :] for s_ref in s_refs], axis=1)
            z = z + jnp.dot(xj.astype(BF16), w_ref[j * D_KV:(j + 1) * D_KV, :], preferred_element_type=F32)
        z_ref[0, part * rows:(part + 1) * rows, :] = z


def _compress_in_paged(pool_t, page_table, tables):
    wbig = tables[0]
    N2 = wbig.shape[1]
    K = wbig.shape[0]
    B, n_pages = page_table.shape
    n_pg = math.gcd(n_pages, PAGES_PER_STEP)
    rows = n_pg * PAGE_SIZE // CMP_STRIDE
    page_spec = lambda k: pl.BlockSpec((1,) + pool_t.shape[1:],
                                       lambda b, i, pt, k=k: (pt[b, i * n_pg + k], 0, 0, 0, 0))
    grid_spec = pltpu.PrefetchScalarGridSpec(
        num_scalar_prefetch=1,
        grid=(B, n_pages // n_pg),
        in_specs=[page_spec(k) for k in range(n_pg)] + [pl.BlockSpec((K, N2), lambda b, i, pt: (0, 0))],
        out_specs=pl.BlockSpec((1, rows, N2), lambda b, i, pt: (b, i, 0)),
        scratch_shapes=[pltpu.VMEM((rows // PAGE_PARTS * CHUNK_PITCH, LANE), F32)
                        for _ in range(PAGE_PARTS * (D_KV // LANE))],
    )
    return pl.pallas_call(
        functools.partial(_compress_in_paged_kernel, n_pg=n_pg),
        out_shape=jax.ShapeDtypeStruct((B, n_pages * PAGE_SIZE // CMP_STRIDE, N2), F32),
        grid_spec=grid_spec,
        compiler_params=_cparams("arbitrary", "arbitrary"),
        name="compress_in_paged",
    )(page_table, *([pool_t] * n_pg), wbig)


def _compress_out_kernel(*refs):
    z_refs, (b1_ref, w2_ref, b2_ref, o_ref) = refs[:-4], refs[-4:]
    z = jnp.concatenate([z_ref[0] for z_ref in z_refs], axis=0)
    n = z.shape[-1] // 2
    rows = z.shape[0]
    second = pltpu.roll(z[:, n:], rows - 1, axis=0)
    hdn = jax.nn.gelu(z[:, :n] + second + b1_ref[...])
    o_ref[0, :rows, :] = jnp.dot(hdn.astype(BF16), w2_ref[...], preferred_element_type=F32) + b2_ref[...]
    if o_ref.shape[1] > rows:
        o_ref[0, rows:, :] = jnp.zeros((o_ref.shape[1] - rows, n), F32)


def _compress_out(zs, tables, n_out):
    _, b1, w2, b2 = tables
    B, _, N2 = zs[0].shape
    return pl.pallas_call(
        _compress_out_kernel,
        out_shape=jax.ShapeDtypeStruct((B, n_out, N2 // 2), F32),
        grid=(B,),
        in_specs=[pl.BlockSpec((1, z.shape[1], N2), lambda b: (b, 0, 0)) for z in zs] + [
                  pl.BlockSpec((1, N2 // 2), lambda b: (0, 0)),
                  pl.BlockSpec((N2 // 2, N2 // 2), lambda b: (0, 0)),
                  pl.BlockSpec((1, N2 // 2), lambda b: (0, 0))],
        out_specs=pl.BlockSpec((1, n_out, N2 // 2), lambda b: (b, 0, 0)),
        compiler_params=_cparams("parallel"),
        name="compress_out",
    )(*zs, b1, w2, b2)


def _rel_bucket(dist):
    n = jnp.maximum(dist, 0)
    max_exact = NUM_BUCKETS // 2
    nf = jnp.maximum(n, 1).astype(F32)
    large = max_exact + (jnp.log(nf / max_exact) / math.log(REL_MAX_DIST / max_exact)
                         * (NUM_BUCKETS - max_exact)).astype(jnp.int32)
    large = jnp.minimum(large, NUM_BUCKETS - 1)
    return jnp.where(n < max_exact, n, large)


def _bias_by_distance(rel_bias, n_max):
    onehot = (_rel_bucket(jnp.arange(n_max))[None, :] == jnp.arange(NUM_BUCKETS)[:, None]).astype(F32)
    return jnp.dot(jnp.transpose(rel_bias.astype(F32)), onehot, precision=HIGHEST)


def _shifted_chunks(bias_n, pad, n_chunks, width):
    n = min(bias_n.shape[1], n_chunks * width - pad)
    ext = jnp.concatenate([jnp.broadcast_to(bias_n[:, :1], (N_HEADS, pad)), bias_n[:, :n],
                           jnp.zeros((N_HEADS, n_chunks * width - pad - n), F32)], axis=1)
    return ext.reshape(N_HEADS, n_chunks, width)


def _bias_tables_kernel(ed_ref, ec_ref, tzs_ref, tzw_ref, cmp_ref, *, tq, tk, n_qt):
    n_ds, n_dw, n_j = tzs_ref.shape[1] - 1, tzw_ref.shape[1] - 1, cmp_ref.shape[1] // 8
    tzs_ref[0, n_ds] = jnp.full((tk, tq), NEG, F32)
    tzw_ref[0, n_dw] = jnp.full((tk, tq), NEG, F32)
    w = tq + tk
    c = lax.broadcasted_iota(jnp.int32, (tk, tq), 0)
    r = lax.broadcasted_iota(jnp.int32, (tk, tq), 1)
    for d in range(n_ds):
        v = jnp.concatenate([ed_ref[0, d:d + 1, :], ed_ref[0, d + 1:d + 2, :]], axis=1)
        t = pltpu.roll(jnp.broadcast_to(v, (tk, w)), w - (tk - 1), axis=1, stride=1, stride_axis=0)[:, :tq]
        dist = d * tk + r - c
        tzs_ref[0, d] = jnp.where(dist >= 0, t, NEG)
        if d < n_dw:
            tzw_ref[0, d] = jnp.where((dist >= 0) & (dist <= WINDOW), t, NEG)
    for j in range(n_j):
        dd = n_qt - 1 - j
        c0, c1 = max(dd, 0), max(dd + 1, 0)
        v = jnp.concatenate([ec_ref[0, c0:c0 + 1, :], ec_ref[0, c1:c1 + 1, :]], axis=1)
        t = pltpu.roll(jnp.broadcast_to(v, (8, w)), w - 7 * CMP_STRIDE, axis=1, stride=CMP_STRIDE, stride_axis=0)
        dist = tq * dd + r[:8] - CMP_STRIDE * c[:8] - (CMP_BLOCK - 1)
        cmp_ref[0, j * 8:(j + 1) * 8, :] = jnp.where(dist >= 0, t[:, :tq], NEG)


def _bias_tables(bias_n, n_qt, n_rb, n_ds, n_dw, tq, tk):
    assert tq == tk == 8 * CMP_STRIDE and n_dw <= n_ds
    n_j = n_rb + n_qt - 1
    ed = _shifted_chunks(bias_n, tk - 1, n_ds + 1, tq)
    ec = _shifted_chunks(bias_n, 7 * CMP_STRIDE + CMP_BLOCK - 1, n_qt + 1, tq)
    head = lambda a: pl.BlockSpec((1,) + a.shape[1:], lambda h: (h,) + (0,) * (a.ndim - 1))
    outs = (jax.ShapeDtypeStruct((N_HEADS, n_ds + 1, tk, tq), F32),
            jax.ShapeDtypeStruct((N_HEADS, n_dw + 1, tk, tq), F32),
            jax.ShapeDtypeStruct((N_HEADS, n_j * 8, tq), F32))
    tzs, tzw, cmp = pl.pallas_call(
        functools.partial(_bias_tables_kernel, tq=tq, tk=tk, n_qt=n_qt),
        out_shape=outs,
        grid=(N_HEADS,),
        in_specs=[head(ed), head(ec)],
        out_specs=tuple(head(o) for o in outs),
        compiler_params=_cparams("parallel"),
        name="bias_tables",
    )(ed, ec)
    grp = lambda a: a.reshape((N_KV_HEADS, GQA) + a.shape[1:])
    return grp(tzs), grp(tzw), cmp


def _pool_matrix(n_cmp_pad, n_blk_pad):
    r = SEL_BLOCK // CMP_STRIDE
    i = np.arange(n_cmp_pad)[None, :]
    j = np.arange(n_blk_pad)[:, None]
    return ((i >= r * j - 1) & (i <= r * j + r - 1)).astype(np.float32)


def _cmp_select_kernel(q_ref, k_ref, vt_ref, bias_ref, pool_ref, o_ref, sel_ref, *, tq):
    qt = pl.program_id(2)
    n_qt = pl.num_programs(2)
    q = q_ref[0, 0].reshape(GQA * tq, HEAD_DIM)
    k = k_ref[0, 0]
    nc = k.shape[0]
    s = _nt_dot(k, q)
    row0 = pl.multiple_of((n_qt - 1 - qt) * 8, 8)
    s = s + jnp.concatenate([bias_ref[g, pl.ds(row0, nc), :] for g in range(GQA)], axis=-1)
    m = jnp.maximum(jnp.max(s, axis=0, keepdims=True), 0.5 * NEG)
    p = jnp.exp(s - m)
    p = p * (1.0 / jnp.maximum(jnp.sum(p, axis=0, keepdims=True), 1e-30))
    ot = jnp.dot(vt_ref[0, 0], p.astype(BF16), preferred_element_type=F32)
    o_ref[0] = jnp.concatenate([ot[:, g * tq:(g + 1) * tq].T for g in range(GQA)], axis=-1).astype(o_ref.dtype)
    imp = p[:, 0:tq]
    for g in range(1, GQA):
        imp = imp + p[:, g * tq:(g + 1) * tq]
    sb = jnp.dot(pool_ref[...], imp, precision=HIGHEST, preferred_element_type=F32)
    nb = sb.shape[0]
    blk = lax.broadcasted_iota(jnp.int32, (nb, tq), 0)
    cur = (qt * tq + lax.broadcasted_iota(jnp.int32, (nb, tq), 1)) // SEL_BLOCK
    causal = blk <= cur
    forced = (blk == 0) | (blk == cur) | (blk == cur - 1)
    sc = jnp.where(forced & causal, 1e4, jnp.where(causal, sb, -1.0))
    groups = [sc[r:r + 8] for r in range(0, nb, 8)]
    sub = lax.broadcasted_iota(jnp.int32, (8, tq), 0)
    ranks = [jnp.zeros((8, tq), F32) for _ in groups]
    for i in range(nb):
        row = sc[i:i + 1, :]
        for gi, grp in enumerate(groups):
            if gi * 8 > i:
                ahead = row >= grp
            elif gi * 8 + 7 < i:
                ahead = row > grp
            else:
                ahead = (row > grp) | ((row == grp) & (sub > i - gi * 8))
            ranks[gi] = ranks[gi] + jnp.where(ahead, 1.0, 0.0)
    rank = jnp.concatenate(ranks, axis=0)
    sel_ref[0, 0] = jnp.where((rank < N_SEL) & causal, 0.0, NEG)


def _cmp_select_prompt(q5, kc, vct, bias_tab, pool):
    B, _, _, T, _ = q5.shape
    NC = kc.shape[2]
    NB = pool.shape[0]
    R = bias_tab.shape[1]
    tq = ATT_TQ
    return pl.pallas_call(
        functools.partial(_cmp_select_kernel, tq=tq),
        out_shape=(jax.ShapeDtypeStruct((B, T, D_ATT), BF16),
                   jax.ShapeDtypeStruct((B, N_KV_HEADS, NB, T), F32)),
        grid=(B, N_KV_HEADS, T // tq),
        in_specs=[pl.BlockSpec((1, 1, GQA, tq, HEAD_DIM), lambda b, h, i: (b, h, 0, i, 0)),
                  pl.BlockSpec((1, 1, NC, HEAD_DIM), lambda b, h, i: (b, h, 0, 0)),
                  pl.BlockSpec((1, 1, HEAD_DIM, NC), lambda b, h, i: (b, h, 0, 0)),
                  pl.BlockSpec((GQA, R, tq), lambda b, h, i: (h, 0, 0)),
                  pl.BlockSpec((NB, NC), lambda b, h, i: (0, 0))],
        out_specs=(pl.BlockSpec((1, tq, GQA * HEAD_DIM), lambda b, h, i: (b, i, h)),
                   pl.BlockSpec((1, 1, NB, tq), lambda b, h, i: (b, h, 0, i))),
        compiler_params=_cparams("parallel", "parallel", "parallel"),
        name="cmp_select_prompt",
    )(q5, kc, vct, bias_tab, pool)


def _sel_win_kernel(q_ref, ks_ref, vst_ref, kw_ref, vwt_ref, sel_ref, tzs_ref, tzw_ref, os_ref, ow_ref, *, tq):
    tk = ATT_TK
    qt = pl.program_id(2)
    q = q_ref[0, 0].reshape(GQA * tq, HEAD_DIM)
    width = GQA * tq
    per_tile = tk // SEL_BLOCK

    def make_sweep(k_ref, vt_ref, tz_ref, use_sel, n_chains, single_trip):
        n_d = tz_ref.shape[2] - 1

        def scores(kt, hi):
            pad = kt > hi
            kt = jnp.minimum(kt, hi)
            off = pl.multiple_of(kt * tk, tk)
            k = k_ref[0, 0, pl.ds(off, tk), :]
            d = jnp.where(pad, n_d, jnp.minimum(qt - kt, n_d - 1))
            bias = [tz_ref[0, g, d] for g in range(GQA)]
            if use_sel:
                rows = sel_ref[0, 0, pl.ds(kt * per_tile, per_tile), :]
                selb = jnp.concatenate([jnp.broadcast_to(rows[i:i + 1], (SEL_BLOCK, tq))
                                        for i in range(per_tile)], axis=0)
                bias = [b + selb for b in bias]
            return _nt_dot(k, q) + jnp.concatenate(bias, axis=1)

        def values_t(kt, lo, hi):
            off = pl.multiple_of(jnp.clip(kt, lo, hi) * tk, tk)
            return vt_ref[0, 0, :, pl.ds(off, tk)]

        def sweep(lo, hi):
            n_trips = (hi - lo + n_chains) // n_chains

            def first_trip():
                out = []
                for c in range(n_chains):
                    s = scores(lo + c, hi)
                    m = jnp.maximum(jnp.max(s, axis=0, keepdims=True), 0.5 * NEG)
                    p = jnp.exp(s - m)
                    out.append((m, jnp.sum(p, axis=0, keepdims=True), jnp.zeros((HEAD_DIM, width), F32),
                                jnp.ones((1, width), F32), p.astype(BF16)))
                return tuple(out)

            def trip(i, chains):
                kt = lo + n_chains * i
                pv = [jnp.dot(values_t(kt - n_chains + c, lo, hi), chains[c][4], preferred_element_type=F32)
                      for c in range(n_chains)]
                ss = [scores(kt + c, hi) for c in range(n_chains)]
                out = []
                for c in range(n_chains):
                    m, l, acc, alpha_prev, _ = chains[c]
                    m_new = jnp.maximum(m, jnp.max(ss[c], axis=0, keepdims=True))
                    alpha = jnp.exp(m - m_new)
                    p = jnp.exp(ss[c] - m_new)
                    l = alpha * l + jnp.sum(p, axis=0, keepdims=True)
                    out.append((m_new, l, alpha_prev * acc + pv[c], alpha, p.astype(BF16)))
                return tuple(out)

            if single_trip:
                done = []
                for c in range(n_chains):
                    s = scores(lo + c, hi)
                    m = jnp.maximum(jnp.max(s, axis=0, keepdims=True), 0.5 * NEG)
                    p = jnp.exp(s - m)
                    done.append((m, jnp.sum(p, axis=0, keepdims=True),
                                 jnp.dot(values_t(lo + c, lo, hi), p.astype(BF16), preferred_element_type=F32)))
            else:
                chains = lax.fori_loop(1, n_trips, trip, first_trip())
                kt_last = lo + n_chains * (n_trips - 1)
                done = []
                for c in range(n_chains):
                    m, l, acc, alpha, p = chains[c]
                    done.append((m, l, alpha * acc + jnp.dot(values_t(kt_last + c, lo, hi), p,
                                                              preferred_element_type=F32)))
            m_all = functools.reduce(jnp.maximum, [m for m, _, _ in done])
            num = den = 0.0
            for m, l, acc in done:
                e = jnp.exp(m - m_all)
                num = num + acc * e
                den = den + l * e
            o = num / jnp.maximum(den, 1e-30)
            return jnp.concatenate([o[:, g * tq:(g + 1) * tq].T for g in range(GQA)], axis=-1)
        return sweep

    n_win = tzw_ref.shape[2] - 1
    os_ref[0] = make_sweep(ks_ref, vst_ref, tzs_ref, True, SEL_CHAINS, False)(0, qt).astype(os_ref.dtype)
    ow_ref[0] = make_sweep(kw_ref, vwt_ref, tzw_ref, False, n_win, True)(
        jnp.maximum(qt - (n_win - 1), 0), qt).astype(ow_ref.dtype)


def _sel_win_prompt(q5, ks, vst, kw, vwt, sel, tzs, tzw):
    B, _, _, T, _ = q5.shape
    NB = sel.shape[2]
    tq = ATT_TQ
    k_spec = pl.BlockSpec((1, 1, T, HEAD_DIM), lambda b, h, i: (b, h, 0, 0))
    vt_spec = pl.BlockSpec((1, 1, HEAD_DIM, T), lambda b, h, i: (b, h, 0, 0))
    tz_spec = lambda tz: pl.BlockSpec((1,) + tz.shape[1:], lambda b, h, i: (h, 0, 0, 0, 0))
    o_spec = pl.BlockSpec((1, tq, GQA * HEAD_DIM), lambda b, h, i: (b, i, h))
    return pl.pallas_call(
        functools.partial(_sel_win_kernel, tq=tq),
        out_shape=(jax.ShapeDtypeStruct((B, T, D_ATT), BF16), jax.ShapeDtypeStruct((B, T, D_ATT), BF16)),
        grid=(B, N_KV_HEADS, T // tq),
        in_specs=[pl.BlockSpec((1, 1, GQA, tq, HEAD_DIM), lambda b, h, i: (b, h, 0, i, 0)),
                  k_spec, vt_spec, k_spec, vt_spec,
                  pl.BlockSpec((1, 1, NB, tq), lambda b, h, i: (b, h, 0, i)),
                  tz_spec(tzs), tz_spec(tzw)],
        out_specs=(o_spec, o_spec),
        compiler_params=_cparams("parallel", "parallel", "parallel"),
        name="sel_win_prompt",
    )(q5, ks, vst, kw, vwt, sel, tzs, tzw)


def _gate_expand_matrix():
    m = np.zeros((3, 2 * LANE, D_ATT), np.float32)
    for r in range(3):
        for h in range(N_HEADS):
            m[r, h * 3 + r, h * HEAD_DIM:(h + 1) * HEAD_DIM] = 1.0
            m[r, LANE + h * 3 + r, h * HEAD_DIM:(h + 1) * HEAD_DIM] = 1.0
    return m


def _split_bf16(x):
    hi = x.astype(BF16)
    return hi, (x - hi.astype(F32)).astype(BF16)


def _post_mixer_kernel(y_ref, u_ref, oc_ref, os_ref, ow_ref, g_ref, x_ref, gate_ref, sh_ref, sc_ref,
                       dskip_ref, wglu_ref, bglu_ref, gexp_ref, wout_ref, lng_ref, lnb_ref,
                       wr_ref, br_ref, x1_ref, hm_ref, te_ref, tw_ref):
    y = y_ref[0] + dskip_ref[...] * u_ref[0]
    gl = jax.nn.gelu(y)
    ssm = gl * jax.nn.sigmoid(jnp.dot(gl.astype(BF16), wglu_ref[...], preferred_element_type=F32)
                              + bglu_ref[...])
    sg = jnp.concatenate(_split_bf16(jax.nn.sigmoid(g_ref[0])), axis=1)
    att = jnp.zeros(oc_ref.shape[1:], F32)
    for r, o_ref in enumerate((oc_ref, os_ref, ow_ref)):
        att = att + jnp.dot(sg, gexp_ref[r], preferred_element_type=F32) * o_ref[0].astype(F32)
    h = (jnp.dot(ssm.astype(BF16), wout_ref[:D_SSM, :], preferred_element_type=F32)
         + jnp.dot(att.astype(BF16), wout_ref[D_SSM:, :], preferred_element_type=F32))
    z = DN_ALPHA * x_ref[0] + gate_ref[0] * h
    x1 = _layer_norm(z) * lng_ref[...] + lnb_ref[...]
    x1_ref[0] = x1
    hm = _layer_norm(x1) * (1.0 + sc_ref[0]) + sh_ref[0]
    hm_ref[0] = hm
    hm_hi, hm_lo = _split_bf16(hm)
    logits = (jnp.dot(hm_hi, wr_ref[0], preferred_element_type=F32)
              + jnp.dot(hm_lo, wr_ref[0], preferred_element_type=F32)
              + jnp.dot(hm_hi, wr_ref[1], preferred_element_type=F32)) + br_ref[...]
    lane = lax.broadcasted_iota(jnp.int32, logits.shape, 1)
    work = jnp.where(lane < N_EXPERTS, logits, -jnp.inf)
    te = jnp.zeros(logits.shape, jnp.int32)
    tv = jnp.zeros(logits.shape, F32)
    for k in range(TOP_K):
        best = jnp.max(work, axis=-1, keepdims=True)
        arg = jnp.min(jnp.where(work == best, lane, LANE), axis=-1, keepdims=True)
        te = jnp.where(lane == k, arg, te)
        tv = jnp.where(lane == k, best, tv)
        work = jnp.where(lane == arg, -jnp.inf, work)
    ex = jnp.where(lane < TOP_K, jnp.exp(tv - tv[:, 0:1]), 0.0)
    te_ref[0] = te
    tw_ref[0] = ex / jnp.sum(ex, axis=-1, keepdims=True)


def _post_mixer(y, u, oc, osel, ow, g, x, gate, shift, scale, w, tm):
    B, T, D = x.shape
    R = gate.shape[1]
    rb = 1 if R == 1 else tm
    mod_map = (lambda b, i: (b, 0, 0)) if R == 1 else (lambda b, i: (b, i, 0))
    row = lambda n: pl.BlockSpec((1, tm, n), lambda b, i: (b, i, 0))
    mod = pl.BlockSpec((1, rb, D), mod_map)
    full = lambda a: pl.BlockSpec(a.shape, lambda b, i: (0,) * a.ndim)
    consts = (w['d_skip'], w['w_glu'], w['b_glu'], w['gexp'], w['w_out'], w['ln1_g'], w['ln1_b'],
              w['w_router'], w['b_router'])
    return pl.pallas_call(
        _post_mixer_kernel,
        out_shape=(jax.ShapeDtypeStruct((B, T, D), F32), jax.ShapeDtypeStruct((B, T, D), F32),
                   jax.ShapeDtypeStruct((B, T, LANE), jnp.int32), jax.ShapeDtypeStruct((B, T, LANE), F32)),
        grid=(B, T // tm),
        in_specs=[row(D_SSM), row(D_SSM), row(D_ATT), row(D_ATT), row(D_ATT), row(LANE), row(D),
                  mod, mod, mod] + [full(a) for a in consts],
        out_specs=(row(D), row(D), row(LANE), row(LANE)),
        compiler_params=_cparams("parallel", "parallel"),
        name="post_mixer",
    )(y, u, oc, osel, ow, g, x, gate, shift, scale, *consts)


def _expert_kernel(e_ref, blk_ref, lo_ref, hi_ref, first_ref, x_ref, wgu_ref, bgu_ref, wd_ref, bd_ref, o_ref,
                   wgu_s, wd_s):
    i = pl.program_id(0)
    fresh = (i == 0) | (e_ref[i] != e_ref[jnp.maximum(i - 1, 0)])

    @pl.when(fresh)
    def _():
        wgu_s[...] = wgu_ref[0].astype(BF16)
        wd_s[...] = wd_ref[0].astype(BF16)

    @pl.when(first_ref[i] == 1)
    def _():
        o_ref[...] = jnp.zeros_like(o_ref)

    @pl.when(hi_ref[i] > lo_ref[i])
    def _():
        gu = jnp.dot(x_ref[...].astype(BF16), wgu_s[...], preferred_element_type=F32) + bgu_ref[0]
        gate = jnp.minimum(gu[:, :D_FF], SWIGLU_LIMIT)
        up = jnp.clip(gu[:, D_FF:], -SWIGLU_LIMIT, SWIGLU_LIMIT)
        hh = (up + 1.0) * gate * jax.nn.sigmoid(SWIGLU_ALPHA * gate)
        y = jnp.dot(hh.astype(BF16), wd_s[...], preferred_element_type=F32) + bd_ref[0]
        row = blk_ref[i] * MOE_ROWS + lax.broadcasted_iota(jnp.int32, (MOE_ROWS, 1), 0)
        o_ref[...] = jnp.where((row >= lo_ref[i]) & (row < hi_ref[i]), y, o_ref[...])


def _experts(xb, items, w_gate_up, b_gate_up, w_down, b_down):
    rows, D = xb.shape
    n_items = items[0].shape[0]
    wmap = lambda i, e, blk, lo, hi, first: (e[i], 0, 0)
    rmap = lambda i, e, blk, lo, hi, first: (blk[i], 0)
    grid_spec = pltpu.PrefetchScalarGridSpec(
        num_scalar_prefetch=5,
        grid=(n_items,),
        in_specs=[pl.BlockSpec((MOE_ROWS, D), rmap),
                  pl.BlockSpec((1, D, 2 * D_FF), wmap),
                  pl.BlockSpec((1, 1, 2 * D_FF), wmap),
                  pl.BlockSpec((1, D_FF, D), wmap),
                  pl.BlockSpec((1, 1, D), wmap)],
        out_specs=pl.BlockSpec((MOE_ROWS, D), rmap),
        scratch_shapes=[pltpu.VMEM((D, 2 * D_FF), BF16), pltpu.VMEM((D_FF, D), BF16)],
    )
    return pl.pallas_call(
        _expert_kernel,
        out_shape=jax.ShapeDtypeStruct((rows, D), F32),
        grid_spec=grid_spec,
        compiler_params=_cparams("arbitrary"),
        name="moe_experts",
    )(*items, xb, w_gate_up, b_gate_up.reshape(N_EXPERTS, 1, 2 * D_FF), w_down,
      b_down.reshape(N_EXPERTS, 1, D))


def _moe_dispatch(top_e, n):
    blk = MOE_ROWS
    nk = n * TOP_K
    cb = 128
    assert nk % cb == 0
    e = top_e.reshape(-1)
    oh = (jnp.arange(N_EXPERTS)[:, None] == e[None, :]).astype(BF16).reshape(N_EXPERTS, nk // cb, cb)
    before = jnp.asarray(np.triu(np.ones((cb, cb), np.float32), 1), dtype=BF16)
    within = jnp.einsum('ebj,ji->ebi', oh, before, preferred_element_type=F32)
    blk_tot = jnp.sum(oh.astype(F32), axis=2)
    blk_off = jnp.cumsum(blk_tot, axis=1) - blk_tot
    counts = jnp.sum(blk_tot, axis=1)
    start = jnp.cumsum(counts) - counts
    dest = jnp.sum((within + (blk_off + start[:, None])[:, :, None]) * oh.astype(F32), axis=0)
    dest = dest.reshape(nk).astype(jnp.int32)
    order = jnp.argsort(dest)
    n_blk = -(-nk // blk)
    row_tok = jnp.concatenate([(order // TOP_K).astype(jnp.int32), jnp.full((n_blk * blk - nk,), n, jnp.int32)])
    counts_i, start_i = counts.astype(jnp.int32), start.astype(jnp.int32)
    first_b = start_i // blk
    last_b = (start_i + counts_i - 1) // blk
    n_it = jnp.where(counts_i > 0, last_b - first_b + 1, 0)
    it_end = jnp.cumsum(n_it)
    it_start = it_end - n_it
    n_items = n_blk + N_EXPERTS - 1
    i = jnp.arange(n_items)
    live = i < it_end[-1]
    it_e = jnp.minimum(jnp.sum(it_end[None, :] <= i[:, None], axis=1), N_EXPERTS - 1)
    it_blk = jnp.where(live, first_b[it_e] + i - it_start[it_e], n_blk - 1)
    it_lo = jnp.where(live, start_i[it_e], 0)
    it_hi = jnp.where(live, start_i[it_e] + counts_i[it_e], 0)
    it_first = jnp.concatenate([jnp.ones((1,), jnp.int32), (it_blk[1:] != it_blk[:-1]).astype(jnp.int32)])
    items = tuple(a.astype(jnp.int32) for a in (it_e, it_blk, it_lo, it_hi, it_first))
    return row_tok, dest.reshape(n, TOP_K), items


def _final_kernel(x_ref, y0_ref, y1_ref, y2_ref, y3_ref, tw_ref, gate_ref, lng_ref, lnb_ref, o_ref):
    tw = tw_ref[0]
    y = jnp.zeros_like(x_ref[0])
    for k, y_ref in enumerate((y0_ref, y1_ref, y2_ref, y3_ref)):
        y = y + tw[:, k:k + 1] * y_ref[0]
    z = DN_ALPHA * x_ref[0] + gate_ref[0] * y
    o_ref[0] = _layer_norm(z) * lng_ref[...] + lnb_ref[...]


def _final(x1, ys, tw, gate, ln_g, ln_b, tm):
    B, T, D = x1.shape
    R = gate.shape[1]
    rb = 1 if R == 1 else tm
    mod_map = (lambda b, i: (b, 0, 0)) if R == 1 else (lambda b, i: (b, i, 0))
    row = lambda n: pl.BlockSpec((1, tm, n), lambda b, i: (b, i, 0))
    vec = pl.BlockSpec((1, D), lambda b, i: (0, 0))
    return pl.pallas_call(
        _final_kernel,
        out_shape=jax.ShapeDtypeStruct((B, T, D), F32),
        grid=(B, T // tm),
        in_specs=[row(D), row(D), row(D), row(D), row(D), row(LANE),
                  pl.BlockSpec((1, rb, D), mod_map), vec, vec],
        out_specs=row(D),
        compiler_params=_cparams("parallel", "parallel"),
        name="moe_combine_ln",
    )(x1, *ys, tw, gate, ln_g, ln_b)


def _cmp_select_step_kernel(q_ref, kv_ref, bias_ref, pool_ref, o_ref, idx_ref, *, n_cmp, n_blk, q_pos):
    q = q_ref[0].astype(BF16)
    ncp = kv_ref.shape[1]
    nbp = pool_ref.shape[1]
    hd = HEAD_DIM
    kv = kv_ref[0]
    kb = [kv[:, h * hd:(h + 1) * hd].astype(BF16) for h in range(N_KV_HEADS)]
    vb = [kv[:, (N_KV_HEADS + h) * hd:(N_KV_HEADS + h + 1) * hd].astype(BF16) for h in range(N_KV_HEADS)]
    row = lax.broadcasted_iota(jnp.int32, (N_HEADS, 1), 0)
    first = row < GQA
    s = jnp.where(first, _nt_dot(q, kb[0]), _nt_dot(q, kb[1])) * (hd ** -0.5)
    s = s + bias_ref[...]
    ci = lax.broadcasted_iota(jnp.int32, (N_HEADS, ncp), 1)
    mask = (ci * CMP_STRIDE + CMP_BLOCK - 1 <= q_pos) & (ci < n_cmp)
    s = jnp.where(mask, s, NEG)
    m = jnp.max(s, axis=-1, keepdims=True)
    p = jnp.where(mask, jnp.exp(s - m), 0.0)
    p = p / jnp.maximum(jnp.sum(p, axis=-1, keepdims=True), 1e-30)
    pb = p.astype(BF16)
    o_ref[0] = jnp.where(first, jnp.dot(pb, vb[0], preferred_element_type=F32),
                         jnp.dot(pb, vb[1], preferred_element_type=F32))
    imp0 = jnp.sum(jnp.where(first, p, 0.0), axis=0, keepdims=True)
    imp1 = jnp.sum(jnp.where(first, 0.0, p), axis=0, keepdims=True)
    imp = jnp.where(first, imp0, imp1)
    sb = jnp.dot(imp, pool_ref[...], precision=HIGHEST, preferred_element_type=F32)
    cur = q_pos // SEL_BLOCK
    bi = lax.broadcasted_iota(jnp.int32, (nbp, nbp), 0)
    bj = lax.broadcasted_iota(jnp.int32, (nbp, nbp), 1)
    blk = lax.broadcasted_iota(jnp.int32, (1, nbp), 1)
    causal = blk <= cur
    forced = (blk == 0) | (blk == cur) | (blk == cur - 1)
    rsel = lax.broadcasted_iota(jnp.int32, (N_SEL, nbp), 0)
    for h in range(N_KV_HEADS):
        sc = jnp.where(forced & causal, 1e4, jnp.where(causal, sb[h * GQA:h * GQA + 1, :], -1.0))
        sc = jnp.where(blk < n_blk, sc, -2.0)
        scb = jnp.broadcast_to(sc, (nbp, nbp))
        col = jnp.sum(jnp.where(bi == bj, scb, 0.0), axis=1, keepdims=True)
        ahead = (col > scb) | ((col == scb) & (bi < bj))
        rank = jnp.sum(ahead.astype(jnp.int32), axis=0, keepdims=True)
        hit = jnp.broadcast_to(rank, (N_SEL, nbp)) == rsel
        idx = jnp.sum(jnp.where(hit, jnp.broadcast_to(blk, (N_SEL, nbp)), 0), axis=1, keepdims=True)
        idx_ref[0, h] = jnp.broadcast_to(idx, (N_SEL, LANE))


def _cmp_select_step(q, ckv, bias, pool, n_cmp, n_blk, q_pos):
    B = q.shape[0]
    NCp = ckv.shape[1]
    return pl.pallas_call(
        functools.partial(_cmp_select_step_kernel, n_cmp=n_cmp, n_blk=n_blk, q_pos=q_pos),
        out_shape=(jax.ShapeDtypeStruct((B, N_HEADS, HEAD_DIM), F32),
                   jax.ShapeDtypeStruct((B, N_KV_HEADS, N_SEL, LANE), jnp.int32)),
        grid=(B,),
        in_specs=[pl.BlockSpec((1, N_HEADS, HEAD_DIM), lambda b: (b, 0, 0)),
                  pl.BlockSpec((1, NCp, D_KV), lambda b: (b, 0, 0)),
                  pl.BlockSpec(bias.shape, lambda b: (0, 0)),
                  pl.BlockSpec(pool.shape, lambda b: (0, 0))],
        out_specs=(pl.BlockSpec((1, N_HEADS, HEAD_DIM), lambda b: (b, 0, 0)),
                   pl.BlockSpec((1, N_KV_HEADS, N_SEL, LANE), lambda b: (b, 0, 0, 0))),
        compiler_params=_cparams("parallel"),
        name="cmp_select_step",
    )(q, ckv, bias, pool)


def _sel_step_kernel(pg_ref, idx_ref, q_ref, *refs, n_past, q_pos):
    page_refs = refs[:N_SEL]
    new_ref, bias_ref, kpos_ref, o_ref = refs[N_SEL:]
    b, h = pl.program_id(0), pl.program_id(1)
    base = (b * N_KV_HEADS + h) * N_SEL
    kts, vts = [], []
    for j in range(N_SEL):
        is_new = idx_ref[base + j] >= n_past
        kts.append(jnp.where(is_new, new_ref[0, 0, 0], page_refs[j][0, 0, 0]))
        vts.append(jnp.where(is_new, new_ref[0, 1, 0], page_refs[j][0, 1, 0]))
    kt = jnp.concatenate(kts, axis=1).astype(BF16)
    vt = jnp.concatenate(vts, axis=1).astype(BF16)
    s = jnp.dot(q_ref[0].astype(BF16), kt, preferred_element_type=F32) * (HEAD_DIM ** -0.5) + bias_ref[0, 0]
    mask = kpos_ref[0, 0] <= q_pos
    s = jnp.where(mask, s, NEG)
    m = jnp.max(s, axis=-1, keepdims=True)
    p = jnp.where(mask, jnp.exp(s - m), 0.0)
    l = jnp.sum(p, axis=-1, keepdims=True)
    o_ref[0, 0] = _nt_dot(p.astype(BF16), vt) / jnp.maximum(l, 1e-30)


def _sel_step(q, pool_t, new_t, bias_sel, kpos, pages, idx_flat, n_past, q_pos):
    B = q.shape[0]
    nk = N_SEL * PAGE_SIZE
    slot = lambda b, h, j: (b * N_KV_HEADS + h) * N_SEL + j
    page_spec = lambda j: pl.BlockSpec((1, 2, 1, HEAD_DIM, PAGE_SIZE),
                                       lambda b, h, pg, ix, j=j: (pg[slot(b, h, j)], 0, h, 0, 0))
    grid_spec = pltpu.PrefetchScalarGridSpec(
        num_scalar_prefetch=2,
        grid=(B, N_KV_HEADS),
        in_specs=[pl.BlockSpec((1, N_HEADS, HEAD_DIM), lambda b, h, pg, ix: (b, 0, 0))]
        + [page_spec(j) for j in range(N_SEL)]
        + [pl.BlockSpec((1, 2, 1, HEAD_DIM, PAGE_SIZE), lambda b, h, pg, ix: (b, 0, h, 0, 0)),
           pl.BlockSpec((1, 1, N_HEADS, nk), lambda b, h, pg, ix: (b, h, 0, 0)),
           pl.BlockSpec((1, 1, 1, nk), lambda b, h, pg, ix: (b, h, 0, 0))],
        out_specs=pl.BlockSpec((1, 1, N_HEADS, HEAD_DIM), lambda b, h, pg, ix: (b, h, 0, 0)),
    )
    return pl.pallas_call(
        functools.partial(_sel_step_kernel, n_past=n_past, q_pos=q_pos),
        out_shape=jax.ShapeDtypeStruct((B, N_KV_HEADS, N_HEADS, HEAD_DIM), F32),
        grid_spec=grid_spec,
        compiler_params=_cparams("arbitrary", "arbitrary"),
        name="sel_step",
    )(pages, idx_flat, q, *([pool_t] * N_SEL), new_t, bias_sel, kpos)


def _win_step_kernel(q_ref, w_ref, new_ref, bias_ref, bias0_ref, o_ref):
    q = q_ref[0]
    qb = q.astype(BF16)
    row = lax.broadcasted_iota(jnp.int32, (N_HEADS, 1), 0)
    first = row < GQA
    hd = HEAD_DIM
    kt = [w_ref[0, 0, h].astype(BF16) for h in range(N_KV_HEADS)]
    vt = [w_ref[0, 1, h].astype(BF16) for h in range(N_KV_HEADS)]
    dots = [jnp.dot(qb, kt[h], preferred_element_type=F32) for h in range(N_KV_HEADS)]
    s = jnp.where(first, dots[0], dots[1]) * (hd ** -0.5) + bias_ref[...]
    new = new_ref[0]
    kn = jnp.where(first, new[:, 0:hd], new[:, hd:2 * hd])
    vn = jnp.where(first, new[:, 2 * hd:3 * hd], new[:, 3 * hd:])
    sn = jnp.sum(q * kn, axis=-1, keepdims=True) * (hd ** -0.5) + bias0_ref[...]
    m = jnp.maximum(jnp.max(s, axis=-1, keepdims=True), sn)
    p = jnp.exp(s - m)
    pn = jnp.exp(sn - m)
    l = jnp.sum(p, axis=-1, keepdims=True) + pn
    pb = p.astype(BF16)
    acc = jnp.where(first, _nt_dot(pb, vt[0]), _nt_dot(pb, vt[1])) + pn * vn
    o_ref[0] = acc / jnp.maximum(l, 1e-30)


def _win_step(q, win_t, new, bias, bias0):
    B, W = win_t.shape[0], win_t.shape[-1]
    return pl.pallas_call(
        _win_step_kernel,
        out_shape=jax.ShapeDtypeStruct((B, N_HEADS, HEAD_DIM), F32),
        grid=(B,),
        in_specs=[pl.BlockSpec((1, N_HEADS, HEAD_DIM), lambda b: (b, 0, 0)),
                  pl.BlockSpec((1,) + win_t.shape[1:], lambda b: (b, 0, 0, 0, 0)),
                  pl.BlockSpec((1, 1, D_KV), lambda b: (b, 0, 0)),
                  pl.BlockSpec((N_HEADS, W), lambda b: (0, 0)),
                  pl.BlockSpec((N_HEADS, 1), lambda b: (0, 0))],
        out_specs=pl.BlockSpec((1, N_HEADS, HEAD_DIM), lambda b: (b, 0, 0)),
        compiler_params=_cparams("parallel"),
        name="win_step",
    )(q, win_t, new, bias, bias0)


def _split_heads(kv, dtype):
    B, L, _ = kv.shape
    kv5 = kv.reshape(B, L, 2, N_KV_HEADS, HEAD_DIM)
    return (jnp.transpose(kv5[:, :, 0], (0, 2, 1, 3)).astype(dtype),
            jnp.transpose(kv5[:, :, 1], (0, 2, 1, 3)).astype(dtype))


def _nsa_prompt(q5, kvc, ks, vst, kw, vwt, cmp_tab, rel_bias):
    B, T, _ = kvc.shape
    nc = T // CMP_STRIDE
    nb = T // SEL_BLOCK
    ckv = _compress_out([_compress_in(kvc.reshape(B, nc, CMP_STRIDE * D_KV), cmp_tab)], cmp_tab, nc)
    kc, vc = _split_heads(ckv, BF16)
    vct = jnp.transpose(vc, (0, 1, 3, 2))
    bias_n = _bias_by_distance(rel_bias, T)
    n_qt, n_kt = T // ATT_TQ, T // ATT_TK
    n_ds = min(n_kt, -(-(REL_MAX_DIST + ATT_TK - 1) // ATT_TK) + 1)
    n_dw = min(n_kt, WINDOW // ATT_TK + 1)
    tzs, tzw, bias_tab = _bias_tables(bias_n, n_qt, nc // 8, n_ds, n_dw, ATT_TQ, ATT_TK)
    pool = jnp.asarray(_pool_matrix(nc, nb))
    o_cmp, sel = _cmp_select_prompt(q5, kc, vct, bias_tab, pool)
    o_sel, o_win = _sel_win_prompt(q5, ks, vst, kw, vwt, sel, tzs, tzw)
    return o_cmp, o_sel, o_win


def _nsa_sample(q, kvc, kvs, kvw, pool_cmp, pool_sel, win_buf, page_table, cmp_tab, rel_bias):
    B = q.shape[0]
    n_pages = page_table.shape[1]
    past_len = n_pages * PAGE_SIZE
    q_pos = past_len
    lp = -(-(past_len + 1) // SEL_BLOCK) * SEL_BLOCK
    n_cmp = lp // CMP_STRIDE - 1
    n_blk = lp // SEL_BLOCK
    n_past_chunks = past_len // CMP_STRIDE
    n_tail = 8
    assert n_past_chunks + n_tail >= n_cmp + 1
    n_chunks = n_past_chunks + n_tail
    feature_major = lambda pool: jnp.transpose(pool, (0, 2, 3, 4, 1))
    z_past = _compress_in_paged(feature_major(pool_cmp), page_table, cmp_tab)
    tail = jnp.pad(kvc[:, None, :], ((0, 0), (0, n_tail * CMP_STRIDE - 1), (0, 0)))
    z_tail = _compress_in(tail.reshape(B, n_tail, CMP_STRIDE * D_KV), cmp_tab)
    ncp = -(-n_chunks // LANE) * LANE
    nbp = -(-n_blk // LANE) * LANE
    ckv = _compress_out([z_past, z_tail], cmp_tab, ncp)
    bias_n = _bias_by_distance(rel_bias, q_pos + 1)
    n_back = max((n_pages + 1) * PAGE_SIZE, ncp * CMP_STRIDE + CMP_BLOCK)
    back = jnp.concatenate([bias_n[:, ::-1], jnp.broadcast_to(bias_n[:, :1], (N_HEADS, n_back - q_pos - 1))], 1)
    bias_c = back[:, CMP_BLOCK - 1:CMP_BLOCK - 1 + ncp * CMP_STRIDE:CMP_STRIDE]
    pool = jnp.asarray(_pool_matrix(ncp, nbp).T)
    q3 = q.reshape(B, N_HEADS, HEAD_DIM)
    o_cmp, idx = _cmp_select_step(q3, ckv, bias_c, pool, n_cmp, n_blk, q_pos)
    idx = idx[..., 0]
    bpp = PAGE_SIZE // SEL_BLOCK
    n_past = n_pages * bpp
    lpage = idx // bpp
    pages = jnp.take_along_axis(page_table, jnp.minimum(lpage, n_pages - 1).reshape(B, -1), axis=1)
    new_t = jnp.pad(kvs.reshape(B, 2, N_KV_HEADS, HEAD_DIM, 1), ((0, 0),) * 4 + ((0, PAGE_SIZE - 1),))
    bias_page = jnp.transpose(back[:, :(n_pages + 1) * PAGE_SIZE].reshape(N_HEADS, n_pages + 1, PAGE_SIZE),
                              (1, 0, 2))
    bias_sel = jnp.transpose(bias_page[lpage], (0, 1, 3, 2, 4)).reshape(B, N_KV_HEADS, N_HEADS, -1)
    kpos = lpage[..., None] * PAGE_SIZE + jnp.arange(PAGE_SIZE)
    ok = (kpos // SEL_BLOCK == idx[..., None]) & (idx <= q_pos // SEL_BLOCK)[..., None]
    kpos = jnp.where(ok, kpos, q_pos + 1).reshape(B, N_KV_HEADS, 1, -1).astype(jnp.int32)
    o_sel = _sel_step(q3, feature_major(pool_sel), new_t, bias_sel, kpos, pages.reshape(-1).astype(jnp.int32),
                      idx.reshape(-1).astype(jnp.int32), n_past, q_pos)
    o_sel = jnp.concatenate([o_sel[:, h, h * GQA:(h + 1) * GQA] for h in range(N_KV_HEADS)], axis=1)
    wb = win_buf.shape[1]
    bias_w = bias_n[:, 1:wb + 1][:, ::-1]
    o_win = _win_step(q3, feature_major(win_buf), kvw[:, None, :], bias_w, bias_n[:, 0:1])
    return o_cmp.reshape(B, D_ATT), o_sel.reshape(B, D_ATT), o_win.reshape(B, D_ATT)


def kernel(x_prompt, x_sample, cache_cmp_kv, cache_sel_kv, state_win_kv, state_ssm_re, state_ssm_im, page_table,
           c_prompt, c_sample, w_ada, b_ada, w_in, lam_re, lam_im, log_dt, b_re, b_im, c_re, c_im, d_skip,
           w_glu, b_glu, phi_pe, phi_w1, phi_b1, phi_w2, phi_b2, rel_bias, w_out, ln1_g, ln1_b,
           w_router, b_router, w_gate_up, b_gate_up, w_down, b_down, ln2_g, ln2_b):
    assert w_ada.shape[0] == DEPTH == 1
    l = 0
    Bp, T, D = x_prompt.shape
    Bs = x_sample.shape[0]
    kv_tail = (2, N_KV_HEADS, HEAD_DIM)

    n_c = Bp + Bs
    c_all = jnp.pad(jnp.concatenate([c_prompt, c_sample], 0), ((0, -n_c % 8), (0, 0)))
    m_all = _adaln(c_all, w_ada[l], b_ada[l])
    m_p = m_all[:Bp].reshape(Bp, 6, D)
    m_s = m_all[Bp:n_c].reshape(Bs, 6, D)
    mod_p = [m_p[:, i:i + 1, :] for i in range(6)]
    mod_s = [m_s[None, :, i, :] for i in range(6)]

    w_in_pad = jnp.pad(w_in[l], ((0, 0), (0, D_IN_PAD - D_IN))).astype(BF16)
    n_levels = max(1, int(math.log2(T // SSM_CHUNK)))
    ssm_tab = _ssm_tables(lam_re[l], lam_im[l], log_dt[l], b_re[l], b_im[l], c_re[l], c_im[l],
                          SSM_CHUNK, n_levels)
    cmp_tab = _compress_tables(phi_pe[l], phi_w1[l], phi_b1[l], phi_w2[l], phi_b2[l])
    w_post = dict(
        d_skip=d_skip[l].reshape(1, D_SSM), w_glu=w_glu[l].astype(BF16), b_glu=b_glu[l].reshape(1, D_SSM),
        gexp=jnp.asarray(_gate_expand_matrix(), dtype=BF16), w_out=w_out[l].astype(BF16),
        ln1_g=ln1_g[l].reshape(1, D), ln1_b=ln1_b[l].reshape(1, D),
        w_router=jnp.stack(_split_bf16(jnp.pad(w_router[l], ((0, 0), (0, LANE - N_EXPERTS))))),
        b_router=jnp.pad(b_router[l], (0, LANE - N_EXPERTS)).reshape(1, LANE))

    u, q5, kvc, kvs, kvw, g, ks, vst, kw, vwt = _mixer_in(x_prompt, mod_p[0], mod_p[1], w_in_pad, 512, True)
    y_ssm, h_p = _ssm_prompt(u, ssm_tab)
    o_cmp, o_sel, o_win = _nsa_prompt(q5, kvc, ks, vst, kw, vwt, cmp_tab, rel_bias)
    x1_p, hm_p, te_p, tw_p = _post_mixer(y_ssm, u, o_cmp, o_sel, o_win, g, x_prompt,
                                         mod_p[2], mod_p[3], mod_p[4], w_post, tm=512)

    u_s, q_s, kvc_s, kvs_s, kvw_s, g_s = _mixer_in(x_sample.reshape(1, Bs, D), mod_s[0], mod_s[1],
                                                   w_in_pad, Bs, False)
    y_s, h_s = _ssm_sample(u_s[0], state_ssm_re[l], state_ssm_im[l], ssm_tab, c_re[l], c_im[l])
    oc_s, os_s, ow_s = _nsa_sample(q_s[0].astype(F32), kvc_s[0], kvs_s[0], kvw_s[0], cache_cmp_kv[l],
                                   cache_sel_kv[l], state_win_kv[l], page_table, cmp_tab, rel_bias)
    x1_s, hm_s, te_s, tw_s = _post_mixer(y_s[None], u_s, oc_s[None], os_s[None], ow_s[None], g_s,
                                         x_sample.reshape(1, Bs, D), mod_s[2], mod_s[3], mod_s[4],
                                         w_post, tm=Bs)

    n_p = Bp * T
    n_all = n_p + Bs
    hm_all = jnp.concatenate([hm_p.reshape(n_p, D), hm_s.reshape(Bs, D)], 0)
    te_all = jnp.concatenate([te_p.reshape(n_p, LANE), te_s.reshape(Bs, LANE)], 0)[:, :TOP_K]
    row_tok, dest, items = _moe_dispatch(te_all, n_all)
    xb = jnp.concatenate([hm_all, jnp.zeros((1, D), F32)], 0)[row_tok]
    yb = _experts(xb, items, w_gate_up[l], b_gate_up[l], w_down[l], b_down[l])
    ys_p = [yb[dest[:n_p, k]].reshape(Bp, T, D) for k in range(TOP_K)]
    ys_s = [yb[dest[n_p:, k]].reshape(1, Bs, D) for k in range(TOP_K)]
    ln2g, ln2b = ln2_g[l].reshape(1, D), ln2_b[l].reshape(1, D)
    out_p = _final(x1_p, ys_p, tw_p, mod_p[5], ln2g, ln2b, tm=512)
    out_s = _final(x1_s, ys_s, tw_s, mod_s[5], ln2g, ln2b, tm=Bs)

    wlen = min(WINDOW, T)
    win_s = jnp.concatenate([state_win_kv[l], kvw_s[0].reshape(Bs, 1, *kv_tail)], 1)[:, -state_win_kv.shape[2]:]
    p_state = SSM_STATE
    return (out_p, out_s.reshape(Bs, 1, D),
            kvc.reshape(1, Bp, T, *kv_tail), kvc_s[0].reshape(1, Bs, 1, *kv_tail),
            kvs.reshape(1, Bp, T, *kv_tail), kvs_s[0].reshape(1, Bs, 1, *kv_tail),
            kvw[:, T - wlen:].reshape(1, Bp, wlen, *kv_tail), win_s[None],
            h_p[None, ..., :p_state], h_p[None, ..., p_state:],
            h_s[None, ..., :p_state], h_s[None, ..., p_state:])
```

```python
import functools
import math

import numpy as np
import jax
import jax.numpy as jnp
from jax import lax
from jax.experimental import pallas as pl
from jax.experimental.pallas import tpu as pltpu

DEPTH = 1
PAGE_SIZE = 128
D_SSM = 512
SSM_GROUP = 16
N_SSM_GROUPS = D_SSM // SSM_GROUP
SSM_STATE = 64
N_HEADS = 8
HEAD_DIM = 64
N_KV_HEADS = 2
GQA = N_HEADS // N_KV_HEADS
D_ATT = N_HEADS * HEAD_DIM
D_KV = 2 * N_KV_HEADS * HEAD_DIM
CMP_STRIDE = 16
CMP_BLOCK = 2 * CMP_STRIDE
SEL_BLOCK = 64
N_SEL = 16
WINDOW = 512
NUM_BUCKETS = 32
REL_MAX_DIST = 1024
N_EXPERTS = 32
TOP_K = 4
D_FF = 1024
SWIGLU_LIMIT = 7.0
SWIGLU_ALPHA = 1.702
DN_ALPHA = (2 * DEPTH) ** 0.25
D_IN = D_SSM + D_ATT + 3 * D_KV + 3 * N_HEADS
NEG = -1e30
F32 = jnp.float32
BF16 = jnp.bfloat16
HIGHEST = lax.Precision.HIGHEST

LANE = 128
D_IN_PAD = -(-D_IN // LANE) * LANE
SSM_CHUNK = 8
ATT_TQ = 128
ATT_TK = 128
SEL_CHAINS = 4
MOE_ROWS = 256
PAGES_PER_STEP = 64
PAGE_PARTS = 4
CHUNK_PITCH = 24
VMEM_LIMIT = 48 * 1024 * 1024
LN_EPS = 1e-5


def _cparams(*sem):
    return pltpu.CompilerParams(dimension_semantics=sem, vmem_limit_bytes=VMEM_LIMIT)


def _nt_dot(a, b):
    return lax.dot_general(a, b, (((1,), (1,)), ((), ())), preferred_element_type=F32)


def _layer_norm(x):
    mu = jnp.mean(x, axis=-1, keepdims=True)
    xc = x - mu
    var = jnp.mean(xc * xc, axis=-1, keepdims=True)
    return xc * lax.rsqrt(var + LN_EPS)


def _adaln_kernel(c_ref, w_ref, b_ref, o_ref):
    c = c_ref[...]
    s = c * jax.nn.sigmoid(c)
    o_ref[...] = jnp.dot(s, w_ref[...], precision=HIGHEST, preferred_element_type=F32) + b_ref[...]


def _adaln(c, w, b):
    n, d = c.shape
    dout = w.shape[1]
    tn = 1024
    return pl.pallas_call(
        _adaln_kernel,
        out_shape=jax.ShapeDtypeStruct((n, dout), F32),
        grid=(dout // tn,),
        in_specs=[pl.BlockSpec((n, d), lambda j: (0, 0)),
                  pl.BlockSpec((d, tn), lambda j: (0, j)),
                  pl.BlockSpec((1, tn), lambda j: (0, j))],
        out_specs=pl.BlockSpec((n, tn), lambda j: (0, j)),
        compiler_params=_cparams("arbitrary"),
        name="adaln",
    )(c, w, b.reshape(1, dout))


def _mixer_in_kernel(x_ref, sh_ref, sc_ref, w_ref, u_ref, q_ref, kvc_ref, kvs_ref, kvw_ref, g_ref, *att_refs):
    h = _layer_norm(x_ref[0]) * (1.0 + sc_ref[0]) + sh_ref[0]
    z = jnp.dot(h.astype(BF16), w_ref[...], preferred_element_type=F32)
    c0 = D_SSM
    c1 = c0 + D_ATT
    c2 = c1 + D_KV
    c3 = c2 + D_KV
    c4 = c3 + D_KV
    u_ref[0] = z[:, :c0]
    kvc_ref[0] = z[:, c1:c2]
    kvs_ref[0] = z[:, c2:c3]
    kvw_ref[0] = z[:, c3:c4]
    g_ref[0] = z[:, c4:c4 + LANE]
    if not att_refs:
        q_ref[0] = z[:, c0:c1].astype(BF16)
        return
    ks_ref, vst_ref, kw_ref, vwt_ref = att_refs
    hd, half = HEAD_DIM, N_KV_HEADS * HEAD_DIM
    for hq in range(N_HEADS):
        q_ref[0, hq // GQA, hq % GQA] = (z[:, c0 + hq * hd:c0 + (hq + 1) * hd] * (hd ** -0.5)).astype(BF16)
    for k_ref, vt_ref, base in ((ks_ref, vst_ref, c2), (kw_ref, vwt_ref, c3)):
        for hk in range(N_KV_HEADS):
            k_ref[0, hk] = z[:, base + hk * hd:base + (hk + 1) * hd].astype(BF16)
        vt = z[:, base + half:base + 2 * half].T
        vt_ref[0] = vt.reshape(N_KV_HEADS, hd, vt.shape[1]).astype(BF16)


def _mixer_in(x, shift, scale, w_pad, tm, attention_layouts):
    B, T, D = x.shape
    R = shift.shape[1]
    rb = 1 if R == 1 else tm
    mod_map = (lambda b, i: (b, 0, 0)) if R == 1 else (lambda b, i: (b, i, 0))
    row = lambda n: pl.BlockSpec((1, tm, n), lambda b, i: (b, i, 0))
    f32 = lambda n: jax.ShapeDtypeStruct((B, T, n), F32)
    if attention_layouts:
        q_shape = jax.ShapeDtypeStruct((B, N_KV_HEADS, GQA, T, HEAD_DIM), BF16)
        q_spec = pl.BlockSpec((1, N_KV_HEADS, GQA, tm, HEAD_DIM), lambda b, i: (b, 0, 0, i, 0))
        k_shape = jax.ShapeDtypeStruct((B, N_KV_HEADS, T, HEAD_DIM), BF16)
        k_spec = pl.BlockSpec((1, N_KV_HEADS, tm, HEAD_DIM), lambda b, i: (b, 0, i, 0))
        vt_shape = jax.ShapeDtypeStruct((B, N_KV_HEADS, HEAD_DIM, T), BF16)
        vt_spec = pl.BlockSpec((1, N_KV_HEADS, HEAD_DIM, tm), lambda b, i: (b, 0, 0, i))
        extra_shapes, extra_specs = (k_shape, vt_shape, k_shape, vt_shape), (k_spec, vt_spec, k_spec, vt_spec)
    else:
        q_shape, q_spec = jax.ShapeDtypeStruct((B, T, D_ATT), BF16), row(D_ATT)
        extra_shapes, extra_specs = (), ()
    return pl.pallas_call(
        _mixer_in_kernel,
        out_shape=(f32(D_SSM), q_shape, f32(D_KV), f32(D_KV), f32(D_KV), f32(LANE)) + extra_shapes,
        grid=(B, T // tm),
        in_specs=[row(D), pl.BlockSpec((1, rb, D), mod_map), pl.BlockSpec((1, rb, D), mod_map),
                  pl.BlockSpec((D, D_IN_PAD), lambda b, i: (0, 0))],
        out_specs=(row(D_SSM), q_spec, row(D_KV), row(D_KV), row(D_KV), row(LANE)) + extra_specs,
        compiler_params=_cparams("parallel", "parallel"),
        name="mixer_in",
    )(x, shift, scale, w_pad)


def _ssm_tables(lam_re, lam_im, log_dt, b_re, b_im, c_re, c_im, L, n_levels):
    G, P = lam_re.shape
    C = b_re.shape[-1]
    dt = jnp.exp(log_dt.astype(F32))[:, None]
    er, ei = lam_re * dt, lam_im * dt

    def power(k):
        kk = k.astype(F32)[:, None, None]
        mag = jnp.exp(kk * er)
        return mag * jnp.cos(kk * ei), mag * jnp.sin(kk * ei)

    lb_re, lb_im = power(jnp.ones((1,), F32))
    nr, ni = lb_re[0] - 1.0, lb_im[0]
    den = lam_re * lam_re + lam_im * lam_im
    fr = (nr * lam_re + ni * lam_im) / den
    fi = (ni * lam_re - nr * lam_im) / den
    bbr = fr[:, :, None] * b_re - fi[:, :, None] * b_im
    bbi = fr[:, :, None] * b_im + fi[:, :, None] * b_re
    pr, pi = power(jnp.arange(L + 1))
    clr = c_re[None] * pr[:, :, None, :] - c_im[None] * pi[:, :, None, :]
    cli = c_re[None] * pi[:, :, None, :] + c_im[None] * pr[:, :, None, :]
    kern = (jnp.einsum('kgcp,gpd->kgcd', clr[:L], bbr, precision=HIGHEST)
            - jnp.einsum('kgcp,gpd->kgcd', cli[:L], bbi, precision=HIGHEST))
    GP = LANE // C
    X = G // GP
    eye = jnp.eye(GP, dtype=BF16)
    place_einsum = functools.partial(jnp.einsum, preferred_element_type=BF16)
    kblk = place_einsum('kxhcd,hj->xkhdjc', kern.astype(BF16).reshape(L, X, GP, C, C), eye)
    kblk = kblk.reshape(X, L, LANE, LANE)
    prr, pir = pr[:L][::-1], pi[:L][::-1]
    ws2 = jnp.stack([prr[..., None] * bbr[None] - pir[..., None] * bbi[None],
                     prr[..., None] * bbi[None] + pir[..., None] * bbr[None]])
    ws = place_einsum('rsxhpd,hj->xshdrjp', ws2.astype(BF16).reshape(2, L, X, GP, P, C), eye)
    ws = ws.reshape(X, L * LANE, 2 * GP * P)
    wy2 = jnp.stack([clr[1:], -cli[1:]])
    wy = place_einsum('rtxhcp,hj->xrhptjc', wy2.astype(BF16).reshape(2, L, X, GP, C, P), eye)
    wy = wy.reshape(X, 2 * GP * P, L * LANE)
    lr, li = power(L * (2 ** jnp.arange(n_levels)))
    lr, li = lr.reshape(n_levels, X, GP * P), li.reshape(n_levels, X, GP * P)
    ar = jnp.transpose(jnp.concatenate([lr, lr], -1), (1, 0, 2))
    ai = jnp.transpose(jnp.concatenate([-li, li], -1), (1, 0, 2))
    return kblk, ws, wy, ar, ai, (lb_re[0], lb_im[0], bbr, bbi)


def _ssm_kernel(u_ref, kblk_ref, ws_ref, wy_ref, ar_ref, ai_ref, y_ref, hl_ref, toep_ref, *, L, nc, n_levels):
    for s in range(L):
        for t in range(L):
            blk = kblk_ref[0, t - s] if t >= s else jnp.zeros((LANE, LANE), BF16)
            toep_ref[s * LANE:(s + 1) * LANE, t * LANE:(t + 1) * LANE] = blk
    u = jnp.concatenate([u_ref[0, pl.ds(t, nc, stride=L), :] for t in range(L)], axis=1).astype(BF16)
    y1 = jnp.dot(u, toep_ref[...], preferred_element_type=F32)
    h = jnp.dot(u, ws_ref[0], preferred_element_type=F32)
    w2 = h.shape[-1]
    rows = lax.broadcasted_iota(jnp.int32, (nc, w2), 0)
    for k in range(n_levels):
        d = 1 << k
        sh = jnp.where(rows >= d, pltpu.roll(h, d, axis=0), 0.0)
        sw = pltpu.roll(sh, w2 // 2, axis=1)
        h = h + ar_ref[0, k:k + 1, :] * sh + ai_ref[0, k:k + 1, :] * sw
    hl_ref[0, 0] = h[nc - 1:nc, :]
    hp = jnp.where(rows >= 1, pltpu.roll(h, 1, axis=0), 0.0)
    y = y1 + jnp.dot(hp.astype(BF16), wy_ref[0], preferred_element_type=F32)
    for t in range(L):
        y_ref[0, pl.ds(t, nc, stride=L), :] = y[:, t * LANE:(t + 1) * LANE]


def _ssm_prompt(u, tables):
    kblk, ws, wy, ar, ai, _ = tables
    B, T, _ = u.shape
    L, P = SSM_CHUNK, SSM_STATE
    X, n_levels, w2 = ar.shape
    GP = w2 // (2 * P)
    nc = T // L
    tab = lambda a: pl.BlockSpec((1,) + a.shape[1:], lambda x, b: (x,) + (0,) * (a.ndim - 1))
    seq = pl.BlockSpec((1, T, LANE), lambda x, b: (b, 0, x))
    y, hl = pl.pallas_call(
        functools.partial(_ssm_kernel, L=L, nc=nc, n_levels=n_levels),
        out_shape=(jax.ShapeDtypeStruct((B, T, D_SSM), F32), jax.ShapeDtypeStruct((X, B, 1, w2), F32)),
        grid=(X, B),
        in_specs=[seq, tab(kblk), tab(ws), tab(wy), tab(ar), tab(ai)],
        out_specs=(seq, pl.BlockSpec((1, 1, 1, w2), lambda x, b: (x, b, 0, 0))),
        scratch_shapes=[pltpu.VMEM((L * LANE, L * LANE), BF16)],
        compiler_params=_cparams("parallel", "parallel"),
        name="ssm_prompt",
    )(u, kblk, ws, wy, ar, ai)
    hl = jnp.transpose(hl.reshape(X, B, 2, GP, P), (1, 0, 3, 2, 4))
    return y, hl.reshape(B, X * GP, 2 * P)


def _ssm_step_kernel(u_ref, h0_ref, bb_ref, lr_ref, li_ref, cy_ref, y_ref, h_ref):
    p = lr_ref.shape[-1] // 2
    bu = jnp.einsum('gbc,gcp->gbp', u_ref[...], bb_ref[...], preferred_element_type=F32)
    h0 = h0_ref[...]
    h0s = jnp.concatenate([h0[..., p:], h0[..., :p]], axis=-1)
    h = lr_ref[...] * h0 + li_ref[...] * h0s + bu
    h_ref[...] = h
    y_ref[...] = jnp.einsum('gbp,gpc->gbc', h.astype(BF16), cy_ref[...], preferred_element_type=F32)


def _ssm_sample(u, h0_re, h0_im, tables, c_re, c_im):
    lb_re, lb_im, bbr, bbi = tables[-1]
    B = u.shape[0]
    G, C, P = N_SSM_GROUPS, SSM_GROUP, SSM_STATE
    ug = jnp.transpose(u.reshape(B, G, C), (1, 0, 2)).astype(BF16)
    h0 = jnp.transpose(jnp.concatenate([h0_re, h0_im], -1), (1, 0, 2)).astype(F32)
    bb = jnp.concatenate([jnp.transpose(bbr, (0, 2, 1)), jnp.transpose(bbi, (0, 2, 1))], -1).astype(BF16)
    lr = jnp.concatenate([lb_re, lb_re], -1)[:, None, :]
    li = jnp.concatenate([-lb_im, lb_im], -1)[:, None, :]
    cy = jnp.concatenate([jnp.transpose(c_re, (0, 2, 1)), -jnp.transpose(c_im, (0, 2, 1))], 1).astype(BF16)
    y, h = pl.pallas_call(
        _ssm_step_kernel,
        out_shape=(jax.ShapeDtypeStruct((G, B, C), F32), jax.ShapeDtypeStruct((G, B, 2 * P), F32)),
        name="ssm_step",
    )(ug, h0, bb, lr, li, cy)
    return jnp.transpose(y, (1, 0, 2)).reshape(B, D_SSM), jnp.transpose(h, (1, 0, 2))


def _compress_tables(phi_pe, phi_w1, phi_b1, phi_w2, phi_b2):
    S, H, Dh = CMP_STRIDE, N_KV_HEADS, HEAD_DIM
    w1 = phi_w1.reshape(2, 2, S, Dh, Dh)
    eye_c = jnp.eye(2, dtype=F32)
    eye_h = jnp.eye(H, dtype=F32)
    wbig = jnp.einsum('cajde,xc,yh->jxydache', w1, eye_c, eye_h).reshape(S * 2 * H * Dh, 2 * 2 * H * Dh)
    pe = jnp.transpose(phi_pe.reshape(2, 2, S, Dh), (1, 2, 0, 3))
    pe_rows = jnp.broadcast_to(pe[:, :, :, None, :], (2, S, 2, H, Dh)).reshape(2, S * 2 * H * Dh)
    n = 2 * H * Dh
    pe_w = (jnp.dot(pe_rows[0], wbig[:, :n], precision=HIGHEST) + jnp.dot(pe_rows[1], wbig[:, n:], precision=HIGHEST))
    b1 = jnp.broadcast_to(phi_b1[:, None, :], (2, H, Dh)).reshape(1, n) + pe_w[None, :]
    w2 = jnp.einsum('cef,cx,hy->chexyf', phi_w2, eye_c, eye_h).reshape(n, n)
    b2 = jnp.broadcast_to(phi_b2[:, None, :], (2, H, Dh)).reshape(1, n)
    return wbig.astype(BF16), b1, w2.astype(BF16), b2


def _compress_in_kernel(x_ref, w_ref, z_ref):
    z_ref[0] = jnp.dot(x_ref[0].astype(BF16), w_ref[...], preferred_element_type=F32)


def _compress_in(x2, tables):
    wbig = tables[0]
    N2 = wbig.shape[1]
    B, n, K = x2.shape
    tr = math.gcd(n, 256)
    return pl.pallas_call(
        _compress_in_kernel,
        out_shape=jax.ShapeDtypeStruct((B, n, N2), F32),
        grid=(B, n // tr),
        in_specs=[pl.BlockSpec((1, tr, K), lambda b, i: (b, i, 0)),
                  pl.BlockSpec((K, N2), lambda b, i: (0, 0))],
        out_specs=pl.BlockSpec((1, tr, N2), lambda b, i: (b, i, 0)),
        compiler_params=_cparams("parallel", "parallel"),
        name="compress_in",
    )(x2, wbig)


def _compress_in_paged_kernel(pt_ref, *refs, n_pg):
    x_refs = refs[:n_pg]
    w_ref, z_ref = refs[n_pg:n_pg + 2]
    scratch = refs[n_pg + 2:]
    n_slab = D_KV // LANE
    pg_part = n_pg // PAGE_PARTS
    cpp = PAGE_SIZE // CMP_STRIDE
    rows = pg_part * cpp
    for part in range(PAGE_PARTS):
        s_refs = scratch[part * n_slab:(part + 1) * n_slab]
        for k in range(pg_part):
            t = x_refs[part * pg_part + k][0].reshape(D_KV, PAGE_SIZE).T
            for c, s_ref in enumerate(s_refs):
                for n in range(cpp):
                    r0 = (k * cpp + n) * CHUNK_PITCH
                    s_ref[r0:r0 + CMP_STRIDE, :] = t[n * CMP_STRIDE:(n + 1) * CMP_STRIDE, c * LANE:(c + 1) * LANE]
        z = jnp.zeros((rows, w_ref.shape[1]), F32)
        for j in range(CMP_STRIDE):
            xj = jnp.concatenate([s_ref[pl.ds(j, rows, stride=CHUNK_PITCH), :] for s_ref in s_refs], axis=1)
            z = z + jnp.dot(xj.astype(BF16), w_ref[j * D_KV:(j + 1) * D_KV, :], preferred_element_type=F32)
        z_ref[0, part * rows:(part + 1) * rows, :] = z


def _compress_in_paged(pool_t, page_table, tables):
    wbig = tables[0]
    N2 = wbig.shape[1]
    K = wbig.shape[0]
    B, n_pages = page_table.shape
    n_pg = math.gcd(n_pages, PAGES_PER_STEP)
    rows = n_pg * PAGE_SIZE // CMP_STRIDE
    page_spec = lambda k: pl.BlockSpec((1,) + pool_t.shape[1:],
                                       lambda b, i, pt, k=k: (pt[b, i * n_pg + k], 0, 0, 0, 0))
    grid_spec = pltpu.PrefetchScalarGridSpec(
        num_scalar_prefetch=1,
        grid=(B, n_pages // n_pg),
        in_specs=[page_spec(k) for k in range(n_pg)] + [pl.BlockSpec((K, N2), lambda b, i, pt: (0, 0))],
        out_specs=pl.BlockSpec((1, rows, N2), lambda b, i, pt: (b, i, 0)),
        scratch_shapes=[pltpu.VMEM((rows // PAGE_PARTS * CHUNK_PITCH, LANE), F32)
                        for _ in range(PAGE_PARTS * (D_KV // LANE))],
    )
    return pl.pallas_call(
        functools.partial(_compress_in_paged_kernel, n_pg=n_pg),
        out_shape=jax.ShapeDtypeStruct((B, n_pages * PAGE_SIZE // CMP_STRIDE, N2), F32),
        grid_spec=grid_spec,
        compiler_params=_cparams("arbitrary", "arbitrary"),
        name="compress_in_paged",
    )(page_table, *([pool_t] * n_pg), wbig)


def _compress_out_kernel(*refs):
    z_refs, (b1_ref, w2_ref, b2_ref, o_ref) = refs[:-4], refs[-4:]
    z = jnp.concatenate([z_ref[0] for z_ref in z_refs], axis=0)
    n = z.shape[-1] // 2
    rows = z.shape[0]
    second = pltpu.roll(z[:, n:], rows - 1, axis=0)
    hdn = jax.nn.gelu(z[:, :n] + second + b1_ref[...])
    o_ref[0, :rows, :] = jnp.dot(hdn.astype(BF16), w2_ref[...], preferred_element_type=F32) + b2_ref[...]
    if o_ref.shape[1] > rows:
        o_ref[0, rows:, :] = jnp.zeros((o_ref.shape[1] - rows, n), F32)


def _compress_out(zs, tables, n_out):
    _, b1, w2, b2 = tables
    B, _, N2 = zs[0].shape
    return pl.pallas_call(
        _compress_out_kernel,
        out_shape=jax.ShapeDtypeStruct((B, n_out, N2 // 2), F32),
        grid=(B,),
        in_specs=[pl.BlockSpec((1, z.shape[1], N2), lambda b: (b, 0, 0)) for z in zs] + [
                  pl.BlockSpec((1, N2 // 2), lambda b: (0, 0)),
                  pl.BlockSpec((N2 // 2, N2 // 2), lambda b: (0, 0)),
                  pl.BlockSpec((1, N2 // 2), lambda b: (0, 0))],
        out_specs=pl.BlockSpec((1, n_out, N2 // 2), lambda b: (b, 0, 0)),
        compiler_params=_cparams("parallel"),
        name="compress_out",
    )(*zs, b1, w2, b2)


def _rel_bucket(dist):
    n = jnp.maximum(dist, 0)
    max_exact = NUM_BUCKETS // 2
    nf = jnp.maximum(n, 1).astype(F32)
    large = max_exact + (jnp.log(nf / max_exact) / math.log(REL_MAX_DIST / max_exact)
                         * (NUM_BUCKETS - max_exact)).astype(jnp.int32)
    large = jnp.minimum(large, NUM_BUCKETS - 1)
    return jnp.where(n < max_exact, n, large)


def _bias_by_distance(rel_bias, n_max):
    onehot = (_rel_bucket(jnp.arange(n_max))[None, :] == jnp.arange(NUM_BUCKETS)[:, None]).astype(F32)
    return jnp.dot(jnp.transpose(rel_bias.astype(F32)), onehot, precision=HIGHEST)


def _shifted_chunks(bias_n, pad, n_chunks, width):
    n = min(bias_n.shape[1], n_chunks * width - pad)
    ext = jnp.concatenate([jnp.broadcast_to(bias_n[:, :1], (N_HEADS, pad)), bias_n[:, :n],
                           jnp.zeros((N_HEADS, n_chunks * width - pad - n), F32)], axis=1)
    return ext.reshape(N_HEADS, n_chunks, width)


def _bias_tables_kernel(ed_ref, ec_ref, tzs_ref, tzw_ref, cmp_ref, *, tq, tk, n_qt):
    n_ds, n_dw, n_j = tzs_ref.shape[1] - 1, tzw_ref.shape[1] - 1, cmp_ref.shape[1] // 8
    tzs_ref[0, n_ds] = jnp.full((tk, tq), NEG, F32)
    tzw_ref[0, n_dw] = jnp.full((tk, tq), NEG, F32)
    w = tq + tk
    c = lax.broadcasted_iota(jnp.int32, (tk, tq), 0)
    r = lax.broadcasted_iota(jnp.int32, (tk, tq), 1)
    for d in range(n_ds):
        v = jnp.concatenate([ed_ref[0, d:d + 1, :], ed_ref[0, d + 1:d + 2, :]], axis=1)
        t = pltpu.roll(jnp.broadcast_to(v, (tk, w)), w - (tk - 1), axis=1, stride=1, stride_axis=0)[:, :tq]
        dist = d * tk + r - c
        tzs_ref[0, d] = jnp.where(dist >= 0, t, NEG)
        if d < n_dw:
            tzw_ref[0, d] = jnp.where((dist >= 0) & (dist <= WINDOW), t, NEG)
    for j in range(n_j):
        dd = n_qt - 1 - j
        c0, c1 = max(dd, 0), max(dd + 1, 0)
        v = jnp.concatenate([ec_ref[0, c0:c0 + 1, :], ec_ref[0, c1:c1 + 1, :]], axis=1)
        t = pltpu.roll(jnp.broadcast_to(v, (8, w)), w - 7 * CMP_STRIDE, axis=1, stride=CMP_STRIDE, stride_axis=0)
        dist = tq * dd + r[:8] - CMP_STRIDE * c[:8] - (CMP_BLOCK - 1)
        cmp_ref[0, j * 8:(j + 1) * 8, :] = jnp.where(dist >= 0, t[:, :tq], NEG)


def _bias_tables(bias_n, n_qt, n_rb, n_ds, n_dw, tq, tk):
    assert tq == tk == 8 * CMP_STRIDE and n_dw <= n_ds
    n_j = n_rb + n_qt - 1
    ed = _shifted_chunks(bias_n, tk - 1, n_ds + 1, tq)
    ec = _shifted_chunks(bias_n, 7 * CMP_STRIDE + CMP_BLOCK - 1, n_qt + 1, tq)
    head = lambda a: pl.BlockSpec((1,) + a.shape[1:], lambda h: (h,) + (0,) * (a.ndim - 1))
    outs = (jax.ShapeDtypeStruct((N_HEADS, n_ds + 1, tk, tq), F32),
            jax.ShapeDtypeStruct((N_HEADS, n_dw + 1, tk, tq), F32),
            jax.ShapeDtypeStruct((N_HEADS, n_j * 8, tq), F32))
    tzs, tzw, cmp = pl.pallas_call(
        functools.partial(_bias_tables_kernel, tq=tq, tk=tk, n_qt=n_qt),
        out_shape=outs,
        grid=(N_HEADS,),
        in_specs=[head(ed), head(ec)],
        out_specs=tuple(head(o) for o in outs),
        compiler_params=_cparams("parallel"),
        name="bias_tables",
    )(ed, ec)
    grp = lambda a: a.reshape((N_KV_HEADS, GQA) + a.shape[1:])
    return grp(tzs), grp(tzw), cmp


def _pool_matrix(n_cmp_pad, n_blk_pad):
    r = SEL_BLOCK // CMP_STRIDE
    i = np.arange(n_cmp_pad)[None, :]
    j = np.arange(n_blk_pad)[:, None]
    return ((i >= r * j - 1) & (i <= r * j + r - 1)).astype(np.float32)


def _cmp_select_kernel(q_ref, k_ref, vt_ref, bias_ref, pool_ref, o_ref, sel_ref, *, tq):
    qt = pl.program_id(2)
    n_qt = pl.num_programs(2)
    q = q_ref[0, 0].reshape(GQA * tq, HEAD_DIM)
    k = k_ref[0, 0]
    nc = k.shape[0]
    s = _nt_dot(k, q)
    row0 = pl.multiple_of((n_qt - 1 - qt) * 8, 8)
    s = s + jnp.concatenate([bias_ref[g, pl.ds(row0, nc), :] for g in range(GQA)], axis=-1)
    m = jnp.maximum(jnp.max(s, axis=0, keepdims=True), 0.5 * NEG)
    p = jnp.exp(s - m)
    p = p * (1.0 / jnp.maximum(jnp.sum(p, axis=0, keepdims=True), 1e-30))
    ot = jnp.dot(vt_ref[0, 0], p.astype(BF16), preferred_element_type=F32)
    o_ref[0] = jnp.concatenate([ot[:, g * tq:(g + 1) * tq].T for g in range(GQA)], axis=-1)
    imp = p[:, 0:tq]
    for g in range(1, GQA):
        imp = imp + p[:, g * tq:(g + 1) * tq]
    sb = jnp.dot(pool_ref[...], imp, precision=HIGHEST, preferred_element_type=F32)
    nb = sb.shape[0]
    blk = lax.broadcasted_iota(jnp.int32, (nb, tq), 0)
    cur = (qt * tq + lax.broadcasted_iota(jnp.int32, (nb, tq), 1)) // SEL_BLOCK
    causal = blk <= cur
    forced = (blk == 0) | (blk == cur) | (blk == cur - 1)
    sc = jnp.where(forced & causal, 1e4, jnp.where(causal, sb, -1.0))
    groups = [sc[r:r + 8] for r in range(0, nb, 8)]
    sub = lax.broadcasted_iota(jnp.int32, (8, tq), 0)
    ranks = [jnp.zeros((8, tq), F32) for _ in groups]
    for i in range(nb):
        row = sc[i:i + 1, :]
        for gi, grp in enumerate(groups):
            if gi * 8 > i:
                ahead = row >= grp
            elif gi * 8 + 7 < i:
                ahead = row > grp
            else:
                ahead = (row > grp) | ((row == grp) & (sub > i - gi * 8))
            ranks[gi] = ranks[gi] + jnp.where(ahead, 1.0, 0.0)
    rank = jnp.concatenate(ranks, axis=0)
    sel_ref[0, 0] = jnp.where((rank < N_SEL) & causal, 0.0, NEG)


def _cmp_select_prompt(q5, kc, vct, bias_tab, pool):
    B, _, _, T, _ = q5.shape
    NC = kc.shape[2]
    NB = pool.shape[0]
    R = bias_tab.shape[1]
    tq = ATT_TQ
    return pl.pallas_call(
        functools.partial(_cmp_select_kernel, tq=tq),
        out_shape=(jax.ShapeDtypeStruct((B, T, D_ATT), F32),
                   jax.ShapeDtypeStruct((B, N_KV_HEADS, NB, T), F32)),
        grid=(B, N_KV_HEADS, T // tq),
        in_specs=[pl.BlockSpec((1, 1, GQA, tq, HEAD_DIM), lambda b, h, i: (b, h, 0, i, 0)),
                  pl.BlockSpec((1, 1, NC, HEAD_DIM), lambda b, h, i: (b, h, 0, 0)),
                  pl.BlockSpec((1, 1, HEAD_DIM, NC), lambda b, h, i: (b, h, 0, 0)),
                  pl.BlockSpec((GQA, R, tq), lambda b, h, i: (h, 0, 0)),
                  pl.BlockSpec((NB, NC), lambda b, h, i: (0, 0))],
        out_specs=(pl.BlockSpec((1, tq, GQA * HEAD_DIM), lambda b, h, i: (b, i, h)),
                   pl.BlockSpec((1, 1, NB, tq), lambda b, h, i: (b, h, 0, i))),
        compiler_params=_cparams("parallel", "parallel", "parallel"),
        name="cmp_select_prompt",
    )(q5, kc, vct, bias_tab, pool)


def _sel_win_kernel(q_ref, ks_ref, vst_ref, kw_ref, vwt_ref, sel_ref, tzs_ref, tzw_ref, os_ref, ow_ref, *, tq):
    tk = ATT_TK
    qt = pl.program_id(2)
    q = q_ref[0, 0].reshape(GQA * tq, HEAD_DIM)
    width = GQA * tq
    per_tile = tk // SEL_BLOCK

    def make_sweep(k_ref, vt_ref, tz_ref, use_sel, n_chains, single_trip):
        n_d = tz_ref.shape[2] - 1

        def scores(kt, hi):
            pad = kt > hi
            kt = jnp.minimum(kt, hi)
            off = pl.multiple_of(kt * tk, tk)
            k = k_ref[0, 0, pl.ds(off, tk), :]
            d = jnp.where(pad, n_d, jnp.minimum(qt - kt, n_d - 1))
            bias = [tz_ref[0, g, d] for g in range(GQA)]
            if use_sel:
                rows = sel_ref[0, 0, pl.ds(kt * per_tile, per_tile), :]
                selb = jnp.concatenate([jnp.broadcast_to(rows[i:i + 1], (SEL_BLOCK, tq))
                                        for i in range(per_tile)], axis=0)
                bias = [b + selb for b in bias]
            return _nt_dot(k, q) + jnp.concatenate(bias, axis=1)

        def values_t(kt, lo, hi):
            off = pl.multiple_of(jnp.clip(kt, lo, hi) * tk, tk)
            return vt_ref[0, 0, :, pl.ds(off, tk)]

        def sweep(lo, hi):
            n_trips = (hi - lo + n_chains) // n_chains

            def first_trip():
                out = []
                for c in range(n_chains):
                    s = scores(lo + c, hi)
                    m = jnp.maximum(jnp.max(s, axis=0, keepdims=True), 0.5 * NEG)
                    p = jnp.exp(s - m)
                    out.append((m, jnp.sum(p, axis=0, keepdims=True), jnp.zeros((HEAD_DIM, width), F32),
                                jnp.ones((1, width), F32), p.astype(BF16)))
                return tuple(out)

            def trip(i, chains):
                kt = lo + n_chains * i
                pv = [jnp.dot(values_t(kt - n_chains + c, lo, hi), chains[c][4], preferred_element_type=F32)
                      for c in range(n_chains)]
                ss = [scores(kt + c, hi) for c in range(n_chains)]
                out = []
                for c in range(n_chains):
                    m, l, acc, alpha_prev, _ = chains[c]
                    m_new = jnp.maximum(m, jnp.max(ss[c], axis=0, keepdims=True))
                    alpha = jnp.exp(m - m_new)
                    p = jnp.exp(ss[c] - m_new)
                    l = alpha * l + jnp.sum(p, axis=0, keepdims=True)
                    out.append((m_new, l, alpha_prev * acc + pv[c], alpha, p.astype(BF16)))
                return tuple(out)

            if single_trip:
                done = []
                for c in range(n_chains):
                    s = scores(lo + c, hi)
                    m = jnp.maximum(jnp.max(s, axis=0, keepdims=True), 0.5 * NEG)
                    p = jnp.exp(s - m)
                    done.append((m, jnp.sum(p, axis=0, keepdims=True),
                                 jnp.dot(values_t(lo + c, lo, hi), p.astype(BF16), preferred_element_type=F32)))
            else:
                chains = lax.fori_loop(1, n_trips, trip, first_trip())
                kt_last = lo + n_chains * (n_trips - 1)
                done = []
                for c in range(n_chains):
                    m, l, acc, alpha, p = chains[c]
                    done.append((m, l, alpha * acc + jnp.dot(values_t(kt_last + c, lo, hi), p,
                                                              preferred_element_type=F32)))
            m_all = functools.reduce(jnp.maximum, [m for m, _, _ in done])
            num = den = 0.0
            for m, l, acc in done:
                e = jnp.exp(m - m_all)
                num = num + acc * e
                den = den + l * e
            o = num / jnp.maximum(den, 1e-30)
            return jnp.concatenate([o[:, g * tq:(g + 1) * tq].T for g in range(GQA)], axis=-1)
        return sweep

    n_win = tzw_ref.shape[2] - 1
    os_ref[0] = make_sweep(ks_ref, vst_ref, tzs_ref, True, SEL_CHAINS, False)(0, qt)
    ow_ref[0] = make_sweep(kw_ref, vwt_ref, tzw_ref, False, n_win, True)(jnp.maximum(qt - (n_win - 1), 0), qt)


def _sel_win_prompt(q5, ks, vst, kw, vwt, sel, tzs, tzw):
    B, _, _, T, _ = q5.shape
    NB = sel.shape[2]
    tq = ATT_TQ
    k_spec = pl.BlockSpec((1, 1, T, HEAD_DIM), lambda b, h, i: (b, h, 0, 0))
    vt_spec = pl.BlockSpec((1, 1, HEAD_DIM, T), lambda b, h, i: (b, h, 0, 0))
    tz_spec = lambda tz: pl.BlockSpec((1,) + tz.shape[1:], lambda b, h, i: (h, 0, 0, 0, 0))
    o_spec = pl.BlockSpec((1, tq, GQA * HEAD_DIM), lambda b, h, i: (b, i, h))
    return pl.pallas_call(
        functools.partial(_sel_win_kernel, tq=tq),
        out_shape=(jax.ShapeDtypeStruct((B, T, D_ATT), F32), jax.ShapeDtypeStruct((B, T, D_ATT), F32)),
        grid=(B, N_KV_HEADS, T // tq),
        in_specs=[pl.BlockSpec((1, 1, GQA, tq, HEAD_DIM), lambda b, h, i: (b, h, 0, i, 0)),
                  k_spec, vt_spec, k_spec, vt_spec,
                  pl.BlockSpec((1, 1, NB, tq), lambda b, h, i: (b, h, 0, i)),
                  tz_spec(tzs), tz_spec(tzw)],
        out_specs=(o_spec, o_spec),
        compiler_params=_cparams("parallel", "parallel", "parallel"),
        name="sel_win_prompt",
    )(q5, ks, vst, kw, vwt, sel, tzs, tzw)


def _gate_expand_matrix():
    m = np.zeros((3, 2 * LANE, D_ATT), np.float32)
    for r in range(3):
        for h in range(N_HEADS):
            m[r, h * 3 + r, h * HEAD_DIM:(h + 1) * HEAD_DIM] = 1.0
            m[r, LANE + h * 3 + r, h * HEAD_DIM:(h + 1) * HEAD_DIM] = 1.0
    return m


def _split_bf16(x):
    hi = x.astype(BF16)
    return hi, (x - hi.astype(F32)).astype(BF16)


def _post_mixer_kernel(y_ref, u_ref, oc_ref, os_ref, ow_ref, g_ref, x_ref, gate_ref, sh_ref, sc_ref,
                       dskip_ref, wglu_ref, bglu_ref, gexp_ref, wout_ref, lng_ref, lnb_ref,
                       wr_ref, br_ref, x1_ref, hm_ref, te_ref, tw_ref):
    y = y_ref[0] + dskip_ref[...] * u_ref[0]
    gl = jax.nn.gelu(y)
    ssm = gl * jax.nn.sigmoid(jnp.dot(gl.astype(BF16), wglu_ref[...], preferred_element_type=F32)
                              + bglu_ref[...])
    sg = jnp.concatenate(_split_bf16(jax.nn.sigmoid(g_ref[0])), axis=1)
    att = jnp.zeros_like(oc_ref[0])
    for r, o_ref in enumerate((oc_ref, os_ref, ow_ref)):
        att = att + jnp.dot(sg, gexp_ref[r], preferred_element_type=F32) * o_ref[0]
    h = (jnp.dot(ssm.astype(BF16), wout_ref[:D_SSM, :], preferred_element_type=F32)
         + jnp.dot(att.astype(BF16), wout_ref[D_SSM:, :], preferred_element_type=F32))
    z = DN_ALPHA * x_ref[0] + gate_ref[0] * h
    x1 = _layer_norm(z) * lng_ref[...] + lnb_ref[...]
    x1_ref[0] = x1
    hm = _layer_norm(x1) * (1.0 + sc_ref[0]) + sh_ref[0]
    hm_ref[0] = hm
    hm_hi, hm_lo = _split_bf16(hm)
    logits = (jnp.dot(hm_hi, wr_ref[0], preferred_element_type=F32)
              + jnp.dot(hm_lo, wr_ref[0], preferred_element_type=F32)
              + jnp.dot(hm_hi, wr_ref[1], preferred_element_type=F32)) + br_ref[...]
    lane = lax.broadcasted_iota(jnp.int32, logits.shape, 1)
    work = jnp.where(lane < N_EXPERTS, logits, -jnp.inf)
    te = jnp.zeros(logits.shape, jnp.int32)
    tv = jnp.zeros(logits.shape, F32)
    for k in range(TOP_K):
        best = jnp.max(work, axis=-1, keepdims=True)
        arg = jnp.min(jnp.where(work == best, lane, LANE), axis=-1, keepdims=True)
        te = jnp.where(lane == k, arg, te)
        tv = jnp.where(lane == k, best, tv)
        work = jnp.where(lane == arg, -jnp.inf, work)
    ex = jnp.where(lane < TOP_K, jnp.exp(tv - tv[:, 0:1]), 0.0)
    te_ref[0] = te
    tw_ref[0] = ex / jnp.sum(ex, axis=-1, keepdims=True)


def _post_mixer(y, u, oc, osel, ow, g, x, gate, shift, scale, w, tm):
    B, T, D = x.shape
    R = gate.shape[1]
    rb = 1 if R == 1 else tm
    mod_map = (lambda b, i: (b, 0, 0)) if R == 1 else (lambda b, i: (b, i, 0))
    row = lambda n: pl.BlockSpec((1, tm, n), lambda b, i: (b, i, 0))
    mod = pl.BlockSpec((1, rb, D), mod_map)
    full = lambda a: pl.BlockSpec(a.shape, lambda b, i: (0,) * a.ndim)
    consts = (w['d_skip'], w['w_glu'], w['b_glu'], w['gexp'], w['w_out'], w['ln1_g'], w['ln1_b'],
              w['w_router'], w['b_router'])
    return pl.pallas_call(
        _post_mixer_kernel,
        out_shape=(jax.ShapeDtypeStruct((B, T, D), F32), jax.ShapeDtypeStruct((B, T, D), F32),
                   jax.ShapeDtypeStruct((B, T, LANE), jnp.int32), jax.ShapeDtypeStruct((B, T, LANE), F32)),
        grid=(B, T // tm),
        in_specs=[row(D_SSM), row(D_SSM), row(D_ATT), row(D_ATT), row(D_ATT), row(LANE), row(D),
                  mod, mod, mod] + [full(a) for a in consts],
        out_specs=(row(D), row(D), row(LANE), row(LANE)),
        compiler_params=_cparams("parallel", "parallel"),
        name="post_mixer",
    )(y, u, oc, osel, ow, g, x, gate, shift, scale, *consts)


def _expert_kernel(e_ref, blk_ref, lo_ref, hi_ref, first_ref, x_ref, wgu_ref, bgu_ref, wd_ref, bd_ref, o_ref,
                   wgu_s, wd_s):
    i = pl.program_id(0)
    fresh = (i == 0) | (e_ref[i] != e_ref[jnp.maximum(i - 1, 0)])

    @pl.when(fresh)
    def _():
        wgu_s[...] = wgu_ref[0].astype(BF16)
        wd_s[...] = wd_ref[0].astype(BF16)

    @pl.when(first_ref[i] == 1)
    def _():
        o_ref[...] = jnp.zeros_like(o_ref)

    @pl.when(hi_ref[i] > lo_ref[i])
    def _():
        gu = jnp.dot(x_ref[...].astype(BF16), wgu_s[...], preferred_element_type=F32) + bgu_ref[0]
        gate = jnp.minimum(gu[:, :D_FF], SWIGLU_LIMIT)
        up = jnp.clip(gu[:, D_FF:], -SWIGLU_LIMIT, SWIGLU_LIMIT)
        hh = (up + 1.0) * gate * jax.nn.sigmoid(SWIGLU_ALPHA * gate)
        y = jnp.dot(hh.astype(BF16), wd_s[...], preferred_element_type=F32) + bd_ref[0]
        row = blk_ref[i] * MOE_ROWS + lax.broadcasted_iota(jnp.int32, (MOE_ROWS, 1), 0)
        o_ref[...] = jnp.where((row >= lo_ref[i]) & (row < hi_ref[i]), y, o_ref[...])


def _experts(xb, items, w_gate_up, b_gate_up, w_down, b_down):
    rows, D = xb.shape
    n_items = items[0].shape[0]
    wmap = lambda i, e, blk, lo, hi, first: (e[i], 0, 0)
    rmap = lambda i, e, blk, lo, hi, first: (blk[i], 0)
    grid_spec = pltpu.PrefetchScalarGridSpec(
        num_scalar_prefetch=5,
        grid=(n_items,),
        in_specs=[pl.BlockSpec((MOE_ROWS, D), rmap),
                  pl.BlockSpec((1, D, 2 * D_FF), wmap),
                  pl.BlockSpec((1, 1, 2 * D_FF), wmap),
                  pl.BlockSpec((1, D_FF, D), wmap),
                  pl.BlockSpec((1, 1, D), wmap)],
        out_specs=pl.BlockSpec((MOE_ROWS, D), rmap),
        scratch_shapes=[pltpu.VMEM((D, 2 * D_FF), BF16), pltpu.VMEM((D_FF, D), BF16)],
    )
    return pl.pallas_call(
        _expert_kernel,
        out_shape=jax.ShapeDtypeStruct((rows, D), F32),
        grid_spec=grid_spec,
        compiler_params=_cparams("arbitrary"),
        name="moe_experts",
    )(*items, xb, w_gate_up, b_gate_up.reshape(N_EXPERTS, 1, 2 * D_FF), w_down,
      b_down.reshape(N_EXPERTS, 1, D))


def _moe_dispatch(top_e, n):
    blk = MOE_ROWS
    nk = n * TOP_K
    cb = 128
    assert nk % cb == 0
    e = top_e.reshape(-1)
    oh = (jnp.arange(N_EXPERTS)[:, None] == e[None, :]).astype(BF16).reshape(N_EXPERTS, nk // cb, cb)
    before = jnp.asarray(np.triu(np.ones((cb, cb), np.float32), 1), dtype=BF16)
    within = jnp.einsum('ebj,ji->ebi', oh, before, preferred_element_type=F32)
    blk_tot = jnp.sum(oh.astype(F32), axis=2)
    blk_off = jnp.cumsum(blk_tot, axis=1) - blk_tot
    counts = jnp.sum(blk_tot, axis=1)
    start = jnp.cumsum(counts) - counts
    dest = jnp.sum((within + (blk_off + start[:, None])[:, :, None]) * oh.astype(F32), axis=0)
    dest = dest.reshape(nk).astype(jnp.int32)
    order = jnp.argsort(dest)
    n_blk = -(-nk // blk)
    row_tok = jnp.concatenate([(order // TOP_K).astype(jnp.int32), jnp.full((n_blk * blk - nk,), n, jnp.int32)])
    counts_i, start_i = counts.astype(jnp.int32), start.astype(jnp.int32)
    first_b = start_i // blk
    last_b = (start_i + counts_i - 1) // blk
    n_it = jnp.where(counts_i > 0, last_b - first_b + 1, 0)
    it_end = jnp.cumsum(n_it)
    it_start = it_end - n_it
    n_items = n_blk + N_EXPERTS - 1
    i = jnp.arange(n_items)
    live = i < it_end[-1]
    it_e = jnp.minimum(jnp.sum(it_end[None, :] <= i[:, None], axis=1), N_EXPERTS - 1)
    it_blk = jnp.where(live, first_b[it_e] + i - it_start[it_e], n_blk - 1)
    it_lo = jnp.where(live, start_i[it_e], 0)
    it_hi = jnp.where(live, start_i[it_e] + counts_i[it_e], 0)
    it_first = jnp.concatenate([jnp.ones((1,), jnp.int32), (it_blk[1:] != it_blk[:-1]).astype(jnp.int32)])
    items = tuple(a.astype(jnp.int32) for a in (it_e, it_blk, it_lo, it_hi, it_first))
    return row_tok, dest.reshape(n, TOP_K), items


def _final_kernel(x_ref, y0_ref, y1_ref, y2_ref, y3_ref, tw_ref, gate_ref, lng_ref, lnb_ref, o_ref):
    tw = tw_ref[0]
    y = jnp.zeros_like(x_ref[0])
    for k, y_ref in enumerate((y0_ref, y1_ref, y2_ref, y3_ref)):
        y = y + tw[:, k:k + 1] * y_ref[0]
    z = DN_ALPHA * x_ref[0] + gate_ref[0] * y
    o_ref[0] = _layer_norm(z) * lng_ref[...] + lnb_ref[...]


def _final(x1, ys, tw, gate, ln_g, ln_b, tm):
    B, T, D = x1.shape
    R = gate.shape[1]
    rb = 1 if R == 1 else tm
    mod_map = (lambda b, i: (b, 0, 0)) if R == 1 else (lambda b, i: (b, i, 0))
    row = lambda n: pl.BlockSpec((1, tm, n), lambda b, i: (b, i, 0))
    vec = pl.BlockSpec((1, D), lambda b, i: (0, 0))
    return pl.pallas_call(
        _final_kernel,
        out_shape=jax.ShapeDtypeStruct((B, T, D), F32),
        grid=(B, T // tm),
        in_specs=[row(D), row(D), row(D), row(D), row(D), row(LANE),
                  pl.BlockSpec((1, rb, D), mod_map), vec, vec],
        out_specs=row(D),
        compiler_params=_cparams("parallel", "parallel"),
        name="moe_combine_ln",
    )(x1, *ys, tw, gate, ln_g, ln_b)


def _cmp_select_step_kernel(q_ref, kv_ref, bias_ref, pool_ref, o_ref, idx_ref, *, n_cmp, n_blk, q_pos):
    q = q_ref[0].astype(BF16)
    ncp = kv_ref.shape[1]
    nbp = pool_ref.shape[1]
    hd = HEAD_DIM
    kv = kv_ref[0]
    kb = [kv[:, h * hd:(h + 1) * hd].astype(BF16) for h in range(N_KV_HEADS)]
    vb = [kv[:, (N_KV_HEADS + h) * hd:(N_KV_HEADS + h + 1) * hd].astype(BF16) for h in range(N_KV_HEADS)]
    row = lax.broadcasted_iota(jnp.int32, (N_HEADS, 1), 0)
    first = row < GQA
    s = jnp.where(first, _nt_dot(q, kb[0]), _nt_dot(q, kb[1])) * (hd ** -0.5)
    s = s + bias_ref[...]
    ci = lax.broadcasted_iota(jnp.int32, (N_HEADS, ncp), 1)
    mask = (ci * CMP_STRIDE + CMP_BLOCK - 1 <= q_pos) & (ci < n_cmp)
    s = jnp.where(mask, s, NEG)
    m = jnp.max(s, axis=-1, keepdims=True)
    p = jnp.where(mask, jnp.exp(s - m), 0.0)
    p = p / jnp.maximum(jnp.sum(p, axis=-1, keepdims=True), 1e-30)
    pb = p.astype(BF16)
    o_ref[0] = jnp.where(first, jnp.dot(pb, vb[0], preferred_element_type=F32),
                         jnp.dot(pb, vb[1], preferred_element_type=F32))
    imp0 = jnp.sum(jnp.where(first, p, 0.0), axis=0, keepdims=True)
    imp1 = jnp.sum(jnp.where(first, 0.0, p), axis=0, keepdims=True)
    imp = jnp.where(first, imp0, imp1)
    sb = jnp.dot(imp, pool_ref[...], precision=HIGHEST, preferred_element_type=F32)
    cur = q_pos // SEL_BLOCK
    bi = lax.broadcasted_iota(jnp.int32, (nbp, nbp), 0)
    bj = lax.broadcasted_iota(jnp.int32, (nbp, nbp), 1)
    blk = lax.broadcasted_iota(jnp.int32, (1, nbp), 1)
    causal = blk <= cur
    forced = (blk == 0) | (blk == cur) | (blk == cur - 1)
    rsel = lax.broadcasted_iota(jnp.int32, (N_SEL, nbp), 0)
    for h in range(N_KV_HEADS):
        sc = jnp.where(forced & causal, 1e4, jnp.where(causal, sb[h * GQA:h * GQA + 1, :], -1.0))
        sc = jnp.where(blk < n_blk, sc, -2.0)
        scb = jnp.broadcast_to(sc, (nbp, nbp))
        col = jnp.sum(jnp.where(bi == bj, scb, 0.0), axis=1, keepdims=True)
        ahead = (col > scb) | ((col == scb) & (bi < bj))
        rank = jnp.sum(ahead.astype(jnp.int32), axis=0, keepdims=True)
        hit = jnp.broadcast_to(rank, (N_SEL, nbp)) == rsel
        idx = jnp.sum(jnp.where(hit, jnp.broadcast_to(blk, (N_SEL, nbp)), 0), axis=1, keepdims=True)
        idx_ref[0, h] = jnp.broadcast_to(idx, (N_SEL, LANE))


def _cmp_select_step(q, ckv, bias, pool, n_cmp, n_blk, q_pos):
    B = q.shape[0]
    NCp = ckv.shape[1]
    return pl.pallas_call(
        functools.partial(_cmp_select_step_kernel, n_cmp=n_cmp, n_blk=n_blk, q_pos=q_pos),
        out_shape=(jax.ShapeDtypeStruct((B, N_HEADS, HEAD_DIM), F32),
                   jax.ShapeDtypeStruct((B, N_KV_HEADS, N_SEL, LANE), jnp.int32)),
        grid=(B,),
        in_specs=[pl.BlockSpec((1, N_HEADS, HEAD_DIM), lambda b: (b, 0, 0)),
                  pl.BlockSpec((1, NCp, D_KV), lambda b: (b, 0, 0)),
                  pl.BlockSpec(bias.shape, lambda b: (0, 0)),
                  pl.BlockSpec(pool.shape, lambda b: (0, 0))],
        out_specs=(pl.BlockSpec((1, N_HEADS, HEAD_DIM), lambda b: (b, 0, 0)),
                   pl.BlockSpec((1, N_KV_HEADS, N_SEL, LANE), lambda b: (b, 0, 0, 0))),
        compiler_params=_cparams("parallel"),
        name="cmp_select_step",
    )(q, ckv, bias, pool)


def _sel_step_kernel(pg_ref, idx_ref, q_ref, *refs, n_past, q_pos):
    page_refs = refs[:N_SEL]
    new_ref, bias_ref, kpos_ref, o_ref = refs[N_SEL:]
    b, h = pl.program_id(0), pl.program_id(1)
    base = (b * N_KV_HEADS + h) * N_SEL
    kts, vts = [], []
    for j in range(N_SEL):
        is_new = idx_ref[base + j] >= n_past
        kts.append(jnp.where(is_new, new_ref[0, 0, 0], page_refs[j][0, 0, 0]))
        vts.append(jnp.where(is_new, new_ref[0, 1, 0], page_refs[j][0, 1, 0]))
    kt = jnp.concatenate(kts, axis=1).astype(BF16)
    vt = jnp.concatenate(vts, axis=1).astype(BF16)
    s = jnp.dot(q_ref[0].astype(BF16), kt, preferred_element_type=F32) * (HEAD_DIM ** -0.5) + bias_ref[0, 0]
    mask = kpos_ref[0, 0] <= q_pos
    s = jnp.where(mask, s, NEG)
    m = jnp.max(s, axis=-1, keepdims=True)
    p = jnp.where(mask, jnp.exp(s - m), 0.0)
    l = jnp.sum(p, axis=-1, keepdims=True)
    o_ref[0, 0] = _nt_dot(p.astype(BF16), vt) / jnp.maximum(l, 1e-30)


def _sel_step(q, pool_t, new_t, bias_sel, kpos, pages, idx_flat, n_past, q_pos):
    B = q.shape[0]
    nk = N_SEL * PAGE_SIZE
    slot = lambda b, h, j: (b * N_KV_HEADS + h) * N_SEL + j
    page_spec = lambda j: pl.BlockSpec((1, 2, 1, HEAD_DIM, PAGE_SIZE),
                                       lambda b, h, pg, ix, j=j: (pg[slot(b, h, j)], 0, h, 0, 0))
    grid_spec = pltpu.PrefetchScalarGridSpec(
        num_scalar_prefetch=2,
        grid=(B, N_KV_HEADS),
        in_specs=[pl.BlockSpec((1, N_HEADS, HEAD_DIM), lambda b, h, pg, ix: (b, 0, 0))]
        + [page_spec(j) for j in range(N_SEL)]
        + [pl.BlockSpec((1, 2, 1, HEAD_DIM, PAGE_SIZE), lambda b, h, pg, ix: (b, 0, h, 0, 0)),
           pl.BlockSpec((1, 1, N_HEADS, nk), lambda b, h, pg, ix: (b, h, 0, 0)),
           pl.BlockSpec((1, 1, 1, nk), lambda b, h, pg, ix: (b, h, 0, 0))],
        out_specs=pl.BlockSpec((1, 1, N_HEADS, HEAD_DIM), lambda b, h, pg, ix: (b, h, 0, 0)),
    )
    return pl.pallas_call(
        functools.partial(_sel_step_kernel, n_past=n_past, q_pos=q_pos),
        out_shape=jax.ShapeDtypeStruct((B, N_KV_HEADS, N_HEADS, HEAD_DIM), F32),
        grid_spec=grid_spec,
        compiler_params=_cparams("arbitrary", "arbitrary"),
        name="sel_step",
    )(pages, idx_flat, q, *([pool_t] * N_SEL), new_t, bias_sel, kpos)


def _win_step_kernel(q_ref, w_ref, new_ref, bias_ref, bias0_ref, o_ref):
    q = q_ref[0]
    qb = q.astype(BF16)
    row = lax.broadcasted_iota(jnp.int32, (N_HEADS, 1), 0)
    first = row < GQA
    hd = HEAD_DIM
    kt = [w_ref[0, 0, h].astype(BF16) for h in range(N_KV_HEADS)]
    vt = [w_ref[0, 1, h].astype(BF16) for h in range(N_KV_HEADS)]
    dots = [jnp.dot(qb, kt[h], preferred_element_type=F32) for h in range(N_KV_HEADS)]
    s = jnp.where(first, dots[0], dots[1]) * (hd ** -0.5) + bias_ref[...]
    new = new_ref[0]
    kn = jnp.where(first, new[:, 0:hd], new[:, hd:2 * hd])
    vn = jnp.where(first, new[:, 2 * hd:3 * hd], new[:, 3 * hd:])
    sn = jnp.sum(q * kn, axis=-1, keepdims=True) * (hd ** -0.5) + bias0_ref[...]
    m = jnp.maximum(jnp.max(s, axis=-1, keepdims=True), sn)
    p = jnp.exp(s - m)
    pn = jnp.exp(sn - m)
    l = jnp.sum(p, axis=-1, keepdims=True) + pn
    pb = p.astype(BF16)
    acc = jnp.where(first, _nt_dot(pb, vt[0]), _nt_dot(pb, vt[1])) + pn * vn
    o_ref[0] = acc / jnp.maximum(l, 1e-30)


def _win_step(q, win_t, new, bias, bias0):
    B, W = win_t.shape[0], win_t.shape[-1]
    return pl.pallas_call(
        _win_step_kernel,
        out_shape=jax.ShapeDtypeStruct((B, N_HEADS, HEAD_DIM), F32),
        grid=(B,),
        in_specs=[pl.BlockSpec((1, N_HEADS, HEAD_DIM), lambda b: (b, 0, 0)),
                  pl.BlockSpec((1,) + win_t.shape[1:], lambda b: (b, 0, 0, 0, 0)),
                  pl.BlockSpec((1, 1, D_KV), lambda b: (b, 0, 0)),
                  pl.BlockSpec((N_HEADS, W), lambda b: (0, 0)),
                  pl.BlockSpec((N_HEADS, 1), lambda b: (0, 0))],
        out_specs=pl.BlockSpec((1, N_HEADS, HEAD_DIM), lambda b: (b, 0, 0)),
        compiler_params=_cparams("parallel"),
        name="win_step",
    )(q, win_t, new, bias, bias0)


def _split_heads(kv, dtype):
    B, L, _ = kv.shape
    kv5 = kv.reshape(B, L, 2, N_KV_HEADS, HEAD_DIM)
    return (jnp.transpose(kv5[:, :, 0], (0, 2, 1, 3)).astype(dtype),
            jnp.transpose(kv5[:, :, 1], (0, 2, 1, 3)).astype(dtype))


def _nsa_prompt(q5, kvc, ks, vst, kw, vwt, cmp_tab, rel_bias):
    B, T, _ = kvc.shape
    nc = T // CMP_STRIDE
    nb = T // SEL_BLOCK
    ckv = _compress_out([_compress_in(kvc.reshape(B, nc, CMP_STRIDE * D_KV), cmp_tab)], cmp_tab, nc)
    kc, vc = _split_heads(ckv, BF16)
    vct = jnp.transpose(vc, (0, 1, 3, 2))
    bias_n = _bias_by_distance(rel_bias, T)
    n_qt, n_kt = T // ATT_TQ, T // ATT_TK
    n_ds = min(n_kt, -(-(REL_MAX_DIST + ATT_TK - 1) // ATT_TK) + 1)
    n_dw = min(n_kt, WINDOW // ATT_TK + 1)
    tzs, tzw, bias_tab = _bias_tables(bias_n, n_qt, nc // 8, n_ds, n_dw, ATT_TQ, ATT_TK)
    pool = jnp.asarray(_pool_matrix(nc, nb))
    o_cmp, sel = _cmp_select_prompt(q5, kc, vct, bias_tab, pool)
    o_sel, o_win = _sel_win_prompt(q5, ks, vst, kw, vwt, sel, tzs, tzw)
    return o_cmp, o_sel, o_win


def _nsa_sample(q, kvc, kvs, kvw, pool_cmp, pool_sel, win_buf, page_table, cmp_tab, rel_bias):
    B = q.shape[0]
    n_pages = page_table.shape[1]
    past_len = n_pages * PAGE_SIZE
    q_pos = past_len
    lp = -(-(past_len + 1) // SEL_BLOCK) * SEL_BLOCK
    n_cmp = lp // CMP_STRIDE - 1
    n_blk = lp // SEL_BLOCK
    n_past_chunks = past_len // CMP_STRIDE
    n_tail = 8
    assert n_past_chunks + n_tail >= n_cmp + 1
    n_chunks = n_past_chunks + n_tail
    feature_major = lambda pool: jnp.transpose(pool, (0, 2, 3, 4, 1))
    z_past = _compress_in_paged(feature_major(pool_cmp), page_table, cmp_tab)
    tail = jnp.pad(kvc[:, None, :], ((0, 0), (0, n_tail * CMP_STRIDE - 1), (0, 0)))
    z_tail = _compress_in(tail.reshape(B, n_tail, CMP_STRIDE * D_KV), cmp_tab)
    ncp = -(-n_chunks // LANE) * LANE
    nbp = -(-n_blk // LANE) * LANE
    ckv = _compress_out([z_past, z_tail], cmp_tab, ncp)
    bias_n = _bias_by_distance(rel_bias, q_pos + 1)
    n_back = max((n_pages + 1) * PAGE_SIZE, ncp * CMP_STRIDE + CMP_BLOCK)
    back = jnp.concatenate([bias_n[:, ::-1], jnp.broadcast_to(bias_n[:, :1], (N_HEADS, n_back - q_pos - 1))], 1)
    bias_c = back[:, CMP_BLOCK - 1:CMP_BLOCK - 1 + ncp * CMP_STRIDE:CMP_STRIDE]
    pool = jnp.asarray(_pool_matrix(ncp, nbp).T)
    q3 = q.reshape(B, N_HEADS, HEAD_DIM)
    o_cmp, idx = _cmp_select_step(q3, ckv, bias_c, pool, n_cmp, n_blk, q_pos)
    idx = idx[..., 0]
    bpp = PAGE_SIZE // SEL_BLOCK
    n_past = n_pages * bpp
    lpage = idx // bpp
    pages = jnp.take_along_axis(page_table, jnp.minimum(lpage, n_pages - 1).reshape(B, -1), axis=1)
    new_t = jnp.pad(kvs.reshape(B, 2, N_KV_HEADS, HEAD_DIM, 1), ((0, 0),) * 4 + ((0, PAGE_SIZE - 1),))
    bias_page = jnp.transpose(back[:, :(n_pages + 1) * PAGE_SIZE].reshape(N_HEADS, n_pages + 1, PAGE_SIZE),
                              (1, 0, 2))
    bias_sel = jnp.transpose(bias_page[lpage], (0, 1, 3, 2, 4)).reshape(B, N_KV_HEADS, N_HEADS, -1)
    kpos = lpage[..., None] * PAGE_SIZE + jnp.arange(PAGE_SIZE)
    ok = (kpos // SEL_BLOCK == idx[..., None]) & (idx <= q_pos // SEL_BLOCK)[..., None]
    kpos = jnp.where(ok, kpos, q_pos + 1).reshape(B, N_KV_HEADS, 1, -1).astype(jnp.int32)
    o_sel = _sel_step(q3, feature_major(pool_sel), new_t, bias_sel, kpos, pages.reshape(-1).astype(jnp.int32),
                      idx.reshape(-1).astype(jnp.int32), n_past, q_pos)
    o_sel = jnp.concatenate([o_sel[:, h, h * GQA:(h + 1) * GQA] for h in range(N_KV_HEADS)], axis=1)
    wb = win_buf.shape[1]
    bias_w = bias_n[:, 1:wb + 1][:, ::-1]
    o_win = _win_step(q3, feature_major(win_buf), kvw[:, None, :], bias_w, bias_n[:, 0:1])
    return o_cmp.reshape(B, D_ATT), o_sel.reshape(B, D_ATT), o_win.reshape(B, D_ATT)


def kernel(x_prompt, x_sample, cache_cmp_kv, cache_sel_kv, state_win_kv, state_ssm_re, state_ssm_im, page_table,
           c_prompt, c_sample, w_ada, b_ada, w_in, lam_re, lam_im, log_dt, b_re, b_im, c_re, c_im, d_skip,
           w_glu, b_glu, phi_pe, phi_w1, phi_b1, phi_w2, phi_b2, rel_bias, w_out, ln1_g, ln1_b,
           w_router, b_router, w_gate_up, b_gate_up, w_down, b_down, ln2_g, ln2_b):
    assert w_ada.shape[0] == DEPTH == 1
    l = 0
    Bp, T, D = x_prompt.shape
    Bs = x_sample.shape[0]
    kv_tail = (2, N_KV_HEADS, HEAD_DIM)

    n_c = Bp + Bs
    c_all = jnp.pad(jnp.concatenate([c_prompt, c_sample], 0), ((0, -n_c % 8), (0, 0)))
    m_all = _adaln(c_all, w_ada[l], b_ada[l])
    m_p = m_all[:Bp].reshape(Bp, 6, D)
    m_s = m_all[Bp:n_c].reshape(Bs, 6, D)
    mod_p = [m_p[:, i:i + 1, :] for i in range(6)]
    mod_s = [m_s[None, :, i, :] for i in range(6)]

    w_in_pad = jnp.pad(w_in[l], ((0, 0), (0, D_IN_PAD - D_IN))).astype(BF16)
    n_levels = max(1, int(math.log2(T // SSM_CHUNK)))
    ssm_tab = _ssm_tables(lam_re[l], lam_im[l], log_dt[l], b_re[l], b_im[l], c_re[l], c_im[l],
                          SSM_CHUNK, n_levels)
    cmp_tab = _compress_tables(phi_pe[l], phi_w1[l], phi_b1[l], phi_w2[l], phi_b2[l])
    w_post = dict(
        d_skip=d_skip[l].reshape(1, D_SSM), w_glu=w_glu[l].astype(BF16), b_glu=b_glu[l].reshape(1, D_SSM),
        gexp=jnp.asarray(_gate_expand_matrix(), dtype=BF16), w_out=w_out[l].astype(BF16),
        ln1_g=ln1_g[l].reshape(1, D), ln1_b=ln1_b[l].reshape(1, D),
        w_router=jnp.stack(_split_bf16(jnp.pad(w_router[l], ((0, 0), (0, LANE - N_EXPERTS))))),
        b_router=jnp.pad(b_router[l], (0, LANE - N_EXPERTS)).reshape(1, LANE))

    u, q5, kvc, kvs, kvw, g, ks, vst, kw, vwt = _mixer_in(x_prompt, mod_p[0], mod_p[1], w_in_pad, 512, True)
    y_ssm, h_p = _ssm_prompt(u, ssm_tab)
    o_cmp, o_sel, o_win = _nsa_prompt(q5, kvc, ks, vst, kw, vwt, cmp_tab, rel_bias)
    x1_p, hm_p, te_p, tw_p = _post_mixer(y_ssm, u, o_cmp, o_sel, o_win, g, x_prompt,
                                         mod_p[2], mod_p[3], mod_p[4], w_post, tm=512)

    u_s, q_s, kvc_s, kvs_s, kvw_s, g_s = _mixer_in(x_sample.reshape(1, Bs, D), mod_s[0], mod_s[1],
                                                   w_in_pad, Bs, False)
    y_s, h_s = _ssm_sample(u_s[0], state_ssm_re[l], state_ssm_im[l], ssm_tab, c_re[l], c_im[l])
    oc_s, os_s, ow_s = _nsa_sample(q_s[0].astype(F32), kvc_s[0], kvs_s[0], kvw_s[0], cache_cmp_kv[l],
                                   cache_sel_kv[l], state_win_kv[l], page_table, cmp_tab, rel_bias)
    x1_s, hm_s, te_s, tw_s = _post_mixer(y_s[None], u_s, oc_s[None], os_s[None], ow_s[None], g_s,
                                         x_sample.reshape(1, Bs, D), mod_s[2], mod_s[3], mod_s[4],
                                         w_post, tm=Bs)

    n_p = Bp * T
    n_all = n_p + Bs
    hm_all = jnp.concatenate([hm_p.reshape(n_p, D), hm_s.reshape(Bs, D)], 0)
    te_all = jnp.concatenate([te_p.reshape(n_p, LANE), te_s.reshape(Bs, LANE)], 0)[:, :TOP_K]
    row_tok, dest, items = _moe_dispatch(te_all, n_all)
    xb = jnp.concatenate([hm_all, jnp.zeros((1, D), F32)], 0)[row_tok]
    yb = _experts(xb, items, w_gate_up[l], b_gate_up[l], w_down[l], b_down[l])
    ys_p = [yb[dest[:n_p, k]].reshape(Bp, T, D) for k in range(TOP_K)]
    ys_s = [yb[dest[n_p:, k]].reshape(1, Bs, D) for k in range(TOP_K)]
    ln2g, ln2b = ln2_g[l].reshape(1, D), ln2_b[l].reshape(1, D)
    out_p = _final(x1_p, ys_p, tw_p, mod_p[5], ln2g, ln2b, tm=512)
    out_s = _final(x1_s, ys_s, tw_s, mod_s[5], ln2g, ln2b, tm=Bs)

    wlen = min(WINDOW, T)
    win_s = jnp.concatenate([state_win_kv[l], kvw_s[0].reshape(Bs, 1, *kv_tail)], 1)[:, -state_win_kv.shape[2]:]
    p_state = SSM_STATE
    return (out_p, out_s.reshape(Bs, 1, D),
            kvc.reshape(1, Bp, T, *kv_tail), kvc_s[0].reshape(1, Bs, 1, *kv_tail),
            kvs.reshape(1, Bp, T, *kv_tail), kvs_s[0].reshape(1, Bs, 1, *kv_tail),
            kvw[:, T - wlen:].reshape(1, Bp, wlen, *kv_tail), win_s[None],
            h_p[None, ..., :p_state], h_p[None, ..., p_state:],
            h_s[None, ..., :p_state], h_s[None, ..., p_state:])
```

```python
import functools
import math

import numpy as np
import jax
import jax.numpy as jnp
from jax import lax
from jax.experimental import pallas as pl
from jax.experimental.pallas import tpu as pltpu

DEPTH = 1
PAGE_SIZE = 128
D_SSM = 512
SSM_GROUP = 16
N_SSM_GROUPS = D_SSM // SSM_GROUP
SSM_STATE = 64
N_HEADS = 8
HEAD_DIM = 64
N_KV_HEADS = 2
GQA = N_HEADS // N_KV_HEADS
D_ATT = N_HEADS * HEAD_DIM
D_KV = 2 * N_KV_HEADS * HEAD_DIM
CMP_STRIDE = 16
CMP_BLOCK = 2 * CMP_STRIDE
SEL_BLOCK = 64
N_SEL = 16
WINDOW = 512
NUM_BUCKETS = 32
REL_MAX_DIST = 1024
N_EXPERTS = 32
TOP_K = 4
D_FF = 1024
SWIGLU_LIMIT = 7.0
SWIGLU_ALPHA = 1.702
DN_ALPHA = (2 * DEPTH) ** 0.25
D_IN = D_SSM + D_ATT + 3 * D_KV + 3 * N_HEADS
NEG = -1e30
F32 = jnp.float32
BF16 = jnp.bfloat16
HIGHEST = lax.Precision.HIGHEST

LANE = 128
D_IN_PAD = -(-D_IN // LANE) * LANE
SSM_CHUNK = 8
ATT_TQ = 128
ATT_TK = 128
SEL_CHAINS = 4
SEL_HEADS = 2
MOE_ROWS = 256
PAGES_PER_STEP = 64
PAGE_PARTS = 4
CHUNK_PITCH = 24
VMEM_LIMIT = 48 * 1024 * 1024
LN_EPS = 1e-5


def _cparams(*sem):
    return pltpu.CompilerParams(dimension_semantics=sem, vmem_limit_bytes=VMEM_LIMIT)


def _nt_dot(a, b):
    return lax.dot_general(a, b, (((1,), (1,)), ((), ())), preferred_element_type=F32)


def _layer_norm(x):
    mu = jnp.mean(x, axis=-1, keepdims=True)
    xc = x - mu
    var = jnp.mean(xc * xc, axis=-1, keepdims=True)
    return xc * lax.rsqrt(var + LN_EPS)


def _adaln_kernel(c_ref, w_ref, b_ref, o_ref):
    c = c_ref[...]
    s = c * jax.nn.sigmoid(c)
    o_ref[...] = jnp.dot(s, w_ref[...], precision=HIGHEST, preferred_element_type=F32) + b_ref[...]


def _adaln(c, w, b):
    n, d = c.shape
    dout = w.shape[1]
    tn = 1024
    return pl.pallas_call(
        _adaln_kernel,
        out_shape=jax.ShapeDtypeStruct((n, dout), F32),
        grid=(dout // tn,),
        in_specs=[pl.BlockSpec((n, d), lambda j: (0, 0)),
                  pl.BlockSpec((d, tn), lambda j: (0, j)),
                  pl.BlockSpec((1, tn), lambda j: (0, j))],
        out_specs=pl.BlockSpec((n, tn), lambda j: (0, j)),
        compiler_params=_cparams("arbitrary"),
        name="adaln",
    )(c, w, b.reshape(1, dout))


def _mixer_in_kernel(x_ref, sh_ref, sc_ref, w_ref, u_ref, q_ref, kvc_ref, kvs_ref, kvw_ref, g_ref, *att_refs):
    h = _layer_norm(x_ref[0]) * (1.0 + sc_ref[0]) + sh_ref[0]
    z = jnp.dot(h.astype(BF16), w_ref[...], preferred_element_type=F32)
    c0 = D_SSM
    c1 = c0 + D_ATT
    c2 = c1 + D_KV
    c3 = c2 + D_KV
    c4 = c3 + D_KV
    u_ref[0] = z[:, :c0]
    kvc_ref[0] = z[:, c1:c2]
    kvs_ref[0] = z[:, c2:c3]
    kvw_ref[0] = z[:, c3:c4]
    g_ref[0] = z[:, c4:c4 + LANE]
    if not att_refs:
        q_ref[0] = z[:, c0:c1].astype(BF16)
        return
    ks_ref, vst_ref, kw_ref, vwt_ref = att_refs
    hd, half = HEAD_DIM, N_KV_HEADS * HEAD_DIM
    for hq in range(N_HEADS):
        q_ref[0, hq // GQA, hq % GQA] = (z[:, c0 + hq * hd:c0 + (hq + 1) * hd] * (hd ** -0.5)).astype(BF16)
    for k_ref, vt_ref, base in ((ks_ref, vst_ref, c2), (kw_ref, vwt_ref, c3)):
        for hk in range(N_KV_HEADS):
            k_ref[0, hk] = z[:, base + hk * hd:base + (hk + 1) * hd].astype(BF16)
        vt = z[:, base + half:base + 2 * half].T
        vt_ref[0] = vt.reshape(N_KV_HEADS, hd, vt.shape[1]).astype(BF16)


def _mixer_in(x, shift, scale, w_pad, tm, attention_layouts):
    B, T, D = x.shape
    R = shift.shape[1]
    rb = 1 if R == 1 else tm
    mod_map = (lambda b, i: (b, 0, 0)) if R == 1 else (lambda b, i: (b, i, 0))
    row = lambda n: pl.BlockSpec((1, tm, n), lambda b, i: (b, i, 0))
    f32 = lambda n: jax.ShapeDtypeStruct((B, T, n), F32)
    if attention_layouts:
        q_shape = jax.ShapeDtypeStruct((B, N_KV_HEADS, GQA, T, HEAD_DIM), BF16)
        q_spec = pl.BlockSpec((1, N_KV_HEADS, GQA, tm, HEAD_DIM), lambda b, i: (b, 0, 0, i, 0))
        k_shape = jax.ShapeDtypeStruct((B, N_KV_HEADS, T, HEAD_DIM), BF16)
        k_spec = pl.BlockSpec((1, N_KV_HEADS, tm, HEAD_DIM), lambda b, i: (b, 0, i, 0))
        vt_shape = jax.ShapeDtypeStruct((B, N_KV_HEADS, HEAD_DIM, T), BF16)
        vt_spec = pl.BlockSpec((1, N_KV_HEADS, HEAD_DIM, tm), lambda b, i: (b, 0, 0, i))
        extra_shapes, extra_specs = (k_shape, vt_shape, k_shape, vt_shape), (k_spec, vt_spec, k_spec, vt_spec)
    else:
        q_shape, q_spec = jax.ShapeDtypeStruct((B, T, D_ATT), BF16), row(D_ATT)
        extra_shapes, extra_specs = (), ()
    return pl.pallas_call(
        _mixer_in_kernel,
        out_shape=(f32(D_SSM), q_shape, f32(D_KV), f32(D_KV), f32(D_KV), f32(LANE)) + extra_shapes,
        grid=(B, T // tm),
        in_specs=[row(D), pl.BlockSpec((1, rb, D), mod_map), pl.BlockSpec((1, rb, D), mod_map),
                  pl.BlockSpec((D, D_IN_PAD), lambda b, i: (0, 0))],
        out_specs=(row(D_SSM), q_spec, row(D_KV), row(D_KV), row(D_KV), row(LANE)) + extra_specs,
        compiler_params=_cparams("parallel", "parallel"),
        name="mixer_in",
    )(x, shift, scale, w_pad)


def _ssm_tables(lam_re, lam_im, log_dt, b_re, b_im, c_re, c_im, L, n_levels):
    G, P = lam_re.shape
    C = b_re.shape[-1]
    dt = jnp.exp(log_dt.astype(F32))[:, None]
    er, ei = lam_re * dt, lam_im * dt

    def power(k):
        kk = k.astype(F32)[:, None, None]
        mag = jnp.exp(kk * er)
        return mag * jnp.cos(kk * ei), mag * jnp.sin(kk * ei)

    lb_re, lb_im = power(jnp.ones((1,), F32))
    nr, ni = lb_re[0] - 1.0, lb_im[0]
    den = lam_re * lam_re + lam_im * lam_im
    fr = (nr * lam_re + ni * lam_im) / den
    fi = (ni * lam_re - nr * lam_im) / den
    bbr = fr[:, :, None] * b_re - fi[:, :, None] * b_im
    bbi = fr[:, :, None] * b_im + fi[:, :, None] * b_re
    pr, pi = power(jnp.arange(L + 1))
    clr = c_re[None] * pr[:, :, None, :] - c_im[None] * pi[:, :, None, :]
    cli = c_re[None] * pi[:, :, None, :] + c_im[None] * pr[:, :, None, :]
    kern = (jnp.einsum('kgcp,gpd->kgcd', clr[:L], bbr, precision=HIGHEST)
            - jnp.einsum('kgcp,gpd->kgcd', cli[:L], bbi, precision=HIGHEST))
    GP = LANE // C
    X = G // GP
    eye = jnp.eye(GP, dtype=BF16)
    place_einsum = functools.partial(jnp.einsum, preferred_element_type=BF16)
    kblk = place_einsum('kxhcd,hj->xkhdjc', kern.astype(BF16).reshape(L, X, GP, C, C), eye)
    kblk = kblk.reshape(X, L, LANE, LANE)
    prr, pir = pr[:L][::-1], pi[:L][::-1]
    ws2 = jnp.stack([prr[..., None] * bbr[None] - pir[..., None] * bbi[None],
                     prr[..., None] * bbi[None] + pir[..., None] * bbr[None]])
    ws = place_einsum('rsxhpd,hj->xshdrjp', ws2.astype(BF16).reshape(2, L, X, GP, P, C), eye)
    ws = ws.reshape(X, L * LANE, 2 * GP * P)
    wy2 = jnp.stack([clr[1:], -cli[1:]])
    wy = place_einsum('rtxhcp,hj->xrhptjc', wy2.astype(BF16).reshape(2, L, X, GP, C, P), eye)
    wy = wy.reshape(X, 2 * GP * P, L * LANE)
    lr, li = power(L * (2 ** jnp.arange(n_levels)))
    lr, li = lr.reshape(n_levels, X, GP * P), li.reshape(n_levels, X, GP * P)
    ar = jnp.transpose(jnp.concatenate([lr, lr], -1), (1, 0, 2))
    ai = jnp.transpose(jnp.concatenate([-li, li], -1), (1, 0, 2))
    return kblk, ws, wy, ar, ai, (lb_re[0], lb_im[0], bbr, bbi)


def _ssm_kernel(u_ref, kblk_ref, ws_ref, wy_ref, ar_ref, ai_ref, y_ref, hl_ref, toep_ref, *, L, nc, n_levels):
    for s in range(L):
        for t in range(L):
            blk = kblk_ref[0, t - s] if t >= s else jnp.zeros((LANE, LANE), BF16)
            toep_ref[s * LANE:(s + 1) * LANE, t * LANE:(t + 1) * LANE] = blk
    u = jnp.concatenate([u_ref[0, pl.ds(t, nc, stride=L), :] for t in range(L)], axis=1).astype(BF16)
    y1 = jnp.dot(u, toep_ref[...], preferred_element_type=F32)
    h = jnp.dot(u, ws_ref[0], preferred_element_type=F32)
    w2 = h.shape[-1]
    rows = lax.broadcasted_iota(jnp.int32, (nc, w2), 0)
    for k in range(n_levels):
        d = 1 << k
        sh = jnp.where(rows >= d, pltpu.roll(h, d, axis=0), 0.0)
        sw = pltpu.roll(sh, w2 // 2, axis=1)
        h = h + ar_ref[0, k:k + 1, :] * sh + ai_ref[0, k:k + 1, :] * sw
    hl_ref[0, 0] = h[nc - 1:nc, :]
    hp = jnp.where(rows >= 1, pltpu.roll(h, 1, axis=0), 0.0)
    y = y1 + jnp.dot(hp.astype(BF16), wy_ref[0], preferred_element_type=F32)
    for t in range(L):
        y_ref[0, pl.ds(t, nc, stride=L), :] = y[:, t * LANE:(t + 1) * LANE]


def _ssm_prompt(u, tables):
    kblk, ws, wy, ar, ai, _ = tables
    B, T, _ = u.shape
    L, P = SSM_CHUNK, SSM_STATE
    X, n_levels, w2 = ar.shape
    GP = w2 // (2 * P)
    nc = T // L
    tab = lambda a: pl.BlockSpec((1,) + a.shape[1:], lambda x, b: (x,) + (0,) * (a.ndim - 1))
    seq = pl.BlockSpec((1, T, LANE), lambda x, b: (b, 0, x))
    y, hl = pl.pallas_call(
        functools.partial(_ssm_kernel, L=L, nc=nc, n_levels=n_levels),
        out_shape=(jax.ShapeDtypeStruct((B, T, D_SSM), F32), jax.ShapeDtypeStruct((X, B, 1, w2), F32)),
        grid=(X, B),
        in_specs=[seq, tab(kblk), tab(ws), tab(wy), tab(ar), tab(ai)],
        out_specs=(seq, pl.BlockSpec((1, 1, 1, w2), lambda x, b: (x, b, 0, 0))),
        scratch_shapes=[pltpu.VMEM((L * LANE, L * LANE), BF16)],
        compiler_params=_cparams("parallel", "parallel"),
        name="ssm_prompt",
    )(u, kblk, ws, wy, ar, ai)
    hl = jnp.transpose(hl.reshape(X, B, 2, GP, P), (1, 0, 3, 2, 4))
    return y, hl.reshape(B, X * GP, 2 * P)


def _ssm_step_kernel(u_ref, h0_ref, bb_ref, lr_ref, li_ref, cy_ref, y_ref, h_ref):
    p = lr_ref.shape[-1] // 2
    bu = jnp.einsum('gbc,gcp->gbp', u_ref[...], bb_ref[...], preferred_element_type=F32)
    h0 = h0_ref[...]
    h0s = jnp.concatenate([h0[..., p:], h0[..., :p]], axis=-1)
    h = lr_ref[...] * h0 + li_ref[...] * h0s + bu
    h_ref[...] = h
    y_ref[...] = jnp.einsum('gbp,gpc->gbc', h.astype(BF16), cy_ref[...], preferred_element_type=F32)


def _ssm_sample(u, h0_re, h0_im, tables, c_re, c_im):
    lb_re, lb_im, bbr, bbi = tables[-1]
    B = u.shape[0]
    G, C, P = N_SSM_GROUPS, SSM_GROUP, SSM_STATE
    ug = jnp.transpose(u.reshape(B, G, C), (1, 0, 2)).astype(BF16)
    h0 = jnp.transpose(jnp.concatenate([h0_re, h0_im], -1), (1, 0, 2)).astype(F32)
    bb = jnp.concatenate([jnp.transpose(bbr, (0, 2, 1)), jnp.transpose(bbi, (0, 2, 1))], -1).astype(BF16)
    lr = jnp.concatenate([lb_re, lb_re], -1)[:, None, :]
    li = jnp.concatenate([-lb_im, lb_im], -1)[:, None, :]
    cy = jnp.concatenate([jnp.transpose(c_re, (0, 2, 1)), -jnp.transpose(c_im, (0, 2, 1))], 1).astype(BF16)
    y, h = pl.pallas_call(
        _ssm_step_kernel,
        out_shape=(jax.ShapeDtypeStruct((G, B, C), F32), jax.ShapeDtypeStruct((G, B, 2 * P), F32)),
        name="ssm_step",
    )(ug, h0, bb, lr, li, cy)
    return jnp.transpose(y, (1, 0, 2)).reshape(B, D_SSM), jnp.transpose(h, (1, 0, 2))


def _compress_tables(phi_pe, phi_w1, phi_b1, phi_w2, phi_b2):
    S, H, Dh = CMP_STRIDE, N_KV_HEADS, HEAD_DIM
    w1 = phi_w1.reshape(2, 2, S, Dh, Dh)
    eye_c = jnp.eye(2, dtype=F32)
    eye_h = jnp.eye(H, dtype=F32)
    wbig = jnp.einsum('cajde,xc,yh->jxydache', w1, eye_c, eye_h).reshape(S * 2 * H * Dh, 2 * 2 * H * Dh)
    pe = jnp.transpose(phi_pe.reshape(2, 2, S, Dh), (1, 2, 0, 3))
    pe_rows = jnp.broadcast_to(pe[:, :, :, None, :], (2, S, 2, H, Dh)).reshape(2, S * 2 * H * Dh)
    n = 2 * H * Dh
    pe_w = (jnp.dot(pe_rows[0], wbig[:, :n], precision=HIGHEST) + jnp.dot(pe_rows[1], wbig[:, n:], precision=HIGHEST))
    b1 = jnp.broadcast_to(phi_b1[:, None, :], (2, H, Dh)).reshape(1, n) + pe_w[None, :]
    w2 = jnp.einsum('cef,cx,hy->chexyf', phi_w2, eye_c, eye_h).reshape(n, n)
    b2 = jnp.broadcast_to(phi_b2[:, None, :], (2, H, Dh)).reshape(1, n)
    return wbig.astype(BF16), b1, w2.astype(BF16), b2


def _compress_in_kernel(x_ref, w_ref, z_ref):
    z_ref[0] = jnp.dot(x_ref[0].astype(BF16), w_ref[...], preferred_element_type=F32)


def _compress_in(x2, tables):
    wbig = tables[0]
    N2 = wbig.shape[1]
    B, n, K = x2.shape
    tr = math.gcd(n, 256)
    return pl.pallas_call(
        _compress_in_kernel,
        out_shape=jax.ShapeDtypeStruct((B, n, N2), F32),
        grid=(B, n // tr),
        in_specs=[pl.BlockSpec((1, tr, K), lambda b, i: (b, i, 0)),
                  pl.BlockSpec((K, N2), lambda b, i: (0, 0))],
        out_specs=pl.BlockSpec((1, tr, N2), lambda b, i: (b, i, 0)),
        compiler_params=_cparams("parallel", "parallel"),
        name="compress_in",
    )(x2, wbig)


def _compress_in_paged_kernel(pt_ref, *refs, n_pg):
    x_refs = refs[:n_pg]
    w_ref, z_ref = refs[n_pg:n_pg + 2]
    scratch = refs[n_pg + 2:]
    n_slab = D_KV // LANE
    pg_part = n_pg // PAGE_PARTS
    cpp = PAGE_SIZE // CMP_STRIDE
    rows = pg_part * cpp
    for part in range(PAGE_PARTS):
        s_refs = scratch[part * n_slab:(part + 1) * n_slab]
        for k in range(pg_part):
            t = x_refs[part * pg_part + k][0].reshape(D_KV, PAGE_SIZE).T
            for c, s_ref in enumerate(s_refs):
                for n in range(cpp):
                    r0 = (k * cpp + n) * CHUNK_PITCH
                    s_ref[r0:r0 + CMP_STRIDE, :] = t[n * CMP_STRIDE:(n + 1) * CMP_STRIDE, c * LANE:(c + 1) * LANE]
        z = jnp.zeros((rows, w_ref.shape[1]), F32)
        for j in range(CMP_STRIDE):
            xj = jnp.concatenate([s_ref[pl.ds(j, rows, stride=CHUNK_PITCH), :] for s_ref in s_refs], axis=1)
            z = z + jnp.dot(xj.astype(BF16), w_ref[j * D_KV:(j + 1) * D_KV, :], preferred_element_type=F32)
        z_ref[0, part * rows:(part + 1) * rows, :] = z


def _compress_in_paged(pool_t, page_table, tables):
    wbig = tables[0]
    N2 = wbig.shape[1]
    K = wbig.shape[0]
    B, n_pages = page_table.shape
    n_pg = math.gcd(n_pages, PAGES_PER_STEP)
    rows = n_pg * PAGE_SIZE // CMP_STRIDE
    page_spec = lambda k: pl.BlockSpec((1,) + pool_t.shape[1:],
                                       lambda b, i, pt, k=k: (pt[b, i * n_pg + k], 0, 0, 0, 0))
    grid_spec = pltpu.PrefetchScalarGridSpec(
        num_scalar_prefetch=1,
        grid=(B, n_pages // n_pg),
        in_specs=[page_spec(k) for k in range(n_pg)] + [pl.BlockSpec((K, N2), lambda b, i, pt: (0, 0))],
        out_specs=pl.BlockSpec((1, rows, N2), lambda b, i, pt: (b, i, 0)),
        scratch_shapes=[pltpu.VMEM((rows // PAGE_PARTS * CHUNK_PITCH, LANE), F32)
                        for _ in range(PAGE_PARTS * (D_KV // LANE))],
    )
    return pl.pallas_call(
        functools.partial(_compress_in_paged_kernel, n_pg=n_pg),
        out_shape=jax.ShapeDtypeStruct((B, n_pages * PAGE_SIZE // CMP_STRIDE, N2), F32),
        grid_spec=grid_spec,
        compiler_params=_cparams("arbitrary", "arbitrary"),
        name="compress_in_paged",
    )(page_table, *([pool_t] * n_pg), wbig)


def _compress_out_kernel(*refs):
    z_refs, (b1_ref, w2_ref, b2_ref, o_ref) = refs[:-4], refs[-4:]
    z = jnp.concatenate([z_ref[0] for z_ref in z_refs], axis=0)
    n = z.shape[-1] // 2
    rows = z.shape[0]
    second = pltpu.roll(z[:, n:], rows - 1, axis=0)
    hdn = jax.nn.gelu(z[:, :n] + second + b1_ref[...])
    o_ref[0, :rows, :] = jnp.dot(hdn.astype(BF16), w2_ref[...], preferred_element_type=F32) + b2_ref[...]
    if o_ref.shape[1] > rows:
        o_ref[0, rows:, :] = jnp.zeros((o_ref.shape[1] - rows, n), F32)


def _compress_out(zs, tables, n_out):
    _, b1, w2, b2 = tables
    B, _, N2 = zs[0].shape
    return pl.pallas_call(
        _compress_out_kernel,
        out_shape=jax.ShapeDtypeStruct((B, n_out, N2 // 2), F32),
        grid=(B,),
        in_specs=[pl.BlockSpec((1, z.shape[1], N2), lambda b: (b, 0, 0)) for z in zs] + [
                  pl.BlockSpec((1, N2 // 2), lambda b: (0, 0)),
                  pl.BlockSpec((N2 // 2, N2 // 2), lambda b: (0, 0)),
                  pl.BlockSpec((1, N2 // 2), lambda b: (0, 0))],
        out_specs=pl.BlockSpec((1, n_out, N2 // 2), lambda b: (b, 0, 0)),
        compiler_params=_cparams("parallel"),
        name="compress_out",
    )(*zs, b1, w2, b2)


def _rel_bucket(dist):
    n = jnp.maximum(dist, 0)
    max_exact = NUM_BUCKETS // 2
    nf = jnp.maximum(n, 1).astype(F32)
    large = max_exact + (jnp.log(nf / max_exact) / math.log(REL_MAX_DIST / max_exact)
                         * (NUM_BUCKETS - max_exact)).astype(jnp.int32)
    large = jnp.minimum(large, NUM_BUCKETS - 1)
    return jnp.where(n < max_exact, n, large)


def _bias_by_distance(rel_bias, n_max):
    onehot = (_rel_bucket(jnp.arange(n_max))[None, :] == jnp.arange(NUM_BUCKETS)[:, None]).astype(F32)
    return jnp.dot(jnp.transpose(rel_bias.astype(F32)), onehot, precision=HIGHEST)


def _shifted_chunks(bias_n, pad, n_chunks, width):
    n = min(bias_n.shape[1], n_chunks * width - pad)
    ext = jnp.concatenate([jnp.broadcast_to(bias_n[:, :1], (N_HEADS, pad)), bias_n[:, :n],
                           jnp.zeros((N_HEADS, n_chunks * width - pad - n), F32)], axis=1)
    return ext.reshape(N_HEADS, n_chunks, width)


def _bias_tables_kernel(ed_ref, ec_ref, tzs_ref, tzw_ref, cmp_ref, *, tq, tk, n_qt):
    n_ds, n_dw, n_j = tzs_ref.shape[1] - 1, tzw_ref.shape[1] - 1, cmp_ref.shape[1] // 8
    tzs_ref[0, n_ds] = jnp.full((tk, tq), NEG, F32)
    tzw_ref[0, n_dw] = jnp.full((tk, tq), NEG, F32)
    w = tq + tk
    c = lax.broadcasted_iota(jnp.int32, (tk, tq), 0)
    r = lax.broadcasted_iota(jnp.int32, (tk, tq), 1)
    for d in range(n_ds):
        v = jnp.concatenate([ed_ref[0, d:d + 1, :], ed_ref[0, d + 1:d + 2, :]], axis=1)
        t = pltpu.roll(jnp.broadcast_to(v, (tk, w)), w - (tk - 1), axis=1, stride=1, stride_axis=0)[:, :tq]
        dist = d * tk + r - c
        tzs_ref[0, d] = jnp.where(dist >= 0, t, NEG)
        if d < n_dw:
            tzw_ref[0, d] = jnp.where((dist >= 0) & (dist <= WINDOW), t, NEG)
    for j in range(n_j):
        dd = n_qt - 1 - j
        c0, c1 = max(dd, 0), max(dd + 1, 0)
        v = jnp.concatenate([ec_ref[0, c0:c0 + 1, :], ec_ref[0, c1:c1 + 1, :]], axis=1)
        t = pltpu.roll(jnp.broadcast_to(v, (8, w)), w - 7 * CMP_STRIDE, axis=1, stride=CMP_STRIDE, stride_axis=0)
        dist = tq * dd + r[:8] - CMP_STRIDE * c[:8] - (CMP_BLOCK - 1)
        cmp_ref[0, j * 8:(j + 1) * 8, :] = jnp.where(dist >= 0, t[:, :tq], NEG)


def _bias_tables(bias_n, n_qt, n_rb, n_ds, n_dw, tq, tk):
    assert tq == tk == 8 * CMP_STRIDE and n_dw <= n_ds
    n_j = n_rb + n_qt - 1
    ed = _shifted_chunks(bias_n, tk - 1, n_ds + 1, tq)
    ec = _shifted_chunks(bias_n, 7 * CMP_STRIDE + CMP_BLOCK - 1, n_qt + 1, tq)
    head = lambda a: pl.BlockSpec((1,) + a.shape[1:], lambda h: (h,) + (0,) * (a.ndim - 1))
    outs = (jax.ShapeDtypeStruct((N_HEADS, n_ds + 1, tk, tq), F32),
            jax.ShapeDtypeStruct((N_HEADS, n_dw + 1, tk, tq), F32),
            jax.ShapeDtypeStruct((N_HEADS, n_j * 8, tq), F32))
    tzs, tzw, cmp = pl.pallas_call(
        functools.partial(_bias_tables_kernel, tq=tq, tk=tk, n_qt=n_qt),
        out_shape=outs,
        grid=(N_HEADS,),
        in_specs=[head(ed), head(ec)],
        out_specs=tuple(head(o) for o in outs),
        compiler_params=_cparams("parallel"),
        name="bias_tables",
    )(ed, ec)
    grp = lambda a: a.reshape((N_KV_HEADS, GQA) + a.shape[1:])
    return grp(tzs), grp(tzw), cmp


def _pool_matrix(n_cmp_pad, n_blk_pad):
    r = SEL_BLOCK // CMP_STRIDE
    i = np.arange(n_cmp_pad)[None, :]
    j = np.arange(n_blk_pad)[:, None]
    return ((i >= r * j - 1) & (i <= r * j + r - 1)).astype(np.float32)


def _cmp_select_kernel(q_ref, k_ref, vt_ref, bias_ref, pool_ref, o_ref, sel_ref, *, tq):
    qt = pl.program_id(2)
    n_qt = pl.num_programs(2)
    q = q_ref[0, 0].reshape(GQA * tq, HEAD_DIM)
    k = k_ref[0, 0]
    nc = k.shape[0]
    s = _nt_dot(k, q)
    row0 = pl.multiple_of((n_qt - 1 - qt) * 8, 8)
    s = s + jnp.concatenate([bias_ref[g, pl.ds(row0, nc), :] for g in range(GQA)], axis=-1)
    m = jnp.maximum(jnp.max(s, axis=0, keepdims=True), 0.5 * NEG)
    p = jnp.exp(s - m)
    p = p * (1.0 / jnp.maximum(jnp.sum(p, axis=0, keepdims=True), 1e-30))
    ot = jnp.dot(vt_ref[0, 0], p.astype(BF16), preferred_element_type=F32)
    o_ref[0] = jnp.concatenate([ot[:, g * tq:(g + 1) * tq].T for g in range(GQA)], axis=-1)
    imp = p[:, 0:tq]
    for g in range(1, GQA):
        imp = imp + p[:, g * tq:(g + 1) * tq]
    sb = jnp.dot(pool_ref[...], imp, precision=HIGHEST, preferred_element_type=F32)
    nb = sb.shape[0]
    blk = lax.broadcasted_iota(jnp.int32, (nb, tq), 0)
    cur = (qt * tq + lax.broadcasted_iota(jnp.int32, (nb, tq), 1)) // SEL_BLOCK
    causal = blk <= cur
    forced = (blk == 0) | (blk == cur) | (blk == cur - 1)
    sc = jnp.where(forced & causal, 1e4, jnp.where(causal, sb, -1.0))
    groups = [sc[r:r + 8] for r in range(0, nb, 8)]
    sub = lax.broadcasted_iota(jnp.int32, (8, tq), 0)
    ranks = [jnp.zeros((8, tq), F32) for _ in groups]
    for i in range(nb):
        row = sc[i:i + 1, :]
        for gi, grp in enumerate(groups):
            if gi * 8 > i:
                ahead = row >= grp
            elif gi * 8 + 7 < i:
                ahead = row > grp
            else:
                ahead = (row > grp) | ((row == grp) & (sub > i - gi * 8))
            ranks[gi] = ranks[gi] + jnp.where(ahead, 1.0, 0.0)
    rank = jnp.concatenate(ranks, axis=0)
    sel_ref[0, 0] = jnp.where((rank < N_SEL) & causal, 0.0, NEG)


def _cmp_select_prompt(q5, kc, vct, bias_tab, pool):
    B, _, _, T, _ = q5.shape
    NC = kc.shape[2]
    NB = pool.shape[0]
    R = bias_tab.shape[1]
    tq = ATT_TQ
    return pl.pallas_call(
        functools.partial(_cmp_select_kernel, tq=tq),
        out_shape=(jax.ShapeDtypeStruct((B, T, D_ATT), F32),
                   jax.ShapeDtypeStruct((B, N_KV_HEADS, NB, T), F32)),
        grid=(B, N_KV_HEADS, T // tq),
        in_specs=[pl.BlockSpec((1, 1, GQA, tq, HEAD_DIM), lambda b, h, i: (b, h, 0, i, 0)),
                  pl.BlockSpec((1, 1, NC, HEAD_DIM), lambda b, h, i: (b, h, 0, 0)),
                  pl.BlockSpec((1, 1, HEAD_DIM, NC), lambda b, h, i: (b, h, 0, 0)),
                  pl.BlockSpec((GQA, R, tq), lambda b, h, i: (h, 0, 0)),
                  pl.BlockSpec((NB, NC), lambda b, h, i: (0, 0))],
        out_specs=(pl.BlockSpec((1, tq, GQA * HEAD_DIM), lambda b, h, i: (b, i, h)),
                   pl.BlockSpec((1, 1, NB, tq), lambda b, h, i: (b, h, 0, i))),
        compiler_params=_cparams("parallel", "parallel", "parallel"),
        name="cmp_select_prompt",
    )(q5, kc, vct, bias_tab, pool)


def _sel_win_kernel(q_ref, ks_ref, vst_ref, kw_ref, vwt_ref, sel_ref, tzs_ref, tzw_ref, os_ref, ow_ref, *, tq):
    tk = ATT_TK
    qt = pl.program_id(3)
    n_hd = q_ref.shape[2]
    q = q_ref[0, 0].reshape(n_hd * tq, HEAD_DIM)
    width = n_hd * tq
    per_tile = tk // SEL_BLOCK

    def make_sweep(k_ref, vt_ref, tz_ref, use_sel, n_chains, single_trip):
        n_d = tz_ref.shape[2] - 1

        def scores(kt, hi):
            pad = kt > hi
            kt = jnp.minimum(kt, hi)
            off = pl.multiple_of(kt * tk, tk)
            k = k_ref[0, 0, pl.ds(off, tk), :]
            d = jnp.where(pad, n_d, jnp.minimum(qt - kt, n_d - 1))
            bias = [tz_ref[0, g, d] for g in range(n_hd)]
            if use_sel:
                rows = sel_ref[0, 0, pl.ds(kt * per_tile, per_tile), :]
                selb = jnp.concatenate([jnp.broadcast_to(rows[i:i + 1], (SEL_BLOCK, tq))
                                        for i in range(per_tile)], axis=0)
                bias = [b + selb for b in bias]
            return _nt_dot(k, q) + jnp.concatenate(bias, axis=1)

        def values_t(kt, lo, hi):
            off = pl.multiple_of(jnp.clip(kt, lo, hi) * tk, tk)
            return vt_ref[0, 0, :, pl.ds(off, tk)]

        def sweep(lo, hi):
            n_trips = (hi - lo + n_chains) // n_chains

            def first_trip():
                out = []
                for c in range(n_chains):
                    s = scores(lo + c, hi)
                    m = jnp.maximum(jnp.max(s, axis=0, keepdims=True), 0.5 * NEG)
                    p = jnp.exp(s - m)
                    out.append((m, jnp.sum(p, axis=0, keepdims=True), jnp.zeros((HEAD_DIM, width), F32),
                                jnp.ones((1, width), F32), p.astype(BF16)))
                return tuple(out)

            def trip(i, chains):
                kt = lo + n_chains * i
                pv = [jnp.dot(values_t(kt - n_chains + c, lo, hi), chains[c][4], preferred_element_type=F32)
                      for c in range(n_chains)]
                ss = [scores(kt + c, hi) for c in range(n_chains)]
                out = []
                for c in range(n_chains):
                    m, l, acc, alpha_prev, _ = chains[c]
                    m_new = jnp.maximum(m, jnp.max(ss[c], axis=0, keepdims=True))
                    alpha = jnp.exp(m - m_new)
                    p = jnp.exp(ss[c] - m_new)
                    l = alpha * l + jnp.sum(p, axis=0, keepdims=True)
                    out.append((m_new, l, alpha_prev * acc + pv[c], alpha, p.astype(BF16)))
                return tuple(out)

            if single_trip:
                done = []
                for c in range(n_chains):
                    s = scores(lo + c, hi)
                    m = jnp.maximum(jnp.max(s, axis=0, keepdims=True), 0.5 * NEG)
                    p = jnp.exp(s - m)
                    done.append((m, jnp.sum(p, axis=0, keepdims=True),
                                 jnp.dot(values_t(lo + c, lo, hi), p.astype(BF16), preferred_element_type=F32)))
            else:
                chains = lax.fori_loop(1, n_trips, trip, first_trip())
                kt_last = lo + n_chains * (n_trips - 1)
                done = []
                for c in range(n_chains):
                    m, l, acc, alpha, p = chains[c]
                    done.append((m, l, alpha * acc + jnp.dot(values_t(kt_last + c, lo, hi), p,
                                                              preferred_element_type=F32)))
            m_all = functools.reduce(jnp.maximum, [m for m, _, _ in done])
            num = den = 0.0
            for m, l, acc in done:
                e = jnp.exp(m - m_all)
                num = num + acc * e
                den = den + l * e
            o = num / jnp.maximum(den, 1e-30)
            return jnp.concatenate([o[:, g * tq:(g + 1) * tq].T for g in range(n_hd)], axis=-1)
        return sweep

    n_win = tzw_ref.shape[2] - 1
    os_ref[0] = make_sweep(ks_ref, vst_ref, tzs_ref, True, SEL_CHAINS, False)(0, qt)
    ow_ref[0] = make_sweep(kw_ref, vwt_ref, tzw_ref, False, n_win, True)(jnp.maximum(qt - (n_win - 1), 0), qt)


def _sel_win_prompt(q5, ks, vst, kw, vwt, sel, tzs, tzw):
    B, _, _, T, _ = q5.shape
    NB = sel.shape[2]
    tq = ATT_TQ
    n_hd, n_hp = SEL_HEADS, GQA // SEL_HEADS
    k_spec = pl.BlockSpec((1, 1, T, HEAD_DIM), lambda b, h, p, i: (b, h, 0, 0))
    vt_spec = pl.BlockSpec((1, 1, HEAD_DIM, T), lambda b, h, p, i: (b, h, 0, 0))
    tz_spec = lambda tz: pl.BlockSpec((1, n_hd) + tz.shape[2:], lambda b, h, p, i: (h, p, 0, 0, 0))
    o_spec = pl.BlockSpec((1, tq, n_hd * HEAD_DIM), lambda b, h, p, i: (b, i, h * n_hp + p))
    return pl.pallas_call(
        functools.partial(_sel_win_kernel, tq=tq),
        out_shape=(jax.ShapeDtypeStruct((B, T, D_ATT), F32), jax.ShapeDtypeStruct((B, T, D_ATT), F32)),
        grid=(B, N_KV_HEADS, n_hp, T // tq),
        in_specs=[pl.BlockSpec((1, 1, n_hd, tq, HEAD_DIM), lambda b, h, p, i: (b, h, p, i, 0)),
                  k_spec, vt_spec, k_spec, vt_spec,
                  pl.BlockSpec((1, 1, NB, tq), lambda b, h, p, i: (b, h, 0, i)),
                  tz_spec(tzs), tz_spec(tzw)],
        out_specs=(o_spec, o_spec),
        compiler_params=_cparams("parallel", "parallel", "parallel", "parallel"),
        name="sel_win_prompt",
    )(q5, ks, vst, kw, vwt, sel, tzs, tzw)


def _gate_expand_matrix():
    m = np.zeros((3, 2 * LANE, D_ATT), np.float32)
    for r in range(3):
        for h in range(N_HEADS):
            m[r, h * 3 + r, h * HEAD_DIM:(h + 1) * HEAD_DIM] = 1.0
            m[r, LANE + h * 3 + r, h * HEAD_DIM:(h + 1) * HEAD_DIM] = 1.0
    return m


def _split_bf16(x):
    hi = x.astype(BF16)
    return hi, (x - hi.astype(F32)).astype(BF16)


def _post_mixer_kernel(y_ref, u_ref, oc_ref, os_ref, ow_ref, g_ref, x_ref, gate_ref, sh_ref, sc_ref,
                       dskip_ref, wglu_ref, bglu_ref, gexp_ref, wout_ref, lng_ref, lnb_ref,
                       wr_ref, br_ref, x1_ref, hm_ref, te_ref, tw_ref):
    y = y_ref[0] + dskip_ref[...] * u_ref[0]
    gl = jax.nn.gelu(y)
    ssm = gl * jax.nn.sigmoid(jnp.dot(gl.astype(BF16), wglu_ref[...], preferred_element_type=F32)
                              + bglu_ref[...])
    sg = jnp.concatenate(_split_bf16(jax.nn.sigmoid(g_ref[0])), axis=1)
    att = jnp.zeros_like(oc_ref[0])
    for r, o_ref in enumerate((oc_ref, os_ref, ow_ref)):
        att = att + jnp.dot(sg, gexp_ref[r], preferred_element_type=F32) * o_ref[0]
    h = (jnp.dot(ssm.astype(BF16), wout_ref[:D_SSM, :], preferred_element_type=F32)
         + jnp.dot(att.astype(BF16), wout_ref[D_SSM:, :], preferred_element_type=F32))
    z = DN_ALPHA * x_ref[0] + gate_ref[0] * h
    x1 = _layer_norm(z) * lng_ref[...] + lnb_ref[...]
    x1_ref[0] = x1
    hm = _layer_norm(x1) * (1.0 + sc_ref[0]) + sh_ref[0]
    hm_ref[0] = hm
    hm_hi, hm_lo = _split_bf16(hm)
    logits = (jnp.dot(hm_hi, wr_ref[0], preferred_element_type=F32)
              + jnp.dot(hm_lo, wr_ref[0], preferred_element_type=F32)
              + jnp.dot(hm_hi, wr_ref[1], preferred_element_type=F32)) + br_ref[...]
    lane = lax.broadcasted_iota(jnp.int32, logits.shape, 1)
    work = jnp.where(lane < N_EXPERTS, logits, -jnp.inf)
    te = jnp.zeros(logits.shape, jnp.int32)
    tv = jnp.zeros(logits.shape, F32)
    for k in range(TOP_K):
        best = jnp.max(work, axis=-1, keepdims=True)
        arg = jnp.min(jnp.where(work == best, lane, LANE), axis=-1, keepdims=True)
        te = jnp.where(lane == k, arg, te)
        tv = jnp.where(lane == k, best, tv)
        work = jnp.where(lane == arg, -jnp.inf, work)
    ex = jnp.where(lane < TOP_K, jnp.exp(tv - tv[:, 0:1]), 0.0)
    te_ref[0] = te
    tw_ref[0] = ex / jnp.sum(ex, axis=-1, keepdims=True)


def _post_mixer(y, u, oc, osel, ow, g, x, gate, shift, scale, w, tm):
    B, T, D = x.shape
    R = gate.shape[1]
    rb = 1 if R == 1 else tm
    mod_map = (lambda b, i: (b, 0, 0)) if R == 1 else (lambda b, i: (b, i, 0))
    row = lambda n: pl.BlockSpec((1, tm, n), lambda b, i: (b, i, 0))
    mod = pl.BlockSpec((1, rb, D), mod_map)
    full = lambda a: pl.BlockSpec(a.shape, lambda b, i: (0,) * a.ndim)
    consts = (w['d_skip'], w['w_glu'], w['b_glu'], w['gexp'], w['w_out'], w['ln1_g'], w['ln1_b'],
              w['w_router'], w['b_router'])
    return pl.pallas_call(
        _post_mixer_kernel,
        out_shape=(jax.ShapeDtypeStruct((B, T, D), F32), jax.ShapeDtypeStruct((B, T, D), F32),
                   jax.ShapeDtypeStruct((B, T, LANE), jnp.int32), jax.ShapeDtypeStruct((B, T, LANE), F32)),
        grid=(B, T // tm),
        in_specs=[row(D_SSM), row(D_SSM), row(D_ATT), row(D_ATT), row(D_ATT), row(LANE), row(D),
                  mod, mod, mod] + [full(a) for a in consts],
        out_specs=(row(D), row(D), row(LANE), row(LANE)),
        compiler_params=_cparams("parallel", "parallel"),
        name="post_mixer",
    )(y, u, oc, osel, ow, g, x, gate, shift, scale, *consts)


def _expert_kernel(e_ref, blk_ref, lo_ref, hi_ref, first_ref, x_ref, wgu_ref, bgu_ref, wd_ref, bd_ref, o_ref,
                   wgu_s, wd_s):
    i = pl.program_id(0)
    fresh = (i == 0) | (e_ref[i] != e_ref[jnp.maximum(i - 1, 0)])

    @pl.when(fresh)
    def _():
        wgu_s[...] = wgu_ref[0].astype(BF16)
        wd_s[...] = wd_ref[0].astype(BF16)

    @pl.when(first_ref[i] == 1)
    def _():
        o_ref[...] = jnp.zeros_like(o_ref)

    @pl.when(hi_ref[i] > lo_ref[i])
    def _():
        gu = jnp.dot(x_ref[...].astype(BF16), wgu_s[...], preferred_element_type=F32) + bgu_ref[0]
        gate = jnp.minimum(gu[:, :D_FF], SWIGLU_LIMIT)
        up = jnp.clip(gu[:, D_FF:], -SWIGLU_LIMIT, SWIGLU_LIMIT)
        hh = (up + 1.0) * gate * jax.nn.sigmoid(SWIGLU_ALPHA * gate)
        y = jnp.dot(hh.astype(BF16), wd_s[...], preferred_element_type=F32) + bd_ref[0]
        row = blk_ref[i] * MOE_ROWS + lax.broadcasted_iota(jnp.int32, (MOE_ROWS, 1), 0)
        o_ref[...] = jnp.where((row >= lo_ref[i]) & (row < hi_ref[i]), y, o_ref[...])


def _experts(xb, items, w_gate_up, b_gate_up, w_down, b_down):
    rows, D = xb.shape
    n_items = items[0].shape[0]
    wmap = lambda i, e, blk, lo, hi, first: (e[i], 0, 0)
    rmap = lambda i, e, blk, lo, hi, first: (blk[i], 0)
    grid_spec = pltpu.PrefetchScalarGridSpec(
        num_scalar_prefetch=5,
        grid=(n_items,),
        in_specs=[pl.BlockSpec((MOE_ROWS, D), rmap),
                  pl.BlockSpec((1, D, 2 * D_FF), wmap),
                  pl.BlockSpec((1, 1, 2 * D_FF), wmap),
                  pl.BlockSpec((1, D_FF, D), wmap),
                  pl.BlockSpec((1, 1, D), wmap)],
        out_specs=pl.BlockSpec((MOE_ROWS, D), rmap),
        scratch_shapes=[pltpu.VMEM((D, 2 * D_FF), BF16), pltpu.VMEM((D_FF, D), BF16)],
    )
    return pl.pallas_call(
        _expert_kernel,
        out_shape=jax.ShapeDtypeStruct((rows, D), F32),
        grid_spec=grid_spec,
        compiler_params=_cparams("arbitrary"),
        name="moe_experts",
    )(*items, xb, w_gate_up, b_gate_up.reshape(N_EXPERTS, 1, 2 * D_FF), w_down,
      b_down.reshape(N_EXPERTS, 1, D))


def _moe_dispatch(top_e, n):
    blk = MOE_ROWS
    nk = n * TOP_K
    cb = 128
    assert nk % cb == 0
    e = top_e.reshape(-1)
    oh = (jnp.arange(N_EXPERTS)[:, None] == e[None, :]).astype(BF16).reshape(N_EXPERTS, nk // cb, cb)
    before = jnp.asarray(np.triu(np.ones((cb, cb), np.float32), 1), dtype=BF16)
    within = jnp.einsum('ebj,ji->ebi', oh, before, preferred_element_type=F32)
    blk_tot = jnp.sum(oh.astype(F32), axis=2)
    blk_off = jnp.cumsum(blk_tot, axis=1) - blk_tot
    counts = jnp.sum(blk_tot, axis=1)
    start = jnp.cumsum(counts) - counts
    dest = jnp.sum((within + (blk_off + start[:, None])[:, :, None]) * oh.astype(F32), axis=0)
    dest = dest.reshape(nk).astype(jnp.int32)
    order = jnp.argsort(dest)
    n_blk = -(-nk // blk)
    row_tok = jnp.concatenate([(order // TOP_K).astype(jnp.int32), jnp.full((n_blk * blk - nk,), n, jnp.int32)])
    counts_i, start_i = counts.astype(jnp.int32), start.astype(jnp.int32)
    first_b = start_i // blk
    last_b = (start_i + counts_i - 1) // blk
    n_it = jnp.where(counts_i > 0, last_b - first_b + 1, 0)
    it_end = jnp.cumsum(n_it)
    it_start = it_end - n_it
    n_items = n_blk + N_EXPERTS - 1
    i = jnp.arange(n_items)
    live = i < it_end[-1]
    it_e = jnp.minimum(jnp.sum(it_end[None, :] <= i[:, None], axis=1), N_EXPERTS - 1)
    it_blk = jnp.where(live, first_b[it_e] + i - it_start[it_e], n_blk - 1)
    it_lo = jnp.where(live, start_i[it_e], 0)
    it_hi = jnp.where(live, start_i[it_e] + counts_i[it_e], 0)
    it_first = jnp.concatenate([jnp.ones((1,), jnp.int32), (it_blk[1:] != it_blk[:-1]).astype(jnp.int32)])
    items = tuple(a.astype(jnp.int32) for a in (it_e, it_blk, it_lo, it_hi, it_first))
    return row_tok, dest.reshape(n, TOP_K), items


def _final_kernel(x_ref, y0_ref, y1_ref, y2_ref, y3_ref, tw_ref, gate_ref, lng_ref, lnb_ref, o_ref):
    tw = tw_ref[0]
    y = jnp.zeros_like(x_ref[0])
    for k, y_ref in enumerate((y0_ref, y1_ref, y2_ref, y3_ref)):
        y = y + tw[:, k:k + 1] * y_ref[0]
    z = DN_ALPHA * x_ref[0] + gate_ref[0] * y
    o_ref[0] = _layer_norm(z) * lng_ref[...] + lnb_ref[...]


def _final(x1, ys, tw, gate, ln_g, ln_b, tm):
    B, T, D = x1.shape
    R = gate.shape[1]
    rb = 1 if R == 1 else tm
    mod_map = (lambda b, i: (b, 0, 0)) if R == 1 else (lambda b, i: (b, i, 0))
    row = lambda n: pl.BlockSpec((1, tm, n), lambda b, i: (b, i, 0))
    vec = pl.BlockSpec((1, D), lambda b, i: (0, 0))
    return pl.pallas_call(
        _final_kernel,
        out_shape=jax.ShapeDtypeStruct((B, T, D), F32),
        grid=(B, T // tm),
        in_specs=[row(D), row(D), row(D), row(D), row(D), row(LANE),
                  pl.BlockSpec((1, rb, D), mod_map), vec, vec],
        out_specs=row(D),
        compiler_params=_cparams("parallel", "parallel"),
        name="moe_combine_ln",
    )(x1, *ys, tw, gate, ln_g, ln_b)


def _cmp_select_step_kernel(q_ref, kv_ref, bias_ref, pool_ref, o_ref, idx_ref, *, n_cmp, n_blk, q_pos):
    q = q_ref[0].astype(BF16)
    ncp = kv_ref.shape[1]
    nbp = pool_ref.shape[1]
    hd = HEAD_DIM
    kv = kv_ref[0]
    kb = [kv[:, h * hd:(h + 1) * hd].astype(BF16) for h in range(N_KV_HEADS)]
    vb = [kv[:, (N_KV_HEADS + h) * hd:(N_KV_HEADS + h + 1) * hd].astype(BF16) for h in range(N_KV_HEADS)]
    row = lax.broadcasted_iota(jnp.int32, (N_HEADS, 1), 0)
    first = row < GQA
    s = jnp.where(first, _nt_dot(q, kb[0]), _nt_dot(q, kb[1])) * (hd ** -0.5)
    s = s + bias_ref[...]
    ci = lax.broadcasted_iota(jnp.int32, (N_HEADS, ncp), 1)
    mask = (ci * CMP_STRIDE + CMP_BLOCK - 1 <= q_pos) & (ci < n_cmp)
    s = jnp.where(mask, s, NEG)
    m = jnp.max(s, axis=-1, keepdims=True)
    p = jnp.where(mask, jnp.exp(s - m), 0.0)
    p = p / jnp.maximum(jnp.sum(p, axis=-1, keepdims=True), 1e-30)
    pb = p.astype(BF16)
    o_ref[0] = jnp.where(first, jnp.dot(pb, vb[0], preferred_element_type=F32),
                         jnp.dot(pb, vb[1], preferred_element_type=F32))
    imp0 = jnp.sum(jnp.where(first, p, 0.0), axis=0, keepdims=True)
    imp1 = jnp.sum(jnp.where(first, 0.0, p), axis=0, keepdims=True)
    imp = jnp.where(first, imp0, imp1)
    sb = jnp.dot(imp, pool_ref[...], precision=HIGHEST, preferred_element_type=F32)
    cur = q_pos // SEL_BLOCK
    bi = lax.broadcasted_iota(jnp.int32, (nbp, nbp), 0)
    bj = lax.broadcasted_iota(jnp.int32, (nbp, nbp), 1)
    blk = lax.broadcasted_iota(jnp.int32, (1, nbp), 1)
    causal = blk <= cur
    forced = (blk == 0) | (blk == cur) | (blk == cur - 1)
    rsel = lax.broadcasted_iota(jnp.int32, (N_SEL, nbp), 0)
    for h in range(N_KV_HEADS):
        sc = jnp.where(forced & causal, 1e4, jnp.where(causal, sb[h * GQA:h * GQA + 1, :], -1.0))
        sc = jnp.where(blk < n_blk, sc, -2.0)
        scb = jnp.broadcast_to(sc, (nbp, nbp))
        col = jnp.sum(jnp.where(bi == bj, scb, 0.0), axis=1, keepdims=True)
        ahead = (col > scb) | ((col == scb) & (bi < bj))
        rank = jnp.sum(ahead.astype(jnp.int32), axis=0, keepdims=True)
        hit = jnp.broadcast_to(rank, (N_SEL, nbp)) == rsel
        idx = jnp.sum(jnp.where(hit, jnp.broadcast_to(blk, (N_SEL, nbp)), 0), axis=1, keepdims=True)
        idx_ref[0, h] = jnp.broadcast_to(idx, (N_SEL, LANE))


def _cmp_select_step(q, ckv, bias, pool, n_cmp, n_blk, q_pos):
    B = q.shape[0]
    NCp = ckv.shape[1]
    return pl.pallas_call(
        functools.partial(_cmp_select_step_kernel, n_cmp=n_cmp, n_blk=n_blk, q_pos=q_pos),
        out_shape=(jax.ShapeDtypeStruct((B, N_HEADS, HEAD_DIM), F32),
                   jax.ShapeDtypeStruct((B, N_KV_HEADS, N_SEL, LANE), jnp.int32)),
        grid=(B,),
        in_specs=[pl.BlockSpec((1, N_HEADS, HEAD_DIM), lambda b: (b, 0, 0)),
                  pl.BlockSpec((1, NCp, D_KV), lambda b: (b, 0, 0)),
                  pl.BlockSpec(bias.shape, lambda b: (0, 0)),
                  pl.BlockSpec(pool.shape, lambda b: (0, 0))],
        out_specs=(pl.BlockSpec((1, N_HEADS, HEAD_DIM), lambda b: (b, 0, 0)),
                   pl.BlockSpec((1, N_KV_HEADS, N_SEL, LANE), lambda b: (b, 0, 0, 0))),
        compiler_params=_cparams("parallel"),
        name="cmp_select_step",
    )(q, ckv, bias, pool)


def _sel_step_kernel(pg_ref, idx_ref, q_ref, *refs, n_past, q_pos):
    page_refs = refs[:N_SEL]
    new_ref, bias_ref, kpos_ref, o_ref = refs[N_SEL:]
    b, h = pl.program_id(0), pl.program_id(1)
    base = (b * N_KV_HEADS + h) * N_SEL
    kts, vts = [], []
    for j in range(N_SEL):
        is_new = idx_ref[base + j] >= n_past
        kts.append(jnp.where(is_new, new_ref[0, 0, 0], page_refs[j][0, 0, 0]))
        vts.append(jnp.where(is_new, new_ref[0, 1, 0], page_refs[j][0, 1, 0]))
    kt = jnp.concatenate(kts, axis=1).astype(BF16)
    vt = jnp.concatenate(vts, axis=1).astype(BF16)
    s = jnp.dot(q_ref[0].astype(BF16), kt, preferred_element_type=F32) * (HEAD_DIM ** -0.5) + bias_ref[0, 0]
    mask = kpos_ref[0, 0] <= q_pos
    s = jnp.where(mask, s, NEG)
    m = jnp.max(s, axis=-1, keepdims=True)
    p = jnp.where(mask, jnp.exp(s - m), 0.0)
    l = jnp.sum(p, axis=-1, keepdims=True)
    o_ref[0, 0] = _nt_dot(p.astype(BF16), vt) / jnp.maximum(l, 1e-30)


def _sel_step(q, pool_t, new_t, bias_sel, kpos, pages, idx_flat, n_past, q_pos):
    B = q.shape[0]
    nk = N_SEL * PAGE_SIZE
    slot = lambda b, h, j: (b * N_KV_HEADS + h) * N_SEL + j
    page_spec = lambda j: pl.BlockSpec((1, 2, 1, HEAD_DIM, PAGE_SIZE),
                                       lambda b, h, pg, ix, j=j: (pg[slot(b, h, j)], 0, h, 0, 0))
    grid_spec = pltpu.PrefetchScalarGridSpec(
        num_scalar_prefetch=2,
        grid=(B, N_KV_HEADS),
        in_specs=[pl.BlockSpec((1, N_HEADS, HEAD_DIM), lambda b, h, pg, ix: (b, 0, 0))]
        + [page_spec(j) for j in range(N_SEL)]
        + [pl.BlockSpec((1, 2, 1, HEAD_DIM, PAGE_SIZE), lambda b, h, pg, ix: (b, 0, h, 0, 0)),
           pl.BlockSpec((1, 1, N_HEADS, nk), lambda b, h, pg, ix: (b, h, 0, 0)),
           pl.BlockSpec((1, 1, 1, nk), lambda b, h, pg, ix: (b, h, 0, 0))],
        out_specs=pl.BlockSpec((1, 1, N_HEADS, HEAD_DIM), lambda b, h, pg, ix: (b, h, 0, 0)),
    )
    return pl.pallas_call(
        functools.partial(_sel_step_kernel, n_past=n_past, q_pos=q_pos),
        out_shape=jax.ShapeDtypeStruct((B, N_KV_HEADS, N_HEADS, HEAD_DIM), F32),
        grid_spec=grid_spec,
        compiler_params=_cparams("arbitrary", "arbitrary"),
        name="sel_step",
    )(pages, idx_flat, q, *([pool_t] * N_SEL), new_t, bias_sel, kpos)


def _win_step_kernel(q_ref, w_ref, new_ref, bias_ref, bias0_ref, o_ref):
    q = q_ref[0]
    qb = q.astype(BF16)
    row = lax.broadcasted_iota(jnp.int32, (N_HEADS, 1), 0)
    first = row < GQA
    hd = HEAD_DIM
    kt = [w_ref[0, 0, h].astype(BF16) for h in range(N_KV_HEADS)]
    vt = [w_ref[0, 1, h].astype(BF16) for h in range(N_KV_HEADS)]
    dots = [jnp.dot(qb, kt[h], preferred_element_type=F32) for h in range(N_KV_HEADS)]
    s = jnp.where(first, dots[0], dots[1]) * (hd ** -0.5) + bias_ref[...]
    new = new_ref[0]
    kn = jnp.where(first, new[:, 0:hd], new[:, hd:2 * hd])
    vn = jnp.where(first, new[:, 2 * hd:3 * hd], new[:, 3 * hd:])
    sn = jnp.sum(q * kn, axis=-1, keepdims=True) * (hd ** -0.5) + bias0_ref[...]
    m = jnp.maximum(jnp.max(s, axis=-1, keepdims=True), sn)
    p = jnp.exp(s - m)
    pn = jnp.exp(sn - m)
    l = jnp.sum(p, axis=-1, keepdims=True) + pn
    pb = p.astype(BF16)
    acc = jnp.where(first, _nt_dot(pb, vt[0]), _nt_dot(pb, vt[1])) + pn * vn
    o_ref[0] = acc / jnp.maximum(l, 1e-30)


def _win_step(q, win_t, new, bias, bias0):
    B, W = win_t.shape[0], win_t.shape[-1]
    return pl.pallas_call(
        _win_step_kernel,
        out_shape=jax.ShapeDtypeStruct((B, N_HEADS, HEAD_DIM), F32),
        grid=(B,),
        in_specs=[pl.BlockSpec((1, N_HEADS, HEAD_DIM), lambda b: (b, 0, 0)),
                  pl.BlockSpec((1,) + win_t.shape[1:], lambda b: (b, 0, 0, 0, 0)),
                  pl.BlockSpec((1, 1, D_KV), lambda b: (b, 0, 0)),
                  pl.BlockSpec((N_HEADS, W), lambda b: (0, 0)),
                  pl.BlockSpec((N_HEADS, 1), lambda b: (0, 0))],
        out_specs=pl.BlockSpec((1, N_HEADS, HEAD_DIM), lambda b: (b, 0, 0)),
        compiler_params=_cparams("parallel"),
        name="win_step",
    )(q, win_t, new, bias, bias0)


def _split_heads(kv, dtype):
    B, L, _ = kv.shape
    kv5 = kv.reshape(B, L, 2, N_KV_HEADS, HEAD_DIM)
    return (jnp.transpose(kv5[:, :, 0], (0, 2, 1, 3)).astype(dtype),
            jnp.transpose(kv5[:, :, 1], (0, 2, 1, 3)).astype(dtype))


def _nsa_prompt(q5, kvc, ks, vst, kw, vwt, cmp_tab, rel_bias):
    B, T, _ = kvc.shape
    nc = T // CMP_STRIDE
    nb = T // SEL_BLOCK
    ckv = _compress_out([_compress_in(kvc.reshape(B, nc, CMP_STRIDE * D_KV), cmp_tab)], cmp_tab, nc)
    kc, vc = _split_heads(ckv, BF16)
    vct = jnp.transpose(vc, (0, 1, 3, 2))
    bias_n = _bias_by_distance(rel_bias, T)
    n_qt, n_kt = T // ATT_TQ, T // ATT_TK
    n_ds = min(n_kt, -(-(REL_MAX_DIST + ATT_TK - 1) // ATT_TK) + 1)
    n_dw = min(n_kt, WINDOW // ATT_TK + 1)
    tzs, tzw, bias_tab = _bias_tables(bias_n, n_qt, nc // 8, n_ds, n_dw, ATT_TQ, ATT_TK)
    pool = jnp.asarray(_pool_matrix(nc, nb))
    o_cmp, sel = _cmp_select_prompt(q5, kc, vct, bias_tab, pool)
    o_sel, o_win = _sel_win_prompt(q5, ks, vst, kw, vwt, sel, tzs, tzw)
    return o_cmp, o_sel, o_win


def _nsa_sample(q, kvc, kvs, kvw, pool_cmp, pool_sel, win_buf, page_table, cmp_tab, rel_bias):
    B = q.shape[0]
    n_pages = page_table.shape[1]
    past_len = n_pages * PAGE_SIZE
    q_pos = past_len
    lp = -(-(past_len + 1) // SEL_BLOCK) * SEL_BLOCK
    n_cmp = lp // CMP_STRIDE - 1
    n_blk = lp // SEL_BLOCK
    n_past_chunks = past_len // CMP_STRIDE
    n_tail = 8
    assert n_past_chunks + n_tail >= n_cmp + 1
    n_chunks = n_past_chunks + n_tail
    feature_major = lambda pool: jnp.transpose(pool, (0, 2, 3, 4, 1))
    z_past = _compress_in_paged(feature_major(pool_cmp), page_table, cmp_tab)
    tail = jnp.pad(kvc[:, None, :], ((0, 0), (0, n_tail * CMP_STRIDE - 1), (0, 0)))
    z_tail = _compress_in(tail.reshape(B, n_tail, CMP_STRIDE * D_KV), cmp_tab)
    ncp = -(-n_chunks // LANE) * LANE
    nbp = -(-n_blk // LANE) * LANE
    ckv = _compress_out([z_past, z_tail], cmp_tab, ncp)
    bias_n = _bias_by_distance(rel_bias, q_pos + 1)
    n_back = max((n_pages + 1) * PAGE_SIZE, ncp * CMP_STRIDE + CMP_BLOCK)
    back = jnp.concatenate([bias_n[:, ::-1], jnp.broadcast_to(bias_n[:, :1], (N_HEADS, n_back - q_pos - 1))], 1)
    bias_c = back[:, CMP_BLOCK - 1:CMP_BLOCK - 1 + ncp * CMP_STRIDE:CMP_STRIDE]
    pool = jnp.asarray(_pool_matrix(ncp, nbp).T)
    q3 = q.reshape(B, N_HEADS, HEAD_DIM)
    o_cmp, idx = _cmp_select_step(q3, ckv, bias_c, pool, n_cmp, n_blk, q_pos)
    idx = idx[..., 0]
    bpp = PAGE_SIZE // SEL_BLOCK
    n_past = n_pages * bpp
    lpage = idx // bpp
    pages = jnp.take_along_axis(page_table, jnp.minimum(lpage, n_pages - 1).reshape(B, -1), axis=1)
    new_t = jnp.pad(kvs.reshape(B, 2, N_KV_HEADS, HEAD_DIM, 1), ((0, 0),) * 4 + ((0, PAGE_SIZE - 1),))
    bias_page = jnp.transpose(back[:, :(n_pages + 1) * PAGE_SIZE].reshape(N_HEADS, n_pages + 1, PAGE_SIZE),
                              (1, 0, 2))
    bias_sel = jnp.transpose(bias_page[lpage], (0, 1, 3, 2, 4)).reshape(B, N_KV_HEADS, N_HEADS, -1)
    kpos = lpage[..., None] * PAGE_SIZE + jnp.arange(PAGE_SIZE)
    ok = (kpos // SEL_BLOCK == idx[..., None]) & (idx <= q_pos // SEL_BLOCK)[..., None]
    kpos = jnp.where(ok, kpos, q_pos + 1).reshape(B, N_KV_HEADS, 1, -1).astype(jnp.int32)
    o_sel = _sel_step(q3, feature_major(pool_sel), new_t, bias_sel, kpos, pages.reshape(-1).astype(jnp.int32),
                      idx.reshape(-1).astype(jnp.int32), n_past, q_pos)
    o_sel = jnp.concatenate([o_sel[:, h, h * GQA:(h + 1) * GQA] for h in range(N_KV_HEADS)], axis=1)
    wb = win_buf.shape[1]
    bias_w = bias_n[:, 1:wb + 1][:, ::-1]
    o_win = _win_step(q3, feature_major(win_buf), kvw[:, None, :], bias_w, bias_n[:, 0:1])
    return o_cmp.reshape(B, D_ATT), o_sel.reshape(B, D_ATT), o_win.reshape(B, D_ATT)


def kernel(x_prompt, x_sample, cache_cmp_kv, cache_sel_kv, state_win_kv, state_ssm_re, state_ssm_im, page_table,
           c_prompt, c_sample, w_ada, b_ada, w_in, lam_re, lam_im, log_dt, b_re, b_im, c_re, c_im, d_skip,
           w_glu, b_glu, phi_pe, phi_w1, phi_b1, phi_w2, phi_b2, rel_bias, w_out, ln1_g, ln1_b,
           w_router, b_router, w_gate_up, b_gate_up, w_down, b_down, ln2_g, ln2_b):
    assert w_ada.shape[0] == DEPTH == 1
    l = 0
    Bp, T, D = x_prompt.shape
    Bs = x_sample.shape[0]
    kv_tail = (2, N_KV_HEADS, HEAD_DIM)

    n_c = Bp + Bs
    c_all = jnp.pad(jnp.concatenate([c_prompt, c_sample], 0), ((0, -n_c % 8), (0, 0)))
    m_all = _adaln(c_all, w_ada[l], b_ada[l])
    m_p = m_all[:Bp].reshape(Bp, 6, D)
    m_s = m_all[Bp:n_c].reshape(Bs, 6, D)
    mod_p = [m_p[:, i:i + 1, :] for i in range(6)]
    mod_s = [m_s[None, :, i, :] for i in range(6)]

    w_in_pad = jnp.pad(w_in[l], ((0, 0), (0, D_IN_PAD - D_IN))).astype(BF16)
    n_levels = max(1, int(math.log2(T // SSM_CHUNK)))
    ssm_tab = _ssm_tables(lam_re[l], lam_im[l], log_dt[l], b_re[l], b_im[l], c_re[l], c_im[l],
                          SSM_CHUNK, n_levels)
    cmp_tab = _compress_tables(phi_pe[l], phi_w1[l], phi_b1[l], phi_w2[l], phi_b2[l])
    w_post = dict(
        d_skip=d_skip[l].reshape(1, D_SSM), w_glu=w_glu[l].astype(BF16), b_glu=b_glu[l].reshape(1, D_SSM),
        gexp=jnp.asarray(_gate_expand_matrix(), dtype=BF16), w_out=w_out[l].astype(BF16),
        ln1_g=ln1_g[l].reshape(1, D), ln1_b=ln1_b[l].reshape(1, D),
        w_router=jnp.stack(_split_bf16(jnp.pad(w_router[l], ((0, 0), (0, LANE - N_EXPERTS))))),
        b_router=jnp.pad(b_router[l], (0, LANE - N_EXPERTS)).reshape(1, LANE))

    u, q5, kvc, kvs, kvw, g, ks, vst, kw, vwt = _mixer_in(x_prompt, mod_p[0], mod_p[1], w_in_pad, 512, True)
    y_ssm, h_p = _ssm_prompt(u, ssm_tab)
    o_cmp, o_sel, o_win = _nsa_prompt(q5, kvc, ks, vst, kw, vwt, cmp_tab, rel_bias)
    x1_p, hm_p, te_p, tw_p = _post_mixer(y_ssm, u, o_cmp, o_sel, o_win, g, x_prompt,
                                         mod_p[2], mod_p[3], mod_p[4], w_post, tm=512)

    u_s, q_s, kvc_s, kvs_s, kvw_s, g_s = _mixer_in(x_sample.reshape(1, Bs, D), mod_s[0], mod_s[1],
                                                   w_in_pad, Bs, False)
    y_s, h_s = _ssm_sample(u_s[0], state_ssm_re[l], state_ssm_im[l], ssm_tab, c_re[l], c_im[l])
    oc_s, os_s, ow_s = _nsa_sample(q_s[0].astype(F32), kvc_s[0], kvs_s[0], kvw_s[0], cache_cmp_kv[l],
                                   cache_sel_kv[l], state_win_kv[l], page_table, cmp_tab, rel_bias)
    x1_s, hm_s, te_s, tw_s = _post_mixer(y_s[None], u_s, oc_s[None], os_s[None], ow_s[None], g_s,
                                         x_sample.reshape(1, Bs, D), mod_s[2], mod_s[3], mod_s[4],
                                         w_post, tm=Bs)

    n_p = Bp * T
    n_all = n_p + Bs
    hm_all = jnp.concatenate([hm_p.reshape(n_p, D), hm_s.reshape(Bs, D)], 0)
    te_all = jnp.concatenate([te_p.reshape(n_p, LANE), te_s.reshape(Bs, LANE)], 0)[:, :TOP_K]
    row_tok, dest, items = _moe_dispatch(te_all, n_all)
    xb = jnp.concatenate([hm_all, jnp.zeros((1, D), F32)], 0)[row_tok]
    yb = _experts(xb, items, w_gate_up[l], b_gate_up[l], w_down[l], b_down[l])
    ys_p = [yb[dest[:n_p, k]].reshape(Bp, T, D) for k in range(TOP_K)]
    ys_s = [yb[dest[n_p:, k]].reshape(1, Bs, D) for k in range(TOP_K)]
    ln2g, ln2b = ln2_g[l].reshape(1, D), ln2_b[l].reshape(1, D)
    out_p = _final(x1_p, ys_p, tw_p, mod_p[5], ln2g, ln2b, tm=512)
    out_s = _final(x1_s, ys_s, tw_s, mod_s[5], ln2g, ln2b, tm=Bs)

    wlen = min(WINDOW, T)
    win_s = jnp.concatenate([state_win_kv[l], kvw_s[0].reshape(Bs, 1, *kv_tail)], 1)[:, -state_win_kv.shape[2]:]
    p_state = SSM_STATE
    return (out_p, out_s.reshape(Bs, 1, D),
            kvc.reshape(1, Bp, T, *kv_tail), kvc_s[0].reshape(1, Bs, 1, *kv_tail),
            kvs.reshape(1, Bp, T, *kv_tail), kvs_s[0].reshape(1, Bs, 1, *kv_tail),
            kvw[:, T - wlen:].reshape(1, Bp, wlen, *kv_tail), win_s[None],
            h_p[None, ..., :p_state], h_p[None, ..., p_state:],
            h_s[None, ..., :p_state], h_s[None, ..., p_state:])
```
